```python
import math
import jax, jax.numpy as jnp
from jax import lax
import numpy as np

D_MODEL = 1024
BATCH = 4
SEQ = 4096
DEPTH = 1

D_INNER = D_MODEL
M_WIDTH = D_INNER // 2
M_HEADS = 4
M_HEAD_DIM = M_WIDTH // M_HEADS
H_WIDTH = D_INNER - M_WIDTH
H_GROUPS = 8
HYENA_ORDER = 2
N_DIR = 2
CHUNK = 128
FILTER_EMB = 33
FILTER_HIDDEN = 64
N_SIN = 3
DECAY_TARGET = 1e-2
FAST_DECAY_PCT = 0.3
SLOW_DECAY_PCT = 1.5
D_FF = 4 * D_MODEL
N_GATE = 2 * N_DIR * M_HEADS
EPS = 1e-6

OFF_MQ = 0
OFF_MK = OFF_MQ + M_WIDTH
OFF_HV = OFF_MK + M_WIDTH
OFF_HX1 = OFF_HV + H_WIDTH
OFF_HX2 = OFF_HX1 + H_WIDTH
N_CONV = OFF_HX2 + H_WIDTH
OFF_MV = N_CONV
OFF_MO = OFF_MV + M_WIDTH
OFF_GATE = OFF_MO + M_WIDTH
N_IN = OFF_GATE + N_GATE

kernel_name = "hybrid_mlstm_hyena_sandwich_block"


def rms_norm(x, w):
    xf = x.astype(jnp.float32)
    y = xf * lax.rsqrt(jnp.mean(xf * xf, axis=-1, keepdims=True) + EPS)
    return (y * w.astype(jnp.float32)).astype(x.dtype)


def short_conv3(u, w, b):
    up = jnp.pad(u, ((0, 0), (1, 1), (0, 0)))
    return up[:, :-2] * w[0] + up[:, 1:-1] * w[1] + up[:, 2:] * w[2] + b


def mlstm_chunkwise(q, k, v, log_i, log_f):
    B, H, S, D = q.shape
    L = CHUNK
    NC = S // L
    qc = q.reshape(B, H, NC, L, D)
    kc = k.reshape(B, H, NC, L, D)
    vc = v.reshape(B, H, NC, L, D)
    li = log_i.reshape(B, H, NC, L)
    bcum = jnp.cumsum(log_f.reshape(B, H, NC, L), axis=-1)
    g = bcum[..., -1]
    a = g[..., None] - bcum + li

    def step(carry, xs):
        C, n, m = carry
        k_c, v_c, a_c, g_c = xs
        m_new = jnp.maximum(g_c + m, jnp.max(a_c, axis=-1))
        decay = jnp.exp(g_c + m - m_new)
        w = jnp.exp(a_c - m_new[..., None])
        C_new = decay[..., None, None] * C + jnp.einsum('bhl,bhld,bhle->bhde', w, k_c, v_c)
        n_new = decay[..., None] * n + jnp.einsum('bhl,bhld->bhd', w, k_c)
        return (C_new, n_new, m_new), (C, n, m)

    init = (jnp.zeros((B, H, D, D), jnp.float32),
            jnp.zeros((B, H, D), jnp.float32),
            jnp.zeros((B, H), jnp.float32))
    xs = (jnp.moveaxis(kc, 2, 0), jnp.moveaxis(vc, 2, 0),
          jnp.moveaxis(a, 2, 0), jnp.moveaxis(g, 2, 0))
    _, (C_s, n_s, m_s) = lax.scan(step, init, xs)
    C_s = jnp.moveaxis(C_s, 0, 2)
    n_s = jnp.moveaxis(n_s, 0, 2)
    m_s = jnp.moveaxis(m_s, 0, 2)

    causal = jnp.tril(jnp.ones((L, L), dtype=bool))
    dmat = bcum[..., :, None] - bcum[..., None, :] + li[..., None, :]
    dmat = jnp.where(causal, dmat, -jnp.inf)
    inter_log = bcum + m_s[..., None]
    m_t = jnp.maximum(inter_log, jnp.max(dmat, axis=-1))
    wts = jnp.exp(dmat - m_t[..., None])
    s_qk = jnp.einsum('bhctd,bhcsd->bhcts', qc, kc) * wts
    inter_scale = jnp.exp(inter_log - m_t)
    num = (jnp.einsum('bhcts,bhcse->bhcte', s_qk, vc)
           + inter_scale[..., None] * jnp.einsum('bhctd,bhcde->bhcte', qc, C_s))
    den = s_qk.sum(-1) + inter_scale * jnp.einsum('bhctd,bhcd->bhct', qc, n_s)
    h = num / jnp.maximum(jnp.abs(den), jnp.exp(-m_t))[..., None]
    return h.reshape(B, H, S, D)


def hyena_filters(L, w1, b1, w2, b2, w3, b3, w4, freq):
    bands = (FILTER_EMB - 1) // 2
    t = jnp.linspace(0.0, 1.0, L, dtype=jnp.float32)[:, None]
    t_resc = jnp.arange(L, dtype=jnp.float32)[:, None]
    ang = 2.0 * math.pi * t_resc / L
    f = jnp.linspace(1e-4, bands - 1, bands, dtype=jnp.float32)[None, :]
    z = jnp.concatenate([t, jnp.cos(f * ang), -jnp.sin(f * ang)], axis=-1)
    fr = freq.astype(jnp.float32)
    h = jnp.sin(fr[0] * (z @ w1.astype(jnp.float32) + b1.astype(jnp.float32)))
    h = jnp.sin(fr[1] * (h @ w2.astype(jnp.float32) + b2.astype(jnp.float32)))
    h = jnp.sin(fr[2] * (h @ w3.astype(jnp.float32) + b3.astype(jnp.float32)))
    h = (h @ w4.astype(jnp.float32)).reshape(L, N_DIR, HYENA_ORDER, H_WIDTH)
    max_decay = math.log(DECAY_TARGET) / FAST_DECAY_PCT
    min_decay = math.log(DECAY_TARGET) / SLOW_DECAY_PCT
    deltas = jnp.linspace(min_decay, max_decay, H_WIDTH, dtype=jnp.float32)
    window = jnp.exp(-t[:, :, None, None] * jnp.abs(deltas))
    return h * window


def two_sided_fft_conv(u, k_fwd, k_bwd, bias):
    S = u.shape[1]
    kern = jnp.concatenate([k_fwd, jnp.zeros_like(k_fwd[:1]), k_bwd[:0:-1]], axis=0)
    K = jnp.fft.rfft(kern, axis=0)
    U = jnp.fft.rfft(u, n=2 * S, axis=1)
    y = jnp.fft.irfft(U * K[None], n=2 * S, axis=1)[:, :S]
    return y + u * bias


def group_rms(u, n_groups, w):
    B, S, C = u.shape
    ug = u.reshape(B, S, n_groups, C // n_groups)
    ug = ug * lax.rsqrt(jnp.mean(ug * ug, axis=-1, keepdims=True) + EPS)
    return ug.reshape(B, S, C) * w.astype(jnp.float32)


def setup_inputs(seed: int = 0) -> dict:
    key = jax.random.key(seed)
    ks = jax.random.split(key, 32)
    f32 = jnp.float32
    nrm = lambda k, shape, scale: jax.random.normal(k, shape, f32) * scale
    gain = lambda k, n: 1.0 + 0.05 * jax.random.normal(k, (n,), f32)
    i_bias = nrm(ks[6], (M_HEADS,), 0.1)
    f_bias = jnp.linspace(3.0, 6.0, M_HEADS, dtype=f32) + nrm(ks[7], (M_HEADS,), 0.1)
    i_bias_b = nrm(ks[8], (M_HEADS,), 0.1)
    f_bias_b = jnp.linspace(3.0, 6.0, M_HEADS, dtype=f32) + nrm(ks[9], (M_HEADS,), 0.1)
    return {
        "x": jax.random.normal(ks[0], (BATCH, SEQ, D_MODEL), f32),
        "norm_mix_pre": gain(ks[1], D_MODEL),
        "norm_mix_post": gain(ks[2], D_MODEL),
        "norm_mlp_pre": gain(ks[3], D_MODEL),
        "norm_mlp_post": gain(ks[4], D_MODEL),
        "w_in": nrm(ks[5], (D_MODEL, N_IN), D_MODEL ** -0.5),
        "b_gates": jnp.concatenate([i_bias, f_bias, i_bias_b, f_bias_b]),
        "conv_w": nrm(ks[10], (3, N_CONV), 3 ** -0.5),
        "conv_b": nrm(ks[11], (N_CONV,), 0.02),
        "mlstm_norm_w": gain(ks[12], M_WIDTH),
        "hyena_norm_w": gain(ks[13], H_WIDTH),
        "filt_w1": nrm(ks[14], (FILTER_EMB, FILTER_HIDDEN), FILTER_EMB ** -0.5),
        "filt_b1": nrm(ks[15], (FILTER_HIDDEN,), 0.02),
        "filt_w2": nrm(ks[16], (FILTER_HIDDEN, FILTER_HIDDEN), FILTER_HIDDEN ** -0.5),
        "filt_b2": nrm(ks[17], (FILTER_HIDDEN,), 0.02),
        "filt_w3": nrm(ks[18], (FILTER_HIDDEN, FILTER_HIDDEN), FILTER_HIDDEN ** -0.5),
        "filt_b3": nrm(ks[19], (FILTER_HIDDEN,), 0.02),
        "filt_w4": nrm(ks[20], (FILTER_HIDDEN, N_DIR * HYENA_ORDER * H_WIDTH), FILTER_HIDDEN ** -0.5),
        "filt_freq": 1.0 + 0.05 * jax.random.normal(ks[21], (N_SIN, FILTER_HIDDEN), f32),
        "filt_bias": nrm(ks[22], (HYENA_ORDER, H_WIDTH), 0.1),
        "w_out": nrm(ks[23], (D_INNER, D_MODEL), D_INNER ** -0.5),
        "w_mlp_in": nrm(ks[24], (D_MODEL, D_FF), D_MODEL ** -0.5),
        "w_mlp_out": nrm(ks[25], (D_FF, D_MODEL), D_FF ** -0.5),
    }


def reference(x, norm_mix_pre, norm_mix_post, norm_mlp_pre, norm_mlp_post, w_in, b_gates,
              conv_w, conv_b, mlstm_norm_w, hyena_norm_w, filt_w1, filt_b1, filt_w2, filt_b2,
              filt_w3, filt_b3, filt_w4, filt_freq, filt_bias, w_out, w_mlp_in, w_mlp_out):
    B, S, _ = x.shape
    f32 = jnp.float32
    for _layer in range(DEPTH):
        hn = rms_norm(x, norm_mix_pre)
        proj = (hn @ w_in).astype(f32)
        cv = short_conv3(proj[..., :N_CONV], conv_w.astype(f32), conv_b.astype(f32))

        def heads(u):
            return u.reshape(B, S, M_HEADS, M_HEAD_DIM).transpose(0, 2, 1, 3)
        q = heads(jax.nn.silu(cv[..., OFF_MQ:OFF_MQ + M_WIDTH]))
        k = heads(jax.nn.silu(cv[..., OFF_MK:OFF_MK + M_WIDTH])) * (M_HEAD_DIM ** -0.5)
        v = heads(proj[..., OFF_MV:OFF_MV + M_WIDTH])
        o_gate = jax.nn.sigmoid(proj[..., OFF_MO:OFF_MO + M_WIDTH])
        gates = (proj[..., OFF_GATE:OFF_GATE + N_GATE] + b_gates.astype(f32)).transpose(0, 2, 1)
        gates = gates.reshape(B, 2 * N_DIR, M_HEADS, S)
        h_fwd = mlstm_chunkwise(q, k, v, gates[:, 0], jax.nn.log_sigmoid(gates[:, 1]))
        flip = lambda u: jnp.flip(u, axis=2)
        h_bwd = flip(mlstm_chunkwise(flip(q), flip(k), flip(v), flip(gates[:, 2]),
                                     flip(jax.nn.log_sigmoid(gates[:, 3]))))
        h_m = (h_fwd + h_bwd).transpose(0, 2, 1, 3).reshape(B, S, M_WIDTH) * o_gate
        y_m = group_rms(h_m, M_HEADS, mlstm_norm_w)

        filt = hyena_filters(S, filt_w1, filt_b1, filt_w2, filt_b2, filt_w3, filt_b3,
                             filt_w4, filt_freq)
        fb = filt_bias.astype(f32)
        z = cv[..., OFF_HV:OFF_HV + H_WIDTH]
        z = cv[..., OFF_HX1:OFF_HX1 + H_WIDTH] * two_sided_fft_conv(z, filt[:, 0, 0], filt[:, 1, 0], fb[0])
        z = cv[..., OFF_HX2:OFF_HX2 + H_WIDTH] * two_sided_fft_conv(z, filt[:, 0, 1], filt[:, 1, 1], fb[1])
        y_h = group_rms(z, H_GROUPS, hyena_norm_w)

        mix = jnp.concatenate([y_m, y_h], axis=-1).astype(x.dtype) @ w_out
        x = x + rms_norm(mix, norm_mix_post)

        hm = rms_norm(x, norm_mlp_pre)
        ff = jnp.square(jax.nn.relu(hm @ w_mlp_in)) @ w_mlp_out
        x = x + rms_norm(ff, norm_mlp_post)
    return x
```

```python
import functools
import math

import numpy as np
import jax
import jax.numpy as jnp
from jax import lax
from jax.experimental import pallas as pl
from jax.experimental.pallas import tpu as pltpu

F32 = jnp.float32
BF16 = jnp.bfloat16

D_MODEL = 1024
M_WIDTH = 512
M_HEADS = 4
HEAD_DIM = 128
H_WIDTH = 512
H_GROUPS = 8
GROUP_W = H_WIDTH // H_GROUPS
CHUNK = 128
FILTER_EMB = 33
FILTER_HIDDEN = 64
DECAY_TARGET = 1e-2
FAST_DECAY_PCT = 0.3
SLOW_DECAY_PCT = 1.5
D_FF = 4 * D_MODEL
N_GATE = 16
EPS = 1e-6
LANES = 128
NEG_BIG = -1e30
VMEM_LIMIT = 56 * 1024 * 1024


def _cparams(*sem):
    return pltpu.CompilerParams(dimension_semantics=sem, vmem_limit_bytes=VMEM_LIMIT)


def _rms_rows(xf, w):
    ms = jnp.mean(xf * xf, axis=-1, keepdims=True)
    return xf * lax.rsqrt(ms + EPS) * w


def _sigmoid(x):
    return 1.0 / (1.0 + jnp.exp(-x))


def _log_sigmoid(x):
    return jnp.minimum(x, 0.0) - jnp.log(1.0 + jnp.exp(-jnp.abs(x)))


def _inproj_kernel(x_ref, nw_ref, wtm_ref, wg_ref, wgt_ref, bg_ref, bgt_ref,
                   hn_ref, tm_ref, gtm_ref, gcm_ref):
    x = x_ref[0]
    hn = _rms_rows(x, nw_ref[...]).astype(BF16)
    hn_ref[0] = hn
    tm_ref[0] = jnp.dot(hn, wtm_ref[...], preferred_element_type=F32).astype(BF16)
    g = jnp.dot(hn, wg_ref[...], preferred_element_type=F32) + bg_ref[...]
    col = lax.broadcasted_iota(jnp.int32, g.shape, 1)
    is_f = (col % 8) >= 4
    gtm_ref[0] = jnp.where(is_f, _log_sigmoid(g), g)
    gt = lax.dot_general(wgt_ref[...], hn, (((1,), (1,)), ((), ())),
                         preferred_element_type=F32) + bgt_ref[...]
    row = lax.broadcasted_iota(jnp.int32, gt.shape, 0)
    is_ft = (row % 8) >= 4
    gcm_ref[0] = jnp.where(is_ft, _log_sigmoid(gt), gt)


def _inproj(x, nw, wtm, wg, wgt, bg, bgt, tm_rows):
    B, S, D = x.shape
    n_tm = wtm.shape[1]
    grid = (B, S // tm_rows)
    const = lambda b, j: (0, 0)
    return pl.pallas_call(
        _inproj_kernel,
        grid=grid,
        in_specs=[
            pl.BlockSpec((1, tm_rows, D), lambda b, j: (b, j, 0)),
            pl.BlockSpec((1, D), const),
            pl.BlockSpec((D, n_tm), const),
            pl.BlockSpec((D, LANES), const),
            pl.BlockSpec((N_GATE, D), const),
            pl.BlockSpec((1, LANES), const),
            pl.BlockSpec((N_GATE, 1), const),
        ],
        out_specs=[
            pl.BlockSpec((1, tm_rows, D), lambda b, j: (b, j, 0)),
            pl.BlockSpec((1, tm_rows, n_tm), lambda b, j: (b, j, 0)),
            pl.BlockSpec((1, tm_rows, LANES), lambda b, j: (b, j, 0)),
            pl.BlockSpec((1, N_GATE, tm_rows), lambda b, j: (b, 0, j)),
        ],
        out_shape=[
            jax.ShapeDtypeStruct((B, S, D), BF16),
            jax.ShapeDtypeStruct((B, S, n_tm), BF16),
            jax.ShapeDtypeStruct((B, S, LANES), F32),
            jax.ShapeDtypeStruct((B, N_GATE, S), F32),
        ],
        compiler_params=_cparams("parallel", "parallel"),
        name="inproj",
    )(x, nw, wtm, wg, wgt, bg, bgt)


def _cmproj_kernel(hn_ref, wt_ref, cw_ref, cb_ref, out_ref, *, act_scale):
    S = hn_ref.shape[1]
    p = lax.dot_general(wt_ref[...], hn_ref[0], (((1,), (1,)), ((), ())),
                        preferred_element_type=F32)
    lane = lax.broadcasted_iota(jnp.int32, p.shape, 1)
    prev = jnp.where(lane == 0, 0.0, pltpu.roll(p, 1, 1))
    nxt = jnp.where(lane == S - 1, 0.0, pltpu.roll(p, S - 1, 1))
    cw = cw_ref[...]
    cv = prev * cw[:, 0:1] + p * cw[:, 1:2] + nxt * cw[:, 2:3] + cb_ref[...]
    if act_scale is not None:
        cv = cv * _sigmoid(cv) * act_scale
    out_ref[0] = cv.astype(BF16)


def _cmproj(hn, wt, cw, cb, cb_rows, act_scale):
    B, S, D = hn.shape
    C = wt.shape[0]
    return pl.pallas_call(
        functools.partial(_cmproj_kernel, act_scale=act_scale),
        grid=(B, C // cb_rows),
        in_specs=[
            pl.BlockSpec((1, S, D), lambda b, c: (b, 0, 0)),
            pl.BlockSpec((cb_rows, D), lambda b, c: (c, 0)),
            pl.BlockSpec((cb_rows, 3), lambda b, c: (c, 0)),
            pl.BlockSpec((cb_rows, 1), lambda b, c: (c, 0)),
        ],
        out_specs=pl.BlockSpec((1, cb_rows, S), lambda b, c: (b, c, 0)),
        out_shape=jax.ShapeDtypeStruct((B, C, S), BF16),
        compiler_params=_cparams("parallel", "arbitrary"),
        name="cmproj_k" if act_scale is not None else "cmproj_h",
    )(hn, wt, cw, cb)


def _mlstm_kernel(q_ref, v_ref, o_ref, kt_ref, g_ref, cw_ref, cb_ref, nw_ref, out_ref,
                  q_s, va_s, rows_s, h_s, c_s):
    S = q_ref.shape[1]
    L = CHUNK
    NC = S // L
    D = HEAD_DIM

    g = g_ref[0, 0]
    lane = lax.broadcasted_iota(jnp.int32, g.shape, 1) % L
    pre = g
    suf = g
    d = 1
    while d < L:
        pre = pre + jnp.where(lane >= d, pltpu.roll(pre, d, 1), 0.0)
        suf = suf + jnp.where(lane < L - d, pltpu.roll(suf, S - d, 1), 0.0)
        d *= 2
    rows_s[0:1, :] = g[0:1, :]
    rows_s[1:2, :] = pre[1:2, :]
    rows_s[2:3, :] = g[2:3, :]
    rows_s[3:4, :] = suf[3:4, :]

    cw = cw_ref[...]
    cb = cb_ref[...]
    row_id = lax.broadcasted_iota(jnp.int32, (L, D), 0)
    ones_col = (lax.broadcasted_iota(jnp.int32, (L, D), 1) == 0).astype(BF16)

    def prep(c, carry):
        r0 = pl.multiple_of(c * L, L)
        cur = q_ref[0, pl.ds(r0, L), :].astype(F32)
        rp = pl.multiple_of(jnp.maximum(r0 - 16, 0), 16)
        rn = pl.multiple_of(jnp.minimum(r0 + L, S - 16), 16)
        halo_p = q_ref[0, pl.ds(rp, 16), :].astype(F32)[15:16, :] * (c > 0).astype(F32)
        halo_n = q_ref[0, pl.ds(rn, 16), :].astype(F32)[0:1, :] * (c < NC - 1).astype(F32)
        prev = jnp.where(row_id == 0, halo_p, pltpu.roll(cur, 1, 0))
        nxt = jnp.where(row_id == L - 1, halo_n, pltpu.roll(cur, L - 1, 0))
        cv = prev * cw[0:1, :] + cur * cw[1:2, :] + nxt * cw[2:3, :] + cb
        q_s[pl.ds(r0, L), :] = (cv * _sigmoid(cv)).astype(BF16)
        va_s[pl.ds(r0, L), 0:D] = v_ref[0, pl.ds(r0, L), :]
        va_s[pl.ds(r0, L), D:2 * D] = ones_col
        return carry

    lax.fori_loop(0, NC, prep, 0)

    ri = lax.broadcasted_iota(jnp.int32, (L, L), 0)
    ci = lax.broadcasted_iota(jnp.int32, (L, L), 1)
    eye = ri == ci

    def direction(li_row, b_row, mask, g_lane, reverse, accumulate):
        c_s[...] = jnp.zeros_like(c_s)

        def step(i, m):
            c = (NC - 1 - i) if reverse else i
            r0 = pl.multiple_of(c * L, L)
            r_li = rows_s[li_row:li_row + 1, pl.ds(r0, L)]
            r_b = rows_s[b_row:b_row + 1, pl.ds(r0, L)]
            g_tot = r_b[:, g_lane:g_lane + 1]
            r_a = g_tot - r_b + r_li
            m_new = jnp.maximum(g_tot + m, jnp.max(r_a, axis=1, keepdims=True))
            decay = jnp.exp(g_tot + m - m_new)
            r_w = jnp.exp(r_a - m_new)
            c_b = jnp.sum(jnp.where(eye, r_b, 0.0), axis=1, keepdims=True)
            dmat = jnp.where(mask, c_b - r_b + r_li, NEG_BIG)
            inter_log = c_b + m
            m_t = jnp.maximum(inter_log, jnp.max(dmat, axis=1, keepdims=True))
            wts = jnp.exp(dmat - m_t)
            qc = q_s[pl.ds(r0, L), :]
            ktc = kt_ref[0, :, pl.ds(r0, L)]
            vac = va_s[pl.ds(r0, L), :]
            sqk = jnp.dot(qc, ktc, preferred_element_type=F32) * wts
            cmat = c_s[...]
            num = (jnp.dot(sqk.astype(BF16), vac, preferred_element_type=F32)
                   + jnp.exp(inter_log - m_t)
                   * jnp.dot(qc, cmat.astype(BF16), preferred_element_type=F32))
            den = jnp.maximum(jnp.abs(num[:, D:D + 1]), jnp.exp(-m_t))
            h = num[:, 0:D] / den
            if accumulate:
                h_s[pl.ds(r0, L), :] = h_s[pl.ds(r0, L), :] + h
            else:
                h_s[pl.ds(r0, L), :] = h
            kw = (ktc.astype(F32) * r_w).astype(BF16)
            c_s[...] = decay * cmat + jnp.dot(kw, vac, preferred_element_type=F32)
            return m_new

        lax.fori_loop(0, NC, step, jnp.zeros((1, 1), F32))

    direction(0, 1, ci <= ri, L - 1, False, False)
    direction(2, 3, ci >= ri, 0, True, True)

    nw = nw_ref[...]

    def fin(c, carry):
        r0 = pl.multiple_of(c * L, L)
        hm = h_s[pl.ds(r0, L), :] * _sigmoid(o_ref[0, pl.ds(r0, L), :].astype(F32))
        out_ref[0, pl.ds(r0, L), :] = _rms_rows(hm, nw).astype(BF16)
        return carry

    lax.fori_loop(0, NC, fin, 0)


def _mlstm(tm, kt, gcm, cwq, cbq, nw):
    B, S, _ = tm.shape
    D = HEAD_DIM
    H = M_HEADS
    return pl.pallas_call(
        _mlstm_kernel,
        grid=(B, H),
        in_specs=[
            pl.BlockSpec((1, S, D), lambda b, h: (b, 0, h)),
            pl.BlockSpec((1, S, D), lambda b, h: (b, 0, H + h)),
            pl.BlockSpec((1, S, D), lambda b, h: (b, 0, 2 * H + h)),
            pl.BlockSpec((1, D, S), lambda b, h: (b, h, 0)),
            pl.BlockSpec((1, 1, 8, S), lambda b, h: (b, h, 0, 0)),
            pl.BlockSpec((3, D), lambda b, h: (0, h)),
            pl.BlockSpec((1, D), lambda b, h: (0, h)),
            pl.BlockSpec((1, D), lambda b, h: (0, h)),
        ],
        out_specs=pl.BlockSpec((1, S, D), lambda b, h: (b, 0, h)),
        out_shape=jax.ShapeDtypeStruct((B, S, M_WIDTH), BF16),
        scratch_shapes=[
            pltpu.VMEM((S, D), BF16),
            pltpu.VMEM((S, 2 * D), BF16),
            pltpu.VMEM((8, S), F32),
            pltpu.VMEM((S, D), F32),
            pltpu.VMEM((D, 2 * D), F32),
        ],
        compiler_params=_cparams("parallel", "parallel"),
        name="mlstm",
    )(tm, tm, tm, kt, gcm, cwq, cbq, nw)


def _filt_mlp_kernel(w1t_ref, b1_ref, w2t_ref, b2_ref, w3t_ref, b3_ref, fr_ref, w4t_ref,
                     out_ref, h3_s, *, cb_rows):
    S = out_ref.shape[1]
    hi = lax.Precision.HIGHEST
    pos = lax.broadcasted_iota(jnp.int32, (1, S), 1).astype(F32)
    t = pos / (S - 1)

    @pl.when(pl.program_id(0) == 0)
    def _():
        bands = (FILTER_EMB - 1) // 2
        ang = (2.0 * math.pi) * pos / S
        fidx = lax.broadcasted_iota(jnp.int32, (bands, 1), 0).astype(F32)
        f = 1e-4 + fidx * ((bands - 1 - 1e-4) / (bands - 1))
        fa = f * ang
        w1t = w1t_ref[...]
        pre = (w1t[:, 0:1] * t
               + jnp.dot(w1t[:, 1:1 + bands], jnp.cos(fa), precision=hi, preferred_element_type=F32)
               - jnp.dot(w1t[:, 1 + bands:], jnp.sin(fa), precision=hi, preferred_element_type=F32))
        fr = fr_ref[...]
        h = jnp.sin(fr[:, 0:1] * (pre + b1_ref[...]))
        h = jnp.sin(fr[:, 1:2] * (jnp.dot(w2t_ref[...], h, precision=hi, preferred_element_type=F32)
                                  + b2_ref[...]))
        h = jnp.sin(fr[:, 2:3] * (jnp.dot(w3t_ref[...], h, precision=hi, preferred_element_type=F32)
                                  + b3_ref[...]))
        h3_s[...] = h

    filt = jnp.dot(w4t_ref[...], h3_s[...], precision=hi, preferred_element_type=F32)
    r = pl.program_id(0) * cb_rows + lax.broadcasted_iota(jnp.int32, (cb_rows, 1), 0)
    ch = (r % H_WIDTH).astype(F32)
    max_decay = math.log(DECAY_TARGET) / FAST_DECAY_PCT
    min_decay = math.log(DECAY_TARGET) / SLOW_DECAY_PCT
    delta = min_decay + ch * ((max_decay - min_decay) / (H_WIDTH - 1))
    filt = filt * jnp.exp(-t * jnp.abs(delta))
    is_bwd = r >= 2 * H_WIDTH
    lane = lax.broadcasted_iota(jnp.int32, (cb_rows, S), 1)
    out_ref[...] = jnp.where(jnp.logical_and(is_bwd, lane == 0), 0.0, filt)


def _filt_mlp(w1t, b1, w2t, b2, w3t, b3, fr, w4t, S, cb_rows=256):
    R = w4t.shape[0]
    Hd = FILTER_HIDDEN
    const = lambda i: (0, 0)
    return pl.pallas_call(
        functools.partial(_filt_mlp_kernel, cb_rows=cb_rows),
        grid=(R // cb_rows,),
        in_specs=[
            pl.BlockSpec((Hd, FILTER_EMB), const),
            pl.BlockSpec((Hd, 1), const),
            pl.BlockSpec((Hd, Hd), const),
            pl.BlockSpec((Hd, 1), const),
            pl.BlockSpec((Hd, Hd), const),
            pl.BlockSpec((Hd, 1), const),
            pl.BlockSpec((Hd, 3), const),
            pl.BlockSpec((cb_rows, Hd), lambda i: (i, 0)),
        ],
        out_specs=pl.BlockSpec((cb_rows, S), lambda i: (i, 0)),
        out_shape=jax.ShapeDtypeStruct((R, S), F32),
        scratch_shapes=[pltpu.VMEM((Hd, S), F32)],
        compiler_params=_cparams("arbitrary"),
        name="filt_mlp",
    )(w1t, b1, w2t, b2, w3t, b3, fr, w4t)


@functools.lru_cache(maxsize=None)
def _dft_constants(S):
    N = 2 * S
    N2 = LANES
    N1 = N // N2
    h = N1 // 2
    k1 = np.arange(N1)
    k2 = np.arange(N2)
    a1 = -2.0 * np.pi * np.outer(k1, k1) / N1
    f1r, f1i = np.cos(a1), np.sin(a1)
    at = -2.0 * np.pi * np.outer(k1, k2) / N
    twr, twi = np.cos(at), np.sin(at)
    a2 = -2.0 * np.pi * np.outer(k2, k2) / N2
    f2r, f2i = np.cos(a2), np.sin(a2)
    s1c = np.block([[f1r[:, :h], -f1i[:, :h]], [f1i[:, :h], f1r[:, :h]]])
    s1r = np.concatenate([f1r[:, :h], f1i[:, :h]], axis=0)
    s2 = np.block([[f2r, f2i], [-f2i, f2r]])
    s2i = np.block([[f2r, -f2i], [f2i, f2r]])
    s1i = np.block([[f1r[:h, :], f1i[:h, :]], [-f1i[:h, :], f1r[:h, :]]])
    cast = lambda a: np.asarray(a, np.float32)
    return dict(s1c=cast(s1c), s1r=cast(s1r), s2=cast(s2), s2i=cast(s2i), s1i=cast(s1i),
                twr=cast(twr), twi=cast(twi), N1=N1, h=h)


def _bf(a):
    return jnp.asarray(a, F32).astype(BF16)


def _filt_fft_kernel(bias_ref, kf_ref, kb_ref, s1r_ref, s2_ref, twr_ref, twi_ref, out_ref,
                     slab_s, *, n_ch):
    N1 = twr_ref.shape[0]
    n_total = N1 * LANES
    o = pl.program_id(0)
    cblk = pl.program_id(1)
    twr = twr_ref[...]
    twi = twi_ref[...]
    s1r = s1r_ref[...]

    def spectrum(src_ref):
        def per_ch(c, carry):
            a = jnp.dot(s1r, src_ref[0, 0, c].astype(BF16), preferred_element_type=F32)
            ar, ai = a[0:N1], a[N1:2 * N1]
            r0 = pl.multiple_of(c * N1, N1)
            slab_s[pl.ds(r0, N1), 0:LANES] = (ar * twr - ai * twi).astype(BF16)
            slab_s[pl.ds(r0, N1), LANES:2 * LANES] = (ar * twi + ai * twr).astype(BF16)
            return carry
        lax.fori_loop(0, n_ch, per_ch, 0)
        return jnp.dot(slab_s[...], s2_ref[...], preferred_element_type=F32)

    xf = spectrum(kf_ref)
    out_ref[0, :, :, 0:LANES] = xf[:, 0:LANES].reshape(n_ch, N1, LANES)
    out_ref[0, :, :, LANES:2 * LANES] = xf[:, LANES:2 * LANES].reshape(n_ch, N1, LANES)
    xb = spectrum(kb_ref)
    scale = 1.0 / n_total
    xbr = xb[:, 0:LANES].reshape(n_ch, N1, LANES)
    xbi = xb[:, LANES:2 * LANES].reshape(n_ch, N1, LANES)
    out_ref[0, :, :, 0:LANES] = (out_ref[0, :, :, 0:LANES] + xbr) * scale
    out_ref[0, :, :, LANES:2 * LANES] = (out_ref[0, :, :, LANES:2 * LANES] - xbi) * scale

    def add_bias(c, carry):
        bias = bias_ref[o * H_WIDTH + cblk * n_ch + c] * scale
        out_ref[0, c, :, 0:LANES] = out_ref[0, c, :, 0:LANES] + bias
        return carry

    lax.fori_loop(0, n_ch, add_bias, 0)


def _filt_fft(bias_flat, filt5, consts, n_ch=32):
    _, _, C, h, _ = filt5.shape
    N1 = consts["N1"]
    const = lambda o, c: (0, 0)
    return pl.pallas_call(
        functools.partial(_filt_fft_kernel, n_ch=n_ch),
        grid=(2, C // n_ch),
        in_specs=[
            pl.BlockSpec(memory_space=pltpu.SMEM),
            pl.BlockSpec((1, 1, n_ch, h, LANES), lambda o, c: (0, o, c, 0, 0)),
            pl.BlockSpec((1, 1, n_ch, h, LANES), lambda o, c: (1, o, c, 0, 0)),
            pl.BlockSpec((2 * N1, h), const),
            pl.BlockSpec((2 * LANES, 2 * LANES), const),
            pl.BlockSpec((N1, LANES), const),
            pl.BlockSpec((N1, LANES), const),
        ],
        out_specs=pl.BlockSpec((1, n_ch, N1, 2 * LANES), lambda o, c: (o, c, 0, 0)),
        out_shape=jax.ShapeDtypeStruct((2, C, N1, 2 * LANES), F32),
        scratch_shapes=[pltpu.VMEM((n_ch * N1, 2 * LANES), BF16)],
        compiler_params=_cparams("parallel", "parallel"),
        name="filt_fft",
    )(bias_flat, filt5, filt5, _bf(consts["s1r"]), _bf(consts["s2"]),
      jnp.asarray(consts["twr"]), jnp.asarray(consts["twi"]))


def _hyena_kernel(nw_ref, v_ref, x1_ref, x2_ref, khat_ref, s1c_ref, s2_ref, s2i_ref, s1i_ref,
                  twr_ref, twi_ref, out_ref, slab_s, q_s, z_s, y_s, *, ch_chunk):
    n_ch = v_ref.shape[1]
    h = v_ref.shape[2]
    N1 = 2 * h
    grp = pl.program_id(0)
    twr = twr_ref[...]
    twi = twi_ref[...]
    s1c = s1c_ref[...]
    s1i = s1i_ref[...]
    n_chunks = n_ch // ch_chunk
    rows = ch_chunk * N1

    def long_conv(src_ref, order, gate_ref, dst_ref):
        def fwd1(c, carry):
            z = jnp.concatenate([src_ref[0, c], src_ref[1, c]], axis=0)
            a = jnp.dot(s1c, z, preferred_element_type=F32)
            ar, ai = a[0:N1], a[N1:2 * N1]
            r0 = pl.multiple_of(c * N1, N1)
            slab_s[pl.ds(r0, N1), 0:LANES] = (ar * twr - ai * twi).astype(BF16)
            slab_s[pl.ds(r0, N1), LANES:2 * LANES] = (ar * twi + ai * twr).astype(BF16)
            return carry
        lax.fori_loop(0, n_ch, fwd1, 0)

        def mid(j, carry):
            r0 = pl.multiple_of(j * rows, rows)
            c0 = pl.multiple_of(j * ch_chunk, ch_chunk)
            x = jnp.dot(slab_s[pl.ds(r0, rows), :], s2_ref[...], preferred_element_type=F32)
            xr = x[:, 0:LANES].reshape(ch_chunk, N1, LANES)
            xi = x[:, LANES:2 * LANES].reshape(ch_chunk, N1, LANES)
            kr = khat_ref[order, pl.ds(c0, ch_chunk), :, 0:LANES]
            ki = khat_ref[order, pl.ds(c0, ch_chunk), :, LANES:2 * LANES]
            yr = (xr * kr - xi * ki).reshape(rows, LANES).astype(BF16)
            yi = (xr * ki + xi * kr).reshape(rows, LANES).astype(BF16)
            y = jnp.concatenate([yr, yi], axis=1)
            p = jnp.dot(y, s2i_ref[...], preferred_element_type=F32)
            pr = p[:, 0:LANES].reshape(ch_chunk, N1, LANES)
            pi = p[:, LANES:2 * LANES].reshape(ch_chunk, N1, LANES)
            q_s[pl.ds(c0, ch_chunk), 0:N1, :] = (pr * twr + pi * twi).astype(BF16)
            q_s[pl.ds(c0, ch_chunk), N1:2 * N1, :] = (pi * twr - pr * twi).astype(BF16)
            return carry
        lax.fori_loop(0, n_chunks, mid, 0)

        def inv1(c, carry):
            y = jnp.dot(s1i, q_s[c], preferred_element_type=F32)
            dst_ref[0, c] = (y[0:h] * gate_ref[0, c].astype(F32)).astype(dst_ref.dtype)
            dst_ref[1, c] = (y[h:2 * h] * gate_ref[1, c].astype(F32)).astype(dst_ref.dtype)
            return carry
        lax.fori_loop(0, n_ch, inv1, 0)

    long_conv(v_ref, 0, x1_ref, z_s)
    long_conv(z_s, 1, x2_ref, y_s)

    def sq(c, acc):
        yv = y_s[:, c]
        return acc + yv * yv
    ms = lax.fori_loop(0, n_ch, sq, jnp.zeros((2, h, LANES), F32)) * (1.0 / n_ch)
    inv = lax.rsqrt(ms + EPS)

    def norm(c, carry):
        w = nw_ref[grp * n_ch + c]
        out_ref[:, c] = (y_s[:, c] * inv * w).astype(BF16)
        return carry
    lax.fori_loop(0, n_ch, norm, 0)


def _hyena(nw, hy5, khat, consts, ch_chunk=16):
    B, C3, h, _ = hy5.shape
    C = C3 // 3
    n_ch = GROUP_W
    G = C // n_ch
    N1 = consts["N1"]
    const = lambda g, p: (0, 0)
    blk = (2, n_ch, h, LANES)
    return pl.pallas_call(
        functools.partial(_hyena_kernel, ch_chunk=ch_chunk),
        grid=(G, B // 2),
        in_specs=[
            pl.BlockSpec(memory_space=pltpu.SMEM),
            pl.BlockSpec(blk, lambda g, p: (p, g, 0, 0)),
            pl.BlockSpec(blk, lambda g, p: (p, G + g, 0, 0)),
            pl.BlockSpec(blk, lambda g, p: (p, 2 * G + g, 0, 0)),
            pl.BlockSpec((2, n_ch, N1, 2 * LANES), lambda g, p: (0, g, 0, 0)),
            pl.BlockSpec((2 * N1, 2 * h), const),
            pl.BlockSpec((2 * LANES, 2 * LANES), const),
            pl.BlockSpec((2 * LANES, 2 * LANES), const),
            pl.BlockSpec((2 * h, 2 * N1), const),
            pl.BlockSpec((N1, LANES), const),
            pl.BlockSpec((N1, LANES), const),
        ],
        out_specs=pl.BlockSpec(blk, lambda g, p: (p, g, 0, 0)),
        out_shape=jax.ShapeDtypeStruct((B, C, h, LANES), BF16),
        scratch_shapes=[
            pltpu.VMEM((n_ch * N1, 2 * LANES), BF16),
            pltpu.VMEM((n_ch, 2 * N1, LANES), BF16),
            pltpu.VMEM((2, n_ch, h, LANES), BF16),
            pltpu.VMEM((2, n_ch, h, LANES), F32),
        ],
        compiler_params=_cparams("parallel", "arbitrary"),
        name="hyena",
    )(nw, hy5, hy5, hy5, khat,
      _bf(consts["s1c"]), _bf(consts["s2"]), _bf(consts["s2i"]), _bf(consts["s1i"]),
      jnp.asarray(consts["twr"]), jnp.asarray(consts["twi"]))


def _outmlp_kernel(x_ref, ym_ref, yh_ref, wo_ref, w1_ref, w2_ref, n_post_ref, n_pre_ref,
                   n_post2_ref, out_ref):
    mix = (jnp.dot(ym_ref[...], wo_ref[0:M_WIDTH, :], preferred_element_type=F32)
           + jnp.dot(yh_ref[...], wo_ref[M_WIDTH:, :], preferred_element_type=F32))
    x1 = x_ref[...] + _rms_rows(mix, n_post_ref[...])
    hm = _rms_rows(x1, n_pre_ref[...]).astype(BF16)
    mid = jnp.maximum(jnp.dot(hm, w1_ref[...], preferred_element_type=F32), 0.0)
    mid = (mid * mid).astype(BF16)
    ff = jnp.dot(mid, w2_ref[...], preferred_element_type=F32)
    out_ref[...] = x1 + _rms_rows(ff, n_post2_ref[...])


def _outmlp(x2, ym, yh, wo, w1, w2, n_post, n_pre, n_post2, tm_rows):
    R, D = x2.shape
    const = lambda i: (0, 0)
    resident = functools.partial(pl.BlockSpec, index_map=const, pipeline_mode=pl.Buffered(1))
    return pl.pallas_call(
        _outmlp_kernel,
        grid=(R // tm_rows,),
        in_specs=[
            pl.BlockSpec((tm_rows, D), lambda i: (i, 0)),
            pl.BlockSpec((tm_rows, M_WIDTH), lambda i: (i, 0)),
            pl.BlockSpec((tm_rows, H_WIDTH), lambda i: (i, 0)),
            resident((D, D)),
            resident((D, D_FF)),
            resident((D_FF, D)),
            pl.BlockSpec((1, D), const),
            pl.BlockSpec((1, D), const),
            pl.BlockSpec((1, D), const),
        ],
        out_specs=pl.BlockSpec((tm_rows, D), lambda i: (i, 0)),
        out_shape=jax.ShapeDtypeStruct((R, D), F32),
        compiler_params=_cparams("parallel"),
        name="outmlp",
    )(x2, ym, yh, wo, w1, w2, n_post, n_pre, n_post2)


def kernel(x, norm_mix_pre, norm_mix_post, norm_mlp_pre, norm_mlp_post, w_in, b_gates,
           conv_w, conv_b, mlstm_norm_w, hyena_norm_w, filt_w1, filt_b1, filt_w2, filt_b2,
           filt_w3, filt_b3, filt_w4, filt_freq, filt_bias, w_out, w_mlp_in, w_mlp_out):
    B, S, D = x.shape
    assert D == D_MODEL and B % 2 == 0 and S % CHUNK == 0
    H = M_HEADS
    row = lambda a: a.astype(F32).reshape(1, -1)
    col = lambda a: a.astype(F32).reshape(-1, 1)
    tm_rows = min(512, S)

    w_in = w_in.astype(F32)
    o_k, o_hv, o_mv, o_gate = M_WIDTH, 2 * M_WIDTH, 2 * M_WIDTH + 3 * H_WIDTH, 3 * M_WIDTH + 3 * H_WIDTH + M_WIDTH
    w_tm = jnp.concatenate([w_in[:, 0:o_k], w_in[:, o_mv:o_gate]], axis=1).astype(BF16)
    w_g = w_in[:, o_gate:o_gate + N_GATE]
    w_gp = jnp.pad(w_g, ((0, 0), (0, LANES - N_GATE))).astype(BF16)
    w_gt = w_g.T.astype(BF16)
    bg = b_gates.astype(F32)
    bg_p = jnp.pad(bg, (0, LANES - N_GATE)).reshape(1, LANES)
    w_kt = w_in[:, o_k:o_hv].T.astype(BF16)
    w_ht = w_in[:, o_hv:o_mv].T.astype(BF16)
    cw = conv_w.astype(F32)
    cbias = conv_b.astype(F32)

    hn, tm, gtm, gcm = _inproj(x, row(norm_mix_pre), w_tm, w_gp, w_gt, bg_p, col(bg), tm_rows)
    del gtm

    kt = _cmproj(hn, w_kt, cw[:, o_k:o_hv].T, col(cbias[o_k:o_hv]), 256, HEAD_DIM ** -0.5)
    hy = _cmproj(hn, w_ht, cw[:, o_hv:o_mv].T, col(cbias[o_hv:o_mv]), 256, None)

    g4 = gcm.reshape(B, 4, H, S).transpose(0, 2, 1, 3)
    g8 = jnp.concatenate([g4, jnp.zeros_like(g4)], axis=2)
    y_m = _mlstm(tm, kt, g8, cw[:, 0:o_k], row(cbias[0:o_k]), row(mlstm_norm_w))

    consts = _dft_constants(S)
    h_rows = consts["h"]
    filt = _filt_mlp(filt_w1.astype(F32).T, col(filt_b1), filt_w2.astype(F32).T, col(filt_b2),
                     filt_w3.astype(F32).T, col(filt_b3), filt_freq.astype(F32).T,
                     filt_w4.astype(F32).T, S)
    filt5 = filt.reshape(2, 2, H_WIDTH, h_rows, LANES)
    khat = _filt_fft(filt_bias.astype(F32).reshape(-1), filt5, consts)

    hy5 = hy.reshape(B, 3 * H_WIDTH, h_rows, LANES)
    y_h = _hyena(hyena_norm_w.astype(F32), hy5, khat, consts)
    y_h = y_h.reshape(B, H_WIDTH, S).transpose(0, 2, 1)

    out = _outmlp(x.reshape(B * S, D), y_m.reshape(B * S, M_WIDTH), y_h.reshape(B * S, H_WIDTH),
                  w_out.astype(BF16), w_mlp_in.astype(BF16), w_mlp_out.astype(BF16),
                  row(norm_mix_post), row(norm_mlp_pre), row(norm_mlp_post), min(256, B * S))
    return out.reshape(B, S, D)
```

```python
import functools
import math

import numpy as np
import jax
import jax.numpy as jnp
from jax import lax
from jax.experimental import pallas as pl
from jax.experimental.pallas import tpu as pltpu

F32 = jnp.float32
BF16 = jnp.bfloat16

D_MODEL = 1024
M_WIDTH = 512
M_HEADS = 4
HEAD_DIM = 128
H_WIDTH = 512
H_GROUPS = 8
GROUP_W = H_WIDTH // H_GROUPS
CHUNK = 128
FILTER_EMB = 33
FILTER_HIDDEN = 64
DECAY_TARGET = 1e-2
FAST_DECAY_PCT = 0.3
SLOW_DECAY_PCT = 1.5
D_FF = 4 * D_MODEL
N_GATE = 16
EPS = 1e-6
LANES = 128
NEG_BIG = -1e30
VMEM_LIMIT = 56 * 1024 * 1024


def _cparams(*sem):
    return pltpu.CompilerParams(dimension_semantics=sem, vmem_limit_bytes=VMEM_LIMIT)


def _rms_rows(xf, w):
    ms = jnp.mean(xf * xf, axis=-1, keepdims=True)
    return xf * lax.rsqrt(ms + EPS) * w


def _sigmoid(x):
    return 1.0 / (1.0 + jnp.exp(-x))


def _log_sigmoid(x):
    return jnp.minimum(x, 0.0) - jnp.log(1.0 + jnp.exp(-jnp.abs(x)))


def _inproj_kernel(x_ref, nw_ref, wtm_ref, wg_ref, wgt_ref, bg_ref, bgt_ref,
                   hn_ref, tm_ref, gtm_ref, gcm_ref):
    x = x_ref[0]
    hn = _rms_rows(x, nw_ref[...]).astype(BF16)
    hn_ref[0] = hn
    tm_ref[0] = jnp.dot(hn, wtm_ref[...], preferred_element_type=F32).astype(BF16)
    g = jnp.dot(hn, wg_ref[...], preferred_element_type=F32) + bg_ref[...]
    col = lax.broadcasted_iota(jnp.int32, g.shape, 1)
    is_f = (col % 8) >= 4
    gtm_ref[0] = jnp.where(is_f, _log_sigmoid(g), g)
    gt = lax.dot_general(wgt_ref[...], hn, (((1,), (1,)), ((), ())),
                         preferred_element_type=F32) + bgt_ref[...]
    row = lax.broadcasted_iota(jnp.int32, gt.shape, 0)
    is_ft = (row % 8) >= 4
    gcm_ref[0] = jnp.where(is_ft, _log_sigmoid(gt), gt)


def _inproj(x, nw, wtm, wg, wgt, bg, bgt, tm_rows):
    B, S, D = x.shape
    n_tm = wtm.shape[1]
    grid = (B, S // tm_rows)
    const = lambda b, j: (0, 0)
    return pl.pallas_call(
        _inproj_kernel,
        grid=grid,
        in_specs=[
            pl.BlockSpec((1, tm_rows, D), lambda b, j: (b, j, 0)),
            pl.BlockSpec((1, D), const),
            pl.BlockSpec((D, n_tm), const),
            pl.BlockSpec((D, LANES), const),
            pl.BlockSpec((N_GATE, D), const),
            pl.BlockSpec((1, LANES), const),
            pl.BlockSpec((N_GATE, 1), const),
        ],
        out_specs=[
            pl.BlockSpec((1, tm_rows, D), lambda b, j: (b, j, 0)),
            pl.BlockSpec((1, tm_rows, n_tm), lambda b, j: (b, j, 0)),
            pl.BlockSpec((1, tm_rows, LANES), lambda b, j: (b, j, 0)),
            pl.BlockSpec((1, N_GATE, tm_rows), lambda b, j: (b, 0, j)),
        ],
        out_shape=[
            jax.ShapeDtypeStruct((B, S, D), BF16),
            jax.ShapeDtypeStruct((B, S, n_tm), BF16),
            jax.ShapeDtypeStruct((B, S, LANES), F32),
            jax.ShapeDtypeStruct((B, N_GATE, S), F32),
        ],
        compiler_params=_cparams("parallel", "parallel"),
        name="inproj",
    )(x, nw, wtm, wg, wgt, bg, bgt)


def _cmproj_kernel(hn_ref, wt_ref, cw_ref, cb_ref, out_ref, *, act_scale):
    S = hn_ref.shape[1]
    p = lax.dot_general(wt_ref[...], hn_ref[0], (((1,), (1,)), ((), ())),
                        preferred_element_type=F32)
    lane = lax.broadcasted_iota(jnp.int32, p.shape, 1)
    prev = jnp.where(lane == 0, 0.0, pltpu.roll(p, 1, 1))
    nxt = jnp.where(lane == S - 1, 0.0, pltpu.roll(p, S - 1, 1))
    cw = cw_ref[...]
    cv = prev * cw[:, 0:1] + p * cw[:, 1:2] + nxt * cw[:, 2:3] + cb_ref[...]
    if act_scale is not None:
        cv = cv * _sigmoid(cv) * act_scale
    out_ref[0] = cv.astype(BF16)


def _cmproj(hn, wt, cw, cb, cb_rows, act_scale):
    B, S, D = hn.shape
    C = wt.shape[0]
    return pl.pallas_call(
        functools.partial(_cmproj_kernel, act_scale=act_scale),
        grid=(B, C // cb_rows),
        in_specs=[
            pl.BlockSpec((1, S, D), lambda b, c: (b, 0, 0)),
            pl.BlockSpec((cb_rows, D), lambda b, c: (c, 0)),
            pl.BlockSpec((cb_rows, 3), lambda b, c: (c, 0)),
            pl.BlockSpec((cb_rows, 1), lambda b, c: (c, 0)),
        ],
        out_specs=pl.BlockSpec((1, cb_rows, S), lambda b, c: (b, c, 0)),
        out_shape=jax.ShapeDtypeStruct((B, C, S), BF16),
        compiler_params=_cparams("parallel", "arbitrary"),
        name="cmproj_k" if act_scale is not None else "cmproj_h",
    )(hn, wt, cw, cb)


def _mlstm_kernel(q_ref, v_ref, o_ref, kt_ref, g_ref, cw_ref, cb_ref, nw_ref, out_ref,
                  q_s, va_s, rows_s, h_s, c_s):
    S = q_ref.shape[1]
    HB = g_ref.shape[1]
    L = CHUNK
    NC = S // L
    D = HEAD_DIM

    for hd in range(HB):
        g = g_ref[0, hd]
        lane = lax.broadcasted_iota(jnp.int32, g.shape, 1) % L
        pre = g
        suf = g
        d = 1
        while d < L:
            pre = pre + jnp.where(lane >= d, pltpu.roll(pre, d, 1), 0.0)
            suf = suf + jnp.where(lane < L - d, pltpu.roll(suf, S - d, 1), 0.0)
            d *= 2
        rows_s[hd, 0:1, :] = g[0:1, :]
        rows_s[hd, 1:2, :] = pre[1:2, :]
        rows_s[hd, 2:3, :] = g[2:3, :]
        rows_s[hd, 3:4, :] = suf[3:4, :]

    cw = cw_ref[...]
    cb = cb_ref[...]
    W = HB * D
    row_id = lax.broadcasted_iota(jnp.int32, (L, W), 0)
    ones_col = (lax.broadcasted_iota(jnp.int32, (L, D), 1) == 0).astype(BF16)

    def prep(c, carry):
        r0 = pl.multiple_of(c * L, L)
        cur = q_ref[0, pl.ds(r0, L), :].astype(F32)
        rp = pl.multiple_of(jnp.maximum(r0 - 16, 0), 16)
        rn = pl.multiple_of(jnp.minimum(r0 + L, S - 16), 16)
        halo_p = q_ref[0, pl.ds(rp, 16), :].astype(F32)[15:16, :] * jnp.where(c > 0, 1.0, 0.0)
        halo_n = q_ref[0, pl.ds(rn, 16), :].astype(F32)[0:1, :] * jnp.where(c < NC - 1, 1.0, 0.0)
        prev = jnp.where(row_id == 0, halo_p, pltpu.roll(cur, 1, 0))
        nxt = jnp.where(row_id == L - 1, halo_n, pltpu.roll(cur, L - 1, 0))
        cv = prev * cw[0:1, :] + cur * cw[1:2, :] + nxt * cw[2:3, :] + cb
        q_s[pl.ds(r0, L), :] = (cv * _sigmoid(cv)).astype(BF16)
        h_s[pl.ds(r0, L), :] = jnp.zeros((L, W), F32)
        for hd in range(HB):
            va_s[hd, pl.ds(r0, L), 0:D] = v_ref[0, pl.ds(r0, L), hd * D:(hd + 1) * D]
            va_s[hd, pl.ds(r0, L), D:2 * D] = ones_col
        return carry

    lax.fori_loop(0, NC, prep, 0)
    c_s[...] = jnp.zeros_like(c_s)

    ri = lax.broadcasted_iota(jnp.int32, (L, L), 0)
    ci = lax.broadcasted_iota(jnp.int32, (L, L), 1)
    eye = ri == ci
    dirs = ((0, 1, ci <= ri, L - 1), (2, 3, ci >= ri, 0))

    def chain(hd, dr, c, m):
        li_row, b_row, mask, g_lane = dirs[dr]
        r0 = pl.multiple_of(c * L, L)
        r_li = rows_s[hd, li_row:li_row + 1, pl.ds(r0, L)]
        r_b = rows_s[hd, b_row:b_row + 1, pl.ds(r0, L)]
        g_tot = r_b[:, g_lane:g_lane + 1]
        r_a = g_tot - r_b + r_li
        m_new = jnp.maximum(g_tot + m, jnp.max(r_a, axis=1, keepdims=True))
        decay = jnp.exp(g_tot + m - m_new)
        r_w = jnp.exp(r_a - m_new)
        c_b = jnp.sum(jnp.where(eye, r_b, 0.0), axis=1, keepdims=True)
        dmat = jnp.where(mask, c_b - r_b + r_li, NEG_BIG)
        inter_log = c_b + m
        m_t = jnp.maximum(inter_log, jnp.max(dmat, axis=1, keepdims=True))
        wts = jnp.exp(dmat - m_t)
        qc = q_s[pl.ds(r0, L), hd * D:(hd + 1) * D]
        ktc = kt_ref[0, hd * D:(hd + 1) * D, pl.ds(r0, L)]
        vac = va_s[hd, pl.ds(r0, L), :]
        sqk = jnp.dot(qc, ktc, preferred_element_type=F32) * wts
        cmat = c_s[hd, dr]
        num = (jnp.dot(sqk.astype(BF16), vac, preferred_element_type=F32)
               + jnp.exp(inter_log - m_t)
               * jnp.dot(qc, cmat.astype(BF16), preferred_element_type=F32))
        den = jnp.maximum(jnp.abs(num[:, D:D + 1]), jnp.exp(-m_t))
        h = num[:, 0:D] / den
        h_s[pl.ds(r0, L), hd * D:(hd + 1) * D] = h_s[pl.ds(r0, L), hd * D:(hd + 1) * D] + h
        kw = (ktc.astype(F32) * r_w).astype(BF16)
        c_s[hd, dr] = decay * cmat + jnp.dot(kw, vac, preferred_element_type=F32)
        return m_new

    def step(i, ms):
        out = []
        for hd in range(HB):
            out.append(chain(hd, 0, i, ms[2 * hd]))
            out.append(chain(hd, 1, NC - 1 - i, ms[2 * hd + 1]))
        return tuple(out)

    lax.fori_loop(0, NC, step, tuple(jnp.zeros((1, 1), F32) for _ in range(2 * HB)))

    nw = nw_ref[...]

    def fin(c, carry):
        r0 = pl.multiple_of(c * L, L)
        hm = h_s[pl.ds(r0, L), :] * _sigmoid(o_ref[0, pl.ds(r0, L), :].astype(F32))
        for hd in range(HB):
            sl = slice(hd * D, (hd + 1) * D)
            out_ref[0, pl.ds(r0, L), sl] = _rms_rows(hm[:, sl], nw[:, sl]).astype(BF16)
        return carry

    lax.fori_loop(0, NC, fin, 0)


def _mlstm(tm, kt, gcm, cwq, cbq, nw, heads_per_step=2):
    B, S, _ = tm.shape
    D = HEAD_DIM
    HB = heads_per_step
    W = HB * D
    NB = M_HEADS // HB
    return pl.pallas_call(
        _mlstm_kernel,
        grid=(B, NB),
        in_specs=[
            pl.BlockSpec((1, S, W), lambda b, h: (b, 0, h)),
            pl.BlockSpec((1, S, W), lambda b, h: (b, 0, NB + h)),
            pl.BlockSpec((1, S, W), lambda b, h: (b, 0, 2 * NB + h)),
            pl.BlockSpec((1, W, S), lambda b, h: (b, h, 0)),
            pl.BlockSpec((1, HB, 8, S), lambda b, h: (b, h, 0, 0)),
            pl.BlockSpec((3, W), lambda b, h: (0, h)),
            pl.BlockSpec((1, W), lambda b, h: (0, h)),
            pl.BlockSpec((1, W), lambda b, h: (0, h)),
        ],
        out_specs=pl.BlockSpec((1, S, W), lambda b, h: (b, 0, h)),
        out_shape=jax.ShapeDtypeStruct((B, S, M_WIDTH), BF16),
        scratch_shapes=[
            pltpu.VMEM((S, W), BF16),
            pltpu.VMEM((HB, S, 2 * D), BF16),
            pltpu.VMEM((HB, 8, S), F32),
            pltpu.VMEM((S, W), F32),
            pltpu.VMEM((HB, 2, D, 2 * D), F32),
        ],
        compiler_params=_cparams("parallel", "parallel"),
        name="mlstm",
    )(tm, tm, tm, kt, gcm, cwq, cbq, nw)


def _filt_mlp_kernel(w1t_ref, b1_ref, w2t_ref, b2_ref, w3t_ref, b3_ref, fr_ref, w4t_ref,
                     out_ref, h3_s, *, cb_rows):
    S = out_ref.shape[1]
    hi = lax.Precision.HIGHEST
    pos = lax.broadcasted_iota(jnp.int32, (1, S), 1).astype(F32)
    t = pos / (S - 1)

    @pl.when(pl.program_id(0) == 0)
    def _():
        bands = (FILTER_EMB - 1) // 2
        ang = (2.0 * math.pi) * pos / S
        fidx = lax.broadcasted_iota(jnp.int32, (bands, 1), 0).astype(F32)
        f = 1e-4 + fidx * ((bands - 1 - 1e-4) / (bands - 1))
        fa = f * ang
        w1t = w1t_ref[...]
        pre = (w1t[:, 0:1] * t
               + jnp.dot(w1t[:, 1:1 + bands], jnp.cos(fa), precision=hi, preferred_element_type=F32)
               - jnp.dot(w1t[:, 1 + bands:], jnp.sin(fa), precision=hi, preferred_element_type=F32))
        fr = fr_ref[...]
        h = jnp.sin(fr[:, 0:1] * (pre + b1_ref[...]))
        h = jnp.sin(fr[:, 1:2] * (jnp.dot(w2t_ref[...], h, precision=hi, preferred_element_type=F32)
                                  + b2_ref[...]))
        h = jnp.sin(fr[:, 2:3] * (jnp.dot(w3t_ref[...], h, precision=hi, preferred_element_type=F32)
                                  + b3_ref[...]))
        h3_s[...] = h

    filt = jnp.dot(w4t_ref[...], h3_s[...], precision=hi, preferred_element_type=F32)
    r = pl.program_id(0) * cb_rows + lax.broadcasted_iota(jnp.int32, (cb_rows, 1), 0)
    ch = (r % H_WIDTH).astype(F32)
    max_decay = math.log(DECAY_TARGET) / FAST_DECAY_PCT
    min_decay = math.log(DECAY_TARGET) / SLOW_DECAY_PCT
    delta = min_decay + ch * ((max_decay - min_decay) / (H_WIDTH - 1))
    filt = filt * jnp.exp(-t * jnp.abs(delta))
    is_bwd = r >= 2 * H_WIDTH
    lane = lax.broadcasted_iota(jnp.int32, (cb_rows, S), 1)
    out_ref[...] = jnp.where(jnp.logical_and(is_bwd, lane == 0), 0.0, filt)


def _filt_mlp(w1t, b1, w2t, b2, w3t, b3, fr, w4t, S, cb_rows=256):
    R = w4t.shape[0]
    Hd = FILTER_HIDDEN
    const = lambda i: (0, 0)
    return pl.pallas_call(
        functools.partial(_filt_mlp_kernel, cb_rows=cb_rows),
        grid=(R // cb_rows,),
        in_specs=[
            pl.BlockSpec((Hd, FILTER_EMB), const),
            pl.BlockSpec((Hd, 1), const),
            pl.BlockSpec((Hd, Hd), const),
            pl.BlockSpec((Hd, 1), const),
            pl.BlockSpec((Hd, Hd), const),
            pl.BlockSpec((Hd, 1), const),
            pl.BlockSpec((Hd, 3), const),
            pl.BlockSpec((cb_rows, Hd), lambda i: (i, 0)),
        ],
        out_specs=pl.BlockSpec((cb_rows, S), lambda i: (i, 0)),
        out_shape=jax.ShapeDtypeStruct((R, S), F32),
        scratch_shapes=[pltpu.VMEM((Hd, S), F32)],
        compiler_params=_cparams("arbitrary"),
        name="filt_mlp",
    )(w1t, b1, w2t, b2, w3t, b3, fr, w4t)


@functools.lru_cache(maxsize=None)
def _dft_constants(S):
    N = 2 * S
    N2 = LANES
    N1 = N // N2
    h = N1 // 2
    k1 = np.arange(N1)
    k2 = np.arange(N2)
    a1 = -2.0 * np.pi * np.outer(k1, k1) / N1
    f1r, f1i = np.cos(a1), np.sin(a1)
    at = -2.0 * np.pi * np.outer(k1, k2) / N
    twr, twi = np.cos(at), np.sin(at)
    a2 = -2.0 * np.pi * np.outer(k2, k2) / N2
    f2r, f2i = np.cos(a2), np.sin(a2)
    s1c = np.block([[f1r[:, :h], -f1i[:, :h]], [f1i[:, :h], f1r[:, :h]]])
    s1r = np.concatenate([f1r[:, :h], f1i[:, :h]], axis=0)
    s2 = np.block([[f2r, f2i], [-f2i, f2r]])
    s2i = np.block([[f2r, -f2i], [f2i, f2r]])
    s1i = np.block([[f1r[:h, :], f1i[:h, :]], [-f1i[:h, :], f1r[:h, :]]])
    cast = lambda a: np.asarray(a, np.float32)
    return dict(s1c=cast(s1c), s1r=cast(s1r), s2=cast(s2), s2i=cast(s2i), s1i=cast(s1i),
                twr=cast(twr), twi=cast(twi), N1=N1, h=h)


def _bf(a):
    return jnp.asarray(a, F32).astype(BF16)


def _filt_fft_kernel(bias_ref, kf_ref, kb_ref, s1r_ref, s2_ref, twr_ref, twi_ref, out_ref,
                     slabf_s, slabb_s, *, n_ch, unroll):
    N1 = twr_ref.shape[0]
    scale = 1.0 / (N1 * LANES)
    twr = twr_ref[...]
    twi = twi_ref[...]
    s1r = s1r_ref[...]

    def stage1(src_ref, dst_ref, c):
        a = jnp.dot(s1r, src_ref[0, 0, c].astype(BF16), preferred_element_type=F32)
        ar, ai = a[0:N1], a[N1:2 * N1]
        r0 = pl.multiple_of(c * N1, N1)
        dst_ref[pl.ds(r0, N1), 0:LANES] = (ar * twr - ai * twi).astype(BF16)
        dst_ref[pl.ds(r0, N1), LANES:2 * LANES] = (ar * twi + ai * twr).astype(BF16)

    def per_group(j, carry):
        for u in range(unroll):
            c = j * unroll + u
            stage1(kf_ref, slabf_s, c)
            stage1(kb_ref, slabb_s, c)
        return carry

    lax.fori_loop(0, n_ch // unroll, per_group, 0)
    xf = jnp.dot(slabf_s[...], s2_ref[...], preferred_element_type=F32)
    xb = jnp.dot(slabb_s[...], s2_ref[...], preferred_element_type=F32)
    bias = bias_ref[0]
    kr = (xf[:, 0:LANES] + xb[:, 0:LANES]).reshape(n_ch, N1, LANES)
    ki = (xf[:, LANES:2 * LANES] - xb[:, LANES:2 * LANES]).reshape(n_ch, N1, LANES)
    out_ref[0, :, :, 0:LANES] = (kr + bias) * scale
    out_ref[0, :, :, LANES:2 * LANES] = ki * scale


def _filt_fft(bias4, filt5, consts, n_ch=32, unroll=4):
    _, _, C, h, _ = filt5.shape
    N1 = consts["N1"]
    const = lambda o, c: (0, 0)
    return pl.pallas_call(
        functools.partial(_filt_fft_kernel, n_ch=n_ch, unroll=unroll),
        grid=(2, C // n_ch),
        in_specs=[
            pl.BlockSpec((1, n_ch, 1, 1), lambda o, c: (o, c, 0, 0)),
            pl.BlockSpec((1, 1, n_ch, h, LANES), lambda o, c: (0, o, c, 0, 0)),
            pl.BlockSpec((1, 1, n_ch, h, LANES), lambda o, c: (1, o, c, 0, 0)),
            pl.BlockSpec((2 * N1, h), const),
            pl.BlockSpec((2 * LANES, 2 * LANES), const),
            pl.BlockSpec((N1, LANES), const),
            pl.BlockSpec((N1, LANES), const),
        ],
        out_specs=pl.BlockSpec((1, n_ch, N1, 2 * LANES), lambda o, c: (o, c, 0, 0)),
        out_shape=jax.ShapeDtypeStruct((2, C, N1, 2 * LANES), F32),
        scratch_shapes=[pltpu.VMEM((n_ch * N1, 2 * LANES), BF16),
                        pltpu.VMEM((n_ch * N1, 2 * LANES), BF16)],
        compiler_params=_cparams("parallel", "parallel"),
        name="filt_fft",
    )(bias4, filt5, filt5, _bf(consts["s1r"]), _bf(consts["s2"]),
      jnp.asarray(consts["twr"]), jnp.asarray(consts["twi"]))


def _hyena_kernel(nw_ref, v_ref, x1_ref, x2_ref, khat_ref, s1c_ref, s2_ref, s2i_ref, s1i_ref,
                  twr_ref, twi_ref, out_ref, slab_s, q_s, z_s, y_s, *, ch_chunk, unroll):
    n_ch = v_ref.shape[1]
    h = v_ref.shape[2]
    N1 = 2 * h
    grp = pl.program_id(0)
    twr = twr_ref[...]
    twi = twi_ref[...]
    s1c = s1c_ref[...]
    s1i = s1i_ref[...]
    n_chunks = n_ch // ch_chunk
    rows = ch_chunk * N1

    def long_conv(src_ref, order, gate_ref, dst_ref):
        def fwd1(j, carry):
            for u in range(unroll):
                c = j * unroll + u
                z = jnp.concatenate([src_ref[0, c], src_ref[1, c]], axis=0)
                a = jnp.dot(s1c, z, preferred_element_type=F32)
                ar, ai = a[0:N1], a[N1:2 * N1]
                r0 = pl.multiple_of(c * N1, N1)
                slab_s[pl.ds(r0, N1), 0:LANES] = (ar * twr - ai * twi).astype(BF16)
                slab_s[pl.ds(r0, N1), LANES:2 * LANES] = (ar * twi + ai * twr).astype(BF16)
            return carry
        lax.fori_loop(0, n_ch // unroll, fwd1, 0)

        def mid(j, carry):
            r0 = pl.multiple_of(j * rows, rows)
            c0 = pl.multiple_of(j * ch_chunk, ch_chunk)
            x = jnp.dot(slab_s[pl.ds(r0, rows), :], s2_ref[...], preferred_element_type=F32)
            xr = x[:, 0:LANES].reshape(ch_chunk, N1, LANES)
            xi = x[:, LANES:2 * LANES].reshape(ch_chunk, N1, LANES)
            kr = khat_ref[order, pl.ds(c0, ch_chunk), :, 0:LANES]
            ki = khat_ref[order, pl.ds(c0, ch_chunk), :, LANES:2 * LANES]
            yr = (xr * kr - xi * ki).reshape(rows, LANES).astype(BF16)
            yi = (xr * ki + xi * kr).reshape(rows, LANES).astype(BF16)
            y = jnp.concatenate([yr, yi], axis=1)
            p = jnp.dot(y, s2i_ref[...], preferred_element_type=F32)
            pr = p[:, 0:LANES].reshape(ch_chunk, N1, LANES)
            pi = p[:, LANES:2 * LANES].reshape(ch_chunk, N1, LANES)
            q_s[pl.ds(c0, ch_chunk), 0:N1, :] = (pr * twr + pi * twi).astype(BF16)
            q_s[pl.ds(c0, ch_chunk), N1:2 * N1, :] = (pi * twr - pr * twi).astype(BF16)
            return carry
        lax.fori_loop(0, n_chunks, mid, 0)

        def inv1(j, carry):
            for u in range(unroll):
                c = j * unroll + u
                y = jnp.dot(s1i, q_s[c], preferred_element_type=F32)
                dst_ref[0, c] = (y[0:h] * gate_ref[0, c].astype(F32)).astype(dst_ref.dtype)
                dst_ref[1, c] = (y[h:2 * h] * gate_ref[1, c].astype(F32)).astype(dst_ref.dtype)
            return carry
        lax.fori_loop(0, n_ch // unroll, inv1, 0)

    long_conv(v_ref, 0, x1_ref, z_s)
    long_conv(z_s, 1, x2_ref, y_s)

    def sq(c, acc):
        yv = y_s[:, c]
        return acc + yv * yv
    ms = lax.fori_loop(0, n_ch, sq, jnp.zeros((2, h, LANES), F32)) * (1.0 / n_ch)
    inv = lax.rsqrt(ms + EPS)

    def norm(c, carry):
        w = nw_ref[grp * n_ch + c]
        out_ref[:, c] = (y_s[:, c] * inv * w).astype(BF16)
        return carry
    lax.fori_loop(0, n_ch, norm, 0)


def _hyena(nw, hy5, khat, consts, ch_chunk=16, unroll=8):
    B, C3, h, _ = hy5.shape
    C = C3 // 3
    n_ch = GROUP_W
    G = C // n_ch
    N1 = consts["N1"]
    const = lambda g, p: (0, 0)
    blk = (2, n_ch, h, LANES)
    return pl.pallas_call(
        functools.partial(_hyena_kernel, ch_chunk=ch_chunk, unroll=unroll),
        grid=(G, B // 2),
        in_specs=[
            pl.BlockSpec(memory_space=pltpu.SMEM),
            pl.BlockSpec(blk, lambda g, p: (p, g, 0, 0)),
            pl.BlockSpec(blk, lambda g, p: (p, G + g, 0, 0)),
            pl.BlockSpec(blk, lambda g, p: (p, 2 * G + g, 0, 0)),
            pl.BlockSpec((2, n_ch, N1, 2 * LANES), lambda g, p: (0, g, 0, 0)),
            pl.BlockSpec((2 * N1, 2 * h), const),
            pl.BlockSpec((2 * LANES, 2 * LANES), const),
            pl.BlockSpec((2 * LANES, 2 * LANES), const),
            pl.BlockSpec((2 * h, 2 * N1), const),
            pl.BlockSpec((N1, LANES), const),
            pl.BlockSpec((N1, LANES), const),
        ],
        out_specs=pl.BlockSpec(blk, lambda g, p: (p, g, 0, 0)),
        out_shape=jax.ShapeDtypeStruct((B, C, h, LANES), BF16),
        scratch_shapes=[
            pltpu.VMEM((n_ch * N1, 2 * LANES), BF16),
            pltpu.VMEM((n_ch, 2 * N1, LANES), BF16),
            pltpu.VMEM((2, n_ch, h, LANES), BF16),
            pltpu.VMEM((2, n_ch, h, LANES), F32),
        ],
        compiler_params=_cparams("parallel", "arbitrary"),
        name="hyena",
    )(nw, hy5, hy5, hy5, khat,
      _bf(consts["s1c"]), _bf(consts["s2"]), _bf(consts["s2i"]), _bf(consts["s1i"]),
      jnp.asarray(consts["twr"]), jnp.asarray(consts["twi"]))


def _outmlp_kernel(x_ref, ym_ref, yh_ref, wo_ref, w1_ref, w2_ref, n_post_ref, n_pre_ref,
                   n_post2_ref, out_ref):
    mix = (jnp.dot(ym_ref[...], wo_ref[0:M_WIDTH, :], preferred_element_type=F32)
           + jnp.dot(yh_ref[...], wo_ref[M_WIDTH:, :], preferred_element_type=F32))
    x1 = x_ref[...] + _rms_rows(mix, n_post_ref[...])
    hm = _rms_rows(x1, n_pre_ref[...]).astype(BF16)
    mid = jnp.maximum(jnp.dot(hm, w1_ref[...], preferred_element_type=F32), 0.0)
    mid = (mid * mid).astype(BF16)
    ff = jnp.dot(mid, w2_ref[...], preferred_element_type=F32)
    out_ref[...] = x1 + _rms_rows(ff, n_post2_ref[...])


def _outmlp(x2, ym, yh, wo, w1, w2, n_post, n_pre, n_post2, tm_rows):
    R, D = x2.shape
    const = lambda i: (0, 0)
    resident = functools.partial(pl.BlockSpec, index_map=const, pipeline_mode=pl.Buffered(1))
    return pl.pallas_call(
        _outmlp_kernel,
        grid=(R // tm_rows,),
        in_specs=[
            pl.BlockSpec((tm_rows, D), lambda i: (i, 0)),
            pl.BlockSpec((tm_rows, M_WIDTH), lambda i: (i, 0)),
            pl.BlockSpec((tm_rows, H_WIDTH), lambda i: (i, 0)),
            resident((D, D)),
            resident((D, D_FF)),
            resident((D_FF, D)),
            pl.BlockSpec((1, D), const),
            pl.BlockSpec((1, D), const),
            pl.BlockSpec((1, D), const),
        ],
        out_specs=pl.BlockSpec((tm_rows, D), lambda i: (i, 0)),
        out_shape=jax.ShapeDtypeStruct((R, D), F32),
        compiler_params=_cparams("parallel"),
        name="outmlp",
    )(x2, ym, yh, wo, w1, w2, n_post, n_pre, n_post2)


def kernel(x, norm_mix_pre, norm_mix_post, norm_mlp_pre, norm_mlp_post, w_in, b_gates,
           conv_w, conv_b, mlstm_norm_w, hyena_norm_w, filt_w1, filt_b1, filt_w2, filt_b2,
           filt_w3, filt_b3, filt_w4, filt_freq, filt_bias, w_out, w_mlp_in, w_mlp_out):
    B, S, D = x.shape
    assert D == D_MODEL and B % 2 == 0 and S % CHUNK == 0
    H = M_HEADS
    row = lambda a: a.astype(F32).reshape(1, -1)
    col = lambda a: a.astype(F32).reshape(-1, 1)
    tm_rows = min(512, S)

    w_in = w_in.astype(F32)
    o_k, o_hv, o_mv, o_gate = M_WIDTH, 2 * M_WIDTH, 2 * M_WIDTH + 3 * H_WIDTH, 3 * M_WIDTH + 3 * H_WIDTH + M_WIDTH
    w_tm = jnp.concatenate([w_in[:, 0:o_k], w_in[:, o_mv:o_gate]], axis=1).astype(BF16)
    w_g = w_in[:, o_gate:o_gate + N_GATE]
    w_gp = jnp.pad(w_g, ((0, 0), (0, LANES - N_GATE))).astype(BF16)
    w_gt = w_g.T.astype(BF16)
    bg = b_gates.astype(F32)
    bg_p = jnp.pad(bg, (0, LANES - N_GATE)).reshape(1, LANES)
    w_kt = w_in[:, o_k:o_hv].T.astype(BF16)
    w_ht = w_in[:, o_hv:o_mv].T.astype(BF16)
    cw = conv_w.astype(F32)
    cbias = conv_b.astype(F32)

    hn, tm, gtm, gcm = _inproj(x, row(norm_mix_pre), w_tm, w_gp, w_gt, bg_p, col(bg), tm_rows)
    del gtm

    kt = _cmproj(hn, w_kt, cw[:, o_k:o_hv].T, col(cbias[o_k:o_hv]), 256, HEAD_DIM ** -0.5)
    hy = _cmproj(hn, w_ht, cw[:, o_hv:o_mv].T, col(cbias[o_hv:o_mv]), 256, None)

    g4 = gcm.reshape(B, 4, H, S).transpose(0, 2, 1, 3)
    g8 = jnp.concatenate([g4, jnp.zeros_like(g4)], axis=2)
    y_m = _mlstm(tm, kt, g8, cw[:, 0:o_k], row(cbias[0:o_k]), row(mlstm_norm_w))

    consts = _dft_constants(S)
    h_rows = consts["h"]
    filt = _filt_mlp(filt_w1.astype(F32).T, col(filt_b1), filt_w2.astype(F32).T, col(filt_b2),
                     filt_w3.astype(F32).T, col(filt_b3), filt_freq.astype(F32).T,
                     filt_w4.astype(F32).T, S)
    filt5 = filt.reshape(2, 2, H_WIDTH, h_rows, LANES)
    khat = _filt_fft(filt_bias.astype(F32).reshape(2, H_WIDTH, 1, 1), filt5, consts)

    hy5 = hy.reshape(B, 3 * H_WIDTH, h_rows, LANES)
    y_h = _hyena(hyena_norm_w.astype(F32), hy5, khat, consts)
    y_h = y_h.reshape(B, H_WIDTH, S).transpose(0, 2, 1)

    out = _outmlp(x.reshape(B * S, D), y_m.reshape(B * S, M_WIDTH), y_h.reshape(B * S, H_WIDTH),
                  w_out.astype(BF16), w_mlp_in.astype(BF16), w_mlp_out.astype(BF16),
                  row(norm_mix_post), row(norm_mlp_pre), row(norm_mlp_post), min(256, B * S))
    return out.reshape(B, S, D)
```

```python
import functools
import math

import numpy as np
import jax
import jax.numpy as jnp
from jax import lax
from jax.experimental import pallas as pl
from jax.experimental.pallas import tpu as pltpu

F32 = jnp.float32
BF16 = jnp.bfloat16

D_MODEL = 1024
M_WIDTH = 512
M_HEADS = 4
HEAD_DIM = 128
H_WIDTH = 512
H_GROUPS = 8
GROUP_W = H_WIDTH // H_GROUPS
CHUNK = 128
FILTER_EMB = 33
FILTER_HIDDEN = 64
DECAY_TARGET = 1e-2
FAST_DECAY_PCT = 0.3
SLOW_DECAY_PCT = 1.5
D_FF = 4 * D_MODEL
N_GATE = 16
EPS = 1e-6
LANES = 128
BF16_ROWS = 16
NEG_BIG = -1e30
VMEM_LIMIT = 56 * 1024 * 1024


def _cparams(*sem):
    return pltpu.CompilerParams(dimension_semantics=sem, vmem_limit_bytes=VMEM_LIMIT)


def _rms_rows(xf, w):
    ms = jnp.mean(xf * xf, axis=-1, keepdims=True)
    return xf * lax.rsqrt(ms + EPS) * w


def _sigmoid(x):
    return 1.0 / (1.0 + jnp.exp(-x))


def _log_sigmoid(x):
    return jnp.minimum(x, 0.0) - jnp.log(1.0 + jnp.exp(-jnp.abs(x)))


def _inproj_kernel(x_ref, nw_ref, wtm_ref, wgt_ref, bgt_ref, hn_ref, tm_ref, gcm_ref):
    x = x_ref[0]
    hn = _rms_rows(x, nw_ref[...]).astype(BF16)
    hn_ref[0] = hn
    tm_ref[0] = jnp.dot(hn, wtm_ref[...], preferred_element_type=F32).astype(BF16)
    gt = lax.dot_general(wgt_ref[...], hn, (((1,), (1,)), ((), ())),
                         preferred_element_type=F32) + bgt_ref[...]
    row = lax.broadcasted_iota(jnp.int32, gt.shape, 0)
    is_ft = (row % 8) >= 4
    gcm_ref[0] = jnp.where(is_ft, _log_sigmoid(gt), gt)


def _inproj(x, nw, wtm, wgt, bgt, tm_rows):
    B, S, D = x.shape
    n_tm = wtm.shape[1]
    grid = (B, S // tm_rows)
    const = lambda b, j: (0, 0)
    return pl.pallas_call(
        _inproj_kernel,
        grid=grid,
        in_specs=[
            pl.BlockSpec((1, tm_rows, D), lambda b, j: (b, j, 0)),
            pl.BlockSpec((1, D), const),
            pl.BlockSpec((D, n_tm), const),
            pl.BlockSpec((N_GATE, D), const),
            pl.BlockSpec((N_GATE, 1), const),
        ],
        out_specs=[
            pl.BlockSpec((1, tm_rows, D), lambda b, j: (b, j, 0)),
            pl.BlockSpec((1, tm_rows, n_tm), lambda b, j: (b, j, 0)),
            pl.BlockSpec((1, N_GATE, tm_rows), lambda b, j: (b, 0, j)),
        ],
        out_shape=[
            jax.ShapeDtypeStruct((B, S, D), BF16),
            jax.ShapeDtypeStruct((B, S, n_tm), BF16),
            jax.ShapeDtypeStruct((B, N_GATE, S), F32),
        ],
        compiler_params=_cparams("parallel", "parallel"),
        name="inproj",
    )(x, nw, wtm, wgt, bgt)


def _cmproj_kernel(hn_ref, wt_ref, cw_ref, cb_ref, out_ref, *, modes):
    S = hn_ref.shape[1]
    p = lax.dot_general(wt_ref[...], hn_ref[0], (((1,), (1,)), ((), ())),
                        preferred_element_type=F32)

    def emit(mode):
        if mode == "plain":
            out = p
        elif mode == "sigmoid":
            out = _sigmoid(p)
        else:
            lane = lax.broadcasted_iota(jnp.int32, p.shape, 1)
            prev = jnp.where(lane == 0, 0.0, pltpu.roll(p, 1, 1))
            nxt = jnp.where(lane == S - 1, 0.0, pltpu.roll(p, S - 1, 1))
            cw = cw_ref[...]
            out = prev * cw[:, 0:1] + p * cw[:, 1:2] + nxt * cw[:, 2:3] + cb_ref[...]
            if mode == "conv_silu":
                out = out * _sigmoid(out)
        out = out.astype(BF16)
        for j in range(S // LANES):
            out_ref[0, j] = out[:, j * LANES:(j + 1) * LANES]

    distinct = sorted(set(modes))
    if len(distinct) == 1:
        emit(distinct[0])
    else:
        blk = pl.program_id(1)
        for mode in distinct:
            hit = functools.reduce(jnp.logical_or, [blk == i for i, m in enumerate(modes) if m == mode])
            pl.when(hit)(functools.partial(emit, mode))


def _cmproj(hn, wt, cw, cb, cb_rows, modes, name):
    B, S, D = hn.shape
    C = wt.shape[0]
    assert len(modes) == C // cb_rows
    return pl.pallas_call(
        functools.partial(_cmproj_kernel, modes=tuple(modes)),
        grid=(B, C // cb_rows),
        in_specs=[
            pl.BlockSpec((1, S, D), lambda b, c: (b, 0, 0)),
            pl.BlockSpec((cb_rows, D), lambda b, c: (c, 0)),
            pl.BlockSpec((cb_rows, 3), lambda b, c: (c, 0)),
            pl.BlockSpec((cb_rows, 1), lambda b, c: (c, 0)),
        ],
        out_specs=pl.BlockSpec((1, S // LANES, cb_rows, LANES), lambda b, c: (b, 0, c, 0)),
        out_shape=jax.ShapeDtypeStruct((B, S // LANES, C, LANES), BF16),
        compiler_params=_cparams("parallel", "arbitrary"),
        name=name,
    )(hn, wt, cw, cb)


def _mlstm_kernel(qt_ref, vt_ref, ot_ref, k_ref, g_ref, cw_ref, cb_ref, nw_ref, out_ref,
                  k_s, va_s, rows_s, h_s, c_s):
    S = k_ref.shape[1]
    HB = g_ref.shape[1]
    L = CHUNK
    NC = S // L
    D = HEAD_DIM
    DA = D + BF16_ROWS
    W = HB * D

    for hd in range(HB):
        g = g_ref[0, hd]
        lane = lax.broadcasted_iota(jnp.int32, g.shape, 1) % L
        pre = g
        suf = g
        d = 1
        while d < L:
            pre = pre + jnp.where(lane >= d, pltpu.roll(pre, d, 1), 0.0)
            suf = suf + jnp.where(lane < L - d, pltpu.roll(suf, S - d, 1), 0.0)
            d *= 2
        rows_s[hd, 0:1, :] = g[0:1, :]
        rows_s[hd, 1:2, :] = pre[1:2, :]
        rows_s[hd, 2:3, :] = g[2:3, :]
        rows_s[hd, 3:4, :] = suf[3:4, :]

    cw = cw_ref[...]
    cb = cb_ref[...]
    row_id = lax.broadcasted_iota(jnp.int32, (L, W), 0)
    ones_row = (lax.broadcasted_iota(jnp.int32, (BF16_ROWS, L), 0) == 0).astype(BF16)
    k_scale = HEAD_DIM ** -0.5

    def prep(c, carry):
        r0 = pl.multiple_of(c * L, L)
        cur = k_ref[0, pl.ds(r0, L), :].astype(F32)
        rp = pl.multiple_of(jnp.maximum(r0 - BF16_ROWS, 0), BF16_ROWS)
        rn = pl.multiple_of(jnp.minimum(r0 + L, S - BF16_ROWS), BF16_ROWS)
        halo_p = (k_ref[0, pl.ds(rp, BF16_ROWS), :].astype(F32)[BF16_ROWS - 1:BF16_ROWS, :]
                  * jnp.where(c > 0, 1.0, 0.0))
        halo_n = (k_ref[0, pl.ds(rn, BF16_ROWS), :].astype(F32)[0:1, :]
                  * jnp.where(c < NC - 1, 1.0, 0.0))
        prev = jnp.where(row_id == 0, halo_p, pltpu.roll(cur, 1, 0))
        nxt = jnp.where(row_id == L - 1, halo_n, pltpu.roll(cur, L - 1, 0))
        cv = prev * cw[0:1, :] + cur * cw[1:2, :] + nxt * cw[2:3, :] + cb
        k_s[pl.ds(r0, L), :] = (cv * _sigmoid(cv) * k_scale).astype(BF16)
        h_s[:, pl.ds(r0, L)] = jnp.zeros((W, L), F32)
        for hd in range(HB):
            va_s[hd, 0:D, pl.ds(r0, L)] = vt_ref[0, c, hd * D:(hd + 1) * D, :]
            va_s[hd, D:DA, pl.ds(r0, L)] = ones_row
        return carry

    lax.fori_loop(0, NC, prep, 0)
    c_s[...] = jnp.zeros_like(c_s)

    si = lax.broadcasted_iota(jnp.int32, (L, L), 0)
    ti = lax.broadcasted_iota(jnp.int32, (L, L), 1)
    eye = si == ti
    dirs = ((0, 1, si <= ti, L - 1), (2, 3, si >= ti, 0))

    def chain(hd, dr, c, m):
        li_row, b_row, mask, g_lane = dirs[dr]
        r0 = pl.multiple_of(c * L, L)
        r_li = rows_s[hd, li_row:li_row + 1, pl.ds(r0, L)]
        r_b = rows_s[hd, b_row:b_row + 1, pl.ds(r0, L)]
        g_tot = r_b[:, g_lane:g_lane + 1]
        r_a = g_tot - r_b + r_li
        m_new = jnp.maximum(g_tot + m, jnp.max(r_a, axis=1, keepdims=True))
        decay = jnp.exp(g_tot + m - m_new)
        r_w = jnp.exp(r_a - m_new)
        a_col = jnp.sum(jnp.where(eye, r_li - r_b, 0.0), axis=1, keepdims=True)
        dmat = jnp.where(mask, a_col + r_b, NEG_BIG)
        inter_log = r_b + m
        m_t = jnp.maximum(inter_log, jnp.max(dmat, axis=0, keepdims=True))
        wts = jnp.exp(dmat - m_t)
        kc = k_s[pl.ds(r0, L), hd * D:(hd + 1) * D]
        qtc = qt_ref[0, c, hd * D:(hd + 1) * D, :]
        vac = va_s[hd, :, pl.ds(r0, L)]
        cmat = c_s[hd, dr]
        both = jnp.dot(jnp.concatenate([kc, cmat.astype(BF16)], axis=0), qtc,
                       preferred_element_type=F32)
        s_w = (both[0:L] * wts).astype(BF16)
        tot = (jnp.dot(vac, s_w, preferred_element_type=F32)
               + jnp.exp(inter_log - m_t) * both[L:L + DA])
        den = jnp.maximum(jnp.abs(tot[D:D + 1]), jnp.exp(-m_t))
        h = tot[0:D] * (1.0 / den)
        h_s[hd * D:(hd + 1) * D, pl.ds(r0, L)] = h_s[hd * D:(hd + 1) * D, pl.ds(r0, L)] + h
        u = (vac.astype(F32) * r_w).astype(BF16)
        c_s[hd, dr] = decay * cmat + jnp.dot(u, kc, preferred_element_type=F32)
        return m_new

    def step(i, ms):
        out = []
        for hd in range(HB):
            out.append(chain(hd, 0, i, ms[2 * hd]))
            out.append(chain(hd, 1, NC - 1 - i, ms[2 * hd + 1]))
        return tuple(out)

    lax.fori_loop(0, NC, step, tuple(jnp.zeros((1, 1), F32) for _ in range(2 * HB)))

    nw = jnp.broadcast_to(nw_ref[...], (W, L))

    def fin(c, carry):
        r0 = pl.multiple_of(c * L, L)
        hm = h_s[:, pl.ds(r0, L)] * ot_ref[0, c].astype(F32)
        for hd in range(HB):
            sl = slice(hd * D, (hd + 1) * D)
            hh = hm[sl]
            ms = jnp.mean(hh * hh, axis=0, keepdims=True)
            out_ref[0, c, sl, :] = (hh * lax.rsqrt(ms + EPS) * nw[sl]).astype(BF16)
        return carry

    lax.fori_loop(0, NC, fin, 0)


def _mlstm(mt, k_tm, gcm, cwk, cbk, nw_col, heads_per_step=2):
    B, S, _ = k_tm.shape
    NC = S // CHUNK
    D = HEAD_DIM
    HB = heads_per_step
    W = HB * D
    NB = M_HEADS // HB
    DA = D + BF16_ROWS
    cm_blk = (1, NC, W, LANES)
    return pl.pallas_call(
        _mlstm_kernel,
        grid=(B, NB),
        in_specs=[
            pl.BlockSpec(cm_blk, lambda b, h: (b, 0, h, 0)),
            pl.BlockSpec(cm_blk, lambda b, h: (b, 0, NB + h, 0)),
            pl.BlockSpec(cm_blk, lambda b, h: (b, 0, 2 * NB + h, 0)),
            pl.BlockSpec((1, S, W), lambda b, h: (b, 0, h)),
            pl.BlockSpec((1, HB, 8, S), lambda b, h: (b, h, 0, 0)),
            pl.BlockSpec((3, W), lambda b, h: (0, h)),
            pl.BlockSpec((1, W), lambda b, h: (0, h)),
            pl.BlockSpec((W, 1), lambda b, h: (h, 0)),
        ],
        out_specs=pl.BlockSpec(cm_blk, lambda b, h: (b, 0, h, 0)),
        out_shape=jax.ShapeDtypeStruct((B, NC, M_WIDTH, LANES), BF16),
        scratch_shapes=[
            pltpu.VMEM((S, W), BF16),
            pltpu.VMEM((HB, DA, S), BF16),
            pltpu.VMEM((HB, 8, S), F32),
            pltpu.VMEM((W, S), F32),
            pltpu.VMEM((HB, 2, DA, D), F32),
        ],
        compiler_params=_cparams("parallel", "parallel"),
        name="mlstm",
    )(mt, mt, mt, k_tm, gcm, cwk, cbk, nw_col)


def _filt_mlp_kernel(w1t_ref, b1_ref, w2t_ref, b2_ref, w3t_ref, b3_ref, fr_ref, w4t_ref,
                     out_ref, h3_s, *, cb_rows):
    S = out_ref.shape[1]
    hi = lax.Precision.HIGHEST
    pos = lax.broadcasted_iota(jnp.int32, (1, S), 1).astype(F32)
    t = pos / (S - 1)

    @pl.when(pl.program_id(0) == 0)
    def _():
        bands = (FILTER_EMB - 1) // 2
        ang = (2.0 * math.pi) * pos / S
        fidx = lax.broadcasted_iota(jnp.int32, (bands, 1), 0).astype(F32)
        f = 1e-4 + fidx * ((bands - 1 - 1e-4) / (bands - 1))
        fa = f * ang
        w1t = w1t_ref[...]
        pre = (w1t[:, 0:1] * t
               + jnp.dot(w1t[:, 1:1 + bands], jnp.cos(fa), precision=hi, preferred_element_type=F32)
               - jnp.dot(w1t[:, 1 + bands:], jnp.sin(fa), precision=hi, preferred_element_type=F32))
        fr = fr_ref[...]
        h = jnp.sin(fr[:, 0:1] * (pre + b1_ref[...]))
        h = jnp.sin(fr[:, 1:2] * (jnp.dot(w2t_ref[...], h, precision=hi, preferred_element_type=F32)
                                  + b2_ref[...]))
        h = jnp.sin(fr[:, 2:3] * (jnp.dot(w3t_ref[...], h, precision=hi, preferred_element_type=F32)
                                  + b3_ref[...]))
        h3_s[...] = h

    filt = jnp.dot(w4t_ref[...], h3_s[...], precision=hi, preferred_element_type=F32)
    r = pl.program_id(0) * cb_rows + lax.broadcasted_iota(jnp.int32, (cb_rows, 1), 0)
    ch = (r % H_WIDTH).astype(F32)
    max_decay = math.log(DECAY_TARGET) / FAST_DECAY_PCT
    min_decay = math.log(DECAY_TARGET) / SLOW_DECAY_PCT
    delta = min_decay + ch * ((max_decay - min_decay) / (H_WIDTH - 1))
    filt = filt * jnp.exp(-t * jnp.abs(delta))
    is_bwd = r >= 2 * H_WIDTH
    lane = lax.broadcasted_iota(jnp.int32, (cb_rows, S), 1)
    out_ref[...] = jnp.where(jnp.logical_and(is_bwd, lane == 0), 0.0, filt)


def _filt_mlp(w1t, b1, w2t, b2, w3t, b3, fr, w4t, S, cb_rows=256):
    R = w4t.shape[0]
    Hd = FILTER_HIDDEN
    const = lambda i: (0, 0)
    return pl.pallas_call(
        functools.partial(_filt_mlp_kernel, cb_rows=cb_rows),
        grid=(R // cb_rows,),
        in_specs=[
            pl.BlockSpec((Hd, FILTER_EMB), const),
            pl.BlockSpec((Hd, 1), const),
            pl.BlockSpec((Hd, Hd), const),
            pl.BlockSpec((Hd, 1), const),
            pl.BlockSpec((Hd, Hd), const),
            pl.BlockSpec((Hd, 1), const),
            pl.BlockSpec((Hd, 3), const),
            pl.BlockSpec((cb_rows, Hd), lambda i: (i, 0)),
        ],
        out_specs=pl.BlockSpec((cb_rows, S), lambda i: (i, 0)),
        out_shape=jax.ShapeDtypeStruct((R, S), F32),
        scratch_shapes=[pltpu.VMEM((Hd, S), F32)],
        compiler_params=_cparams("arbitrary"),
        name="filt_mlp",
    )(w1t, b1, w2t, b2, w3t, b3, fr, w4t)


@functools.lru_cache(maxsize=None)
def _dft_constants(S):
    N = 2 * S
    N2 = LANES
    N1 = N // N2
    h = N1 // 2
    k1 = np.arange(N1)
    k2 = np.arange(N2)
    a1 = -2.0 * np.pi * np.outer(k1, k1) / N1
    f1r, f1i = np.cos(a1), np.sin(a1)
    at = -2.0 * np.pi * np.outer(k1, k2) / N
    twr, twi = np.cos(at), np.sin(at)
    a2 = -2.0 * np.pi * np.outer(k2, k2) / N2
    f2r, f2i = np.cos(a2), np.sin(a2)
    s1c = np.block([[f1r[:, :h], -f1i[:, :h]], [f1i[:, :h], f1r[:, :h]]])
    s1r = np.concatenate([f1r[:, :h], f1i[:, :h]], axis=0)
    s2 = np.block([[f2r, f2i], [-f2i, f2r]])
    s2i = np.block([[f2r, -f2i], [f2i, f2r]])
    s1i = np.block([[f1r[:h, :], f1i[:h, :]], [-f1i[:h, :], f1r[:h, :]]])
    cast = lambda a: np.asarray(a, np.float32)
    return dict(s1c=cast(s1c), s1r=cast(s1r), s2=cast(s2), s2i=cast(s2i), s1i=cast(s1i),
                twr=cast(twr), twi=cast(twi), N1=N1, h=h)


def _bf(a):
    return jnp.asarray(a, F32).astype(BF16)


def _filt_fft_kernel(bias_ref, kf_ref, kb_ref, s1r_ref, s2_ref, twr_ref, twi_ref, out_ref,
                     slabf_s, slabb_s, *, n_ch, unroll):
    N1 = twr_ref.shape[0]
    scale = 1.0 / (N1 * LANES)
    twr = twr_ref[...]
    twi = twi_ref[...]
    s1r = s1r_ref[...]

    def stage1(src_ref, dst_ref, c):
        a = jnp.dot(s1r, src_ref[0, 0, c].astype(BF16), preferred_element_type=F32)
        ar, ai = a[0:N1], a[N1:2 * N1]
        r0 = pl.multiple_of(c * N1, N1)
        dst_ref[pl.ds(r0, N1), 0:LANES] = (ar * twr - ai * twi).astype(BF16)
        dst_ref[pl.ds(r0, N1), LANES:2 * LANES] = (ar * twi + ai * twr).astype(BF16)

    def per_group(j, carry):
        for u in range(unroll):
            c = j * unroll + u
            stage1(kf_ref, slabf_s, c)
            stage1(kb_ref, slabb_s, c)
        return carry

    lax.fori_loop(0, n_ch // unroll, per_group, 0)
    xf = jnp.dot(slabf_s[...], s2_ref[...], preferred_element_type=F32)
    xb = jnp.dot(slabb_s[...], s2_ref[...], preferred_element_type=F32)
    bias = bias_ref[0]
    kr = (xf[:, 0:LANES] + xb[:, 0:LANES]).reshape(n_ch, N1, LANES)
    ki = (xf[:, LANES:2 * LANES] - xb[:, LANES:2 * LANES]).reshape(n_ch, N1, LANES)
    out_ref[0, :, :, 0:LANES] = (kr + bias) * scale
    out_ref[0, :, :, LANES:2 * LANES] = ki * scale


def _filt_fft(bias4, filt5, consts, n_ch=32, unroll=4):
    _, _, C, h, _ = filt5.shape
    N1 = consts["N1"]
    const = lambda o, c: (0, 0)
    return pl.pallas_call(
        functools.partial(_filt_fft_kernel, n_ch=n_ch, unroll=unroll),
        grid=(2, C // n_ch),
        in_specs=[
            pl.BlockSpec((1, n_ch, 1, 1), lambda o, c: (o, c, 0, 0)),
            pl.BlockSpec((1, 1, n_ch, h, LANES), lambda o, c: (0, o, c, 0, 0)),
            pl.BlockSpec((1, 1, n_ch, h, LANES), lambda o, c: (1, o, c, 0, 0)),
            pl.BlockSpec((2 * N1, h), const),
            pl.BlockSpec((2 * LANES, 2 * LANES), const),
            pl.BlockSpec((N1, LANES), const),
            pl.BlockSpec((N1, LANES), const),
        ],
        out_specs=pl.BlockSpec((1, n_ch, N1, 2 * LANES), lambda o, c: (o, c, 0, 0)),
        out_shape=jax.ShapeDtypeStruct((2, C, N1, 2 * LANES), F32),
        scratch_shapes=[pltpu.VMEM((n_ch * N1, 2 * LANES), BF16),
                        pltpu.VMEM((n_ch * N1, 2 * LANES), BF16)],
        compiler_params=_cparams("parallel", "parallel"),
        name="filt_fft",
    )(bias4, filt5, filt5, _bf(consts["s1r"]), _bf(consts["s2"]),
      jnp.asarray(consts["twr"]), jnp.asarray(consts["twi"]))


def _hyena_kernel(nw_ref, v_ref, x1_ref, x2_ref, khat_ref, s1c_ref, s2_ref, s2i_ref, s1i_ref,
                  twr_ref, twi_ref, out_ref, src_s, slab_s, q_s, y_s, *, ch_chunk, unroll):
    h = v_ref.shape[1]
    n_ch = v_ref.shape[2]
    N1 = 2 * h
    twr = twr_ref[...]
    twi = twi_ref[...]
    s1c = s1c_ref[...]
    s1i = s1i_ref[...]
    n_chunks = n_ch // ch_chunk
    rows = ch_chunk * N1
    pitch = src_s.shape[0] // (2 * h)

    def tile_rows(i):
        return pl.ds(pl.multiple_of(i * pitch, 8), n_ch)

    def chan_rows(c, bb):
        return pl.ds(bb * h * pitch + c, h, stride=pitch)

    def long_conv(order):
        def fwd1(j, carry):
            for u in range(unroll):
                c = j * unroll + u
                z = jnp.concatenate([src_s[chan_rows(c, 0), :], src_s[chan_rows(c, 1), :]], axis=0)
                a = jnp.dot(s1c, z.astype(BF16), preferred_element_type=F32)
                ar, ai = a[0:N1], a[N1:2 * N1]
                r0 = pl.multiple_of(c * N1, N1)
                slab_s[pl.ds(r0, N1), 0:LANES] = (ar * twr - ai * twi).astype(BF16)
                slab_s[pl.ds(r0, N1), LANES:2 * LANES] = (ar * twi + ai * twr).astype(BF16)
            return carry
        lax.fori_loop(0, n_ch // unroll, fwd1, 0)

        def mid(j, carry):
            r0 = pl.multiple_of(j * rows, rows)
            c0 = pl.multiple_of(j * ch_chunk, ch_chunk)
            x = jnp.dot(slab_s[pl.ds(r0, rows), :], s2_ref[...], preferred_element_type=F32)
            xr = x[:, 0:LANES].reshape(ch_chunk, N1, LANES)
            xi = x[:, LANES:2 * LANES].reshape(ch_chunk, N1, LANES)
            kr = khat_ref[order, pl.ds(c0, ch_chunk), :, 0:LANES]
            ki = khat_ref[order, pl.ds(c0, ch_chunk), :, LANES:2 * LANES]
            yr = (xr * kr - xi * ki).reshape(rows, LANES).astype(BF16)
            yi = (xr * ki + xi * kr).reshape(rows, LANES).astype(BF16)
            y = jnp.concatenate([yr, yi], axis=1)
            p = jnp.dot(y, s2i_ref[...], preferred_element_type=F32)
            pr = p[:, 0:LANES].reshape(ch_chunk, N1, LANES)
            pi = p[:, LANES:2 * LANES].reshape(ch_chunk, N1, LANES)
            q_s[pl.ds(c0, ch_chunk), 0:N1, :] = (pr * twr + pi * twi).astype(BF16)
            q_s[pl.ds(c0, ch_chunk), N1:2 * N1, :] = (pi * twr - pr * twi).astype(BF16)
            return carry
        lax.fori_loop(0, n_chunks, mid, 0)

        def inv1(j, carry):
            for u in range(unroll):
                c = j * unroll + u
                y = jnp.dot(s1i, q_s[c], preferred_element_type=F32)
                y_s[chan_rows(c, 0), :] = y[0:h]
                y_s[chan_rows(c, 1), :] = y[h:2 * h]
            return carry
        lax.fori_loop(0, n_ch // unroll, inv1, 0)

    def load_in(i, carry):
        src_s[tile_rows(i), :] = v_ref[i // h, i % h].astype(F32)
        return carry
    lax.fori_loop(0, 2 * h, load_in, 0)
    long_conv(0)

    def gate1(i, carry):
        src_s[tile_rows(i), :] = y_s[tile_rows(i), :] * x1_ref[i // h, i % h].astype(F32)
        return carry
    lax.fori_loop(0, 2 * h, gate1, 0)
    long_conv(1)

    nw = jnp.broadcast_to(nw_ref[...], (n_ch, LANES))

    def gate2_norm(i, carry):
        z = y_s[tile_rows(i), :] * x2_ref[i // h, i % h].astype(F32)
        ms = jnp.mean(z * z, axis=0, keepdims=True)
        out_ref[i // h, i % h] = (z * lax.rsqrt(ms + EPS) * nw).astype(BF16)
        return carry
    lax.fori_loop(0, 2 * h, gate2_norm, 0)


def _hyena(nw_col, hy4, khat, consts, ch_chunk=16, unroll=8):
    B, h, C3, _ = hy4.shape
    C = C3 // 3
    n_ch = GROUP_W
    G = C // n_ch
    N1 = consts["N1"]
    const = lambda g, p: (0, 0)
    blk = (2, h, n_ch, LANES)
    pitch = n_ch + 8
    return pl.pallas_call(
        functools.partial(_hyena_kernel, ch_chunk=ch_chunk, unroll=unroll),
        grid=(G, B // 2),
        in_specs=[
            pl.BlockSpec((n_ch, 1), lambda g, p: (g, 0)),
            pl.BlockSpec(blk, lambda g, p: (p, 0, g, 0)),
            pl.BlockSpec(blk, lambda g, p: (p, 0, G + g, 0)),
            pl.BlockSpec(blk, lambda g, p: (p, 0, 2 * G + g, 0)),
            pl.BlockSpec((2, n_ch, N1, 2 * LANES), lambda g, p: (0, g, 0, 0)),
            pl.BlockSpec((2 * N1, 2 * h), const),
            pl.BlockSpec((2 * LANES, 2 * LANES), const),
            pl.BlockSpec((2 * LANES, 2 * LANES), const),
            pl.BlockSpec((2 * h, 2 * N1), const),
            pl.BlockSpec((N1, LANES), const),
            pl.BlockSpec((N1, LANES), const),
        ],
        out_specs=pl.BlockSpec(blk, lambda g, p: (p, 0, g, 0)),
        out_shape=jax.ShapeDtypeStruct((B, h, C, LANES), BF16),
        scratch_shapes=[
            pltpu.VMEM((2 * h * pitch, LANES), F32),
            pltpu.VMEM((n_ch * N1, 2 * LANES), BF16),
            pltpu.VMEM((n_ch, 2 * N1, LANES), BF16),
            pltpu.VMEM((2 * h * pitch, LANES), F32),
        ],
        compiler_params=_cparams("parallel", "arbitrary"),
        name="hyena",
    )(nw_col, hy4, hy4, hy4, khat,
      _bf(consts["s1c"]), _bf(consts["s2"]), _bf(consts["s2i"]), _bf(consts["s1i"]),
      jnp.asarray(consts["twr"]), jnp.asarray(consts["twi"]))


def _outmlp_kernel(x_ref, ym_ref, yh_ref, wo_ref, w1_ref, w2_ref, n_post_ref, n_pre_ref,
                   n_post2_ref, out_ref):
    n_t = ym_ref.shape[1]
    tiles = []
    for j in range(n_t):
        yt = jnp.concatenate([ym_ref[0, j], yh_ref[0, j]], axis=0)
        tiles.append(yt.astype(F32).T.astype(BF16))
    y = tiles[0] if n_t == 1 else jnp.concatenate(tiles, axis=0)
    mix = jnp.dot(y, wo_ref[...], preferred_element_type=F32)
    x1 = x_ref[0] + _rms_rows(mix, n_post_ref[...])
    hm = _rms_rows(x1, n_pre_ref[...]).astype(BF16)
    mid = jnp.maximum(jnp.dot(hm, w1_ref[...], preferred_element_type=F32), 0.0)
    mid = (mid * mid).astype(BF16)
    ff = jnp.dot(mid, w2_ref[...], preferred_element_type=F32)
    out_ref[0] = x1 + _rms_rows(ff, n_post2_ref[...])


def _outmlp(x, ym, yh, wo, w1, w2, n_post, n_pre, n_post2, tm_rows):
    B, S, D = x.shape
    n_t = tm_rows // LANES
    const = lambda b, i: (0, 0)
    resident = functools.partial(pl.BlockSpec, index_map=const, pipeline_mode=pl.Buffered(1))
    return pl.pallas_call(
        _outmlp_kernel,
        grid=(B, S // tm_rows),
        in_specs=[
            pl.BlockSpec((1, tm_rows, D), lambda b, i: (b, i, 0)),
            pl.BlockSpec((1, n_t, M_WIDTH, LANES), lambda b, i: (b, i, 0, 0)),
            pl.BlockSpec((1, n_t, H_WIDTH, LANES), lambda b, i: (b, i, 0, 0)),
            resident((D, D)),
            resident((D, D_FF)),
            resident((D_FF, D)),
            pl.BlockSpec((1, D), const),
            pl.BlockSpec((1, D), const),
            pl.BlockSpec((1, D), const),
        ],
        out_specs=pl.BlockSpec((1, tm_rows, D), lambda b, i: (b, i, 0)),
        out_shape=jax.ShapeDtypeStruct((B, S, D), F32),
        compiler_params=_cparams("parallel", "parallel"),
        name="outmlp",
    )(x, ym, yh, wo, w1, w2, n_post, n_pre, n_post2)


def kernel(x, norm_mix_pre, norm_mix_post, norm_mlp_pre, norm_mlp_post, w_in, b_gates,
           conv_w, conv_b, mlstm_norm_w, hyena_norm_w, filt_w1, filt_b1, filt_w2, filt_b2,
           filt_w3, filt_b3, filt_w4, filt_freq, filt_bias, w_out, w_mlp_in, w_mlp_out):
    B, S, D = x.shape
    assert D == D_MODEL and B % 2 == 0 and S % CHUNK == 0
    H = M_HEADS
    row = lambda a: a.astype(F32).reshape(1, -1)
    col = lambda a: a.astype(F32).reshape(-1, 1)
    tm_rows = min(512, S)
    cb_rows = 256

    w_in = w_in.astype(F32)
    o_k = M_WIDTH
    o_hv = 2 * M_WIDTH
    o_mv = o_hv + 3 * H_WIDTH
    o_mo = o_mv + M_WIDTH
    o_gate = o_mo + M_WIDTH
    w_k = w_in[:, o_k:o_hv].astype(BF16)
    w_gt = w_in[:, o_gate:o_gate + N_GATE].T.astype(BF16)
    w_mt = jnp.concatenate([w_in[:, 0:o_k], w_in[:, o_mv:o_gate]], axis=1).T.astype(BF16)
    w_ht = w_in[:, o_hv:o_mv].T.astype(BF16)
    cw = conv_w.astype(F32)
    cbias = conv_b.astype(F32)
    n_m = 3 * M_WIDTH
    cw_m = jnp.pad(cw[:, 0:o_k].T, ((0, n_m - M_WIDTH), (0, 0)))
    cb_m = jnp.pad(col(cbias[0:o_k]), ((0, n_m - M_WIDTH), (0, 0)))
    per = M_WIDTH // cb_rows
    modes_m = ["conv_silu"] * per + ["plain"] * per + ["sigmoid"] * per

    hn, k_tm, gcm = _inproj(x, row(norm_mix_pre), w_k, w_gt, col(b_gates), tm_rows)
    mt = _cmproj(hn, w_mt, cw_m, cb_m, cb_rows, modes_m, "cmproj_m")
    hy = _cmproj(hn, w_ht, cw[:, o_hv:o_mv].T, col(cbias[o_hv:o_mv]), cb_rows,
                 ["conv"] * (3 * H_WIDTH // cb_rows), "cmproj_h")

    g4 = gcm.reshape(B, 4, H, S).transpose(0, 2, 1, 3)
    g8 = jnp.concatenate([g4, jnp.zeros_like(g4)], axis=2)
    y_m = _mlstm(mt, k_tm, g8, cw[:, o_k:o_hv], row(cbias[o_k:o_hv]), col(mlstm_norm_w))

    consts = _dft_constants(S)
    h_rows = consts["h"]
    filt = _filt_mlp(filt_w1.astype(F32).T, col(filt_b1), filt_w2.astype(F32).T, col(filt_b2),
                     filt_w3.astype(F32).T, col(filt_b3), filt_freq.astype(F32).T,
                     filt_w4.astype(F32).T, S)
    filt5 = filt.reshape(2, 2, H_WIDTH, h_rows, LANES)
    khat = _filt_fft(filt_bias.astype(F32).reshape(2, H_WIDTH, 1, 1), filt5, consts)

    y_h = _hyena(col(hyena_norm_w), hy, khat, consts)

    return _outmlp(x, y_m, y_h, w_out.astype(BF16), w_mlp_in.astype(BF16), w_mlp_out.astype(BF16),
                   row(norm_mix_post), row(norm_mlp_pre), row(norm_mlp_post), min(256, S))
```

```python
import functools
import math

import numpy as np
import jax
import jax.numpy as jnp
from jax import lax
from jax.experimental import pallas as pl
from jax.experimental.pallas import tpu as pltpu

F32 = jnp.float32
BF16 = jnp.bfloat16

D_MODEL = 1024
M_WIDTH = 512
M_HEADS = 4
HEAD_DIM = 128
H_WIDTH = 512
H_GROUPS = 8
GROUP_W = H_WIDTH // H_GROUPS
CHUNK = 128
FILTER_EMB = 33
FILTER_HIDDEN = 64
DECAY_TARGET = 1e-2
FAST_DECAY_PCT = 0.3
SLOW_DECAY_PCT = 1.5
D_FF = 4 * D_MODEL
N_GATE = 16
EPS = 1e-6
LANES = 128
BF16_ROWS = 16
NEG_BIG = -1e30
VMEM_LIMIT = 56 * 1024 * 1024


def _cparams(*sem):
    return pltpu.CompilerParams(dimension_semantics=sem, vmem_limit_bytes=VMEM_LIMIT)


def _rms_rows(xf, w):
    ms = jnp.mean(xf * xf, axis=-1, keepdims=True)
    return xf * lax.rsqrt(ms + EPS) * w


def _sigmoid(x):
    return 1.0 / (1.0 + jnp.exp(-x))


def _log_sigmoid(x):
    return jnp.minimum(x, 0.0) - jnp.log(1.0 + jnp.exp(-jnp.abs(x)))


def _inproj_kernel(x_ref, nw_ref, wtm_ref, wgt_ref, bgt_ref, hn_ref, tm_ref, gcm_ref):
    x = x_ref[0]
    hn = _rms_rows(x, nw_ref[...]).astype(BF16)
    hn_ref[0] = hn
    tm_ref[0] = jnp.dot(hn, wtm_ref[...], preferred_element_type=F32).astype(BF16)
    gt = lax.dot_general(wgt_ref[...], hn, (((1,), (1,)), ((), ())),
                         preferred_element_type=F32) + bgt_ref[...]
    row = lax.broadcasted_iota(jnp.int32, gt.shape, 0)
    is_ft = (row % 8) >= 4
    gcm_ref[0] = jnp.where(is_ft, _log_sigmoid(gt), gt)


def _inproj(x, nw, wtm, wgt, bgt, tm_rows):
    B, S, D = x.shape
    n_tm = wtm.shape[1]
    grid = (B, S // tm_rows)
    const = lambda b, j: (0, 0)
    return pl.pallas_call(
        _inproj_kernel,
        grid=grid,
        in_specs=[
            pl.BlockSpec((1, tm_rows, D), lambda b, j: (b, j, 0)),
            pl.BlockSpec((1, D), const),
            pl.BlockSpec((D, n_tm), const),
            pl.BlockSpec((N_GATE, D), const),
            pl.BlockSpec((N_GATE, 1), const),
        ],
        out_specs=[
            pl.BlockSpec((1, tm_rows, D), lambda b, j: (b, j, 0)),
            pl.BlockSpec((1, tm_rows, n_tm), lambda b, j: (b, j, 0)),
            pl.BlockSpec((1, N_GATE, tm_rows), lambda b, j: (b, 0, j)),
        ],
        out_shape=[
            jax.ShapeDtypeStruct((B, S, D), BF16),
            jax.ShapeDtypeStruct((B, S, n_tm), BF16),
            jax.ShapeDtypeStruct((B, N_GATE, S), F32),
        ],
        compiler_params=_cparams("parallel", "parallel"),
        name="inproj",
    )(x, nw, wtm, wgt, bgt)


def _cmproj_kernel(hn_ref, wt_ref, cw_ref, cb_ref, out_ref, *, modes):
    S = hn_ref.shape[1]
    p = lax.dot_general(wt_ref[...], hn_ref[0], (((1,), (1,)), ((), ())),
                        preferred_element_type=F32)

    def emit(mode):
        if mode == "plain":
            out = p
        elif mode == "sigmoid":
            out = _sigmoid(p)
        else:
            lane = lax.broadcasted_iota(jnp.int32, p.shape, 1)
            prev = jnp.where(lane == 0, 0.0, pltpu.roll(p, 1, 1))
            nxt = jnp.where(lane == S - 1, 0.0, pltpu.roll(p, S - 1, 1))
            cw = cw_ref[...]
            out = prev * cw[:, 0:1] + p * cw[:, 1:2] + nxt * cw[:, 2:3] + cb_ref[...]
            if mode == "conv_silu":
                out = out * _sigmoid(out)
        out = out.astype(BF16)
        for j in range(S // LANES):
            out_ref[0, j] = out[:, j * LANES:(j + 1) * LANES]

    distinct = sorted(set(modes))
    if len(distinct) == 1:
        emit(distinct[0])
    else:
        blk = pl.program_id(1)
        for mode in distinct:
            hit = functools.reduce(jnp.logical_or, [blk == i for i, m in enumerate(modes) if m == mode])
            pl.when(hit)(functools.partial(emit, mode))


def _cmproj(hn, wt, cw, cb, cb_rows, modes, name):
    B, S, D = hn.shape
    C = wt.shape[0]
    assert len(modes) == C // cb_rows
    return pl.pallas_call(
        functools.partial(_cmproj_kernel, modes=tuple(modes)),
        grid=(B, C // cb_rows),
        in_specs=[
            pl.BlockSpec((1, S, D), lambda b, c: (b, 0, 0)),
            pl.BlockSpec((cb_rows, D), lambda b, c: (c, 0)),
            pl.BlockSpec((cb_rows, 3), lambda b, c: (c, 0)),
            pl.BlockSpec((cb_rows, 1), lambda b, c: (c, 0)),
        ],
        out_specs=pl.BlockSpec((1, S // LANES, cb_rows, LANES), lambda b, c: (b, 0, c, 0)),
        out_shape=jax.ShapeDtypeStruct((B, S // LANES, C, LANES), BF16),
        compiler_params=_cparams("parallel", "arbitrary"),
        name=name,
    )(hn, wt, cw, cb)


def _mlstm_kernel(qt_ref, vt_ref, ot_ref, k_ref, g_ref, cw_ref, cb_ref, nw_ref, out_ref,
                  k_s, va_s, rows_s, h_s, c_s):
    S = k_ref.shape[1]
    HB = g_ref.shape[1]
    L = CHUNK
    NC = S // L
    D = HEAD_DIM
    DA = D + BF16_ROWS
    W = HB * D

    for hd in range(HB):
        g = g_ref[0, hd]
        lane = lax.broadcasted_iota(jnp.int32, g.shape, 1) % L
        pre = g
        suf = g
        d = 1
        while d < L:
            pre = pre + jnp.where(lane >= d, pltpu.roll(pre, d, 1), 0.0)
            suf = suf + jnp.where(lane < L - d, pltpu.roll(suf, S - d, 1), 0.0)
            d *= 2
        rows_s[hd, 0:1, :] = g[0:1, :]
        rows_s[hd, 1:2, :] = pre[1:2, :]
        rows_s[hd, 2:3, :] = g[2:3, :]
        rows_s[hd, 3:4, :] = suf[3:4, :]

    cw = cw_ref[...]
    cb = cb_ref[...]
    row_id = lax.broadcasted_iota(jnp.int32, (L, W), 0)
    ones_row = (lax.broadcasted_iota(jnp.int32, (BF16_ROWS, L), 0) == 0).astype(BF16)
    k_scale = HEAD_DIM ** -0.5

    def prep(c, carry):
        r0 = pl.multiple_of(c * L, L)
        cur = k_ref[0, pl.ds(r0, L), :].astype(F32)
        rp = pl.multiple_of(jnp.maximum(r0 - BF16_ROWS, 0), BF16_ROWS)
        rn = pl.multiple_of(jnp.minimum(r0 + L, S - BF16_ROWS), BF16_ROWS)
        halo_p = (k_ref[0, pl.ds(rp, BF16_ROWS), :].astype(F32)[BF16_ROWS - 1:BF16_ROWS, :]
                  * jnp.where(c > 0, 1.0, 0.0))
        halo_n = (k_ref[0, pl.ds(rn, BF16_ROWS), :].astype(F32)[0:1, :]
                  * jnp.where(c < NC - 1, 1.0, 0.0))
        prev = jnp.where(row_id == 0, halo_p, pltpu.roll(cur, 1, 0))
        nxt = jnp.where(row_id == L - 1, halo_n, pltpu.roll(cur, L - 1, 0))
        cv = prev * cw[0:1, :] + cur * cw[1:2, :] + nxt * cw[2:3, :] + cb
        k_s[pl.ds(r0, L), :] = (cv * _sigmoid(cv) * k_scale).astype(BF16)
        h_s[:, pl.ds(r0, L)] = jnp.zeros((W, L), F32)
        for hd in range(HB):
            va_s[hd, 0:D, pl.ds(r0, L)] = vt_ref[0, c, hd * D:(hd + 1) * D, :]
            va_s[hd, D:DA, pl.ds(r0, L)] = ones_row
        return carry

    lax.fori_loop(0, NC, prep, 0)
    c_s[...] = jnp.zeros_like(c_s)

    si = lax.broadcasted_iota(jnp.int32, (L, L), 0)
    ti = lax.broadcasted_iota(jnp.int32, (L, L), 1)
    eye = si == ti
    dirs = ((0, 1, si <= ti, L - 1), (2, 3, si >= ti, 0))

    def chain(hd, dr, c, m):
        li_row, b_row, mask, g_lane = dirs[dr]
        r0 = pl.multiple_of(c * L, L)
        r_li = rows_s[hd, li_row:li_row + 1, pl.ds(r0, L)]
        r_b = rows_s[hd, b_row:b_row + 1, pl.ds(r0, L)]
        g_tot = r_b[:, g_lane:g_lane + 1]
        r_a = g_tot - r_b + r_li
        m_new = jnp.maximum(g_tot + m, jnp.max(r_a, axis=1, keepdims=True))
        decay = jnp.exp(g_tot + m - m_new)
        r_w = jnp.exp(r_a - m_new)
        a_col = jnp.sum(jnp.where(eye, r_li - r_b, 0.0), axis=1, keepdims=True)
        dmat = jnp.where(mask, a_col + r_b, NEG_BIG)
        inter_log = r_b + m
        m_t = jnp.maximum(inter_log, jnp.max(dmat, axis=0, keepdims=True))
        wts = jnp.exp(dmat - m_t)
        kc = k_s[pl.ds(r0, L), hd * D:(hd + 1) * D]
        qtc = qt_ref[0, c, hd * D:(hd + 1) * D, :]
        vac = va_s[hd, :, pl.ds(r0, L)]
        cmat = c_s[hd, dr]
        both = jnp.dot(jnp.concatenate([kc, cmat.astype(BF16)], axis=0), qtc,
                       preferred_element_type=F32)
        s_w = (both[0:L] * wts).astype(BF16)
        tot = (jnp.dot(vac, s_w, preferred_element_type=F32)
               + jnp.exp(inter_log - m_t) * both[L:L + DA])
        den = jnp.maximum(jnp.abs(tot[D:D + 1]), jnp.exp(-m_t))
        h = tot[0:D] * (1.0 / den)
        h_s[hd * D:(hd + 1) * D, pl.ds(r0, L)] = h_s[hd * D:(hd + 1) * D, pl.ds(r0, L)] + h
        u = (vac.astype(F32) * r_w).astype(BF16)
        c_s[hd, dr] = decay * cmat + jnp.dot(u, kc, preferred_element_type=F32)
        return m_new

    def step(i, ms):
        out = []
        for hd in range(HB):
            out.append(chain(hd, 0, i, ms[2 * hd]))
            out.append(chain(hd, 1, NC - 1 - i, ms[2 * hd + 1]))
        return tuple(out)

    lax.fori_loop(0, NC, step, tuple(jnp.zeros((1, 1), F32) for _ in range(2 * HB)))

    nw = jnp.broadcast_to(nw_ref[...], (W, L))

    def fin(c, carry):
        r0 = pl.multiple_of(c * L, L)
        hm = h_s[:, pl.ds(r0, L)] * ot_ref[0, c].astype(F32)
        for hd in range(HB):
            sl = slice(hd * D, (hd + 1) * D)
            hh = hm[sl]
            ms = jnp.mean(hh * hh, axis=0, keepdims=True)
            out_ref[0, c, sl, :] = (hh * lax.rsqrt(ms + EPS) * nw[sl]).astype(BF16)
        return carry

    lax.fori_loop(0, NC, fin, 0, unroll=2)


def _mlstm(mt, k_tm, gcm, cwk, cbk, nw_col, heads_per_step=2):
    B, S, _ = k_tm.shape
    NC = S // CHUNK
    D = HEAD_DIM
    HB = heads_per_step
    W = HB * D
    NB = M_HEADS // HB
    DA = D + BF16_ROWS
    cm_blk = (1, NC, W, LANES)
    return pl.pallas_call(
        _mlstm_kernel,
        grid=(B, NB),
        in_specs=[
            pl.BlockSpec(cm_blk, lambda b, h: (b, 0, h, 0)),
            pl.BlockSpec(cm_blk, lambda b, h: (b, 0, NB + h, 0)),
            pl.BlockSpec(cm_blk, lambda b, h: (b, 0, 2 * NB + h, 0)),
            pl.BlockSpec((1, S, W), lambda b, h: (b, 0, h)),
            pl.BlockSpec((1, HB, 8, S), lambda b, h: (b, h, 0, 0)),
            pl.BlockSpec((3, W), lambda b, h: (0, h)),
            pl.BlockSpec((1, W), lambda b, h: (0, h)),
            pl.BlockSpec((W, 1), lambda b, h: (h, 0)),
        ],
        out_specs=pl.BlockSpec(cm_blk, lambda b, h: (b, 0, h, 0)),
        out_shape=jax.ShapeDtypeStruct((B, NC, M_WIDTH, LANES), BF16),
        scratch_shapes=[
            pltpu.VMEM((S, W), BF16),
            pltpu.VMEM((HB, DA, S), BF16),
            pltpu.VMEM((HB, 8, S), F32),
            pltpu.VMEM((W, S), F32),
            pltpu.VMEM((HB, 2, DA, D), F32),
        ],
        compiler_params=_cparams("parallel", "parallel"),
        name="mlstm",
    )(mt, mt, mt, k_tm, gcm, cwk, cbk, nw_col)


def _filt_mlp_kernel(w1t_ref, b1_ref, w2t_ref, b2_ref, w3t_ref, b3_ref, fr_ref, w4t_ref,
                     out_ref, h3_s, *, cb_rows):
    S = out_ref.shape[1]
    hi = lax.Precision.HIGHEST
    pos = lax.broadcasted_iota(jnp.int32, (1, S), 1).astype(F32)
    t = pos / (S - 1)

    @pl.when(pl.program_id(0) == 0)
    def _():
        bands = (FILTER_EMB - 1) // 2
        ang = (2.0 * math.pi) * pos / S
        fidx = lax.broadcasted_iota(jnp.int32, (bands, 1), 0).astype(F32)
        f = 1e-4 + fidx * ((bands - 1 - 1e-4) / (bands - 1))
        fa = f * ang
        w1t = w1t_ref[...]
        pre = (w1t[:, 0:1] * t
               + jnp.dot(w1t[:, 1:1 + bands], jnp.cos(fa), precision=hi, preferred_element_type=F32)
               - jnp.dot(w1t[:, 1 + bands:], jnp.sin(fa), precision=hi, preferred_element_type=F32))
        fr = fr_ref[...]
        h = jnp.sin(fr[:, 0:1] * (pre + b1_ref[...]))
        h = jnp.sin(fr[:, 1:2] * (jnp.dot(w2t_ref[...], h, precision=hi, preferred_element_type=F32)
                                  + b2_ref[...]))
        h = jnp.sin(fr[:, 2:3] * (jnp.dot(w3t_ref[...], h, precision=hi, preferred_element_type=F32)
                                  + b3_ref[...]))
        h3_s[...] = h

    filt = jnp.dot(w4t_ref[...].astype(BF16), h3_s[...].astype(BF16),
                   preferred_element_type=F32)
    r = pl.program_id(0) * cb_rows + lax.broadcasted_iota(jnp.int32, (cb_rows, 1), 0)
    ch = (r % H_WIDTH).astype(F32)
    max_decay = math.log(DECAY_TARGET) / FAST_DECAY_PCT
    min_decay = math.log(DECAY_TARGET) / SLOW_DECAY_PCT
    delta = min_decay + ch * ((max_decay - min_decay) / (H_WIDTH - 1))
    filt = filt * jnp.exp(-t * jnp.abs(delta))
    is_bwd = r >= 2 * H_WIDTH
    lane = lax.broadcasted_iota(jnp.int32, (cb_rows, S), 1)
    out_ref[...] = jnp.where(jnp.logical_and(is_bwd, lane == 0), 0.0, filt)


def _filt_mlp(w1t, b1, w2t, b2, w3t, b3, fr, w4t, S, cb_rows=256):
    R = w4t.shape[0]
    Hd = FILTER_HIDDEN
    const = lambda i: (0, 0)
    return pl.pallas_call(
        functools.partial(_filt_mlp_kernel, cb_rows=cb_rows),
        grid=(R // cb_rows,),
        in_specs=[
            pl.BlockSpec((Hd, FILTER_EMB), const),
            pl.BlockSpec((Hd, 1), const),
            pl.BlockSpec((Hd, Hd), const),
            pl.BlockSpec((Hd, 1), const),
            pl.BlockSpec((Hd, Hd), const),
            pl.BlockSpec((Hd, 1), const),
            pl.BlockSpec((Hd, 3), const),
            pl.BlockSpec((cb_rows, Hd), lambda i: (i, 0)),
        ],
        out_specs=pl.BlockSpec((cb_rows, S), lambda i: (i, 0)),
        out_shape=jax.ShapeDtypeStruct((R, S), F32),
        scratch_shapes=[pltpu.VMEM((Hd, S), F32)],
        compiler_params=_cparams("arbitrary"),
        name="filt_mlp",
    )(w1t, b1, w2t, b2, w3t, b3, fr, w4t)


@functools.lru_cache(maxsize=None)
def _dft_constants(S):
    N = 2 * S
    N2 = LANES
    N1 = N // N2
    h = N1 // 2
    k1 = np.arange(N1)
    k2 = np.arange(N2)
    a1 = -2.0 * np.pi * np.outer(k1, k1) / N1
    f1r, f1i = np.cos(a1), np.sin(a1)
    at = -2.0 * np.pi * np.outer(k1, k2) / N
    twr, twi = np.cos(at), np.sin(at)
    a2 = -2.0 * np.pi * np.outer(k2, k2) / N2
    f2r, f2i = np.cos(a2), np.sin(a2)
    s1c = np.block([[f1r[:, :h], -f1i[:, :h]], [f1i[:, :h], f1r[:, :h]]])
    s1r = np.concatenate([f1r[:, :h], f1i[:, :h]], axis=0)
    s2 = np.block([[f2r, f2i], [-f2i, f2r]])
    s2i = np.block([[f2r, -f2i], [f2i, f2r]])
    s1i = np.block([[f1r[:h, :], f1i[:h, :]], [-f1i[:h, :], f1r[:h, :]]])
    cast = lambda a: np.asarray(a, np.float32)
    return dict(s1c=cast(s1c), s1r=cast(s1r), s2=cast(s2), s2i=cast(s2i), s1i=cast(s1i),
                twr=cast(twr), twi=cast(twi), N1=N1, h=h)


def _bf(a):
    return jnp.asarray(a, F32).astype(BF16)


def _filt_fft_kernel(bias_ref, kf_ref, kb_ref, s1r_ref, s2_ref, twr_ref, twi_ref, out_ref,
                     slabf_s, slabb_s, *, n_ch, unroll):
    N1 = twr_ref.shape[0]
    scale = 1.0 / (N1 * LANES)
    twr = twr_ref[...]
    twi = twi_ref[...]
    s1r = s1r_ref[...]

    def stage1(src_ref, dst_ref, c):
        a = jnp.dot(s1r, src_ref[0, 0, c].astype(BF16), preferred_element_type=F32)
        ar, ai = a[0:N1], a[N1:2 * N1]
        r0 = pl.multiple_of(c * N1, N1)
        dst_ref[pl.ds(r0, N1), 0:LANES] = (ar * twr - ai * twi).astype(BF16)
        dst_ref[pl.ds(r0, N1), LANES:2 * LANES] = (ar * twi + ai * twr).astype(BF16)

    def per_group(j, carry):
        for u in range(unroll):
            c = j * unroll + u
            stage1(kf_ref, slabf_s, c)
            stage1(kb_ref, slabb_s, c)
        return carry

    lax.fori_loop(0, n_ch // unroll, per_group, 0)
    xf = jnp.dot(slabf_s[...], s2_ref[...], preferred_element_type=F32)
    xb = jnp.dot(slabb_s[...], s2_ref[...], preferred_element_type=F32)
    bias = bias_ref[0]
    kr = (xf[:, 0:LANES] + xb[:, 0:LANES]).reshape(n_ch, N1, LANES)
    ki = (xf[:, LANES:2 * LANES] - xb[:, LANES:2 * LANES]).reshape(n_ch, N1, LANES)
    out_ref[0, :, :, 0:LANES] = (kr + bias) * scale
    out_ref[0, :, :, LANES:2 * LANES] = ki * scale


def _filt_fft(bias4, filt5, consts, n_ch=32, unroll=8):
    _, _, C, h, _ = filt5.shape
    N1 = consts["N1"]
    const = lambda o, c: (0, 0)
    return pl.pallas_call(
        functools.partial(_filt_fft_kernel, n_ch=n_ch, unroll=unroll),
        grid=(2, C // n_ch),
        in_specs=[
            pl.BlockSpec((1, n_ch, 1, 1), lambda o, c: (o, c, 0, 0)),
            pl.BlockSpec((1, 1, n_ch, h, LANES), lambda o, c: (0, o, c, 0, 0)),
            pl.BlockSpec((1, 1, n_ch, h, LANES), lambda o, c: (1, o, c, 0, 0)),
            pl.BlockSpec((2 * N1, h), const),
            pl.BlockSpec((2 * LANES, 2 * LANES), const),
            pl.BlockSpec((N1, LANES), const),
            pl.BlockSpec((N1, LANES), const),
        ],
        out_specs=pl.BlockSpec((1, n_ch, N1, 2 * LANES), lambda o, c: (o, c, 0, 0)),
        out_shape=jax.ShapeDtypeStruct((2, C, N1, 2 * LANES), F32),
        scratch_shapes=[pltpu.VMEM((n_ch * N1, 2 * LANES), BF16),
                        pltpu.VMEM((n_ch * N1, 2 * LANES), BF16)],
        compiler_params=_cparams("parallel", "parallel"),
        name="filt_fft",
    )(bias4, filt5, filt5, _bf(consts["s1r"]), _bf(consts["s2"]),
      jnp.asarray(consts["twr"]), jnp.asarray(consts["twi"]))


def _hyena_kernel(nw_ref, v_ref, x1_ref, x2_ref, khat_ref, s1c_ref, s2_ref, s2i_ref, s1i_ref,
                  twr_ref, twi_ref, out_ref, src_s, slab_s, q_s, y_s, *, ch_chunk, unroll, mid_unroll):
    h = v_ref.shape[1]
    n_ch = v_ref.shape[2]
    N1 = 2 * h
    twr = twr_ref[...]
    twi = twi_ref[...]
    s1c = s1c_ref[...]
    s1i = s1i_ref[...]
    n_chunks = n_ch // ch_chunk
    rows = ch_chunk * N1
    pitch = src_s.shape[0] // (2 * h)

    def tile_rows(i):
        return pl.ds(pl.multiple_of(i * pitch, 8), n_ch)

    def chan_rows(c, bb):
        return pl.ds(bb * h * pitch + c, h, stride=pitch)

    def long_conv(order):
        def fwd1(j, carry):
            for u in range(unroll):
                c = j * unroll + u
                z = jnp.concatenate([src_s[chan_rows(c, 0), :], src_s[chan_rows(c, 1), :]], axis=0)
                a = jnp.dot(s1c, z.astype(BF16), preferred_element_type=F32)
                ar, ai = a[0:N1], a[N1:2 * N1]
                r0 = pl.multiple_of(c * N1, N1)
                slab_s[pl.ds(r0, N1), 0:LANES] = (ar * twr - ai * twi).astype(BF16)
                slab_s[pl.ds(r0, N1), LANES:2 * LANES] = (ar * twi + ai * twr).astype(BF16)
            return carry
        lax.fori_loop(0, n_ch // unroll, fwd1, 0)

        def mid(j, carry):
            r0 = pl.multiple_of(j * rows, rows)
            c0 = pl.multiple_of(j * ch_chunk, ch_chunk)
            x = jnp.dot(slab_s[pl.ds(r0, rows), :], s2_ref[...], preferred_element_type=F32)
            xr = x[:, 0:LANES].reshape(ch_chunk, N1, LANES)
            xi = x[:, LANES:2 * LANES].reshape(ch_chunk, N1, LANES)
            kr = khat_ref[order, pl.ds(c0, ch_chunk), :, 0:LANES]
            ki = khat_ref[order, pl.ds(c0, ch_chunk), :, LANES:2 * LANES]
            yr = (xr * kr - xi * ki).reshape(rows, LANES).astype(BF16)
            yi = (xr * ki + xi * kr).reshape(rows, LANES).astype(BF16)
            y = jnp.concatenate([yr, yi], axis=1)
            p = jnp.dot(y, s2i_ref[...], preferred_element_type=F32)
            pr = p[:, 0:LANES].reshape(ch_chunk, N1, LANES)
            pi = p[:, LANES:2 * LANES].reshape(ch_chunk, N1, LANES)
            q_s[pl.ds(c0, ch_chunk), 0:N1, :] = (pr * twr + pi * twi).astype(BF16)
            q_s[pl.ds(c0, ch_chunk), N1:2 * N1, :] = (pi * twr - pr * twi).astype(BF16)
            return carry
        lax.fori_loop(0, n_chunks, mid, 0, unroll=mid_unroll)

        def inv1(j, carry):
            for u in range(unroll):
                c = j * unroll + u
                y = jnp.dot(s1i, q_s[c], preferred_element_type=F32)
                y_s[chan_rows(c, 0), :] = y[0:h]
                y_s[chan_rows(c, 1), :] = y[h:2 * h]
            return carry
        lax.fori_loop(0, n_ch // unroll, inv1, 0)

    def load_in(i, carry):
        src_s[tile_rows(i), :] = v_ref[i // h, i % h].astype(F32)
        return carry
    lax.fori_loop(0, 2 * h, load_in, 0, unroll=4)
    long_conv(0)

    def gate1(i, carry):
        src_s[tile_rows(i), :] = y_s[tile_rows(i), :] * x1_ref[i // h, i % h].astype(F32)
        return carry
    lax.fori_loop(0, 2 * h, gate1, 0, unroll=4)
    long_conv(1)

    nw = jnp.broadcast_to(nw_ref[...], (n_ch, LANES))

    def gate2_norm(i, carry):
        z = y_s[tile_rows(i), :] * x2_ref[i // h, i % h].astype(F32)
        ms = jnp.mean(z * z, axis=0, keepdims=True)
        out_ref[i // h, i % h] = (z * lax.rsqrt(ms + EPS) * nw).astype(BF16)
        return carry
    lax.fori_loop(0, 2 * h, gate2_norm, 0, unroll=4)


def _hyena(nw_col, hy4, khat, consts, ch_chunk=8, unroll=8, mid_unroll=4):
    B, h, C3, _ = hy4.shape
    C = C3 // 3
    n_ch = GROUP_W
    G = C // n_ch
    N1 = consts["N1"]
    const = lambda g, p: (0, 0)
    blk = (2, h, n_ch, LANES)
    pitch = n_ch + 8
    return pl.pallas_call(
        functools.partial(_hyena_kernel, ch_chunk=ch_chunk, unroll=unroll, mid_unroll=mid_unroll),
        grid=(G, B // 2),
        in_specs=[
            pl.BlockSpec((n_ch, 1), lambda g, p: (g, 0)),
            pl.BlockSpec(blk, lambda g, p: (p, 0, g, 0)),
            pl.BlockSpec(blk, lambda g, p: (p, 0, G + g, 0)),
            pl.BlockSpec(blk, lambda g, p: (p, 0, 2 * G + g, 0)),
            pl.BlockSpec((2, n_ch, N1, 2 * LANES), lambda g, p: (0, g, 0, 0)),
            pl.BlockSpec((2 * N1, 2 * h), const),
            pl.BlockSpec((2 * LANES, 2 * LANES), const),
            pl.BlockSpec((2 * LANES, 2 * LANES), const),
            pl.BlockSpec((2 * h, 2 * N1), const),
            pl.BlockSpec((N1, LANES), const),
            pl.BlockSpec((N1, LANES), const),
        ],
        out_specs=pl.BlockSpec(blk, lambda g, p: (p, 0, g, 0)),
        out_shape=jax.ShapeDtypeStruct((B, h, C, LANES), BF16),
        scratch_shapes=[
            pltpu.VMEM((2 * h * pitch, LANES), F32),
            pltpu.VMEM((n_ch * N1, 2 * LANES), BF16),
            pltpu.VMEM((n_ch, 2 * N1, LANES), BF16),
            pltpu.VMEM((2 * h * pitch, LANES), F32),
        ],
        compiler_params=_cparams("parallel", "arbitrary"),
        name="hyena",
    )(nw_col, hy4, hy4, hy4, khat,
      _bf(consts["s1c"]), _bf(consts["s2"]), _bf(consts["s2i"]), _bf(consts["s1i"]),
      jnp.asarray(consts["twr"]), jnp.asarray(consts["twi"]))


def _outmlp_kernel(x_ref, ym_ref, yh_ref, wo_ref, w1_ref, w2_ref, n_post_ref, n_pre_ref,
                   n_post2_ref, out_ref):
    n_t = ym_ref.shape[1]
    tiles = []
    for j in range(n_t):
        yt = jnp.concatenate([ym_ref[0, j], yh_ref[0, j]], axis=0)
        tiles.append(yt.astype(F32).T.astype(BF16))
    y = tiles[0] if n_t == 1 else jnp.concatenate(tiles, axis=0)
    mix = jnp.dot(y, wo_ref[...], preferred_element_type=F32)
    x1 = x_ref[0] + _rms_rows(mix, n_post_ref[...])
    hm = _rms_rows(x1, n_pre_ref[...]).astype(BF16)
    mid = jnp.maximum(jnp.dot(hm, w1_ref[...], preferred_element_type=F32), 0.0)
    mid = (mid * mid).astype(BF16)
    ff = jnp.dot(mid, w2_ref[...], preferred_element_type=F32)
    out_ref[0] = x1 + _rms_rows(ff, n_post2_ref[...])


def _outmlp(x, ym, yh, wo, w1, w2, n_post, n_pre, n_post2, tm_rows):
    B, S, D = x.shape
    n_t = tm_rows // LANES
    const = lambda b, i: (0, 0)
    resident = functools.partial(pl.BlockSpec, index_map=const, pipeline_mode=pl.Buffered(1))
    return pl.pallas_call(
        _outmlp_kernel,
        grid=(B, S // tm_rows),
        in_specs=[
            pl.BlockSpec((1, tm_rows, D), lambda b, i: (b, i, 0)),
            pl.BlockSpec((1, n_t, M_WIDTH, LANES), lambda b, i: (b, i, 0, 0)),
            pl.BlockSpec((1, n_t, H_WIDTH, LANES), lambda b, i: (b, i, 0, 0)),
            resident((D, D)),
            resident((D, D_FF)),
            resident((D_FF, D)),
            pl.BlockSpec((1, D), const),
            pl.BlockSpec((1, D), const),
            pl.BlockSpec((1, D), const),
        ],
        out_specs=pl.BlockSpec((1, tm_rows, D), lambda b, i: (b, i, 0)),
        out_shape=jax.ShapeDtypeStruct((B, S, D), F32),
        compiler_params=_cparams("parallel", "parallel"),
        name="outmlp",
    )(x, ym, yh, wo, w1, w2, n_post, n_pre, n_post2)


def kernel(x, norm_mix_pre, norm_mix_post, norm_mlp_pre, norm_mlp_post, w_in, b_gates,
           conv_w, conv_b, mlstm_norm_w, hyena_norm_w, filt_w1, filt_b1, filt_w2, filt_b2,
           filt_w3, filt_b3, filt_w4, filt_freq, filt_bias, w_out, w_mlp_in, w_mlp_out):
    B, S, D = x.shape
    assert D == D_MODEL and B % 2 == 0 and S % CHUNK == 0
    H = M_HEADS
    row = lambda a: a.astype(F32).reshape(1, -1)
    col = lambda a: a.astype(F32).reshape(-1, 1)
    tm_rows = min(512, S)
    cb_rows = 256

    w_in = w_in.astype(F32)
    o_k = M_WIDTH
    o_hv = 2 * M_WIDTH
    o_mv = o_hv + 3 * H_WIDTH
    o_mo = o_mv + M_WIDTH
    o_gate = o_mo + M_WIDTH
    w_k = w_in[:, o_k:o_hv].astype(BF16)
    w_gt = w_in[:, o_gate:o_gate + N_GATE].T.astype(BF16)
    w_mt = jnp.concatenate([w_in[:, 0:o_k], w_in[:, o_mv:o_gate]], axis=1).T.astype(BF16)
    w_ht = w_in[:, o_hv:o_mv].T.astype(BF16)
    cw = conv_w.astype(F32)
    cbias = conv_b.astype(F32)
    n_m = 3 * M_WIDTH
    cw_m = jnp.pad(cw[:, 0:o_k].T, ((0, n_m - M_WIDTH), (0, 0)))
    cb_m = jnp.pad(col(cbias[0:o_k]), ((0, n_m - M_WIDTH), (0, 0)))
    per = M_WIDTH // cb_rows
    modes_m = ["conv_silu"] * per + ["plain"] * per + ["sigmoid"] * per

    hn, k_tm, gcm = _inproj(x, row(norm_mix_pre), w_k, w_gt, col(b_gates), tm_rows)
    mt = _cmproj(hn, w_mt, cw_m, cb_m, cb_rows, modes_m, "cmproj_m")
    hy = _cmproj(hn, w_ht, cw[:, o_hv:o_mv].T, col(cbias[o_hv:o_mv]), cb_rows,
                 ["conv"] * (3 * H_WIDTH // cb_rows), "cmproj_h")

    g4 = gcm.reshape(B, 4, H, S).transpose(0, 2, 1, 3)
    g8 = jnp.concatenate([g4, jnp.zeros_like(g4)], axis=2)
    y_m = _mlstm(mt, k_tm, g8, cw[:, o_k:o_hv], row(cbias[o_k:o_hv]), col(mlstm_norm_w))

    consts = _dft_constants(S)
    h_rows = consts["h"]
    filt = _filt_mlp(filt_w1.astype(F32).T, col(filt_b1), filt_w2.astype(F32).T, col(filt_b2),
                     filt_w3.astype(F32).T, col(filt_b3), filt_freq.astype(F32).T,
                     filt_w4.astype(F32).T, S)
    filt5 = filt.reshape(2, 2, H_WIDTH, h_rows, LANES)
    khat = _filt_fft(filt_bias.astype(F32).reshape(2, H_WIDTH, 1, 1), filt5, consts)

    y_h = _hyena(col(hyena_norm_w), hy, khat, consts)

    return _outmlp(x, y_m, y_h, w_out.astype(BF16), w_mlp_in.astype(BF16), w_mlp_out.astype(BF16),
                   row(norm_mix_post), row(norm_mlp_pre), row(norm_mlp_post), min(512, S))
```

```python
import functools
import math

import numpy as np
import jax
import jax.numpy as jnp
from jax import lax
from jax.experimental import pallas as pl
from jax.experimental.pallas import tpu as pltpu

F32 = jnp.float32
BF16 = jnp.bfloat16

D_MODEL = 1024
M_WIDTH = 512
M_HEADS = 4
HEAD_DIM = 128
H_WIDTH = 512
H_GROUPS = 8
GROUP_W = H_WIDTH // H_GROUPS
CHUNK = 128
FILTER_EMB = 33
FILTER_HIDDEN = 64
DECAY_TARGET = 1e-2
FAST_DECAY_PCT = 0.3
SLOW_DECAY_PCT = 1.5
D_FF = 4 * D_MODEL
N_GATE = 16
EPS = 1e-6
LANES = 128
BF16_ROWS = 16
NEG_BIG = -1e30
VMEM_LIMIT = 56 * 1024 * 1024


def _cparams(*sem):
    return pltpu.CompilerParams(dimension_semantics=sem, vmem_limit_bytes=VMEM_LIMIT)


def _rms_rows(xf, w):
    ms = jnp.mean(xf * xf, axis=-1, keepdims=True)
    return xf * lax.rsqrt(ms + EPS) * w


def _sigmoid(x):
    return 1.0 / (1.0 + jnp.exp(-x))


def _log_sigmoid(x):
    return jnp.minimum(x, 0.0) - jnp.log(1.0 + jnp.exp(-jnp.abs(x)))


def _proj_kernel(x_ref, xp_ref, xn_ref, nw_ref, wc_ref, wn_ref, wgt_ref, bgt_ref, cw_ref, cb_ref,
                 mt_ref, hy_ref, k_ref, gcm_ref, pc_s, *, col_blk):
    TM = x_ref.shape[1]
    HALO = xp_ref.shape[1]
    n_conv = wc_ref.shape[1]
    j = pl.program_id(1)
    nw = nw_ref[...]
    hn = _rms_rows(x_ref[0], nw).astype(BF16)
    hp = _rms_rows(xp_ref[0], nw).astype(BF16)
    hx = _rms_rows(xn_ref[0], nw).astype(BF16)
    keep_p = jnp.where(j > 0, 1.0, 0.0)
    keep_n = jnp.where(j < pl.num_programs(1) - 1, 1.0, 0.0)
    gt = lax.dot_general(wgt_ref[...], hn, (((1,), (1,)), ((), ())),
                         preferred_element_type=F32) + bgt_ref[...]
    row = lax.broadcasted_iota(jnp.int32, gt.shape, 0)
    gcm_ref[0] = jnp.where((row % 8) >= 4, _log_sigmoid(gt), gt)

    k_scale = HEAD_DIM ** -0.5
    o_k = M_WIDTH
    o_h = 2 * M_WIDTH
    n_rb = TM // LANES
    for c0 in range(0, n_conv, col_blk):
        wblk = wc_ref[:, c0:c0 + col_blk]
        main = jnp.dot(hn, wblk, preferred_element_type=F32)
        prev = jnp.dot(hp, wblk, preferred_element_type=F32) * keep_p
        nxt = jnp.dot(hx, wblk, preferred_element_type=F32) * keep_n
        for u in range(col_blk // LANES):
            ls = slice(u * LANES, (u + 1) * LANES)
            sl = c0 // LANES + u
            pc_s[sl, 0:HALO, :] = prev[:, ls]
            pc_s[sl, HALO:HALO + TM, :] = main[:, ls]
            pc_s[sl, HALO + TM:2 * HALO + TM, :] = nxt[:, ls]
        for u in range(col_blk // LANES):
            sl = c0 // LANES + u
            cc = c0 + u * LANES
            cs = slice(cc, cc + LANES)
            for rb in range(n_rb):
                r = HALO + rb * LANES
                cv = (pc_s[sl, pl.ds(r - 1, LANES, stride=1), :] * cw_ref[0:1, cs]
                      + pc_s[sl, r:r + LANES, :] * cw_ref[1:2, cs]
                      + pc_s[sl, pl.ds(r + 1, LANES, stride=1), :] * cw_ref[2:3, cs]
                      + cb_ref[:, cs])
                if cc < o_k:
                    mt_ref[0, rb, cs, :] = (cv * _sigmoid(cv)).T.astype(BF16)
                elif cc < o_h:
                    k_ref[0, rb * LANES:(rb + 1) * LANES, cc - o_k:cc - o_k + LANES] = (
                        cv * _sigmoid(cv) * k_scale).astype(BF16)
                else:
                    hy_ref[0, rb, cc - o_h:cc - o_h + LANES, :] = cv.T.astype(BF16)
    for c0 in range(0, 2 * M_WIDTH, col_blk):
        pv = jnp.dot(hn, wn_ref[:, c0:c0 + col_blk], preferred_element_type=F32)
        if c0 >= M_WIDTH:
            pv = _sigmoid(pv)
        for rb in range(n_rb):
            mt_ref[0, rb, M_WIDTH + c0:M_WIDTH + c0 + col_blk, :] = (
                pv[rb * LANES:(rb + 1) * LANES].T.astype(BF16))


def _proj(x, nw, wc, wn, wgt, bgt, cw, cb, tm_rows, col_blk=256):
    B, S, D = x.shape
    HALO = BF16_ROWS
    n_conv = wc.shape[1]
    n_nc = wn.shape[1]
    n_t = tm_rows // LANES
    hb = tm_rows // HALO
    last = S // HALO - 1
    const = lambda b, j: (0, 0)
    resident = functools.partial(pl.BlockSpec, index_map=const, pipeline_mode=pl.Buffered(1))
    return pl.pallas_call(
        functools.partial(_proj_kernel, col_blk=col_blk),
        grid=(B, S // tm_rows),
        in_specs=[
            pl.BlockSpec((1, tm_rows, D), lambda b, j: (b, j, 0)),
            pl.BlockSpec((1, HALO, D), lambda b, j: (b, jnp.maximum(j * hb - 1, 0), 0)),
            pl.BlockSpec((1, HALO, D), lambda b, j: (b, jnp.minimum((j + 1) * hb, last), 0)),
            pl.BlockSpec((1, D), const),
            resident((D, n_conv)),
            resident((D, n_nc)),
            pl.BlockSpec((N_GATE, D), const),
            pl.BlockSpec((N_GATE, 1), const),
            pl.BlockSpec((3, n_conv), const),
            pl.BlockSpec((1, n_conv), const),
        ],
        out_specs=[
            pl.BlockSpec((1, n_t, 3 * M_WIDTH, LANES), lambda b, j: (b, j, 0, 0)),
            pl.BlockSpec((1, n_t, 3 * H_WIDTH, LANES), lambda b, j: (b, j, 0, 0)),
            pl.BlockSpec((1, tm_rows, M_WIDTH), lambda b, j: (b, j, 0)),
            pl.BlockSpec((1, N_GATE, tm_rows), lambda b, j: (b, 0, j)),
        ],
        out_shape=[
            jax.ShapeDtypeStruct((B, S // LANES, 3 * M_WIDTH, LANES), BF16),
            jax.ShapeDtypeStruct((B, S // LANES, 3 * H_WIDTH, LANES), BF16),
            jax.ShapeDtypeStruct((B, S, M_WIDTH), BF16),
            jax.ShapeDtypeStruct((B, N_GATE, S), F32),
        ],
        scratch_shapes=[pltpu.VMEM((n_conv // LANES, tm_rows + 2 * HALO, LANES), F32)],
        compiler_params=_cparams("parallel", "parallel"),
        name="proj",
    )(x, x, x, nw, wc, wn, wgt, bgt, cw, cb)


def _mlstm_kernel(qt_ref, vt_ref, ot_ref, k_ref, g_ref, nw_ref, out_ref,
                  va_s, rows_s, h_s, c_s):
    S = k_ref.shape[1]
    HB = g_ref.shape[1]
    L = CHUNK
    NC = S // L
    D = HEAD_DIM
    DA = D + BF16_ROWS
    W = HB * D

    for hd in range(HB):
        g = g_ref[0, hd]
        lane = lax.broadcasted_iota(jnp.int32, g.shape, 1) % L
        pre = g
        suf = g
        d = 1
        while d < L:
            pre = pre + jnp.where(lane >= d, pltpu.roll(pre, d, 1), 0.0)
            suf = suf + jnp.where(lane < L - d, pltpu.roll(suf, S - d, 1), 0.0)
            d *= 2
        rows_s[hd, 0:1, :] = g[0:1, :]
        rows_s[hd, 1:2, :] = pre[1:2, :]
        rows_s[hd, 2:3, :] = g[2:3, :]
        rows_s[hd, 3:4, :] = suf[3:4, :]

    ones_row = (lax.broadcasted_iota(jnp.int32, (BF16_ROWS, L), 0) == 0).astype(BF16)

    def prep(c, carry):
        r0 = pl.multiple_of(c * L, L)
        h_s[:, pl.ds(r0, L)] = jnp.zeros((W, L), F32)
        for hd in range(HB):
            va_s[hd, 0:D, pl.ds(r0, L)] = vt_ref[0, c, hd * D:(hd + 1) * D, :]
            va_s[hd, D:DA, pl.ds(r0, L)] = ones_row
        return carry

    lax.fori_loop(0, NC, prep, 0)
    c_s[...] = jnp.zeros_like(c_s)

    si = lax.broadcasted_iota(jnp.int32, (L, L), 0)
    ti = lax.broadcasted_iota(jnp.int32, (L, L), 1)
    eye = si == ti
    dirs = ((0, 1, si <= ti, L - 1), (2, 3, si >= ti, 0))

    def chain(hd, dr, c, m):
        li_row, b_row, mask, g_lane = dirs[dr]
        r0 = pl.multiple_of(c * L, L)
        r_li = rows_s[hd, li_row:li_row + 1, pl.ds(r0, L)]
        r_b = rows_s[hd, b_row:b_row + 1, pl.ds(r0, L)]
        g_tot = r_b[:, g_lane:g_lane + 1]
        r_a = g_tot - r_b + r_li
        m_new = jnp.maximum(g_tot + m, jnp.max(r_a, axis=1, keepdims=True))
        decay = jnp.exp(g_tot + m - m_new)
        r_w = jnp.exp(r_a - m_new)
        a_col = jnp.sum(jnp.where(eye, r_li - r_b, 0.0), axis=1, keepdims=True)
        dmat = jnp.where(mask, a_col + r_b, NEG_BIG)
        inter_log = r_b + m
        m_t = jnp.maximum(inter_log, jnp.max(dmat, axis=0, keepdims=True))
        wts = jnp.exp(dmat - m_t)
        kc = k_ref[0, pl.ds(r0, L), hd * D:(hd + 1) * D]
        qtc = qt_ref[0, c, hd * D:(hd + 1) * D, :]
        vac = va_s[hd, :, pl.ds(r0, L)]
        cmat = c_s[hd, dr]
        both = jnp.dot(jnp.concatenate([kc, cmat.astype(BF16)], axis=0), qtc,
                       preferred_element_type=F32)
        s_w = (both[0:L] * wts).astype(BF16)
        tot = (jnp.dot(vac, s_w, preferred_element_type=F32)
               + jnp.exp(inter_log - m_t) * both[L:L + DA])
        den = jnp.maximum(jnp.abs(tot[D:D + 1]), jnp.exp(-m_t))
        h = tot[0:D] * (1.0 / den)
        h_s[hd * D:(hd + 1) * D, pl.ds(r0, L)] = h_s[hd * D:(hd + 1) * D, pl.ds(r0, L)] + h
        u = (vac.astype(F32) * r_w).astype(BF16)
        c_s[hd, dr] = decay * cmat + jnp.dot(u, kc, preferred_element_type=F32)
        return m_new

    def step(i, ms):
        out = []
        for hd in range(HB):
            out.append(chain(hd, 0, i, ms[2 * hd]))
            out.append(chain(hd, 1, NC - 1 - i, ms[2 * hd + 1]))
        return tuple(out)

    lax.fori_loop(0, NC, step, tuple(jnp.zeros((1, 1), F32) for _ in range(2 * HB)))

    nw = jnp.broadcast_to(nw_ref[...], (W, L))

    def fin(c, carry):
        r0 = pl.multiple_of(c * L, L)
        hm = h_s[:, pl.ds(r0, L)] * ot_ref[0, c].astype(F32)
        for hd in range(HB):
            sl = slice(hd * D, (hd + 1) * D)
            hh = hm[sl]
            ms = jnp.mean(hh * hh, axis=0, keepdims=True)
            out_ref[0, c, sl, :] = (hh * lax.rsqrt(ms + EPS) * nw[sl]).astype(BF16)
        return carry

    lax.fori_loop(0, NC, fin, 0, unroll=2)


def _mlstm(mt, k_tm, gcm, nw_col, heads_per_step=2):
    B, S, _ = k_tm.shape
    NC = S // CHUNK
    D = HEAD_DIM
    HB = heads_per_step
    W = HB * D
    NB = M_HEADS // HB
    DA = D + BF16_ROWS
    cm_blk = (1, NC, W, LANES)
    return pl.pallas_call(
        _mlstm_kernel,
        grid=(B, NB),
        in_specs=[
            pl.BlockSpec(cm_blk, lambda b, h: (b, 0, h, 0)),
            pl.BlockSpec(cm_blk, lambda b, h: (b, 0, NB + h, 0)),
            pl.BlockSpec(cm_blk, lambda b, h: (b, 0, 2 * NB + h, 0)),
            pl.BlockSpec((1, S, W), lambda b, h: (b, 0, h)),
            pl.BlockSpec((1, HB, 8, S), lambda b, h: (b, h, 0, 0)),
            pl.BlockSpec((W, 1), lambda b, h: (h, 0)),
        ],
        out_specs=pl.BlockSpec(cm_blk, lambda b, h: (b, 0, h, 0)),
        out_shape=jax.ShapeDtypeStruct((B, NC, M_WIDTH, LANES), BF16),
        scratch_shapes=[
            pltpu.VMEM((HB, DA, S), BF16),
            pltpu.VMEM((HB, 8, S), F32),
            pltpu.VMEM((W, S), F32),
            pltpu.VMEM((HB, 2, DA, D), F32),
        ],
        compiler_params=_cparams("parallel", "parallel"),
        name="mlstm",
    )(mt, mt, mt, k_tm, gcm, nw_col)


def _filt_mlp_kernel(w1t_ref, b1_ref, w2t_ref, b2_ref, w3t_ref, b3_ref, fr_ref, w4t_ref,
                     out_ref, h3_s, *, cb_rows):
    S = out_ref.shape[1]
    hi = lax.Precision.HIGHEST
    pos = lax.broadcasted_iota(jnp.int32, (1, S), 1).astype(F32)
    t = pos / (S - 1)

    @pl.when(pl.program_id(0) == 0)
    def _():
        bands = (FILTER_EMB - 1) // 2
        ang = (2.0 * math.pi) * pos / S
        fidx = lax.broadcasted_iota(jnp.int32, (bands, 1), 0).astype(F32)
        f = 1e-4 + fidx * ((bands - 1 - 1e-4) / (bands - 1))
        fa = f * ang
        w1t = w1t_ref[...]
        pre = (w1t[:, 0:1] * t
               + jnp.dot(w1t[:, 1:1 + bands], jnp.cos(fa), precision=hi, preferred_element_type=F32)
               - jnp.dot(w1t[:, 1 + bands:], jnp.sin(fa), precision=hi, preferred_element_type=F32))
        fr = fr_ref[...]
        h = jnp.sin(fr[:, 0:1] * (pre + b1_ref[...]))
        h = jnp.sin(fr[:, 1:2] * (jnp.dot(w2t_ref[...], h, precision=hi, preferred_element_type=F32)
                                  + b2_ref[...]))
        h = jnp.sin(fr[:, 2:3] * (jnp.dot(w3t_ref[...], h, precision=hi, preferred_element_type=F32)
                                  + b3_ref[...]))
        h3_s[...] = h

    filt = jnp.dot(w4t_ref[...].astype(BF16), h3_s[...].astype(BF16),
                   preferred_element_type=F32)
    r = pl.program_id(0) * cb_rows + lax.broadcasted_iota(jnp.int32, (cb_rows, 1), 0)
    ch = (r % H_WIDTH).astype(F32)
    max_decay = math.log(DECAY_TARGET) / FAST_DECAY_PCT
    min_decay = math.log(DECAY_TARGET) / SLOW_DECAY_PCT
    delta = min_decay + ch * ((max_decay - min_decay) / (H_WIDTH - 1))
    filt = filt * jnp.exp(-t * jnp.abs(delta))
    is_bwd = r >= 2 * H_WIDTH
    lane = lax.broadcasted_iota(jnp.int32, (cb_rows, S), 1)
    out_ref[...] = jnp.where(jnp.logical_and(is_bwd, lane == 0), 0.0, filt)


def _filt_mlp(w1t, b1, w2t, b2, w3t, b3, fr, w4t, S, cb_rows=256):
    R = w4t.shape[0]
    Hd = FILTER_HIDDEN
    const = lambda i: (0, 0)
    return pl.pallas_call(
        functools.partial(_filt_mlp_kernel, cb_rows=cb_rows),
        grid=(R // cb_rows,),
        in_specs=[
            pl.BlockSpec((Hd, FILTER_EMB), const),
            pl.BlockSpec((Hd, 1), const),
            pl.BlockSpec((Hd, Hd), const),
            pl.BlockSpec((Hd, 1), const),
            pl.BlockSpec((Hd, Hd), const),
            pl.BlockSpec((Hd, 1), const),
            pl.BlockSpec((Hd, 3), const),
            pl.BlockSpec((cb_rows, Hd), lambda i: (i, 0)),
        ],
        out_specs=pl.BlockSpec((cb_rows, S), lambda i: (i, 0)),
        out_shape=jax.ShapeDtypeStruct((R, S), F32),
        scratch_shapes=[pltpu.VMEM((Hd, S), F32)],
        compiler_params=_cparams("arbitrary"),
        name="filt_mlp",
    )(w1t, b1, w2t, b2, w3t, b3, fr, w4t)


@functools.lru_cache(maxsize=None)
def _dft_constants(S):
    N = 2 * S
    N2 = LANES
    N1 = N // N2
    h = N1 // 2
    k1 = np.arange(N1)
    k2 = np.arange(N2)
    a1 = -2.0 * np.pi * np.outer(k1, k1) / N1
    f1r, f1i = np.cos(a1), np.sin(a1)
    at = -2.0 * np.pi * np.outer(k1, k2) / N
    twr, twi = np.cos(at), np.sin(at)
    a2 = -2.0 * np.pi * np.outer(k2, k2) / N2
    f2r, f2i = np.cos(a2), np.sin(a2)
    s1c = np.block([[f1r[:, :h], -f1i[:, :h]], [f1i[:, :h], f1r[:, :h]]])
    s1r = np.concatenate([f1r[:, :h], f1i[:, :h]], axis=0)
    s2 = np.block([[f2r, f2i], [-f2i, f2r]])
    s2i = np.block([[f2r, -f2i], [f2i, f2r]])
    s1i = np.block([[f1r[:h, :], f1i[:h, :]], [-f1i[:h, :], f1r[:h, :]]])
    cast = lambda a: np.asarray(a, np.float32)
    return dict(s1c=cast(s1c), s1r=cast(s1r), s2=cast(s2), s2i=cast(s2i), s1i=cast(s1i),
                twr=cast(twr), twi=cast(twi), N1=N1, h=h)


def _bf(a):
    return jnp.asarray(a, F32).astype(BF16)


def _filt_fft_kernel(bias_ref, kf_ref, kb_ref, s1r_ref, s2_ref, twr_ref, twi_ref, out_ref,
                     slabf_s, slabb_s, *, n_ch, unroll):
    N1 = twr_ref.shape[0]
    scale = 1.0 / (N1 * LANES)
    twr = twr_ref[...]
    twi = twi_ref[...]
    s1r = s1r_ref[...]

    def stage1(src_ref, dst_ref, c):
        a = jnp.dot(s1r, src_ref[0, 0, c].astype(BF16), preferred_element_type=F32)
        ar, ai = a[0:N1], a[N1:2 * N1]
        r0 = pl.multiple_of(c * N1, N1)
        dst_ref[pl.ds(r0, N1), 0:LANES] = (ar * twr - ai * twi).astype(BF16)
        dst_ref[pl.ds(r0, N1), LANES:2 * LANES] = (ar * twi + ai * twr).astype(BF16)

    def per_group(j, carry):
        for u in range(unroll):
            c = j * unroll + u
            stage1(kf_ref, slabf_s, c)
            stage1(kb_ref, slabb_s, c)
        return carry

    lax.fori_loop(0, n_ch // unroll, per_group, 0)
    xf = jnp.dot(slabf_s[...], s2_ref[...], preferred_element_type=F32)
    xb = jnp.dot(slabb_s[...], s2_ref[...], preferred_element_type=F32)
    bias = bias_ref[0]
    kr = (xf[:, 0:LANES] + xb[:, 0:LANES]).reshape(n_ch, N1, LANES)
    ki = (xf[:, LANES:2 * LANES] - xb[:, LANES:2 * LANES]).reshape(n_ch, N1, LANES)
    out_ref[0, :, :, 0:LANES] = (kr + bias) * scale
    out_ref[0, :, :, LANES:2 * LANES] = ki * scale


def _filt_fft(bias4, filt5, consts, n_ch=32, unroll=8):
    _, _, C, h, _ = filt5.shape
    N1 = consts["N1"]
    const = lambda o, c: (0, 0)
    return pl.pallas_call(
        functools.partial(_filt_fft_kernel, n_ch=n_ch, unroll=unroll),
        grid=(2, C // n_ch),
        in_specs=[
            pl.BlockSpec((1, n_ch, 1, 1), lambda o, c: (o, c, 0, 0)),
            pl.BlockSpec((1, 1, n_ch, h, LANES), lambda o, c: (0, o, c, 0, 0)),
            pl.BlockSpec((1, 1, n_ch, h, LANES), lambda o, c: (1, o, c, 0, 0)),
            pl.BlockSpec((2 * N1, h), const),
            pl.BlockSpec((2 * LANES, 2 * LANES), const),
            pl.BlockSpec((N1, LANES), const),
            pl.BlockSpec((N1, LANES), const),
        ],
        out_specs=pl.BlockSpec((1, n_ch, N1, 2 * LANES), lambda o, c: (o, c, 0, 0)),
        out_shape=jax.ShapeDtypeStruct((2, C, N1, 2 * LANES), F32),
        scratch_shapes=[pltpu.VMEM((n_ch * N1, 2 * LANES), BF16),
                        pltpu.VMEM((n_ch * N1, 2 * LANES), BF16)],
        compiler_params=_cparams("parallel", "parallel"),
        name="filt_fft",
    )(bias4, filt5, filt5, _bf(consts["s1r"]), _bf(consts["s2"]),
      jnp.asarray(consts["twr"]), jnp.asarray(consts["twi"]))


def _hyena_kernel(nw_ref, v_ref, x1_ref, x2_ref, khat_ref, s1c_ref, s2_ref, s2i_ref, s1i_ref,
                  twr_ref, twi_ref, out_ref, src_s, slab_s, q_s, y_s, *, ch_chunk, unroll, mid_unroll):
    h = v_ref.shape[1]
    n_ch = v_ref.shape[2]
    N1 = 2 * h
    twr = twr_ref[...]
    twi = twi_ref[...]
    s1c = s1c_ref[...]
    s1i = s1i_ref[...]
    n_chunks = n_ch // ch_chunk
    rows = ch_chunk * N1
    pitch = src_s.shape[0] // (2 * h)

    def tile_rows(i):
        return pl.ds(pl.multiple_of(i * pitch, 8), n_ch)

    def chan_rows(c, bb):
        return pl.ds(bb * h * pitch + c, h, stride=pitch)

    def long_conv(order):
        def fwd1(j, carry):
            for u in range(unroll):
                c = j * unroll + u
                z = jnp.concatenate([src_s[chan_rows(c, 0), :], src_s[chan_rows(c, 1), :]], axis=0)
                a = jnp.dot(s1c, z.astype(BF16), preferred_element_type=F32)
                ar, ai = a[0:N1], a[N1:2 * N1]
                r0 = pl.multiple_of(c * N1, N1)
                slab_s[pl.ds(r0, N1), 0:LANES] = (ar * twr - ai * twi).astype(BF16)
                slab_s[pl.ds(r0, N1), LANES:2 * LANES] = (ar * twi + ai * twr).astype(BF16)
            return carry
        lax.fori_loop(0, n_ch // unroll, fwd1, 0)

        def mid(j, carry):
            r0 = pl.multiple_of(j * rows, rows)
            c0 = pl.multiple_of(j * ch_chunk, ch_chunk)
            x = jnp.dot(slab_s[pl.ds(r0, rows), :], s2_ref[...], preferred_element_type=F32)
            xr = x[:, 0:LANES].reshape(ch_chunk, N1, LANES)
            xi = x[:, LANES:2 * LANES].reshape(ch_chunk, N1, LANES)
            kr = khat_ref[order, pl.ds(c0, ch_chunk), :, 0:LANES]
            ki = khat_ref[order, pl.ds(c0, ch_chunk), :, LANES:2 * LANES]
            yr = (xr * kr - xi * ki).reshape(rows, LANES).astype(BF16)
            yi = (xr * ki + xi * kr).reshape(rows, LANES).astype(BF16)
            y = jnp.concatenate([yr, yi], axis=1)
            p = jnp.dot(y, s2i_ref[...], preferred_element_type=F32)
            pr = p[:, 0:LANES].reshape(ch_chunk, N1, LANES)
            pi = p[:, LANES:2 * LANES].reshape(ch_chunk, N1, LANES)
            q_s[pl.ds(c0, ch_chunk), 0:N1, :] = (pr * twr + pi * twi).astype(BF16)
            q_s[pl.ds(c0, ch_chunk), N1:2 * N1, :] = (pi * twr - pr * twi).astype(BF16)
            return carry
        lax.fori_loop(0, n_chunks, mid, 0, unroll=mid_unroll)

        def inv1(j, carry):
            for u in range(unroll):
                c = j * unroll + u
                y = jnp.dot(s1i, q_s[c], preferred_element_type=F32)
                y_s[chan_rows(c, 0), :] = y[0:h]
                y_s[chan_rows(c, 1), :] = y[h:2 * h]
            return carry
        lax.fori_loop(0, n_ch // unroll, inv1, 0)

    def load_in(i, carry):
        src_s[tile_rows(i), :] = v_ref[i // h, i % h].astype(F32)
        return carry
    lax.fori_loop(0, 2 * h, load_in, 0, unroll=4)
    long_conv(0)

    def gate1(i, carry):
        src_s[tile_rows(i), :] = y_s[tile_rows(i), :] * x1_ref[i // h, i % h].astype(F32)
        return carry
    lax.fori_loop(0, 2 * h, gate1, 0, unroll=4)
    long_conv(1)

    nw = jnp.broadcast_to(nw_ref[...], (n_ch, LANES))

    def gate2_norm(i, carry):
        z = y_s[tile_rows(i), :] * x2_ref[i // h, i % h].astype(F32)
        ms = jnp.mean(z * z, axis=0, keepdims=True)
        out_ref[i // h, i % h] = (z * lax.rsqrt(ms + EPS) * nw).astype(BF16)
        return carry
    lax.fori_loop(0, 2 * h, gate2_norm, 0, unroll=4)


def _hyena(nw_col, hy4, khat, consts, ch_chunk=8, unroll=8, mid_unroll=4):
    B, h, C3, _ = hy4.shape
    C = C3 // 3
    n_ch = GROUP_W
    G = C // n_ch
    N1 = consts["N1"]
    const = lambda g, p: (0, 0)
    blk = (2, h, n_ch, LANES)
    pitch = n_ch + 8
    return pl.pallas_call(
        functools.partial(_hyena_kernel, ch_chunk=ch_chunk, unroll=unroll, mid_unroll=mid_unroll),
        grid=(G, B // 2),
        in_specs=[
            pl.BlockSpec((n_ch, 1), lambda g, p: (g, 0)),
            pl.BlockSpec(blk, lambda g, p: (p, 0, g, 0)),
            pl.BlockSpec(blk, lambda g, p: (p, 0, G + g, 0)),
            pl.BlockSpec(blk, lambda g, p: (p, 0, 2 * G + g, 0)),
            pl.BlockSpec((2, n_ch, N1, 2 * LANES), lambda g, p: (0, g, 0, 0)),
            pl.BlockSpec((2 * N1, 2 * h), const),
            pl.BlockSpec((2 * LANES, 2 * LANES), const),
            pl.BlockSpec((2 * LANES, 2 * LANES), const),
            pl.BlockSpec((2 * h, 2 * N1), const),
            pl.BlockSpec((N1, LANES), const),
            pl.BlockSpec((N1, LANES), const),
        ],
        out_specs=pl.BlockSpec(blk, lambda g, p: (p, 0, g, 0)),
        out_shape=jax.ShapeDtypeStruct((B, h, C, LANES), BF16),
        scratch_shapes=[
            pltpu.VMEM((2 * h * pitch, LANES), F32),
            pltpu.VMEM((n_ch * N1, 2 * LANES), BF16),
            pltpu.VMEM((n_ch, 2 * N1, LANES), BF16),
            pltpu.VMEM((2 * h * pitch, LANES), F32),
        ],
        compiler_params=_cparams("parallel", "arbitrary"),
        name="hyena",
    )(nw_col, hy4, hy4, hy4, khat,
      _bf(consts["s1c"]), _bf(consts["s2"]), _bf(consts["s2i"]), _bf(consts["s1i"]),
      jnp.asarray(consts["twr"]), jnp.asarray(consts["twi"]))


def _outmlp_kernel(x_ref, ym_ref, yh_ref, wo_ref, w1_ref, w2_ref, n_post_ref, n_pre_ref,
                   n_post2_ref, out_ref):
    n_t = ym_ref.shape[1]
    tiles = []
    for j in range(n_t):
        yt = jnp.concatenate([ym_ref[0, j], yh_ref[0, j]], axis=0)
        tiles.append(yt.astype(F32).T.astype(BF16))
    y = tiles[0] if n_t == 1 else jnp.concatenate(tiles, axis=0)
    mix = jnp.dot(y, wo_ref[...], preferred_element_type=F32)
    x1 = x_ref[0] + _rms_rows(mix, n_post_ref[...])
    hm = _rms_rows(x1, n_pre_ref[...]).astype(BF16)
    mid = jnp.maximum(jnp.dot(hm, w1_ref[...], preferred_element_type=F32), 0.0)
    mid = (mid * mid).astype(BF16)
    ff = jnp.dot(mid, w2_ref[...], preferred_element_type=F32)
    out_ref[0] = x1 + _rms_rows(ff, n_post2_ref[...])


def _outmlp(x, ym, yh, wo, w1, w2, n_post, n_pre, n_post2, tm_rows):
    B, S, D = x.shape
    n_t = tm_rows // LANES
    const = lambda b, i: (0, 0)
    resident = functools.partial(pl.BlockSpec, index_map=const, pipeline_mode=pl.Buffered(1))
    return pl.pallas_call(
        _outmlp_kernel,
        grid=(B, S // tm_rows),
        in_specs=[
            pl.BlockSpec((1, tm_rows, D), lambda b, i: (b, i, 0)),
            pl.BlockSpec((1, n_t, M_WIDTH, LANES), lambda b, i: (b, i, 0, 0)),
            pl.BlockSpec((1, n_t, H_WIDTH, LANES), lambda b, i: (b, i, 0, 0)),
            resident((D, D)),
            resident((D, D_FF)),
            resident((D_FF, D)),
            pl.BlockSpec((1, D), const),
            pl.BlockSpec((1, D), const),
            pl.BlockSpec((1, D), const),
        ],
        out_specs=pl.BlockSpec((1, tm_rows, D), lambda b, i: (b, i, 0)),
        out_shape=jax.ShapeDtypeStruct((B, S, D), F32),
        compiler_params=_cparams("parallel", "parallel"),
        name="outmlp",
    )(x, ym, yh, wo, w1, w2, n_post, n_pre, n_post2)


def kernel(x, norm_mix_pre, norm_mix_post, norm_mlp_pre, norm_mlp_post, w_in, b_gates,
           conv_w, conv_b, mlstm_norm_w, hyena_norm_w, filt_w1, filt_b1, filt_w2, filt_b2,
           filt_w3, filt_b3, filt_w4, filt_freq, filt_bias, w_out, w_mlp_in, w_mlp_out):
    B, S, D = x.shape
    assert D == D_MODEL and B % 2 == 0 and S % CHUNK == 0
    H = M_HEADS
    row = lambda a: a.astype(F32).reshape(1, -1)
    col = lambda a: a.astype(F32).reshape(-1, 1)
    tm_rows = min(1024, S)

    n_conv = 2 * M_WIDTH + 3 * H_WIDTH
    o_gate = n_conv + 2 * M_WIDTH
    w_c = w_in[:, 0:n_conv].astype(BF16)
    w_n = w_in[:, n_conv:o_gate].astype(BF16)
    w_gt = w_in[:, o_gate:o_gate + N_GATE].T.astype(BF16)

    mt, hy, k_tm, gcm = _proj(x, row(norm_mix_pre), w_c, w_n, w_gt, col(b_gates),
                              conv_w.astype(F32), row(conv_b), tm_rows)

    g4 = gcm.reshape(B, 4, H, S).transpose(0, 2, 1, 3)
    g8 = jnp.concatenate([g4, jnp.zeros_like(g4)], axis=2)
    y_m = _mlstm(mt, k_tm, g8, col(mlstm_norm_w))

    consts = _dft_constants(S)
    h_rows = consts["h"]
    filt = _filt_mlp(filt_w1.astype(F32).T, col(filt_b1), filt_w2.astype(F32).T, col(filt_b2),
                     filt_w3.astype(F32).T, col(filt_b3), filt_freq.astype(F32).T,
                     filt_w4.astype(F32).T, S)
    filt5 = filt.reshape(2, 2, H_WIDTH, h_rows, LANES)
    khat = _filt_fft(filt_bias.astype(F32).reshape(2, H_WIDTH, 1, 1), filt5, consts)

    y_h = _hyena(col(hyena_norm_w), hy, khat, consts)

    return _outmlp(x, y_m, y_h, w_out.astype(BF16), w_mlp_in.astype(BF16), w_mlp_out.astype(BF16),
                   row(norm_mix_post), row(norm_mlp_pre), row(norm_mlp_post), min(512, S))
```

```python
import functools
import math

import numpy as np
import jax
import jax.numpy as jnp
from jax import lax
from jax.experimental import pallas as pl
from jax.experimental.pallas import tpu as pltpu

F32 = jnp.float32
BF16 = jnp.bfloat16

D_MODEL = 1024
M_WIDTH = 512
M_HEADS = 4
HEAD_DIM = 128
H_WIDTH = 512
H_GROUPS = 8
GROUP_W = H_WIDTH // H_GROUPS
CHUNK = 128
FILTER_EMB = 33
FILTER_HIDDEN = 64
DECAY_TARGET = 1e-2
FAST_DECAY_PCT = 0.3
SLOW_DECAY_PCT = 1.5
D_FF = 4 * D_MODEL
N_GATE = 16
EPS = 1e-6
LANES = 128
BF16_ROWS = 16
NEG_BIG = -1e30
VMEM_LIMIT = 56 * 1024 * 1024


def _cparams(*sem):
    return pltpu.CompilerParams(dimension_semantics=sem, vmem_limit_bytes=VMEM_LIMIT)


def _rms_rows(xf, w):
    ms = jnp.mean(xf * xf, axis=-1, keepdims=True)
    return xf * lax.rsqrt(ms + EPS) * w


def _sigmoid(x):
    return 1.0 / (1.0 + jnp.exp(-x))


def _log_sigmoid(x):
    return jnp.minimum(x, 0.0) - jnp.log(1.0 + jnp.exp(-jnp.abs(x)))


def _proj_kernel(x_ref, xp_ref, xn_ref, nw_ref, w_ref, wgt_ref, bgt_ref, cw_ref, cb_ref,
                 mt_ref, hy_ref, k_ref, gcm_ref, pc_s, *, col_blk):
    TM = x_ref.shape[1]
    HALO = xp_ref.shape[1]
    n_conv = cw_ref.shape[1]
    j = pl.program_id(1)
    nw = nw_ref[...]
    hn = _rms_rows(x_ref[0], nw).astype(BF16)
    hp = _rms_rows(xp_ref[0], nw).astype(BF16)
    hx = _rms_rows(xn_ref[0], nw).astype(BF16)
    keep_p = jnp.where(j > 0, 1.0, 0.0)
    keep_n = jnp.where(j < pl.num_programs(1) - 1, 1.0, 0.0)
    gt = lax.dot_general(wgt_ref[...], hn, (((1,), (1,)), ((), ())),
                         preferred_element_type=F32) + bgt_ref[...]
    row = lax.broadcasted_iota(jnp.int32, gt.shape, 0)
    gcm_ref[0] = jnp.where((row % 8) >= 4, _log_sigmoid(gt), gt)

    k_scale = HEAD_DIM ** -0.5
    o_k = M_WIDTH
    o_h = 2 * M_WIDTH
    n_rb = TM // LANES
    for c0 in range(0, n_conv, col_blk):
        wblk = w_ref[:, c0:c0 + col_blk].astype(BF16)
        main = jnp.dot(hn, wblk, preferred_element_type=F32)
        prev = jnp.dot(hp, wblk, preferred_element_type=F32) * keep_p
        nxt = jnp.dot(hx, wblk, preferred_element_type=F32) * keep_n
        for u in range(col_blk // LANES):
            ls = slice(u * LANES, (u + 1) * LANES)
            sl = c0 // LANES + u
            pc_s[sl, 0:HALO, :] = prev[:, ls]
            pc_s[sl, HALO:HALO + TM, :] = main[:, ls]
            pc_s[sl, HALO + TM:2 * HALO + TM, :] = nxt[:, ls]
        for u in range(col_blk // LANES):
            sl = c0 // LANES + u
            cc = c0 + u * LANES
            cs = slice(cc, cc + LANES)
            for rb in range(n_rb):
                r = HALO + rb * LANES
                cv = (pc_s[sl, pl.ds(r - 1, LANES, stride=1), :] * cw_ref[0:1, cs]
                      + pc_s[sl, r:r + LANES, :] * cw_ref[1:2, cs]
                      + pc_s[sl, pl.ds(r + 1, LANES, stride=1), :] * cw_ref[2:3, cs]
                      + cb_ref[:, cs])
                if cc < o_k:
                    mt_ref[0, rb, cs, :] = (cv * _sigmoid(cv)).T.astype(BF16)
                elif cc < o_h:
                    k_ref[0, rb * LANES:(rb + 1) * LANES, cc - o_k:cc - o_k + LANES] = (
                        cv * _sigmoid(cv) * k_scale).astype(BF16)
                else:
                    hy_ref[0, rb, cc - o_h:cc - o_h + LANES, :] = cv.T.astype(BF16)
    for c0 in range(0, 2 * M_WIDTH, col_blk):
        pv = jnp.dot(hn, w_ref[:, n_conv + c0:n_conv + c0 + col_blk].astype(BF16),
                     preferred_element_type=F32)
        if c0 >= M_WIDTH:
            pv = _sigmoid(pv)
        for rb in range(n_rb):
            mt_ref[0, rb, M_WIDTH + c0:M_WIDTH + c0 + col_blk, :] = (
                pv[rb * LANES:(rb + 1) * LANES].T.astype(BF16))


def _proj(x, nw, w_in, wgt, bgt, cw, cb, tm_rows, col_blk=256):
    B, S, D = x.shape
    HALO = BF16_ROWS
    n_conv = cw.shape[1]
    n_cols = n_conv + 2 * M_WIDTH
    n_t = tm_rows // LANES
    hb = tm_rows // HALO
    last = S // HALO - 1
    const = lambda b, j: (0, 0)
    resident = functools.partial(pl.BlockSpec, index_map=const, pipeline_mode=pl.Buffered(1))
    return pl.pallas_call(
        functools.partial(_proj_kernel, col_blk=col_blk),
        grid=(B, S // tm_rows),
        in_specs=[
            pl.BlockSpec((1, tm_rows, D), lambda b, j: (b, j, 0)),
            pl.BlockSpec((1, HALO, D), lambda b, j: (b, jnp.maximum(j * hb - 1, 0), 0)),
            pl.BlockSpec((1, HALO, D), lambda b, j: (b, jnp.minimum((j + 1) * hb, last), 0)),
            pl.BlockSpec((1, D), const),
            resident((D, n_cols)),
            pl.BlockSpec((N_GATE, D), const),
            pl.BlockSpec((N_GATE, 1), const),
            pl.BlockSpec((3, n_conv), const),
            pl.BlockSpec((1, n_conv), const),
        ],
        out_specs=[
            pl.BlockSpec((1, n_t, 3 * M_WIDTH, LANES), lambda b, j: (b, j, 0, 0)),
            pl.BlockSpec((1, n_t, 3 * H_WIDTH, LANES), lambda b, j: (b, j, 0, 0)),
            pl.BlockSpec((1, tm_rows, M_WIDTH), lambda b, j: (b, j, 0)),
            pl.BlockSpec((1, N_GATE, tm_rows), lambda b, j: (b, 0, j)),
        ],
        out_shape=[
            jax.ShapeDtypeStruct((B, S // LANES, 3 * M_WIDTH, LANES), BF16),
            jax.ShapeDtypeStruct((B, S // LANES, 3 * H_WIDTH, LANES), BF16),
            jax.ShapeDtypeStruct((B, S, M_WIDTH), BF16),
            jax.ShapeDtypeStruct((B, N_GATE, S), F32),
        ],
        scratch_shapes=[pltpu.VMEM((n_conv // LANES, tm_rows + 2 * HALO, LANES), F32)],
        compiler_params=_cparams("parallel", "parallel"),
        name="proj",
    )(x, x, x, nw, w_in, wgt, bgt, cw, cb)


def _mlstm_kernel(qt_ref, vt_ref, ot_ref, k_ref, g_ref, nw_ref, out_ref,
                  va_s, rows_s, h_s, c_s):
    S = k_ref.shape[1]
    HB = g_ref.shape[1]
    L = CHUNK
    NC = S // L
    D = HEAD_DIM
    DA = D + BF16_ROWS
    W = HB * D

    for hd in range(HB):
        g = g_ref[0, hd]
        lane = lax.broadcasted_iota(jnp.int32, g.shape, 1) % L
        pre = g
        suf = g
        d = 1
        while d < L:
            pre = pre + jnp.where(lane >= d, pltpu.roll(pre, d, 1), 0.0)
            suf = suf + jnp.where(lane < L - d, pltpu.roll(suf, S - d, 1), 0.0)
            d *= 2
        rows_s[hd, 0:1, :] = g[0:1, :]
        rows_s[hd, 1:2, :] = pre[1:2, :]
        rows_s[hd, 2:3, :] = g[2:3, :]
        rows_s[hd, 3:4, :] = suf[3:4, :]

    ones_row = (lax.broadcasted_iota(jnp.int32, (BF16_ROWS, L), 0) == 0).astype(BF16)

    def prep(c, carry):
        r0 = pl.multiple_of(c * L, L)
        h_s[:, pl.ds(r0, L)] = jnp.zeros((W, L), F32)
        for hd in range(HB):
            va_s[hd, 0:D, pl.ds(r0, L)] = vt_ref[0, c, hd * D:(hd + 1) * D, :]
            va_s[hd, D:DA, pl.ds(r0, L)] = ones_row
        return carry

    lax.fori_loop(0, NC, prep, 0)
    c_s[...] = jnp.zeros_like(c_s)

    si = lax.broadcasted_iota(jnp.int32, (L, L), 0)
    ti = lax.broadcasted_iota(jnp.int32, (L, L), 1)
    eye = si == ti
    dirs = ((0, 1, si <= ti, L - 1), (2, 3, si >= ti, 0))

    def chain(hd, dr, c, m):
        li_row, b_row, mask, g_lane = dirs[dr]
        r0 = pl.multiple_of(c * L, L)
        r_li = rows_s[hd, li_row:li_row + 1, pl.ds(r0, L)]
        r_b = rows_s[hd, b_row:b_row + 1, pl.ds(r0, L)]
        g_tot = r_b[:, g_lane:g_lane + 1]
        r_a = g_tot - r_b + r_li
        m_new = jnp.maximum(g_tot + m, jnp.max(r_a, axis=1, keepdims=True))
        decay = jnp.exp(g_tot + m - m_new)
        r_w = jnp.exp(r_a - m_new)
        a_col = jnp.sum(jnp.where(eye, r_li - r_b, 0.0), axis=1, keepdims=True)
        dmat = jnp.where(mask, a_col + r_b, NEG_BIG)
        inter_log = r_b + m
        m_t = jnp.maximum(inter_log, jnp.max(dmat, axis=0, keepdims=True))
        wts = jnp.exp(dmat - m_t)
        kc = k_ref[0, pl.ds(r0, L), hd * D:(hd + 1) * D]
        qtc = qt_ref[0, c, hd * D:(hd + 1) * D, :]
        vac = va_s[hd, :, pl.ds(r0, L)]
        cmat = c_s[hd, dr]
        both = jnp.dot(jnp.concatenate([kc, cmat.astype(BF16)], axis=0), qtc,
                       preferred_element_type=F32)
        s_w = (both[0:L] * wts).astype(BF16)
        tot = (jnp.dot(vac, s_w, preferred_element_type=F32)
               + jnp.exp(inter_log - m_t) * both[L:L + DA])
        den = jnp.maximum(jnp.abs(tot[D:D + 1]), jnp.exp(-m_t))
        h = tot[0:D] * (1.0 / den)
        h_s[hd * D:(hd + 1) * D, pl.ds(r0, L)] = h_s[hd * D:(hd + 1) * D, pl.ds(r0, L)] + h
        u = (vac.astype(F32) * r_w).astype(BF16)
        c_s[hd, dr] = decay * cmat + jnp.dot(u, kc, preferred_element_type=F32)
        return m_new

    def step(i, ms):
        out = []
        for hd in range(HB):
            out.append(chain(hd, 0, i, ms[2 * hd]))
            out.append(chain(hd, 1, NC - 1 - i, ms[2 * hd + 1]))
        return tuple(out)

    lax.fori_loop(0, NC, step, tuple(jnp.zeros((1, 1), F32) for _ in range(2 * HB)))

    nw = jnp.broadcast_to(nw_ref[...], (W, L))

    def fin(c, carry):
        r0 = pl.multiple_of(c * L, L)
        hm = h_s[:, pl.ds(r0, L)] * ot_ref[0, c].astype(F32)
        for hd in range(HB):
            sl = slice(hd * D, (hd + 1) * D)
            hh = hm[sl]
            ms = jnp.mean(hh * hh, axis=0, keepdims=True)
            out_ref[0, c, sl, :] = (hh * lax.rsqrt(ms + EPS) * nw[sl]).astype(BF16)
        return carry

    lax.fori_loop(0, NC, fin, 0, unroll=2)


def _mlstm(mt, k_tm, gcm, nw_col, heads_per_step=2):
    B, S, _ = k_tm.shape
    NC = S // CHUNK
    D = HEAD_DIM
    HB = heads_per_step
    W = HB * D
    NB = M_HEADS // HB
    DA = D + BF16_ROWS
    cm_blk = (1, NC, W, LANES)
    return pl.pallas_call(
        _mlstm_kernel,
        grid=(B, NB),
        in_specs=[
            pl.BlockSpec(cm_blk, lambda b, h: (b, 0, h, 0)),
            pl.BlockSpec(cm_blk, lambda b, h: (b, 0, NB + h, 0)),
            pl.BlockSpec(cm_blk, lambda b, h: (b, 0, 2 * NB + h, 0)),
            pl.BlockSpec((1, S, W), lambda b, h: (b, 0, h)),
            pl.BlockSpec((1, HB, 8, S), lambda b, h: (b, h, 0, 0)),
            pl.BlockSpec((W, 1), lambda b, h: (h, 0)),
        ],
        out_specs=pl.BlockSpec(cm_blk, lambda b, h: (b, 0, h, 0)),
        out_shape=jax.ShapeDtypeStruct((B, NC, M_WIDTH, LANES), BF16),
        scratch_shapes=[
            pltpu.VMEM((HB, DA, S), BF16),
            pltpu.VMEM((HB, 8, S), F32),
            pltpu.VMEM((W, S), F32),
            pltpu.VMEM((HB, 2, DA, D), F32),
        ],
        compiler_params=_cparams("parallel", "parallel"),
        name="mlstm",
    )(mt, mt, mt, k_tm, gcm, nw_col)


def _filt_mlp_kernel(w1t_ref, b1_ref, w2t_ref, b2_ref, w3t_ref, b3_ref, fr_ref, w4f_ref, w4b_ref,
                     out_ref, h3_s, *, cb_rows):
    S = h3_s.shape[2]
    hi = lax.Precision.HIGHEST
    lane = lax.broadcasted_iota(jnp.int32, (1, S), 1)
    pos_f = lane.astype(F32)
    pos_b = (S - lane).astype(F32)

    def features(pos):
        bands = (FILTER_EMB - 1) // 2
        t = pos / (S - 1)
        ang = (2.0 * math.pi) * pos / S
        fidx = lax.broadcasted_iota(jnp.int32, (bands, 1), 0).astype(F32)
        f = 1e-4 + fidx * ((bands - 1 - 1e-4) / (bands - 1))
        fa = f * ang
        w1t = w1t_ref[...]
        pre = (w1t[:, 0:1] * t
               + jnp.dot(w1t[:, 1:1 + bands], jnp.cos(fa), precision=hi, preferred_element_type=F32)
               - jnp.dot(w1t[:, 1 + bands:], jnp.sin(fa), precision=hi, preferred_element_type=F32))
        fr = fr_ref[...]
        h = jnp.sin(fr[:, 0:1] * (pre + b1_ref[...]))
        h = jnp.sin(fr[:, 1:2] * (jnp.dot(w2t_ref[...], h, precision=hi, preferred_element_type=F32)
                                  + b2_ref[...]))
        return jnp.sin(fr[:, 2:3] * (jnp.dot(w3t_ref[...], h, precision=hi, preferred_element_type=F32)
                                     + b3_ref[...]))

    @pl.when(pl.program_id(0) == 0)
    def _():
        h3_s[0] = features(pos_f)
        h3_s[1] = features(pos_b)

    r = pl.program_id(0) * cb_rows + lax.broadcasted_iota(jnp.int32, (cb_rows, 1), 0)
    ch = (r % H_WIDTH).astype(F32)
    max_decay = math.log(DECAY_TARGET) / FAST_DECAY_PCT
    min_decay = math.log(DECAY_TARGET) / SLOW_DECAY_PCT
    delta = jnp.abs(min_decay + ch * ((max_decay - min_decay) / (H_WIDTH - 1)))
    n_t = S // LANES
    for half, (w_ref, pos) in enumerate(((w4f_ref, pos_f), (w4b_ref, pos_b))):
        filt = jnp.dot(w_ref[...].astype(BF16), h3_s[half].astype(BF16),
                       preferred_element_type=F32)
        filt = filt * jnp.exp(-(pos / (S - 1)) * delta)
        if half == 1:
            filt = jnp.where(lane == 0, 0.0, filt)
        for u in range(n_t):
            out_ref[half * n_t + u] = filt[:, u * LANES:(u + 1) * LANES]


def _filt_mlp(w1t, b1, w2t, b2, w3t, b3, fr, w4t, S, cb_rows=256):
    R = w4t.shape[0] // 2
    Hd = FILTER_HIDDEN
    const = lambda i: (0, 0)
    nblk = R // cb_rows
    return pl.pallas_call(
        functools.partial(_filt_mlp_kernel, cb_rows=cb_rows),
        grid=(nblk,),
        in_specs=[
            pl.BlockSpec((Hd, FILTER_EMB), const),
            pl.BlockSpec((Hd, 1), const),
            pl.BlockSpec((Hd, Hd), const),
            pl.BlockSpec((Hd, 1), const),
            pl.BlockSpec((Hd, Hd), const),
            pl.BlockSpec((Hd, 1), const),
            pl.BlockSpec((Hd, 3), const),
            pl.BlockSpec((cb_rows, Hd), lambda i: (i, 0)),
            pl.BlockSpec((cb_rows, Hd), lambda i: (nblk + i, 0)),
        ],
        out_specs=pl.BlockSpec((2 * S // LANES, cb_rows, LANES), lambda i: (0, i, 0)),
        out_shape=jax.ShapeDtypeStruct((2 * S // LANES, R, LANES), F32),
        scratch_shapes=[pltpu.VMEM((2, Hd, S), F32)],
        compiler_params=_cparams("arbitrary"),
        name="filt_mlp",
    )(w1t, b1, w2t, b2, w3t, b3, fr, w4t, w4t)


@functools.lru_cache(maxsize=None)
def _dft_constants(S):
    N = 2 * S
    N2 = LANES
    N1 = N // N2
    h = N1 // 2
    k1 = np.arange(N1)
    k2 = np.arange(N2)
    a1 = -2.0 * np.pi * np.outer(k1, k1) / N1
    f1r, f1i = np.cos(a1), np.sin(a1)
    at = -2.0 * np.pi * np.outer(k1, k2) / N
    twr, twi = np.cos(at), np.sin(at)
    a2 = -2.0 * np.pi * np.outer(k2, k2) / N2
    f2r, f2i = np.cos(a2), np.sin(a2)
    s1c = np.block([[f1r[:, :h], -f1i[:, :h]], [f1i[:, :h], f1r[:, :h]]])
    s1r = np.concatenate([f1r, f1i], axis=0)
    s2 = np.block([[f2r, f2i], [-f2i, f2r]])
    s2i = np.block([[f2r, -f2i], [f2i, f2r]])
    s1i = np.block([[f1r[:h, :], f1i[:h, :]], [-f1i[:h, :], f1r[:h, :]]])
    cast = lambda a: np.asarray(a, np.float32)
    return dict(s1c=cast(s1c), s1r=cast(s1r), s2=cast(s2), s2i=cast(s2i), s1i=cast(s1i),
                twr=cast(twr), twi=cast(twi), N1=N1, h=h)


def _bf(a):
    return jnp.asarray(a, F32).astype(BF16)


def _filt_fft_kernel(bias_ref, kern_ref, s1r_ref, s2_ref, twr_ref, twi_ref, out_ref,
                     src_s, slab_s, *, unroll):
    N1 = twr_ref.shape[0]
    n_ch = kern_ref.shape[1]
    pitch = src_s.shape[0] // N1
    scale = 1.0 / (N1 * LANES)
    twr = twr_ref[...]
    twi = twi_ref[...]
    s1r = s1r_ref[...]

    def load_in(t1, carry):
        src_s[pl.ds(pl.multiple_of(t1 * pitch, 8), n_ch), :] = kern_ref[t1]
        return carry
    lax.fori_loop(0, N1, load_in, 0, unroll=4)

    def per_group(j, carry):
        for u in range(unroll):
            c = j * unroll + u
            z = src_s[pl.ds(c, N1, stride=pitch), :].astype(BF16)
            a = jnp.dot(s1r, z, preferred_element_type=F32)
            ar, ai = a[0:N1], a[N1:2 * N1]
            r0 = pl.multiple_of(c * N1, N1)
            slab_s[pl.ds(r0, N1), 0:LANES] = (ar * twr - ai * twi).astype(BF16)
            slab_s[pl.ds(r0, N1), LANES:2 * LANES] = (ar * twi + ai * twr).astype(BF16)
        return carry

    lax.fori_loop(0, n_ch // unroll, per_group, 0)
    x = jnp.dot(slab_s[...], s2_ref[...], preferred_element_type=F32)
    bias = bias_ref[...]
    out_ref[:, :, 0:LANES] = (x[:, 0:LANES].reshape(n_ch, N1, LANES) + bias) * scale
    out_ref[:, :, LANES:2 * LANES] = x[:, LANES:2 * LANES].reshape(n_ch, N1, LANES) * scale


def _filt_fft(bias3, kern, consts, n_ch=32, unroll=8):
    N1, R, _ = kern.shape
    const = lambda i: (0, 0)
    pitch = n_ch + 8
    return pl.pallas_call(
        functools.partial(_filt_fft_kernel, unroll=unroll),
        grid=(R // n_ch,),
        in_specs=[
            pl.BlockSpec((n_ch, 1, 1), lambda i: (i, 0, 0)),
            pl.BlockSpec((N1, n_ch, LANES), lambda i: (0, i, 0)),
            pl.BlockSpec((2 * N1, N1), const),
            pl.BlockSpec((2 * LANES, 2 * LANES), const),
            pl.BlockSpec((N1, LANES), const),
            pl.BlockSpec((N1, LANES), const),
        ],
        out_specs=pl.BlockSpec((n_ch, N1, 2 * LANES), lambda i: (i, 0, 0)),
        out_shape=jax.ShapeDtypeStruct((R, N1, 2 * LANES), F32),
        scratch_shapes=[pltpu.VMEM((N1 * pitch, LANES), F32),
                        pltpu.VMEM((n_ch * N1, 2 * LANES), BF16)],
        compiler_params=_cparams("parallel"),
        name="filt_fft",
    )(bias3, kern, _bf(consts["s1r"]), _bf(consts["s2"]),
      jnp.asarray(consts["twr"]), jnp.asarray(consts["twi"]))


def _hyena_kernel(nw_ref, v_ref, x1_ref, x2_ref, khat_ref, s1c_ref, s2_ref, s2i_ref, s1i_ref,
                  twr_ref, twi_ref, out_ref, src_s, slab_s, q_s, y_s, *, ch_chunk, unroll, mid_unroll):
    h = v_ref.shape[1]
    n_ch = v_ref.shape[2]
    N1 = 2 * h
    twr = twr_ref[...]
    twi = twi_ref[...]
    s1c = s1c_ref[...]
    s1i = s1i_ref[...]
    n_chunks = n_ch // ch_chunk
    rows = ch_chunk * N1
    pitch = src_s.shape[0] // (2 * h)

    def tile_rows(i):
        return pl.ds(pl.multiple_of(i * pitch, 8), n_ch)

    def chan_rows(c, bb):
        return pl.ds(bb * h * pitch + c, h, stride=pitch)

    def long_conv(order):
        def fwd1(j, carry):
            for u in range(unroll):
                c = j * unroll + u
                z = jnp.concatenate([src_s[chan_rows(c, 0), :], src_s[chan_rows(c, 1), :]], axis=0)
                a = jnp.dot(s1c, z.astype(BF16), preferred_element_type=F32)
                ar, ai = a[0:N1], a[N1:2 * N1]
                r0 = pl.multiple_of(c * N1, N1)
                slab_s[pl.ds(r0, N1), 0:LANES] = (ar * twr - ai * twi).astype(BF16)
                slab_s[pl.ds(r0, N1), LANES:2 * LANES] = (ar * twi + ai * twr).astype(BF16)
            return carry
        lax.fori_loop(0, n_ch // unroll, fwd1, 0)

        def mid(j, carry):
            r0 = pl.multiple_of(j * rows, rows)
            c0 = pl.multiple_of(j * ch_chunk, ch_chunk)
            x = jnp.dot(slab_s[pl.ds(r0, rows), :], s2_ref[...], preferred_element_type=F32)
            xr = x[:, 0:LANES].reshape(ch_chunk, N1, LANES)
            xi = x[:, LANES:2 * LANES].reshape(ch_chunk, N1, LANES)
            kr = khat_ref[order, pl.ds(c0, ch_chunk), :, 0:LANES]
            ki = khat_ref[order, pl.ds(c0, ch_chunk), :, LANES:2 * LANES]
            yr = (xr * kr - xi * ki).reshape(rows, LANES).astype(BF16)
            yi = (xr * ki + xi * kr).reshape(rows, LANES).astype(BF16)
            y = jnp.concatenate([yr, yi], axis=1)
            p = jnp.dot(y, s2i_ref[...], preferred_element_type=F32)
            pr = p[:, 0:LANES].reshape(ch_chunk, N1, LANES)
            pi = p[:, LANES:2 * LANES].reshape(ch_chunk, N1, LANES)
            q_s[pl.ds(c0, ch_chunk), 0:N1, :] = (pr * twr + pi * twi).astype(BF16)
            q_s[pl.ds(c0, ch_chunk), N1:2 * N1, :] = (pi * twr - pr * twi).astype(BF16)
            return carry
        lax.fori_loop(0, n_chunks, mid, 0, unroll=mid_unroll)

        def inv1(j, carry):
            for u in range(unroll):
                c = j * unroll + u
                y = jnp.dot(s1i, q_s[c], preferred_element_type=F32)
                y_s[chan_rows(c, 0), :] = y[0:h]
                y_s[chan_rows(c, 1), :] = y[h:2 * h]
            return carry
        lax.fori_loop(0, n_ch // unroll, inv1, 0)

    def load_in(i, carry):
        src_s[tile_rows(i), :] = v_ref[i // h, i % h].astype(F32)
        return carry
    lax.fori_loop(0, 2 * h, load_in, 0, unroll=4)
    long_conv(0)

    def gate1(i, carry):
        src_s[tile_rows(i), :] = y_s[tile_rows(i), :] * x1_ref[i // h, i % h].astype(F32)
        return carry
    lax.fori_loop(0, 2 * h, gate1, 0, unroll=4)
    long_conv(1)

    nw = jnp.broadcast_to(nw_ref[...], (n_ch, LANES))

    def gate2_norm(i, carry):
        z = y_s[tile_rows(i), :] * x2_ref[i // h, i % h].astype(F32)
        ms = jnp.mean(z * z, axis=0, keepdims=True)
        out_ref[i // h, i % h] = (z * lax.rsqrt(ms + EPS) * nw).astype(BF16)
        return carry
    lax.fori_loop(0, 2 * h, gate2_norm, 0, unroll=4)


def _hyena(nw_col, hy4, khat, consts, ch_chunk=8, unroll=8, mid_unroll=4):
    B, h, C3, _ = hy4.shape
    C = C3 // 3
    n_ch = GROUP_W
    G = C // n_ch
    N1 = consts["N1"]
    const = lambda g, p: (0, 0)
    blk = (2, h, n_ch, LANES)
    pitch = n_ch + 8
    return pl.pallas_call(
        functools.partial(_hyena_kernel, ch_chunk=ch_chunk, unroll=unroll, mid_unroll=mid_unroll),
        grid=(G, B // 2),
        in_specs=[
            pl.BlockSpec((n_ch, 1), lambda g, p: (g, 0)),
            pl.BlockSpec(blk, lambda g, p: (p, 0, g, 0)),
            pl.BlockSpec(blk, lambda g, p: (p, 0, G + g, 0)),
            pl.BlockSpec(blk, lambda g, p: (p, 0, 2 * G + g, 0)),
            pl.BlockSpec((2, n_ch, N1, 2 * LANES), lambda g, p: (0, g, 0, 0)),
            pl.BlockSpec((2 * N1, 2 * h), const),
            pl.BlockSpec((2 * LANES, 2 * LANES), const),
            pl.BlockSpec((2 * LANES, 2 * LANES), const),
            pl.BlockSpec((2 * h, 2 * N1), const),
            pl.BlockSpec((N1, LANES), const),
            pl.BlockSpec((N1, LANES), const),
        ],
        out_specs=pl.BlockSpec(blk, lambda g, p: (p, 0, g, 0)),
        out_shape=jax.ShapeDtypeStruct((B, h, C, LANES), BF16),
        scratch_shapes=[
            pltpu.VMEM((2 * h * pitch, LANES), F32),
            pltpu.VMEM((n_ch * N1, 2 * LANES), BF16),
            pltpu.VMEM((n_ch, 2 * N1, LANES), BF16),
            pltpu.VMEM((2 * h * pitch, LANES), F32),
        ],
        compiler_params=_cparams("parallel", "arbitrary"),
        name="hyena",
    )(nw_col, hy4, hy4, hy4, khat,
      _bf(consts["s1c"]), _bf(consts["s2"]), _bf(consts["s2i"]), _bf(consts["s1i"]),
      jnp.asarray(consts["twr"]), jnp.asarray(consts["twi"]))


def _outmlp_kernel(x_ref, ym_ref, yh_ref, wo_ref, w1_ref, w2_ref, n_post_ref, n_pre_ref,
                   n_post2_ref, out_ref):
    n_t = ym_ref.shape[1]
    tiles = []
    for j in range(n_t):
        yt = jnp.concatenate([ym_ref[0, j], yh_ref[0, j]], axis=0)
        tiles.append(yt.astype(F32).T.astype(BF16))
    y = tiles[0] if n_t == 1 else jnp.concatenate(tiles, axis=0)
    mix = jnp.dot(y, wo_ref[...], preferred_element_type=F32)
    x1 = x_ref[0] + _rms_rows(mix, n_post_ref[...])
    hm = _rms_rows(x1, n_pre_ref[...]).astype(BF16)
    mid = jnp.maximum(jnp.dot(hm, w1_ref[...], preferred_element_type=F32), 0.0)
    mid = (mid * mid).astype(BF16)
    ff = jnp.dot(mid, w2_ref[...], preferred_element_type=F32)
    out_ref[0] = x1 + _rms_rows(ff, n_post2_ref[...])


def _outmlp(x, ym, yh, wo, w1, w2, n_post, n_pre, n_post2, tm_rows):
    B, S, D = x.shape
    n_t = tm_rows // LANES
    const = lambda b, i: (0, 0)
    resident = functools.partial(pl.BlockSpec, index_map=const, pipeline_mode=pl.Buffered(1))
    return pl.pallas_call(
        _outmlp_kernel,
        grid=(B, S // tm_rows),
        in_specs=[
            pl.BlockSpec((1, tm_rows, D), lambda b, i: (b, i, 0)),
            pl.BlockSpec((1, n_t, M_WIDTH, LANES), lambda b, i: (b, i, 0, 0)),
            pl.BlockSpec((1, n_t, H_WIDTH, LANES), lambda b, i: (b, i, 0, 0)),
            resident((D, D)),
            resident((D, D_FF)),
            resident((D_FF, D)),
            pl.BlockSpec((1, D), const),
            pl.BlockSpec((1, D), const),
            pl.BlockSpec((1, D), const),
        ],
        out_specs=pl.BlockSpec((1, tm_rows, D), lambda b, i: (b, i, 0)),
        out_shape=jax.ShapeDtypeStruct((B, S, D), F32),
        compiler_params=_cparams("parallel", "parallel"),
        name="outmlp",
    )(x, ym, yh, wo, w1, w2, n_post, n_pre, n_post2)


def kernel(x, norm_mix_pre, norm_mix_post, norm_mlp_pre, norm_mlp_post, w_in, b_gates,
           conv_w, conv_b, mlstm_norm_w, hyena_norm_w, filt_w1, filt_b1, filt_w2, filt_b2,
           filt_w3, filt_b3, filt_w4, filt_freq, filt_bias, w_out, w_mlp_in, w_mlp_out):
    B, S, D = x.shape
    assert D == D_MODEL and B % 2 == 0 and S % CHUNK == 0
    H = M_HEADS
    row = lambda a: a.astype(F32).reshape(1, -1)
    col = lambda a: a.astype(F32).reshape(-1, 1)
    tm_rows = min(1024, S)

    n_conv = 2 * M_WIDTH + 3 * H_WIDTH
    o_gate = n_conv + 2 * M_WIDTH
    w_gt = w_in[:, o_gate:o_gate + N_GATE].T.astype(BF16)

    mt, hy, k_tm, gcm = _proj(x, row(norm_mix_pre), w_in.astype(F32), w_gt, col(b_gates),
                              conv_w.astype(F32), row(conv_b), tm_rows)

    g4 = gcm.reshape(B, 4, H, S).transpose(0, 2, 1, 3)
    g8 = jnp.concatenate([g4, jnp.zeros_like(g4)], axis=2)
    y_m = _mlstm(mt, k_tm, g8, col(mlstm_norm_w))

    consts = _dft_constants(S)
    kern = _filt_mlp(filt_w1.astype(F32).T, col(filt_b1), filt_w2.astype(F32).T, col(filt_b2),
                     filt_w3.astype(F32).T, col(filt_b3), filt_freq.astype(F32).T,
                     filt_w4.astype(F32).T, S)
    khat = _filt_fft(filt_bias.astype(F32).reshape(2 * H_WIDTH, 1, 1), kern, consts)
    khat = khat.reshape(2, H_WIDTH, consts["N1"], 2 * LANES)

    y_h = _hyena(col(hyena_norm_w), hy, khat, consts)

    return _outmlp(x, y_m, y_h, w_out.astype(BF16), w_mlp_in.astype(BF16), w_mlp_out.astype(BF16),
                   row(norm_mix_post), row(norm_mlp_pre), row(norm_mlp_post), min(512, S))
```

```python
import functools
import math

import numpy as np
import jax
import jax.numpy as jnp
from jax import lax
from jax.experimental import pallas as pl
from jax.experimental.pallas import tpu as pltpu

F32 = jnp.float32
BF16 = jnp.bfloat16

D_MODEL = 1024
M_WIDTH = 512
M_HEADS = 4
HEAD_DIM = 128
H_WIDTH = 512
H_GROUPS = 8
GROUP_W = H_WIDTH // H_GROUPS
CHUNK = 128
FILTER_EMB = 33
FILTER_HIDDEN = 64
DECAY_TARGET = 1e-2
FAST_DECAY_PCT = 0.3
SLOW_DECAY_PCT = 1.5
D_FF = 4 * D_MODEL
N_GATE = 16
EPS = 1e-6
LANES = 128
BF16_ROWS = 16
NEG_BIG = -1e30
VMEM_LIMIT = 56 * 1024 * 1024


def _cparams(*sem):
    return pltpu.CompilerParams(dimension_semantics=sem, vmem_limit_bytes=VMEM_LIMIT)


def _rms_rows(xf, w):
    ms = jnp.mean(xf * xf, axis=-1, keepdims=True)
    return xf * lax.rsqrt(ms + EPS) * w


def _sigmoid(x):
    return 1.0 / (1.0 + jnp.exp(-x))


def _log_sigmoid(x):
    return jnp.minimum(x, 0.0) - jnp.log(1.0 + jnp.exp(-jnp.abs(x)))


def _proj_kernel(x_ref, xp_ref, xn_ref, nw_ref, wt_ref, bgt_ref, cw_ref, cb_ref,
                 mt_ref, hy_ref, k_ref, gcm_ref, pc_s, *, col_blk):
    TM = x_ref.shape[1]
    HALO = xp_ref.shape[1]
    n_conv = cw_ref.shape[1]
    j = pl.program_id(1)
    nw = nw_ref[...]
    hn = _rms_rows(x_ref[0], nw).astype(BF16)
    hp = _rms_rows(xp_ref[0], nw).astype(BF16)
    hx = _rms_rows(xn_ref[0], nw).astype(BF16)
    keep_p = jnp.where(j > 0, 1.0, 0.0)
    keep_n = jnp.where(j < pl.num_programs(1) - 1, 1.0, 0.0)
    n_cols = n_conv + 2 * M_WIDTH
    nt_dims = (((1,), (1,)), ((), ()))
    gt = lax.dot_general(wt_ref[n_cols:n_cols + N_GATE, :].astype(BF16), hn, nt_dims,
                         preferred_element_type=F32) + bgt_ref[...]
    row = lax.broadcasted_iota(jnp.int32, gt.shape, 0)
    gcm_ref[0] = jnp.where((row % 8) >= 4, _log_sigmoid(gt), gt)

    k_scale = HEAD_DIM ** -0.5
    o_k = M_WIDTH
    o_h = 2 * M_WIDTH
    n_rb = TM // LANES
    for c0 in range(0, n_conv, col_blk):
        wblk = wt_ref[c0:c0 + col_blk, :].T.astype(BF16)
        main = jnp.dot(hn, wblk, preferred_element_type=F32)
        prev = jnp.dot(hp, wblk, preferred_element_type=F32) * keep_p
        nxt = jnp.dot(hx, wblk, preferred_element_type=F32) * keep_n
        for u in range(col_blk // LANES):
            ls = slice(u * LANES, (u + 1) * LANES)
            sl = c0 // LANES + u
            pc_s[sl, 0:HALO, :] = prev[:, ls]
            pc_s[sl, HALO:HALO + TM, :] = main[:, ls]
            pc_s[sl, HALO + TM:2 * HALO + TM, :] = nxt[:, ls]
        for u in range(col_blk // LANES):
            sl = c0 // LANES + u
            cc = c0 + u * LANES
            cs = slice(cc, cc + LANES)
            for rb in range(n_rb):
                r = HALO + rb * LANES
                cv = (pc_s[sl, pl.ds(r - 1, LANES, stride=1), :] * cw_ref[0:1, cs]
                      + pc_s[sl, r:r + LANES, :] * cw_ref[1:2, cs]
                      + pc_s[sl, pl.ds(r + 1, LANES, stride=1), :] * cw_ref[2:3, cs]
                      + cb_ref[:, cs])
                if cc < o_k:
                    mt_ref[0, rb, cs, :] = (cv * _sigmoid(cv)).T.astype(BF16)
                elif cc < o_h:
                    k_ref[0, rb * LANES:(rb + 1) * LANES, cc - o_k:cc - o_k + LANES] = (
                        cv * _sigmoid(cv) * k_scale).astype(BF16)
                else:
                    hy_ref[0, rb, cc - o_h:cc - o_h + LANES, :] = cv.T.astype(BF16)
    for c0 in range(0, 2 * M_WIDTH, col_blk):
        pv = jnp.dot(hn, wt_ref[n_conv + c0:n_conv + c0 + col_blk, :].T.astype(BF16),
                     preferred_element_type=F32)
        if c0 >= M_WIDTH:
            pv = _sigmoid(pv)
        for rb in range(n_rb):
            mt_ref[0, rb, M_WIDTH + c0:M_WIDTH + c0 + col_blk, :] = (
                pv[rb * LANES:(rb + 1) * LANES].T.astype(BF16))


def _proj(x, nw, w_in_t, bgt, cw, cb, tm_rows, col_blk=256):
    B, S, D = x.shape
    HALO = BF16_ROWS
    n_conv = cw.shape[1]
    n_t = tm_rows // LANES
    hb = tm_rows // HALO
    last = S // HALO - 1
    const = lambda b, j: (0, 0)
    resident = functools.partial(pl.BlockSpec, index_map=const, pipeline_mode=pl.Buffered(1))
    return pl.pallas_call(
        functools.partial(_proj_kernel, col_blk=col_blk),
        grid=(B, S // tm_rows),
        in_specs=[
            pl.BlockSpec((1, tm_rows, D), lambda b, j: (b, j, 0)),
            pl.BlockSpec((1, HALO, D), lambda b, j: (b, jnp.maximum(j * hb - 1, 0), 0)),
            pl.BlockSpec((1, HALO, D), lambda b, j: (b, jnp.minimum((j + 1) * hb, last), 0)),
            pl.BlockSpec((1, D), const),
            resident(w_in_t.shape),
            pl.BlockSpec((N_GATE, 1), const),
            pl.BlockSpec((3, n_conv), const),
            pl.BlockSpec((1, n_conv), const),
        ],
        out_specs=[
            pl.BlockSpec((1, n_t, 3 * M_WIDTH, LANES), lambda b, j: (b, j, 0, 0)),
            pl.BlockSpec((1, n_t, 3 * H_WIDTH, LANES), lambda b, j: (b, j, 0, 0)),
            pl.BlockSpec((1, tm_rows, M_WIDTH), lambda b, j: (b, j, 0)),
            pl.BlockSpec((1, N_GATE, tm_rows), lambda b, j: (b, 0, j)),
        ],
        out_shape=[
            jax.ShapeDtypeStruct((B, S // LANES, 3 * M_WIDTH, LANES), BF16),
            jax.ShapeDtypeStruct((B, S // LANES, 3 * H_WIDTH, LANES), BF16),
            jax.ShapeDtypeStruct((B, S, M_WIDTH), BF16),
            jax.ShapeDtypeStruct((B, N_GATE, S), F32),
        ],
        scratch_shapes=[pltpu.VMEM((n_conv // LANES, tm_rows + 2 * HALO, LANES), F32)],
        compiler_params=_cparams("parallel", "parallel"),
        name="proj",
    )(x, x, x, nw, w_in_t, bgt, cw, cb)


def _mlstm_kernel(qt_ref, vt_ref, ot_ref, k_ref, g_ref, nw_ref, out_ref,
                  va_s, rows_s, h_s, c_s):
    S = k_ref.shape[1]
    HB = g_ref.shape[1]
    L = CHUNK
    NC = S // L
    D = HEAD_DIM
    DA = D + BF16_ROWS
    W = HB * D

    for hd in range(HB):
        g = g_ref[0, hd]
        lane = lax.broadcasted_iota(jnp.int32, g.shape, 1) % L
        pre = g
        suf = g
        d = 1
        while d < L:
            pre = pre + jnp.where(lane >= d, pltpu.roll(pre, d, 1), 0.0)
            suf = suf + jnp.where(lane < L - d, pltpu.roll(suf, S - d, 1), 0.0)
            d *= 2
        rows_s[hd, 0:1, :] = g[0:1, :]
        rows_s[hd, 1:2, :] = pre[1:2, :]
        rows_s[hd, 2:3, :] = g[2:3, :]
        rows_s[hd, 3:4, :] = suf[3:4, :]

    ones_row = (lax.broadcasted_iota(jnp.int32, (BF16_ROWS, L), 0) == 0).astype(BF16)

    def prep(c, carry):
        r0 = pl.multiple_of(c * L, L)
        h_s[:, pl.ds(r0, L)] = jnp.zeros((W, L), F32)
        for hd in range(HB):
            va_s[hd, 0:D, pl.ds(r0, L)] = vt_ref[0, c, hd * D:(hd + 1) * D, :]
            va_s[hd, D:DA, pl.ds(r0, L)] = ones_row
        return carry

    lax.fori_loop(0, NC, prep, 0)
    c_s[...] = jnp.zeros_like(c_s)

    si = lax.broadcasted_iota(jnp.int32, (L, L), 0)
    ti = lax.broadcasted_iota(jnp.int32, (L, L), 1)
    eye = si == ti
    dirs = ((0, 1, si <= ti, L - 1), (2, 3, si >= ti, 0))

    def chain(hd, dr, c, m):
        li_row, b_row, mask, g_lane = dirs[dr]
        r0 = pl.multiple_of(c * L, L)
        r_li = rows_s[hd, li_row:li_row + 1, pl.ds(r0, L)]
        r_b = rows_s[hd, b_row:b_row + 1, pl.ds(r0, L)]
        g_tot = r_b[:, g_lane:g_lane + 1]
        r_a = g_tot - r_b + r_li
        m_new = jnp.maximum(g_tot + m, jnp.max(r_a, axis=1, keepdims=True))
        decay = jnp.exp(g_tot + m - m_new)
        r_w = jnp.exp(r_a - m_new)
        a_col = jnp.sum(jnp.where(eye, r_li - r_b, 0.0), axis=1, keepdims=True)
        dmat = jnp.where(mask, a_col + r_b, NEG_BIG)
        inter_log = r_b + m
        m_t = jnp.maximum(inter_log, jnp.max(dmat, axis=0, keepdims=True))
        wts = jnp.exp(dmat - m_t)
        kc = k_ref[0, pl.ds(r0, L), hd * D:(hd + 1) * D]
        qtc = qt_ref[0, c, hd * D:(hd + 1) * D, :]
        vac = va_s[hd, :, pl.ds(r0, L)]
        cmat = c_s[hd, dr]
        both = jnp.dot(jnp.concatenate([kc, cmat.astype(BF16)], axis=0), qtc,
                       preferred_element_type=F32)
        s_w = (both[0:L] * wts).astype(BF16)
        tot = (jnp.dot(vac, s_w, preferred_element_type=F32)
               + jnp.exp(inter_log - m_t) * both[L:L + DA])
        den = jnp.maximum(jnp.abs(tot[D:D + 1]), jnp.exp(-m_t))
        h = tot[0:D] * (1.0 / den)
        h_s[hd * D:(hd + 1) * D, pl.ds(r0, L)] = h_s[hd * D:(hd + 1) * D, pl.ds(r0, L)] + h
        u = (vac.astype(F32) * r_w).astype(BF16)
        c_s[hd, dr] = decay * cmat + jnp.dot(u, kc, preferred_element_type=F32)
        return m_new

    def step(i, ms):
        out = []
        for hd in range(HB):
            out.append(chain(hd, 0, i, ms[2 * hd]))
            out.append(chain(hd, 1, NC - 1 - i, ms[2 * hd + 1]))
        return tuple(out)

    lax.fori_loop(0, NC, step, tuple(jnp.zeros((1, 1), F32) for _ in range(2 * HB)))

    nw = jnp.broadcast_to(nw_ref[...], (W, L))

    def fin(c, carry):
        r0 = pl.multiple_of(c * L, L)
        hm = h_s[:, pl.ds(r0, L)] * ot_ref[0, c].astype(F32)
        for hd in range(HB):
            sl = slice(hd * D, (hd + 1) * D)
            hh = hm[sl]
            ms = jnp.mean(hh * hh, axis=0, keepdims=True)
            out_ref[0, c, sl, :] = (hh * lax.rsqrt(ms + EPS) * nw[sl]).astype(BF16)
        return carry

    lax.fori_loop(0, NC, fin, 0, unroll=2)


def _mlstm(mt, k_tm, gcm, nw_col, heads_per_step=2):
    B, S, _ = k_tm.shape
    NC = S // CHUNK
    D = HEAD_DIM
    HB = heads_per_step
    W = HB * D
    NB = M_HEADS // HB
    DA = D + BF16_ROWS
    cm_blk = (1, NC, W, LANES)
    return pl.pallas_call(
        _mlstm_kernel,
        grid=(B, NB),
        in_specs=[
            pl.BlockSpec(cm_blk, lambda b, h: (b, 0, h, 0)),
            pl.BlockSpec(cm_blk, lambda b, h: (b, 0, NB + h, 0)),
            pl.BlockSpec(cm_blk, lambda b, h: (b, 0, 2 * NB + h, 0)),
            pl.BlockSpec((1, S, W), lambda b, h: (b, 0, h)),
            pl.BlockSpec((1, HB, 8, S), lambda b, h: (b, h, 0, 0)),
            pl.BlockSpec((W, 1), lambda b, h: (h, 0)),
        ],
        out_specs=pl.BlockSpec(cm_blk, lambda b, h: (b, 0, h, 0)),
        out_shape=jax.ShapeDtypeStruct((B, NC, M_WIDTH, LANES), BF16),
        scratch_shapes=[
            pltpu.VMEM((HB, DA, S), BF16),
            pltpu.VMEM((HB, 8, S), F32),
            pltpu.VMEM((W, S), F32),
            pltpu.VMEM((HB, 2, DA, D), F32),
        ],
        compiler_params=_cparams("parallel", "parallel"),
        name="mlstm",
    )(mt, mt, mt, k_tm, gcm, nw_col)


def _filt_mlp_kernel(w1t_ref, b1_ref, w2t_ref, b2_ref, w3t_ref, b3_ref, fr_ref, w4f_ref, w4b_ref,
                     out_ref, h3_s, *, cb_rows):
    S = h3_s.shape[2]
    hi = lax.Precision.HIGHEST
    lane = lax.broadcasted_iota(jnp.int32, (1, S), 1)
    pos_f = lane.astype(F32)
    pos_b = (S - lane).astype(F32)

    def features(pos):
        bands = (FILTER_EMB - 1) // 2
        t = pos / (S - 1)
        ang = (2.0 * math.pi) * pos / S
        fidx = lax.broadcasted_iota(jnp.int32, (bands, 1), 0).astype(F32)
        f = 1e-4 + fidx * ((bands - 1 - 1e-4) / (bands - 1))
        fa = f * ang
        w1t = w1t_ref[...]
        pre = (w1t[:, 0:1] * t
               + jnp.dot(w1t[:, 1:1 + bands], jnp.cos(fa), precision=hi, preferred_element_type=F32)
               - jnp.dot(w1t[:, 1 + bands:], jnp.sin(fa), precision=hi, preferred_element_type=F32))
        fr = fr_ref[...]
        h = jnp.sin(fr[:, 0:1] * (pre + b1_ref[...]))
        h = jnp.sin(fr[:, 1:2] * (jnp.dot(w2t_ref[...], h, precision=hi, preferred_element_type=F32)
                                  + b2_ref[...]))
        return jnp.sin(fr[:, 2:3] * (jnp.dot(w3t_ref[...], h, precision=hi, preferred_element_type=F32)
                                     + b3_ref[...]))

    @pl.when(pl.program_id(0) == 0)
    def _():
        h3 = features(pos_f)
        h3_s[0] = h3
        n_t = S // LANES
        anti = (lax.broadcasted_iota(jnp.int32, (LANES, LANES), 0)
                + lax.broadcasted_iota(jnp.int32, (LANES, LANES), 1) == LANES - 1).astype(F32)
        rev = jnp.concatenate(
            [jnp.dot(h3[:, (n_t - 1 - u) * LANES:(n_t - u) * LANES], anti, precision=hi,
                     preferred_element_type=F32) for u in range(n_t)], axis=1)
        h3_s[1] = pltpu.roll(rev, 1, 1)

    r = pl.program_id(0) * cb_rows + lax.broadcasted_iota(jnp.int32, (cb_rows, 1), 0)
    ch = (r % H_WIDTH).astype(F32)
    max_decay = math.log(DECAY_TARGET) / FAST_DECAY_PCT
    min_decay = math.log(DECAY_TARGET) / SLOW_DECAY_PCT
    delta = jnp.abs(min_decay + ch * ((max_decay - min_decay) / (H_WIDTH - 1)))
    n_t = S // LANES
    for half, (w_ref, pos) in enumerate(((w4f_ref, pos_f), (w4b_ref, pos_b))):
        filt = jnp.dot(w_ref[...].astype(BF16), h3_s[half].astype(BF16),
                       preferred_element_type=F32)
        filt = filt * jnp.exp(-(pos / (S - 1)) * delta)
        if half == 1:
            filt = jnp.where(lane == 0, 0.0, filt)
        for u in range(n_t):
            out_ref[half * n_t + u] = filt[:, u * LANES:(u + 1) * LANES]


def _filt_mlp(w1t, b1, w2t, b2, w3t, b3, fr, w4t, S, cb_rows=256):
    R = w4t.shape[0] // 2
    Hd = FILTER_HIDDEN
    const = lambda i: (0, 0)
    nblk = R // cb_rows
    return pl.pallas_call(
        functools.partial(_filt_mlp_kernel, cb_rows=cb_rows),
        grid=(nblk,),
        in_specs=[
            pl.BlockSpec((Hd, FILTER_EMB), const),
            pl.BlockSpec((Hd, 1), const),
            pl.BlockSpec((Hd, Hd), const),
            pl.BlockSpec((Hd, 1), const),
            pl.BlockSpec((Hd, Hd), const),
            pl.BlockSpec((Hd, 1), const),
            pl.BlockSpec((Hd, 3), const),
            pl.BlockSpec((cb_rows, Hd), lambda i: (i, 0)),
            pl.BlockSpec((cb_rows, Hd), lambda i: (nblk + i, 0)),
        ],
        out_specs=pl.BlockSpec((2 * S // LANES, cb_rows, LANES), lambda i: (0, i, 0)),
        out_shape=jax.ShapeDtypeStruct((2 * S // LANES, R, LANES), F32),
        scratch_shapes=[pltpu.VMEM((2, Hd, S), F32)],
        compiler_params=_cparams("arbitrary"),
        name="filt_mlp",
    )(w1t, b1, w2t, b2, w3t, b3, fr, w4t, w4t)


@functools.lru_cache(maxsize=None)
def _dft_constants(S):
    N = 2 * S
    N2 = LANES
    N1 = N // N2
    h = N1 // 2
    k1 = np.arange(N1)
    k2 = np.arange(N2)
    a1 = -2.0 * np.pi * np.outer(k1, k1) / N1
    f1r, f1i = np.cos(a1), np.sin(a1)
    at = -2.0 * np.pi * np.outer(k1, k2) / N
    twr, twi = np.cos(at), np.sin(at)
    a2 = -2.0 * np.pi * np.outer(k2, k2) / N2
    f2r, f2i = np.cos(a2), np.sin(a2)
    s1c = np.block([[f1r[:, :h], -f1i[:, :h]], [f1i[:, :h], f1r[:, :h]]])
    s1r = np.concatenate([f1r, f1i], axis=0)
    s2 = np.block([[f2r, f2i], [-f2i, f2r]])
    s2i = np.block([[f2r, -f2i], [f2i, f2r]])
    s1i = np.block([[f1r[:h, :], f1i[:h, :]], [-f1i[:h, :], f1r[:h, :]]])
    cast = lambda a: np.asarray(a, np.float32)
    return dict(s1c=cast(s1c), s1r=cast(s1r), s2=cast(s2), s2i=cast(s2i), s1i=cast(s1i),
                twr=cast(twr), twi=cast(twi), N1=N1, h=h)


def _bf(a):
    return jnp.asarray(a, F32).astype(BF16)


def _filt_fft_kernel(bias_ref, kern_ref, s1r_ref, s2_ref, twr_ref, twi_ref, out_ref,
                     src_s, slab_s, *, unroll):
    N1 = twr_ref.shape[0]
    n_ch = kern_ref.shape[1]
    pitch = src_s.shape[0] // N1
    scale = 1.0 / (N1 * LANES)
    twr = twr_ref[...]
    twi = twi_ref[...]
    s1r = s1r_ref[...]

    def load_in(t1, carry):
        src_s[pl.ds(pl.multiple_of(t1 * pitch, 8), n_ch), :] = kern_ref[t1]
        return carry
    lax.fori_loop(0, N1, load_in, 0, unroll=4)

    def per_group(j, carry):
        for u in range(unroll):
            c = j * unroll + u
            z = src_s[pl.ds(c, N1, stride=pitch), :].astype(BF16)
            a = jnp.dot(s1r, z, preferred_element_type=F32)
            ar, ai = a[0:N1], a[N1:2 * N1]
            r0 = pl.multiple_of(c * N1, N1)
            slab_s[pl.ds(r0, N1), 0:LANES] = (ar * twr - ai * twi).astype(BF16)
            slab_s[pl.ds(r0, N1), LANES:2 * LANES] = (ar * twi + ai * twr).astype(BF16)
        return carry

    lax.fori_loop(0, n_ch // unroll, per_group, 0)
    x = jnp.dot(slab_s[...], s2_ref[...], preferred_element_type=F32)
    bias = bias_ref[...]
    out_ref[:, :, 0:LANES] = ((x[:, 0:LANES].reshape(n_ch, N1, LANES) + bias) * scale).astype(BF16)
    out_ref[:, :, LANES:2 * LANES] = (x[:, LANES:2 * LANES].reshape(n_ch, N1, LANES)
                                      * scale).astype(BF16)


def _filt_fft(bias3, kern, consts, n_ch=32, unroll=8):
    N1, R, _ = kern.shape
    const = lambda i: (0, 0)
    pitch = n_ch + 8
    return pl.pallas_call(
        functools.partial(_filt_fft_kernel, unroll=unroll),
        grid=(R // n_ch,),
        in_specs=[
            pl.BlockSpec((n_ch, 1, 1), lambda i: (i, 0, 0)),
            pl.BlockSpec((N1, n_ch, LANES), lambda i: (0, i, 0)),
            pl.BlockSpec((2 * N1, N1), const),
            pl.BlockSpec((2 * LANES, 2 * LANES), const),
            pl.BlockSpec((N1, LANES), const),
            pl.BlockSpec((N1, LANES), const),
        ],
        out_specs=pl.BlockSpec((n_ch, N1, 2 * LANES), lambda i: (i, 0, 0)),
        out_shape=jax.ShapeDtypeStruct((R, N1, 2 * LANES), BF16),
        scratch_shapes=[pltpu.VMEM((N1 * pitch, LANES), F32),
                        pltpu.VMEM((n_ch * N1, 2 * LANES), BF16)],
        compiler_params=_cparams("parallel"),
        name="filt_fft",
    )(bias3, kern, _bf(consts["s1r"]), _bf(consts["s2"]),
      jnp.asarray(consts["twr"]), jnp.asarray(consts["twi"]))


def _hyena_kernel(nw_ref, v_ref, x1_ref, x2_ref, khat_ref, s1c_ref, s2_ref, s2i_ref, s1i_ref,
                  twr_ref, twi_ref, out_ref, src_s, slab_s, q_s, y_s, *, ch_chunk, unroll, mid_unroll):
    h = v_ref.shape[1]
    n_ch = v_ref.shape[2]
    N1 = 2 * h
    twr = twr_ref[...]
    twi = twi_ref[...]
    s1c = s1c_ref[...]
    s1i = s1i_ref[...]
    n_chunks = n_ch // ch_chunk
    rows = ch_chunk * N1
    pitch = src_s.shape[0] // (2 * h)

    def tile_rows(i):
        return pl.ds(pl.multiple_of(i * pitch, 8), n_ch)

    def chan_rows(c, bb):
        return pl.ds(bb * h * pitch + c, h, stride=pitch)

    def long_conv(order):
        def fwd1(j, carry):
            for u in range(unroll):
                c = j * unroll + u
                z = jnp.concatenate([src_s[chan_rows(c, 0), :], src_s[chan_rows(c, 1), :]], axis=0)
                a = jnp.dot(s1c, z.astype(BF16), preferred_element_type=F32)
                ar, ai = a[0:N1], a[N1:2 * N1]
                r0 = pl.multiple_of(c * N1, N1)
                slab_s[pl.ds(r0, N1), 0:LANES] = (ar * twr - ai * twi).astype(BF16)
                slab_s[pl.ds(r0, N1), LANES:2 * LANES] = (ar * twi + ai * twr).astype(BF16)
            return carry
        lax.fori_loop(0, n_ch // unroll, fwd1, 0)

        def mid(j, carry):
            r0 = pl.multiple_of(j * rows, rows)
            c0 = pl.multiple_of(j * ch_chunk, ch_chunk)
            x = jnp.dot(slab_s[pl.ds(r0, rows), :], s2_ref[...], preferred_element_type=F32)
            xr = x[:, 0:LANES].reshape(ch_chunk, N1, LANES)
            xi = x[:, LANES:2 * LANES].reshape(ch_chunk, N1, LANES)
            kr = khat_ref[order, pl.ds(c0, ch_chunk), :, 0:LANES].astype(F32)
            ki = khat_ref[order, pl.ds(c0, ch_chunk), :, LANES:2 * LANES].astype(F32)
            yr = (xr * kr - xi * ki).reshape(rows, LANES).astype(BF16)
            yi = (xr * ki + xi * kr).reshape(rows, LANES).astype(BF16)
            y = jnp.concatenate([yr, yi], axis=1)
            p = jnp.dot(y, s2i_ref[...], preferred_element_type=F32)
            pr = p[:, 0:LANES].reshape(ch_chunk, N1, LANES)
            pi = p[:, LANES:2 * LANES].reshape(ch_chunk, N1, LANES)
            q_s[pl.ds(c0, ch_chunk), 0:N1, :] = (pr * twr + pi * twi).astype(BF16)
            q_s[pl.ds(c0, ch_chunk), N1:2 * N1, :] = (pi * twr - pr * twi).astype(BF16)
            return carry
        lax.fori_loop(0, n_chunks, mid, 0, unroll=mid_unroll)

        def inv1(j, carry):
            for u in range(unroll):
                c = j * unroll + u
                y = jnp.dot(s1i, q_s[c], preferred_element_type=F32)
                y_s[chan_rows(c, 0), :] = y[0:h]
                y_s[chan_rows(c, 1), :] = y[h:2 * h]
            return carry
        lax.fori_loop(0, n_ch // unroll, inv1, 0)

    def load_in(i, carry):
        src_s[tile_rows(i), :] = v_ref[i // h, i % h].astype(F32)
        return carry
    lax.fori_loop(0, 2 * h, load_in, 0, unroll=4)
    long_conv(0)

    def gate1(i, carry):
        src_s[tile_rows(i), :] = y_s[tile_rows(i), :] * x1_ref[i // h, i % h].astype(F32)
        return carry
    lax.fori_loop(0, 2 * h, gate1, 0, unroll=4)
    long_conv(1)

    nw = jnp.broadcast_to(nw_ref[...], (n_ch, LANES))

    def gate2_norm(i, carry):
        z = y_s[tile_rows(i), :] * x2_ref[i // h, i % h].astype(F32)
        ms = jnp.mean(z * z, axis=0, keepdims=True)
        out_ref[i // h, i % h] = (z * lax.rsqrt(ms + EPS) * nw).astype(BF16)
        return carry
    lax.fori_loop(0, 2 * h, gate2_norm, 0, unroll=4)


def _hyena(nw_col, hy4, khat, consts, ch_chunk=8, unroll=8, mid_unroll=4):
    B, h, C3, _ = hy4.shape
    C = C3 // 3
    n_ch = GROUP_W
    G = C // n_ch
    N1 = consts["N1"]
    const = lambda g, p: (0, 0)
    blk = (2, h, n_ch, LANES)
    pitch = n_ch + 8
    return pl.pallas_call(
        functools.partial(_hyena_kernel, ch_chunk=ch_chunk, unroll=unroll, mid_unroll=mid_unroll),
        grid=(G, B // 2),
        in_specs=[
            pl.BlockSpec((n_ch, 1), lambda g, p: (g, 0)),
            pl.BlockSpec(blk, lambda g, p: (p, 0, g, 0)),
            pl.BlockSpec(blk, lambda g, p: (p, 0, G + g, 0)),
            pl.BlockSpec(blk, lambda g, p: (p, 0, 2 * G + g, 0)),
            pl.BlockSpec((2, n_ch, N1, 2 * LANES), lambda g, p: (0, g, 0, 0)),
            pl.BlockSpec((2 * N1, 2 * h), const),
            pl.BlockSpec((2 * LANES, 2 * LANES), const),
            pl.BlockSpec((2 * LANES, 2 * LANES), const),
            pl.BlockSpec((2 * h, 2 * N1), const),
            pl.BlockSpec((N1, LANES), const),
            pl.BlockSpec((N1, LANES), const),
        ],
        out_specs=pl.BlockSpec(blk, lambda g, p: (p, 0, g, 0)),
        out_shape=jax.ShapeDtypeStruct((B, h, C, LANES), BF16),
        scratch_shapes=[
            pltpu.VMEM((2 * h * pitch, LANES), F32),
            pltpu.VMEM((n_ch * N1, 2 * LANES), BF16),
            pltpu.VMEM((n_ch, 2 * N1, LANES), BF16),
            pltpu.VMEM((2 * h * pitch, LANES), F32),
        ],
        compiler_params=_cparams("parallel", "arbitrary"),
        name="hyena",
    )(nw_col, hy4, hy4, hy4, khat,
      _bf(consts["s1c"]), _bf(consts["s2"]), _bf(consts["s2i"]), _bf(consts["s1i"]),
      jnp.asarray(consts["twr"]), jnp.asarray(consts["twi"]))


def _outmlp_kernel(x_ref, ym_ref, yh_ref, wo_ref, w1_ref, w2_ref, n_post_ref, n_pre_ref,
                   n_post2_ref, out_ref):
    n_t = ym_ref.shape[1]
    tiles = []
    for j in range(n_t):
        yt = jnp.concatenate([ym_ref[0, j], yh_ref[0, j]], axis=0)
        tiles.append(yt.astype(F32).T.astype(BF16))
    y = tiles[0] if n_t == 1 else jnp.concatenate(tiles, axis=0)
    mix = jnp.dot(y, wo_ref[...], preferred_element_type=F32)
    x1 = x_ref[0] + _rms_rows(mix, n_post_ref[...])
    hm = _rms_rows(x1, n_pre_ref[...]).astype(BF16)
    mid = jnp.maximum(jnp.dot(hm, w1_ref[...], preferred_element_type=F32), 0.0)
    mid = (mid * mid).astype(BF16)
    ff = jnp.dot(mid, w2_ref[...], preferred_element_type=F32)
    out_ref[0] = x1 + _rms_rows(ff, n_post2_ref[...])


def _outmlp(x, ym, yh, wo, w1, w2, n_post, n_pre, n_post2, tm_rows):
    B, S, D = x.shape
    n_t = tm_rows // LANES
    const = lambda b, i: (0, 0)
    resident = functools.partial(pl.BlockSpec, index_map=const, pipeline_mode=pl.Buffered(1))
    return pl.pallas_call(
        _outmlp_kernel,
        grid=(B, S // tm_rows),
        in_specs=[
            pl.BlockSpec((1, tm_rows, D), lambda b, i: (b, i, 0)),
            pl.BlockSpec((1, n_t, M_WIDTH, LANES), lambda b, i: (b, i, 0, 0)),
            pl.BlockSpec((1, n_t, H_WIDTH, LANES), lambda b, i: (b, i, 0, 0)),
            resident((D, D)),
            resident((D, D_FF)),
            resident((D_FF, D)),
            pl.BlockSpec((1, D), const),
            pl.BlockSpec((1, D), const),
            pl.BlockSpec((1, D), const),
        ],
        out_specs=pl.BlockSpec((1, tm_rows, D), lambda b, i: (b, i, 0)),
        out_shape=jax.ShapeDtypeStruct((B, S, D), F32),
        compiler_params=_cparams("parallel", "parallel"),
        name="outmlp",
    )(x, ym, yh, wo, w1, w2, n_post, n_pre, n_post2)


def kernel(x, norm_mix_pre, norm_mix_post, norm_mlp_pre, norm_mlp_post, w_in, b_gates,
           conv_w, conv_b, mlstm_norm_w, hyena_norm_w, filt_w1, filt_b1, filt_w2, filt_b2,
           filt_w3, filt_b3, filt_w4, filt_freq, filt_bias, w_out, w_mlp_in, w_mlp_out):
    B, S, D = x.shape
    assert D == D_MODEL and B % 2 == 0 and S % CHUNK == 0
    H = M_HEADS
    row = lambda a: a.astype(F32).reshape(1, -1)
    col = lambda a: a.astype(F32).reshape(-1, 1)
    tm_rows = min(1024, S)

    n_conv = 2 * M_WIDTH + 3 * H_WIDTH
    o_gate = n_conv + 2 * M_WIDTH
    assert w_in.shape == (D, o_gate + N_GATE)
    mt, hy, k_tm, gcm = _proj(x, row(norm_mix_pre), w_in.astype(F32).T, col(b_gates),
                              conv_w.astype(F32), row(conv_b), tm_rows)

    g4 = gcm.reshape(B, 4, H, S).transpose(0, 2, 1, 3)
    g8 = jnp.concatenate([g4, jnp.zeros_like(g4)], axis=2)
    y_m = _mlstm(mt, k_tm, g8, col(mlstm_norm_w))

    consts = _dft_constants(S)
    kern = _filt_mlp(filt_w1.astype(F32).T, col(filt_b1), filt_w2.astype(F32).T, col(filt_b2),
                     filt_w3.astype(F32).T, col(filt_b3), filt_freq.astype(F32).T,
                     filt_w4.astype(F32).T, S)
    khat = _filt_fft(filt_bias.astype(F32).reshape(2 * H_WIDTH, 1, 1), kern, consts)
    khat = khat.reshape(2, H_WIDTH, consts["N1"], 2 * LANES)

    y_h = _hyena(col(hyena_norm_w), hy, khat, consts)

    return _outmlp(x, y_m, y_h, w_out.astype(BF16), w_mlp_in.astype(BF16), w_mlp_out.astype(BF16),
                   row(norm_mix_post), row(norm_mlp_pre), row(norm_mlp_post), min(512, S))
```

```python
import functools
import math

import numpy as np
import jax
import jax.numpy as jnp
from jax import lax
from jax.experimental import pallas as pl
from jax.experimental.pallas import tpu as pltpu

F32 = jnp.float32
BF16 = jnp.bfloat16

D_MODEL = 1024
M_WIDTH = 512
M_HEADS = 4
HEAD_DIM = 128
H_WIDTH = 512
H_GROUPS = 8
GROUP_W = H_WIDTH // H_GROUPS
CHUNK = 128
FILTER_EMB = 33
FILTER_HIDDEN = 64
DECAY_TARGET = 1e-2
FAST_DECAY_PCT = 0.3
SLOW_DECAY_PCT = 1.5
D_FF = 4 * D_MODEL
N_GATE = 16
EPS = 1e-6
LANES = 128
BF16_ROWS = 16
NEG_BIG = -1e30
VMEM_LIMIT = 56 * 1024 * 1024


def _cparams(*sem):
    return pltpu.CompilerParams(dimension_semantics=sem, vmem_limit_bytes=VMEM_LIMIT)


def _rms_rows(xf, w):
    ms = jnp.mean(xf * xf, axis=-1, keepdims=True)
    return xf * lax.rsqrt(ms + EPS) * w


def _sigmoid(x):
    return 1.0 / (1.0 + jnp.exp(-x))


def _log_sigmoid(x):
    return jnp.minimum(x, 0.0) - jnp.log(1.0 + jnp.exp(-jnp.abs(x)))


def _proj_kernel(x_ref, xp_ref, xn_ref, nw_ref, wt_ref, bgt_ref, cw_ref, cb_ref,
                 mt_ref, hy_ref, k_ref, gcm_ref, pc_s, *, col_blk):
    TM = x_ref.shape[1]
    HALO = xp_ref.shape[1]
    n_conv = cw_ref.shape[1]
    j = pl.program_id(1)
    nw = nw_ref[...]
    hn = _rms_rows(x_ref[0], nw).astype(BF16)
    hp = _rms_rows(xp_ref[0], nw).astype(BF16)
    hx = _rms_rows(xn_ref[0], nw).astype(BF16)
    keep_p = jnp.where(j > 0, 1.0, 0.0)
    keep_n = jnp.where(j < pl.num_programs(1) - 1, 1.0, 0.0)
    n_cols = n_conv + 2 * M_WIDTH
    nt_dims = (((1,), (1,)), ((), ()))
    gt = lax.dot_general(wt_ref[n_cols:n_cols + N_GATE, :].astype(BF16), hn, nt_dims,
                         preferred_element_type=F32) + bgt_ref[...]
    row = lax.broadcasted_iota(jnp.int32, gt.shape, 0)
    gcm_ref[0] = jnp.where((row % 8) >= 4, _log_sigmoid(gt), gt)

    k_scale = HEAD_DIM ** -0.5
    o_k = M_WIDTH
    o_h = 2 * M_WIDTH
    n_rb = TM // LANES
    for c0 in range(0, n_conv, col_blk):
        wblk = wt_ref[c0:c0 + col_blk, :].T.astype(BF16)
        main = jnp.dot(hn, wblk, preferred_element_type=F32)
        prev = jnp.dot(hp, wblk, preferred_element_type=F32) * keep_p
        nxt = jnp.dot(hx, wblk, preferred_element_type=F32) * keep_n
        for u in range(col_blk // LANES):
            ls = slice(u * LANES, (u + 1) * LANES)
            sl = c0 // LANES + u
            pc_s[sl, 0:HALO, :] = prev[:, ls]
            pc_s[sl, HALO:HALO + TM, :] = main[:, ls]
            pc_s[sl, HALO + TM:2 * HALO + TM, :] = nxt[:, ls]
        for u in range(col_blk // LANES):
            sl = c0 // LANES + u
            cc = c0 + u * LANES
            cs = slice(cc, cc + LANES)
            for rb in range(n_rb):
                r = HALO + rb * LANES
                cv = (pc_s[sl, pl.ds(r - 1, LANES, stride=1), :] * cw_ref[0:1, cs]
                      + pc_s[sl, r:r + LANES, :] * cw_ref[1:2, cs]
                      + pc_s[sl, pl.ds(r + 1, LANES, stride=1), :] * cw_ref[2:3, cs]
                      + cb_ref[:, cs])
                if cc < o_k:
                    mt_ref[0, rb, cs, :] = (cv * _sigmoid(cv)).T.astype(BF16)
                elif cc < o_h:
                    k_ref[0, rb * LANES:(rb + 1) * LANES, cc - o_k:cc - o_k + LANES] = (
                        cv * _sigmoid(cv) * k_scale).astype(BF16)
                else:
                    hy_ref[0, rb, cc - o_h:cc - o_h + LANES, :] = cv.T.astype(BF16)
    for c0 in range(0, 2 * M_WIDTH, col_blk):
        pv = jnp.dot(hn, wt_ref[n_conv + c0:n_conv + c0 + col_blk, :].T.astype(BF16),
                     preferred_element_type=F32)
        if c0 >= M_WIDTH:
            pv = _sigmoid(pv)
        for rb in range(n_rb):
            mt_ref[0, rb, M_WIDTH + c0:M_WIDTH + c0 + col_blk, :] = (
                pv[rb * LANES:(rb + 1) * LANES].T.astype(BF16))


def _proj(x, nw, w_in_t, bgt, cw, cb, tm_rows, col_blk=256):
    B, S, D = x.shape
    HALO = BF16_ROWS
    n_conv = cw.shape[1]
    n_t = tm_rows // LANES
    hb = tm_rows // HALO
    last = S // HALO - 1
    const = lambda b, j: (0, 0)
    resident = functools.partial(pl.BlockSpec, index_map=const, pipeline_mode=pl.Buffered(1))
    return pl.pallas_call(
        functools.partial(_proj_kernel, col_blk=col_blk),
        grid=(B, S // tm_rows),
        in_specs=[
            pl.BlockSpec((1, tm_rows, D), lambda b, j: (b, j, 0)),
            pl.BlockSpec((1, HALO, D), lambda b, j: (b, jnp.maximum(j * hb - 1, 0), 0)),
            pl.BlockSpec((1, HALO, D), lambda b, j: (b, jnp.minimum((j + 1) * hb, last), 0)),
            pl.BlockSpec((1, D), const),
            resident(w_in_t.shape),
            pl.BlockSpec((N_GATE, 1), const),
            pl.BlockSpec((3, n_conv), const),
            pl.BlockSpec((1, n_conv), const),
        ],
        out_specs=[
            pl.BlockSpec((1, n_t, 3 * M_WIDTH, LANES), lambda b, j: (b, j, 0, 0)),
            pl.BlockSpec((1, n_t, 3 * H_WIDTH, LANES), lambda b, j: (b, j, 0, 0)),
            pl.BlockSpec((1, tm_rows, M_WIDTH), lambda b, j: (b, j, 0)),
            pl.BlockSpec((1, N_GATE, tm_rows), lambda b, j: (b, 0, j)),
        ],
        out_shape=[
            jax.ShapeDtypeStruct((B, S // LANES, 3 * M_WIDTH, LANES), BF16),
            jax.ShapeDtypeStruct((B, S // LANES, 3 * H_WIDTH, LANES), BF16),
            jax.ShapeDtypeStruct((B, S, M_WIDTH), BF16),
            jax.ShapeDtypeStruct((B, N_GATE, S), F32),
        ],
        scratch_shapes=[pltpu.VMEM((n_conv // LANES, tm_rows + 2 * HALO, LANES), F32)],
        compiler_params=_cparams("parallel", "parallel"),
        name="proj",
    )(x, x, x, nw, w_in_t, bgt, cw, cb)


def _mlstm_kernel(qt_ref, vt_ref, k_ref, g_ref, out_ref, va_s, rows_s, h_s, c_s):
    S = k_ref.shape[1]
    HB = g_ref.shape[1]
    L = CHUNK
    NC = S // L
    D = HEAD_DIM
    DA = D + BF16_ROWS
    W = HB * D

    for hd in range(HB):
        g = g_ref[0, hd]
        lane = lax.broadcasted_iota(jnp.int32, g.shape, 1) % L
        pre = g
        suf = g
        d = 1
        while d < L:
            pre = pre + jnp.where(lane >= d, pltpu.roll(pre, d, 1), 0.0)
            suf = suf + jnp.where(lane < L - d, pltpu.roll(suf, S - d, 1), 0.0)
            d *= 2
        rows_s[hd, 0:1, :] = g[0:1, :]
        rows_s[hd, 1:2, :] = pre[1:2, :]
        rows_s[hd, 2:3, :] = g[2:3, :]
        rows_s[hd, 3:4, :] = suf[3:4, :]

    ones_row = (lax.broadcasted_iota(jnp.int32, (BF16_ROWS, L), 0) == 0).astype(BF16)

    def prep(c, carry):
        r0 = pl.multiple_of(c * L, L)
        h_s[:, pl.ds(r0, L)] = jnp.zeros((W, L), F32)
        for hd in range(HB):
            va_s[hd, 0:D, pl.ds(r0, L)] = vt_ref[0, c, hd * D:(hd + 1) * D, :]
            va_s[hd, D:DA, pl.ds(r0, L)] = ones_row
        return carry

    lax.fori_loop(0, NC, prep, 0)
    c_s[...] = jnp.zeros_like(c_s)

    si = lax.broadcasted_iota(jnp.int32, (L, L), 0)
    ti = lax.broadcasted_iota(jnp.int32, (L, L), 1)
    eye = si == ti
    dirs = ((0, 1, si <= ti, L - 1), (2, 3, si >= ti, 0))

    def chain(hd, dr, c, m):
        li_row, b_row, mask, g_lane = dirs[dr]
        r0 = pl.multiple_of(c * L, L)
        r_li = rows_s[hd, li_row:li_row + 1, pl.ds(r0, L)]
        r_b = rows_s[hd, b_row:b_row + 1, pl.ds(r0, L)]
        g_tot = r_b[:, g_lane:g_lane + 1]
        r_a = g_tot - r_b + r_li
        m_new = jnp.maximum(g_tot + m, jnp.max(r_a, axis=1, keepdims=True))
        decay = jnp.exp(g_tot + m - m_new)
        r_w = jnp.exp(r_a - m_new)
        a_col = jnp.sum(jnp.where(eye, r_li - r_b, 0.0), axis=1, keepdims=True)
        dmat = jnp.where(mask, a_col + r_b, NEG_BIG)
        inter_log = r_b + m
        m_t = jnp.maximum(inter_log, jnp.max(dmat, axis=0, keepdims=True))
        wts = jnp.exp(dmat - m_t)
        kc = k_ref[0, pl.ds(r0, L), hd * D:(hd + 1) * D]
        qtc = qt_ref[0, c, hd * D:(hd + 1) * D, :]
        vac = va_s[hd, :, pl.ds(r0, L)]
        cmat = c_s[hd, dr]
        both = jnp.dot(jnp.concatenate([kc, cmat.astype(BF16)], axis=0), qtc,
                       preferred_element_type=F32)
        s_w = (both[0:L] * wts).astype(BF16)
        tot = (jnp.dot(vac, s_w, preferred_element_type=F32)
               + jnp.exp(inter_log - m_t) * both[L:L + DA])
        den = jnp.maximum(jnp.abs(tot[D:D + 1]), jnp.exp(-m_t))
        h = tot[0:D] * (1.0 / den)
        h_s[hd * D:(hd + 1) * D, pl.ds(r0, L)] = h_s[hd * D:(hd + 1) * D, pl.ds(r0, L)] + h
        u = (vac.astype(F32) * r_w).astype(BF16)
        c_s[hd, dr] = decay * cmat + jnp.dot(u, kc, preferred_element_type=F32)
        return m_new

    def step(i, ms):
        out = []
        for hd in range(HB):
            out.append(chain(hd, 0, i, ms[2 * hd]))
            out.append(chain(hd, 1, NC - 1 - i, ms[2 * hd + 1]))
        return tuple(out)

    lax.fori_loop(0, NC, step, tuple(jnp.zeros((1, 1), F32) for _ in range(2 * HB)))

    def fin(c, carry):
        out_ref[0, c] = h_s[:, pl.ds(pl.multiple_of(c * L, L), L)].astype(BF16)
        return carry

    lax.fori_loop(0, NC, fin, 0, unroll=4)


def _mlstm(mt, k_tm, gcm, heads_per_step=4):
    B, S, _ = k_tm.shape
    NC = S // CHUNK
    D = HEAD_DIM
    HB = heads_per_step
    W = HB * D
    NB = M_HEADS // HB
    DA = D + BF16_ROWS
    cm_blk = (1, NC, W, LANES)
    return pl.pallas_call(
        _mlstm_kernel,
        grid=(B, NB),
        in_specs=[
            pl.BlockSpec(cm_blk, lambda b, h: (b, 0, h, 0)),
            pl.BlockSpec(cm_blk, lambda b, h: (b, 0, NB + h, 0)),
            pl.BlockSpec((1, S, W), lambda b, h: (b, 0, h)),
            pl.BlockSpec((1, HB, 8, S), lambda b, h: (b, h, 0, 0)),
        ],
        out_specs=pl.BlockSpec(cm_blk, lambda b, h: (b, 0, h, 0)),
        out_shape=jax.ShapeDtypeStruct((B, NC, M_WIDTH, LANES), BF16),
        scratch_shapes=[
            pltpu.VMEM((HB, DA, S), BF16),
            pltpu.VMEM((HB, 8, S), F32),
            pltpu.VMEM((W, S), F32),
            pltpu.VMEM((HB, 2, DA, D), F32),
        ],
        compiler_params=_cparams("parallel", "parallel"),
        name="mlstm",
    )(mt, mt, k_tm, gcm)


def _filt_mlp_kernel(w1t_ref, b1_ref, w2t_ref, b2_ref, w3t_ref, b3_ref, fr_ref, w4f_ref, w4b_ref,
                     out_ref, h3_s, *, cb_rows):
    S = h3_s.shape[2]
    hi = lax.Precision.HIGHEST
    lane = lax.broadcasted_iota(jnp.int32, (1, S), 1)
    pos_f = lane.astype(F32)
    pos_b = (S - lane).astype(F32)

    def features(pos):
        bands = (FILTER_EMB - 1) // 2
        t = pos / (S - 1)
        ang = (2.0 * math.pi) * pos / S
        fidx = lax.broadcasted_iota(jnp.int32, (bands, 1), 0).astype(F32)
        f = 1e-4 + fidx * ((bands - 1 - 1e-4) / (bands - 1))
        fa = f * ang
        w1t = w1t_ref[...]
        pre = (w1t[:, 0:1] * t
               + jnp.dot(w1t[:, 1:1 + bands], jnp.cos(fa), precision=hi, preferred_element_type=F32)
               - jnp.dot(w1t[:, 1 + bands:], jnp.sin(fa), precision=hi, preferred_element_type=F32))
        fr = fr_ref[...]
        h = jnp.sin(fr[:, 0:1] * (pre + b1_ref[...]))
        h = jnp.sin(fr[:, 1:2] * (jnp.dot(w2t_ref[...], h, precision=hi, preferred_element_type=F32)
                                  + b2_ref[...]))
        return jnp.sin(fr[:, 2:3] * (jnp.dot(w3t_ref[...], h, precision=hi, preferred_element_type=F32)
                                     + b3_ref[...]))

    @pl.when(pl.program_id(0) == 0)
    def _():
        h3 = features(pos_f)
        h3_s[0] = h3
        n_t = S // LANES
        anti = (lax.broadcasted_iota(jnp.int32, (LANES, LANES), 0)
                + lax.broadcasted_iota(jnp.int32, (LANES, LANES), 1) == LANES - 1).astype(F32)
        rev = jnp.concatenate(
            [jnp.dot(h3[:, (n_t - 1 - u) * LANES:(n_t - u) * LANES], anti, precision=hi,
                     preferred_element_type=F32) for u in range(n_t)], axis=1)
        h3_s[1] = pltpu.roll(rev, 1, 1)

    r = pl.program_id(0) * cb_rows + lax.broadcasted_iota(jnp.int32, (cb_rows, 1), 0)
    ch = (r % H_WIDTH).astype(F32)
    max_decay = math.log(DECAY_TARGET) / FAST_DECAY_PCT
    min_decay = math.log(DECAY_TARGET) / SLOW_DECAY_PCT
    delta = jnp.abs(min_decay + ch * ((max_decay - min_decay) / (H_WIDTH - 1)))
    n_t = S // LANES
    for half, (w_ref, pos) in enumerate(((w4f_ref, pos_f), (w4b_ref, pos_b))):
        filt = jnp.dot(w_ref[...].astype(BF16), h3_s[half].astype(BF16),
                       preferred_element_type=F32)
        filt = filt * jnp.exp(-(pos / (S - 1)) * delta)
        if half == 1:
            filt = jnp.where(lane == 0, 0.0, filt)
        for u in range(n_t):
            out_ref[half * n_t + u] = filt[:, u * LANES:(u + 1) * LANES]


def _filt_mlp(w1t, b1, w2t, b2, w3t, b3, fr, w4t, S, cb_rows=256):
    R = w4t.shape[0] // 2
    Hd = FILTER_HIDDEN
    const = lambda i: (0, 0)
    nblk = R // cb_rows
    return pl.pallas_call(
        functools.partial(_filt_mlp_kernel, cb_rows=cb_rows),
        grid=(nblk,),
        in_specs=[
            pl.BlockSpec((Hd, FILTER_EMB), const),
            pl.BlockSpec((Hd, 1), const),
            pl.BlockSpec((Hd, Hd), const),
            pl.BlockSpec((Hd, 1), const),
            pl.BlockSpec((Hd, Hd), const),
            pl.BlockSpec((Hd, 1), const),
            pl.BlockSpec((Hd, 3), const),
            pl.BlockSpec((cb_rows, Hd), lambda i: (i, 0)),
            pl.BlockSpec((cb_rows, Hd), lambda i: (nblk + i, 0)),
        ],
        out_specs=pl.BlockSpec((2 * S // LANES, cb_rows, LANES), lambda i: (0, i, 0)),
        out_shape=jax.ShapeDtypeStruct((2 * S // LANES, R, LANES), F32),
        scratch_shapes=[pltpu.VMEM((2, Hd, S), F32)],
        compiler_params=_cparams("arbitrary"),
        name="filt_mlp",
    )(w1t, b1, w2t, b2, w3t, b3, fr, w4t, w4t)


@functools.lru_cache(maxsize=None)
def _dft_constants(S):
    N = 2 * S
    N2 = LANES
    N1 = N // N2
    h = N1 // 2
    k1 = np.arange(N1)
    k2 = np.arange(N2)
    a1 = -2.0 * np.pi * np.outer(k1, k1) / N1
    f1r, f1i = np.cos(a1), np.sin(a1)
    at = -2.0 * np.pi * np.outer(k1, k2) / N
    twr, twi = np.cos(at), np.sin(at)
    a2 = -2.0 * np.pi * np.outer(k2, k2) / N2
    f2r, f2i = np.cos(a2), np.sin(a2)
    s1c = np.block([[f1r[:, :h], -f1i[:, :h]], [f1i[:, :h], f1r[:, :h]]])
    s1r = np.concatenate([f1r, f1i], axis=0)
    s2 = np.block([[f2r, f2i], [-f2i, f2r]])
    s2i = np.block([[f2r, -f2i], [f2i, f2r]])
    s1i = np.block([[f1r[:h, :], f1i[:h, :]], [-f1i[:h, :], f1r[:h, :]]])
    cast = lambda a: np.asarray(a, np.float32)
    return dict(s1c=cast(s1c), s1r=cast(s1r), s2=cast(s2), s2i=cast(s2i), s1i=cast(s1i),
                twr=cast(twr), twi=cast(twi), N1=N1, h=h)


def _bf(a):
    return jnp.asarray(a, F32).astype(BF16)


def _filt_fft_kernel(bias_ref, kern_ref, s1r_ref, s2_ref, twr_ref, twi_ref, out_ref,
                     src_s, slab_s, *, unroll):
    N1 = twr_ref.shape[0]
    n_ch = kern_ref.shape[1]
    pitch = src_s.shape[0] // N1
    scale = 1.0 / (N1 * LANES)
    twr = twr_ref[...]
    twi = twi_ref[...]
    s1r = s1r_ref[...]

    def load_in(t1, carry):
        src_s[pl.ds(pl.multiple_of(t1 * pitch, 8), n_ch), :] = kern_ref[t1]
        return carry
    lax.fori_loop(0, N1, load_in, 0, unroll=4)

    def per_group(j, carry):
        for u in range(unroll):
            c = j * unroll + u
            z = src_s[pl.ds(c, N1, stride=pitch), :].astype(BF16)
            a = jnp.dot(s1r, z, preferred_element_type=F32)
            ar, ai = a[0:N1], a[N1:2 * N1]
            r0 = pl.multiple_of(c * N1, N1)
            slab_s[pl.ds(r0, N1), 0:LANES] = (ar * twr - ai * twi).astype(BF16)
            slab_s[pl.ds(r0, N1), LANES:2 * LANES] = (ar * twi + ai * twr).astype(BF16)
        return carry

    lax.fori_loop(0, n_ch // unroll, per_group, 0)
    x = jnp.dot(slab_s[...], s2_ref[...], preferred_element_type=F32)
    bias = bias_ref[...]
    out_ref[:, :, 0:LANES] = ((x[:, 0:LANES].reshape(n_ch, N1, LANES) + bias) * scale).astype(BF16)
    out_ref[:, :, LANES:2 * LANES] = (x[:, LANES:2 * LANES].reshape(n_ch, N1, LANES)
                                      * scale).astype(BF16)


def _filt_fft(bias3, kern, consts, n_ch=32, unroll=8):
    N1, R, _ = kern.shape
    const = lambda i: (0, 0)
    pitch = n_ch + 8
    return pl.pallas_call(
        functools.partial(_filt_fft_kernel, unroll=unroll),
        grid=(R // n_ch,),
        in_specs=[
            pl.BlockSpec((n_ch, 1, 1), lambda i: (i, 0, 0)),
            pl.BlockSpec((N1, n_ch, LANES), lambda i: (0, i, 0)),
            pl.BlockSpec((2 * N1, N1), const),
            pl.BlockSpec((2 * LANES, 2 * LANES), const),
            pl.BlockSpec((N1, LANES), const),
            pl.BlockSpec((N1, LANES), const),
        ],
        out_specs=pl.BlockSpec((n_ch, N1, 2 * LANES), lambda i: (i, 0, 0)),
        out_shape=jax.ShapeDtypeStruct((R, N1, 2 * LANES), BF16),
        scratch_shapes=[pltpu.VMEM((N1 * pitch, LANES), F32),
                        pltpu.VMEM((n_ch * N1, 2 * LANES), BF16)],
        compiler_params=_cparams("parallel"),
        name="filt_fft",
    )(bias3, kern, _bf(consts["s1r"]), _bf(consts["s2"]),
      jnp.asarray(consts["twr"]), jnp.asarray(consts["twi"]))


def _hyena_kernel(nw_ref, v_ref, x1_ref, x2_ref, khat_ref, s1c_ref, s2_ref, s2i_ref, s1i_ref,
                  twr_ref, twi_ref, out_ref, src_s, slab_s, yf_s, q_s, y_s, *, unroll):
    h = v_ref.shape[1]
    n_ch = v_ref.shape[2]
    N1 = 2 * h
    twr = twr_ref[...]
    twi = twi_ref[...]
    s1c = s1c_ref[...]
    s1i = s1i_ref[...]
    n_groups = n_ch // unroll
    rows = unroll * N1
    pitch = src_s.shape[0] // (2 * h)

    def tile_rows(i):
        return pl.ds(pl.multiple_of(i * pitch, 8), n_ch)

    def chan_rows(c, bb):
        return pl.ds(bb * h * pitch + c, h, stride=pitch)

    def long_conv(order):
        def stage_a(g):
            for u in range(unroll):
                c = g * unroll + u
                z = jnp.concatenate([src_s[chan_rows(c, 0), :], src_s[chan_rows(c, 1), :]], axis=0)
                a = jnp.dot(s1c, z.astype(BF16), preferred_element_type=F32)
                ar, ai = a[0:N1], a[N1:2 * N1]
                r0 = c * N1 if isinstance(c, int) else pl.multiple_of(c * N1, N1)
                slab_s[pl.ds(r0, N1), 0:LANES] = (ar * twr - ai * twi).astype(BF16)
                slab_s[pl.ds(r0, N1), LANES:2 * LANES] = (ar * twi + ai * twr).astype(BF16)

        def stage_m1(g):
            r0 = g * rows if isinstance(g, int) else pl.multiple_of(g * rows, rows)
            c0 = g * unroll if isinstance(g, int) else pl.multiple_of(g * unroll, unroll)
            x = jnp.dot(slab_s[pl.ds(r0, rows), :], s2_ref[...], preferred_element_type=F32)
            xr = x[:, 0:LANES].reshape(unroll, N1, LANES)
            xi = x[:, LANES:2 * LANES].reshape(unroll, N1, LANES)
            kr = khat_ref[order, pl.ds(c0, unroll), :, 0:LANES].astype(F32)
            ki = khat_ref[order, pl.ds(c0, unroll), :, LANES:2 * LANES].astype(F32)
            yf_s[pl.ds(r0, rows), 0:LANES] = (xr * kr - xi * ki).reshape(rows, LANES).astype(BF16)
            yf_s[pl.ds(r0, rows), LANES:2 * LANES] = (xr * ki + xi * kr).reshape(rows, LANES).astype(BF16)

        def stage_m2(g):
            r0 = g * rows if isinstance(g, int) else pl.multiple_of(g * rows, rows)
            c0 = g * unroll if isinstance(g, int) else pl.multiple_of(g * unroll, unroll)
            p = jnp.dot(yf_s[pl.ds(r0, rows), :], s2i_ref[...], preferred_element_type=F32)
            pr = p[:, 0:LANES].reshape(unroll, N1, LANES)
            pi = p[:, LANES:2 * LANES].reshape(unroll, N1, LANES)
            q_s[pl.ds(c0, unroll), 0:N1, :] = (pr * twr + pi * twi).astype(BF16)
            q_s[pl.ds(c0, unroll), N1:2 * N1, :] = (pi * twr - pr * twi).astype(BF16)

        def stage_i(g):
            for u in range(unroll):
                c = g * unroll + u
                y = jnp.dot(s1i, q_s[c], preferred_element_type=F32)
                y_s[chan_rows(c, 0), :] = y[0:h]
                y_s[chan_rows(c, 1), :] = y[h:2 * h]

        G = n_groups
        stage_a(0)
        stage_m1(0)
        stage_a(1)
        stage_m2(0)
        stage_m1(1)
        stage_a(2)

        def steady(i, carry):
            stage_i(i)
            stage_m2(i + 1)
            stage_m1(i + 2)
            stage_a(i + 3)
            return carry
        lax.fori_loop(0, G - 3, steady, 0)
        stage_i(G - 3)
        stage_m2(G - 2)
        stage_m1(G - 1)
        stage_i(G - 2)
        stage_m2(G - 1)
        stage_i(G - 1)

    def load_in(i, carry):
        src_s[tile_rows(i), :] = v_ref[i // h, i % h].astype(F32)
        return carry
    lax.fori_loop(0, 2 * h, load_in, 0, unroll=4)
    long_conv(0)

    def gate1(i, carry):
        src_s[tile_rows(i), :] = y_s[tile_rows(i), :] * x1_ref[i // h, i % h].astype(F32)
        return carry
    lax.fori_loop(0, 2 * h, gate1, 0, unroll=4)
    long_conv(1)

    nw = jnp.broadcast_to(nw_ref[...], (n_ch, LANES))

    def gate2_norm(i, carry):
        z = y_s[tile_rows(i), :] * x2_ref[i // h, i % h].astype(F32)
        ms = jnp.mean(z * z, axis=0, keepdims=True)
        out_ref[i // h, i % h] = (z * lax.rsqrt(ms + EPS) * nw).astype(BF16)
        return carry
    lax.fori_loop(0, 2 * h, gate2_norm, 0, unroll=4)


def _hyena(nw_col, hy4, khat, consts, unroll=8):
    B, h, C3, _ = hy4.shape
    C = C3 // 3
    n_ch = GROUP_W
    G = C // n_ch
    N1 = consts["N1"]
    const = lambda g, p: (0, 0)
    blk = (2, h, n_ch, LANES)
    pitch = n_ch + 8
    return pl.pallas_call(
        functools.partial(_hyena_kernel, unroll=unroll),
        grid=(G, B // 2),
        in_specs=[
            pl.BlockSpec((n_ch, 1), lambda g, p: (g, 0)),
            pl.BlockSpec(blk, lambda g, p: (p, 0, g, 0)),
            pl.BlockSpec(blk, lambda g, p: (p, 0, G + g, 0)),
            pl.BlockSpec(blk, lambda g, p: (p, 0, 2 * G + g, 0)),
            pl.BlockSpec((2, n_ch, N1, 2 * LANES), lambda g, p: (0, g, 0, 0)),
            pl.BlockSpec((2 * N1, 2 * h), const),
            pl.BlockSpec((2 * LANES, 2 * LANES), const),
            pl.BlockSpec((2 * LANES, 2 * LANES), const),
            pl.BlockSpec((2 * h, 2 * N1), const),
            pl.BlockSpec((N1, LANES), const),
            pl.BlockSpec((N1, LANES), const),
        ],
        out_specs=pl.BlockSpec(blk, lambda g, p: (p, 0, g, 0)),
        out_shape=jax.ShapeDtypeStruct((B, h, C, LANES), BF16),
        scratch_shapes=[
            pltpu.VMEM((2 * h * pitch, LANES), F32),
            pltpu.VMEM((n_ch * N1, 2 * LANES), BF16),
            pltpu.VMEM((n_ch * N1, 2 * LANES), BF16),
            pltpu.VMEM((n_ch, 2 * N1, LANES), BF16),
            pltpu.VMEM((2 * h * pitch, LANES), F32),
        ],
        compiler_params=_cparams("parallel", "arbitrary"),
        name="hyena",
    )(nw_col, hy4, hy4, hy4, khat,
      _bf(consts["s1c"]), _bf(consts["s2"]), _bf(consts["s2i"]), _bf(consts["s1i"]),
      jnp.asarray(consts["twr"]), jnp.asarray(consts["twi"]))


def _outmlp_kernel(x_ref, hm_ref, og_ref, yh_ref, nwm_ref, wo_ref, w1_ref, w2_ref, n_post_ref,
                   n_pre_ref, n_post2_ref, out_ref):
    n_t = hm_ref.shape[1]
    nwm = jnp.broadcast_to(nwm_ref[...], (M_WIDTH, LANES))
    tiles = []
    for j in range(n_t):
        hg = hm_ref[0, j].astype(F32) * og_ref[0, j].astype(F32)
        heads = []
        for hd in range(M_HEADS):
            hh = hg[hd * HEAD_DIM:(hd + 1) * HEAD_DIM]
            ms = jnp.mean(hh * hh, axis=0, keepdims=True)
            heads.append(hh * lax.rsqrt(ms + EPS))
        ym = jnp.concatenate(heads, axis=0) * nwm
        yt = jnp.concatenate([ym, yh_ref[0, j].astype(F32)], axis=0)
        tiles.append(yt.T.astype(BF16))
    y = tiles[0] if n_t == 1 else jnp.concatenate(tiles, axis=0)
    mix = jnp.dot(y, wo_ref[...], preferred_element_type=F32)
    x1 = x_ref[0] + _rms_rows(mix, n_post_ref[...])
    hm = _rms_rows(x1, n_pre_ref[...]).astype(BF16)
    mid = jnp.maximum(jnp.dot(hm, w1_ref[...], preferred_element_type=F32), 0.0)
    mid = (mid * mid).astype(BF16)
    ff = jnp.dot(mid, w2_ref[...], preferred_element_type=F32)
    out_ref[0] = x1 + _rms_rows(ff, n_post2_ref[...])


def _outmlp(x, hm, mt, yh, nwm_col, wo, w1, w2, n_post, n_pre, n_post2, tm_rows):
    B, S, D = x.shape
    n_t = tm_rows // LANES
    const = lambda b, i: (0, 0)
    resident = functools.partial(pl.BlockSpec, index_map=const, pipeline_mode=pl.Buffered(1))
    return pl.pallas_call(
        _outmlp_kernel,
        grid=(B, S // tm_rows),
        in_specs=[
            pl.BlockSpec((1, tm_rows, D), lambda b, i: (b, i, 0)),
            pl.BlockSpec((1, n_t, M_WIDTH, LANES), lambda b, i: (b, i, 0, 0)),
            pl.BlockSpec((1, n_t, M_WIDTH, LANES), lambda b, i: (b, i, 2, 0)),
            pl.BlockSpec((1, n_t, H_WIDTH, LANES), lambda b, i: (b, i, 0, 0)),
            pl.BlockSpec((M_WIDTH, 1), const),
            resident((D, D)),
            resident((D, D_FF)),
            resident((D_FF, D)),
            pl.BlockSpec((1, D), const),
            pl.BlockSpec((1, D), const),
            pl.BlockSpec((1, D), const),
        ],
        out_specs=pl.BlockSpec((1, tm_rows, D), lambda b, i: (b, i, 0)),
        out_shape=jax.ShapeDtypeStruct((B, S, D), F32),
        compiler_params=_cparams("parallel", "parallel"),
        name="outmlp",
    )(x, hm, mt, yh, nwm_col, wo, w1, w2, n_post, n_pre, n_post2)


def kernel(x, norm_mix_pre, norm_mix_post, norm_mlp_pre, norm_mlp_post, w_in, b_gates,
           conv_w, conv_b, mlstm_norm_w, hyena_norm_w, filt_w1, filt_b1, filt_w2, filt_b2,
           filt_w3, filt_b3, filt_w4, filt_freq, filt_bias, w_out, w_mlp_in, w_mlp_out):
    B, S, D = x.shape
    assert D == D_MODEL and B % 2 == 0 and S % CHUNK == 0
    H = M_HEADS
    row = lambda a: a.astype(F32).reshape(1, -1)
    col = lambda a: a.astype(F32).reshape(-1, 1)
    tm_rows = min(1024, S)

    n_conv = 2 * M_WIDTH + 3 * H_WIDTH
    o_gate = n_conv + 2 * M_WIDTH
    assert w_in.shape == (D, o_gate + N_GATE)
    mt, hy, k_tm, gcm = _proj(x, row(norm_mix_pre), w_in.astype(F32).T, col(b_gates),
                              conv_w.astype(F32), row(conv_b), tm_rows)

    g4 = gcm.reshape(B, 4, H, S).transpose(0, 2, 1, 3)
    g8 = jnp.concatenate([g4, jnp.zeros_like(g4)], axis=2)
    h_m = _mlstm(mt, k_tm, g8)

    consts = _dft_constants(S)
    kern = _filt_mlp(filt_w1.astype(F32).T, col(filt_b1), filt_w2.astype(F32).T, col(filt_b2),
                     filt_w3.astype(F32).T, col(filt_b3), filt_freq.astype(F32).T,
                     filt_w4.astype(F32).T, S)
    khat = _filt_fft(filt_bias.astype(F32).reshape(2 * H_WIDTH, 1, 1), kern, consts)
    khat = khat.reshape(2, H_WIDTH, consts["N1"], 2 * LANES)

    y_h = _hyena(col(hyena_norm_w), hy, khat, consts)

    return _outmlp(x, h_m, mt, y_h, col(mlstm_norm_w),
                   w_out.astype(BF16), w_mlp_in.astype(BF16), w_mlp_out.astype(BF16),
                   row(norm_mix_post), row(norm_mlp_pre), row(norm_mlp_post), min(512, S))
```

```python
import functools
import math

import numpy as np
import jax
import jax.numpy as jnp
from jax import lax
from jax.experimental import pallas as pl
from jax.experimental.pallas import tpu as pltpu

F32 = jnp.float32
BF16 = jnp.bfloat16

D_MODEL = 1024
M_WIDTH = 512
M_HEADS = 4
HEAD_DIM = 128
H_WIDTH = 512
H_GROUPS = 8
GROUP_W = H_WIDTH // H_GROUPS
CHUNK = 128
FILTER_EMB = 33
FILTER_HIDDEN = 64
DECAY_TARGET = 1e-2
FAST_DECAY_PCT = 0.3
SLOW_DECAY_PCT = 1.5
D_FF = 4 * D_MODEL
N_GATE = 16
EPS = 1e-6
LANES = 128
BF16_ROWS = 16
NEG_BIG = -1e30
VMEM_LIMIT = 56 * 1024 * 1024


def _cparams(*sem):
    return pltpu.CompilerParams(dimension_semantics=sem, vmem_limit_bytes=VMEM_LIMIT)


def _rms_rows(xf, w):
    ms = jnp.mean(xf * xf, axis=-1, keepdims=True)
    return xf * lax.rsqrt(ms + EPS) * w


def _sigmoid(x):
    return 1.0 / (1.0 + jnp.exp(-x))


def _log_sigmoid(x):
    return jnp.minimum(x, 0.0) - jnp.log(1.0 + jnp.exp(-jnp.abs(x)))


def _proj_kernel(x_ref, xp_ref, xn_ref, nw_ref, wt_ref, bgt_ref, cw_ref, cb_ref, cwk_ref, cbk_ref,
                 cm_ref, k_ref, gcm_ref, hall_s, pc_s, *, col_blk):
    TM = x_ref.shape[1]
    HALO = xp_ref.shape[1]
    j = pl.program_id(1)
    nw = nw_ref[...]
    hn = _rms_rows(x_ref[0], nw).astype(BF16)
    hall_s[0:HALO, :] = _rms_rows(xp_ref[0], nw).astype(BF16)
    hall_s[HALO:HALO + TM, :] = hn
    hall_s[HALO + TM:2 * HALO + TM, :] = _rms_rows(xn_ref[0], nw).astype(BF16)
    keep_p = jnp.where(j > 0, 1.0, 0.0)
    keep_n = jnp.where(j < pl.num_programs(1) - 1, 1.0, 0.0)
    n_cm = cw_ref.shape[1]
    n_blk = n_cm // col_blk
    w_k0 = M_WIDTH
    w_gate = n_cm + M_WIDTH
    nt_dims = (((1,), (1,)), ((), ()))
    gt = lax.dot_general(wt_ref[w_gate:w_gate + N_GATE, :].astype(BF16), hn, nt_dims,
                         preferred_element_type=F32) + bgt_ref[...]
    row = lax.broadcasted_iota(jnp.int32, gt.shape, 0)
    gcm_ref[0] = jnp.where((row % 8) >= 4, _log_sigmoid(gt), gt)

    n_rb = TM // LANES
    n_sl = col_blk // LANES

    def project(w_row0, slot):
        wblk = wt_ref[pl.ds(w_row0, col_blk), :].T.astype(BF16)
        res = jnp.dot(hall_s[...], wblk, preferred_element_type=F32)
        for u in range(n_sl):
            ls = slice(u * LANES, (u + 1) * LANES)
            pc_s[slot, u, 0:HALO, :] = res[0:HALO, ls] * keep_p
            pc_s[slot, u, HALO:HALO + TM, :] = res[HALO:HALO + TM, ls]
            pc_s[slot, u, HALO + TM:2 * HALO + TM, :] = res[HALO + TM:2 * HALO + TM, ls] * keep_n

    def conv_tile(slot, u, rb, taps, bias):
        r = HALO + rb * LANES
        return (pc_s[slot, u, pl.ds(r - 1, LANES, stride=1), :] * taps[0:1]
                + pc_s[slot, u, r:r + LANES, :] * taps[1:2]
                + pc_s[slot, u, pl.ds(r + 1, LANES, stride=1), :] * taps[2:3] + bias)

    k_scale = HEAD_DIM ** -0.5
    for kb in range(M_WIDTH // col_blk):
        project(w_k0 + kb * col_blk, kb)
        for u in range(n_sl):
            cs = slice(kb * col_blk + u * LANES, kb * col_blk + (u + 1) * LANES)
            for rb in range(n_rb):
                cv = conv_tile(kb, u, rb, cwk_ref[:, cs], cbk_ref[:, cs])
                k_ref[0, rb * LANES:(rb + 1) * LANES, cs] = (cv * _sigmoid(cv) * k_scale).astype(BF16)

    def w_row_of(c):
        return jnp.where(c < M_WIDTH // col_blk, c, c + M_WIDTH // col_blk) * col_blk

    def emit(c, slot, act):
        for u in range(n_sl):
            col0 = c * col_blk + u * LANES
            if not isinstance(c, int):
                col0 = pl.multiple_of(col0, LANES)
            taps = cw_ref[:, pl.ds(col0, LANES)]
            bias = cb_ref[:, pl.ds(col0, LANES)]
            for rb in range(n_rb):
                cv = conv_tile(slot, u, rb, taps, bias)
                if act == "silu":
                    cv = cv * _sigmoid(cv)
                elif act == "sigmoid":
                    cv = _sigmoid(cv)
                cm_ref[0, rb, pl.ds(col0, LANES), :] = cv.T.astype(BF16)

    def run(lo, hi, act):
        def body(i, carry):
            c = lo + 2 * i
            emit(c, 0, act)
            project(pl.multiple_of(w_row_of(c + 1), col_blk), 1)
            emit(c + 1, 1, act)
            project(pl.multiple_of(w_row_of(c + 2), col_blk), 0)
            return carry
        lax.fori_loop(0, (hi - lo) // 2, body, 0)

    n_q = M_WIDTH // col_blk
    assert n_q % 2 == 0 and n_blk % 2 == 0
    project(0, 0)
    run(0, n_q, "silu")
    run(n_q, n_blk - n_q, "none")
    run(n_blk - n_q, n_blk - 2, "sigmoid")
    emit(n_blk - 2, 0, "sigmoid")
    project((n_blk - 1 + n_q) * col_blk, 1)
    emit(n_blk - 1, 1, "sigmoid")


def _proj(x, nw, w_in_t, bgt, cw_cm, cb_cm, cw_k, cb_k, tm_rows, col_blk=256):
    B, S, D = x.shape
    HALO = BF16_ROWS
    n_cm = cw_cm.shape[1]
    n_t = tm_rows // LANES
    hb = tm_rows // HALO
    last = S // HALO - 1
    const = lambda b, j: (0, 0)
    resident = functools.partial(pl.BlockSpec, index_map=const, pipeline_mode=pl.Buffered(1))
    return pl.pallas_call(
        functools.partial(_proj_kernel, col_blk=col_blk),
        grid=(B, S // tm_rows),
        in_specs=[
            pl.BlockSpec((1, tm_rows, D), lambda b, j: (b, j, 0)),
            pl.BlockSpec((1, HALO, D), lambda b, j: (b, jnp.maximum(j * hb - 1, 0), 0)),
            pl.BlockSpec((1, HALO, D), lambda b, j: (b, jnp.minimum((j + 1) * hb, last), 0)),
            pl.BlockSpec((1, D), const),
            resident(w_in_t.shape),
            pl.BlockSpec((N_GATE, 1), const),
            pl.BlockSpec((3, n_cm), const),
            pl.BlockSpec((1, n_cm), const),
            pl.BlockSpec((3, M_WIDTH), const),
            pl.BlockSpec((1, M_WIDTH), const),
        ],
        out_specs=[
            pl.BlockSpec((1, n_t, n_cm, LANES), lambda b, j: (b, j, 0, 0)),
            pl.BlockSpec((1, tm_rows, M_WIDTH), lambda b, j: (b, j, 0)),
            pl.BlockSpec((1, N_GATE, tm_rows), lambda b, j: (b, 0, j)),
        ],
        out_shape=[
            jax.ShapeDtypeStruct((B, S // LANES, n_cm, LANES), BF16),
            jax.ShapeDtypeStruct((B, S, M_WIDTH), BF16),
            jax.ShapeDtypeStruct((B, N_GATE, S), F32),
        ],
        scratch_shapes=[pltpu.VMEM((tm_rows + 2 * HALO, D), BF16),
                        pltpu.VMEM((2, col_blk // LANES, tm_rows + 2 * HALO, LANES), F32)],
        compiler_params=_cparams("parallel", "parallel"),
        name="proj",
    )(x, x, x, nw, w_in_t, bgt, cw_cm, cb_cm, cw_k, cb_k)


def _mlstm_kernel(qt_ref, vt_ref, k_ref, g_ref, out_ref, va_s, rows_s, h_s, c_s):
    S = k_ref.shape[1]
    HB = g_ref.shape[1]
    L = CHUNK
    NC = S // L
    D = HEAD_DIM
    DA = D + BF16_ROWS
    W = HB * D

    for hd in range(HB):
        g = g_ref[0, hd]
        lane = lax.broadcasted_iota(jnp.int32, g.shape, 1) % L
        pre = g
        suf = g
        d = 1
        while d < L:
            pre = pre + jnp.where(lane >= d, pltpu.roll(pre, d, 1), 0.0)
            suf = suf + jnp.where(lane < L - d, pltpu.roll(suf, S - d, 1), 0.0)
            d *= 2
        rows_s[hd, 0:1, :] = g[0:1, :]
        rows_s[hd, 1:2, :] = pre[1:2, :]
        rows_s[hd, 2:3, :] = g[2:3, :]
        rows_s[hd, 3:4, :] = suf[3:4, :]

    ones_row = (lax.broadcasted_iota(jnp.int32, (BF16_ROWS, L), 0) == 0).astype(BF16)

    def prep(c, carry):
        r0 = pl.multiple_of(c * L, L)
        h_s[:, pl.ds(r0, L)] = jnp.zeros((W, L), F32)
        for hd in range(HB):
            va_s[hd, 0:D, pl.ds(r0, L)] = vt_ref[0, c, hd * D:(hd + 1) * D, :]
            va_s[hd, D:DA, pl.ds(r0, L)] = ones_row
        return carry

    lax.fori_loop(0, NC, prep, 0)
    c_s[...] = jnp.zeros_like(c_s)

    si = lax.broadcasted_iota(jnp.int32, (L, L), 0)
    ti = lax.broadcasted_iota(jnp.int32, (L, L), 1)
    eye = si == ti
    dirs = ((0, 1, si <= ti, L - 1), (2, 3, si >= ti, 0))

    def chain(hd, dr, c, m):
        li_row, b_row, mask, g_lane = dirs[dr]
        r0 = pl.multiple_of(c * L, L)
        r_li = rows_s[hd, li_row:li_row + 1, pl.ds(r0, L)]
        r_b = rows_s[hd, b_row:b_row + 1, pl.ds(r0, L)]
        g_tot = r_b[:, g_lane:g_lane + 1]
        r_a = g_tot - r_b + r_li
        m_new = jnp.maximum(g_tot + m, jnp.max(r_a, axis=1, keepdims=True))
        decay = jnp.exp(g_tot + m - m_new)
        r_w = jnp.exp(r_a - m_new)
        a_col = jnp.sum(jnp.where(eye, r_li - r_b, 0.0), axis=1, keepdims=True)
        dmat = jnp.where(mask, a_col + r_b, NEG_BIG)
        inter_log = r_b + m
        m_t = jnp.maximum(inter_log, jnp.max(dmat, axis=0, keepdims=True))
        wts = jnp.exp(dmat - m_t)
        kc = k_ref[0, pl.ds(r0, L), hd * D:(hd + 1) * D]
        qtc = qt_ref[0, c, hd * D:(hd + 1) * D, :]
        vac = va_s[hd, :, pl.ds(r0, L)]
        cmat = c_s[hd, dr]
        both = jnp.dot(jnp.concatenate([kc, cmat.astype(BF16)], axis=0), qtc,
                       preferred_element_type=F32)
        s_w = (both[0:L] * wts).astype(BF16)
        tot = (jnp.dot(vac, s_w, preferred_element_type=F32)
               + jnp.exp(inter_log - m_t) * both[L:L + DA])
        den = jnp.maximum(jnp.abs(tot[D:D + 1]), jnp.exp(-m_t))
        h = tot[0:D] * (1.0 / den)
        h_s[hd * D:(hd + 1) * D, pl.ds(r0, L)] = h_s[hd * D:(hd + 1) * D, pl.ds(r0, L)] + h
        u = (vac.astype(F32) * r_w).astype(BF16)
        c_s[hd, dr] = decay * cmat + jnp.dot(u, kc, preferred_element_type=F32)
        return m_new

    def step(i, ms):
        out = []
        for hd in range(HB):
            out.append(chain(hd, 0, i, ms[2 * hd]))
            out.append(chain(hd, 1, NC - 1 - i, ms[2 * hd + 1]))
        return tuple(out)

    lax.fori_loop(0, NC, step, tuple(jnp.zeros((1, 1), F32) for _ in range(2 * HB)))

    def fin(c, carry):
        out_ref[0, c] = h_s[:, pl.ds(pl.multiple_of(c * L, L), L)].astype(BF16)
        return carry

    lax.fori_loop(0, NC, fin, 0, unroll=4)


def _mlstm(mt, k_tm, gcm, heads_per_step=4):
    B, S, _ = k_tm.shape
    NC = S // CHUNK
    D = HEAD_DIM
    HB = heads_per_step
    v0 = (M_WIDTH + 3 * H_WIDTH) // (HB * D)
    W = HB * D
    NB = M_HEADS // HB
    DA = D + BF16_ROWS
    cm_blk = (1, NC, W, LANES)
    return pl.pallas_call(
        _mlstm_kernel,
        grid=(B, NB),
        in_specs=[
            pl.BlockSpec(cm_blk, lambda b, h: (b, 0, h, 0)),
            pl.BlockSpec(cm_blk, lambda b, h: (b, 0, v0 + h, 0)),
            pl.BlockSpec((1, S, W), lambda b, h: (b, 0, h)),
            pl.BlockSpec((1, HB, 8, S), lambda b, h: (b, h, 0, 0)),
        ],
        out_specs=pl.BlockSpec(cm_blk, lambda b, h: (b, 0, h, 0)),
        out_shape=jax.ShapeDtypeStruct((B, NC, M_WIDTH, LANES), BF16),
        scratch_shapes=[
            pltpu.VMEM((HB, DA, S), BF16),
            pltpu.VMEM((HB, 8, S), F32),
            pltpu.VMEM((W, S), F32),
            pltpu.VMEM((HB, 2, DA, D), F32),
        ],
        compiler_params=_cparams("parallel", "parallel"),
        name="mlstm",
    )(mt, mt, k_tm, gcm)


def _filt_mlp_kernel(w1t_ref, b1_ref, w2t_ref, b2_ref, w3t_ref, b3_ref, fr_ref, w4f_ref, w4b_ref,
                     out_ref, h3_s, *, cb_rows):
    S = h3_s.shape[2]
    hi = lax.Precision.HIGHEST
    lane = lax.broadcasted_iota(jnp.int32, (1, S), 1)
    pos_f = lane.astype(F32)
    pos_b = (S - lane).astype(F32)

    def features(pos):
        bands = (FILTER_EMB - 1) // 2
        t = pos / (S - 1)
        ang = (2.0 * math.pi) * pos / S
        fidx = lax.broadcasted_iota(jnp.int32, (bands, 1), 0).astype(F32)
        f = 1e-4 + fidx * ((bands - 1 - 1e-4) / (bands - 1))
        fa = f * ang
        w1t = w1t_ref[...]
        pre = (w1t[:, 0:1] * t
               + jnp.dot(w1t[:, 1:1 + bands], jnp.cos(fa), precision=hi, preferred_element_type=F32)
               - jnp.dot(w1t[:, 1 + bands:], jnp.sin(fa), precision=hi, preferred_element_type=F32))
        fr = fr_ref[...]
        h = jnp.sin(fr[:, 0:1] * (pre + b1_ref[...]))
        h = jnp.sin(fr[:, 1:2] * (jnp.dot(w2t_ref[...], h, precision=hi, preferred_element_type=F32)
                                  + b2_ref[...]))
        return jnp.sin(fr[:, 2:3] * (jnp.dot(w3t_ref[...], h, precision=hi, preferred_element_type=F32)
                                     + b3_ref[...]))

    @pl.when(pl.program_id(0) == 0)
    def _():
        h3 = features(pos_f)
        h3_s[0] = h3
        n_t = S // LANES
        anti = (lax.broadcasted_iota(jnp.int32, (LANES, LANES), 0)
                + lax.broadcasted_iota(jnp.int32, (LANES, LANES), 1) == LANES - 1).astype(F32)
        rev = jnp.concatenate(
            [jnp.dot(h3[:, (n_t - 1 - u) * LANES:(n_t - u) * LANES], anti, precision=hi,
                     preferred_element_type=F32) for u in range(n_t)], axis=1)
        h3_s[1] = pltpu.roll(rev, 1, 1)

    r = pl.program_id(0) * cb_rows + lax.broadcasted_iota(jnp.int32, (cb_rows, 1), 0)
    ch = (r % H_WIDTH).astype(F32)
    max_decay = math.log(DECAY_TARGET) / FAST_DECAY_PCT
    min_decay = math.log(DECAY_TARGET) / SLOW_DECAY_PCT
    delta = jnp.abs(min_decay + ch * ((max_decay - min_decay) / (H_WIDTH - 1)))
    n_t = S // LANES
    for half, (w_ref, pos) in enumerate(((w4f_ref, pos_f), (w4b_ref, pos_b))):
        filt = jnp.dot(w_ref[...].astype(BF16), h3_s[half].astype(BF16),
                       preferred_element_type=F32)
        filt = filt * jnp.exp(-(pos / (S - 1)) * delta)
        if half == 1:
            filt = jnp.where(lane == 0, 0.0, filt)
        for u in range(n_t):
            out_ref[half * n_t + u] = filt[:, u * LANES:(u + 1) * LANES].astype(BF16)


def _filt_mlp(w1t, b1, w2t, b2, w3t, b3, fr, w4t, S, cb_rows=256):
    R = w4t.shape[0] // 2
    Hd = FILTER_HIDDEN
    const = lambda i: (0, 0)
    nblk = R // cb_rows
    return pl.pallas_call(
        functools.partial(_filt_mlp_kernel, cb_rows=cb_rows),
        grid=(nblk,),
        in_specs=[
            pl.BlockSpec((Hd, FILTER_EMB), const),
            pl.BlockSpec((Hd, 1), const),
            pl.BlockSpec((Hd, Hd), const),
            pl.BlockSpec((Hd, 1), const),
            pl.BlockSpec((Hd, Hd), const),
            pl.BlockSpec((Hd, 1), const),
            pl.BlockSpec((Hd, 3), const),
            pl.BlockSpec((cb_rows, Hd), lambda i: (i, 0)),
            pl.BlockSpec((cb_rows, Hd), lambda i: (nblk + i, 0)),
        ],
        out_specs=pl.BlockSpec((2 * S // LANES, cb_rows, LANES), lambda i: (0, i, 0)),
        out_shape=jax.ShapeDtypeStruct((2 * S // LANES, R, LANES), BF16),
        scratch_shapes=[pltpu.VMEM((2, Hd, S), F32)],
        compiler_params=_cparams("arbitrary"),
        name="filt_mlp",
    )(w1t, b1, w2t, b2, w3t, b3, fr, w4t, w4t)


@functools.lru_cache(maxsize=None)
def _dft_constants(S):
    N = 2 * S
    N2 = LANES
    N1 = N // N2
    h = N1 // 2
    k1 = np.arange(N1)
    k2 = np.arange(N2)
    a1 = -2.0 * np.pi * np.outer(k1, k1) / N1
    f1r, f1i = np.cos(a1), np.sin(a1)
    at = -2.0 * np.pi * np.outer(k1, k2) / N
    twr, twi = np.cos(at), np.sin(at)
    a2 = -2.0 * np.pi * np.outer(k2, k2) / N2
    f2r, f2i = np.cos(a2), np.sin(a2)
    s1c = np.block([[f1r[:, :h], -f1i[:, :h]], [f1i[:, :h], f1r[:, :h]]])
    s1r = np.concatenate([f1r, f1i], axis=0)
    s2 = np.block([[f2r, f2i], [-f2i, f2r]])
    s2i = np.block([[f2r, -f2i], [f2i, f2r]])
    s1i = np.block([[f1r[:h, :], f1i[:h, :]], [-f1i[:h, :], f1r[:h, :]]])
    cast = lambda a: np.asarray(a, np.float32)
    return dict(s1c=cast(s1c), s1r=cast(s1r), s2=cast(s2), s2i=cast(s2i), s1i=cast(s1i),
                twr=cast(twr), twi=cast(twi), N1=N1, h=h)


def _bf(a):
    return jnp.asarray(a, F32).astype(BF16)


def _filt_fft_kernel(bias_ref, kern_ref, s1r_ref, s2_ref, twr_ref, twi_ref, out_ref,
                     src_s, slab_s, *, unroll):
    N1 = twr_ref.shape[0]
    n_ch = kern_ref.shape[1]
    pitch = src_s.shape[0] // N1
    scale = 1.0 / (N1 * LANES)
    twr = twr_ref[...]
    twi = twi_ref[...]
    s1r = s1r_ref[...]

    def load_in(t1, carry):
        src_s[pl.ds(pl.multiple_of(t1 * pitch, 8), n_ch), :] = kern_ref[t1].astype(F32)
        return carry
    lax.fori_loop(0, N1, load_in, 0, unroll=4)

    def per_group(j, carry):
        for u in range(unroll):
            c = j * unroll + u
            z = src_s[pl.ds(c, N1, stride=pitch), :].astype(BF16)
            a = jnp.dot(s1r, z, preferred_element_type=F32)
            ar, ai = a[0:N1], a[N1:2 * N1]
            r0 = pl.multiple_of(c * N1, N1)
            slab_s[pl.ds(r0, N1), 0:LANES] = (ar * twr - ai * twi).astype(BF16)
            slab_s[pl.ds(r0, N1), LANES:2 * LANES] = (ar * twi + ai * twr).astype(BF16)
        return carry

    lax.fori_loop(0, n_ch // unroll, per_group, 0)
    x = jnp.dot(slab_s[...], s2_ref[...], preferred_element_type=F32)
    bias = bias_ref[...]
    out_ref[:, :, 0:LANES] = ((x[:, 0:LANES].reshape(n_ch, N1, LANES) + bias) * scale).astype(BF16)
    out_ref[:, :, LANES:2 * LANES] = (x[:, LANES:2 * LANES].reshape(n_ch, N1, LANES)
                                      * scale).astype(BF16)


def _filt_fft(bias3, kern, consts, n_ch=32, unroll=8):
    N1, R, _ = kern.shape
    const = lambda i: (0, 0)
    pitch = n_ch + 8
    return pl.pallas_call(
        functools.partial(_filt_fft_kernel, unroll=unroll),
        grid=(R // n_ch,),
        in_specs=[
            pl.BlockSpec((n_ch, 1, 1), lambda i: (i, 0, 0)),
            pl.BlockSpec((N1, n_ch, LANES), lambda i: (0, i, 0)),
            pl.BlockSpec((2 * N1, N1), const),
            pl.BlockSpec((2 * LANES, 2 * LANES), const),
            pl.BlockSpec((N1, LANES), const),
            pl.BlockSpec((N1, LANES), const),
        ],
        out_specs=pl.BlockSpec((n_ch, N1, 2 * LANES), lambda i: (i, 0, 0)),
        out_shape=jax.ShapeDtypeStruct((R, N1, 2 * LANES), BF16),
        scratch_shapes=[pltpu.VMEM((N1 * pitch, LANES), F32),
                        pltpu.VMEM((n_ch * N1, 2 * LANES), BF16)],
        compiler_params=_cparams("parallel"),
        name="filt_fft",
    )(bias3, kern, _bf(consts["s1r"]), _bf(consts["s2"]),
      jnp.asarray(consts["twr"]), jnp.asarray(consts["twi"]))


def _hyena_kernel(nw_ref, v_ref, x1_ref, x2_ref, khat_ref, s1c_ref, s2_ref, s2i_ref, s1i_ref,
                  twr_ref, twi_ref, out_ref, src_s, slab_s, yf_s, q_s, y_s, *, unroll):
    h = v_ref.shape[1]
    n_ch = v_ref.shape[2]
    N1 = 2 * h
    twr = twr_ref[...]
    twi = twi_ref[...]
    s1c = s1c_ref[...]
    s1i = s1i_ref[...]
    n_groups = n_ch // unroll
    rows = unroll * N1
    pitch = src_s.shape[0] // (2 * h)

    def tile_rows(i):
        return pl.ds(pl.multiple_of(i * pitch, 8), n_ch)

    def chan_rows(c, bb):
        return pl.ds(bb * h * pitch + c, h, stride=pitch)

    def long_conv(order):
        def stage_a(g):
            for u in range(unroll):
                c = g * unroll + u
                z = jnp.concatenate([src_s[chan_rows(c, 0), :], src_s[chan_rows(c, 1), :]], axis=0)
                a = jnp.dot(s1c, z.astype(BF16), preferred_element_type=F32)
                ar, ai = a[0:N1], a[N1:2 * N1]
                r0 = c * N1 if isinstance(c, int) else pl.multiple_of(c * N1, N1)
                slab_s[pl.ds(r0, N1), 0:LANES] = (ar * twr - ai * twi).astype(BF16)
                slab_s[pl.ds(r0, N1), LANES:2 * LANES] = (ar * twi + ai * twr).astype(BF16)

        def stage_m1(g):
            r0 = g * rows if isinstance(g, int) else pl.multiple_of(g * rows, rows)
            c0 = g * unroll if isinstance(g, int) else pl.multiple_of(g * unroll, unroll)
            x = jnp.dot(slab_s[pl.ds(r0, rows), :], s2_ref[...], preferred_element_type=F32)
            xr = x[:, 0:LANES].reshape(unroll, N1, LANES)
            xi = x[:, LANES:2 * LANES].reshape(unroll, N1, LANES)
            kr = khat_ref[order, pl.ds(c0, unroll), :, 0:LANES].astype(F32)
            ki = khat_ref[order, pl.ds(c0, unroll), :, LANES:2 * LANES].astype(F32)
            yf_s[pl.ds(r0, rows), 0:LANES] = (xr * kr - xi * ki).reshape(rows, LANES).astype(BF16)
            yf_s[pl.ds(r0, rows), LANES:2 * LANES] = (xr * ki + xi * kr).reshape(rows, LANES).astype(BF16)

        def stage_m2(g):
            r0 = g * rows if isinstance(g, int) else pl.multiple_of(g * rows, rows)
            c0 = g * unroll if isinstance(g, int) else pl.multiple_of(g * unroll, unroll)
            p = jnp.dot(yf_s[pl.ds(r0, rows), :], s2i_ref[...], preferred_element_type=F32)
            pr = p[:, 0:LANES].reshape(unroll, N1, LANES)
            pi = p[:, LANES:2 * LANES].reshape(unroll, N1, LANES)
            q_s[pl.ds(c0, unroll), 0:N1, :] = (pr * twr + pi * twi).astype(BF16)
            q_s[pl.ds(c0, unroll), N1:2 * N1, :] = (pi * twr - pr * twi).astype(BF16)

        def stage_i(g):
            for u in range(unroll):
                c = g * unroll + u
                y = jnp.dot(s1i, q_s[c], preferred_element_type=F32)
                y_s[chan_rows(c, 0), :] = y[0:h]
                y_s[chan_rows(c, 1), :] = y[h:2 * h]

        G = n_groups
        stage_a(0)
        stage_m1(0)
        stage_a(1)
        stage_m2(0)
        stage_m1(1)
        stage_a(2)

        def steady(i, carry):
            stage_i(i)
            stage_m2(i + 1)
            stage_m1(i + 2)
            stage_a(i + 3)
            return carry
        lax.fori_loop(0, G - 3, steady, 0)
        stage_i(G - 3)
        stage_m2(G - 2)
        stage_m1(G - 1)
        stage_i(G - 2)
        stage_m2(G - 1)
        stage_i(G - 1)

    def per_tile(fn):
        for bb in range(2):
            def body(t1, carry, bb=bb):
                fn(bb, t1, tile_rows(bb * h + t1))
                return carry
            lax.fori_loop(0, h, body, 0, unroll=8)

    def load_in(bb, t1, rws):
        src_s[rws, :] = v_ref[bb, t1].astype(F32)
    per_tile(load_in)
    long_conv(0)

    def gate1(bb, t1, rws):
        src_s[rws, :] = y_s[rws, :] * x1_ref[bb, t1].astype(F32)
    per_tile(gate1)
    long_conv(1)

    nw = jnp.broadcast_to(nw_ref[...], (n_ch, LANES))

    def gate2_norm(bb, t1, rws):
        z = y_s[rws, :] * x2_ref[bb, t1].astype(F32)
        ms = jnp.mean(z * z, axis=0, keepdims=True)
        out_ref[bb, t1] = (z * lax.rsqrt(ms + EPS) * nw).astype(BF16)
    per_tile(gate2_norm)


def _hyena(nw_col, hy4, khat, consts, unroll=8):
    B, h, _, _ = hy4.shape
    C = H_WIDTH
    n_ch = GROUP_W
    G = C // n_ch
    g0 = M_WIDTH // n_ch
    N1 = consts["N1"]
    const = lambda g, p: (0, 0)
    blk = (2, h, n_ch, LANES)
    pitch = n_ch + 8
    return pl.pallas_call(
        functools.partial(_hyena_kernel, unroll=unroll),
        grid=(G, B // 2),
        in_specs=[
            pl.BlockSpec((n_ch, 1), lambda g, p: (g, 0)),
            pl.BlockSpec(blk, lambda g, p: (p, 0, g0 + g, 0)),
            pl.BlockSpec(blk, lambda g, p: (p, 0, g0 + G + g, 0)),
            pl.BlockSpec(blk, lambda g, p: (p, 0, g0 + 2 * G + g, 0)),
            pl.BlockSpec((2, n_ch, N1, 2 * LANES), lambda g, p: (0, g, 0, 0)),
            pl.BlockSpec((2 * N1, 2 * h), const),
            pl.BlockSpec((2 * LANES, 2 * LANES), const),
            pl.BlockSpec((2 * LANES, 2 * LANES), const),
            pl.BlockSpec((2 * h, 2 * N1), const),
            pl.BlockSpec((N1, LANES), const),
            pl.BlockSpec((N1, LANES), const),
        ],
        out_specs=pl.BlockSpec(blk, lambda g, p: (p, 0, g, 0)),
        out_shape=jax.ShapeDtypeStruct((B, h, C, LANES), BF16),
        scratch_shapes=[
            pltpu.VMEM((2 * h * pitch, LANES), F32),
            pltpu.VMEM((n_ch * N1, 2 * LANES), BF16),
            pltpu.VMEM((n_ch * N1, 2 * LANES), BF16),
            pltpu.VMEM((n_ch, 2 * N1, LANES), BF16),
            pltpu.VMEM((2 * h * pitch, LANES), F32),
        ],
        compiler_params=_cparams("parallel", "arbitrary"),
        name="hyena",
    )(nw_col, hy4, hy4, hy4, khat,
      _bf(consts["s1c"]), _bf(consts["s2"]), _bf(consts["s2i"]), _bf(consts["s1i"]),
      jnp.asarray(consts["twr"]), jnp.asarray(consts["twi"]))


def _outmlp_kernel(x_ref, hm_ref, og_ref, yh_ref, nwm_ref, wo_ref, w1_ref, w2_ref, n_post_ref,
                   n_pre_ref, n_post2_ref, out_ref):
    n_t = hm_ref.shape[1]
    nwm = jnp.broadcast_to(nwm_ref[...], (M_WIDTH, LANES))
    tiles = []
    for j in range(n_t):
        hg = hm_ref[0, j].astype(F32) * og_ref[0, j].astype(F32)
        heads = []
        for hd in range(M_HEADS):
            hh = hg[hd * HEAD_DIM:(hd + 1) * HEAD_DIM]
            ms = jnp.mean(hh * hh, axis=0, keepdims=True)
            heads.append(hh * lax.rsqrt(ms + EPS))
        ym = jnp.concatenate(heads, axis=0) * nwm
        yt = jnp.concatenate([ym, yh_ref[0, j].astype(F32)], axis=0)
        tiles.append(yt.T.astype(BF16))
    y = tiles[0] if n_t == 1 else jnp.concatenate(tiles, axis=0)
    mix = jnp.dot(y, wo_ref[...], preferred_element_type=F32)
    x1 = x_ref[0] + _rms_rows(mix, n_post_ref[...])
    hm = _rms_rows(x1, n_pre_ref[...]).astype(BF16)
    mid = jnp.maximum(jnp.dot(hm, w1_ref[...], preferred_element_type=F32), 0.0)
    mid = (mid * mid).astype(BF16)
    ff = jnp.dot(mid, w2_ref[...], preferred_element_type=F32)
    out_ref[0] = x1 + _rms_rows(ff, n_post2_ref[...])


def _outmlp(x, hm, mt, yh, nwm_col, wo, w1, w2, n_post, n_pre, n_post2, tm_rows):
    B, S, D = x.shape
    n_t = tm_rows // LANES
    og0 = mt.shape[2] // M_WIDTH - 1
    const = lambda b, i: (0, 0)
    resident = functools.partial(pl.BlockSpec, index_map=const, pipeline_mode=pl.Buffered(1))
    return pl.pallas_call(
        _outmlp_kernel,
        grid=(B, S // tm_rows),
        in_specs=[
            pl.BlockSpec((1, tm_rows, D), lambda b, i: (b, i, 0)),
            pl.BlockSpec((1, n_t, M_WIDTH, LANES), lambda b, i: (b, i, 0, 0)),
            pl.BlockSpec((1, n_t, M_WIDTH, LANES), lambda b, i: (b, i, og0, 0)),
            pl.BlockSpec((1, n_t, H_WIDTH, LANES), lambda b, i: (b, i, 0, 0)),
            pl.BlockSpec((M_WIDTH, 1), const),
            resident((D, D)),
            resident((D, D_FF)),
            resident((D_FF, D)),
            pl.BlockSpec((1, D), const),
            pl.BlockSpec((1, D), const),
            pl.BlockSpec((1, D), const),
        ],
        out_specs=pl.BlockSpec((1, tm_rows, D), lambda b, i: (b, i, 0)),
        out_shape=jax.ShapeDtypeStruct((B, S, D), F32),
        compiler_params=_cparams("parallel", "parallel"),
        name="outmlp",
    )(x, hm, mt, yh, nwm_col, wo, w1, w2, n_post, n_pre, n_post2)


def kernel(x, norm_mix_pre, norm_mix_post, norm_mlp_pre, norm_mlp_post, w_in, b_gates,
           conv_w, conv_b, mlstm_norm_w, hyena_norm_w, filt_w1, filt_b1, filt_w2, filt_b2,
           filt_w3, filt_b3, filt_w4, filt_freq, filt_bias, w_out, w_mlp_in, w_mlp_out):
    B, S, D = x.shape
    assert D == D_MODEL and B % 2 == 0 and S % CHUNK == 0
    H = M_HEADS
    row = lambda a: a.astype(F32).reshape(1, -1)
    col = lambda a: a.astype(F32).reshape(-1, 1)
    tm_rows = min(1024, S)

    n_conv = 2 * M_WIDTH + 3 * H_WIDTH
    o_gate = n_conv + 2 * M_WIDTH
    assert w_in.shape == (D, o_gate + N_GATE)
    cw = conv_w.astype(F32)
    cbias = conv_b.astype(F32)
    ident = jnp.zeros((3, 2 * M_WIDTH), F32).at[1].set(1.0)
    cw_cm = jnp.concatenate([cw[:, 0:M_WIDTH], cw[:, 2 * M_WIDTH:n_conv], ident], axis=1)
    cb_cm = jnp.concatenate([cbias[0:M_WIDTH], cbias[2 * M_WIDTH:n_conv],
                             jnp.zeros((2 * M_WIDTH,), F32)]).reshape(1, -1)
    mt, k_tm, gcm = _proj(x, row(norm_mix_pre), w_in.astype(F32).T, col(b_gates), cw_cm, cb_cm,
                          cw[:, M_WIDTH:2 * M_WIDTH], row(cbias[M_WIDTH:2 * M_WIDTH]), tm_rows)

    g4 = gcm.reshape(B, 4, H, S).transpose(0, 2, 1, 3)
    g8 = jnp.concatenate([g4, jnp.zeros_like(g4)], axis=2)
    h_m = _mlstm(mt, k_tm, g8)

    consts = _dft_constants(S)
    kern = _filt_mlp(filt_w1.astype(F32).T, col(filt_b1), filt_w2.astype(F32).T, col(filt_b2),
                     filt_w3.astype(F32).T, col(filt_b3), filt_freq.astype(F32).T,
                     filt_w4.astype(F32).T, S)
    khat = _filt_fft(filt_bias.astype(F32).reshape(2 * H_WIDTH, 1, 1), kern, consts)
    khat = khat.reshape(2, H_WIDTH, consts["N1"], 2 * LANES)

    y_h = _hyena(col(hyena_norm_w), mt, khat, consts)

    return _outmlp(x, h_m, mt, y_h, col(mlstm_norm_w),
                   w_out.astype(BF16), w_mlp_in.astype(BF16), w_mlp_out.astype(BF16),
                   row(norm_mix_post), row(norm_mlp_pre), row(norm_mlp_post), min(512, S))
```

```python
import functools
import math

import numpy as np
import jax
import jax.numpy as jnp
from jax import lax
from jax.experimental import pallas as pl
from jax.experimental.pallas import tpu as pltpu

F32 = jnp.float32
BF16 = jnp.bfloat16

D_MODEL = 1024
M_WIDTH = 512
M_HEADS = 4
HEAD_DIM = 128
H_WIDTH = 512
H_GROUPS = 8
GROUP_W = H_WIDTH // H_GROUPS
CHUNK = 128
FILTER_EMB = 33
FILTER_HIDDEN = 64
DECAY_TARGET = 1e-2
FAST_DECAY_PCT = 0.3
SLOW_DECAY_PCT = 1.5
D_FF = 4 * D_MODEL
N_GATE = 16
EPS = 1e-6
LANES = 128
BF16_ROWS = 16
NEG_BIG = -1e30
VMEM_LIMIT = 56 * 1024 * 1024


def _cparams(*sem):
    return pltpu.CompilerParams(dimension_semantics=sem, vmem_limit_bytes=VMEM_LIMIT)


def _rms_rows(xf, w):
    ms = jnp.mean(xf * xf, axis=-1, keepdims=True)
    return xf * lax.rsqrt(ms + EPS) * w


def _sigmoid(x):
    return 1.0 / (1.0 + jnp.exp(-x))


def _log_sigmoid(x):
    return jnp.minimum(x, 0.0) - jnp.log(1.0 + jnp.exp(-jnp.abs(x)))


def _proj_kernel(x_ref, xp_ref, xn_ref, nw_ref, wt_ref, bgt_ref, cw_ref, cb_ref, cwk_ref, cbk_ref,
                 cm_ref, k_ref, gcm_ref, hall_s, pc_s, *, col_blk):
    TM = x_ref.shape[1]
    HALO = xp_ref.shape[1]
    j = pl.program_id(1)
    nw = nw_ref[...]
    hn = _rms_rows(x_ref[0], nw).astype(BF16)
    hall_s[0:HALO, :] = _rms_rows(xp_ref[0], nw).astype(BF16)
    hall_s[HALO:HALO + TM, :] = hn
    hall_s[HALO + TM:2 * HALO + TM, :] = _rms_rows(xn_ref[0], nw).astype(BF16)
    keep_p = jnp.where(j > 0, 1.0, 0.0)
    keep_n = jnp.where(j < pl.num_programs(1) - 1, 1.0, 0.0)
    n_cm = cw_ref.shape[1]
    n_blk = n_cm // col_blk
    w_k0 = M_WIDTH
    w_gate = n_cm + M_WIDTH
    nt_dims = (((1,), (1,)), ((), ()))
    gt = lax.dot_general(wt_ref[w_gate:w_gate + N_GATE, :].astype(BF16), hn, nt_dims,
                         preferred_element_type=F32) + bgt_ref[...]
    row = lax.broadcasted_iota(jnp.int32, gt.shape, 0)
    gcm_ref[0] = jnp.where((row % 8) >= 4, _log_sigmoid(gt), gt)

    n_rb = TM // LANES
    n_sl = col_blk // LANES

    def project(w_row0, slot):
        wblk = wt_ref[pl.ds(w_row0, col_blk), :].T.astype(BF16)
        res = jnp.dot(hall_s[...], wblk, preferred_element_type=F32)
        for u in range(n_sl):
            ls = slice(u * LANES, (u + 1) * LANES)
            pc_s[slot, u, 0:HALO, :] = res[0:HALO, ls] * keep_p
            pc_s[slot, u, HALO:HALO + TM, :] = res[HALO:HALO + TM, ls]
            pc_s[slot, u, HALO + TM:2 * HALO + TM, :] = res[HALO + TM:2 * HALO + TM, ls] * keep_n

    def conv_tile(slot, u, rb, taps, bias):
        r = HALO + rb * LANES
        return (pc_s[slot, u, pl.ds(r - 1, LANES, stride=1), :] * taps[0:1]
                + pc_s[slot, u, r:r + LANES, :] * taps[1:2]
                + pc_s[slot, u, pl.ds(r + 1, LANES, stride=1), :] * taps[2:3] + bias)

    k_scale = HEAD_DIM ** -0.5
    for kb in range(M_WIDTH // col_blk):
        project(w_k0 + kb * col_blk, kb)
        for u in range(n_sl):
            cs = slice(kb * col_blk + u * LANES, kb * col_blk + (u + 1) * LANES)
            for rb in range(n_rb):
                cv = conv_tile(kb, u, rb, cwk_ref[:, cs], cbk_ref[:, cs])
                k_ref[0, rb * LANES:(rb + 1) * LANES, cs] = (cv * _sigmoid(cv) * k_scale).astype(BF16)

    def w_row_of(c):
        return jnp.where(c < M_WIDTH // col_blk, c, c + M_WIDTH // col_blk) * col_blk

    def emit(c, slot, act):
        for u in range(n_sl):
            col0 = c * col_blk + u * LANES
            if not isinstance(c, int):
                col0 = pl.multiple_of(col0, LANES)
            taps = cw_ref[:, pl.ds(col0, LANES)]
            bias = cb_ref[:, pl.ds(col0, LANES)]
            for rb in range(n_rb):
                cv = conv_tile(slot, u, rb, taps, bias)
                if act == "silu":
                    cv = cv * _sigmoid(cv)
                elif act == "sigmoid":
                    cv = _sigmoid(cv)
                cm_ref[0, rb, pl.ds(col0, LANES), :] = cv.T.astype(BF16)

    def run(lo, hi, act):
        def body(i, carry):
            c = lo + 2 * i
            emit(c, 0, act)
            project(pl.multiple_of(w_row_of(c + 1), col_blk), 1)
            emit(c + 1, 1, act)
            project(pl.multiple_of(w_row_of(c + 2), col_blk), 0)
            return carry
        lax.fori_loop(0, (hi - lo) // 2, body, 0)

    n_q = M_WIDTH // col_blk
    assert n_q % 2 == 0 and n_blk % 2 == 0
    project(0, 0)
    run(0, n_q, "silu")
    run(n_q, n_blk - n_q, "none")
    run(n_blk - n_q, n_blk - 2, "sigmoid")
    emit(n_blk - 2, 0, "sigmoid")
    project((n_blk - 1 + n_q) * col_blk, 1)
    emit(n_blk - 1, 1, "sigmoid")


def _proj(x, nw, w_in_t, bgt, cw_cm, cb_cm, cw_k, cb_k, tm_rows, col_blk=256):
    B, S, D = x.shape
    HALO = BF16_ROWS
    n_cm = cw_cm.shape[1]
    n_t = tm_rows // LANES
    hb = tm_rows // HALO
    last = S // HALO - 1
    const = lambda b, j: (0, 0)
    resident = functools.partial(pl.BlockSpec, index_map=const, pipeline_mode=pl.Buffered(1))
    return pl.pallas_call(
        functools.partial(_proj_kernel, col_blk=col_blk),
        grid=(B, S // tm_rows),
        in_specs=[
            pl.BlockSpec((1, tm_rows, D), lambda b, j: (b, j, 0)),
            pl.BlockSpec((1, HALO, D), lambda b, j: (b, jnp.maximum(j * hb - 1, 0), 0)),
            pl.BlockSpec((1, HALO, D), lambda b, j: (b, jnp.minimum((j + 1) * hb, last), 0)),
            pl.BlockSpec((1, D), const),
            resident(w_in_t.shape),
            pl.BlockSpec((N_GATE, 1), const),
            pl.BlockSpec((3, n_cm), const),
            pl.BlockSpec((1, n_cm), const),
            pl.BlockSpec((3, M_WIDTH), const),
            pl.BlockSpec((1, M_WIDTH), const),
        ],
        out_specs=[
            pl.BlockSpec((1, n_t, n_cm, LANES), lambda b, j: (b, j, 0, 0)),
            pl.BlockSpec((1, tm_rows, M_WIDTH), lambda b, j: (b, j, 0)),
            pl.BlockSpec((1, N_GATE, tm_rows), lambda b, j: (b, 0, j)),
        ],
        out_shape=[
            jax.ShapeDtypeStruct((B, S // LANES, n_cm, LANES), BF16),
            jax.ShapeDtypeStruct((B, S, M_WIDTH), BF16),
            jax.ShapeDtypeStruct((B, N_GATE, S), F32),
        ],
        scratch_shapes=[pltpu.VMEM((tm_rows + 2 * HALO, D), BF16),
                        pltpu.VMEM((2, col_blk // LANES, tm_rows + 2 * HALO, LANES), F32)],
        compiler_params=_cparams("parallel", "parallel"),
        name="proj",
    )(x, x, x, nw, w_in_t, bgt, cw_cm, cb_cm, cw_k, cb_k)


def _mlstm_kernel(qt_ref, vt_ref, k_ref, g_ref, out_ref, va_s, rows_s, h_s, c_s):
    S = k_ref.shape[1]
    HB = g_ref.shape[1]
    L = CHUNK
    NC = S // L
    D = HEAD_DIM
    DA = D + BF16_ROWS
    W = HB * D

    for hd in range(HB):
        g = g_ref[0, hd]
        lane = lax.broadcasted_iota(jnp.int32, g.shape, 1) % L
        pre = g
        suf = g
        d = 1
        while d < L:
            pre = pre + jnp.where(lane >= d, pltpu.roll(pre, d, 1), 0.0)
            suf = suf + jnp.where(lane < L - d, pltpu.roll(suf, S - d, 1), 0.0)
            d *= 2
        rows_s[hd, 0:1, :] = g[0:1, :]
        rows_s[hd, 1:2, :] = pre[1:2, :]
        rows_s[hd, 2:3, :] = g[2:3, :]
        rows_s[hd, 3:4, :] = suf[3:4, :]

    ones_row = (lax.broadcasted_iota(jnp.int32, (BF16_ROWS, L), 0) == 0).astype(BF16)

    def prep(c, carry):
        r0 = pl.multiple_of(c * L, L)
        h_s[:, pl.ds(r0, L)] = jnp.zeros((W, L), F32)
        for hd in range(HB):
            va_s[hd, 0:D, pl.ds(r0, L)] = vt_ref[0, c, hd * D:(hd + 1) * D, :]
            va_s[hd, D:DA, pl.ds(r0, L)] = ones_row
        return carry

    lax.fori_loop(0, NC, prep, 0)
    c_s[...] = jnp.zeros_like(c_s)

    si = lax.broadcasted_iota(jnp.int32, (L, L), 0)
    ti = lax.broadcasted_iota(jnp.int32, (L, L), 1)
    eye = si == ti
    dirs = ((0, 1, si <= ti, L - 1), (2, 3, si >= ti, 0))

    def chain(hd, dr, c, m):
        li_row, b_row, mask, g_lane = dirs[dr]
        r0 = pl.multiple_of(c * L, L)
        r_li = rows_s[hd, li_row:li_row + 1, pl.ds(r0, L)]
        r_b = rows_s[hd, b_row:b_row + 1, pl.ds(r0, L)]
        g_tot = r_b[:, g_lane:g_lane + 1]
        r_a = g_tot - r_b + r_li
        m_new = jnp.maximum(g_tot + m, jnp.max(r_a, axis=1, keepdims=True))
        decay = jnp.exp(g_tot + m - m_new)
        r_w = jnp.exp(r_a - m_new)
        a_col = jnp.sum(jnp.where(eye, r_li - r_b, 0.0), axis=1, keepdims=True)
        dmat = jnp.where(mask, a_col + r_b, NEG_BIG)
        inter_log = r_b + m
        m_t = jnp.maximum(inter_log, jnp.max(dmat, axis=0, keepdims=True))
        wts = jnp.exp(dmat - m_t)
        kc = k_ref[0, pl.ds(r0, L), hd * D:(hd + 1) * D]
        qtc = qt_ref[0, c, hd * D:(hd + 1) * D, :]
        vac = va_s[hd, :, pl.ds(r0, L)]
        cmat = c_s[hd, dr]
        both = jnp.dot(jnp.concatenate([kc, cmat.astype(BF16)], axis=0), qtc,
                       preferred_element_type=F32)
        s_w = (both[0:L] * wts).astype(BF16)
        tot = (jnp.dot(vac, s_w, preferred_element_type=F32)
               + jnp.exp(inter_log - m_t) * both[L:L + DA])
        den = jnp.maximum(jnp.abs(tot[D:D + 1]), jnp.exp(-m_t))
        h = tot[0:D] * (1.0 / den)
        h_s[hd * D:(hd + 1) * D, pl.ds(r0, L)] = h_s[hd * D:(hd + 1) * D, pl.ds(r0, L)] + h
        u = (vac.astype(F32) * r_w).astype(BF16)
        c_s[hd, dr] = decay * cmat + jnp.dot(u, kc, preferred_element_type=F32)
        return m_new

    def step(i, ms):
        out = []
        for hd in range(HB):
            out.append(chain(hd, 0, i, ms[2 * hd]))
            out.append(chain(hd, 1, NC - 1 - i, ms[2 * hd + 1]))
        return tuple(out)

    lax.fori_loop(0, NC, step, tuple(jnp.zeros((1, 1), F32) for _ in range(2 * HB)))

    def fin(c, carry):
        out_ref[0, c] = h_s[:, pl.ds(pl.multiple_of(c * L, L), L)].astype(BF16)
        return carry

    lax.fori_loop(0, NC, fin, 0, unroll=4)


def _mlstm(mt, k_tm, gcm, heads_per_step=4):
    B, S, _ = k_tm.shape
    NC = S // CHUNK
    D = HEAD_DIM
    HB = heads_per_step
    v0 = (M_WIDTH + 3 * H_WIDTH) // (HB * D)
    W = HB * D
    NB = M_HEADS // HB
    DA = D + BF16_ROWS
    cm_blk = (1, NC, W, LANES)
    return pl.pallas_call(
        _mlstm_kernel,
        grid=(B, NB),
        in_specs=[
            pl.BlockSpec(cm_blk, lambda b, h: (b, 0, h, 0)),
            pl.BlockSpec(cm_blk, lambda b, h: (b, 0, v0 + h, 0)),
            pl.BlockSpec((1, S, W), lambda b, h: (b, 0, h)),
            pl.BlockSpec((1, HB, 8, S), lambda b, h: (b, h, 0, 0)),
        ],
        out_specs=pl.BlockSpec(cm_blk, lambda b, h: (b, 0, h, 0)),
        out_shape=jax.ShapeDtypeStruct((B, NC, M_WIDTH, LANES), BF16),
        scratch_shapes=[
            pltpu.VMEM((HB, DA, S), BF16),
            pltpu.VMEM((HB, 8, S), F32),
            pltpu.VMEM((W, S), F32),
            pltpu.VMEM((HB, 2, DA, D), F32),
        ],
        compiler_params=_cparams("parallel", "parallel"),
        name="mlstm",
    )(mt, mt, k_tm, gcm)


def _filt_mlp_kernel(w1t_ref, b1_ref, w2t_ref, b2_ref, w3t_ref, b3_ref, fr_ref, w4f_ref, w4b_ref,
                     out_ref, h3_s, *, cb_rows):
    S = h3_s.shape[2]
    hi = lax.Precision.HIGHEST
    lane = lax.broadcasted_iota(jnp.int32, (1, S), 1)
    pos_f = lane.astype(F32)
    pos_b = (S - lane).astype(F32)

    def features(pos):
        bands = (FILTER_EMB - 1) // 2
        t = pos / (S - 1)
        ang = (2.0 * math.pi) * pos / S
        fidx = lax.broadcasted_iota(jnp.int32, (bands, 1), 0).astype(F32)
        f = 1e-4 + fidx * ((bands - 1 - 1e-4) / (bands - 1))
        fa = f * ang
        w1t = w1t_ref[...]
        pre = (w1t[:, 0:1] * t
               + jnp.dot(w1t[:, 1:1 + bands], jnp.cos(fa), precision=hi, preferred_element_type=F32)
               - jnp.dot(w1t[:, 1 + bands:], jnp.sin(fa), precision=hi, preferred_element_type=F32))
        fr = fr_ref[...]
        h = jnp.sin(fr[:, 0:1] * (pre + b1_ref[...]))
        h = jnp.sin(fr[:, 1:2] * (jnp.dot(w2t_ref[...], h, precision=hi, preferred_element_type=F32)
                                  + b2_ref[...]))
        return jnp.sin(fr[:, 2:3] * (jnp.dot(w3t_ref[...], h, precision=hi, preferred_element_type=F32)
                                     + b3_ref[...]))

    @pl.when(pl.program_id(0) == 0)
    def _():
        h3 = features(pos_f)
        h3_s[0] = h3
        n_t = S // LANES
        anti = (lax.broadcasted_iota(jnp.int32, (LANES, LANES), 0)
                + lax.broadcasted_iota(jnp.int32, (LANES, LANES), 1) == LANES - 1).astype(F32)
        rev = jnp.concatenate(
            [jnp.dot(h3[:, (n_t - 1 - u) * LANES:(n_t - u) * LANES], anti, precision=hi,
                     preferred_element_type=F32) for u in range(n_t)], axis=1)
        h3_s[1] = pltpu.roll(rev, 1, 1)

    r = pl.program_id(0) * cb_rows + lax.broadcasted_iota(jnp.int32, (cb_rows, 1), 0)
    ch = (r % H_WIDTH).astype(F32)
    max_decay = math.log(DECAY_TARGET) / FAST_DECAY_PCT
    min_decay = math.log(DECAY_TARGET) / SLOW_DECAY_PCT
    delta = jnp.abs(min_decay + ch * ((max_decay - min_decay) / (H_WIDTH - 1)))
    n_t = S // LANES
    for half, (w_ref, pos) in enumerate(((w4f_ref, pos_f), (w4b_ref, pos_b))):
        filt = jnp.dot(w_ref[...].astype(BF16), h3_s[half].astype(BF16),
                       preferred_element_type=F32)
        filt = filt * jnp.exp(-(pos / (S - 1)) * delta)
        if half == 1:
            filt = jnp.where(lane == 0, 0.0, filt)
        for u in range(n_t):
            out_ref[half * n_t + u] = filt[:, u * LANES:(u + 1) * LANES].astype(BF16)


def _filt_mlp(w1t, b1, w2t, b2, w3t, b3, fr, w4t, S, cb_rows=256):
    R = w4t.shape[0] // 2
    Hd = FILTER_HIDDEN
    const = lambda i: (0, 0)
    nblk = R // cb_rows
    return pl.pallas_call(
        functools.partial(_filt_mlp_kernel, cb_rows=cb_rows),
        grid=(nblk,),
        in_specs=[
            pl.BlockSpec((Hd, FILTER_EMB), const),
            pl.BlockSpec((Hd, 1), const),
            pl.BlockSpec((Hd, Hd), const),
            pl.BlockSpec((Hd, 1), const),
            pl.BlockSpec((Hd, Hd), const),
            pl.BlockSpec((Hd, 1), const),
            pl.BlockSpec((Hd, 3), const),
            pl.BlockSpec((cb_rows, Hd), lambda i: (i, 0)),
            pl.BlockSpec((cb_rows, Hd), lambda i: (nblk + i, 0)),
        ],
        out_specs=pl.BlockSpec((2 * S // LANES, cb_rows, LANES), lambda i: (0, i, 0)),
        out_shape=jax.ShapeDtypeStruct((2 * S // LANES, R, LANES), BF16),
        scratch_shapes=[pltpu.VMEM((2, Hd, S), F32)],
        compiler_params=_cparams("arbitrary"),
        name="filt_mlp",
    )(w1t, b1, w2t, b2, w3t, b3, fr, w4t, w4t)


@functools.lru_cache(maxsize=None)
def _dft_constants(S):
    N = 2 * S
    N2 = LANES
    N1 = N // N2
    h = N1 // 2
    k1 = np.arange(N1)
    k2 = np.arange(N2)
    a1 = -2.0 * np.pi * np.outer(k1, k1) / N1
    f1r, f1i = np.cos(a1), np.sin(a1)
    at = -2.0 * np.pi * np.outer(k1, k2) / N
    twr, twi = np.cos(at), np.sin(at)
    a2 = -2.0 * np.pi * np.outer(k2, k2) / N2
    f2r, f2i = np.cos(a2), np.sin(a2)
    s1c = np.block([[f1r[:, :h], -f1i[:, :h]], [f1i[:, :h], f1r[:, :h]]])
    s1r = np.concatenate([f1r, f1i], axis=0)
    s2 = np.block([[f2r, f2i], [-f2i, f2r]])
    s2i = np.block([[f2r, -f2i], [f2i, f2r]])
    s1i = np.block([[f1r[:h, :], f1i[:h, :]], [-f1i[:h, :], f1r[:h, :]]])
    cast = lambda a: np.asarray(a, np.float32)
    return dict(s1c=cast(s1c), s1r=cast(s1r), s2=cast(s2), s2i=cast(s2i), s1i=cast(s1i),
                twr=cast(twr), twi=cast(twi), N1=N1, h=h)


def _bf(a):
    return jnp.asarray(a, F32).astype(BF16)


def _filt_fft_kernel(bias_ref, kern_ref, s1r_ref, s2_ref, twr_ref, twi_ref, out_ref,
                     src_s, slab_s, *, unroll):
    N1 = twr_ref.shape[0]
    n_ch = kern_ref.shape[1]
    pitch = src_s.shape[0] // N1
    scale = 1.0 / (N1 * LANES)
    twr = twr_ref[...]
    twi = twi_ref[...]
    s1r = s1r_ref[...]

    def load_in(t1, carry):
        src_s[pl.ds(pl.multiple_of(t1 * pitch, 8), n_ch), :] = kern_ref[t1].astype(F32)
        return carry
    lax.fori_loop(0, N1, load_in, 0, unroll=4)

    def per_group(j, carry):
        for u in range(0, unroll, 2):
            cs = (j * unroll + u, j * unroll + u + 1)
            z = jnp.concatenate([src_s[pl.ds(c, N1, stride=pitch), :] for c in cs], axis=1)
            a = jnp.dot(s1r, z.astype(BF16), preferred_element_type=F32)
            for i, c in enumerate(cs):
                ar = a[0:N1, i * LANES:(i + 1) * LANES]
                ai = a[N1:2 * N1, i * LANES:(i + 1) * LANES]
                r0 = pl.multiple_of(c * N1, N1)
                slab_s[pl.ds(r0, N1), 0:LANES] = (ar * twr - ai * twi).astype(BF16)
                slab_s[pl.ds(r0, N1), LANES:2 * LANES] = (ar * twi + ai * twr).astype(BF16)
        return carry

    lax.fori_loop(0, n_ch // unroll, per_group, 0)
    x = jnp.dot(slab_s[...], s2_ref[...], preferred_element_type=F32)
    bias = bias_ref[...]
    out_ref[:, :, 0:LANES] = ((x[:, 0:LANES].reshape(n_ch, N1, LANES) + bias) * scale).astype(BF16)
    out_ref[:, :, LANES:2 * LANES] = (x[:, LANES:2 * LANES].reshape(n_ch, N1, LANES)
                                      * scale).astype(BF16)


def _filt_fft(bias3, kern, consts, n_ch=32, unroll=8):
    N1, R, _ = kern.shape
    const = lambda i: (0, 0)
    pitch = n_ch + 8
    return pl.pallas_call(
        functools.partial(_filt_fft_kernel, unroll=unroll),
        grid=(R // n_ch,),
        in_specs=[
            pl.BlockSpec((n_ch, 1, 1), lambda i: (i, 0, 0)),
            pl.BlockSpec((N1, n_ch, LANES), lambda i: (0, i, 0)),
            pl.BlockSpec((2 * N1, N1), const),
            pl.BlockSpec((2 * LANES, 2 * LANES), const),
            pl.BlockSpec((N1, LANES), const),
            pl.BlockSpec((N1, LANES), const),
        ],
        out_specs=pl.BlockSpec((n_ch, N1, 2 * LANES), lambda i: (i, 0, 0)),
        out_shape=jax.ShapeDtypeStruct((R, N1, 2 * LANES), BF16),
        scratch_shapes=[pltpu.VMEM((N1 * pitch, LANES), F32),
                        pltpu.VMEM((n_ch * N1, 2 * LANES), BF16)],
        compiler_params=_cparams("parallel"),
        name="filt_fft",
    )(bias3, kern, _bf(consts["s1r"]), _bf(consts["s2"]),
      jnp.asarray(consts["twr"]), jnp.asarray(consts["twi"]))


def _hyena_kernel(nw_ref, v_ref, x1_ref, x2_ref, khat_ref, s1c_ref, s2_ref, s2i_ref, s1i_ref,
                  twr_ref, twi_ref, out_ref, src_s, slab_s, yf_s, q_s, y_s, *, unroll):
    h = v_ref.shape[1]
    n_ch = v_ref.shape[2]
    N1 = 2 * h
    twr = twr_ref[...]
    twi = twi_ref[...]
    s1c = s1c_ref[...]
    s1i = s1i_ref[...]
    n_groups = n_ch // unroll
    rows = unroll * N1
    pitch = src_s.shape[0] // (2 * h)

    def tile_rows(i):
        return pl.ds(pl.multiple_of(i * pitch, 8), n_ch)

    def chan_rows(c, bb):
        return pl.ds(bb * h * pitch + c, h, stride=pitch)

    def long_conv(order):
        def stage_a(g):
            for u in range(0, unroll, 2):
                cs = (g * unroll + u, g * unroll + u + 1)
                z = jnp.concatenate(
                    [jnp.concatenate([src_s[chan_rows(c, 0), :], src_s[chan_rows(c, 1), :]], axis=0)
                     for c in cs], axis=1)
                a = jnp.dot(s1c, z.astype(BF16), preferred_element_type=F32)
                for i, c in enumerate(cs):
                    ar = a[0:N1, i * LANES:(i + 1) * LANES]
                    ai = a[N1:2 * N1, i * LANES:(i + 1) * LANES]
                    r0 = c * N1 if isinstance(c, int) else pl.multiple_of(c * N1, N1)
                    slab_s[pl.ds(r0, N1), 0:LANES] = (ar * twr - ai * twi).astype(BF16)
                    slab_s[pl.ds(r0, N1), LANES:2 * LANES] = (ar * twi + ai * twr).astype(BF16)

        def stage_m1(g):
            r0 = g * rows if isinstance(g, int) else pl.multiple_of(g * rows, rows)
            c0 = g * unroll if isinstance(g, int) else pl.multiple_of(g * unroll, unroll)
            x = jnp.dot(slab_s[pl.ds(r0, rows), :], s2_ref[...], preferred_element_type=F32)
            xr = x[:, 0:LANES].reshape(unroll, N1, LANES)
            xi = x[:, LANES:2 * LANES].reshape(unroll, N1, LANES)
            kr = khat_ref[order, pl.ds(c0, unroll), :, 0:LANES].astype(F32)
            ki = khat_ref[order, pl.ds(c0, unroll), :, LANES:2 * LANES].astype(F32)
            yf_s[pl.ds(r0, rows), 0:LANES] = (xr * kr - xi * ki).reshape(rows, LANES).astype(BF16)
            yf_s[pl.ds(r0, rows), LANES:2 * LANES] = (xr * ki + xi * kr).reshape(rows, LANES).astype(BF16)

        def stage_m2(g):
            r0 = g * rows if isinstance(g, int) else pl.multiple_of(g * rows, rows)
            c0 = g * unroll if isinstance(g, int) else pl.multiple_of(g * unroll, unroll)
            p = jnp.dot(yf_s[pl.ds(r0, rows), :], s2i_ref[...], preferred_element_type=F32)
            pr = p[:, 0:LANES].reshape(unroll, N1, LANES)
            pi = p[:, LANES:2 * LANES].reshape(unroll, N1, LANES)
            q_s[pl.ds(c0, unroll), 0:N1, :] = (pr * twr + pi * twi).astype(BF16)
            q_s[pl.ds(c0, unroll), N1:2 * N1, :] = (pi * twr - pr * twi).astype(BF16)

        def stage_i(g):
            for u in range(0, unroll, 2):
                cs = (g * unroll + u, g * unroll + u + 1)
                q2 = jnp.concatenate([q_s[c] for c in cs], axis=1)
                y = jnp.dot(s1i, q2, preferred_element_type=F32)
                for i, c in enumerate(cs):
                    y_s[chan_rows(c, 0), :] = y[0:h, i * LANES:(i + 1) * LANES]
                    y_s[chan_rows(c, 1), :] = y[h:2 * h, i * LANES:(i + 1) * LANES]

        G = n_groups
        stage_a(0)
        stage_m1(0)
        stage_a(1)
        stage_m2(0)
        stage_m1(1)
        stage_a(2)

        def steady(i, carry):
            stage_i(i)
            stage_m2(i + 1)
            stage_m1(i + 2)
            stage_a(i + 3)
            return carry
        lax.fori_loop(0, G - 3, steady, 0)
        stage_i(G - 3)
        stage_m2(G - 2)
        stage_m1(G - 1)
        stage_i(G - 2)
        stage_m2(G - 1)
        stage_i(G - 1)

    def per_tile(fn):
        for bb in range(2):
            def body(t1, carry, bb=bb):
                fn(bb, t1, tile_rows(bb * h + t1))
                return carry
            lax.fori_loop(0, h, body, 0, unroll=8)

    def load_in(bb, t1, rws):
        src_s[rws, :] = v_ref[bb, t1].astype(F32)
    per_tile(load_in)
    long_conv(0)

    def gate1(bb, t1, rws):
        src_s[rws, :] = y_s[rws, :] * x1_ref[bb, t1].astype(F32)
    per_tile(gate1)
    long_conv(1)

    nw = jnp.broadcast_to(nw_ref[...], (n_ch, LANES))

    def gate2_norm(bb, t1, rws):
        z = y_s[rws, :] * x2_ref[bb, t1].astype(F32)
        ms = jnp.mean(z * z, axis=0, keepdims=True)
        out_ref[bb, t1] = (z * lax.rsqrt(ms + EPS) * nw).astype(BF16)
    per_tile(gate2_norm)


def _hyena(nw_col, hy4, khat, consts, unroll=8):
    B, h, _, _ = hy4.shape
    C = H_WIDTH
    n_ch = GROUP_W
    G = C // n_ch
    g0 = M_WIDTH // n_ch
    N1 = consts["N1"]
    const = lambda g, p: (0, 0)
    blk = (2, h, n_ch, LANES)
    pitch = n_ch + 8
    return pl.pallas_call(
        functools.partial(_hyena_kernel, unroll=unroll),
        grid=(G, B // 2),
        in_specs=[
            pl.BlockSpec((n_ch, 1), lambda g, p: (g, 0)),
            pl.BlockSpec(blk, lambda g, p: (p, 0, g0 + g, 0)),
            pl.BlockSpec(blk, lambda g, p: (p, 0, g0 + G + g, 0)),
            pl.BlockSpec(blk, lambda g, p: (p, 0, g0 + 2 * G + g, 0)),
            pl.BlockSpec((2, n_ch, N1, 2 * LANES), lambda g, p: (0, g, 0, 0)),
            pl.BlockSpec((2 * N1, 2 * h), const),
            pl.BlockSpec((2 * LANES, 2 * LANES), const),
            pl.BlockSpec((2 * LANES, 2 * LANES), const),
            pl.BlockSpec((2 * h, 2 * N1), const),
            pl.BlockSpec((N1, LANES), const),
            pl.BlockSpec((N1, LANES), const),
        ],
        out_specs=pl.BlockSpec(blk, lambda g, p: (p, 0, g, 0)),
        out_shape=jax.ShapeDtypeStruct((B, h, C, LANES), BF16),
        scratch_shapes=[
            pltpu.VMEM((2 * h * pitch, LANES), F32),
            pltpu.VMEM((n_ch * N1, 2 * LANES), BF16),
            pltpu.VMEM((n_ch * N1, 2 * LANES), BF16),
            pltpu.VMEM((n_ch, 2 * N1, LANES), BF16),
            pltpu.VMEM((2 * h * pitch, LANES), F32),
        ],
        compiler_params=_cparams("parallel", "arbitrary"),
        name="hyena",
    )(nw_col, hy4, hy4, hy4, khat,
      _bf(consts["s1c"]), _bf(consts["s2"]), _bf(consts["s2i"]), _bf(consts["s1i"]),
      jnp.asarray(consts["twr"]), jnp.asarray(consts["twi"]))


def _outmlp_kernel(x_ref, hm_ref, og_ref, yh_ref, nwm_ref, wo_ref, w1_ref, w2_ref, n_post_ref,
                   n_pre_ref, n_post2_ref, out_ref):
    n_t = hm_ref.shape[1]
    nwm = jnp.broadcast_to(nwm_ref[...], (M_WIDTH, LANES))
    tiles = []
    for j in range(n_t):
        hg = hm_ref[0, j].astype(F32) * og_ref[0, j].astype(F32)
        heads = []
        for hd in range(M_HEADS):
            hh = hg[hd * HEAD_DIM:(hd + 1) * HEAD_DIM]
            ms = jnp.mean(hh * hh, axis=0, keepdims=True)
            heads.append(hh * lax.rsqrt(ms + EPS))
        ym = jnp.concatenate(heads, axis=0) * nwm
        yt = jnp.concatenate([ym, yh_ref[0, j].astype(F32)], axis=0)
        tiles.append(yt.T.astype(BF16))
    y = tiles[0] if n_t == 1 else jnp.concatenate(tiles, axis=0)
    mix = jnp.dot(y, wo_ref[...], preferred_element_type=F32)
    x1 = x_ref[0] + _rms_rows(mix, n_post_ref[...])
    hm = _rms_rows(x1, n_pre_ref[...]).astype(BF16)
    mid = jnp.maximum(jnp.dot(hm, w1_ref[...], preferred_element_type=F32), 0.0)
    mid = (mid * mid).astype(BF16)
    ff = jnp.dot(mid, w2_ref[...], preferred_element_type=F32)
    out_ref[0] = x1 + _rms_rows(ff, n_post2_ref[...])


def _outmlp(x, hm, mt, yh, nwm_col, wo, w1, w2, n_post, n_pre, n_post2, tm_rows):
    B, S, D = x.shape
    n_t = tm_rows // LANES
    og0 = mt.shape[2] // M_WIDTH - 1
    const = lambda b, i: (0, 0)
    resident = functools.partial(pl.BlockSpec, index_map=const, pipeline_mode=pl.Buffered(1))
    return pl.pallas_call(
        _outmlp_kernel,
        grid=(B, S // tm_rows),
        in_specs=[
            pl.BlockSpec((1, tm_rows, D), lambda b, i: (b, i, 0)),
            pl.BlockSpec((1, n_t, M_WIDTH, LANES), lambda b, i: (b, i, 0, 0)),
            pl.BlockSpec((1, n_t, M_WIDTH, LANES), lambda b, i: (b, i, og0, 0)),
            pl.BlockSpec((1, n_t, H_WIDTH, LANES), lambda b, i: (b, i, 0, 0)),
            pl.BlockSpec((M_WIDTH, 1), const),
            resident((D, D)),
            resident((D, D_FF)),
            resident((D_FF, D)),
            pl.BlockSpec((1, D), const),
            pl.BlockSpec((1, D), const),
            pl.BlockSpec((1, D), const),
        ],
        out_specs=pl.BlockSpec((1, tm_rows, D), lambda b, i: (b, i, 0)),
        out_shape=jax.ShapeDtypeStruct((B, S, D), F32),
        compiler_params=_cparams("parallel", "parallel"),
        name="outmlp",
    )(x, hm, mt, yh, nwm_col, wo, w1, w2, n_post, n_pre, n_post2)


def kernel(x, norm_mix_pre, norm_mix_post, norm_mlp_pre, norm_mlp_post, w_in, b_gates,
           conv_w, conv_b, mlstm_norm_w, hyena_norm_w, filt_w1, filt_b1, filt_w2, filt_b2,
           filt_w3, filt_b3, filt_w4, filt_freq, filt_bias, w_out, w_mlp_in, w_mlp_out):
    B, S, D = x.shape
    assert D == D_MODEL and B % 2 == 0 and S % CHUNK == 0
    H = M_HEADS
    row = lambda a: a.astype(F32).reshape(1, -1)
    col = lambda a: a.astype(F32).reshape(-1, 1)
    tm_rows = min(1024, S)

    n_conv = 2 * M_WIDTH + 3 * H_WIDTH
    o_gate = n_conv + 2 * M_WIDTH
    assert w_in.shape == (D, o_gate + N_GATE)
    cw = conv_w.astype(F32)
    cbias = conv_b.astype(F32)
    ident = jnp.zeros((3, 2 * M_WIDTH), F32).at[1].set(1.0)
    cw_cm = jnp.concatenate([cw[:, 0:M_WIDTH], cw[:, 2 * M_WIDTH:n_conv], ident], axis=1)
    cb_cm = jnp.concatenate([cbias[0:M_WIDTH], cbias[2 * M_WIDTH:n_conv],
                             jnp.zeros((2 * M_WIDTH,), F32)]).reshape(1, -1)
    mt, k_tm, gcm = _proj(x, row(norm_mix_pre), w_in.astype(F32).T, col(b_gates), cw_cm, cb_cm,
                          cw[:, M_WIDTH:2 * M_WIDTH], row(cbias[M_WIDTH:2 * M_WIDTH]), tm_rows)

    g4 = gcm.reshape(B, 4, H, S).transpose(0, 2, 1, 3)
    g8 = jnp.concatenate([g4, jnp.zeros_like(g4)], axis=2)
    h_m = _mlstm(mt, k_tm, g8)

    consts = _dft_constants(S)
    kern = _filt_mlp(filt_w1.astype(F32).T, col(filt_b1), filt_w2.astype(F32).T, col(filt_b2),
                     filt_w3.astype(F32).T, col(filt_b3), filt_freq.astype(F32).T,
                     filt_w4.astype(F32).T, S)
    khat = _filt_fft(filt_bias.astype(F32).reshape(2 * H_WIDTH, 1, 1), kern, consts)
    khat = khat.reshape(2, H_WIDTH, consts["N1"], 2 * LANES)

    y_h = _hyena(col(hyena_norm_w), mt, khat, consts)

    return _outmlp(x, h_m, mt, y_h, col(mlstm_norm_w),
                   w_out.astype(BF16), w_mlp_in.astype(BF16), w_mlp_out.astype(BF16),
                   row(norm_mix_post), row(norm_mlp_pre), row(norm_mlp_post), min(512, S))
```

```python
import functools
import math

import numpy as np
import jax
import jax.numpy as jnp
from jax import lax
from jax.experimental import pallas as pl
from jax.experimental.pallas import tpu as pltpu

F32 = jnp.float32
BF16 = jnp.bfloat16

D_MODEL = 1024
M_WIDTH = 512
M_HEADS = 4
HEAD_DIM = 128
H_WIDTH = 512
H_GROUPS = 8
GROUP_W = H_WIDTH // H_GROUPS
CHUNK = 128
FILTER_EMB = 33
FILTER_HIDDEN = 64
DECAY_TARGET = 1e-2
FAST_DECAY_PCT = 0.3
SLOW_DECAY_PCT = 1.5
D_FF = 4 * D_MODEL
N_GATE = 16
EPS = 1e-6
LANES = 128
BF16_ROWS = 16
NEG_BIG = -1e30
VMEM_LIMIT = 56 * 1024 * 1024


def _cparams(*sem):
    return pltpu.CompilerParams(dimension_semantics=sem, vmem_limit_bytes=VMEM_LIMIT)


def _rms_rows(xf, w):
    ms = jnp.mean(xf * xf, axis=-1, keepdims=True)
    return xf * lax.rsqrt(ms + EPS) * w


def _sigmoid(x):
    return 1.0 / (1.0 + jnp.exp(-x))


def _log_sigmoid(x):
    return jnp.minimum(x, 0.0) - jnp.log(1.0 + jnp.exp(-jnp.abs(x)))


def _proj_kernel(x_ref, xp_ref, xn_ref, nw_ref, wt_ref, bgt_ref, cw_ref, cb_ref, cwk_ref, cbk_ref,
                 cm_ref, k_ref, gcm_ref, wbf_s, hall_s, pc_s, *, col_blk):
    TM = x_ref.shape[1]
    HALO = xp_ref.shape[1]
    j = pl.program_id(1)

    @pl.when(jnp.logical_and(pl.program_id(0) == 0, j == 0))
    def _():
        for c0 in range(0, wbf_s.shape[1], col_blk):
            wbf_s[:, c0:c0 + col_blk] = wt_ref[c0:c0 + col_blk, :].T.astype(BF16)

    nw = nw_ref[...]
    hn = _rms_rows(x_ref[0], nw).astype(BF16)
    hall_s[0:HALO, :] = _rms_rows(xp_ref[0], nw).astype(BF16)
    hall_s[HALO:HALO + TM, :] = hn
    hall_s[HALO + TM:2 * HALO + TM, :] = _rms_rows(xn_ref[0], nw).astype(BF16)
    keep_p = jnp.where(j > 0, 1.0, 0.0)
    keep_n = jnp.where(j < pl.num_programs(1) - 1, 1.0, 0.0)
    n_cm = cw_ref.shape[1]
    n_blk = n_cm // col_blk
    w_k0 = M_WIDTH
    w_gate = n_cm + M_WIDTH
    nt_dims = (((1,), (1,)), ((), ()))
    gt = lax.dot_general(wt_ref[w_gate:w_gate + N_GATE, :].astype(BF16), hn, nt_dims,
                         preferred_element_type=F32) + bgt_ref[...]
    row = lax.broadcasted_iota(jnp.int32, gt.shape, 0)
    gcm_ref[0] = jnp.where((row % 8) >= 4, _log_sigmoid(gt), gt)

    n_rb = TM // LANES
    n_sl = col_blk // LANES

    def project(w_row0, slot):
        wblk = wbf_s[:, pl.ds(w_row0, col_blk)]
        res = jnp.dot(hall_s[...], wblk, preferred_element_type=F32)
        for u in range(n_sl):
            ls = slice(u * LANES, (u + 1) * LANES)
            pc_s[slot, u, 0:HALO, :] = res[0:HALO, ls] * keep_p
            pc_s[slot, u, HALO:HALO + TM, :] = res[HALO:HALO + TM, ls]
            pc_s[slot, u, HALO + TM:2 * HALO + TM, :] = res[HALO + TM:2 * HALO + TM, ls] * keep_n

    def conv_tile(slot, u, rb, taps, bias):
        r = HALO + rb * LANES
        return (pc_s[slot, u, pl.ds(r - 1, LANES, stride=1), :] * taps[0:1]
                + pc_s[slot, u, r:r + LANES, :] * taps[1:2]
                + pc_s[slot, u, pl.ds(r + 1, LANES, stride=1), :] * taps[2:3] + bias)

    n_q = M_WIDTH // col_blk
    per_trip = 4
    assert n_blk % per_trip == 0

    def w_row_of(c):
        row_blk = jnp.where(c < n_q, c, c + n_q)
        return jnp.where(c == n_blk, w_k0 // col_blk, row_blk) * col_blk

    def emit(c, slot):
        for u in range(n_sl):
            col0 = pl.multiple_of(c * col_blk + u * LANES, LANES)
            taps = cw_ref[:, pl.ds(col0, LANES)]
            bias = cb_ref[:, pl.ds(col0, LANES)]
            for rb in range(n_rb):
                cm_ref[0, rb, pl.ds(col0, LANES), :] = conv_tile(slot, u, rb, taps, bias).T.astype(BF16)

    def body(i, carry):
        for u in range(per_trip):
            c = per_trip * i + u
            emit(c, u % 2)
            project(pl.multiple_of(w_row_of(c + 1), col_blk), (u + 1) % 2)
        return carry

    project(0, 0)
    lax.fori_loop(0, n_blk // per_trip, body, 0)

    k_scale = HEAD_DIM ** -0.5
    n_k = M_WIDTH // col_blk
    for kb in range(n_k):
        if kb > 0:
            project(w_k0 + kb * col_blk, kb % 2)
        for u in range(n_sl):
            cs = slice(kb * col_blk + u * LANES, kb * col_blk + (u + 1) * LANES)
            for rb in range(n_rb):
                cv = conv_tile(kb % 2, u, rb, cwk_ref[:, cs], cbk_ref[:, cs])
                k_ref[0, rb * LANES:(rb + 1) * LANES, cs] = (cv * _sigmoid(cv) * k_scale).astype(BF16)


def _proj(x, nw, w_in_t, bgt, cw_cm, cb_cm, cw_k, cb_k, tm_rows, col_blk=256):
    B, S, D = x.shape
    HALO = BF16_ROWS
    n_cm = cw_cm.shape[1]
    n_t = tm_rows // LANES
    hb = tm_rows // HALO
    last = S // HALO - 1
    const = lambda b, j: (0, 0)
    resident = functools.partial(pl.BlockSpec, index_map=const, pipeline_mode=pl.Buffered(1))
    return pl.pallas_call(
        functools.partial(_proj_kernel, col_blk=col_blk),
        grid=(B, S // tm_rows),
        in_specs=[
            pl.BlockSpec((1, tm_rows, D), lambda b, j: (b, j, 0)),
            pl.BlockSpec((1, HALO, D), lambda b, j: (b, jnp.maximum(j * hb - 1, 0), 0)),
            pl.BlockSpec((1, HALO, D), lambda b, j: (b, jnp.minimum((j + 1) * hb, last), 0)),
            pl.BlockSpec((1, D), const),
            resident(w_in_t.shape),
            pl.BlockSpec((N_GATE, 1), const),
            pl.BlockSpec((3, n_cm), const),
            pl.BlockSpec((1, n_cm), const),
            pl.BlockSpec((3, M_WIDTH), const),
            pl.BlockSpec((1, M_WIDTH), const),
        ],
        out_specs=[
            pl.BlockSpec((1, n_t, n_cm, LANES), lambda b, j: (b, j, 0, 0)),
            pl.BlockSpec((1, tm_rows, M_WIDTH), lambda b, j: (b, j, 0)),
            pl.BlockSpec((1, N_GATE, tm_rows), lambda b, j: (b, 0, j)),
        ],
        out_shape=[
            jax.ShapeDtypeStruct((B, S // LANES, n_cm, LANES), BF16),
            jax.ShapeDtypeStruct((B, S, M_WIDTH), BF16),
            jax.ShapeDtypeStruct((B, N_GATE, S), F32),
        ],
        scratch_shapes=[pltpu.VMEM((D, n_cm + M_WIDTH), BF16),
                        pltpu.VMEM((tm_rows + 2 * HALO, D), BF16),
                        pltpu.VMEM((2, col_blk // LANES, tm_rows + 2 * HALO, LANES), F32)],
        compiler_params=_cparams("arbitrary", "arbitrary"),
        name="proj",
    )(x, x, x, nw, w_in_t, bgt, cw_cm, cb_cm, cw_k, cb_k)


def _mlstm_kernel(qt_ref, vt_ref, k_ref, g_ref, out_ref, q_s, va_s, rows_s, h_s, c_s):
    S = k_ref.shape[1]
    HB = g_ref.shape[1]
    L = CHUNK
    NC = S // L
    D = HEAD_DIM
    DA = D + BF16_ROWS
    W = HB * D

    for hd in range(HB):
        g = g_ref[0, hd]
        lane = lax.broadcasted_iota(jnp.int32, g.shape, 1) % L
        pre = g
        suf = g
        d = 1
        while d < L:
            pre = pre + jnp.where(lane >= d, pltpu.roll(pre, d, 1), 0.0)
            suf = suf + jnp.where(lane < L - d, pltpu.roll(suf, S - d, 1), 0.0)
            d *= 2
        rows_s[hd, 0:1, :] = g[0:1, :]
        rows_s[hd, 1:2, :] = pre[1:2, :]
        rows_s[hd, 2:3, :] = g[2:3, :]
        rows_s[hd, 3:4, :] = suf[3:4, :]

    ones_row = (lax.broadcasted_iota(jnp.int32, (BF16_ROWS, L), 0) == 0).astype(BF16)

    def prep(c, carry):
        r0 = pl.multiple_of(c * L, L)
        h_s[:, pl.ds(r0, L)] = jnp.zeros((W, L), F32)
        qpre = qt_ref[0, c].astype(F32)
        q_s[c] = (qpre * _sigmoid(qpre)).astype(BF16)
        for hd in range(HB):
            va_s[hd, 0:D, pl.ds(r0, L)] = vt_ref[0, c, hd * D:(hd + 1) * D, :]
            va_s[hd, D:DA, pl.ds(r0, L)] = ones_row
        return carry

    lax.fori_loop(0, NC, prep, 0)
    c_s[...] = jnp.zeros_like(c_s)

    si = lax.broadcasted_iota(jnp.int32, (L, L), 0)
    ti = lax.broadcasted_iota(jnp.int32, (L, L), 1)
    eye = si == ti
    dirs = ((0, 1, si <= ti, L - 1), (2, 3, si >= ti, 0))

    def chain(hd, dr, c, m):
        li_row, b_row, mask, g_lane = dirs[dr]
        r0 = pl.multiple_of(c * L, L)
        r_li = rows_s[hd, li_row:li_row + 1, pl.ds(r0, L)]
        r_b = rows_s[hd, b_row:b_row + 1, pl.ds(r0, L)]
        g_tot = r_b[:, g_lane:g_lane + 1]
        r_a = g_tot - r_b + r_li
        m_new = jnp.maximum(g_tot + m, jnp.max(r_a, axis=1, keepdims=True))
        decay = jnp.exp(g_tot + m - m_new)
        r_w = jnp.exp(r_a - m_new)
        a_col = jnp.sum(jnp.where(eye, r_li - r_b, 0.0), axis=1, keepdims=True)
        dmat = jnp.where(mask, a_col + r_b, NEG_BIG)
        inter_log = r_b + m
        m_t = jnp.maximum(inter_log, jnp.max(dmat, axis=0, keepdims=True))
        wts = jnp.exp(dmat - m_t)
        kc = k_ref[0, pl.ds(r0, L), hd * D:(hd + 1) * D]
        qtc = q_s[c, hd * D:(hd + 1) * D, :]
        vac = va_s[hd, :, pl.ds(r0, L)]
        cmat = c_s[hd, dr]
        both = jnp.dot(jnp.concatenate([kc, cmat.astype(BF16)], axis=0), qtc,
                       preferred_element_type=F32)
        s_w = (both[0:L] * wts).astype(BF16)
        tot = (jnp.dot(vac, s_w, preferred_element_type=F32)
               + jnp.exp(inter_log - m_t) * both[L:L + DA])
        den = jnp.maximum(jnp.abs(tot[D:D + 1]), jnp.exp(-m_t))
        h = tot[0:D] * (1.0 / den)
        h_s[hd * D:(hd + 1) * D, pl.ds(r0, L)] = h_s[hd * D:(hd + 1) * D, pl.ds(r0, L)] + h
        u = (vac.astype(F32) * r_w).astype(BF16)
        c_s[hd, dr] = decay * cmat + jnp.dot(u, kc, preferred_element_type=F32)
        return m_new

    def step(i, ms):
        out = []
        for hd in range(HB):
            out.append(chain(hd, 0, i, ms[2 * hd]))
            out.append(chain(hd, 1, NC - 1 - i, ms[2 * hd + 1]))
        return tuple(out)

    lax.fori_loop(0, NC, step, tuple(jnp.zeros((1, 1), F32) for _ in range(2 * HB)))

    def fin(c, carry):
        out_ref[0, c] = h_s[:, pl.ds(pl.multiple_of(c * L, L), L)].astype(BF16)
        return carry

    lax.fori_loop(0, NC, fin, 0, unroll=4)


def _mlstm(mt, k_tm, gcm, heads_per_step=4):
    B, S, _ = k_tm.shape
    NC = S // CHUNK
    D = HEAD_DIM
    HB = heads_per_step
    v0 = (M_WIDTH + 3 * H_WIDTH) // (HB * D)
    W = HB * D
    NB = M_HEADS // HB
    DA = D + BF16_ROWS
    cm_blk = (1, NC, W, LANES)
    return pl.pallas_call(
        _mlstm_kernel,
        grid=(B, NB),
        in_specs=[
            pl.BlockSpec(cm_blk, lambda b, h: (b, 0, h, 0)),
            pl.BlockSpec(cm_blk, lambda b, h: (b, 0, v0 + h, 0)),
            pl.BlockSpec((1, S, W), lambda b, h: (b, 0, h)),
            pl.BlockSpec((1, HB, 8, S), lambda b, h: (b, h, 0, 0)),
        ],
        out_specs=pl.BlockSpec(cm_blk, lambda b, h: (b, 0, h, 0)),
        out_shape=jax.ShapeDtypeStruct((B, NC, M_WIDTH, LANES), BF16),
        scratch_shapes=[
            pltpu.VMEM((NC, W, LANES), BF16),
            pltpu.VMEM((HB, DA, S), BF16),
            pltpu.VMEM((HB, 8, S), F32),
            pltpu.VMEM((W, S), F32),
            pltpu.VMEM((HB, 2, DA, D), F32),
        ],
        compiler_params=_cparams("parallel", "parallel"),
        name="mlstm",
    )(mt, mt, k_tm, gcm)


def _filt_mlp_kernel(w1t_ref, b1_ref, w2t_ref, b2_ref, w3t_ref, b3_ref, fr_ref, w4f_ref, w4b_ref,
                     out_ref, h3_s, *, cb_rows):
    S = h3_s.shape[2]
    hi = lax.Precision.HIGHEST
    lane = lax.broadcasted_iota(jnp.int32, (1, S), 1)
    pos_f = lane.astype(F32)
    pos_b = (S - lane).astype(F32)

    def features(pos):
        bands = (FILTER_EMB - 1) // 2
        t = pos / (S - 1)
        ang = (2.0 * math.pi) * pos / S
        fidx = lax.broadcasted_iota(jnp.int32, (bands, 1), 0).astype(F32)
        f = 1e-4 + fidx * ((bands - 1 - 1e-4) / (bands - 1))
        fa = f * ang
        w1t = w1t_ref[...]
        pre = (w1t[:, 0:1] * t
               + jnp.dot(w1t[:, 1:1 + bands], jnp.cos(fa), precision=hi, preferred_element_type=F32)
               - jnp.dot(w1t[:, 1 + bands:], jnp.sin(fa), precision=hi, preferred_element_type=F32))
        fr = fr_ref[...]
        h = jnp.sin(fr[:, 0:1] * (pre + b1_ref[...]))
        h = jnp.sin(fr[:, 1:2] * (jnp.dot(w2t_ref[...], h, precision=hi, preferred_element_type=F32)
                                  + b2_ref[...]))
        return jnp.sin(fr[:, 2:3] * (jnp.dot(w3t_ref[...], h, precision=hi, preferred_element_type=F32)
                                     + b3_ref[...]))

    @pl.when(pl.program_id(0) == 0)
    def _():
        h3 = features(pos_f)
        h3_s[0] = h3
        n_t = S // LANES
        anti = (lax.broadcasted_iota(jnp.int32, (LANES, LANES), 0)
                + lax.broadcasted_iota(jnp.int32, (LANES, LANES), 1) == LANES - 1).astype(F32)
        rev = jnp.concatenate(
            [jnp.dot(h3[:, (n_t - 1 - u) * LANES:(n_t - u) * LANES], anti, precision=hi,
                     preferred_element_type=F32) for u in range(n_t)], axis=1)
        h3_s[1] = pltpu.roll(rev, 1, 1)

    r = pl.program_id(0) * cb_rows + lax.broadcasted_iota(jnp.int32, (cb_rows, 1), 0)
    ch = (r % H_WIDTH).astype(F32)
    max_decay = math.log(DECAY_TARGET) / FAST_DECAY_PCT
    min_decay = math.log(DECAY_TARGET) / SLOW_DECAY_PCT
    delta = jnp.abs(min_decay + ch * ((max_decay - min_decay) / (H_WIDTH - 1)))
    n_t = S // LANES
    for half, (w_ref, pos) in enumerate(((w4f_ref, pos_f), (w4b_ref, pos_b))):
        filt = jnp.dot(w_ref[...].astype(BF16), h3_s[half].astype(BF16),
                       preferred_element_type=F32)
        filt = filt * jnp.exp(-(pos / (S - 1)) * delta)
        if half == 1:
            filt = jnp.where(lane == 0, 0.0, filt)
        for u in range(n_t):
            out_ref[half * n_t + u] = filt[:, u * LANES:(u + 1) * LANES].astype(BF16)


def _filt_mlp(w1t, b1, w2t, b2, w3t, b3, fr, w4t, S, cb_rows=256):
    R = w4t.shape[0] // 2
    Hd = FILTER_HIDDEN
    const = lambda i: (0, 0)
    nblk = R // cb_rows
    return pl.pallas_call(
        functools.partial(_filt_mlp_kernel, cb_rows=cb_rows),
        grid=(nblk,),
        in_specs=[
            pl.BlockSpec((Hd, FILTER_EMB), const),
            pl.BlockSpec((Hd, 1), const),
            pl.BlockSpec((Hd, Hd), const),
            pl.BlockSpec((Hd, 1), const),
            pl.BlockSpec((Hd, Hd), const),
            pl.BlockSpec((Hd, 1), const),
            pl.BlockSpec((Hd, 3), const),
            pl.BlockSpec((cb_rows, Hd), lambda i: (i, 0)),
            pl.BlockSpec((cb_rows, Hd), lambda i: (nblk + i, 0)),
        ],
        out_specs=pl.BlockSpec((2 * S // LANES, cb_rows, LANES), lambda i: (0, i, 0)),
        out_shape=jax.ShapeDtypeStruct((2 * S // LANES, R, LANES), BF16),
        scratch_shapes=[pltpu.VMEM((2, Hd, S), F32)],
        compiler_params=_cparams("arbitrary"),
        name="filt_mlp",
    )(w1t, b1, w2t, b2, w3t, b3, fr, w4t, w4t)


@functools.lru_cache(maxsize=None)
def _dft_constants(S):
    N = 2 * S
    N2 = LANES
    N1 = N // N2
    h = N1 // 2
    k1 = np.arange(N1)
    k2 = np.arange(N2)
    a1 = -2.0 * np.pi * np.outer(k1, k1) / N1
    f1r, f1i = np.cos(a1), np.sin(a1)
    at = -2.0 * np.pi * np.outer(k1, k2) / N
    twr, twi = np.cos(at), np.sin(at)
    a2 = -2.0 * np.pi * np.outer(k2, k2) / N2
    f2r, f2i = np.cos(a2), np.sin(a2)
    s1c = np.block([[f1r[:, :h], -f1i[:, :h]], [f1i[:, :h], f1r[:, :h]]])
    s1r = np.concatenate([f1r, f1i], axis=0)
    s2 = np.block([[f2r, f2i], [-f2i, f2r]])
    s2i = np.block([[f2r, -f2i], [f2i, f2r]])
    s1i = np.block([[f1r[:h, :], f1i[:h, :]], [-f1i[:h, :], f1r[:h, :]]])
    cast = lambda a: np.asarray(a, np.float32)
    return dict(s1c=cast(s1c), s1r=cast(s1r), s2=cast(s2), s2i=cast(s2i), s1i=cast(s1i),
                twr=cast(twr), twi=cast(twi), N1=N1, h=h)


def _bf(a):
    return jnp.asarray(a, F32).astype(BF16)


def _filt_fft_kernel(bias_ref, kern_ref, s1r_ref, s2_ref, twr_ref, twi_ref, out_ref,
                     src_s, slab_s, *, unroll):
    N1 = twr_ref.shape[0]
    n_ch = kern_ref.shape[1]
    pitch = src_s.shape[0] // N1
    scale = 1.0 / (N1 * LANES)
    twr = twr_ref[...]
    twi = twi_ref[...]
    s1r = s1r_ref[...]

    def load_in(t1, carry):
        src_s[pl.ds(pl.multiple_of(t1 * pitch, 8), n_ch), :] = kern_ref[t1].astype(F32)
        return carry
    lax.fori_loop(0, N1, load_in, 0, unroll=4)

    def per_group(j, carry):
        for u in range(0, unroll, 2):
            cs = (j * unroll + u, j * unroll + u + 1)
            z = jnp.concatenate([src_s[pl.ds(c, N1, stride=pitch), :] for c in cs], axis=1)
            a = jnp.dot(s1r, z.astype(BF16), preferred_element_type=F32)
            for i, c in enumerate(cs):
                ar = a[0:N1, i * LANES:(i + 1) * LANES]
                ai = a[N1:2 * N1, i * LANES:(i + 1) * LANES]
                r0 = pl.multiple_of(c * N1, N1)
                slab_s[pl.ds(r0, N1), 0:LANES] = (ar * twr - ai * twi).astype(BF16)
                slab_s[pl.ds(r0, N1), LANES:2 * LANES] = (ar * twi + ai * twr).astype(BF16)
        return carry

    lax.fori_loop(0, n_ch // unroll, per_group, 0)
    x = jnp.dot(slab_s[...], s2_ref[...], preferred_element_type=F32)
    bias = bias_ref[...]
    out_ref[:, :, 0:LANES] = ((x[:, 0:LANES].reshape(n_ch, N1, LANES) + bias) * scale).astype(BF16)
    out_ref[:, :, LANES:2 * LANES] = (x[:, LANES:2 * LANES].reshape(n_ch, N1, LANES)
                                      * scale).astype(BF16)


def _filt_fft(bias3, kern, consts, n_ch=32, unroll=8):
    N1, R, _ = kern.shape
    const = lambda i: (0, 0)
    pitch = n_ch + 8
    return pl.pallas_call(
        functools.partial(_filt_fft_kernel, unroll=unroll),
        grid=(R // n_ch,),
        in_specs=[
            pl.BlockSpec((n_ch, 1, 1), lambda i: (i, 0, 0)),
            pl.BlockSpec((N1, n_ch, LANES), lambda i: (0, i, 0)),
            pl.BlockSpec((2 * N1, N1), const),
            pl.BlockSpec((2 * LANES, 2 * LANES), const),
            pl.BlockSpec((N1, LANES), const),
            pl.BlockSpec((N1, LANES), const),
        ],
        out_specs=pl.BlockSpec((n_ch, N1, 2 * LANES), lambda i: (i, 0, 0)),
        out_shape=jax.ShapeDtypeStruct((R, N1, 2 * LANES), BF16),
        scratch_shapes=[pltpu.VMEM((N1 * pitch, LANES), F32),
                        pltpu.VMEM((n_ch * N1, 2 * LANES), BF16)],
        compiler_params=_cparams("parallel"),
        name="filt_fft",
    )(bias3, kern, _bf(consts["s1r"]), _bf(consts["s2"]),
      jnp.asarray(consts["twr"]), jnp.asarray(consts["twi"]))


def _hyena_kernel(nw_ref, v_ref, x1_ref, x2_ref, khat_ref, s1c_ref, s2_ref, s2i_ref, s1i_ref,
                  twr_ref, twi_ref, out_ref, src_s, slab_s, yf_s, q_s, y_s, *, unroll):
    h = v_ref.shape[1]
    n_ch = v_ref.shape[2]
    N1 = 2 * h
    twr = twr_ref[...]
    twi = twi_ref[...]
    s1c = s1c_ref[...]
    s1i = s1i_ref[...]
    n_groups = n_ch // unroll
    rows = unroll * N1
    pitch = src_s.shape[0] // (2 * h)

    def tile_rows(i):
        return pl.ds(pl.multiple_of(i * pitch, 8), n_ch)

    def chan_rows(c, bb):
        return pl.ds(bb * h * pitch + c, h, stride=pitch)

    def long_conv(order):
        def stage_a(g):
            for u in range(0, unroll, 2):
                cs = (g * unroll + u, g * unroll + u + 1)
                z = jnp.concatenate(
                    [jnp.concatenate([src_s[chan_rows(c, 0), :], src_s[chan_rows(c, 1), :]], axis=0)
                     for c in cs], axis=1)
                a = jnp.dot(s1c, z.astype(BF16), preferred_element_type=F32)
                for i, c in enumerate(cs):
                    ar = a[0:N1, i * LANES:(i + 1) * LANES]
                    ai = a[N1:2 * N1, i * LANES:(i + 1) * LANES]
                    r0 = c * N1 if isinstance(c, int) else pl.multiple_of(c * N1, N1)
                    slab_s[pl.ds(r0, N1), 0:LANES] = (ar * twr - ai * twi).astype(BF16)
                    slab_s[pl.ds(r0, N1), LANES:2 * LANES] = (ar * twi + ai * twr).astype(BF16)

        def stage_m1(g):
            r0 = g * rows if isinstance(g, int) else pl.multiple_of(g * rows, rows)
            c0 = g * unroll if isinstance(g, int) else pl.multiple_of(g * unroll, unroll)
            x = jnp.dot(slab_s[pl.ds(r0, rows), :], s2_ref[...], preferred_element_type=F32)
            xr = x[:, 0:LANES].reshape(unroll, N1, LANES)
            xi = x[:, LANES:2 * LANES].reshape(unroll, N1, LANES)
            kr = khat_ref[order, pl.ds(c0, unroll), :, 0:LANES].astype(F32)
            ki = khat_ref[order, pl.ds(c0, unroll), :, LANES:2 * LANES].astype(F32)
            yf_s[pl.ds(r0, rows), 0:LANES] = (xr * kr - xi * ki).reshape(rows, LANES).astype(BF16)
            yf_s[pl.ds(r0, rows), LANES:2 * LANES] = (xr * ki + xi * kr).reshape(rows, LANES).astype(BF16)

        def stage_m2(g):
            r0 = g * rows if isinstance(g, int) else pl.multiple_of(g * rows, rows)
            c0 = g * unroll if isinstance(g, int) else pl.multiple_of(g * unroll, unroll)
            p = jnp.dot(yf_s[pl.ds(r0, rows), :], s2i_ref[...], preferred_element_type=F32)
            pr = p[:, 0:LANES].reshape(unroll, N1, LANES)
            pi = p[:, LANES:2 * LANES].reshape(unroll, N1, LANES)
            q_s[pl.ds(c0, unroll), 0:N1, :] = (pr * twr + pi * twi).astype(BF16)
            q_s[pl.ds(c0, unroll), N1:2 * N1, :] = (pi * twr - pr * twi).astype(BF16)

        def stage_i(g):
            for u in range(0, unroll, 2):
                cs = (g * unroll + u, g * unroll + u + 1)
                q2 = jnp.concatenate([q_s[c] for c in cs], axis=1)
                y = jnp.dot(s1i, q2, preferred_element_type=F32)
                for i, c in enumerate(cs):
                    y_s[chan_rows(c, 0), :] = y[0:h, i * LANES:(i + 1) * LANES]
                    y_s[chan_rows(c, 1), :] = y[h:2 * h, i * LANES:(i + 1) * LANES]

        G = n_groups
        stage_a(0)
        stage_m1(0)
        stage_a(1)
        stage_m2(0)
        stage_m1(1)
        stage_a(2)

        def steady(i, carry):
            stage_i(i)
            stage_m2(i + 1)
            stage_m1(i + 2)
            stage_a(i + 3)
            return carry
        lax.fori_loop(0, G - 3, steady, 0)
        stage_i(G - 3)
        stage_m2(G - 2)
        stage_m1(G - 1)
        stage_i(G - 2)
        stage_m2(G - 1)
        stage_i(G - 1)

    def per_tile(fn):
        for bb in range(2):
            def body(t1, carry, bb=bb):
                fn(bb, t1, tile_rows(bb * h + t1))
                return carry
            lax.fori_loop(0, h, body, 0, unroll=8)

    def load_in(bb, t1, rws):
        src_s[rws, :] = v_ref[bb, t1].astype(F32)
    per_tile(load_in)
    long_conv(0)

    def gate1(bb, t1, rws):
        src_s[rws, :] = y_s[rws, :] * x1_ref[bb, t1].astype(F32)
    per_tile(gate1)
    long_conv(1)

    nw = jnp.broadcast_to(nw_ref[...], (n_ch, LANES))

    def gate2_norm(bb, t1, rws):
        z = y_s[rws, :] * x2_ref[bb, t1].astype(F32)
        ms = jnp.mean(z * z, axis=0, keepdims=True)
        out_ref[bb, t1] = (z * lax.rsqrt(ms + EPS) * nw).astype(BF16)
    per_tile(gate2_norm)


def _hyena(nw_col, hy4, khat, consts, unroll=8):
    B, h, _, _ = hy4.shape
    C = H_WIDTH
    n_ch = GROUP_W
    G = C // n_ch
    g0 = M_WIDTH // n_ch
    N1 = consts["N1"]
    const = lambda g, p: (0, 0)
    blk = (2, h, n_ch, LANES)
    pitch = n_ch + 8
    return pl.pallas_call(
        functools.partial(_hyena_kernel, unroll=unroll),
        grid=(G, B // 2),
        in_specs=[
            pl.BlockSpec((n_ch, 1), lambda g, p: (g, 0)),
            pl.BlockSpec(blk, lambda g, p: (p, 0, g0 + g, 0)),
            pl.BlockSpec(blk, lambda g, p: (p, 0, g0 + G + g, 0)),
            pl.BlockSpec(blk, lambda g, p: (p, 0, g0 + 2 * G + g, 0)),
            pl.BlockSpec((2, n_ch, N1, 2 * LANES), lambda g, p: (0, g, 0, 0)),
            pl.BlockSpec((2 * N1, 2 * h), const),
            pl.BlockSpec((2 * LANES, 2 * LANES), const),
            pl.BlockSpec((2 * LANES, 2 * LANES), const),
            pl.BlockSpec((2 * h, 2 * N1), const),
            pl.BlockSpec((N1, LANES), const),
            pl.BlockSpec((N1, LANES), const),
        ],
        out_specs=pl.BlockSpec(blk, lambda g, p: (p, 0, g, 0)),
        out_shape=jax.ShapeDtypeStruct((B, h, C, LANES), BF16),
        scratch_shapes=[
            pltpu.VMEM((2 * h * pitch, LANES), F32),
            pltpu.VMEM((n_ch * N1, 2 * LANES), BF16),
            pltpu.VMEM((n_ch * N1, 2 * LANES), BF16),
            pltpu.VMEM((n_ch, 2 * N1, LANES), BF16),
            pltpu.VMEM((2 * h * pitch, LANES), F32),
        ],
        compiler_params=_cparams("parallel", "arbitrary"),
        name="hyena",
    )(nw_col, hy4, hy4, hy4, khat,
      _bf(consts["s1c"]), _bf(consts["s2"]), _bf(consts["s2i"]), _bf(consts["s1i"]),
      jnp.asarray(consts["twr"]), jnp.asarray(consts["twi"]))


def _outmlp_kernel(x_ref, hm_ref, og_ref, yh_ref, nwm_ref, wo_ref, w1_ref, w2_ref, n_post_ref,
                   n_pre_ref, n_post2_ref, out_ref):
    n_t = hm_ref.shape[1]
    nwm = jnp.broadcast_to(nwm_ref[...], (M_WIDTH, LANES))
    tiles = []
    for j in range(n_t):
        hg = hm_ref[0, j].astype(F32) * _sigmoid(og_ref[0, j].astype(F32))
        heads = []
        for hd in range(M_HEADS):
            hh = hg[hd * HEAD_DIM:(hd + 1) * HEAD_DIM]
            ms = jnp.mean(hh * hh, axis=0, keepdims=True)
            heads.append(hh * lax.rsqrt(ms + EPS))
        ym = jnp.concatenate(heads, axis=0) * nwm
        yt = jnp.concatenate([ym, yh_ref[0, j].astype(F32)], axis=0)
        tiles.append(yt.T.astype(BF16))
    y = tiles[0] if n_t == 1 else jnp.concatenate(tiles, axis=0)
    mix = jnp.dot(y, wo_ref[...], preferred_element_type=F32)
    x1 = x_ref[0] + _rms_rows(mix, n_post_ref[...])
    hm = _rms_rows(x1, n_pre_ref[...]).astype(BF16)
    mid = jnp.maximum(jnp.dot(hm, w1_ref[...], preferred_element_type=F32), 0.0)
    mid = (mid * mid).astype(BF16)
    ff = jnp.dot(mid, w2_ref[...], preferred_element_type=F32)
    out_ref[0] = x1 + _rms_rows(ff, n_post2_ref[...])


def _outmlp(x, hm, mt, yh, nwm_col, wo, w1, w2, n_post, n_pre, n_post2, tm_rows):
    B, S, D = x.shape
    n_t = tm_rows // LANES
    og0 = mt.shape[2] // M_WIDTH - 1
    const = lambda b, i: (0, 0)
    resident = functools.partial(pl.BlockSpec, index_map=const, pipeline_mode=pl.Buffered(1))
    return pl.pallas_call(
        _outmlp_kernel,
        grid=(B, S // tm_rows),
        in_specs=[
            pl.BlockSpec((1, tm_rows, D), lambda b, i: (b, i, 0)),
            pl.BlockSpec((1, n_t, M_WIDTH, LANES), lambda b, i: (b, i, 0, 0)),
            pl.BlockSpec((1, n_t, M_WIDTH, LANES), lambda b, i: (b, i, og0, 0)),
            pl.BlockSpec((1, n_t, H_WIDTH, LANES), lambda b, i: (b, i, 0, 0)),
            pl.BlockSpec((M_WIDTH, 1), const),
            resident((D, D)),
            resident((D, D_FF)),
            resident((D_FF, D)),
            pl.BlockSpec((1, D), const),
            pl.BlockSpec((1, D), const),
            pl.BlockSpec((1, D), const),
        ],
        out_specs=pl.BlockSpec((1, tm_rows, D), lambda b, i: (b, i, 0)),
        out_shape=jax.ShapeDtypeStruct((B, S, D), F32),
        compiler_params=_cparams("parallel", "parallel"),
        name="outmlp",
    )(x, hm, mt, yh, nwm_col, wo, w1, w2, n_post, n_pre, n_post2)


def kernel(x, norm_mix_pre, norm_mix_post, norm_mlp_pre, norm_mlp_post, w_in, b_gates,
           conv_w, conv_b, mlstm_norm_w, hyena_norm_w, filt_w1, filt_b1, filt_w2, filt_b2,
           filt_w3, filt_b3, filt_w4, filt_freq, filt_bias, w_out, w_mlp_in, w_mlp_out):
    B, S, D = x.shape
    assert D == D_MODEL and B % 2 == 0 and S % CHUNK == 0
    H = M_HEADS
    row = lambda a: a.astype(F32).reshape(1, -1)
    col = lambda a: a.astype(F32).reshape(-1, 1)
    tm_rows = min(1024, S)

    n_conv = 2 * M_WIDTH + 3 * H_WIDTH
    o_gate = n_conv + 2 * M_WIDTH
    assert w_in.shape == (D, o_gate + N_GATE)
    cw = conv_w.astype(F32)
    cbias = conv_b.astype(F32)
    ident = jnp.zeros((3, 2 * M_WIDTH), F32).at[1].set(1.0)
    cw_cm = jnp.concatenate([cw[:, 0:M_WIDTH], cw[:, 2 * M_WIDTH:n_conv], ident], axis=1)
    cb_cm = jnp.concatenate([cbias[0:M_WIDTH], cbias[2 * M_WIDTH:n_conv],
                             jnp.zeros((2 * M_WIDTH,), F32)]).reshape(1, -1)
    mt, k_tm, gcm = _proj(x, row(norm_mix_pre), w_in.astype(F32).T, col(b_gates), cw_cm, cb_cm,
                          cw[:, M_WIDTH:2 * M_WIDTH], row(cbias[M_WIDTH:2 * M_WIDTH]), tm_rows)

    g4 = gcm.reshape(B, 4, H, S).transpose(0, 2, 1, 3)
    g8 = jnp.concatenate([g4, jnp.zeros_like(g4)], axis=2)
    h_m = _mlstm(mt, k_tm, g8)

    consts = _dft_constants(S)
    kern = _filt_mlp(filt_w1.astype(F32).T, col(filt_b1), filt_w2.astype(F32).T, col(filt_b2),
                     filt_w3.astype(F32).T, col(filt_b3), filt_freq.astype(F32).T,
                     filt_w4.astype(F32).T, S)
    khat = _filt_fft(filt_bias.astype(F32).reshape(2 * H_WIDTH, 1, 1), kern, consts)
    khat = khat.reshape(2, H_WIDTH, consts["N1"], 2 * LANES)

    y_h = _hyena(col(hyena_norm_w), mt, khat, consts)

    return _outmlp(x, h_m, mt, y_h, col(mlstm_norm_w),
                   w_out.astype(BF16), w_mlp_in.astype(BF16), w_mlp_out.astype(BF16),
                   row(norm_mix_post), row(norm_mlp_pre), row(norm_mlp_post), min(512, S))
```

```python
import functools
import math

import numpy as np
import jax
import jax.numpy as jnp
from jax import lax
from jax.experimental import pallas as pl
from jax.experimental.pallas import tpu as pltpu

F32 = jnp.float32
BF16 = jnp.bfloat16

D_MODEL = 1024
M_WIDTH = 512
M_HEADS = 4
HEAD_DIM = 128
H_WIDTH = 512
H_GROUPS = 8
GROUP_W = H_WIDTH // H_GROUPS
CHUNK = 128
FILTER_EMB = 33
FILTER_HIDDEN = 64
DECAY_TARGET = 1e-2
FAST_DECAY_PCT = 0.3
SLOW_DECAY_PCT = 1.5
D_FF = 4 * D_MODEL
N_GATE = 16
EPS = 1e-6
LANES = 128
BF16_ROWS = 16
NEG_BIG = -1e30
VMEM_LIMIT = 56 * 1024 * 1024


def _cparams(*sem):
    return pltpu.CompilerParams(dimension_semantics=sem, vmem_limit_bytes=VMEM_LIMIT)


def _rms_rows(xf, w):
    ms = jnp.mean(xf * xf, axis=-1, keepdims=True)
    return xf * lax.rsqrt(ms + EPS) * w


def _sigmoid(x):
    return 1.0 / (1.0 + jnp.exp(-x))


def _log_sigmoid(x):
    return jnp.minimum(x, 0.0) - jnp.log(1.0 + jnp.exp(-jnp.abs(x)))


def _proj_kernel(x_ref, xp_ref, xn_ref, nw_ref, wt_ref, bgt_ref, cw_ref, cb_ref, cwk_ref, cbk_ref,
                 cm_ref, k_ref, gcm_ref, wbf_s, hall_s, pc_s, *, col_blk):
    TM = x_ref.shape[1]
    HALO = xp_ref.shape[1]
    j = pl.program_id(1)

    @pl.when(jnp.logical_and(pl.program_id(0) == 0, j == 0))
    def _():
        for c0 in range(0, wbf_s.shape[1], col_blk):
            wbf_s[:, c0:c0 + col_blk] = wt_ref[c0:c0 + col_blk, :].T.astype(BF16)

    nw = nw_ref[...]
    hn = _rms_rows(x_ref[0], nw).astype(BF16)
    hall_s[0:HALO, :] = _rms_rows(xp_ref[0], nw).astype(BF16)
    hall_s[HALO:HALO + TM, :] = hn
    hall_s[HALO + TM:2 * HALO + TM, :] = _rms_rows(xn_ref[0], nw).astype(BF16)
    keep_p = jnp.where(j > 0, 1.0, 0.0)
    keep_n = jnp.where(j < pl.num_programs(1) - 1, 1.0, 0.0)
    n_cm = cw_ref.shape[1]
    n_blk = n_cm // col_blk
    w_k0 = M_WIDTH
    w_gate = n_cm + M_WIDTH
    nt_dims = (((1,), (1,)), ((), ()))
    gt = lax.dot_general(wt_ref[w_gate:w_gate + N_GATE, :].astype(BF16), hn, nt_dims,
                         preferred_element_type=F32) + bgt_ref[...]
    row = lax.broadcasted_iota(jnp.int32, gt.shape, 0)
    gcm_ref[0] = jnp.where((row % 8) >= 4, _log_sigmoid(gt), gt)

    n_rb = TM // LANES
    n_sl = col_blk // LANES

    def project(w_row0, slot):
        wblk = wbf_s[:, pl.ds(w_row0, col_blk)]
        res = jnp.dot(hall_s[...], wblk, preferred_element_type=F32)
        for u in range(n_sl):
            ls = slice(u * LANES, (u + 1) * LANES)
            pc_s[slot, u, 0:HALO, :] = res[0:HALO, ls] * keep_p
            pc_s[slot, u, HALO:HALO + TM, :] = res[HALO:HALO + TM, ls]
            pc_s[slot, u, HALO + TM:2 * HALO + TM, :] = res[HALO + TM:2 * HALO + TM, ls] * keep_n

    def conv_tile(slot, u, rb, taps, bias):
        r = HALO + rb * LANES
        return (pc_s[slot, u, pl.ds(r - 1, LANES, stride=1), :] * taps[0:1]
                + pc_s[slot, u, r:r + LANES, :] * taps[1:2]
                + pc_s[slot, u, pl.ds(r + 1, LANES, stride=1), :] * taps[2:3] + bias)

    n_q = M_WIDTH // col_blk
    per_trip = 4
    assert n_blk % per_trip == 0

    def w_row_of(c):
        row_blk = jnp.where(c < n_q, c, c + n_q)
        return jnp.where(c == n_blk, w_k0 // col_blk, row_blk) * col_blk

    def emit(c, slot):
        for u in range(n_sl):
            col0 = pl.multiple_of(c * col_blk + u * LANES, LANES)
            taps = cw_ref[:, pl.ds(col0, LANES)]
            bias = cb_ref[:, pl.ds(col0, LANES)]
            for rb in range(n_rb):
                cm_ref[0, rb, pl.ds(col0, LANES), :] = conv_tile(slot, u, rb, taps, bias).T.astype(BF16)

    def body(i, carry):
        for u in range(per_trip):
            c = per_trip * i + u
            emit(c, u % 2)
            project(pl.multiple_of(w_row_of(c + 1), col_blk), (u + 1) % 2)
        return carry

    project(0, 0)
    lax.fori_loop(0, n_blk // per_trip, body, 0)

    k_scale = HEAD_DIM ** -0.5
    n_k = M_WIDTH // col_blk
    for kb in range(n_k):
        if kb > 0:
            project(w_k0 + kb * col_blk, kb % 2)
        for u in range(n_sl):
            cs = slice(kb * col_blk + u * LANES, kb * col_blk + (u + 1) * LANES)
            for rb in range(n_rb):
                cv = conv_tile(kb % 2, u, rb, cwk_ref[:, cs], cbk_ref[:, cs])
                k_ref[0, rb * LANES:(rb + 1) * LANES, cs] = (cv * _sigmoid(cv) * k_scale).astype(BF16)


def _proj(x, nw, w_in_t, bgt, cw_cm, cb_cm, cw_k, cb_k, tm_rows, col_blk=256):
    B, S, D = x.shape
    HALO = BF16_ROWS
    n_cm = cw_cm.shape[1]
    n_t = tm_rows // LANES
    hb = tm_rows // HALO
    last = S // HALO - 1
    const = lambda b, j: (0, 0)
    resident = functools.partial(pl.BlockSpec, index_map=const, pipeline_mode=pl.Buffered(1))
    return pl.pallas_call(
        functools.partial(_proj_kernel, col_blk=col_blk),
        grid=(B, S // tm_rows),
        in_specs=[
            pl.BlockSpec((1, tm_rows, D), lambda b, j: (b, j, 0)),
            pl.BlockSpec((1, HALO, D), lambda b, j: (b, jnp.maximum(j * hb - 1, 0), 0)),
            pl.BlockSpec((1, HALO, D), lambda b, j: (b, jnp.minimum((j + 1) * hb, last), 0)),
            pl.BlockSpec((1, D), const),
            resident(w_in_t.shape),
            pl.BlockSpec((N_GATE, 1), const),
            pl.BlockSpec((3, n_cm), const),
            pl.BlockSpec((1, n_cm), const),
            pl.BlockSpec((3, M_WIDTH), const),
            pl.BlockSpec((1, M_WIDTH), const),
        ],
        out_specs=[
            pl.BlockSpec((1, n_t, n_cm, LANES), lambda b, j: (b, j, 0, 0)),
            pl.BlockSpec((1, tm_rows, M_WIDTH), lambda b, j: (b, j, 0)),
            pl.BlockSpec((1, N_GATE, tm_rows), lambda b, j: (b, 0, j)),
        ],
        out_shape=[
            jax.ShapeDtypeStruct((B, S // LANES, n_cm, LANES), BF16),
            jax.ShapeDtypeStruct((B, S, M_WIDTH), BF16),
            jax.ShapeDtypeStruct((B, N_GATE, S), F32),
        ],
        scratch_shapes=[pltpu.VMEM((D, n_cm + M_WIDTH), BF16),
                        pltpu.VMEM((tm_rows + 2 * HALO, D), BF16),
                        pltpu.VMEM((2, col_blk // LANES, tm_rows + 2 * HALO, LANES), F32)],
        compiler_params=_cparams("arbitrary", "arbitrary"),
        name="proj",
    )(x, x, x, nw, w_in_t, bgt, cw_cm, cb_cm, cw_k, cb_k)


def _mlstm_kernel(qt_ref, vt_ref, k_ref, g_ref, out_ref, q_s, va_s, rows_s, h_s, c_s):
    S = k_ref.shape[1]
    HB = g_ref.shape[1]
    L = CHUNK
    NC = S // L
    D = HEAD_DIM
    DA = D + BF16_ROWS
    W = HB * D

    for hd in range(HB):
        g = g_ref[0, hd]
        lane = lax.broadcasted_iota(jnp.int32, g.shape, 1) % L
        pre = g
        suf = g
        d = 1
        while d < L:
            pre = pre + jnp.where(lane >= d, pltpu.roll(pre, d, 1), 0.0)
            suf = suf + jnp.where(lane < L - d, pltpu.roll(suf, S - d, 1), 0.0)
            d *= 2
        rows_s[hd, 0:1, :] = g[0:1, :]
        rows_s[hd, 1:2, :] = pre[1:2, :]
        rows_s[hd, 2:3, :] = g[2:3, :]
        rows_s[hd, 3:4, :] = suf[3:4, :]

    ones_row = (lax.broadcasted_iota(jnp.int32, (BF16_ROWS, L), 0) == 0).astype(BF16)

    def prep(c, carry):
        r0 = pl.multiple_of(c * L, L)
        h_s[:, pl.ds(r0, L)] = jnp.zeros((W, L), F32)
        qpre = qt_ref[0, c].astype(F32)
        q_s[c] = (qpre * _sigmoid(qpre)).astype(BF16)
        for hd in range(HB):
            va_s[hd, 0:D, pl.ds(r0, L)] = vt_ref[0, c, hd * D:(hd + 1) * D, :]
            va_s[hd, D:DA, pl.ds(r0, L)] = ones_row
        return carry

    lax.fori_loop(0, NC, prep, 0)
    c_s[...] = jnp.zeros_like(c_s)

    si = lax.broadcasted_iota(jnp.int32, (L, L), 0)
    ti = lax.broadcasted_iota(jnp.int32, (L, L), 1)
    eye = si == ti
    dirs = ((0, 1, si <= ti, L - 1), (2, 3, si >= ti, 0))

    def chain(hd, dr, c, m):
        li_row, b_row, mask, g_lane = dirs[dr]
        r0 = pl.multiple_of(c * L, L)
        r_li = rows_s[hd, li_row:li_row + 1, pl.ds(r0, L)]
        r_b = rows_s[hd, b_row:b_row + 1, pl.ds(r0, L)]
        g_tot = r_b[:, g_lane:g_lane + 1]
        r_a = g_tot - r_b + r_li
        m_new = jnp.maximum(g_tot + m, jnp.max(r_a, axis=1, keepdims=True))
        decay = jnp.exp(g_tot + m - m_new)
        r_w = jnp.exp(r_a - m_new)
        a_col = jnp.sum(jnp.where(eye, r_li - r_b, 0.0), axis=1, keepdims=True)
        dmat = jnp.where(mask, a_col + r_b, NEG_BIG)
        inter_log = r_b + m
        m_t = jnp.maximum(inter_log, jnp.max(dmat, axis=0, keepdims=True))
        wts = jnp.exp(dmat - m_t)
        kc = k_ref[0, pl.ds(r0, L), hd * D:(hd + 1) * D]
        qtc = q_s[c, hd * D:(hd + 1) * D, :]
        vac = va_s[hd, :, pl.ds(r0, L)]
        cmat = c_s[hd, dr]
        both = jnp.dot(jnp.concatenate([kc, cmat.astype(BF16)], axis=0), qtc,
                       preferred_element_type=F32)
        s_w = (both[0:L] * wts).astype(BF16)
        tot = (jnp.dot(vac, s_w, preferred_element_type=F32)
               + jnp.exp(inter_log - m_t) * both[L:L + DA])
        den = jnp.maximum(jnp.abs(tot[D:D + 1]), jnp.exp(-m_t))
        h = tot[0:D] * (1.0 / den)
        h_s[hd * D:(hd + 1) * D, pl.ds(r0, L)] = h_s[hd * D:(hd + 1) * D, pl.ds(r0, L)] + h
        u = (vac.astype(F32) * r_w).astype(BF16)
        c_s[hd, dr] = decay * cmat + jnp.dot(u, kc, preferred_element_type=F32)
        return m_new

    def step(i, ms):
        out = []
        for hd in range(HB):
            out.append(chain(hd, 0, i, ms[2 * hd]))
            out.append(chain(hd, 1, NC - 1 - i, ms[2 * hd + 1]))
        return tuple(out)

    lax.fori_loop(0, NC, step, tuple(jnp.zeros((1, 1), F32) for _ in range(2 * HB)))

    def fin(c, carry):
        out_ref[0, c] = h_s[:, pl.ds(pl.multiple_of(c * L, L), L)].astype(BF16)
        return carry

    lax.fori_loop(0, NC, fin, 0, unroll=4)


def _mlstm(mt, k_tm, gcm, heads_per_step=4):
    B, S, _ = k_tm.shape
    NC = S // CHUNK
    D = HEAD_DIM
    HB = heads_per_step
    v0 = (M_WIDTH + 3 * H_WIDTH) // (HB * D)
    W = HB * D
    NB = M_HEADS // HB
    DA = D + BF16_ROWS
    cm_blk = (1, NC, W, LANES)
    return pl.pallas_call(
        _mlstm_kernel,
        grid=(B, NB),
        in_specs=[
            pl.BlockSpec(cm_blk, lambda b, h: (b, 0, h, 0)),
            pl.BlockSpec(cm_blk, lambda b, h: (b, 0, v0 + h, 0)),
            pl.BlockSpec((1, S, W), lambda b, h: (b, 0, h)),
            pl.BlockSpec((1, HB, 8, S), lambda b, h: (b, h, 0, 0)),
        ],
        out_specs=pl.BlockSpec(cm_blk, lambda b, h: (b, 0, h, 0)),
        out_shape=jax.ShapeDtypeStruct((B, NC, M_WIDTH, LANES), BF16),
        scratch_shapes=[
            pltpu.VMEM((NC, W, LANES), BF16),
            pltpu.VMEM((HB, DA, S), BF16),
            pltpu.VMEM((HB, 8, S), F32),
            pltpu.VMEM((W, S), F32),
            pltpu.VMEM((HB, 2, DA, D), F32),
        ],
        compiler_params=_cparams("parallel", "parallel"),
        name="mlstm",
    )(mt, mt, k_tm, gcm)


def _filt_mlp_kernel(w1t_ref, b1_ref, w2t_ref, b2_ref, w3t_ref, b3_ref, fr_ref, w4f_ref, w4b_ref,
                     out_ref, h3_s, *, cb_rows):
    S = h3_s.shape[2]
    hi = lax.Precision.HIGHEST
    lane = lax.broadcasted_iota(jnp.int32, (1, S), 1)
    pos_f = lane.astype(F32)
    pos_b = (S - lane).astype(F32)

    def features(pos):
        bands = (FILTER_EMB - 1) // 2
        t = pos / (S - 1)
        ang = (2.0 * math.pi) * pos / S
        fidx = lax.broadcasted_iota(jnp.int32, (bands, 1), 0).astype(F32)
        f = 1e-4 + fidx * ((bands - 1 - 1e-4) / (bands - 1))
        fa = f * ang
        w1t = w1t_ref[...]
        pre = (w1t[:, 0:1] * t
               + jnp.dot(w1t[:, 1:1 + bands], jnp.cos(fa), precision=hi, preferred_element_type=F32)
               - jnp.dot(w1t[:, 1 + bands:], jnp.sin(fa), precision=hi, preferred_element_type=F32))
        fr = fr_ref[...]
        h = jnp.sin(fr[:, 0:1] * (pre + b1_ref[...]))
        h = jnp.sin(fr[:, 1:2] * (jnp.dot(w2t_ref[...], h, precision=hi, preferred_element_type=F32)
                                  + b2_ref[...]))
        return jnp.sin(fr[:, 2:3] * (jnp.dot(w3t_ref[...], h, precision=hi, preferred_element_type=F32)
                                     + b3_ref[...]))

    @pl.when(pl.program_id(0) == 0)
    def _():
        h3 = features(pos_f)
        h3_s[0] = h3
        n_t = S // LANES
        anti = (lax.broadcasted_iota(jnp.int32, (LANES, LANES), 0)
                + lax.broadcasted_iota(jnp.int32, (LANES, LANES), 1) == LANES - 1).astype(F32)
        rev = jnp.concatenate(
            [jnp.dot(h3[:, (n_t - 1 - u) * LANES:(n_t - u) * LANES], anti, precision=hi,
                     preferred_element_type=F32) for u in range(n_t)], axis=1)
        h3_s[1] = pltpu.roll(rev, 1, 1)

    r = pl.program_id(0) * cb_rows + lax.broadcasted_iota(jnp.int32, (cb_rows, 1), 0)
    ch = (r % H_WIDTH).astype(F32)
    max_decay = math.log(DECAY_TARGET) / FAST_DECAY_PCT
    min_decay = math.log(DECAY_TARGET) / SLOW_DECAY_PCT
    delta = jnp.abs(min_decay + ch * ((max_decay - min_decay) / (H_WIDTH - 1)))
    n_t = S // LANES
    for half, (w_ref, pos) in enumerate(((w4f_ref, pos_f), (w4b_ref, pos_b))):
        filt = jnp.dot(w_ref[...].astype(BF16), h3_s[half].astype(BF16),
                       preferred_element_type=F32)
        filt = filt * jnp.exp(-(pos / (S - 1)) * delta)
        if half == 1:
            filt = jnp.where(lane == 0, 0.0, filt)
        for u in range(n_t):
            out_ref[half * n_t + u] = filt[:, u * LANES:(u + 1) * LANES].astype(BF16)


def _filt_mlp(w1t, b1, w2t, b2, w3t, b3, fr, w4t, S, cb_rows=256):
    R = w4t.shape[0] // 2
    Hd = FILTER_HIDDEN
    const = lambda i: (0, 0)
    nblk = R // cb_rows
    return pl.pallas_call(
        functools.partial(_filt_mlp_kernel, cb_rows=cb_rows),
        grid=(nblk,),
        in_specs=[
            pl.BlockSpec((Hd, FILTER_EMB), const),
            pl.BlockSpec((Hd, 1), const),
            pl.BlockSpec((Hd, Hd), const),
            pl.BlockSpec((Hd, 1), const),
            pl.BlockSpec((Hd, Hd), const),
            pl.BlockSpec((Hd, 1), const),
            pl.BlockSpec((Hd, 3), const),
            pl.BlockSpec((cb_rows, Hd), lambda i: (i, 0)),
            pl.BlockSpec((cb_rows, Hd), lambda i: (nblk + i, 0)),
        ],
        out_specs=pl.BlockSpec((2 * S // LANES, cb_rows, LANES), lambda i: (0, i, 0)),
        out_shape=jax.ShapeDtypeStruct((2 * S // LANES, R, LANES), BF16),
        scratch_shapes=[pltpu.VMEM((2, Hd, S), F32)],
        compiler_params=_cparams("arbitrary"),
        name="filt_mlp",
    )(w1t, b1, w2t, b2, w3t, b3, fr, w4t, w4t)


@functools.lru_cache(maxsize=None)
def _dft_constants(S):
    N = 2 * S
    N2 = LANES
    N1 = N // N2
    h = N1 // 2
    k1 = np.arange(N1)
    k2 = np.arange(N2)
    a1 = -2.0 * np.pi * np.outer(k1, k1) / N1
    f1r, f1i = np.cos(a1), np.sin(a1)
    at = -2.0 * np.pi * np.outer(k1, k2) / N
    twr, twi = np.cos(at), np.sin(at)
    a2 = -2.0 * np.pi * np.outer(k2, k2) / N2
    f2r, f2i = np.cos(a2), np.sin(a2)
    s1c = np.block([[f1r[:, :h], -f1i[:, :h]], [f1i[:, :h], f1r[:, :h]]])
    s1r = np.concatenate([f1r, f1i], axis=0)
    s2 = np.block([[f2r, f2i], [-f2i, f2r]])
    s2i = np.block([[f2r, -f2i], [f2i, f2r]])
    s1i = np.block([[f1r[:h, :], f1i[:h, :]], [-f1i[:h, :], f1r[:h, :]]])
    cast = lambda a: np.asarray(a, np.float32)
    return dict(s1c=cast(s1c), s1r=cast(s1r), s2=cast(s2), s2i=cast(s2i), s1i=cast(s1i),
                twr=cast(twr), twi=cast(twi), N1=N1, h=h)


def _bf(a):
    return jnp.asarray(a, F32).astype(BF16)


def _filt_fft_kernel(bias_ref, kern_ref, s1r_ref, s2_ref, twr_ref, twi_ref, out_ref,
                     src_s, slab_s, *, unroll):
    N1 = twr_ref.shape[0]
    n_ch = kern_ref.shape[1]
    pitch = src_s.shape[0] // N1
    scale = 1.0 / (N1 * LANES)
    twr = twr_ref[...]
    twi = twi_ref[...]
    s1r = s1r_ref[...]

    def load_in(t1, carry):
        src_s[pl.ds(pl.multiple_of(t1 * pitch, 8), n_ch), :] = kern_ref[t1].astype(F32)
        return carry
    lax.fori_loop(0, N1, load_in, 0, unroll=4)

    def per_group(j, carry):
        for u in range(0, unroll, 2):
            cs = (j * unroll + u, j * unroll + u + 1)
            z = jnp.concatenate([src_s[pl.ds(c, N1, stride=pitch), :] for c in cs], axis=1)
            a = jnp.dot(s1r, z.astype(BF16), preferred_element_type=F32)
            for i, c in enumerate(cs):
                ar = a[0:N1, i * LANES:(i + 1) * LANES]
                ai = a[N1:2 * N1, i * LANES:(i + 1) * LANES]
                r0 = pl.multiple_of(c * N1, N1)
                slab_s[pl.ds(r0, N1), 0:LANES] = (ar * twr - ai * twi).astype(BF16)
                slab_s[pl.ds(r0, N1), LANES:2 * LANES] = (ar * twi + ai * twr).astype(BF16)
        return carry

    lax.fori_loop(0, n_ch // unroll, per_group, 0)
    x = jnp.dot(slab_s[...], s2_ref[...], preferred_element_type=F32)
    bias = bias_ref[...]
    out_ref[:, :, 0:LANES] = ((x[:, 0:LANES].reshape(n_ch, N1, LANES) + bias) * scale).astype(BF16)
    out_ref[:, :, LANES:2 * LANES] = (x[:, LANES:2 * LANES].reshape(n_ch, N1, LANES)
                                      * scale).astype(BF16)


def _filt_fft(bias3, kern, consts, n_ch=32, unroll=32):
    N1, R, _ = kern.shape
    const = lambda i: (0, 0)
    pitch = n_ch + 8
    return pl.pallas_call(
        functools.partial(_filt_fft_kernel, unroll=unroll),
        grid=(R // n_ch,),
        in_specs=[
            pl.BlockSpec((n_ch, 1, 1), lambda i: (i, 0, 0)),
            pl.BlockSpec((N1, n_ch, LANES), lambda i: (0, i, 0)),
            pl.BlockSpec((2 * N1, N1), const),
            pl.BlockSpec((2 * LANES, 2 * LANES), const),
            pl.BlockSpec((N1, LANES), const),
            pl.BlockSpec((N1, LANES), const),
        ],
        out_specs=pl.BlockSpec((n_ch, N1, 2 * LANES), lambda i: (i, 0, 0)),
        out_shape=jax.ShapeDtypeStruct((R, N1, 2 * LANES), BF16),
        scratch_shapes=[pltpu.VMEM((N1 * pitch, LANES), F32),
                        pltpu.VMEM((n_ch * N1, 2 * LANES), BF16)],
        compiler_params=_cparams("parallel"),
        name="filt_fft",
    )(bias3, kern, _bf(consts["s1r"]), _bf(consts["s2"]),
      jnp.asarray(consts["twr"]), jnp.asarray(consts["twi"]))


def _hyena_kernel(nw_ref, v_ref, x1_ref, x2_ref, khat_ref, s1c_ref, s2_ref, s2i_ref, s1i_ref,
                  twr_ref, twi_ref, out_ref, src_s, slab_s, yf_s, q_s, y_s, *, unroll):
    h = v_ref.shape[1]
    n_ch = v_ref.shape[2]
    N1 = 2 * h
    twr = twr_ref[...]
    twi = twi_ref[...]
    s1c = s1c_ref[...]
    s1i = s1i_ref[...]
    n_groups = n_ch // unroll
    rows = unroll * N1
    pitch = src_s.shape[0] // (2 * h)

    def tile_rows(i):
        return pl.ds(pl.multiple_of(i * pitch, 8), n_ch)

    def chan_rows(c, bb):
        return pl.ds(bb * h * pitch + c, h, stride=pitch)

    def long_conv(order):
        def stage_a(g):
            for u in range(0, unroll, 2):
                cs = (g * unroll + u, g * unroll + u + 1)
                z = jnp.concatenate(
                    [jnp.concatenate([src_s[chan_rows(c, 0), :], src_s[chan_rows(c, 1), :]], axis=0)
                     for c in cs], axis=1)
                a = jnp.dot(s1c, z.astype(BF16), preferred_element_type=F32)
                for i, c in enumerate(cs):
                    ar = a[0:N1, i * LANES:(i + 1) * LANES]
                    ai = a[N1:2 * N1, i * LANES:(i + 1) * LANES]
                    r0 = c * N1 if isinstance(c, int) else pl.multiple_of(c * N1, N1)
                    slab_s[pl.ds(r0, N1), 0:LANES] = (ar * twr - ai * twi).astype(BF16)
                    slab_s[pl.ds(r0, N1), LANES:2 * LANES] = (ar * twi + ai * twr).astype(BF16)

        def stage_m1(g):
            r0 = g * rows if isinstance(g, int) else pl.multiple_of(g * rows, rows)
            c0 = g * unroll if isinstance(g, int) else pl.multiple_of(g * unroll, unroll)
            x = jnp.dot(slab_s[pl.ds(r0, rows), :], s2_ref[...], preferred_element_type=F32)
            xr = x[:, 0:LANES].reshape(unroll, N1, LANES)
            xi = x[:, LANES:2 * LANES].reshape(unroll, N1, LANES)
            kr = khat_ref[order, pl.ds(c0, unroll), :, 0:LANES].astype(F32)
            ki = khat_ref[order, pl.ds(c0, unroll), :, LANES:2 * LANES].astype(F32)
            yf_s[pl.ds(r0, rows), 0:LANES] = (xr * kr - xi * ki).reshape(rows, LANES).astype(BF16)
            yf_s[pl.ds(r0, rows), LANES:2 * LANES] = (xr * ki + xi * kr).reshape(rows, LANES).astype(BF16)

        def stage_m2(g):
            r0 = g * rows if isinstance(g, int) else pl.multiple_of(g * rows, rows)
            c0 = g * unroll if isinstance(g, int) else pl.multiple_of(g * unroll, unroll)
            p = jnp.dot(yf_s[pl.ds(r0, rows), :], s2i_ref[...], preferred_element_type=F32)
            pr = p[:, 0:LANES].reshape(unroll, N1, LANES)
            pi = p[:, LANES:2 * LANES].reshape(unroll, N1, LANES)
            q_s[pl.ds(c0, unroll), 0:N1, :] = (pr * twr + pi * twi).astype(BF16)
            q_s[pl.ds(c0, unroll), N1:2 * N1, :] = (pi * twr - pr * twi).astype(BF16)

        def stage_i(g):
            for u in range(0, unroll, 2):
                cs = (g * unroll + u, g * unroll + u + 1)
                q2 = jnp.concatenate([q_s[c] for c in cs], axis=1)
                y = jnp.dot(s1i, q2, preferred_element_type=F32)
                for i, c in enumerate(cs):
                    y_s[chan_rows(c, 0), :] = y[0:h, i * LANES:(i + 1) * LANES]
                    y_s[chan_rows(c, 1), :] = y[h:2 * h, i * LANES:(i + 1) * LANES]

        G = n_groups
        for t in range(G + 3):
            if 0 <= t - 3 < G:
                stage_i(t - 3)
            if 0 <= t - 2 < G:
                stage_m2(t - 2)
            if 0 <= t - 1 < G:
                stage_m1(t - 1)
            if t < G:
                stage_a(t)

    def per_tile(fn):
        for bb in range(2):
            def body(t1, carry, bb=bb):
                fn(bb, t1, tile_rows(bb * h + t1))
                return carry
            lax.fori_loop(0, h, body, 0, unroll=8)

    def load_in(bb, t1, rws):
        src_s[rws, :] = v_ref[bb, t1].astype(F32)
    per_tile(load_in)
    long_conv(0)

    def gate1(bb, t1, rws):
        src_s[rws, :] = y_s[rws, :] * x1_ref[bb, t1].astype(F32)
    per_tile(gate1)
    long_conv(1)

    nw = jnp.broadcast_to(nw_ref[...], (n_ch, LANES))

    def gate2_norm(bb, t1, rws):
        z = y_s[rws, :] * x2_ref[bb, t1].astype(F32)
        ms = jnp.mean(z * z, axis=0, keepdims=True)
        out_ref[bb, t1] = (z * lax.rsqrt(ms + EPS) * nw).astype(BF16)
    per_tile(gate2_norm)


def _hyena(nw_col, hy4, khat, consts, unroll=8):
    B, h, _, _ = hy4.shape
    C = H_WIDTH
    n_ch = GROUP_W
    G = C // n_ch
    g0 = M_WIDTH // n_ch
    N1 = consts["N1"]
    const = lambda g, p: (0, 0)
    blk = (2, h, n_ch, LANES)
    pitch = n_ch + 8
    return pl.pallas_call(
        functools.partial(_hyena_kernel, unroll=unroll),
        grid=(G, B // 2),
        in_specs=[
            pl.BlockSpec((n_ch, 1), lambda g, p: (g, 0)),
            pl.BlockSpec(blk, lambda g, p: (p, 0, g0 + g, 0)),
            pl.BlockSpec(blk, lambda g, p: (p, 0, g0 + G + g, 0)),
            pl.BlockSpec(blk, lambda g, p: (p, 0, g0 + 2 * G + g, 0)),
            pl.BlockSpec((2, n_ch, N1, 2 * LANES), lambda g, p: (0, g, 0, 0)),
            pl.BlockSpec((2 * N1, 2 * h), const),
            pl.BlockSpec((2 * LANES, 2 * LANES), const),
            pl.BlockSpec((2 * LANES, 2 * LANES), const),
            pl.BlockSpec((2 * h, 2 * N1), const),
            pl.BlockSpec((N1, LANES), const),
            pl.BlockSpec((N1, LANES), const),
        ],
        out_specs=pl.BlockSpec(blk, lambda g, p: (p, 0, g, 0)),
        out_shape=jax.ShapeDtypeStruct((B, h, C, LANES), BF16),
        scratch_shapes=[
            pltpu.VMEM((2 * h * pitch, LANES), F32),
            pltpu.VMEM((n_ch * N1, 2 * LANES), BF16),
            pltpu.VMEM((n_ch * N1, 2 * LANES), BF16),
            pltpu.VMEM((n_ch, 2 * N1, LANES), BF16),
            pltpu.VMEM((2 * h * pitch, LANES), F32),
        ],
        compiler_params=_cparams("parallel", "arbitrary"),
        name="hyena",
    )(nw_col, hy4, hy4, hy4, khat,
      _bf(consts["s1c"]), _bf(consts["s2"]), _bf(consts["s2i"]), _bf(consts["s1i"]),
      jnp.asarray(consts["twr"]), jnp.asarray(consts["twi"]))


def _outmlp_kernel(x_ref, hm_ref, og_ref, yh_ref, nwm_ref, wo_ref, w1_ref, w2_ref, n_post_ref,
                   n_pre_ref, n_post2_ref, out_ref):
    n_t = hm_ref.shape[1]
    nwm = jnp.broadcast_to(nwm_ref[...], (M_WIDTH, LANES))
    tiles = []
    for j in range(n_t):
        hg = hm_ref[0, j].astype(F32) * _sigmoid(og_ref[0, j].astype(F32))
        heads = []
        for hd in range(M_HEADS):
            hh = hg[hd * HEAD_DIM:(hd + 1) * HEAD_DIM]
            ms = jnp.mean(hh * hh, axis=0, keepdims=True)
            heads.append(hh * lax.rsqrt(ms + EPS))
        ym = jnp.concatenate(heads, axis=0) * nwm
        yt = jnp.concatenate([ym, yh_ref[0, j].astype(F32)], axis=0)
        tiles.append(yt.T.astype(BF16))
    y = tiles[0] if n_t == 1 else jnp.concatenate(tiles, axis=0)
    mix = jnp.dot(y, wo_ref[...], preferred_element_type=F32)
    x1 = x_ref[0] + _rms_rows(mix, n_post_ref[...])
    hm = _rms_rows(x1, n_pre_ref[...]).astype(BF16)
    mid = jnp.maximum(jnp.dot(hm, w1_ref[...], preferred_element_type=F32), 0.0)
    mid = (mid * mid).astype(BF16)
    ff = jnp.dot(mid, w2_ref[...], preferred_element_type=F32)
    out_ref[0] = x1 + _rms_rows(ff, n_post2_ref[...])


def _outmlp(x, hm, mt, yh, nwm_col, wo, w1, w2, n_post, n_pre, n_post2, tm_rows):
    B, S, D = x.shape
    n_t = tm_rows // LANES
    og0 = mt.shape[2] // M_WIDTH - 1
    const = lambda b, i: (0, 0)
    resident = functools.partial(pl.BlockSpec, index_map=const, pipeline_mode=pl.Buffered(1))
    return pl.pallas_call(
        _outmlp_kernel,
        grid=(B, S // tm_rows),
        in_specs=[
            pl.BlockSpec((1, tm_rows, D), lambda b, i: (b, i, 0)),
            pl.BlockSpec((1, n_t, M_WIDTH, LANES), lambda b, i: (b, i, 0, 0)),
            pl.BlockSpec((1, n_t, M_WIDTH, LANES), lambda b, i: (b, i, og0, 0)),
            pl.BlockSpec((1, n_t, H_WIDTH, LANES), lambda b, i: (b, i, 0, 0)),
            pl.BlockSpec((M_WIDTH, 1), const),
            resident((D, D)),
            resident((D, D_FF)),
            resident((D_FF, D)),
            pl.BlockSpec((1, D), const),
            pl.BlockSpec((1, D), const),
            pl.BlockSpec((1, D), const),
        ],
        out_specs=pl.BlockSpec((1, tm_rows, D), lambda b, i: (b, i, 0)),
        out_shape=jax.ShapeDtypeStruct((B, S, D), F32),
        compiler_params=_cparams("parallel", "parallel"),
        name="outmlp",
    )(x, hm, mt, yh, nwm_col, wo, w1, w2, n_post, n_pre, n_post2)


def kernel(x, norm_mix_pre, norm_mix_post, norm_mlp_pre, norm_mlp_post, w_in, b_gates,
           conv_w, conv_b, mlstm_norm_w, hyena_norm_w, filt_w1, filt_b1, filt_w2, filt_b2,
           filt_w3, filt_b3, filt_w4, filt_freq, filt_bias, w_out, w_mlp_in, w_mlp_out):
    B, S, D = x.shape
    assert D == D_MODEL and B % 2 == 0 and S % CHUNK == 0
    H = M_HEADS
    row = lambda a: a.astype(F32).reshape(1, -1)
    col = lambda a: a.astype(F32).reshape(-1, 1)
    tm_rows = min(1024, S)

    n_conv = 2 * M_WIDTH + 3 * H_WIDTH
    o_gate = n_conv + 2 * M_WIDTH
    assert w_in.shape == (D, o_gate + N_GATE)
    cw = conv_w.astype(F32)
    cbias = conv_b.astype(F32)
    ident = jnp.zeros((3, 2 * M_WIDTH), F32).at[1].set(1.0)
    cw_cm = jnp.concatenate([cw[:, 0:M_WIDTH], cw[:, 2 * M_WIDTH:n_conv], ident], axis=1)
    cb_cm = jnp.concatenate([cbias[0:M_WIDTH], cbias[2 * M_WIDTH:n_conv],
                             jnp.zeros((2 * M_WIDTH,), F32)]).reshape(1, -1)
    mt, k_tm, gcm = _proj(x, row(norm_mix_pre), w_in.astype(F32).T, col(b_gates), cw_cm, cb_cm,
                          cw[:, M_WIDTH:2 * M_WIDTH], row(cbias[M_WIDTH:2 * M_WIDTH]), tm_rows)

    g4 = gcm.reshape(B, 4, H, S).transpose(0, 2, 1, 3)
    g8 = jnp.concatenate([g4, jnp.zeros_like(g4)], axis=2)
    h_m = _mlstm(mt, k_tm, g8)

    consts = _dft_constants(S)
    kern = _filt_mlp(filt_w1.astype(F32).T, col(filt_b1), filt_w2.astype(F32).T, col(filt_b2),
                     filt_w3.astype(F32).T, col(filt_b3), filt_freq.astype(F32).T,
                     filt_w4.astype(F32).T, S)
    khat = _filt_fft(filt_bias.astype(F32).reshape(2 * H_WIDTH, 1, 1), kern, consts)
    khat = khat.reshape(2, H_WIDTH, consts["N1"], 2 * LANES)

    y_h = _hyena(col(hyena_norm_w), mt, khat, consts)

    return _outmlp(x, h_m, mt, y_h, col(mlstm_norm_w),
                   w_out.astype(BF16), w_mlp_in.astype(BF16), w_mlp_out.astype(BF16),
                   row(norm_mix_post), row(norm_mlp_pre), row(norm_mlp_post), min(512, S))
```

```python
import functools
import math

import numpy as np
import jax
import jax.numpy as jnp
from jax import lax
from jax.experimental import pallas as pl
from jax.experimental.pallas import tpu as pltpu

F32 = jnp.float32
BF16 = jnp.bfloat16

D_MODEL = 1024
M_WIDTH = 512
M_HEADS = 4
HEAD_DIM = 128
H_WIDTH = 512
H_GROUPS = 8
GROUP_W = H_WIDTH // H_GROUPS
CHUNK = 128
FILTER_EMB = 33
FILTER_HIDDEN = 64
DECAY_TARGET = 1e-2
FAST_DECAY_PCT = 0.3
SLOW_DECAY_PCT = 1.5
D_FF = 4 * D_MODEL
N_GATE = 16
EPS = 1e-6
LANES = 128
BF16_ROWS = 16
NEG_BIG = -1e30
VMEM_LIMIT = 56 * 1024 * 1024


def _cparams(*sem):
    return pltpu.CompilerParams(dimension_semantics=sem, vmem_limit_bytes=VMEM_LIMIT)


def _rms_rows(xf, w):
    ms = jnp.mean(xf * xf, axis=-1, keepdims=True)
    return xf * lax.rsqrt(ms + EPS) * w


def _sigmoid(x):
    return 1.0 / (1.0 + jnp.exp(-x))


def _log_sigmoid(x):
    return jnp.minimum(x, 0.0) - jnp.log(1.0 + jnp.exp(-jnp.abs(x)))


def _proj_kernel(x_ref, xp_ref, xn_ref, nw_ref, wt_ref, bgt_ref, cw_ref, cb_ref, cwk_ref, cbk_ref,
                 cm_ref, k_ref, gcm_ref, wbf_s, hall_s, pc_s, *, col_blk):
    TM = x_ref.shape[1]
    HALO = xp_ref.shape[1]
    j = pl.program_id(1)

    @pl.when(jnp.logical_and(pl.program_id(0) == 0, j == 0))
    def _():
        for c0 in range(0, wbf_s.shape[1], col_blk):
            wbf_s[:, c0:c0 + col_blk] = wt_ref[c0:c0 + col_blk, :].T.astype(BF16)

    nw = nw_ref[...]
    hn = _rms_rows(x_ref[0], nw).astype(BF16)
    hall_s[0:HALO, :] = _rms_rows(xp_ref[0], nw).astype(BF16)
    hall_s[HALO:HALO + TM, :] = hn
    hall_s[HALO + TM:2 * HALO + TM, :] = _rms_rows(xn_ref[0], nw).astype(BF16)
    keep_p = jnp.where(j > 0, 1.0, 0.0)
    keep_n = jnp.where(j < pl.num_programs(1) - 1, 1.0, 0.0)
    n_cm = cw_ref.shape[1]
    n_blk = n_cm // col_blk
    w_k0 = M_WIDTH
    w_gate = n_cm + M_WIDTH
    nt_dims = (((1,), (1,)), ((), ()))
    gt = lax.dot_general(wt_ref[w_gate:w_gate + N_GATE, :].astype(BF16), hn, nt_dims,
                         preferred_element_type=F32) + bgt_ref[...]
    row = lax.broadcasted_iota(jnp.int32, gt.shape, 0)
    gcm_ref[0] = jnp.where((row % 8) >= 4, _log_sigmoid(gt), gt)

    n_rb = TM // LANES
    n_sl = col_blk // LANES

    def project(w_row0, slot):
        wblk = wbf_s[:, pl.ds(w_row0, col_blk)]
        res = jnp.dot(hall_s[...], wblk, preferred_element_type=F32)
        for u in range(n_sl):
            ls = slice(u * LANES, (u + 1) * LANES)
            pc_s[slot, u, 0:HALO, :] = res[0:HALO, ls] * keep_p
            pc_s[slot, u, HALO:HALO + TM, :] = res[HALO:HALO + TM, ls]
            pc_s[slot, u, HALO + TM:2 * HALO + TM, :] = res[HALO + TM:2 * HALO + TM, ls] * keep_n

    def conv_tile(slot, u, rb, taps, bias):
        r = HALO + rb * LANES
        return (pc_s[slot, u, pl.ds(r - 1, LANES, stride=1), :] * taps[0:1]
                + pc_s[slot, u, r:r + LANES, :] * taps[1:2]
                + pc_s[slot, u, pl.ds(r + 1, LANES, stride=1), :] * taps[2:3] + bias)

    n_q = M_WIDTH // col_blk
    per_trip = 12
    assert n_blk % per_trip == 0

    def w_row_of(c):
        row_blk = jnp.where(c < n_q, c, c + n_q)
        return jnp.where(c == n_blk, w_k0 // col_blk, row_blk) * col_blk

    def emit(c, slot):
        for u in range(n_sl):
            col0 = pl.multiple_of(c * col_blk + u * LANES, LANES)
            taps = cw_ref[:, pl.ds(col0, LANES)]
            bias = cb_ref[:, pl.ds(col0, LANES)]
            for rb in range(n_rb):
                cm_ref[0, rb, pl.ds(col0, LANES), :] = conv_tile(slot, u, rb, taps, bias).T.astype(BF16)

    def body(i, carry):
        for u in range(per_trip):
            c = per_trip * i + u
            emit(c, u % 2)
            project(pl.multiple_of(w_row_of(c + 1), col_blk), (u + 1) % 2)
        return carry

    project(0, 0)
    lax.fori_loop(0, n_blk // per_trip, body, 0)

    k_scale = HEAD_DIM ** -0.5
    n_k = M_WIDTH // col_blk
    for kb in range(n_k):
        if kb > 0:
            project(w_k0 + kb * col_blk, kb % 2)
        for u in range(n_sl):
            cs = slice(kb * col_blk + u * LANES, kb * col_blk + (u + 1) * LANES)
            for rb in range(n_rb):
                cv = conv_tile(kb % 2, u, rb, cwk_ref[:, cs], cbk_ref[:, cs])
                k_ref[0, rb * LANES:(rb + 1) * LANES, cs] = (cv * _sigmoid(cv) * k_scale).astype(BF16)


def _proj(x, nw, w_in_t, bgt, cw_cm, cb_cm, cw_k, cb_k, tm_rows, col_blk=256):
    B, S, D = x.shape
    HALO = BF16_ROWS
    n_cm = cw_cm.shape[1]
    n_t = tm_rows // LANES
    hb = tm_rows // HALO
    last = S // HALO - 1
    const = lambda b, j: (0, 0)
    resident = functools.partial(pl.BlockSpec, index_map=const, pipeline_mode=pl.Buffered(1))
    return pl.pallas_call(
        functools.partial(_proj_kernel, col_blk=col_blk),
        grid=(B, S // tm_rows),
        in_specs=[
            pl.BlockSpec((1, tm_rows, D), lambda b, j: (b, j, 0)),
            pl.BlockSpec((1, HALO, D), lambda b, j: (b, jnp.maximum(j * hb - 1, 0), 0)),
            pl.BlockSpec((1, HALO, D), lambda b, j: (b, jnp.minimum((j + 1) * hb, last), 0)),
            pl.BlockSpec((1, D), const),
            resident(w_in_t.shape),
            pl.BlockSpec((N_GATE, 1), const),
            pl.BlockSpec((3, n_cm), const),
            pl.BlockSpec((1, n_cm), const),
            pl.BlockSpec((3, M_WIDTH), const),
            pl.BlockSpec((1, M_WIDTH), const),
        ],
        out_specs=[
            pl.BlockSpec((1, n_t, n_cm, LANES), lambda b, j: (b, j, 0, 0)),
            pl.BlockSpec((1, tm_rows, M_WIDTH), lambda b, j: (b, j, 0)),
            pl.BlockSpec((1, N_GATE, tm_rows), lambda b, j: (b, 0, j)),
        ],
        out_shape=[
            jax.ShapeDtypeStruct((B, S // LANES, n_cm, LANES), BF16),
            jax.ShapeDtypeStruct((B, S, M_WIDTH), BF16),
            jax.ShapeDtypeStruct((B, N_GATE, S), F32),
        ],
        scratch_shapes=[pltpu.VMEM((D, n_cm + M_WIDTH), BF16),
                        pltpu.VMEM((tm_rows + 2 * HALO, D), BF16),
                        pltpu.VMEM((2, col_blk // LANES, tm_rows + 2 * HALO, LANES), F32)],
        compiler_params=_cparams("arbitrary", "arbitrary"),
        name="proj",
    )(x, x, x, nw, w_in_t, bgt, cw_cm, cb_cm, cw_k, cb_k)


def _mlstm_kernel(qt_ref, vt_ref, k_ref, g_ref, out_ref, q_s, va_s, rows_s, h_s, c_s):
    S = k_ref.shape[1]
    HB = g_ref.shape[1]
    L = CHUNK
    NC = S // L
    D = HEAD_DIM
    DA = D + BF16_ROWS
    W = HB * D

    for hd in range(HB):
        g = g_ref[0, hd]
        lane = lax.broadcasted_iota(jnp.int32, g.shape, 1) % L
        pre = g
        suf = g
        d = 1
        while d < L:
            pre = pre + jnp.where(lane >= d, pltpu.roll(pre, d, 1), 0.0)
            suf = suf + jnp.where(lane < L - d, pltpu.roll(suf, S - d, 1), 0.0)
            d *= 2
        rows_s[hd, 0:1, :] = g[0:1, :]
        rows_s[hd, 1:2, :] = pre[1:2, :]
        rows_s[hd, 2:3, :] = g[2:3, :]
        rows_s[hd, 3:4, :] = suf[3:4, :]

    ones_row = (lax.broadcasted_iota(jnp.int32, (BF16_ROWS, L), 0) == 0).astype(BF16)

    def prep(c, carry):
        r0 = pl.multiple_of(c * L, L)
        h_s[:, pl.ds(r0, L)] = jnp.zeros((W, L), F32)
        qpre = qt_ref[0, c].astype(F32)
        q_s[c] = (qpre * _sigmoid(qpre)).astype(BF16)
        for hd in range(HB):
            va_s[hd, 0:D, pl.ds(r0, L)] = vt_ref[0, c, hd * D:(hd + 1) * D, :]
            va_s[hd, D:DA, pl.ds(r0, L)] = ones_row
        return carry

    lax.fori_loop(0, NC, prep, 0)
    c_s[...] = jnp.zeros_like(c_s)

    si = lax.broadcasted_iota(jnp.int32, (L, L), 0)
    ti = lax.broadcasted_iota(jnp.int32, (L, L), 1)
    eye = si == ti
    dirs = ((0, 1, si <= ti, L - 1), (2, 3, si >= ti, 0))

    def chain(hd, dr, c, m):
        li_row, b_row, mask, g_lane = dirs[dr]
        r0 = pl.multiple_of(c * L, L)
        r_li = rows_s[hd, li_row:li_row + 1, pl.ds(r0, L)]
        r_b = rows_s[hd, b_row:b_row + 1, pl.ds(r0, L)]
        g_tot = r_b[:, g_lane:g_lane + 1]
        r_a = g_tot - r_b + r_li
        m_new = jnp.maximum(g_tot + m, jnp.max(r_a, axis=1, keepdims=True))
        decay = jnp.exp(g_tot + m - m_new)
        r_w = jnp.exp(r_a - m_new)
        a_col = jnp.sum(jnp.where(eye, r_li - r_b, 0.0), axis=1, keepdims=True)
        dmat = jnp.where(mask, a_col + r_b, NEG_BIG)
        inter_log = r_b + m
        m_t = jnp.maximum(inter_log, jnp.max(dmat, axis=0, keepdims=True))
        wts = jnp.exp(dmat - m_t)
        kc = k_ref[0, pl.ds(r0, L), hd * D:(hd + 1) * D]
        qtc = q_s[c, hd * D:(hd + 1) * D, :]
        vac = va_s[hd, :, pl.ds(r0, L)]
        cmat = c_s[hd, dr]
        both = jnp.dot(jnp.concatenate([kc, cmat.astype(BF16)], axis=0), qtc,
                       preferred_element_type=F32)
        s_w = (both[0:L] * wts).astype(BF16)
        tot = (jnp.dot(vac, s_w, preferred_element_type=F32)
               + jnp.exp(inter_log - m_t) * both[L:L + DA])
        den = jnp.maximum(jnp.abs(tot[D:D + 1]), jnp.exp(-m_t))
        h = tot[0:D] * (1.0 / den)
        h_s[hd * D:(hd + 1) * D, pl.ds(r0, L)] = h_s[hd * D:(hd + 1) * D, pl.ds(r0, L)] + h
        u = (vac.astype(F32) * r_w).astype(BF16)
        c_s[hd, dr] = decay * cmat + jnp.dot(u, kc, preferred_element_type=F32)
        return m_new

    def step(i, ms):
        out = []
        for hd in range(HB):
            out.append(chain(hd, 0, i, ms[2 * hd]))
            out.append(chain(hd, 1, NC - 1 - i, ms[2 * hd + 1]))
        return tuple(out)

    lax.fori_loop(0, NC, step, tuple(jnp.zeros((1, 1), F32) for _ in range(2 * HB)))

    def fin(c, carry):
        out_ref[0, c] = h_s[:, pl.ds(pl.multiple_of(c * L, L), L)].astype(BF16)
        return carry

    lax.fori_loop(0, NC, fin, 0, unroll=4)


def _mlstm(mt, k_tm, gcm, heads_per_step=4):
    B, S, _ = k_tm.shape
    NC = S // CHUNK
    D = HEAD_DIM
    HB = heads_per_step
    v0 = (M_WIDTH + 3 * H_WIDTH) // (HB * D)
    W = HB * D
    NB = M_HEADS // HB
    DA = D + BF16_ROWS
    cm_blk = (1, NC, W, LANES)
    return pl.pallas_call(
        _mlstm_kernel,
        grid=(B, NB),
        in_specs=[
            pl.BlockSpec(cm_blk, lambda b, h: (b, 0, h, 0)),
            pl.BlockSpec(cm_blk, lambda b, h: (b, 0, v0 + h, 0)),
            pl.BlockSpec((1, S, W), lambda b, h: (b, 0, h)),
            pl.BlockSpec((1, HB, 8, S), lambda b, h: (b, h, 0, 0)),
        ],
        out_specs=pl.BlockSpec(cm_blk, lambda b, h: (b, 0, h, 0)),
        out_shape=jax.ShapeDtypeStruct((B, NC, M_WIDTH, LANES), BF16),
        scratch_shapes=[
            pltpu.VMEM((NC, W, LANES), BF16),
            pltpu.VMEM((HB, DA, S), BF16),
            pltpu.VMEM((HB, 8, S), F32),
            pltpu.VMEM((W, S), F32),
            pltpu.VMEM((HB, 2, DA, D), F32),
        ],
        compiler_params=_cparams("parallel", "parallel"),
        name="mlstm",
    )(mt, mt, k_tm, gcm)


def _filt_mlp_kernel(w1t_ref, b1_ref, w2t_ref, b2_ref, w3t_ref, b3_ref, fr_ref, w4f_ref, w4b_ref,
                     out_ref, h3_s, *, cb_rows):
    S = h3_s.shape[2]
    hi = lax.Precision.HIGHEST
    lane = lax.broadcasted_iota(jnp.int32, (1, S), 1)
    pos_f = lane.astype(F32)
    pos_b = (S - lane).astype(F32)

    def features(pos):
        bands = (FILTER_EMB - 1) // 2
        t = pos / (S - 1)
        ang = (2.0 * math.pi) * pos / S
        fidx = lax.broadcasted_iota(jnp.int32, (bands, 1), 0).astype(F32)
        f = 1e-4 + fidx * ((bands - 1 - 1e-4) / (bands - 1))
        fa = f * ang
        w1t = w1t_ref[...]
        pre = (w1t[:, 0:1] * t
               + jnp.dot(w1t[:, 1:1 + bands], jnp.cos(fa), precision=hi, preferred_element_type=F32)
               - jnp.dot(w1t[:, 1 + bands:], jnp.sin(fa), precision=hi, preferred_element_type=F32))
        fr = fr_ref[...]
        h = jnp.sin(fr[:, 0:1] * (pre + b1_ref[...]))
        h = jnp.sin(fr[:, 1:2] * (jnp.dot(w2t_ref[...], h, precision=hi, preferred_element_type=F32)
                                  + b2_ref[...]))
        return jnp.sin(fr[:, 2:3] * (jnp.dot(w3t_ref[...], h, precision=hi, preferred_element_type=F32)
                                     + b3_ref[...]))

    @pl.when(pl.program_id(0) == 0)
    def _():
        h3 = features(pos_f)
        h3_s[0] = h3
        n_t = S // LANES
        anti = (lax.broadcasted_iota(jnp.int32, (LANES, LANES), 0)
                + lax.broadcasted_iota(jnp.int32, (LANES, LANES), 1) == LANES - 1).astype(F32)
        rev = jnp.concatenate(
            [jnp.dot(h3[:, (n_t - 1 - u) * LANES:(n_t - u) * LANES], anti, precision=hi,
                     preferred_element_type=F32) for u in range(n_t)], axis=1)
        h3_s[1] = pltpu.roll(rev, 1, 1)

    r = pl.program_id(0) * cb_rows + lax.broadcasted_iota(jnp.int32, (cb_rows, 1), 0)
    ch = (r % H_WIDTH).astype(F32)
    max_decay = math.log(DECAY_TARGET) / FAST_DECAY_PCT
    min_decay = math.log(DECAY_TARGET) / SLOW_DECAY_PCT
    delta = jnp.abs(min_decay + ch * ((max_decay - min_decay) / (H_WIDTH - 1)))
    n_t = S // LANES
    for half, (w_ref, pos) in enumerate(((w4f_ref, pos_f), (w4b_ref, pos_b))):
        filt = jnp.dot(w_ref[...].astype(BF16), h3_s[half].astype(BF16),
                       preferred_element_type=F32)
        filt = filt * jnp.exp(-(pos / (S - 1)) * delta)
        if half == 1:
            filt = jnp.where(lane == 0, 0.0, filt)
        for u in range(n_t):
            out_ref[half * n_t + u] = filt[:, u * LANES:(u + 1) * LANES].astype(BF16)


def _filt_mlp(w1t, b1, w2t, b2, w3t, b3, fr, w4t, S, cb_rows=256):
    R = w4t.shape[0] // 2
    Hd = FILTER_HIDDEN
    const = lambda i: (0, 0)
    nblk = R // cb_rows
    return pl.pallas_call(
        functools.partial(_filt_mlp_kernel, cb_rows=cb_rows),
        grid=(nblk,),
        in_specs=[
            pl.BlockSpec((Hd, FILTER_EMB), const),
            pl.BlockSpec((Hd, 1), const),
            pl.BlockSpec((Hd, Hd), const),
            pl.BlockSpec((Hd, 1), const),
            pl.BlockSpec((Hd, Hd), const),
            pl.BlockSpec((Hd, 1), const),
            pl.BlockSpec((Hd, 3), const),
            pl.BlockSpec((cb_rows, Hd), lambda i: (i, 0)),
            pl.BlockSpec((cb_rows, Hd), lambda i: (nblk + i, 0)),
        ],
        out_specs=pl.BlockSpec((2 * S // LANES, cb_rows, LANES), lambda i: (0, i, 0)),
        out_shape=jax.ShapeDtypeStruct((2 * S // LANES, R, LANES), BF16),
        scratch_shapes=[pltpu.VMEM((2, Hd, S), F32)],
        compiler_params=_cparams("arbitrary"),
        name="filt_mlp",
    )(w1t, b1, w2t, b2, w3t, b3, fr, w4t, w4t)


@functools.lru_cache(maxsize=None)
def _dft_constants(S):
    N = 2 * S
    N2 = LANES
    N1 = N // N2
    h = N1 // 2
    k1 = np.arange(N1)
    k2 = np.arange(N2)
    a1 = -2.0 * np.pi * np.outer(k1, k1) / N1
    f1r, f1i = np.cos(a1), np.sin(a1)
    at = -2.0 * np.pi * np.outer(k1, k2) / N
    twr, twi = np.cos(at), np.sin(at)
    a2 = -2.0 * np.pi * np.outer(k2, k2) / N2
    f2r, f2i = np.cos(a2), np.sin(a2)
    s1c = np.block([[f1r[:, :h], -f1i[:, :h]], [f1i[:, :h], f1r[:, :h]]])
    s1r = np.concatenate([f1r, f1i], axis=0)
    s2 = np.block([[f2r, f2i], [-f2i, f2r]])
    s2i = np.block([[f2r, -f2i], [f2i, f2r]])
    s1i = np.block([[f1r[:h, :], f1i[:h, :]], [-f1i[:h, :], f1r[:h, :]]])
    cast = lambda a: np.asarray(a, np.float32)
    return dict(s1c=cast(s1c), s1r=cast(s1r), s2=cast(s2), s2i=cast(s2i), s1i=cast(s1i),
                twr=cast(twr), twi=cast(twi), N1=N1, h=h)


def _bf(a):
    return jnp.asarray(a, F32).astype(BF16)


def _filt_fft_kernel(bias_ref, kern_ref, s1r_ref, s2_ref, twr_ref, twi_ref, out_ref,
                     src_s, slab_s, *, unroll):
    N1 = twr_ref.shape[0]
    n_ch = kern_ref.shape[1]
    pitch = src_s.shape[0] // N1
    scale = 1.0 / (N1 * LANES)
    twr = twr_ref[...]
    twi = twi_ref[...]
    s1r = s1r_ref[...]

    def load_in(t1, carry):
        src_s[pl.ds(pl.multiple_of(t1 * pitch, 8), n_ch), :] = kern_ref[t1].astype(F32)
        return carry
    lax.fori_loop(0, N1, load_in, 0, unroll=4)

    def per_group(j, carry):
        for u in range(0, unroll, 2):
            cs = (j * unroll + u, j * unroll + u + 1)
            z = jnp.concatenate([src_s[pl.ds(c, N1, stride=pitch), :] for c in cs], axis=1)
            a = jnp.dot(s1r, z.astype(BF16), preferred_element_type=F32)
            for i, c in enumerate(cs):
                ar = a[0:N1, i * LANES:(i + 1) * LANES]
                ai = a[N1:2 * N1, i * LANES:(i + 1) * LANES]
                r0 = pl.multiple_of(c * N1, N1)
                slab_s[pl.ds(r0, N1), 0:LANES] = (ar * twr - ai * twi).astype(BF16)
                slab_s[pl.ds(r0, N1), LANES:2 * LANES] = (ar * twi + ai * twr).astype(BF16)
        return carry

    lax.fori_loop(0, n_ch // unroll, per_group, 0)
    x = jnp.dot(slab_s[...], s2_ref[...], preferred_element_type=F32)
    bias = bias_ref[...]
    out_ref[:, :, 0:LANES] = ((x[:, 0:LANES].reshape(n_ch, N1, LANES) + bias) * scale).astype(BF16)
    out_ref[:, :, LANES:2 * LANES] = (x[:, LANES:2 * LANES].reshape(n_ch, N1, LANES)
                                      * scale).astype(BF16)


def _filt_fft(bias3, kern, consts, n_ch=32, unroll=32):
    N1, R, _ = kern.shape
    const = lambda i: (0, 0)
    pitch = n_ch + 8
    return pl.pallas_call(
        functools.partial(_filt_fft_kernel, unroll=unroll),
        grid=(R // n_ch,),
        in_specs=[
            pl.BlockSpec((n_ch, 1, 1), lambda i: (i, 0, 0)),
            pl.BlockSpec((N1, n_ch, LANES), lambda i: (0, i, 0)),
            pl.BlockSpec((2 * N1, N1), const),
            pl.BlockSpec((2 * LANES, 2 * LANES), const),
            pl.BlockSpec((N1, LANES), const),
            pl.BlockSpec((N1, LANES), const),
        ],
        out_specs=pl.BlockSpec((n_ch, N1, 2 * LANES), lambda i: (i, 0, 0)),
        out_shape=jax.ShapeDtypeStruct((R, N1, 2 * LANES), BF16),
        scratch_shapes=[pltpu.VMEM((N1 * pitch, LANES), F32),
                        pltpu.VMEM((n_ch * N1, 2 * LANES), BF16)],
        compiler_params=_cparams("parallel"),
        name="filt_fft",
    )(bias3, kern, _bf(consts["s1r"]), _bf(consts["s2"]),
      jnp.asarray(consts["twr"]), jnp.asarray(consts["twi"]))


def _hyena_kernel(nw_ref, v_ref, x1_ref, x2_ref, khat_ref, s1c_ref, s2_ref, s2i_ref, s1i_ref,
                  twr_ref, twi_ref, out_ref, src_s, slab_s, yf_s, q_s, y_s, *, unroll):
    h = v_ref.shape[1]
    n_ch = v_ref.shape[2]
    N1 = 2 * h
    twr = twr_ref[...]
    twi = twi_ref[...]
    s1c = s1c_ref[...]
    s1i = s1i_ref[...]
    n_groups = n_ch // unroll
    rows = unroll * N1
    pitch = src_s.shape[0] // (2 * h)

    def tile_rows(i):
        return pl.ds(pl.multiple_of(i * pitch, 8), n_ch)

    def chan_rows(c, bb):
        return pl.ds(bb * h * pitch + c, h, stride=pitch)

    def long_conv(order):
        def stage_a(g):
            for u in range(0, unroll, 2):
                cs = (g * unroll + u, g * unroll + u + 1)
                z = jnp.concatenate(
                    [jnp.concatenate([src_s[chan_rows(c, 0), :], src_s[chan_rows(c, 1), :]], axis=0)
                     for c in cs], axis=1)
                a = jnp.dot(s1c, z.astype(BF16), preferred_element_type=F32)
                for i, c in enumerate(cs):
                    ar = a[0:N1, i * LANES:(i + 1) * LANES]
                    ai = a[N1:2 * N1, i * LANES:(i + 1) * LANES]
                    r0 = c * N1 if isinstance(c, int) else pl.multiple_of(c * N1, N1)
                    slab_s[pl.ds(r0, N1), 0:LANES] = (ar * twr - ai * twi).astype(BF16)
                    slab_s[pl.ds(r0, N1), LANES:2 * LANES] = (ar * twi + ai * twr).astype(BF16)

        def stage_m1(g):
            r0 = g * rows if isinstance(g, int) else pl.multiple_of(g * rows, rows)
            c0 = g * unroll if isinstance(g, int) else pl.multiple_of(g * unroll, unroll)
            x = jnp.dot(slab_s[pl.ds(r0, rows), :], s2_ref[...], preferred_element_type=F32)
            xr = x[:, 0:LANES].reshape(unroll, N1, LANES)
            xi = x[:, LANES:2 * LANES].reshape(unroll, N1, LANES)
            kr = khat_ref[order, pl.ds(c0, unroll), :, 0:LANES].astype(F32)
            ki = khat_ref[order, pl.ds(c0, unroll), :, LANES:2 * LANES].astype(F32)
            yf_s[pl.ds(r0, rows), 0:LANES] = (xr * kr - xi * ki).reshape(rows, LANES).astype(BF16)
            yf_s[pl.ds(r0, rows), LANES:2 * LANES] = (xr * ki + xi * kr).reshape(rows, LANES).astype(BF16)

        def stage_m2(g):
            r0 = g * rows if isinstance(g, int) else pl.multiple_of(g * rows, rows)
            c0 = g * unroll if isinstance(g, int) else pl.multiple_of(g * unroll, unroll)
            p = jnp.dot(yf_s[pl.ds(r0, rows), :], s2i_ref[...], preferred_element_type=F32)
            pr = p[:, 0:LANES].reshape(unroll, N1, LANES)
            pi = p[:, LANES:2 * LANES].reshape(unroll, N1, LANES)
            q_s[pl.ds(c0, unroll), 0:N1, :] = (pr * twr + pi * twi).astype(BF16)
            q_s[pl.ds(c0, unroll), N1:2 * N1, :] = (pi * twr - pr * twi).astype(BF16)

        def stage_i(g):
            for u in range(0, unroll, 2):
                cs = (g * unroll + u, g * unroll + u + 1)
                q2 = jnp.concatenate([q_s[c] for c in cs], axis=1)
                y = jnp.dot(s1i, q2, preferred_element_type=F32)
                for i, c in enumerate(cs):
                    y_s[chan_rows(c, 0), :] = y[0:h, i * LANES:(i + 1) * LANES]
                    y_s[chan_rows(c, 1), :] = y[h:2 * h, i * LANES:(i + 1) * LANES]

        G = n_groups
        for t in range(G + 3):
            if 0 <= t - 3 < G:
                stage_i(t - 3)
            if 0 <= t - 2 < G:
                stage_m2(t - 2)
            if 0 <= t - 1 < G:
                stage_m1(t - 1)
            if t < G:
                stage_a(t)

    def per_tile(fn):
        for bb in range(2):
            def body(t1, carry, bb=bb):
                fn(bb, t1, tile_rows(bb * h + t1))
                return carry
            lax.fori_loop(0, h, body, 0, unroll=8)

    def load_in(bb, t1, rws):
        src_s[rws, :] = v_ref[bb, t1].astype(F32)
    per_tile(load_in)
    long_conv(0)

    def gate1(bb, t1, rws):
        src_s[rws, :] = y_s[rws, :] * x1_ref[bb, t1].astype(F32)
    per_tile(gate1)
    long_conv(1)

    nw = jnp.broadcast_to(nw_ref[...], (n_ch, LANES))

    def gate2_norm(bb, t1, rws):
        z = y_s[rws, :] * x2_ref[bb, t1].astype(F32)
        ms = jnp.mean(z * z, axis=0, keepdims=True)
        out_ref[bb, t1] = (z * lax.rsqrt(ms + EPS) * nw).astype(BF16)
    per_tile(gate2_norm)


def _hyena(nw_col, hy4, khat, consts, unroll=8):
    B, h, _, _ = hy4.shape
    C = H_WIDTH
    n_ch = GROUP_W
    G = C // n_ch
    g0 = M_WIDTH // n_ch
    N1 = consts["N1"]
    const = lambda g, p: (0, 0)
    blk = (2, h, n_ch, LANES)
    pitch = n_ch + 8
    return pl.pallas_call(
        functools.partial(_hyena_kernel, unroll=unroll),
        grid=(G, B // 2),
        in_specs=[
            pl.BlockSpec((n_ch, 1), lambda g, p: (g, 0)),
            pl.BlockSpec(blk, lambda g, p: (p, 0, g0 + g, 0)),
            pl.BlockSpec(blk, lambda g, p: (p, 0, g0 + G + g, 0)),
            pl.BlockSpec(blk, lambda g, p: (p, 0, g0 + 2 * G + g, 0)),
            pl.BlockSpec((2, n_ch, N1, 2 * LANES), lambda g, p: (0, g, 0, 0)),
            pl.BlockSpec((2 * N1, 2 * h), const),
            pl.BlockSpec((2 * LANES, 2 * LANES), const),
            pl.BlockSpec((2 * LANES, 2 * LANES), const),
            pl.BlockSpec((2 * h, 2 * N1), const),
            pl.BlockSpec((N1, LANES), const),
            pl.BlockSpec((N1, LANES), const),
        ],
        out_specs=pl.BlockSpec(blk, lambda g, p: (p, 0, g, 0)),
        out_shape=jax.ShapeDtypeStruct((B, h, C, LANES), BF16),
        scratch_shapes=[
            pltpu.VMEM((2 * h * pitch, LANES), F32),
            pltpu.VMEM((n_ch * N1, 2 * LANES), BF16),
            pltpu.VMEM((n_ch * N1, 2 * LANES), BF16),
            pltpu.VMEM((n_ch, 2 * N1, LANES), BF16),
            pltpu.VMEM((2 * h * pitch, LANES), F32),
        ],
        compiler_params=_cparams("parallel", "arbitrary"),
        name="hyena",
    )(nw_col, hy4, hy4, hy4, khat,
      _bf(consts["s1c"]), _bf(consts["s2"]), _bf(consts["s2i"]), _bf(consts["s1i"]),
      jnp.asarray(consts["twr"]), jnp.asarray(consts["twi"]))


def _outmlp_kernel(x_ref, hm_ref, og_ref, yh_ref, nwm_ref, wo_ref, w1_ref, w2_ref, n_post_ref,
                   n_pre_ref, n_post2_ref, out_ref):
    n_t = hm_ref.shape[1]
    nwm = jnp.broadcast_to(nwm_ref[...], (M_WIDTH, LANES))
    tiles = []
    for j in range(n_t):
        hg = hm_ref[0, j].astype(F32) * _sigmoid(og_ref[0, j].astype(F32))
        heads = []
        for hd in range(M_HEADS):
            hh = hg[hd * HEAD_DIM:(hd + 1) * HEAD_DIM]
            ms = jnp.mean(hh * hh, axis=0, keepdims=True)
            heads.append(hh * lax.rsqrt(ms + EPS))
        ym = jnp.concatenate(heads, axis=0) * nwm
        yt = jnp.concatenate([ym, yh_ref[0, j].astype(F32)], axis=0)
        tiles.append(yt.T.astype(BF16))
    y = tiles[0] if n_t == 1 else jnp.concatenate(tiles, axis=0)
    mix = jnp.dot(y, wo_ref[...], preferred_element_type=F32)
    x1 = x_ref[0] + _rms_rows(mix, n_post_ref[...])
    hm = _rms_rows(x1, n_pre_ref[...]).astype(BF16)
    mid = jnp.maximum(jnp.dot(hm, w1_ref[...], preferred_element_type=F32), 0.0)
    mid = (mid * mid).astype(BF16)
    ff = jnp.dot(mid, w2_ref[...], preferred_element_type=F32)
    out_ref[0] = x1 + _rms_rows(ff, n_post2_ref[...])


def _outmlp(x, hm, mt, yh, nwm_col, wo, w1, w2, n_post, n_pre, n_post2, tm_rows):
    B, S, D = x.shape
    n_t = tm_rows // LANES
    og0 = mt.shape[2] // M_WIDTH - 1
    const = lambda b, i: (0, 0)
    resident = functools.partial(pl.BlockSpec, index_map=const, pipeline_mode=pl.Buffered(1))
    return pl.pallas_call(
        _outmlp_kernel,
        grid=(B, S // tm_rows),
        in_specs=[
            pl.BlockSpec((1, tm_rows, D), lambda b, i: (b, i, 0)),
            pl.BlockSpec((1, n_t, M_WIDTH, LANES), lambda b, i: (b, i, 0, 0)),
            pl.BlockSpec((1, n_t, M_WIDTH, LANES), lambda b, i: (b, i, og0, 0)),
            pl.BlockSpec((1, n_t, H_WIDTH, LANES), lambda b, i: (b, i, 0, 0)),
            pl.BlockSpec((M_WIDTH, 1), const),
            resident((D, D)),
            resident((D, D_FF)),
            resident((D_FF, D)),
            pl.BlockSpec((1, D), const),
            pl.BlockSpec((1, D), const),
            pl.BlockSpec((1, D), const),
        ],
        out_specs=pl.BlockSpec((1, tm_rows, D), lambda b, i: (b, i, 0)),
        out_shape=jax.ShapeDtypeStruct((B, S, D), F32),
        compiler_params=_cparams("parallel", "parallel"),
        name="outmlp",
    )(x, hm, mt, yh, nwm_col, wo, w1, w2, n_post, n_pre, n_post2)


def kernel(x, norm_mix_pre, norm_mix_post, norm_mlp_pre, norm_mlp_post, w_in, b_gates,
           conv_w, conv_b, mlstm_norm_w, hyena_norm_w, filt_w1, filt_b1, filt_w2, filt_b2,
           filt_w3, filt_b3, filt_w4, filt_freq, filt_bias, w_out, w_mlp_in, w_mlp_out):
    B, S, D = x.shape
    assert D == D_MODEL and B % 2 == 0 and S % CHUNK == 0
    H = M_HEADS
    row = lambda a: a.astype(F32).reshape(1, -1)
    col = lambda a: a.astype(F32).reshape(-1, 1)
    tm_rows = min(1024, S)

    n_conv = 2 * M_WIDTH + 3 * H_WIDTH
    o_gate = n_conv + 2 * M_WIDTH
    assert w_in.shape == (D, o_gate + N_GATE)
    cw = conv_w.astype(F32)
    cbias = conv_b.astype(F32)
    ident = jnp.zeros((3, 2 * M_WIDTH), F32).at[1].set(1.0)
    cw_cm = jnp.concatenate([cw[:, 0:M_WIDTH], cw[:, 2 * M_WIDTH:n_conv], ident], axis=1)
    cb_cm = jnp.concatenate([cbias[0:M_WIDTH], cbias[2 * M_WIDTH:n_conv],
                             jnp.zeros((2 * M_WIDTH,), F32)]).reshape(1, -1)
    mt, k_tm, gcm = _proj(x, row(norm_mix_pre), w_in.astype(F32).T, col(b_gates), cw_cm, cb_cm,
                          cw[:, M_WIDTH:2 * M_WIDTH], row(cbias[M_WIDTH:2 * M_WIDTH]), tm_rows)

    g4 = gcm.reshape(B, 4, H, S).transpose(0, 2, 1, 3)
    g8 = jnp.concatenate([g4, jnp.zeros_like(g4)], axis=2)
    h_m = _mlstm(mt, k_tm, g8)

    consts = _dft_constants(S)
    kern = _filt_mlp(filt_w1.astype(F32).T, col(filt_b1), filt_w2.astype(F32).T, col(filt_b2),
                     filt_w3.astype(F32).T, col(filt_b3), filt_freq.astype(F32).T,
                     filt_w4.astype(F32).T, S)
    khat = _filt_fft(filt_bias.astype(F32).reshape(2 * H_WIDTH, 1, 1), kern, consts)
    khat = khat.reshape(2, H_WIDTH, consts["N1"], 2 * LANES)

    y_h = _hyena(col(hyena_norm_w), mt, khat, consts)

    return _outmlp(x, h_m, mt, y_h, col(mlstm_norm_w),
                   w_out.astype(BF16), w_mlp_in.astype(BF16), w_mlp_out.astype(BF16),
                   row(norm_mix_post), row(norm_mlp_pre), row(norm_mlp_post), min(512, S))
```

```python
import functools
import math

import numpy as np
import jax
import jax.numpy as jnp
from jax import lax
from jax.experimental import pallas as pl
from jax.experimental.pallas import tpu as pltpu

F32 = jnp.float32
BF16 = jnp.bfloat16

D_MODEL = 1024
M_WIDTH = 512
M_HEADS = 4
HEAD_DIM = 128
H_WIDTH = 512
H_GROUPS = 8
GROUP_W = H_WIDTH // H_GROUPS
CHUNK = 128
FILTER_EMB = 33
FILTER_HIDDEN = 64
DECAY_TARGET = 1e-2
FAST_DECAY_PCT = 0.3
SLOW_DECAY_PCT = 1.5
D_FF = 4 * D_MODEL
N_GATE = 16
EPS = 1e-6
LANES = 128
BF16_ROWS = 16
NEG_BIG = -1e30
VMEM_LIMIT = 56 * 1024 * 1024


def _cparams(*sem):
    return pltpu.CompilerParams(dimension_semantics=sem, vmem_limit_bytes=VMEM_LIMIT)


def _rms_rows(xf, w):
    ms = jnp.mean(xf * xf, axis=-1, keepdims=True)
    return xf * lax.rsqrt(ms + EPS) * w


def _sigmoid(x):
    return 1.0 / (1.0 + jnp.exp(-x))


def _log_sigmoid(x):
    return jnp.minimum(x, 0.0) - jnp.log(1.0 + jnp.exp(-jnp.abs(x)))


def _proj_kernel(x_ref, xp_ref, xn_ref, nw_ref, wt_ref, bgt_ref, cw_ref, cb_ref, cwk_ref, cbk_ref,
                 cm_ref, k_ref, gcm_ref, wbf_s, hall_s, pc_s, *, col_blk):
    TM = x_ref.shape[1]
    HALO = xp_ref.shape[1]
    j = pl.program_id(1)

    @pl.when(jnp.logical_and(pl.program_id(0) == 0, j == 0))
    def _():
        for c0 in range(0, wbf_s.shape[1], col_blk):
            wbf_s[:, c0:c0 + col_blk] = wt_ref[c0:c0 + col_blk, :].T.astype(BF16)

    nw = nw_ref[...]
    hn = _rms_rows(x_ref[0], nw).astype(BF16)
    hall_s[0:HALO, :] = _rms_rows(xp_ref[0], nw).astype(BF16)
    hall_s[HALO:HALO + TM, :] = hn
    hall_s[HALO + TM:2 * HALO + TM, :] = _rms_rows(xn_ref[0], nw).astype(BF16)
    keep_p = jnp.where(j > 0, 1.0, 0.0)
    keep_n = jnp.where(j < pl.num_programs(1) - 1, 1.0, 0.0)
    n_cm = cw_ref.shape[1]
    n_blk = n_cm // col_blk
    w_k0 = M_WIDTH
    w_gate = n_cm + M_WIDTH
    nt_dims = (((1,), (1,)), ((), ()))
    gt = lax.dot_general(wt_ref[w_gate:w_gate + N_GATE, :].astype(BF16), hn, nt_dims,
                         preferred_element_type=F32) + bgt_ref[...]
    row = lax.broadcasted_iota(jnp.int32, gt.shape, 0)
    gcm_ref[0] = jnp.where((row % 8) >= 4, _log_sigmoid(gt), gt)

    n_rb = TM // LANES
    n_sl = col_blk // LANES

    def project(w_row0, slot):
        wblk = wbf_s[:, pl.ds(w_row0, col_blk)]
        res = jnp.dot(hall_s[...], wblk, preferred_element_type=F32)
        for u in range(n_sl):
            ls = slice(u * LANES, (u + 1) * LANES)
            pc_s[slot, u, 0:HALO, :] = res[0:HALO, ls] * keep_p
            pc_s[slot, u, HALO:HALO + TM, :] = res[HALO:HALO + TM, ls]
            pc_s[slot, u, HALO + TM:2 * HALO + TM, :] = res[HALO + TM:2 * HALO + TM, ls] * keep_n

    def conv_tile(slot, u, rb, taps, bias):
        r = HALO + rb * LANES
        return (pc_s[slot, u, pl.ds(r - 1, LANES, stride=1), :] * taps[0:1]
                + pc_s[slot, u, r:r + LANES, :] * taps[1:2]
                + pc_s[slot, u, pl.ds(r + 1, LANES, stride=1), :] * taps[2:3] + bias)

    n_q = M_WIDTH // col_blk
    per_trip = 12
    assert n_blk % per_trip == 0

    def w_row_of(c):
        row_blk = jnp.where(c < n_q, c, c + n_q)
        return jnp.where(c == n_blk, w_k0 // col_blk, row_blk) * col_blk

    def emit(c, slot):
        for u in range(n_sl):
            col0 = pl.multiple_of(c * col_blk + u * LANES, LANES)
            taps = cw_ref[:, pl.ds(col0, LANES)]
            bias = cb_ref[:, pl.ds(col0, LANES)]
            for rb in range(n_rb):
                cm_ref[0, rb, pl.ds(col0, LANES), :] = conv_tile(slot, u, rb, taps, bias).T.astype(BF16)

    def body(i, carry):
        for u in range(per_trip):
            c = per_trip * i + u
            emit(c, u % 2)
            project(pl.multiple_of(w_row_of(c + 1), col_blk), (u + 1) % 2)
        return carry

    project(0, 0)
    lax.fori_loop(0, n_blk // per_trip, body, 0)

    k_scale = HEAD_DIM ** -0.5
    n_k = M_WIDTH // col_blk
    for kb in range(n_k):
        if kb > 0:
            project(w_k0 + kb * col_blk, kb % 2)
        for u in range(n_sl):
            cs = slice(kb * col_blk + u * LANES, kb * col_blk + (u + 1) * LANES)
            for rb in range(n_rb):
                cv = conv_tile(kb % 2, u, rb, cwk_ref[:, cs], cbk_ref[:, cs])
                k_ref[0, rb * LANES:(rb + 1) * LANES, cs] = (cv * _sigmoid(cv) * k_scale).astype(BF16)


def _proj(x, nw, w_in_t, bgt, cw_cm, cb_cm, cw_k, cb_k, tm_rows, col_blk=256):
    B, S, D = x.shape
    HALO = BF16_ROWS
    n_cm = cw_cm.shape[1]
    n_t = tm_rows // LANES
    hb = tm_rows // HALO
    last = S // HALO - 1
    const = lambda b, j: (0, 0)
    resident = functools.partial(pl.BlockSpec, index_map=const, pipeline_mode=pl.Buffered(1))
    return pl.pallas_call(
        functools.partial(_proj_kernel, col_blk=col_blk),
        grid=(B, S // tm_rows),
        in_specs=[
            pl.BlockSpec((1, tm_rows, D), lambda b, j: (b, j, 0)),
            pl.BlockSpec((1, HALO, D), lambda b, j: (b, jnp.maximum(j * hb - 1, 0), 0)),
            pl.BlockSpec((1, HALO, D), lambda b, j: (b, jnp.minimum((j + 1) * hb, last), 0)),
            pl.BlockSpec((1, D), const),
            resident(w_in_t.shape),
            pl.BlockSpec((N_GATE, 1), const),
            pl.BlockSpec((3, n_cm), const),
            pl.BlockSpec((1, n_cm), const),
            pl.BlockSpec((3, M_WIDTH), const),
            pl.BlockSpec((1, M_WIDTH), const),
        ],
        out_specs=[
            pl.BlockSpec((1, n_t, n_cm, LANES), lambda b, j: (b, j, 0, 0)),
            pl.BlockSpec((1, tm_rows, M_WIDTH), lambda b, j: (b, j, 0)),
            pl.BlockSpec((1, N_GATE, tm_rows), lambda b, j: (b, 0, j)),
        ],
        out_shape=[
            jax.ShapeDtypeStruct((B, S // LANES, n_cm, LANES), BF16),
            jax.ShapeDtypeStruct((B, S, M_WIDTH), BF16),
            jax.ShapeDtypeStruct((B, N_GATE, S), F32),
        ],
        scratch_shapes=[pltpu.VMEM((D, n_cm + M_WIDTH), BF16),
                        pltpu.VMEM((tm_rows + 2 * HALO, D), BF16),
                        pltpu.VMEM((2, col_blk // LANES, tm_rows + 2 * HALO, LANES), F32)],
        compiler_params=_cparams("arbitrary", "arbitrary"),
        name="proj",
    )(x, x, x, nw, w_in_t, bgt, cw_cm, cb_cm, cw_k, cb_k)


def _mlstm_kernel(qt_ref, vt_ref, k_ref, g_ref, out_ref, q_s, va_s, rows_s, stab_s, h_s, c_s):
    S = k_ref.shape[1]
    HB = g_ref.shape[1]
    L = CHUNK
    NC = S // L
    D = HEAD_DIM
    DA = D + BF16_ROWS
    W = HB * D

    R_A, R_MX, R_W, R_DEC, R_ISC, R_EMT, R_GT, R_AMAX = range(8)

    def scan(x, op, fill):
        lane = lax.broadcasted_iota(jnp.int32, x.shape, 1) % L
        pre = x
        suf = x
        d = 1
        while d < L:
            pre = op(pre, jnp.where(lane >= d, pltpu.roll(pre, d, 1), fill))
            suf = op(suf, jnp.where(lane < L - d, pltpu.roll(suf, S - d, 1), fill))
            d *= 2
        return pre, suf

    chains = [(hd, dr) for hd in range(HB) for dr in range(2)]
    assert HB % 2 == 0 and len(chains) <= 8
    a_rows = {}
    for h0 in range(0, HB, 2):
        g = jnp.concatenate([g_ref[0, h0, 0:4, :], g_ref[0, h0 + 1, 0:4, :]], axis=0)
        pre, suf = scan(g, jnp.add, 0.0)
        tot = pre + suf - g
        for j, hd in enumerate((h0, h0 + 1)):
            o = 4 * j
            b_rows = (pre[o + 1:o + 2], suf[o + 3:o + 4])
            for dr in range(2):
                a_rows[hd, dr] = g[o + 2 * dr:o + 2 * dr + 1] - b_rows[dr]
                rows_s[hd, dr, R_A:R_A + 1, :] = a_rows[hd, dr]
                rows_s[hd, dr, R_EMT:R_EMT + 1, :] = b_rows[dr]
                rows_s[hd, dr, R_GT:R_GT + 1, :] = tot[o + 2 * dr + 1:o + 2 * dr + 2]
    a8 = jnp.concatenate([a_rows[ch] for ch in chains]
                         + [jnp.zeros((8 - len(chains), S), F32)] * (len(chains) < 8), axis=0)
    pmax, smax = scan(a8, jnp.maximum, NEG_BIG)
    for i, (hd, dr) in enumerate(chains):
        rows_s[hd, dr, R_MX:R_MX + 1, :] = pmax[i:i + 1] if dr == 0 else smax[i:i + 1]
        rows_s[hd, dr, R_AMAX:R_AMAX + 1, :] = jnp.maximum(pmax[i:i + 1], smax[i:i + 1])

    def chunk_of(dr, i):
        return i if dr == 0 else NC - 1 - i

    def stabilisers(i, ms):
        out = []
        for (hd, dr), m in zip(chains, ms):
            r0 = pl.multiple_of(chunk_of(dr, i) * L, L)
            gt = rows_s[hd, dr, R_GT:R_GT + 1, pl.ds(r0, L)]
            m_end = jnp.maximum(gt + m, gt + rows_s[hd, dr, R_AMAX:R_AMAX + 1, pl.ds(r0, L)])
            stab_s[hd, dr, 0:1, pl.ds(r0, L)] = m
            stab_s[hd, dr, 1:2, pl.ds(r0, L)] = m_end
            out.append(m_end)
        return tuple(out)

    lax.fori_loop(0, NC, stabilisers, tuple(jnp.zeros((1, L), F32) for _ in chains), unroll=4)

    for hd, dr in chains:
        a = rows_s[hd, dr, R_A:R_A + 1, :]
        gt = rows_s[hd, dr, R_GT:R_GT + 1, :]
        m = stab_s[hd, dr, 0:1, :]
        m_end = stab_s[hd, dr, 1:2, :]
        mx = jnp.maximum(m, rows_s[hd, dr, R_MX:R_MX + 1, :])
        b = rows_s[hd, dr, R_EMT:R_EMT + 1, :]
        rows_s[hd, dr, R_MX:R_MX + 1, :] = mx
        rows_s[hd, dr, R_W:R_W + 1, :] = jnp.exp(gt + a - m_end)
        rows_s[hd, dr, R_DEC:R_DEC + 1, :] = jnp.exp(gt + m - m_end)
        rows_s[hd, dr, R_ISC:R_ISC + 1, :] = jnp.exp(m - mx)
        rows_s[hd, dr, R_EMT:R_EMT + 1, :] = jnp.exp(-(b + mx))

    ones_row = (lax.broadcasted_iota(jnp.int32, (BF16_ROWS, L), 0) == 0).astype(BF16)

    def prep(c, carry):
        r0 = pl.multiple_of(c * L, L)
        h_s[:, pl.ds(r0, L)] = jnp.zeros((W, L), F32)
        qpre = qt_ref[0, c].astype(F32)
        q_s[c] = (qpre * _sigmoid(qpre)).astype(BF16)
        for hd in range(HB):
            va_s[hd, 0:D, pl.ds(r0, L)] = vt_ref[0, c, hd * D:(hd + 1) * D, :]
            va_s[hd, D:DA, pl.ds(r0, L)] = ones_row
        return carry

    lax.fori_loop(0, NC, prep, 0)
    c_s[...] = jnp.zeros_like(c_s)

    si = lax.broadcasted_iota(jnp.int32, (L, L), 0)
    ti = lax.broadcasted_iota(jnp.int32, (L, L), 1)
    eye = si == ti
    valid = (si <= ti, si >= ti)

    def chain(hd, dr, c):
        r0 = pl.multiple_of(c * L, L)
        rows = rows_s[hd, dr, :, pl.ds(r0, L)]
        a_col = jnp.sum(jnp.where(eye, rows[R_A:R_A + 1], 0.0), axis=1, keepdims=True)
        wts = jnp.exp(jnp.where(valid[dr], a_col - rows[R_MX:R_MX + 1], NEG_BIG))
        kc = k_ref[0, pl.ds(r0, L), hd * D:(hd + 1) * D]
        qtc = q_s[c, hd * D:(hd + 1) * D, :]
        vac = va_s[hd, :, pl.ds(r0, L)]
        cmat = c_s[hd, dr]
        both = jnp.dot(jnp.concatenate([kc, cmat.astype(BF16)], axis=0), qtc,
                       preferred_element_type=F32)
        s_w = (both[0:L] * wts).astype(BF16)
        tot = (jnp.dot(vac, s_w, preferred_element_type=F32)
               + rows[R_ISC:R_ISC + 1] * both[L:L + DA])
        den = jnp.maximum(jnp.abs(tot[D:D + 1]), rows[R_EMT:R_EMT + 1])
        h = tot[0:D] * (1.0 / den)
        h_s[hd * D:(hd + 1) * D, pl.ds(r0, L)] = h_s[hd * D:(hd + 1) * D, pl.ds(r0, L)] + h
        u = (vac.astype(F32) * rows[R_W:R_W + 1]).astype(BF16)
        c_s[hd, dr] = rows[R_DEC:R_DEC + 1, 0:1] * cmat + jnp.dot(u, kc, preferred_element_type=F32)

    def step(i, carry):
        for hd, dr in chains:
            chain(hd, dr, chunk_of(dr, i))
        return carry

    lax.fori_loop(0, NC, step, 0)

    def fin(c, carry):
        out_ref[0, c] = h_s[:, pl.ds(pl.multiple_of(c * L, L), L)].astype(BF16)
        return carry

    lax.fori_loop(0, NC, fin, 0, unroll=4)


def _mlstm(mt, k_tm, gcm, heads_per_step=4):
    B, S, _ = k_tm.shape
    NC = S // CHUNK
    D = HEAD_DIM
    HB = heads_per_step
    v0 = (M_WIDTH + 3 * H_WIDTH) // (HB * D)
    W = HB * D
    NB = M_HEADS // HB
    DA = D + BF16_ROWS
    cm_blk = (1, NC, W, LANES)
    return pl.pallas_call(
        _mlstm_kernel,
        grid=(B, NB),
        in_specs=[
            pl.BlockSpec(cm_blk, lambda b, h: (b, 0, h, 0)),
            pl.BlockSpec(cm_blk, lambda b, h: (b, 0, v0 + h, 0)),
            pl.BlockSpec((1, S, W), lambda b, h: (b, 0, h)),
            pl.BlockSpec((1, HB, 8, S), lambda b, h: (b, h, 0, 0)),
        ],
        out_specs=pl.BlockSpec(cm_blk, lambda b, h: (b, 0, h, 0)),
        out_shape=jax.ShapeDtypeStruct((B, NC, M_WIDTH, LANES), BF16),
        scratch_shapes=[
            pltpu.VMEM((NC, W, LANES), BF16),
            pltpu.VMEM((HB, DA, S), BF16),
            pltpu.VMEM((HB, 2, 8, S), F32),
            pltpu.VMEM((HB, 2, 8, S), F32),
            pltpu.VMEM((W, S), F32),
            pltpu.VMEM((HB, 2, DA, D), F32),
        ],
        compiler_params=_cparams("parallel", "parallel"),
        name="mlstm",
    )(mt, mt, k_tm, gcm)


def _filt_mlp_kernel(w1t_ref, b1_ref, w2t_ref, b2_ref, w3t_ref, b3_ref, fr_ref, w4f_ref, w4b_ref,
                     out_ref, h3_s, *, cb_rows):
    S = h3_s.shape[2]
    hi = lax.Precision.HIGHEST
    lane = lax.broadcasted_iota(jnp.int32, (1, S), 1)
    pos_f = lane.astype(F32)
    pos_b = (S - lane).astype(F32)

    def features(pos):
        bands = (FILTER_EMB - 1) // 2
        t = pos / (S - 1)
        ang = (2.0 * math.pi) * pos / S
        fidx = lax.broadcasted_iota(jnp.int32, (bands, 1), 0).astype(F32)
        f = 1e-4 + fidx * ((bands - 1 - 1e-4) / (bands - 1))
        fa = f * ang
        w1t = w1t_ref[...]
        pre = (w1t[:, 0:1] * t
               + jnp.dot(w1t[:, 1:1 + bands], jnp.cos(fa), precision=hi, preferred_element_type=F32)
               - jnp.dot(w1t[:, 1 + bands:], jnp.sin(fa), precision=hi, preferred_element_type=F32))
        fr = fr_ref[...]
        h = jnp.sin(fr[:, 0:1] * (pre + b1_ref[...]))
        h = jnp.sin(fr[:, 1:2] * (jnp.dot(w2t_ref[...], h, precision=hi, preferred_element_type=F32)
                                  + b2_ref[...]))
        return jnp.sin(fr[:, 2:3] * (jnp.dot(w3t_ref[...], h, precision=hi, preferred_element_type=F32)
                                     + b3_ref[...]))

    @pl.when(pl.program_id(0) == 0)
    def _():
        h3 = features(pos_f)
        h3_s[0] = h3
        n_t = S // LANES
        anti = (lax.broadcasted_iota(jnp.int32, (LANES, LANES), 0)
                + lax.broadcasted_iota(jnp.int32, (LANES, LANES), 1) == LANES - 1).astype(F32)
        rev = jnp.concatenate(
            [jnp.dot(h3[:, (n_t - 1 - u) * LANES:(n_t - u) * LANES], anti, precision=hi,
                     preferred_element_type=F32) for u in range(n_t)], axis=1)
        h3_s[1] = pltpu.roll(rev, 1, 1)

    r = pl.program_id(0) * cb_rows + lax.broadcasted_iota(jnp.int32, (cb_rows, 1), 0)
    ch = (r % H_WIDTH).astype(F32)
    max_decay = math.log(DECAY_TARGET) / FAST_DECAY_PCT
    min_decay = math.log(DECAY_TARGET) / SLOW_DECAY_PCT
    delta = jnp.abs(min_decay + ch * ((max_decay - min_decay) / (H_WIDTH - 1)))
    n_t = S // LANES
    for half, (w_ref, pos) in enumerate(((w4f_ref, pos_f), (w4b_ref, pos_b))):
        filt = jnp.dot(w_ref[...].astype(BF16), h3_s[half].astype(BF16),
                       preferred_element_type=F32)
        filt = filt * jnp.exp(-(pos / (S - 1)) * delta)
        if half == 1:
            filt = jnp.where(lane == 0, 0.0, filt)
        for u in range(n_t):
            out_ref[half * n_t + u] = filt[:, u * LANES:(u + 1) * LANES].astype(BF16)


def _filt_mlp(w1t, b1, w2t, b2, w3t, b3, fr, w4t, S, cb_rows=256):
    R = w4t.shape[0] // 2
    Hd = FILTER_HIDDEN
    const = lambda i: (0, 0)
    nblk = R // cb_rows
    return pl.pallas_call(
        functools.partial(_filt_mlp_kernel, cb_rows=cb_rows),
        grid=(nblk,),
        in_specs=[
            pl.BlockSpec((Hd, FILTER_EMB), const),
            pl.BlockSpec((Hd, 1), const),
            pl.BlockSpec((Hd, Hd), const),
            pl.BlockSpec((Hd, 1), const),
            pl.BlockSpec((Hd, Hd), const),
            pl.BlockSpec((Hd, 1), const),
            pl.BlockSpec((Hd, 3), const),
            pl.BlockSpec((cb_rows, Hd), lambda i: (i, 0)),
            pl.BlockSpec((cb_rows, Hd), lambda i: (nblk + i, 0)),
        ],
        out_specs=pl.BlockSpec((2 * S // LANES, cb_rows, LANES), lambda i: (0, i, 0)),
        out_shape=jax.ShapeDtypeStruct((2 * S // LANES, R, LANES), BF16),
        scratch_shapes=[pltpu.VMEM((2, Hd, S), F32)],
        compiler_params=_cparams("arbitrary"),
        name="filt_mlp",
    )(w1t, b1, w2t, b2, w3t, b3, fr, w4t, w4t)


@functools.lru_cache(maxsize=None)
def _dft_constants(S):
    N = 2 * S
    N2 = LANES
    N1 = N // N2
    h = N1 // 2
    k1 = np.arange(N1)
    k2 = np.arange(N2)
    a1 = -2.0 * np.pi * np.outer(k1, k1) / N1
    f1r, f1i = np.cos(a1), np.sin(a1)
    at = -2.0 * np.pi * np.outer(k1, k2) / N
    twr, twi = np.cos(at), np.sin(at)
    a2 = -2.0 * np.pi * np.outer(k2, k2) / N2
    f2r, f2i = np.cos(a2), np.sin(a2)
    s1c = np.block([[f1r[:, :h], -f1i[:, :h]], [f1i[:, :h], f1r[:, :h]]])
    s1r = np.concatenate([f1r, f1i], axis=0)
    s2 = np.block([[f2r, f2i], [-f2i, f2r]])
    s2i = np.block([[f2r, -f2i], [f2i, f2r]])
    s1i = np.block([[f1r[:h, :], f1i[:h, :]], [-f1i[:h, :], f1r[:h, :]]])
    cast = lambda a: np.asarray(a, np.float32)
    return dict(s1c=cast(s1c), s1r=cast(s1r), s2=cast(s2), s2i=cast(s2i), s1i=cast(s1i),
                twr=cast(twr), twi=cast(twi), N1=N1, h=h)


def _bf(a):
    return jnp.asarray(a, F32).astype(BF16)


def _filt_fft_kernel(bias_ref, kern_ref, s1r_ref, s2_ref, twr_ref, twi_ref, out_ref,
                     src_s, slab_s, *, unroll):
    N1 = twr_ref.shape[0]
    n_ch = kern_ref.shape[1]
    pitch = src_s.shape[0] // N1
    scale = 1.0 / (N1 * LANES)
    twr = twr_ref[...]
    twi = twi_ref[...]
    s1r = s1r_ref[...]

    def load_in(t1, carry):
        src_s[pl.ds(pl.multiple_of(t1 * pitch, 8), n_ch), :] = kern_ref[t1].astype(F32)
        return carry
    lax.fori_loop(0, N1, load_in, 0, unroll=4)

    def per_group(j, carry):
        for u in range(0, unroll, 2):
            cs = (j * unroll + u, j * unroll + u + 1)
            z = jnp.concatenate([src_s[pl.ds(c, N1, stride=pitch), :] for c in cs], axis=1)
            a = jnp.dot(s1r, z.astype(BF16), preferred_element_type=F32)
            for i, c in enumerate(cs):
                ar = a[0:N1, i * LANES:(i + 1) * LANES]
                ai = a[N1:2 * N1, i * LANES:(i + 1) * LANES]
                r0 = pl.multiple_of(c * N1, N1)
                slab_s[pl.ds(r0, N1), 0:LANES] = (ar * twr - ai * twi).astype(BF16)
                slab_s[pl.ds(r0, N1), LANES:2 * LANES] = (ar * twi + ai * twr).astype(BF16)
        return carry

    lax.fori_loop(0, n_ch // unroll, per_group, 0)
    x = jnp.dot(slab_s[...], s2_ref[...], preferred_element_type=F32)
    bias = bias_ref[...]
    out_ref[:, :, 0:LANES] = ((x[:, 0:LANES].reshape(n_ch, N1, LANES) + bias) * scale).astype(BF16)
    out_ref[:, :, LANES:2 * LANES] = (x[:, LANES:2 * LANES].reshape(n_ch, N1, LANES)
                                      * scale).astype(BF16)


def _filt_fft(bias3, kern, consts, n_ch=32, unroll=32):
    N1, R, _ = kern.shape
    const = lambda i: (0, 0)
    pitch = n_ch + 8
    return pl.pallas_call(
        functools.partial(_filt_fft_kernel, unroll=unroll),
        grid=(R // n_ch,),
        in_specs=[
            pl.BlockSpec((n_ch, 1, 1), lambda i: (i, 0, 0)),
            pl.BlockSpec((N1, n_ch, LANES), lambda i: (0, i, 0)),
            pl.BlockSpec((2 * N1, N1), const),
            pl.BlockSpec((2 * LANES, 2 * LANES), const),
            pl.BlockSpec((N1, LANES), const),
            pl.BlockSpec((N1, LANES), const),
        ],
        out_specs=pl.BlockSpec((n_ch, N1, 2 * LANES), lambda i: (i, 0, 0)),
        out_shape=jax.ShapeDtypeStruct((R, N1, 2 * LANES), BF16),
        scratch_shapes=[pltpu.VMEM((N1 * pitch, LANES), F32),
                        pltpu.VMEM((n_ch * N1, 2 * LANES), BF16)],
        compiler_params=_cparams("parallel"),
        name="filt_fft",
    )(bias3, kern, _bf(consts["s1r"]), _bf(consts["s2"]),
      jnp.asarray(consts["twr"]), jnp.asarray(consts["twi"]))


def _hyena_kernel(nw_ref, v_ref, x1_ref, x2_ref, khat_ref, s1c_ref, s2_ref, s2i_ref, s1i_ref,
                  twr_ref, twi_ref, out_ref, src_s, slab_s, yf_s, q_s, y_s, *, unroll):
    h = v_ref.shape[1]
    n_ch = v_ref.shape[2]
    N1 = 2 * h
    twr = twr_ref[...]
    twi = twi_ref[...]
    s1c = s1c_ref[...]
    s1i = s1i_ref[...]
    n_groups = n_ch // unroll
    rows = unroll * N1
    pitch = src_s.shape[0] // (2 * h)

    def tile_rows(i):
        return pl.ds(pl.multiple_of(i * pitch, 8), n_ch)

    def chan_rows(c, bb):
        return pl.ds(bb * h * pitch + c, h, stride=pitch)

    def long_conv(order):
        def stage_a(g):
            for u in range(0, unroll, 2):
                cs = (g * unroll + u, g * unroll + u + 1)
                z = jnp.concatenate(
                    [jnp.concatenate([src_s[chan_rows(c, 0), :], src_s[chan_rows(c, 1), :]], axis=0)
                     for c in cs], axis=1)
                a = jnp.dot(s1c, z.astype(BF16), preferred_element_type=F32)
                for i, c in enumerate(cs):
                    ar = a[0:N1, i * LANES:(i + 1) * LANES]
                    ai = a[N1:2 * N1, i * LANES:(i + 1) * LANES]
                    r0 = c * N1 if isinstance(c, int) else pl.multiple_of(c * N1, N1)
                    slab_s[pl.ds(r0, N1), 0:LANES] = (ar * twr - ai * twi).astype(BF16)
                    slab_s[pl.ds(r0, N1), LANES:2 * LANES] = (ar * twi + ai * twr).astype(BF16)

        def stage_m1(g):
            r0 = g * rows if isinstance(g, int) else pl.multiple_of(g * rows, rows)
            c0 = g * unroll if isinstance(g, int) else pl.multiple_of(g * unroll, unroll)
            x = jnp.dot(slab_s[pl.ds(r0, rows), :], s2_ref[...], preferred_element_type=F32)
            xr = x[:, 0:LANES].reshape(unroll, N1, LANES)
            xi = x[:, LANES:2 * LANES].reshape(unroll, N1, LANES)
            kr = khat_ref[order, pl.ds(c0, unroll), :, 0:LANES].astype(F32)
            ki = khat_ref[order, pl.ds(c0, unroll), :, LANES:2 * LANES].astype(F32)
            yf_s[pl.ds(r0, rows), 0:LANES] = (xr * kr - xi * ki).reshape(rows, LANES).astype(BF16)
            yf_s[pl.ds(r0, rows), LANES:2 * LANES] = (xr * ki + xi * kr).reshape(rows, LANES).astype(BF16)

        def stage_m2(g):
            r0 = g * rows if isinstance(g, int) else pl.multiple_of(g * rows, rows)
            c0 = g * unroll if isinstance(g, int) else pl.multiple_of(g * unroll, unroll)
            p = jnp.dot(yf_s[pl.ds(r0, rows), :], s2i_ref[...], preferred_element_type=F32)
            pr = p[:, 0:LANES].reshape(unroll, N1, LANES)
            pi = p[:, LANES:2 * LANES].reshape(unroll, N1, LANES)
            q_s[pl.ds(c0, unroll), 0:N1, :] = (pr * twr + pi * twi).astype(BF16)
            q_s[pl.ds(c0, unroll), N1:2 * N1, :] = (pi * twr - pr * twi).astype(BF16)

        def stage_i(g):
            for u in range(0, unroll, 2):
                cs = (g * unroll + u, g * unroll + u + 1)
                q2 = jnp.concatenate([q_s[c] for c in cs], axis=1)
                y = jnp.dot(s1i, q2, preferred_element_type=F32)
                for i, c in enumerate(cs):
                    y_s[chan_rows(c, 0), :] = y[0:h, i * LANES:(i + 1) * LANES]
                    y_s[chan_rows(c, 1), :] = y[h:2 * h, i * LANES:(i + 1) * LANES]

        G = n_groups
        for t in range(G + 3):
            if 0 <= t - 3 < G:
                stage_i(t - 3)
            if 0 <= t - 2 < G:
                stage_m2(t - 2)
            if 0 <= t - 1 < G:
                stage_m1(t - 1)
            if t < G:
                stage_a(t)

    def per_tile(fn):
        for bb in range(2):
            def body(t1, carry, bb=bb):
                fn(bb, t1, tile_rows(bb * h + t1))
                return carry
            lax.fori_loop(0, h, body, 0, unroll=8)

    def load_in(bb, t1, rws):
        src_s[rws, :] = v_ref[bb, t1].astype(F32)
    per_tile(load_in)
    long_conv(0)

    def gate1(bb, t1, rws):
        src_s[rws, :] = y_s[rws, :] * x1_ref[bb, t1].astype(F32)
    per_tile(gate1)
    long_conv(1)

    nw = jnp.broadcast_to(nw_ref[...], (n_ch, LANES))

    def gate2_norm(bb, t1, rws):
        z = y_s[rws, :] * x2_ref[bb, t1].astype(F32)
        ms = jnp.mean(z * z, axis=0, keepdims=True)
        out_ref[bb, t1] = (z * lax.rsqrt(ms + EPS) * nw).astype(BF16)
    per_tile(gate2_norm)


def _hyena(nw_col, hy4, khat, consts, unroll=8):
    B, h, _, _ = hy4.shape
    C = H_WIDTH
    n_ch = GROUP_W
    G = C // n_ch
    g0 = M_WIDTH // n_ch
    N1 = consts["N1"]
    const = lambda g, p: (0, 0)
    blk = (2, h, n_ch, LANES)
    pitch = n_ch + 8
    return pl.pallas_call(
        functools.partial(_hyena_kernel, unroll=unroll),
        grid=(G, B // 2),
        in_specs=[
            pl.BlockSpec((n_ch, 1), lambda g, p: (g, 0)),
            pl.BlockSpec(blk, lambda g, p: (p, 0, g0 + g, 0)),
            pl.BlockSpec(blk, lambda g, p: (p, 0, g0 + G + g, 0)),
            pl.BlockSpec(blk, lambda g, p: (p, 0, g0 + 2 * G + g, 0)),
            pl.BlockSpec((2, n_ch, N1, 2 * LANES), lambda g, p: (0, g, 0, 0)),
            pl.BlockSpec((2 * N1, 2 * h), const),
            pl.BlockSpec((2 * LANES, 2 * LANES), const),
            pl.BlockSpec((2 * LANES, 2 * LANES), const),
            pl.BlockSpec((2 * h, 2 * N1), const),
            pl.BlockSpec((N1, LANES), const),
            pl.BlockSpec((N1, LANES), const),
        ],
        out_specs=pl.BlockSpec(blk, lambda g, p: (p, 0, g, 0)),
        out_shape=jax.ShapeDtypeStruct((B, h, C, LANES), BF16),
        scratch_shapes=[
            pltpu.VMEM((2 * h * pitch, LANES), F32),
            pltpu.VMEM((n_ch * N1, 2 * LANES), BF16),
            pltpu.VMEM((n_ch * N1, 2 * LANES), BF16),
            pltpu.VMEM((n_ch, 2 * N1, LANES), BF16),
            pltpu.VMEM((2 * h * pitch, LANES), F32),
        ],
        compiler_params=_cparams("parallel", "arbitrary"),
        name="hyena",
    )(nw_col, hy4, hy4, hy4, khat,
      _bf(consts["s1c"]), _bf(consts["s2"]), _bf(consts["s2i"]), _bf(consts["s1i"]),
      jnp.asarray(consts["twr"]), jnp.asarray(consts["twi"]))


def _outmlp_kernel(x_ref, hm_ref, og_ref, yh_ref, nwm_ref, wo_ref, w1_ref, w2_ref, n_post_ref,
                   n_pre_ref, n_post2_ref, out_ref):
    n_t = hm_ref.shape[1]
    nwm = jnp.broadcast_to(nwm_ref[...], (M_WIDTH, LANES))
    tiles = []
    for j in range(n_t):
        hg = hm_ref[0, j].astype(F32) * _sigmoid(og_ref[0, j].astype(F32))
        heads = []
        for hd in range(M_HEADS):
            hh = hg[hd * HEAD_DIM:(hd + 1) * HEAD_DIM]
            ms = jnp.mean(hh * hh, axis=0, keepdims=True)
            heads.append(hh * lax.rsqrt(ms + EPS))
        ym = jnp.concatenate(heads, axis=0) * nwm
        yt = jnp.concatenate([ym, yh_ref[0, j].astype(F32)], axis=0)
        tiles.append(yt.T.astype(BF16))
    y = tiles[0] if n_t == 1 else jnp.concatenate(tiles, axis=0)
    mix = jnp.dot(y, wo_ref[...], preferred_element_type=F32)
    x1 = x_ref[0] + _rms_rows(mix, n_post_ref[...])
    hm = _rms_rows(x1, n_pre_ref[...]).astype(BF16)
    mid = jnp.maximum(jnp.dot(hm, w1_ref[...], preferred_element_type=F32), 0.0)
    mid = (mid * mid).astype(BF16)
    ff = jnp.dot(mid, w2_ref[...], preferred_element_type=F32)
    out_ref[0] = x1 + _rms_rows(ff, n_post2_ref[...])


def _outmlp(x, hm, mt, yh, nwm_col, wo, w1, w2, n_post, n_pre, n_post2, tm_rows):
    B, S, D = x.shape
    n_t = tm_rows // LANES
    og0 = mt.shape[2] // M_WIDTH - 1
    const = lambda b, i: (0, 0)
    resident = functools.partial(pl.BlockSpec, index_map=const, pipeline_mode=pl.Buffered(1))
    return pl.pallas_call(
        _outmlp_kernel,
        grid=(B, S // tm_rows),
        in_specs=[
            pl.BlockSpec((1, tm_rows, D), lambda b, i: (b, i, 0)),
            pl.BlockSpec((1, n_t, M_WIDTH, LANES), lambda b, i: (b, i, 0, 0)),
            pl.BlockSpec((1, n_t, M_WIDTH, LANES), lambda b, i: (b, i, og0, 0)),
            pl.BlockSpec((1, n_t, H_WIDTH, LANES), lambda b, i: (b, i, 0, 0)),
            pl.BlockSpec((M_WIDTH, 1), const),
            resident((D, D)),
            resident((D, D_FF)),
            resident((D_FF, D)),
            pl.BlockSpec((1, D), const),
            pl.BlockSpec((1, D), const),
            pl.BlockSpec((1, D), const),
        ],
        out_specs=pl.BlockSpec((1, tm_rows, D), lambda b, i: (b, i, 0)),
        out_shape=jax.ShapeDtypeStruct((B, S, D), F32),
        compiler_params=_cparams("parallel", "parallel"),
        name="outmlp",
    )(x, hm, mt, yh, nwm_col, wo, w1, w2, n_post, n_pre, n_post2)


def kernel(x, norm_mix_pre, norm_mix_post, norm_mlp_pre, norm_mlp_post, w_in, b_gates,
           conv_w, conv_b, mlstm_norm_w, hyena_norm_w, filt_w1, filt_b1, filt_w2, filt_b2,
           filt_w3, filt_b3, filt_w4, filt_freq, filt_bias, w_out, w_mlp_in, w_mlp_out):
    B, S, D = x.shape
    assert D == D_MODEL and B % 2 == 0 and S % CHUNK == 0
    H = M_HEADS
    row = lambda a: a.astype(F32).reshape(1, -1)
    col = lambda a: a.astype(F32).reshape(-1, 1)
    tm_rows = min(1024, S)

    n_conv = 2 * M_WIDTH + 3 * H_WIDTH
    o_gate = n_conv + 2 * M_WIDTH
    assert w_in.shape == (D, o_gate + N_GATE)
    cw = conv_w.astype(F32)
    cbias = conv_b.astype(F32)
    ident = jnp.zeros((3, 2 * M_WIDTH), F32).at[1].set(1.0)
    cw_cm = jnp.concatenate([cw[:, 0:M_WIDTH], cw[:, 2 * M_WIDTH:n_conv], ident], axis=1)
    cb_cm = jnp.concatenate([cbias[0:M_WIDTH], cbias[2 * M_WIDTH:n_conv],
                             jnp.zeros((2 * M_WIDTH,), F32)]).reshape(1, -1)
    mt, k_tm, gcm = _proj(x, row(norm_mix_pre), w_in.astype(F32).T, col(b_gates), cw_cm, cb_cm,
                          cw[:, M_WIDTH:2 * M_WIDTH], row(cbias[M_WIDTH:2 * M_WIDTH]), tm_rows)

    g4 = gcm.reshape(B, 4, H, S).transpose(0, 2, 1, 3)
    g8 = jnp.concatenate([g4, jnp.zeros_like(g4)], axis=2)
    h_m = _mlstm(mt, k_tm, g8)

    consts = _dft_constants(S)
    kern = _filt_mlp(filt_w1.astype(F32).T, col(filt_b1), filt_w2.astype(F32).T, col(filt_b2),
                     filt_w3.astype(F32).T, col(filt_b3), filt_freq.astype(F32).T,
                     filt_w4.astype(F32).T, S)
    khat = _filt_fft(filt_bias.astype(F32).reshape(2 * H_WIDTH, 1, 1), kern, consts)
    khat = khat.reshape(2, H_WIDTH, consts["N1"], 2 * LANES)

    y_h = _hyena(col(hyena_norm_w), mt, khat, consts)

    return _outmlp(x, h_m, mt, y_h, col(mlstm_norm_w),
                   w_out.astype(BF16), w_mlp_in.astype(BF16), w_mlp_out.astype(BF16),
                   row(norm_mix_post), row(norm_mlp_pre), row(norm_mlp_post), min(512, S))
```

```python
import functools
import math

import numpy as np
import jax
import jax.numpy as jnp
from jax import lax
from jax.experimental import pallas as pl
from jax.experimental.pallas import tpu as pltpu

F32 = jnp.float32
BF16 = jnp.bfloat16

D_MODEL = 1024
M_WIDTH = 512
M_HEADS = 4
HEAD_DIM = 128
H_WIDTH = 512
H_GROUPS = 8
GROUP_W = H_WIDTH // H_GROUPS
CHUNK = 128
FILTER_EMB = 33
FILTER_HIDDEN = 64
DECAY_TARGET = 1e-2
FAST_DECAY_PCT = 0.3
SLOW_DECAY_PCT = 1.5
D_FF = 4 * D_MODEL
N_GATE = 16
EPS = 1e-6
LANES = 128
BF16_ROWS = 16
NEG_BIG = -1e30
VMEM_LIMIT = 56 * 1024 * 1024

PROJ_ROWS = 1024
PROJ_COL_BLK = 256
MLSTM_HEADS_PER_STEP = 4
FILT_MLP_ROWS = 256
FILT_FFT_ROWS = 32
HYENA_GROUP = 8
OUTMLP_ROWS = 512


def _cparams(*sem):
    return pltpu.CompilerParams(dimension_semantics=sem, vmem_limit_bytes=VMEM_LIMIT)


def _rms_rows(xf, w):
    ms = jnp.mean(xf * xf, axis=-1, keepdims=True)
    return xf * lax.rsqrt(ms + EPS) * w


def _sigmoid(x):
    return 1.0 / (1.0 + jnp.exp(-x))


def _log_sigmoid(x):
    return jnp.minimum(x, 0.0) - jnp.log(1.0 + jnp.exp(-jnp.abs(x)))


def _proj_kernel(x_ref, xp_ref, xn_ref, nw_ref, wt_ref, bgt_ref, cw_ref, cb_ref, cwk_ref, cbk_ref,
                 cm_ref, k_ref, gcm_ref, wbf_s, hall_s, pc_s, *, col_blk):
    TM = x_ref.shape[1]
    HALO = xp_ref.shape[1]
    j = pl.program_id(1)

    @pl.when(jnp.logical_and(pl.program_id(0) == 0, j == 0))
    def _():
        for c0 in range(0, wbf_s.shape[1], col_blk):
            wbf_s[:, c0:c0 + col_blk] = wt_ref[c0:c0 + col_blk, :].T.astype(BF16)

    nw = nw_ref[...]
    hn = _rms_rows(x_ref[0], nw).astype(BF16)
    hall_s[0:HALO, :] = _rms_rows(xp_ref[0], nw).astype(BF16)
    hall_s[HALO:HALO + TM, :] = hn
    hall_s[HALO + TM:2 * HALO + TM, :] = _rms_rows(xn_ref[0], nw).astype(BF16)
    keep_p = jnp.where(j > 0, 1.0, 0.0)
    keep_n = jnp.where(j < pl.num_programs(1) - 1, 1.0, 0.0)
    n_cm = cw_ref.shape[1]
    n_blk = n_cm // col_blk
    w_k0 = M_WIDTH
    w_gate = n_cm + M_WIDTH
    n_typ = N_GATE // M_HEADS
    order = [t * M_HEADS + hd for hd in range(M_HEADS) for t in range(n_typ)]
    wg = jnp.concatenate([wt_ref[w_gate + r:w_gate + r + 1, :] for r in order], axis=0)
    bg = jnp.concatenate([bgt_ref[r:r + 1, :] for r in order], axis=0)
    nt_dims = (((1,), (1,)), ((), ()))
    gt = lax.dot_general(wg.astype(BF16), hn, nt_dims, preferred_element_type=F32) + bg
    row = lax.broadcasted_iota(jnp.int32, gt.shape, 0)
    gcm_ref[0] = jnp.where((row % 2) == 1, _log_sigmoid(gt), gt)

    n_rb = TM // LANES
    n_sl = col_blk // LANES

    def project(w_row0, slot):
        wblk = wbf_s[:, pl.ds(w_row0, col_blk)]
        res = jnp.dot(hall_s[...], wblk, preferred_element_type=F32)
        for u in range(n_sl):
            ls = slice(u * LANES, (u + 1) * LANES)
            pc_s[slot, u, 0:HALO, :] = res[0:HALO, ls] * keep_p
            pc_s[slot, u, HALO:HALO + TM, :] = res[HALO:HALO + TM, ls]
            pc_s[slot, u, HALO + TM:2 * HALO + TM, :] = res[HALO + TM:2 * HALO + TM, ls] * keep_n

    def conv_tile(slot, u, rb, taps, bias):
        r = HALO + rb * LANES
        return (pc_s[slot, u, pl.ds(r - 1, LANES, stride=1), :] * taps[0:1]
                + pc_s[slot, u, r:r + LANES, :] * taps[1:2]
                + pc_s[slot, u, pl.ds(r + 1, LANES, stride=1), :] * taps[2:3] + bias)

    n_q = M_WIDTH // col_blk
    assert n_blk % 2 == 0

    def w_row_of(c):
        if c == n_blk:
            return w_k0
        return (c if c < n_q else c + n_q) * col_blk

    def emit(c, slot):
        for u in range(n_sl):
            cs = slice(c * col_blk + u * LANES, c * col_blk + (u + 1) * LANES)
            taps = cw_ref[:, cs]
            bias = cb_ref[:, cs]
            for rb in range(n_rb):
                cm_ref[0, rb, cs, :] = conv_tile(slot, u, rb, taps, bias).T.astype(BF16)

    project(0, 0)
    for c in range(n_blk):
        emit(c, c % 2)
        project(w_row_of(c + 1), (c + 1) % 2)

    k_scale = HEAD_DIM ** -0.5
    n_k = M_WIDTH // col_blk
    for kb in range(n_k):
        if kb > 0:
            project(w_k0 + kb * col_blk, kb % 2)
        for u in range(n_sl):
            cs = slice(kb * col_blk + u * LANES, kb * col_blk + (u + 1) * LANES)
            for rb in range(n_rb):
                cv = conv_tile(kb % 2, u, rb, cwk_ref[:, cs], cbk_ref[:, cs])
                k_ref[0, rb * LANES:(rb + 1) * LANES, cs] = (cv * _sigmoid(cv) * k_scale).astype(BF16)


def _proj(x, nw, w_in_t, bgt, cw_cm, cb_cm, cw_k, cb_k, tm_rows, col_blk=PROJ_COL_BLK):
    B, S, D = x.shape
    HALO = BF16_ROWS
    n_cm = cw_cm.shape[1]
    n_t = tm_rows // LANES
    hb = tm_rows // HALO
    last = S // HALO - 1
    const = lambda b, j: (0, 0)
    resident = functools.partial(pl.BlockSpec, index_map=const, pipeline_mode=pl.Buffered(1))
    return pl.pallas_call(
        functools.partial(_proj_kernel, col_blk=col_blk),
        grid=(B, S // tm_rows),
        in_specs=[
            pl.BlockSpec((1, tm_rows, D), lambda b, j: (b, j, 0)),
            pl.BlockSpec((1, HALO, D), lambda b, j: (b, jnp.maximum(j * hb - 1, 0), 0)),
            pl.BlockSpec((1, HALO, D), lambda b, j: (b, jnp.minimum((j + 1) * hb, last), 0)),
            pl.BlockSpec((1, D), const),
            resident(w_in_t.shape),
            pl.BlockSpec((N_GATE, 1), const),
            pl.BlockSpec((3, n_cm), const),
            pl.BlockSpec((1, n_cm), const),
            pl.BlockSpec((3, M_WIDTH), const),
            pl.BlockSpec((1, M_WIDTH), const),
        ],
        out_specs=[
            pl.BlockSpec((1, n_t, n_cm, LANES), lambda b, j: (b, j, 0, 0)),
            pl.BlockSpec((1, tm_rows, M_WIDTH), lambda b, j: (b, j, 0)),
            pl.BlockSpec((1, N_GATE, tm_rows), lambda b, j: (b, 0, j)),
        ],
        out_shape=[
            jax.ShapeDtypeStruct((B, S // LANES, n_cm, LANES), BF16),
            jax.ShapeDtypeStruct((B, S, M_WIDTH), BF16),
            jax.ShapeDtypeStruct((B, N_GATE, S), F32),
        ],
        scratch_shapes=[pltpu.VMEM((D, n_cm + M_WIDTH), BF16),
                        pltpu.VMEM((tm_rows + 2 * HALO, D), BF16),
                        pltpu.VMEM((2, col_blk // LANES, tm_rows + 2 * HALO, LANES), F32)],
        compiler_params=_cparams("arbitrary", "arbitrary"),
        name="proj",
    )(x, x, x, nw, w_in_t, bgt, cw_cm, cb_cm, cw_k, cb_k)


def _mlstm_kernel(qt_ref, vt_ref, k_ref, g_ref, out_ref, q_s, va_s, rows_s, stab_s, h_s, c_s):
    S = k_ref.shape[1]
    HB = g_ref.shape[1]
    L = CHUNK
    NC = S // L
    D = HEAD_DIM
    DA = D + BF16_ROWS
    W = HB * D

    R_A, R_MX, R_W, R_DEC, R_ISC, R_EMT, R_GT, R_AMAX = range(8)

    def scan(x, op, fill):
        lane = lax.broadcasted_iota(jnp.int32, x.shape, 1) % L
        pre = x
        suf = x
        d = 1
        while d < L:
            pre = op(pre, jnp.where(lane >= d, pltpu.roll(pre, d, 1), fill))
            suf = op(suf, jnp.where(lane < L - d, pltpu.roll(suf, S - d, 1), fill))
            d *= 2
        return pre, suf

    chains = [(hd, dr) for hd in range(HB) for dr in range(2)]
    assert HB % 2 == 0 and len(chains) <= 8
    a_rows = {}
    for h0 in range(0, HB, 2):
        g = jnp.concatenate([g_ref[0, h0, 0:4, :], g_ref[0, h0 + 1, 0:4, :]], axis=0)
        pre, suf = scan(g, jnp.add, 0.0)
        tot = pre + suf - g
        for j, hd in enumerate((h0, h0 + 1)):
            o = 4 * j
            b_rows = (pre[o + 1:o + 2], suf[o + 3:o + 4])
            for dr in range(2):
                a_rows[hd, dr] = g[o + 2 * dr:o + 2 * dr + 1] - b_rows[dr]
                rows_s[hd, dr, R_A:R_A + 1, :] = a_rows[hd, dr]
                rows_s[hd, dr, R_EMT:R_EMT + 1, :] = b_rows[dr]
                rows_s[hd, dr, R_GT:R_GT + 1, :] = tot[o + 2 * dr + 1:o + 2 * dr + 2]
    a8 = jnp.concatenate([a_rows[ch] for ch in chains]
                         + [jnp.zeros((8 - len(chains), S), F32)] * (len(chains) < 8), axis=0)
    pmax, smax = scan(a8, jnp.maximum, NEG_BIG)
    for i, (hd, dr) in enumerate(chains):
        rows_s[hd, dr, R_MX:R_MX + 1, :] = pmax[i:i + 1] if dr == 0 else smax[i:i + 1]
        rows_s[hd, dr, R_AMAX:R_AMAX + 1, :] = jnp.maximum(pmax[i:i + 1], smax[i:i + 1])

    def chunk_of(dr, i):
        return i if dr == 0 else NC - 1 - i

    def stabilisers(i, ms):
        out = []
        for (hd, dr), m in zip(chains, ms):
            r0 = pl.multiple_of(chunk_of(dr, i) * L, L)
            gt = rows_s[hd, dr, R_GT:R_GT + 1, pl.ds(r0, L)]
            m_end = jnp.maximum(gt + m, gt + rows_s[hd, dr, R_AMAX:R_AMAX + 1, pl.ds(r0, L)])
            stab_s[hd, dr, 0:1, pl.ds(r0, L)] = m
            stab_s[hd, dr, 1:2, pl.ds(r0, L)] = m_end
            out.append(m_end)
        return tuple(out)

    lax.fori_loop(0, NC, stabilisers, tuple(jnp.zeros((1, L), F32) for _ in chains), unroll=4)

    for hd, dr in chains:
        a = rows_s[hd, dr, R_A:R_A + 1, :]
        gt = rows_s[hd, dr, R_GT:R_GT + 1, :]
        m = stab_s[hd, dr, 0:1, :]
        m_end = stab_s[hd, dr, 1:2, :]
        mx = jnp.maximum(m, rows_s[hd, dr, R_MX:R_MX + 1, :])
        b = rows_s[hd, dr, R_EMT:R_EMT + 1, :]
        rows_s[hd, dr, R_MX:R_MX + 1, :] = mx
        rows_s[hd, dr, R_W:R_W + 1, :] = jnp.exp(gt + a - m_end)
        rows_s[hd, dr, R_DEC:R_DEC + 1, :] = jnp.exp(gt + m - m_end)
        rows_s[hd, dr, R_ISC:R_ISC + 1, :] = jnp.exp(m - mx)
        rows_s[hd, dr, R_EMT:R_EMT + 1, :] = jnp.exp(-(b + mx))

    ones_row = (lax.broadcasted_iota(jnp.int32, (BF16_ROWS, L), 0) == 0).astype(BF16)

    def prep(c, carry):
        r0 = pl.multiple_of(c * L, L)
        h_s[:, pl.ds(r0, L)] = jnp.zeros((W, L), F32)
        qpre = qt_ref[0, c].astype(F32)
        q_s[c] = (qpre * _sigmoid(qpre)).astype(BF16)
        for hd in range(HB):
            va_s[hd, 0:D, pl.ds(r0, L)] = vt_ref[0, c, hd * D:(hd + 1) * D, :]
            va_s[hd, D:DA, pl.ds(r0, L)] = ones_row
        return carry

    lax.fori_loop(0, NC, prep, 0)
    c_s[...] = jnp.zeros_like(c_s)

    si = lax.broadcasted_iota(jnp.int32, (L, L), 0)
    ti = lax.broadcasted_iota(jnp.int32, (L, L), 1)
    eye = si == ti
    valid = (si <= ti, si >= ti)

    def chain(hd, dr, c):
        r0 = pl.multiple_of(c * L, L)
        rows = rows_s[hd, dr, :, pl.ds(r0, L)]
        a_col = jnp.sum(jnp.where(eye, rows[R_A:R_A + 1], 0.0), axis=1, keepdims=True)
        wts = jnp.exp(jnp.where(valid[dr], a_col - rows[R_MX:R_MX + 1], NEG_BIG))
        kc = k_ref[0, pl.ds(r0, L), hd * D:(hd + 1) * D]
        qtc = q_s[c, hd * D:(hd + 1) * D, :]
        vac = va_s[hd, :, pl.ds(r0, L)]
        cmat = c_s[hd, dr]
        both = jnp.dot(jnp.concatenate([kc, cmat.astype(BF16)], axis=0), qtc,
                       preferred_element_type=F32)
        s_w = (both[0:L] * wts).astype(BF16)
        tot = (jnp.dot(vac, s_w, preferred_element_type=F32)
               + rows[R_ISC:R_ISC + 1] * both[L:L + DA])
        den = jnp.maximum(jnp.abs(tot[D:D + 1]), rows[R_EMT:R_EMT + 1])
        h = tot[0:D] * (1.0 / den)
        h_s[hd * D:(hd + 1) * D, pl.ds(r0, L)] = h_s[hd * D:(hd + 1) * D, pl.ds(r0, L)] + h
        u = (vac.astype(F32) * rows[R_W:R_W + 1]).astype(BF16)
        c_s[hd, dr] = rows[R_DEC:R_DEC + 1, 0:1] * cmat + jnp.dot(u, kc, preferred_element_type=F32)

    def step(i, carry):
        for hd, dr in chains:
            chain(hd, dr, chunk_of(dr, i))
        return carry

    lax.fori_loop(0, NC, step, 0)

    def fin(c, carry):
        out_ref[0, c] = h_s[:, pl.ds(pl.multiple_of(c * L, L), L)].astype(BF16)
        return carry

    lax.fori_loop(0, NC, fin, 0, unroll=4)


def _mlstm(mt, k_tm, gcm, heads_per_step=MLSTM_HEADS_PER_STEP):
    B, S, _ = k_tm.shape
    NC = S // CHUNK
    D = HEAD_DIM
    HB = heads_per_step
    v0 = (M_WIDTH + 3 * H_WIDTH) // (HB * D)
    W = HB * D
    NB = M_HEADS // HB
    DA = D + BF16_ROWS
    cm_blk = (1, NC, W, LANES)
    return pl.pallas_call(
        _mlstm_kernel,
        grid=(B, NB),
        in_specs=[
            pl.BlockSpec(cm_blk, lambda b, h: (b, 0, h, 0)),
            pl.BlockSpec(cm_blk, lambda b, h: (b, 0, v0 + h, 0)),
            pl.BlockSpec((1, S, W), lambda b, h: (b, 0, h)),
            pl.BlockSpec((1, HB, N_GATE // M_HEADS, S), lambda b, h: (b, h, 0, 0)),
        ],
        out_specs=pl.BlockSpec(cm_blk, lambda b, h: (b, 0, h, 0)),
        out_shape=jax.ShapeDtypeStruct((B, NC, M_WIDTH, LANES), BF16),
        scratch_shapes=[
            pltpu.VMEM((NC, W, LANES), BF16),
            pltpu.VMEM((HB, DA, S), BF16),
            pltpu.VMEM((HB, 2, 8, S), F32),
            pltpu.VMEM((HB, 2, 8, S), F32),
            pltpu.VMEM((W, S), F32),
            pltpu.VMEM((HB, 2, DA, D), F32),
        ],
        compiler_params=_cparams("parallel", "parallel"),
        name="mlstm",
    )(mt, mt, k_tm, gcm)


def _filt_mlp_kernel(w1t_ref, b1_ref, w2t_ref, b2_ref, w3t_ref, b3_ref, fr_ref, w4f_ref, w4b_ref,
                     out_ref, h3_s, *, cb_rows):
    S = h3_s.shape[2]
    hi = lax.Precision.HIGHEST
    lane = lax.broadcasted_iota(jnp.int32, (1, S), 1)
    pos_f = lane.astype(F32)
    pos_b = (S - lane).astype(F32)

    def features(pos):
        bands = (FILTER_EMB - 1) // 2
        t = pos / (S - 1)
        ang = (2.0 * math.pi) * pos / S
        fidx = lax.broadcasted_iota(jnp.int32, (bands, 1), 0).astype(F32)
        f = 1e-4 + fidx * ((bands - 1 - 1e-4) / (bands - 1))
        fa = f * ang
        w1t = w1t_ref[...]
        pre = (w1t[:, 0:1] * t
               + jnp.dot(w1t[:, 1:1 + bands], jnp.cos(fa), precision=hi, preferred_element_type=F32)
               - jnp.dot(w1t[:, 1 + bands:], jnp.sin(fa), precision=hi, preferred_element_type=F32))
        fr = fr_ref[...]
        h = jnp.sin(fr[:, 0:1] * (pre + b1_ref[...]))
        h = jnp.sin(fr[:, 1:2] * (jnp.dot(w2t_ref[...], h, precision=hi, preferred_element_type=F32)
                                  + b2_ref[...]))
        return jnp.sin(fr[:, 2:3] * (jnp.dot(w3t_ref[...], h, precision=hi, preferred_element_type=F32)
                                     + b3_ref[...]))

    @pl.when(pl.program_id(0) == 0)
    def _():
        h3 = features(pos_f)
        h3_s[0] = h3
        n_t = S // LANES
        anti = (lax.broadcasted_iota(jnp.int32, (LANES, LANES), 0)
                + lax.broadcasted_iota(jnp.int32, (LANES, LANES), 1) == LANES - 1).astype(F32)
        rev = jnp.concatenate(
            [jnp.dot(h3[:, (n_t - 1 - u) * LANES:(n_t - u) * LANES], anti, precision=hi,
                     preferred_element_type=F32) for u in range(n_t)], axis=1)
        h3_s[1] = pltpu.roll(rev, 1, 1)

    r = pl.program_id(0) * cb_rows + lax.broadcasted_iota(jnp.int32, (cb_rows, 1), 0)
    ch = (r % H_WIDTH).astype(F32)
    max_decay = math.log(DECAY_TARGET) / FAST_DECAY_PCT
    min_decay = math.log(DECAY_TARGET) / SLOW_DECAY_PCT
    delta = jnp.abs(min_decay + ch * ((max_decay - min_decay) / (H_WIDTH - 1)))
    n_t = S // LANES
    for half, (w_ref, pos) in enumerate(((w4f_ref, pos_f), (w4b_ref, pos_b))):
        filt = jnp.dot(w_ref[...].astype(BF16), h3_s[half].astype(BF16),
                       preferred_element_type=F32)
        filt = filt * jnp.exp(-(pos / (S - 1)) * delta)
        if half == 1:
            filt = jnp.where(lane == 0, 0.0, filt)
        for u in range(n_t):
            out_ref[half * n_t + u] = filt[:, u * LANES:(u + 1) * LANES].astype(BF16)


def _filt_mlp(w1t, b1, w2t, b2, w3t, b3, fr, w4t, S, cb_rows=FILT_MLP_ROWS):
    R = w4t.shape[0] // 2
    Hd = FILTER_HIDDEN
    const = lambda i: (0, 0)
    nblk = R // cb_rows
    return pl.pallas_call(
        functools.partial(_filt_mlp_kernel, cb_rows=cb_rows),
        grid=(nblk,),
        in_specs=[
            pl.BlockSpec((Hd, FILTER_EMB), const),
            pl.BlockSpec((Hd, 1), const),
            pl.BlockSpec((Hd, Hd), const),
            pl.BlockSpec((Hd, 1), const),
            pl.BlockSpec((Hd, Hd), const),
            pl.BlockSpec((Hd, 1), const),
            pl.BlockSpec((Hd, 3), const),
            pl.BlockSpec((cb_rows, Hd), lambda i: (i, 0)),
            pl.BlockSpec((cb_rows, Hd), lambda i: (nblk + i, 0)),
        ],
        out_specs=pl.BlockSpec((2 * S // LANES, cb_rows, LANES), lambda i: (0, i, 0)),
        out_shape=jax.ShapeDtypeStruct((2 * S // LANES, R, LANES), BF16),
        scratch_shapes=[pltpu.VMEM((2, Hd, S), F32)],
        compiler_params=_cparams("arbitrary"),
        name="filt_mlp",
    )(w1t, b1, w2t, b2, w3t, b3, fr, w4t, w4t)


@functools.lru_cache(maxsize=None)
def _dft_constants(S):
    N = 2 * S
    N2 = LANES
    N1 = N // N2
    h = N1 // 2
    k1 = np.arange(N1)
    k2 = np.arange(N2)
    a1 = -2.0 * np.pi * np.outer(k1, k1) / N1
    f1r, f1i = np.cos(a1), np.sin(a1)
    at = -2.0 * np.pi * np.outer(k1, k2) / N
    twr, twi = np.cos(at), np.sin(at)
    a2 = -2.0 * np.pi * np.outer(k2, k2) / N2
    f2r, f2i = np.cos(a2), np.sin(a2)
    s1c = np.block([[f1r[:, :h], -f1i[:, :h]], [f1i[:, :h], f1r[:, :h]]])
    s1r = np.concatenate([f1r, f1i], axis=0)
    s2 = np.block([[f2r, f2i], [-f2i, f2r]])
    s2i = np.block([[f2r, -f2i], [f2i, f2r]])
    s1i = np.block([[f1r[:h, :], f1i[:h, :]], [-f1i[:h, :], f1r[:h, :]]])
    cast = lambda a: np.asarray(a, np.float32)
    return dict(s1c=cast(s1c), s1r=cast(s1r), s2=cast(s2), s2i=cast(s2i), s1i=cast(s1i),
                twr=cast(twr), twi=cast(twi), N1=N1, h=h)


def _bf(a):
    return jnp.asarray(a, F32).astype(BF16)


def _filt_fft_kernel(bias_ref, kern_ref, s1r_ref, s2_ref, twr_ref, twi_ref, out_ref,
                     src_s, slab_s, *, unroll):
    N1 = twr_ref.shape[0]
    n_ch = kern_ref.shape[1]
    pitch = src_s.shape[0] // N1
    scale = 1.0 / (N1 * LANES)
    twr = twr_ref[...]
    twi = twi_ref[...]
    s1r = s1r_ref[...]

    def load_in(t1, carry):
        src_s[pl.ds(pl.multiple_of(t1 * pitch, 8), n_ch), :] = kern_ref[t1].astype(F32)
        return carry
    lax.fori_loop(0, N1, load_in, 0, unroll=4)

    def per_group(j, carry):
        for u in range(0, unroll, 2):
            cs = (j * unroll + u, j * unroll + u + 1)
            z = jnp.concatenate([src_s[pl.ds(c, N1, stride=pitch), :] for c in cs], axis=1)
            a = jnp.dot(s1r, z.astype(BF16), preferred_element_type=F32)
            for i, c in enumerate(cs):
                ar = a[0:N1, i * LANES:(i + 1) * LANES]
                ai = a[N1:2 * N1, i * LANES:(i + 1) * LANES]
                r0 = pl.multiple_of(c * N1, N1)
                slab_s[pl.ds(r0, N1), 0:LANES] = (ar * twr - ai * twi).astype(BF16)
                slab_s[pl.ds(r0, N1), LANES:2 * LANES] = (ar * twi + ai * twr).astype(BF16)
        return carry

    lax.fori_loop(0, n_ch // unroll, per_group, 0)
    x = jnp.dot(slab_s[...], s2_ref[...], preferred_element_type=F32)
    bias = bias_ref[...]
    out_ref[:, :, 0:LANES] = ((x[:, 0:LANES].reshape(n_ch, N1, LANES) + bias) * scale).astype(BF16)
    out_ref[:, :, LANES:2 * LANES] = (x[:, LANES:2 * LANES].reshape(n_ch, N1, LANES)
                                      * scale).astype(BF16)


def _filt_fft(bias3, kern, consts, n_ch=FILT_FFT_ROWS, unroll=FILT_FFT_ROWS):
    N1, R, _ = kern.shape
    const = lambda i: (0, 0)
    pitch = n_ch + 8
    return pl.pallas_call(
        functools.partial(_filt_fft_kernel, unroll=unroll),
        grid=(R // n_ch,),
        in_specs=[
            pl.BlockSpec((n_ch, 1, 1), lambda i: (i, 0, 0)),
            pl.BlockSpec((N1, n_ch, LANES), lambda i: (0, i, 0)),
            pl.BlockSpec((2 * N1, N1), const),
            pl.BlockSpec((2 * LANES, 2 * LANES), const),
            pl.BlockSpec((N1, LANES), const),
            pl.BlockSpec((N1, LANES), const),
        ],
        out_specs=pl.BlockSpec((n_ch, N1, 2 * LANES), lambda i: (i, 0, 0)),
        out_shape=jax.ShapeDtypeStruct((R, N1, 2 * LANES), BF16),
        scratch_shapes=[pltpu.VMEM((N1 * pitch, LANES), F32),
                        pltpu.VMEM((n_ch * N1, 2 * LANES), BF16)],
        compiler_params=_cparams("parallel"),
        name="filt_fft",
    )(bias3, kern, _bf(consts["s1r"]), _bf(consts["s2"]),
      jnp.asarray(consts["twr"]), jnp.asarray(consts["twi"]))


def _hyena_kernel(nw_ref, v_ref, x1_ref, x2_ref, khat_ref, s1c_ref, s2_ref, s2i_ref, s1i_ref,
                  twr_ref, twi_ref, out_ref, src_s, slab_s, yf_s, q_s, y_s, *, unroll):
    h = v_ref.shape[1]
    n_ch = v_ref.shape[2]
    N1 = 2 * h
    twr = twr_ref[...]
    twi = twi_ref[...]
    s1c = s1c_ref[...]
    s1i = s1i_ref[...]
    n_groups = n_ch // unroll
    rows = unroll * N1
    pitch = src_s.shape[0] // (2 * h)

    def tile_rows(i):
        return pl.ds(pl.multiple_of(i * pitch, 8), n_ch)

    def chan_rows(c, bb):
        return pl.ds(bb * h * pitch + c, h, stride=pitch)

    def long_conv(order):
        def stage_a(g):
            for u in range(0, unroll, 2):
                cs = (g * unroll + u, g * unroll + u + 1)
                z = jnp.concatenate(
                    [jnp.concatenate([src_s[chan_rows(c, 0), :], src_s[chan_rows(c, 1), :]], axis=0)
                     for c in cs], axis=1)
                a = jnp.dot(s1c, z.astype(BF16), preferred_element_type=F32)
                for i, c in enumerate(cs):
                    ar = a[0:N1, i * LANES:(i + 1) * LANES]
                    ai = a[N1:2 * N1, i * LANES:(i + 1) * LANES]
                    r0 = c * N1 if isinstance(c, int) else pl.multiple_of(c * N1, N1)
                    slab_s[pl.ds(r0, N1), 0:LANES] = (ar * twr - ai * twi).astype(BF16)
                    slab_s[pl.ds(r0, N1), LANES:2 * LANES] = (ar * twi + ai * twr).astype(BF16)

        def stage_m1(g):
            r0 = g * rows if isinstance(g, int) else pl.multiple_of(g * rows, rows)
            c0 = g * unroll if isinstance(g, int) else pl.multiple_of(g * unroll, unroll)
            x = jnp.dot(slab_s[pl.ds(r0, rows), :], s2_ref[...], preferred_element_type=F32)
            xr = x[:, 0:LANES].reshape(unroll, N1, LANES)
            xi = x[:, LANES:2 * LANES].reshape(unroll, N1, LANES)
            kr = khat_ref[order, pl.ds(c0, unroll), :, 0:LANES].astype(F32)
            ki = khat_ref[order, pl.ds(c0, unroll), :, LANES:2 * LANES].astype(F32)
            yf_s[pl.ds(r0, rows), 0:LANES] = (xr * kr - xi * ki).reshape(rows, LANES).astype(BF16)
            yf_s[pl.ds(r0, rows), LANES:2 * LANES] = (xr * ki + xi * kr).reshape(rows, LANES).astype(BF16)

        def stage_m2(g):
            r0 = g * rows if isinstance(g, int) else pl.multiple_of(g * rows, rows)
            c0 = g * unroll if isinstance(g, int) else pl.multiple_of(g * unroll, unroll)
            p = jnp.dot(yf_s[pl.ds(r0, rows), :], s2i_ref[...], preferred_element_type=F32)
            pr = p[:, 0:LANES].reshape(unroll, N1, LANES)
            pi = p[:, LANES:2 * LANES].reshape(unroll, N1, LANES)
            q_s[pl.ds(c0, unroll), 0:N1, :] = (pr * twr + pi * twi).astype(BF16)
            q_s[pl.ds(c0, unroll), N1:2 * N1, :] = (pi * twr - pr * twi).astype(BF16)

        def stage_i(g):
            for u in range(0, unroll, 2):
                cs = (g * unroll + u, g * unroll + u + 1)
                q2 = jnp.concatenate([q_s[c] for c in cs], axis=1)
                y = jnp.dot(s1i, q2, preferred_element_type=F32)
                for i, c in enumerate(cs):
                    y_s[chan_rows(c, 0), :] = y[0:h, i * LANES:(i + 1) * LANES]
                    y_s[chan_rows(c, 1), :] = y[h:2 * h, i * LANES:(i + 1) * LANES]

        G = n_groups
        for t in range(G + 3):
            if 0 <= t - 3 < G:
                stage_i(t - 3)
            if 0 <= t - 2 < G:
                stage_m2(t - 2)
            if 0 <= t - 1 < G:
                stage_m1(t - 1)
            if t < G:
                stage_a(t)

    def per_tile(fn):
        for bb in range(2):
            def body(t1, carry, bb=bb):
                fn(bb, t1, tile_rows(bb * h + t1))
                return carry
            lax.fori_loop(0, h, body, 0, unroll=8)

    def load_in(bb, t1, rws):
        src_s[rws, :] = v_ref[bb, t1].astype(F32)
    per_tile(load_in)
    long_conv(0)

    def gate1(bb, t1, rws):
        src_s[rws, :] = y_s[rws, :] * x1_ref[bb, t1].astype(F32)
    per_tile(gate1)
    long_conv(1)

    nw = jnp.broadcast_to(nw_ref[...], (n_ch, LANES))

    def gate2_norm(bb, t1, rws):
        z = y_s[rws, :] * x2_ref[bb, t1].astype(F32)
        ms = jnp.mean(z * z, axis=0, keepdims=True)
        out_ref[bb, t1] = (z * lax.rsqrt(ms + EPS) * nw).astype(BF16)
    per_tile(gate2_norm)


def _hyena(nw_col, hy4, khat, consts, unroll=HYENA_GROUP):
    B, h, _, _ = hy4.shape
    C = H_WIDTH
    n_ch = GROUP_W
    G = C // n_ch
    g0 = M_WIDTH // n_ch
    N1 = consts["N1"]
    const = lambda g, p: (0, 0)
    blk = (2, h, n_ch, LANES)
    pitch = n_ch + 8
    return pl.pallas_call(
        functools.partial(_hyena_kernel, unroll=unroll),
        grid=(G, B // 2),
        in_specs=[
            pl.BlockSpec((n_ch, 1), lambda g, p: (g, 0)),
            pl.BlockSpec(blk, lambda g, p: (p, 0, g0 + g, 0)),
            pl.BlockSpec(blk, lambda g, p: (p, 0, g0 + G + g, 0)),
            pl.BlockSpec(blk, lambda g, p: (p, 0, g0 + 2 * G + g, 0)),
            pl.BlockSpec((2, n_ch, N1, 2 * LANES), lambda g, p: (0, g, 0, 0)),
            pl.BlockSpec((2 * N1, 2 * h), const),
            pl.BlockSpec((2 * LANES, 2 * LANES), const),
            pl.BlockSpec((2 * LANES, 2 * LANES), const),
            pl.BlockSpec((2 * h, 2 * N1), const),
            pl.BlockSpec((N1, LANES), const),
            pl.BlockSpec((N1, LANES), const),
        ],
        out_specs=pl.BlockSpec(blk, lambda g, p: (p, 0, g, 0)),
        out_shape=jax.ShapeDtypeStruct((B, h, C, LANES), BF16),
        scratch_shapes=[
            pltpu.VMEM((2 * h * pitch, LANES), F32),
            pltpu.VMEM((n_ch * N1, 2 * LANES), BF16),
            pltpu.VMEM((n_ch * N1, 2 * LANES), BF16),
            pltpu.VMEM((n_ch, 2 * N1, LANES), BF16),
            pltpu.VMEM((2 * h * pitch, LANES), F32),
        ],
        compiler_params=_cparams("parallel", "arbitrary"),
        name="hyena",
    )(nw_col, hy4, hy4, hy4, khat,
      _bf(consts["s1c"]), _bf(consts["s2"]), _bf(consts["s2i"]), _bf(consts["s1i"]),
      jnp.asarray(consts["twr"]), jnp.asarray(consts["twi"]))


def _outmlp_kernel(x_ref, hm_ref, og_ref, yh_ref, nwm_ref, wo_ref, w1_ref, w2_ref, n_post_ref,
                   n_pre_ref, n_post2_ref, out_ref):
    n_t = hm_ref.shape[1]
    nwm = jnp.broadcast_to(nwm_ref[...], (M_WIDTH, LANES))
    tiles = []
    for j in range(n_t):
        hg = hm_ref[0, j].astype(F32) * _sigmoid(og_ref[0, j].astype(F32))
        heads = []
        for hd in range(M_HEADS):
            hh = hg[hd * HEAD_DIM:(hd + 1) * HEAD_DIM]
            ms = jnp.mean(hh * hh, axis=0, keepdims=True)
            heads.append(hh * lax.rsqrt(ms + EPS))
        ym = jnp.concatenate(heads, axis=0) * nwm
        yt = jnp.concatenate([ym, yh_ref[0, j].astype(F32)], axis=0)
        tiles.append(yt.T.astype(BF16))
    y = tiles[0] if n_t == 1 else jnp.concatenate(tiles, axis=0)
    mix = jnp.dot(y, wo_ref[...], preferred_element_type=F32)
    x1 = x_ref[0] + _rms_rows(mix, n_post_ref[...])
    hm = _rms_rows(x1, n_pre_ref[...]).astype(BF16)
    mid = jnp.maximum(jnp.dot(hm, w1_ref[...], preferred_element_type=F32), 0.0)
    mid = (mid * mid).astype(BF16)
    ff = jnp.dot(mid, w2_ref[...], preferred_element_type=F32)
    out_ref[0] = x1 + _rms_rows(ff, n_post2_ref[...])


def _outmlp(x, hm, mt, yh, nwm_col, wo, w1, w2, n_post, n_pre, n_post2, tm_rows):
    B, S, D = x.shape
    n_t = tm_rows // LANES
    og0 = mt.shape[2] // M_WIDTH - 1
    const = lambda b, i: (0, 0)
    resident = functools.partial(pl.BlockSpec, index_map=const, pipeline_mode=pl.Buffered(1))
    return pl.pallas_call(
        _outmlp_kernel,
        grid=(B, S // tm_rows),
        in_specs=[
            pl.BlockSpec((1, tm_rows, D), lambda b, i: (b, i, 0)),
            pl.BlockSpec((1, n_t, M_WIDTH, LANES), lambda b, i: (b, i, 0, 0)),
            pl.BlockSpec((1, n_t, M_WIDTH, LANES), lambda b, i: (b, i, og0, 0)),
            pl.BlockSpec((1, n_t, H_WIDTH, LANES), lambda b, i: (b, i, 0, 0)),
            pl.BlockSpec((M_WIDTH, 1), const),
            resident((D, D)),
            resident((D, D_FF)),
            resident((D_FF, D)),
            pl.BlockSpec((1, D), const),
            pl.BlockSpec((1, D), const),
            pl.BlockSpec((1, D), const),
        ],
        out_specs=pl.BlockSpec((1, tm_rows, D), lambda b, i: (b, i, 0)),
        out_shape=jax.ShapeDtypeStruct((B, S, D), F32),
        compiler_params=_cparams("parallel", "parallel"),
        name="outmlp",
    )(x, hm, mt, yh, nwm_col, wo, w1, w2, n_post, n_pre, n_post2)


def kernel(x, norm_mix_pre, norm_mix_post, norm_mlp_pre, norm_mlp_post, w_in, b_gates,
           conv_w, conv_b, mlstm_norm_w, hyena_norm_w, filt_w1, filt_b1, filt_w2, filt_b2,
           filt_w3, filt_b3, filt_w4, filt_freq, filt_bias, w_out, w_mlp_in, w_mlp_out):
    B, S, D = x.shape
    assert D == D_MODEL and B % 2 == 0 and S % CHUNK == 0
    H = M_HEADS
    row = lambda a: a.astype(F32).reshape(1, -1)
    col = lambda a: a.astype(F32).reshape(-1, 1)
    tm_rows = min(PROJ_ROWS, S)

    n_conv = 2 * M_WIDTH + 3 * H_WIDTH
    o_gate = n_conv + 2 * M_WIDTH
    assert w_in.shape == (D, o_gate + N_GATE)
    cw = conv_w.astype(F32)
    cbias = conv_b.astype(F32)
    ident = jnp.zeros((3, 2 * M_WIDTH), F32).at[1].set(1.0)
    cw_cm = jnp.concatenate([cw[:, 0:M_WIDTH], cw[:, 2 * M_WIDTH:n_conv], ident], axis=1)
    cb_cm = jnp.concatenate([cbias[0:M_WIDTH], cbias[2 * M_WIDTH:n_conv],
                             jnp.zeros((2 * M_WIDTH,), F32)]).reshape(1, -1)
    mt, k_tm, gcm = _proj(x, row(norm_mix_pre), w_in.astype(F32).T, col(b_gates), cw_cm, cb_cm,
                          cw[:, M_WIDTH:2 * M_WIDTH], row(cbias[M_WIDTH:2 * M_WIDTH]), tm_rows)

    h_m = _mlstm(mt, k_tm, gcm.reshape(B, H, N_GATE // H, S))

    consts = _dft_constants(S)
    kern = _filt_mlp(filt_w1.astype(F32).T, col(filt_b1), filt_w2.astype(F32).T, col(filt_b2),
                     filt_w3.astype(F32).T, col(filt_b3), filt_freq.astype(F32).T,
                     filt_w4.astype(F32).T, S)
    khat = _filt_fft(filt_bias.astype(F32).reshape(2 * H_WIDTH, 1, 1), kern, consts)
    khat = khat.reshape(2, H_WIDTH, consts["N1"], 2 * LANES)

    y_h = _hyena(col(hyena_norm_w), mt, khat, consts)

    return _outmlp(x, h_m, mt, y_h, col(mlstm_norm_w),
                   w_out.astype(BF16), w_mlp_in.astype(BF16), w_mlp_out.astype(BF16),
                   row(norm_mix_post), row(norm_mlp_pre), row(norm_mlp_post), min(OUTMLP_ROWS, S))
```

```python
import functools
import math

import numpy as np
import jax
import jax.numpy as jnp
from jax import lax
from jax.experimental import pallas as pl
from jax.experimental.pallas import tpu as pltpu

F32 = jnp.float32
BF16 = jnp.bfloat16

D_MODEL = 1024
M_WIDTH = 512
M_HEADS = 4
HEAD_DIM = 128
H_WIDTH = 512
H_GROUPS = 8
GROUP_W = H_WIDTH // H_GROUPS
CHUNK = 128
FILTER_EMB = 33
FILTER_HIDDEN = 64
DECAY_TARGET = 1e-2
FAST_DECAY_PCT = 0.3
SLOW_DECAY_PCT = 1.5
D_FF = 4 * D_MODEL
N_GATE = 16
EPS = 1e-6
LANES = 128
BF16_ROWS = 16
NEG_BIG = -1e30
VMEM_LIMIT = 56 * 1024 * 1024

PROJ_ROWS = 1024
PROJ_COL_BLK = 256
MLSTM_HEADS_PER_STEP = 4
FILT_MLP_ROWS = 256
FILT_FFT_ROWS = 32
HYENA_GROUP = 8
OUTMLP_ROWS = 512


def _cparams(*sem):
    return pltpu.CompilerParams(dimension_semantics=sem, vmem_limit_bytes=VMEM_LIMIT)


def _rms_rows(xf, w):
    ms = jnp.mean(xf * xf, axis=-1, keepdims=True)
    return xf * lax.rsqrt(ms + EPS) * w


def _sigmoid(x):
    return 1.0 / (1.0 + jnp.exp(-x))


def _log_sigmoid(x):
    return jnp.minimum(x, 0.0) - jnp.log(1.0 + jnp.exp(-jnp.abs(x)))


def _proj_kernel(x_ref, xp_ref, xn_ref, nw_ref, wt_ref, bgt_ref, cw_ref, cb_ref,
                 cm_ref, k_ref, gcm_ref, wbf_s, hall_s, pc_s, *, col_blk):
    TM = x_ref.shape[1]
    HALO = xp_ref.shape[1]
    j = pl.program_id(1)

    @pl.when(jnp.logical_and(pl.program_id(0) == 0, j == 0))
    def _():
        for c0 in range(0, wbf_s.shape[1], col_blk):
            wbf_s[:, c0:c0 + col_blk] = wt_ref[c0:c0 + col_blk, :].T.astype(BF16)

    nw = nw_ref[...]
    hn = _rms_rows(x_ref[0], nw).astype(BF16)
    hall_s[0:HALO, :] = _rms_rows(xp_ref[0], nw).astype(BF16)
    hall_s[HALO:HALO + TM, :] = hn
    hall_s[HALO + TM:2 * HALO + TM, :] = _rms_rows(xn_ref[0], nw).astype(BF16)
    keep_p = jnp.where(j > 0, 1.0, 0.0)
    keep_n = jnp.where(j < pl.num_programs(1) - 1, 1.0, 0.0)
    n_cm = cm_ref.shape[2]
    n_conv = cw_ref.shape[1]
    n_blk = n_cm // col_blk
    w_k0 = M_WIDTH
    w_gate = n_cm + M_WIDTH
    n_typ = N_GATE // M_HEADS
    order = [t * M_HEADS + hd for hd in range(M_HEADS) for t in range(n_typ)]
    wg = jnp.concatenate([wt_ref[w_gate + r:w_gate + r + 1, :] for r in order], axis=0)
    bg = jnp.concatenate([bgt_ref[r:r + 1, :] for r in order], axis=0)
    nt_dims = (((1,), (1,)), ((), ()))
    gt = lax.dot_general(wg.astype(BF16), hn, nt_dims, preferred_element_type=F32) + bg
    row = lax.broadcasted_iota(jnp.int32, gt.shape, 0)
    gcm_ref[0] = jnp.where((row % 2) == 1, _log_sigmoid(gt), gt)

    n_rb = TM // LANES
    n_sl = col_blk // LANES

    def project(w_row0, slot):
        wblk = wbf_s[:, pl.ds(w_row0, col_blk)]
        res = jnp.dot(hall_s[...], wblk, preferred_element_type=F32)
        for u in range(n_sl):
            ls = slice(u * LANES, (u + 1) * LANES)
            pc_s[slot, u, 0:HALO, :] = res[0:HALO, ls] * keep_p
            pc_s[slot, u, HALO:HALO + TM, :] = res[HALO:HALO + TM, ls]
            pc_s[slot, u, HALO + TM:2 * HALO + TM, :] = res[HALO + TM:2 * HALO + TM, ls] * keep_n

    def conv_tile(slot, u, rb, taps, bias):
        r = HALO + rb * LANES
        return (pc_s[slot, u, pl.ds(r - 1, LANES, stride=1), :] * taps[0:1]
                + pc_s[slot, u, r:r + LANES, :] * taps[1:2]
                + pc_s[slot, u, pl.ds(r + 1, LANES, stride=1), :] * taps[2:3] + bias)

    n_q = M_WIDTH // col_blk
    assert n_blk % 2 == 0

    def w_row_of(c):
        if c == n_blk:
            return w_k0
        return (c if c < n_q else c + n_q) * col_blk

    def emit(c, slot):
        for u in range(n_sl):
            cs = slice(c * col_blk + u * LANES, c * col_blk + (u + 1) * LANES)
            w0 = w_row_of(c) + u * LANES
            for rb in range(n_rb):
                if w0 < n_conv:
                    cv = conv_tile(slot, u, rb, cw_ref[:, w0:w0 + LANES], cb_ref[:, w0:w0 + LANES])
                else:
                    cv = pc_s[slot, u, HALO + rb * LANES:HALO + (rb + 1) * LANES, :]
                cm_ref[0, rb, cs, :] = cv.T.astype(BF16)

    project(0, 0)
    for c in range(n_blk):
        emit(c, c % 2)
        project(w_row_of(c + 1), (c + 1) % 2)

    k_scale = HEAD_DIM ** -0.5
    n_k = M_WIDTH // col_blk
    for kb in range(n_k):
        if kb > 0:
            project(w_k0 + kb * col_blk, kb % 2)
        for u in range(n_sl):
            cs = slice(kb * col_blk + u * LANES, kb * col_blk + (u + 1) * LANES)
            for rb in range(n_rb):
                ks = slice(w_k0 + cs.start, w_k0 + cs.stop)
                cv = conv_tile(kb % 2, u, rb, cw_ref[:, ks], cb_ref[:, ks])
                k_ref[0, rb * LANES:(rb + 1) * LANES, cs] = (cv * _sigmoid(cv) * k_scale).astype(BF16)


def _proj(x, nw, w_in_t, cols, bg_blk, cw, cb, tm_rows, col_blk=PROJ_COL_BLK):
    B, S, D = x.shape
    HALO = BF16_ROWS
    n_conv = cw.shape[1]
    n_cm = w_in_t.shape[0] - N_GATE - M_WIDTH
    n_t = tm_rows // LANES
    hb = tm_rows // HALO
    last = S // HALO - 1
    const = lambda b, j: (0, 0)
    resident = functools.partial(pl.BlockSpec, index_map=const, pipeline_mode=pl.Buffered(1))
    return pl.pallas_call(
        functools.partial(_proj_kernel, col_blk=col_blk),
        grid=(B, S // tm_rows),
        in_specs=[
            pl.BlockSpec((1, tm_rows, D), lambda b, j: (b, j, 0)),
            pl.BlockSpec((1, HALO, D), lambda b, j: (b, jnp.maximum(j * hb - 1, 0), 0)),
            pl.BlockSpec((1, HALO, D), lambda b, j: (b, jnp.minimum((j + 1) * hb, last), 0)),
            pl.BlockSpec((1, D), const),
            resident(w_in_t.shape),
            pl.BlockSpec((N_GATE, 1), lambda b, j: (bg_blk, 0)),
            pl.BlockSpec((3, n_conv), const),
            pl.BlockSpec((1, n_conv), const),
        ],
        out_specs=[
            pl.BlockSpec((1, n_t, n_cm, LANES), lambda b, j: (b, j, 0, 0)),
            pl.BlockSpec((1, tm_rows, M_WIDTH), lambda b, j: (b, j, 0)),
            pl.BlockSpec((1, N_GATE, tm_rows), lambda b, j: (b, 0, j)),
        ],
        out_shape=[
            jax.ShapeDtypeStruct((B, S // LANES, n_cm, LANES), BF16),
            jax.ShapeDtypeStruct((B, S, M_WIDTH), BF16),
            jax.ShapeDtypeStruct((B, N_GATE, S), F32),
        ],
        scratch_shapes=[pltpu.VMEM((D, n_cm + M_WIDTH), BF16),
                        pltpu.VMEM((tm_rows + 2 * HALO, D), BF16),
                        pltpu.VMEM((2, col_blk // LANES, tm_rows + 2 * HALO, LANES), F32)],
        compiler_params=_cparams("arbitrary", "arbitrary"),
        name="proj",
    )(x, x, x, nw, w_in_t, cols, cw, cb)


def _mlstm_kernel(qt_ref, vt_ref, k_ref, g_ref, out_ref, q_s, va_s, rows_s, stab_s, h_s, c_s):
    S = k_ref.shape[1]
    HB = g_ref.shape[1]
    L = CHUNK
    NC = S // L
    D = HEAD_DIM
    DA = D + BF16_ROWS
    W = HB * D

    R_A, R_MX, R_W, R_DEC, R_ISC, R_EMT, R_GT, R_AMAX = range(8)

    def scan(x, op, fill):
        lane = lax.broadcasted_iota(jnp.int32, x.shape, 1) % L
        pre = x
        suf = x
        d = 1
        while d < L:
            pre = op(pre, jnp.where(lane >= d, pltpu.roll(pre, d, 1), fill))
            suf = op(suf, jnp.where(lane < L - d, pltpu.roll(suf, S - d, 1), fill))
            d *= 2
        return pre, suf

    chains = [(hd, dr) for hd in range(HB) for dr in range(2)]
    assert HB % 2 == 0 and len(chains) <= 8
    a_rows = {}
    for h0 in range(0, HB, 2):
        g = jnp.concatenate([g_ref[0, h0, 0:4, :], g_ref[0, h0 + 1, 0:4, :]], axis=0)
        pre, suf = scan(g, jnp.add, 0.0)
        tot = pre + suf - g
        for j, hd in enumerate((h0, h0 + 1)):
            o = 4 * j
            b_rows = (pre[o + 1:o + 2], suf[o + 3:o + 4])
            for dr in range(2):
                a_rows[hd, dr] = g[o + 2 * dr:o + 2 * dr + 1] - b_rows[dr]
                rows_s[hd, dr, R_A:R_A + 1, :] = a_rows[hd, dr]
                rows_s[hd, dr, R_EMT:R_EMT + 1, :] = b_rows[dr]
                rows_s[hd, dr, R_GT:R_GT + 1, :] = tot[o + 2 * dr + 1:o + 2 * dr + 2]
    a8 = jnp.concatenate([a_rows[ch] for ch in chains]
                         + [jnp.zeros((8 - len(chains), S), F32)] * (len(chains) < 8), axis=0)
    pmax, smax = scan(a8, jnp.maximum, NEG_BIG)
    for i, (hd, dr) in enumerate(chains):
        rows_s[hd, dr, R_MX:R_MX + 1, :] = pmax[i:i + 1] if dr == 0 else smax[i:i + 1]
        rows_s[hd, dr, R_AMAX:R_AMAX + 1, :] = jnp.maximum(pmax[i:i + 1], smax[i:i + 1])

    def chunk_of(dr, i):
        return i if dr == 0 else NC - 1 - i

    def stabilisers(i, ms):
        out = []
        for (hd, dr), m in zip(chains, ms):
            r0 = pl.multiple_of(chunk_of(dr, i) * L, L)
            gt = rows_s[hd, dr, R_GT:R_GT + 1, pl.ds(r0, L)]
            m_end = jnp.maximum(gt + m, gt + rows_s[hd, dr, R_AMAX:R_AMAX + 1, pl.ds(r0, L)])
            stab_s[hd, dr, 0:1, pl.ds(r0, L)] = m
            stab_s[hd, dr, 1:2, pl.ds(r0, L)] = m_end
            out.append(m_end)
        return tuple(out)

    lax.fori_loop(0, NC, stabilisers, tuple(jnp.zeros((1, L), F32) for _ in chains), unroll=4)

    for hd, dr in chains:
        a = rows_s[hd, dr, R_A:R_A + 1, :]
        gt = rows_s[hd, dr, R_GT:R_GT + 1, :]
        m = stab_s[hd, dr, 0:1, :]
        m_end = stab_s[hd, dr, 1:2, :]
        mx = jnp.maximum(m, rows_s[hd, dr, R_MX:R_MX + 1, :])
        b = rows_s[hd, dr, R_EMT:R_EMT + 1, :]
        rows_s[hd, dr, R_MX:R_MX + 1, :] = mx
        rows_s[hd, dr, R_W:R_W + 1, :] = jnp.exp(gt + a - m_end)
        rows_s[hd, dr, R_DEC:R_DEC + 1, :] = jnp.exp(gt + m - m_end)
        rows_s[hd, dr, R_ISC:R_ISC + 1, :] = jnp.exp(m - mx)
        rows_s[hd, dr, R_EMT:R_EMT + 1, :] = jnp.exp(-(b + mx))

    ones_row = (lax.broadcasted_iota(jnp.int32, (BF16_ROWS, L), 0) == 0).astype(BF16)

    def prep(c, carry):
        r0 = pl.multiple_of(c * L, L)
        h_s[:, pl.ds(r0, L)] = jnp.zeros((W, L), F32)
        qpre = qt_ref[0, c].astype(F32)
        q_s[c] = (qpre * _sigmoid(qpre)).astype(BF16)
        for hd in range(HB):
            va_s[hd, 0:D, pl.ds(r0, L)] = vt_ref[0, c, hd * D:(hd + 1) * D, :]
            va_s[hd, D:DA, pl.ds(r0, L)] = ones_row
        return carry

    lax.fori_loop(0, NC, prep, 0)
    c_s[...] = jnp.zeros_like(c_s)

    si = lax.broadcasted_iota(jnp.int32, (L, L), 0)
    ti = lax.broadcasted_iota(jnp.int32, (L, L), 1)
    eye = si == ti
    valid = (si <= ti, si >= ti)

    def chain(hd, dr, c):
        r0 = pl.multiple_of(c * L, L)
        rows = rows_s[hd, dr, :, pl.ds(r0, L)]
        a_col = jnp.sum(jnp.where(eye, rows[R_A:R_A + 1], 0.0), axis=1, keepdims=True)
        wts = jnp.exp(jnp.where(valid[dr], a_col - rows[R_MX:R_MX + 1], NEG_BIG))
        kc = k_ref[0, pl.ds(r0, L), hd * D:(hd + 1) * D]
        qtc = q_s[c, hd * D:(hd + 1) * D, :]
        vac = va_s[hd, :, pl.ds(r0, L)]
        cmat = c_s[hd, dr]
        both = jnp.dot(jnp.concatenate([kc, cmat.astype(BF16)], axis=0), qtc,
                       preferred_element_type=F32)
        s_w = (both[0:L] * wts).astype(BF16)
        tot = (jnp.dot(vac, s_w, preferred_element_type=F32)
               + rows[R_ISC:R_ISC + 1] * both[L:L + DA])
        den = jnp.maximum(jnp.abs(tot[D:D + 1]), rows[R_EMT:R_EMT + 1])
        h = tot[0:D] * (1.0 / den)
        h_s[hd * D:(hd + 1) * D, pl.ds(r0, L)] = h_s[hd * D:(hd + 1) * D, pl.ds(r0, L)] + h
        u = (vac.astype(F32) * rows[R_W:R_W + 1]).astype(BF16)
        c_s[hd, dr] = rows[R_DEC:R_DEC + 1, 0:1] * cmat + jnp.dot(u, kc, preferred_element_type=F32)

    def step(i, carry):
        for hd, dr in chains:
            chain(hd, dr, chunk_of(dr, i))
        return carry

    lax.fori_loop(0, NC, step, 0)

    def fin(c, carry):
        out_ref[0, c] = h_s[:, pl.ds(pl.multiple_of(c * L, L), L)].astype(BF16)
        return carry

    lax.fori_loop(0, NC, fin, 0, unroll=4)


def _mlstm(mt, k_tm, gcm, heads_per_step=MLSTM_HEADS_PER_STEP):
    B, S, _ = k_tm.shape
    NC = S // CHUNK
    D = HEAD_DIM
    HB = heads_per_step
    v0 = (M_WIDTH + 3 * H_WIDTH) // (HB * D)
    W = HB * D
    NB = M_HEADS // HB
    DA = D + BF16_ROWS
    cm_blk = (1, NC, W, LANES)
    return pl.pallas_call(
        _mlstm_kernel,
        grid=(B, NB),
        in_specs=[
            pl.BlockSpec(cm_blk, lambda b, h: (b, 0, h, 0)),
            pl.BlockSpec(cm_blk, lambda b, h: (b, 0, v0 + h, 0)),
            pl.BlockSpec((1, S, W), lambda b, h: (b, 0, h)),
            pl.BlockSpec((1, HB, N_GATE // M_HEADS, S), lambda b, h: (b, h, 0, 0)),
        ],
        out_specs=pl.BlockSpec(cm_blk, lambda b, h: (b, 0, h, 0)),
        out_shape=jax.ShapeDtypeStruct((B, NC, M_WIDTH, LANES), BF16),
        scratch_shapes=[
            pltpu.VMEM((NC, W, LANES), BF16),
            pltpu.VMEM((HB, DA, S), BF16),
            pltpu.VMEM((HB, 2, 8, S), F32),
            pltpu.VMEM((HB, 2, 8, S), F32),
            pltpu.VMEM((W, S), F32),
            pltpu.VMEM((HB, 2, DA, D), F32),
        ],
        compiler_params=_cparams("parallel", "parallel"),
        name="mlstm",
    )(mt, mt, k_tm, gcm)


def _filt_mlp_kernel(w1t_ref, b1_ref, w2t_ref, b2_ref, w3t_ref, b3_ref, fr_ref, w4f_ref, w4b_ref,
                     out_ref, h3_s, *, cb_rows):
    S = h3_s.shape[2]
    hi = lax.Precision.HIGHEST
    lane = lax.broadcasted_iota(jnp.int32, (1, S), 1)
    pos_f = lane.astype(F32)
    pos_b = (S - lane).astype(F32)

    def features(pos):
        bands = (FILTER_EMB - 1) // 2
        t = pos / (S - 1)
        ang = (2.0 * math.pi) * pos / S
        fidx = lax.broadcasted_iota(jnp.int32, (bands, 1), 0).astype(F32)
        f = 1e-4 + fidx * ((bands - 1 - 1e-4) / (bands - 1))
        fa = f * ang
        w1t = w1t_ref[...]
        pre = (w1t[:, 0:1] * t
               + jnp.dot(w1t[:, 1:1 + bands], jnp.cos(fa), precision=hi, preferred_element_type=F32)
               - jnp.dot(w1t[:, 1 + bands:], jnp.sin(fa), precision=hi, preferred_element_type=F32))
        fr = fr_ref[...]
        h = jnp.sin(fr[:, 0:1] * (pre + b1_ref[...]))
        h = jnp.sin(fr[:, 1:2] * (jnp.dot(w2t_ref[...], h, precision=hi, preferred_element_type=F32)
                                  + b2_ref[...]))
        return jnp.sin(fr[:, 2:3] * (jnp.dot(w3t_ref[...], h, precision=hi, preferred_element_type=F32)
                                     + b3_ref[...]))

    @pl.when(pl.program_id(0) == 0)
    def _():
        h3 = features(pos_f)
        h3_s[0] = h3
        n_t = S // LANES
        anti = (lax.broadcasted_iota(jnp.int32, (LANES, LANES), 0)
                + lax.broadcasted_iota(jnp.int32, (LANES, LANES), 1) == LANES - 1).astype(F32)
        rev = jnp.concatenate(
            [jnp.dot(h3[:, (n_t - 1 - u) * LANES:(n_t - u) * LANES], anti, precision=hi,
                     preferred_element_type=F32) for u in range(n_t)], axis=1)
        h3_s[1] = pltpu.roll(rev, 1, 1)

    r = pl.program_id(0) * cb_rows + lax.broadcasted_iota(jnp.int32, (cb_rows, 1), 0)
    ch = (r % H_WIDTH).astype(F32)
    max_decay = math.log(DECAY_TARGET) / FAST_DECAY_PCT
    min_decay = math.log(DECAY_TARGET) / SLOW_DECAY_PCT
    delta = jnp.abs(min_decay + ch * ((max_decay - min_decay) / (H_WIDTH - 1)))
    n_t = S // LANES
    for half, (w_ref, pos) in enumerate(((w4f_ref, pos_f), (w4b_ref, pos_b))):
        filt = jnp.dot(w_ref[...].astype(BF16), h3_s[half].astype(BF16),
                       preferred_element_type=F32)
        filt = filt * jnp.exp(-(pos / (S - 1)) * delta)
        if half == 1:
            filt = jnp.where(lane == 0, 0.0, filt)
        for u in range(n_t):
            out_ref[half * n_t + u] = filt[:, u * LANES:(u + 1) * LANES].astype(BF16)


def _filt_mlp(w1t, cols, fb_blk, w2t, w3t, fr, w4t, S, cb_rows=FILT_MLP_ROWS):
    R = w4t.shape[0] // 2
    Hd = FILTER_HIDDEN
    const = lambda i: (0, 0)
    nblk = R // cb_rows
    return pl.pallas_call(
        functools.partial(_filt_mlp_kernel, cb_rows=cb_rows),
        grid=(nblk,),
        in_specs=[
            pl.BlockSpec((Hd, FILTER_EMB), const),
            pl.BlockSpec((Hd, 1), lambda i: (fb_blk, 0)),
            pl.BlockSpec((Hd, Hd), const),
            pl.BlockSpec((Hd, 1), lambda i: (fb_blk + 1, 0)),
            pl.BlockSpec((Hd, Hd), const),
            pl.BlockSpec((Hd, 1), lambda i: (fb_blk + 2, 0)),
            pl.BlockSpec((Hd, 3), const),
            pl.BlockSpec((cb_rows, Hd), lambda i: (i, 0)),
            pl.BlockSpec((cb_rows, Hd), lambda i: (nblk + i, 0)),
        ],
        out_specs=pl.BlockSpec((2 * S // LANES, cb_rows, LANES), lambda i: (0, i, 0)),
        out_shape=jax.ShapeDtypeStruct((2 * S // LANES, R, LANES), BF16),
        scratch_shapes=[pltpu.VMEM((2, Hd, S), F32)],
        compiler_params=_cparams("arbitrary"),
        name="filt_mlp",
    )(w1t, cols, w2t, cols, w3t, cols, fr, w4t, w4t)


@functools.lru_cache(maxsize=None)
def _dft_constants(S):
    N = 2 * S
    N2 = LANES
    N1 = N // N2
    h = N1 // 2
    k1 = np.arange(N1)
    k2 = np.arange(N2)
    a1 = -2.0 * np.pi * np.outer(k1, k1) / N1
    f1r, f1i = np.cos(a1), np.sin(a1)
    at = -2.0 * np.pi * np.outer(k1, k2) / N
    twr, twi = np.cos(at), np.sin(at)
    a2 = -2.0 * np.pi * np.outer(k2, k2) / N2
    f2r, f2i = np.cos(a2), np.sin(a2)
    s1c = np.block([[f1r[:, :h], -f1i[:, :h]], [f1i[:, :h], f1r[:, :h]]])
    s1r = np.concatenate([f1r, f1i], axis=0)
    s2 = np.block([[f2r, f2i], [-f2i, f2r]])
    s2i = np.block([[f2r, -f2i], [f2i, f2r]])
    s1i = np.block([[f1r[:h, :], f1i[:h, :]], [-f1i[:h, :], f1r[:h, :]]])
    cast = lambda a: np.asarray(a, np.float32)
    return dict(s1c=cast(s1c), s1r=cast(s1r), s2=cast(s2), s2i=cast(s2i), s1i=cast(s1i),
                twr=cast(twr), twi=cast(twi), N1=N1, h=h)


def _bf(a):
    return jnp.asarray(a, F32).astype(BF16)


def _filt_fft_kernel(bias_ref, kern_ref, s1r_ref, s2_ref, twr_ref, twi_ref, out_ref,
                     src_s, slab_s, *, unroll):
    N1 = twr_ref.shape[0]
    n_ch = kern_ref.shape[1]
    pitch = src_s.shape[0] // N1
    scale = 1.0 / (N1 * LANES)
    twr = twr_ref[...]
    twi = twi_ref[...]
    s1r = s1r_ref[...]

    def load_in(t1, carry):
        src_s[pl.ds(pl.multiple_of(t1 * pitch, 8), n_ch), :] = kern_ref[t1].astype(F32)
        return carry
    lax.fori_loop(0, N1, load_in, 0, unroll=4)

    def per_group(j, carry):
        for u in range(0, unroll, 2):
            cs = (j * unroll + u, j * unroll + u + 1)
            z = jnp.concatenate([src_s[pl.ds(c, N1, stride=pitch), :] for c in cs], axis=1)
            a = jnp.dot(s1r, z.astype(BF16), preferred_element_type=F32)
            for i, c in enumerate(cs):
                ar = a[0:N1, i * LANES:(i + 1) * LANES]
                ai = a[N1:2 * N1, i * LANES:(i + 1) * LANES]
                r0 = pl.multiple_of(c * N1, N1)
                slab_s[pl.ds(r0, N1), 0:LANES] = (ar * twr - ai * twi).astype(BF16)
                slab_s[pl.ds(r0, N1), LANES:2 * LANES] = (ar * twi + ai * twr).astype(BF16)
        return carry

    lax.fori_loop(0, n_ch // unroll, per_group, 0)
    x = jnp.dot(slab_s[...], s2_ref[...], preferred_element_type=F32)
    bias = bias_ref[...]
    out_ref[:, :, 0:LANES] = ((x[:, 0:LANES].reshape(n_ch, N1, LANES) + bias) * scale).astype(BF16)
    out_ref[:, :, LANES:2 * LANES] = (x[:, LANES:2 * LANES].reshape(n_ch, N1, LANES)
                                      * scale).astype(BF16)


def _filt_fft(bias3, kern, consts, n_ch=FILT_FFT_ROWS, unroll=FILT_FFT_ROWS):
    N1, R, _ = kern.shape
    const = lambda i: (0, 0)
    pitch = n_ch + 8
    return pl.pallas_call(
        functools.partial(_filt_fft_kernel, unroll=unroll),
        grid=(R // n_ch,),
        in_specs=[
            pl.BlockSpec((n_ch, 1, 1), lambda i: (i, 0, 0)),
            pl.BlockSpec((N1, n_ch, LANES), lambda i: (0, i, 0)),
            pl.BlockSpec((2 * N1, N1), const),
            pl.BlockSpec((2 * LANES, 2 * LANES), const),
            pl.BlockSpec((N1, LANES), const),
            pl.BlockSpec((N1, LANES), const),
        ],
        out_specs=pl.BlockSpec((n_ch, N1, 2 * LANES), lambda i: (i, 0, 0)),
        out_shape=jax.ShapeDtypeStruct((R, N1, 2 * LANES), BF16),
        scratch_shapes=[pltpu.VMEM((N1 * pitch, LANES), F32),
                        pltpu.VMEM((n_ch * N1, 2 * LANES), BF16)],
        compiler_params=_cparams("parallel"),
        name="filt_fft",
    )(bias3, kern, _bf(consts["s1r"]), _bf(consts["s2"]),
      jnp.asarray(consts["twr"]), jnp.asarray(consts["twi"]))


def _hyena_kernel(nw_ref, v_ref, x1_ref, x2_ref, khat_ref, s1c_ref, s2_ref, s2i_ref, s1i_ref,
                  twr_ref, twi_ref, out_ref, src_s, slab_s, yf_s, q_s, y_s, *, unroll):
    h = v_ref.shape[1]
    n_ch = v_ref.shape[2]
    N1 = 2 * h
    twr = twr_ref[...]
    twi = twi_ref[...]
    s1c = s1c_ref[...]
    s1i = s1i_ref[...]
    n_groups = n_ch // unroll
    rows = unroll * N1
    pitch = src_s.shape[0] // (2 * h)

    def tile_rows(i):
        return pl.ds(pl.multiple_of(i * pitch, 8), n_ch)

    def chan_rows(c, bb):
        return pl.ds(bb * h * pitch + c, h, stride=pitch)

    def long_conv(order):
        def stage_a(g):
            for u in range(0, unroll, 2):
                cs = (g * unroll + u, g * unroll + u + 1)
                z = jnp.concatenate(
                    [jnp.concatenate([src_s[chan_rows(c, 0), :], src_s[chan_rows(c, 1), :]], axis=0)
                     for c in cs], axis=1)
                a = jnp.dot(s1c, z.astype(BF16), preferred_element_type=F32)
                for i, c in enumerate(cs):
                    ar = a[0:N1, i * LANES:(i + 1) * LANES]
                    ai = a[N1:2 * N1, i * LANES:(i + 1) * LANES]
                    r0 = c * N1 if isinstance(c, int) else pl.multiple_of(c * N1, N1)
                    slab_s[pl.ds(r0, N1), 0:LANES] = (ar * twr - ai * twi).astype(BF16)
                    slab_s[pl.ds(r0, N1), LANES:2 * LANES] = (ar * twi + ai * twr).astype(BF16)

        def stage_m1(g):
            r0 = g * rows if isinstance(g, int) else pl.multiple_of(g * rows, rows)
            c0 = g * unroll if isinstance(g, int) else pl.multiple_of(g * unroll, unroll)
            x = jnp.dot(slab_s[pl.ds(r0, rows), :], s2_ref[...], preferred_element_type=F32)
            xr = x[:, 0:LANES].reshape(unroll, N1, LANES)
            xi = x[:, LANES:2 * LANES].reshape(unroll, N1, LANES)
            kr = khat_ref[order, pl.ds(c0, unroll), :, 0:LANES].astype(F32)
            ki = khat_ref[order, pl.ds(c0, unroll), :, LANES:2 * LANES].astype(F32)
            yf_s[pl.ds(r0, rows), 0:LANES] = (xr * kr - xi * ki).reshape(rows, LANES).astype(BF16)
            yf_s[pl.ds(r0, rows), LANES:2 * LANES] = (xr * ki + xi * kr).reshape(rows, LANES).astype(BF16)

        def stage_m2(g):
            r0 = g * rows if isinstance(g, int) else pl.multiple_of(g * rows, rows)
            c0 = g * unroll if isinstance(g, int) else pl.multiple_of(g * unroll, unroll)
            p = jnp.dot(yf_s[pl.ds(r0, rows), :], s2i_ref[...], preferred_element_type=F32)
            pr = p[:, 0:LANES].reshape(unroll, N1, LANES)
            pi = p[:, LANES:2 * LANES].reshape(unroll, N1, LANES)
            q_s[pl.ds(c0, unroll), 0:N1, :] = (pr * twr + pi * twi).astype(BF16)
            q_s[pl.ds(c0, unroll), N1:2 * N1, :] = (pi * twr - pr * twi).astype(BF16)

        def stage_i(g):
            for u in range(0, unroll, 2):
                cs = (g * unroll + u, g * unroll + u + 1)
                q2 = jnp.concatenate([q_s[c] for c in cs], axis=1)
                y = jnp.dot(s1i, q2, preferred_element_type=F32)
                for i, c in enumerate(cs):
                    y_s[chan_rows(c, 0), :] = y[0:h, i * LANES:(i + 1) * LANES]
                    y_s[chan_rows(c, 1), :] = y[h:2 * h, i * LANES:(i + 1) * LANES]

        G = n_groups
        for t in range(G + 3):
            if 0 <= t - 3 < G:
                stage_i(t - 3)
            if 0 <= t - 2 < G:
                stage_m2(t - 2)
            if 0 <= t - 1 < G:
                stage_m1(t - 1)
            if t < G:
                stage_a(t)

    def per_tile(fn):
        for bb in range(2):
            def body(t1, carry, bb=bb):
                fn(bb, t1, tile_rows(bb * h + t1))
                return carry
            lax.fori_loop(0, h, body, 0, unroll=8)

    def load_in(bb, t1, rws):
        src_s[rws, :] = v_ref[bb, t1].astype(F32)
    per_tile(load_in)
    long_conv(0)

    def gate1(bb, t1, rws):
        src_s[rws, :] = y_s[rws, :] * x1_ref[bb, t1].astype(F32)
    per_tile(gate1)
    long_conv(1)

    nw = jnp.broadcast_to(nw_ref[...], (n_ch, LANES))

    def gate2_norm(bb, t1, rws):
        z = y_s[rws, :] * x2_ref[bb, t1].astype(F32)
        ms = jnp.mean(z * z, axis=0, keepdims=True)
        out_ref[bb, t1] = (z * lax.rsqrt(ms + EPS) * nw).astype(BF16)
    per_tile(gate2_norm)


def _hyena(cols, nw_blk, hy4, khat, consts, unroll=HYENA_GROUP):
    B, h, _, _ = hy4.shape
    C = H_WIDTH
    n_ch = GROUP_W
    G = C // n_ch
    g0 = M_WIDTH // n_ch
    N1 = consts["N1"]
    const = lambda g, p: (0, 0)
    blk = (2, h, n_ch, LANES)
    pitch = n_ch + 8
    return pl.pallas_call(
        functools.partial(_hyena_kernel, unroll=unroll),
        grid=(G, B // 2),
        in_specs=[
            pl.BlockSpec((n_ch, 1), lambda g, p: (nw_blk + g, 0)),
            pl.BlockSpec(blk, lambda g, p: (p, 0, g0 + g, 0)),
            pl.BlockSpec(blk, lambda g, p: (p, 0, g0 + G + g, 0)),
            pl.BlockSpec(blk, lambda g, p: (p, 0, g0 + 2 * G + g, 0)),
            pl.BlockSpec((2, n_ch, N1, 2 * LANES), lambda g, p: (0, g, 0, 0)),
            pl.BlockSpec((2 * N1, 2 * h), const),
            pl.BlockSpec((2 * LANES, 2 * LANES), const),
            pl.BlockSpec((2 * LANES, 2 * LANES), const),
            pl.BlockSpec((2 * h, 2 * N1), const),
            pl.BlockSpec((N1, LANES), const),
            pl.BlockSpec((N1, LANES), const),
        ],
        out_specs=pl.BlockSpec(blk, lambda g, p: (p, 0, g, 0)),
        out_shape=jax.ShapeDtypeStruct((B, h, C, LANES), BF16),
        scratch_shapes=[
            pltpu.VMEM((2 * h * pitch, LANES), F32),
            pltpu.VMEM((n_ch * N1, 2 * LANES), BF16),
            pltpu.VMEM((n_ch * N1, 2 * LANES), BF16),
            pltpu.VMEM((n_ch, 2 * N1, LANES), BF16),
            pltpu.VMEM((2 * h * pitch, LANES), F32),
        ],
        compiler_params=_cparams("parallel", "arbitrary"),
        name="hyena",
    )(cols, hy4, hy4, hy4, khat,
      _bf(consts["s1c"]), _bf(consts["s2"]), _bf(consts["s2i"]), _bf(consts["s1i"]),
      jnp.asarray(consts["twr"]), jnp.asarray(consts["twi"]))


def _outmlp_kernel(x_ref, hm_ref, og_ref, yh_ref, nwm_ref, wo_ref, w1_ref, w2_ref, n_post_ref,
                   n_pre_ref, n_post2_ref, out_ref):
    n_t = hm_ref.shape[1]
    nwm = jnp.broadcast_to(nwm_ref[...], (M_WIDTH, LANES))
    tiles = []
    for j in range(n_t):
        hg = hm_ref[0, j].astype(F32) * _sigmoid(og_ref[0, j].astype(F32))
        heads = []
        for hd in range(M_HEADS):
            hh = hg[hd * HEAD_DIM:(hd + 1) * HEAD_DIM]
            ms = jnp.mean(hh * hh, axis=0, keepdims=True)
            heads.append(hh * lax.rsqrt(ms + EPS))
        ym = jnp.concatenate(heads, axis=0) * nwm
        yt = jnp.concatenate([ym, yh_ref[0, j].astype(F32)], axis=0)
        tiles.append(yt.T.astype(BF16))
    y = tiles[0] if n_t == 1 else jnp.concatenate(tiles, axis=0)
    mix = jnp.dot(y, wo_ref[...], preferred_element_type=F32)
    x1 = x_ref[0] + _rms_rows(mix, n_post_ref[...])
    hm = _rms_rows(x1, n_pre_ref[...]).astype(BF16)
    mid = jnp.maximum(jnp.dot(hm, w1_ref[...], preferred_element_type=F32), 0.0)
    mid = (mid * mid).astype(BF16)
    ff = jnp.dot(mid, w2_ref[...], preferred_element_type=F32)
    out_ref[0] = x1 + _rms_rows(ff, n_post2_ref[...])


def _outmlp(x, hm, mt, yh, nwm_col, wo, w1, w2, n_post, n_pre, n_post2, tm_rows):
    B, S, D = x.shape
    n_t = tm_rows // LANES
    og0 = mt.shape[2] // M_WIDTH - 1
    const = lambda b, i: (0, 0)
    resident = functools.partial(pl.BlockSpec, index_map=const, pipeline_mode=pl.Buffered(1))
    return pl.pallas_call(
        _outmlp_kernel,
        grid=(B, S // tm_rows),
        in_specs=[
            pl.BlockSpec((1, tm_rows, D), lambda b, i: (b, i, 0)),
            pl.BlockSpec((1, n_t, M_WIDTH, LANES), lambda b, i: (b, i, 0, 0)),
            pl.BlockSpec((1, n_t, M_WIDTH, LANES), lambda b, i: (b, i, og0, 0)),
            pl.BlockSpec((1, n_t, H_WIDTH, LANES), lambda b, i: (b, i, 0, 0)),
            pl.BlockSpec((M_WIDTH, 1), const),
            resident((D, D)),
            resident((D, D_FF)),
            resident((D_FF, D)),
            pl.BlockSpec((1, D), const),
            pl.BlockSpec((1, D), const),
            pl.BlockSpec((1, D), const),
        ],
        out_specs=pl.BlockSpec((1, tm_rows, D), lambda b, i: (b, i, 0)),
        out_shape=jax.ShapeDtypeStruct((B, S, D), F32),
        compiler_params=_cparams("parallel", "parallel"),
        name="outmlp",
    )(x, hm, mt, yh, nwm_col, wo, w1, w2, n_post, n_pre, n_post2)


def kernel(x, norm_mix_pre, norm_mix_post, norm_mlp_pre, norm_mlp_post, w_in, b_gates,
           conv_w, conv_b, mlstm_norm_w, hyena_norm_w, filt_w1, filt_b1, filt_w2, filt_b2,
           filt_w3, filt_b3, filt_w4, filt_freq, filt_bias, w_out, w_mlp_in, w_mlp_out):
    B, S, D = x.shape
    assert D == D_MODEL and B % 2 == 0 and S % CHUNK == 0
    H = M_HEADS
    row = lambda a: a.astype(F32).reshape(1, -1)
    tm_rows = min(PROJ_ROWS, S)
    cols = jnp.concatenate([mlstm_norm_w, hyena_norm_w, filt_b1, filt_b2, filt_b3, b_gates]
                           ).astype(F32).reshape(-1, 1)
    off_hn, off_fb, off_bg = M_WIDTH, M_WIDTH + H_WIDTH, M_WIDTH + H_WIDTH + 3 * FILTER_HIDDEN
    assert off_hn % GROUP_W == 0 and off_fb % FILTER_HIDDEN == 0 and off_bg % N_GATE == 0

    n_conv = 2 * M_WIDTH + 3 * H_WIDTH
    o_gate = n_conv + 2 * M_WIDTH
    assert w_in.shape == (D, o_gate + N_GATE)
    assert conv_w.shape == (3, n_conv)
    mt, k_tm, gcm = _proj(x, row(norm_mix_pre), w_in.astype(F32).T, cols, off_bg // N_GATE,
                          conv_w.astype(F32), row(conv_b), tm_rows)

    h_m = _mlstm(mt, k_tm, gcm.reshape(B, H, N_GATE // H, S))

    consts = _dft_constants(S)
    kern = _filt_mlp(filt_w1.astype(F32).T, cols, off_fb // FILTER_HIDDEN, filt_w2.astype(F32).T,
                     filt_w3.astype(F32).T, filt_freq.astype(F32).T,
                     filt_w4.astype(F32).T, S)
    khat = _filt_fft(filt_bias.astype(F32).reshape(2 * H_WIDTH, 1, 1), kern, consts)
    khat = khat.reshape(2, H_WIDTH, consts["N1"], 2 * LANES)

    y_h = _hyena(cols, off_hn // GROUP_W, mt, khat, consts)

    return _outmlp(x, h_m, mt, y_h, cols,
                   w_out.astype(BF16), w_mlp_in.astype(BF16), w_mlp_out.astype(BF16),
                   row(norm_mix_post), row(norm_mlp_pre), row(norm_mlp_post), min(OUTMLP_ROWS, S))
```

```python
import functools
import math

import numpy as np
import jax
import jax.numpy as jnp
from jax import lax
from jax.experimental import pallas as pl
from jax.experimental.pallas import tpu as pltpu

F32 = jnp.float32
BF16 = jnp.bfloat16

D_MODEL = 1024
M_WIDTH = 512
M_HEADS = 4
HEAD_DIM = 128
H_WIDTH = 512
H_GROUPS = 8
GROUP_W = H_WIDTH // H_GROUPS
CHUNK = 128
FILTER_EMB = 33
FILTER_HIDDEN = 64
DECAY_TARGET = 1e-2
FAST_DECAY_PCT = 0.3
SLOW_DECAY_PCT = 1.5
D_FF = 4 * D_MODEL
N_GATE = 16
EPS = 1e-6
LANES = 128
BF16_ROWS = 16
NEG_BIG = -1e30
VMEM_LIMIT = 56 * 1024 * 1024

PROJ_ROWS = 1024
PROJ_COL_BLK = 256
MLSTM_HEADS_PER_STEP = 4
FILT_MLP_ROWS = 256
FILT_FFT_ROWS = 32
HYENA_GROUP = 8
OUTMLP_ROWS = 512


def _cparams(*sem):
    return pltpu.CompilerParams(dimension_semantics=sem, vmem_limit_bytes=VMEM_LIMIT)


def _rms_rows(xf, w):
    ms = jnp.mean(xf * xf, axis=-1, keepdims=True)
    return xf * lax.rsqrt(ms + EPS) * w


def _sigmoid(x):
    return 1.0 / (1.0 + jnp.exp(-x))


def _log_sigmoid(x):
    return jnp.minimum(x, 0.0) - jnp.log(1.0 + jnp.exp(-jnp.abs(x)))


def _proj_kernel(x_ref, xp_ref, xn_ref, nw_ref, wt_ref, bgt_ref, cw_ref, cb_ref,
                 cm_ref, k_ref, gcm_ref, wbf_s, hall_s, pc_s, *, col_blk):
    TM = x_ref.shape[1]
    HALO = xp_ref.shape[1]
    j = pl.program_id(1)

    @pl.when(jnp.logical_and(pl.program_id(0) == 0, j == 0))
    def _():
        for c0 in range(0, wbf_s.shape[1], col_blk):
            wbf_s[:, c0:c0 + col_blk] = wt_ref[c0:c0 + col_blk, :].T.astype(BF16)

    nw = nw_ref[...]
    hn = _rms_rows(x_ref[0], nw).astype(BF16)
    hall_s[0:HALO, :] = _rms_rows(xp_ref[0], nw).astype(BF16)
    hall_s[HALO:HALO + TM, :] = hn
    hall_s[HALO + TM:2 * HALO + TM, :] = _rms_rows(xn_ref[0], nw).astype(BF16)
    keep_p = jnp.where(j > 0, 1.0, 0.0)
    keep_n = jnp.where(j < pl.num_programs(1) - 1, 1.0, 0.0)
    n_cm = cm_ref.shape[2]
    n_conv = cw_ref.shape[1]
    n_blk = n_cm // col_blk
    w_k0 = M_WIDTH
    w_gate = n_cm + M_WIDTH
    n_typ = N_GATE // M_HEADS
    order = [t * M_HEADS + hd for hd in range(M_HEADS) for t in range(n_typ)]
    wg = jnp.concatenate([wt_ref[w_gate + r:w_gate + r + 1, :] for r in order], axis=0)
    bg = jnp.concatenate([bgt_ref[r:r + 1, :] for r in order], axis=0)
    nt_dims = (((1,), (1,)), ((), ()))
    gt = lax.dot_general(wg.astype(BF16), hn, nt_dims, preferred_element_type=F32) + bg
    row = lax.broadcasted_iota(jnp.int32, gt.shape, 0)
    lg = jnp.where((row % 2) == 1, _log_sigmoid(gt), gt)
    for hd in range(M_HEADS):
        gcm_ref[0, hd] = lg[hd * n_typ:(hd + 1) * n_typ]

    n_rb = TM // LANES
    n_sl = col_blk // LANES

    def project(w_row0, slot):
        wblk = wbf_s[:, pl.ds(w_row0, col_blk)]
        res = jnp.dot(hall_s[...], wblk, preferred_element_type=F32)
        for u in range(n_sl):
            ls = slice(u * LANES, (u + 1) * LANES)
            pc_s[slot, u, 0:HALO, :] = res[0:HALO, ls] * keep_p
            pc_s[slot, u, HALO:HALO + TM, :] = res[HALO:HALO + TM, ls]
            pc_s[slot, u, HALO + TM:2 * HALO + TM, :] = res[HALO + TM:2 * HALO + TM, ls] * keep_n

    def conv_tile(slot, u, rb, taps, bias):
        r = HALO + rb * LANES
        return (pc_s[slot, u, pl.ds(r - 1, LANES, stride=1), :] * taps[0:1]
                + pc_s[slot, u, r:r + LANES, :] * taps[1:2]
                + pc_s[slot, u, pl.ds(r + 1, LANES, stride=1), :] * taps[2:3] + bias)

    n_q = M_WIDTH // col_blk
    assert n_blk % 2 == 0

    def w_row_of(c):
        if c == n_blk:
            return w_k0
        return (c if c < n_q else c + n_q) * col_blk

    def emit(c, slot):
        for u in range(n_sl):
            cs = slice(c * col_blk + u * LANES, c * col_blk + (u + 1) * LANES)
            w0 = w_row_of(c) + u * LANES
            for rb in range(n_rb):
                if w0 < n_conv:
                    cv = conv_tile(slot, u, rb, cw_ref[:, w0:w0 + LANES], cb_ref[:, w0:w0 + LANES])
                else:
                    cv = pc_s[slot, u, HALO + rb * LANES:HALO + (rb + 1) * LANES, :]
                cm_ref[0, rb, cs, :] = cv.T.astype(BF16)

    project(0, 0)
    for c in range(n_blk):
        emit(c, c % 2)
        project(w_row_of(c + 1), (c + 1) % 2)

    k_scale = HEAD_DIM ** -0.5
    n_k = M_WIDTH // col_blk
    for kb in range(n_k):
        if kb > 0:
            project(w_k0 + kb * col_blk, kb % 2)
        for u in range(n_sl):
            cs = slice(kb * col_blk + u * LANES, kb * col_blk + (u + 1) * LANES)
            for rb in range(n_rb):
                ks = slice(w_k0 + cs.start, w_k0 + cs.stop)
                cv = conv_tile(kb % 2, u, rb, cw_ref[:, ks], cb_ref[:, ks])
                k_ref[0, rb * LANES:(rb + 1) * LANES, cs] = (cv * _sigmoid(cv) * k_scale).astype(BF16)


def _proj(x, nw, w_in_t, cols, bg_blk, cw, cb, tm_rows, col_blk=PROJ_COL_BLK):
    B, S, D = x.shape
    HALO = BF16_ROWS
    n_conv = cw.shape[1]
    n_cm = w_in_t.shape[0] - N_GATE - M_WIDTH
    n_t = tm_rows // LANES
    hb = tm_rows // HALO
    last = S // HALO - 1
    const = lambda b, j: (0, 0)
    resident = functools.partial(pl.BlockSpec, index_map=const, pipeline_mode=pl.Buffered(1))
    return pl.pallas_call(
        functools.partial(_proj_kernel, col_blk=col_blk),
        grid=(B, S // tm_rows),
        in_specs=[
            pl.BlockSpec((1, tm_rows, D), lambda b, j: (b, j, 0)),
            pl.BlockSpec((1, HALO, D), lambda b, j: (b, jnp.maximum(j * hb - 1, 0), 0)),
            pl.BlockSpec((1, HALO, D), lambda b, j: (b, jnp.minimum((j + 1) * hb, last), 0)),
            pl.BlockSpec((1, D), const),
            resident(w_in_t.shape),
            pl.BlockSpec((N_GATE, 1), lambda b, j: (bg_blk, 0)),
            pl.BlockSpec((3, n_conv), const),
            pl.BlockSpec((1, n_conv), const),
        ],
        out_specs=[
            pl.BlockSpec((1, n_t, n_cm, LANES), lambda b, j: (b, j, 0, 0)),
            pl.BlockSpec((1, tm_rows, M_WIDTH), lambda b, j: (b, j, 0)),
            pl.BlockSpec((1, M_HEADS, N_GATE // M_HEADS, tm_rows), lambda b, j: (b, 0, 0, j)),
        ],
        out_shape=[
            jax.ShapeDtypeStruct((B, S // LANES, n_cm, LANES), BF16),
            jax.ShapeDtypeStruct((B, S, M_WIDTH), BF16),
            jax.ShapeDtypeStruct((B, M_HEADS, N_GATE // M_HEADS, S), F32),
        ],
        scratch_shapes=[pltpu.VMEM((D, n_cm + M_WIDTH), BF16),
                        pltpu.VMEM((tm_rows + 2 * HALO, D), BF16),
                        pltpu.VMEM((2, col_blk // LANES, tm_rows + 2 * HALO, LANES), F32)],
        compiler_params=_cparams("arbitrary", "arbitrary"),
        name="proj",
    )(x, x, x, nw, w_in_t, cols, cw, cb)


def _mlstm_kernel(qt_ref, vt_ref, k_ref, g_ref, out_ref, q_s, va_s, rows_s, stab_s, h_s, c_s):
    S = k_ref.shape[1]
    HB = g_ref.shape[1]
    L = CHUNK
    NC = S // L
    D = HEAD_DIM
    DA = D + BF16_ROWS
    W = HB * D

    R_A, R_MX, R_W, R_DEC, R_ISC, R_EMT, R_GT, R_AMAX = range(8)

    def scan(x, op, fill):
        lane = lax.broadcasted_iota(jnp.int32, x.shape, 1) % L
        pre = x
        suf = x
        d = 1
        while d < L:
            pre = op(pre, jnp.where(lane >= d, pltpu.roll(pre, d, 1), fill))
            suf = op(suf, jnp.where(lane < L - d, pltpu.roll(suf, S - d, 1), fill))
            d *= 2
        return pre, suf

    chains = [(hd, dr) for hd in range(HB) for dr in range(2)]
    assert HB % 2 == 0 and len(chains) <= 8
    a_rows = {}
    for h0 in range(0, HB, 2):
        g = jnp.concatenate([g_ref[0, h0, 0:4, :], g_ref[0, h0 + 1, 0:4, :]], axis=0)
        pre, suf = scan(g, jnp.add, 0.0)
        tot = pre + suf - g
        for j, hd in enumerate((h0, h0 + 1)):
            o = 4 * j
            b_rows = (pre[o + 1:o + 2], suf[o + 3:o + 4])
            for dr in range(2):
                a_rows[hd, dr] = g[o + 2 * dr:o + 2 * dr + 1] - b_rows[dr]
                rows_s[hd, dr, R_A:R_A + 1, :] = a_rows[hd, dr]
                rows_s[hd, dr, R_EMT:R_EMT + 1, :] = b_rows[dr]
                rows_s[hd, dr, R_GT:R_GT + 1, :] = tot[o + 2 * dr + 1:o + 2 * dr + 2]
    a8 = jnp.concatenate([a_rows[ch] for ch in chains]
                         + [jnp.zeros((8 - len(chains), S), F32)] * (len(chains) < 8), axis=0)
    pmax, smax = scan(a8, jnp.maximum, NEG_BIG)
    for i, (hd, dr) in enumerate(chains):
        rows_s[hd, dr, R_MX:R_MX + 1, :] = pmax[i:i + 1] if dr == 0 else smax[i:i + 1]
        rows_s[hd, dr, R_AMAX:R_AMAX + 1, :] = jnp.maximum(pmax[i:i + 1], smax[i:i + 1])

    def chunk_of(dr, i):
        return i if dr == 0 else NC - 1 - i

    def stabilisers(i, ms):
        out = []
        for (hd, dr), m in zip(chains, ms):
            r0 = pl.multiple_of(chunk_of(dr, i) * L, L)
            gt = rows_s[hd, dr, R_GT:R_GT + 1, pl.ds(r0, L)]
            m_end = jnp.maximum(gt + m, gt + rows_s[hd, dr, R_AMAX:R_AMAX + 1, pl.ds(r0, L)])
            stab_s[hd, dr, 0:1, pl.ds(r0, L)] = m
            stab_s[hd, dr, 1:2, pl.ds(r0, L)] = m_end
            out.append(m_end)
        return tuple(out)

    lax.fori_loop(0, NC, stabilisers, tuple(jnp.zeros((1, L), F32) for _ in chains), unroll=4)

    for hd, dr in chains:
        a = rows_s[hd, dr, R_A:R_A + 1, :]
        gt = rows_s[hd, dr, R_GT:R_GT + 1, :]
        m = stab_s[hd, dr, 0:1, :]
        m_end = stab_s[hd, dr, 1:2, :]
        mx = jnp.maximum(m, rows_s[hd, dr, R_MX:R_MX + 1, :])
        b = rows_s[hd, dr, R_EMT:R_EMT + 1, :]
        rows_s[hd, dr, R_MX:R_MX + 1, :] = mx
        rows_s[hd, dr, R_W:R_W + 1, :] = jnp.exp(gt + a - m_end)
        rows_s[hd, dr, R_DEC:R_DEC + 1, :] = jnp.exp(gt + m - m_end)
        rows_s[hd, dr, R_ISC:R_ISC + 1, :] = jnp.exp(m - mx)
        rows_s[hd, dr, R_EMT:R_EMT + 1, :] = jnp.exp(-(b + mx))

    ones_row = (lax.broadcasted_iota(jnp.int32, (BF16_ROWS, L), 0) == 0).astype(BF16)

    def prep(c, carry):
        r0 = pl.multiple_of(c * L, L)
        h_s[:, pl.ds(r0, L)] = jnp.zeros((W, L), F32)
        qpre = qt_ref[0, c].astype(F32)
        q_s[c] = (qpre * _sigmoid(qpre)).astype(BF16)
        for hd in range(HB):
            va_s[hd, 0:D, pl.ds(r0, L)] = vt_ref[0, c, hd * D:(hd + 1) * D, :]
            va_s[hd, D:DA, pl.ds(r0, L)] = ones_row
        return carry

    lax.fori_loop(0, NC, prep, 0)
    c_s[...] = jnp.zeros_like(c_s)

    si = lax.broadcasted_iota(jnp.int32, (L, L), 0)
    ti = lax.broadcasted_iota(jnp.int32, (L, L), 1)
    eye = si == ti
    valid = (si <= ti, si >= ti)

    def chain(hd, dr, c):
        r0 = pl.multiple_of(c * L, L)
        rows = rows_s[hd, dr, :, pl.ds(r0, L)]
        a_col = jnp.sum(jnp.where(eye, rows[R_A:R_A + 1], 0.0), axis=1, keepdims=True)
        wts = jnp.exp(jnp.where(valid[dr], a_col - rows[R_MX:R_MX + 1], NEG_BIG))
        kc = k_ref[0, pl.ds(r0, L), hd * D:(hd + 1) * D]
        qtc = q_s[c, hd * D:(hd + 1) * D, :]
        vac = va_s[hd, :, pl.ds(r0, L)]
        cmat = c_s[hd, dr]
        both = jnp.dot(jnp.concatenate([kc, cmat.astype(BF16)], axis=0), qtc,
                       preferred_element_type=F32)
        s_w = (both[0:L] * wts).astype(BF16)
        tot = (jnp.dot(vac, s_w, preferred_element_type=F32)
               + rows[R_ISC:R_ISC + 1] * both[L:L + DA])
        den = jnp.maximum(jnp.abs(tot[D:D + 1]), rows[R_EMT:R_EMT + 1])
        h = tot[0:D] * (1.0 / den)
        h_s[hd * D:(hd + 1) * D, pl.ds(r0, L)] = h_s[hd * D:(hd + 1) * D, pl.ds(r0, L)] + h
        u = (vac.astype(F32) * rows[R_W:R_W + 1]).astype(BF16)
        c_s[hd, dr] = rows[R_DEC:R_DEC + 1, 0:1] * cmat + jnp.dot(u, kc, preferred_element_type=F32)

    def step(i, carry):
        for hd, dr in chains:
            chain(hd, dr, chunk_of(dr, i))
        return carry

    lax.fori_loop(0, NC, step, 0)

    def fin(c, carry):
        out_ref[0, c] = h_s[:, pl.ds(pl.multiple_of(c * L, L), L)].astype(BF16)
        return carry

    lax.fori_loop(0, NC, fin, 0, unroll=4)


def _mlstm(mt, k_tm, gcm, heads_per_step=MLSTM_HEADS_PER_STEP):
    B, S, _ = k_tm.shape
    NC = S // CHUNK
    D = HEAD_DIM
    HB = heads_per_step
    v0 = (M_WIDTH + 3 * H_WIDTH) // (HB * D)
    W = HB * D
    NB = M_HEADS // HB
    DA = D + BF16_ROWS
    cm_blk = (1, NC, W, LANES)
    return pl.pallas_call(
        _mlstm_kernel,
        grid=(B, NB),
        in_specs=[
            pl.BlockSpec(cm_blk, lambda b, h: (b, 0, h, 0)),
            pl.BlockSpec(cm_blk, lambda b, h: (b, 0, v0 + h, 0)),
            pl.BlockSpec((1, S, W), lambda b, h: (b, 0, h)),
            pl.BlockSpec((1, HB, N_GATE // M_HEADS, S), lambda b, h: (b, h, 0, 0)),
        ],
        out_specs=pl.BlockSpec(cm_blk, lambda b, h: (b, 0, h, 0)),
        out_shape=jax.ShapeDtypeStruct((B, NC, M_WIDTH, LANES), BF16),
        scratch_shapes=[
            pltpu.VMEM((NC, W, LANES), BF16),
            pltpu.VMEM((HB, DA, S), BF16),
            pltpu.VMEM((HB, 2, 8, S), F32),
            pltpu.VMEM((HB, 2, 8, S), F32),
            pltpu.VMEM((W, S), F32),
            pltpu.VMEM((HB, 2, DA, D), F32),
        ],
        compiler_params=_cparams("parallel", "parallel"),
        name="mlstm",
    )(mt, mt, k_tm, gcm)


def _filt_mlp_kernel(w1t_ref, b1_ref, w2_ref, b2_ref, w3_ref, b3_ref, fr_ref, w4f_ref, w4b_ref,
                     out_ref, h3_s, *, cb_rows):
    S = h3_s.shape[2]
    hi = lax.Precision.HIGHEST
    tn_dims = (((0,), (0,)), ((), ()))
    lane = lax.broadcasted_iota(jnp.int32, (1, S), 1)
    pos_f = lane.astype(F32)
    pos_b = (S - lane).astype(F32)

    def features(pos):
        bands = (FILTER_EMB - 1) // 2
        t = pos / (S - 1)
        ang = (2.0 * math.pi) * pos / S
        fidx = lax.broadcasted_iota(jnp.int32, (bands, 1), 0).astype(F32)
        f = 1e-4 + fidx * ((bands - 1 - 1e-4) / (bands - 1))
        fa = f * ang
        w1t = w1t_ref[...]
        pre = (w1t[:, 0:1] * t
               + jnp.dot(w1t[:, 1:1 + bands], jnp.cos(fa), precision=hi, preferred_element_type=F32)
               - jnp.dot(w1t[:, 1 + bands:], jnp.sin(fa), precision=hi, preferred_element_type=F32))
        fr = fr_ref[...]
        h = jnp.sin(fr[:, 0:1] * (pre + b1_ref[...]))
        h = jnp.sin(fr[:, 1:2] * (lax.dot_general(w2_ref[...], h, tn_dims, precision=hi,
                                                  preferred_element_type=F32) + b2_ref[...]))
        return jnp.sin(fr[:, 2:3] * (lax.dot_general(w3_ref[...], h, tn_dims, precision=hi,
                                                     preferred_element_type=F32) + b3_ref[...]))

    @pl.when(pl.program_id(0) == 0)
    def _():
        h3 = features(pos_f)
        h3_s[0] = h3
        n_t = S // LANES
        anti = (lax.broadcasted_iota(jnp.int32, (LANES, LANES), 0)
                + lax.broadcasted_iota(jnp.int32, (LANES, LANES), 1) == LANES - 1).astype(F32)
        rev = jnp.concatenate(
            [jnp.dot(h3[:, (n_t - 1 - u) * LANES:(n_t - u) * LANES], anti, precision=hi,
                     preferred_element_type=F32) for u in range(n_t)], axis=1)
        h3_s[1] = pltpu.roll(rev, 1, 1)

    r = pl.program_id(0) * cb_rows + lax.broadcasted_iota(jnp.int32, (cb_rows, 1), 0)
    ch = (r % H_WIDTH).astype(F32)
    max_decay = math.log(DECAY_TARGET) / FAST_DECAY_PCT
    min_decay = math.log(DECAY_TARGET) / SLOW_DECAY_PCT
    delta = jnp.abs(min_decay + ch * ((max_decay - min_decay) / (H_WIDTH - 1)))
    n_t = S // LANES
    for half, (w_ref, pos) in enumerate(((w4f_ref, pos_f), (w4b_ref, pos_b))):
        filt = lax.dot_general(w_ref[...].astype(BF16), h3_s[half].astype(BF16), tn_dims,
                               preferred_element_type=F32)
        filt = filt * jnp.exp(-(pos / (S - 1)) * delta)
        if half == 1:
            filt = jnp.where(lane == 0, 0.0, filt)
        for u in range(n_t):
            out_ref[half * n_t + u] = filt[:, u * LANES:(u + 1) * LANES].astype(BF16)


def _filt_mlp(w1t, cols, fb_blk, w2, w3, fr, w4, S, cb_rows=FILT_MLP_ROWS):
    R = w4.shape[1] // 2
    Hd = FILTER_HIDDEN
    const = lambda i: (0, 0)
    nblk = R // cb_rows
    return pl.pallas_call(
        functools.partial(_filt_mlp_kernel, cb_rows=cb_rows),
        grid=(nblk,),
        in_specs=[
            pl.BlockSpec((Hd, FILTER_EMB), const),
            pl.BlockSpec((Hd, 1), lambda i: (fb_blk, 0)),
            pl.BlockSpec((Hd, Hd), const),
            pl.BlockSpec((Hd, 1), lambda i: (fb_blk + 1, 0)),
            pl.BlockSpec((Hd, Hd), const),
            pl.BlockSpec((Hd, 1), lambda i: (fb_blk + 2, 0)),
            pl.BlockSpec((Hd, 3), const),
            pl.BlockSpec((Hd, cb_rows), lambda i: (0, i)),
            pl.BlockSpec((Hd, cb_rows), lambda i: (0, nblk + i)),
        ],
        out_specs=pl.BlockSpec((2 * S // LANES, cb_rows, LANES), lambda i: (0, i, 0)),
        out_shape=jax.ShapeDtypeStruct((2 * S // LANES, R, LANES), BF16),
        scratch_shapes=[pltpu.VMEM((2, Hd, S), F32)],
        compiler_params=_cparams("arbitrary"),
        name="filt_mlp",
    )(w1t, cols, w2, cols, w3, cols, fr, w4, w4)


@functools.lru_cache(maxsize=None)
def _dft_constants(S):
    N = 2 * S
    N2 = LANES
    N1 = N // N2
    h = N1 // 2
    k1 = np.arange(N1)
    k2 = np.arange(N2)
    a1 = -2.0 * np.pi * np.outer(k1, k1) / N1
    f1r, f1i = np.cos(a1), np.sin(a1)
    at = -2.0 * np.pi * np.outer(k1, k2) / N
    twr, twi = np.cos(at), np.sin(at)
    a2 = -2.0 * np.pi * np.outer(k2, k2) / N2
    f2r, f2i = np.cos(a2), np.sin(a2)
    s1c = np.block([[f1r[:, :h], -f1i[:, :h]], [f1i[:, :h], f1r[:, :h]]])
    s1r = np.concatenate([f1r, f1i], axis=0)
    s2 = np.block([[f2r, f2i], [-f2i, f2r]])
    s2i = np.block([[f2r, -f2i], [f2i, f2r]])
    s1i = np.block([[f1r[:h, :], f1i[:h, :]], [-f1i[:h, :], f1r[:h, :]]])
    cast = lambda a: np.asarray(a, np.float32)
    return dict(s1c=cast(s1c), s1r=cast(s1r), s2=cast(s2), s2i=cast(s2i), s1i=cast(s1i),
                twr=cast(twr), twi=cast(twi), N1=N1, h=h)


def _bf(a):
    return jnp.asarray(a, F32).astype(BF16)


def _filt_fft_kernel(bias_ref, kern_ref, s1r_ref, s2_ref, twr_ref, twi_ref, out_ref,
                     src_s, slab_s, *, unroll):
    N1 = twr_ref.shape[0]
    n_ch = kern_ref.shape[1]
    pitch = src_s.shape[0] // N1
    scale = 1.0 / (N1 * LANES)
    twr = twr_ref[...]
    twi = twi_ref[...]
    s1r = s1r_ref[...]

    def load_in(t1, carry):
        src_s[pl.ds(pl.multiple_of(t1 * pitch, 8), n_ch), :] = kern_ref[t1].astype(F32)
        return carry
    lax.fori_loop(0, N1, load_in, 0, unroll=4)

    def per_group(j, carry):
        for u in range(0, unroll, 2):
            cs = (j * unroll + u, j * unroll + u + 1)
            z = jnp.concatenate([src_s[pl.ds(c, N1, stride=pitch), :] for c in cs], axis=1)
            a = jnp.dot(s1r, z.astype(BF16), preferred_element_type=F32)
            for i, c in enumerate(cs):
                ar = a[0:N1, i * LANES:(i + 1) * LANES]
                ai = a[N1:2 * N1, i * LANES:(i + 1) * LANES]
                r0 = pl.multiple_of(c * N1, N1)
                slab_s[pl.ds(r0, N1), 0:LANES] = (ar * twr - ai * twi).astype(BF16)
                slab_s[pl.ds(r0, N1), LANES:2 * LANES] = (ar * twi + ai * twr).astype(BF16)
        return carry

    lax.fori_loop(0, n_ch // unroll, per_group, 0)
    x = jnp.dot(slab_s[...], s2_ref[...], preferred_element_type=F32)
    bias = bias_ref[...]
    out_ref[:, :, 0:LANES] = ((x[:, 0:LANES].reshape(n_ch, N1, LANES) + bias) * scale).astype(BF16)
    out_ref[:, :, LANES:2 * LANES] = (x[:, LANES:2 * LANES].reshape(n_ch, N1, LANES)
                                      * scale).astype(BF16)


def _filt_fft(bias3, kern, consts, n_ch=FILT_FFT_ROWS, unroll=FILT_FFT_ROWS):
    N1, R, _ = kern.shape
    const = lambda i: (0, 0)
    pitch = n_ch + 8
    return pl.pallas_call(
        functools.partial(_filt_fft_kernel, unroll=unroll),
        grid=(R // n_ch,),
        in_specs=[
            pl.BlockSpec((n_ch, 1, 1), lambda i: (i, 0, 0)),
            pl.BlockSpec((N1, n_ch, LANES), lambda i: (0, i, 0)),
            pl.BlockSpec((2 * N1, N1), const),
            pl.BlockSpec((2 * LANES, 2 * LANES), const),
            pl.BlockSpec((N1, LANES), const),
            pl.BlockSpec((N1, LANES), const),
        ],
        out_specs=pl.BlockSpec((n_ch, N1, 2 * LANES), lambda i: (i, 0, 0)),
        out_shape=jax.ShapeDtypeStruct((R, N1, 2 * LANES), BF16),
        scratch_shapes=[pltpu.VMEM((N1 * pitch, LANES), F32),
                        pltpu.VMEM((n_ch * N1, 2 * LANES), BF16)],
        compiler_params=_cparams("parallel"),
        name="filt_fft",
    )(bias3, kern, _bf(consts["s1r"]), _bf(consts["s2"]),
      jnp.asarray(consts["twr"]), jnp.asarray(consts["twi"]))


def _hyena_kernel(nw_ref, v_ref, x1_ref, x2_ref, khat_ref, s1c_ref, s2_ref, s2i_ref, s1i_ref,
                  twr_ref, twi_ref, out_ref, src_s, slab_s, yf_s, q_s, y_s, *, unroll):
    h = v_ref.shape[1]
    n_ch = v_ref.shape[2]
    N1 = 2 * h
    twr = twr_ref[...]
    twi = twi_ref[...]
    s1c = s1c_ref[...]
    s1i = s1i_ref[...]
    n_groups = n_ch // unroll
    rows = unroll * N1
    pitch = src_s.shape[0] // (2 * h)

    def tile_rows(i):
        return pl.ds(pl.multiple_of(i * pitch, 8), n_ch)

    def chan_rows(c, bb):
        return pl.ds(bb * h * pitch + c, h, stride=pitch)

    def long_conv(order):
        def stage_a(g):
            for u in range(0, unroll, 2):
                cs = (g * unroll + u, g * unroll + u + 1)
                z = jnp.concatenate(
                    [jnp.concatenate([src_s[chan_rows(c, 0), :], src_s[chan_rows(c, 1), :]], axis=0)
                     for c in cs], axis=1)
                a = jnp.dot(s1c, z.astype(BF16), preferred_element_type=F32)
                for i, c in enumerate(cs):
                    ar = a[0:N1, i * LANES:(i + 1) * LANES]
                    ai = a[N1:2 * N1, i * LANES:(i + 1) * LANES]
                    r0 = c * N1 if isinstance(c, int) else pl.multiple_of(c * N1, N1)
                    slab_s[pl.ds(r0, N1), 0:LANES] = (ar * twr - ai * twi).astype(BF16)
                    slab_s[pl.ds(r0, N1), LANES:2 * LANES] = (ar * twi + ai * twr).astype(BF16)

        def stage_m1(g):
            r0 = g * rows if isinstance(g, int) else pl.multiple_of(g * rows, rows)
            c0 = g * unroll if isinstance(g, int) else pl.multiple_of(g * unroll, unroll)
            x = jnp.dot(slab_s[pl.ds(r0, rows), :], s2_ref[...], preferred_element_type=F32)
            xr = x[:, 0:LANES].reshape(unroll, N1, LANES)
            xi = x[:, LANES:2 * LANES].reshape(unroll, N1, LANES)
            kr = khat_ref[order, pl.ds(c0, unroll), :, 0:LANES].astype(F32)
            ki = khat_ref[order, pl.ds(c0, unroll), :, LANES:2 * LANES].astype(F32)
            yf_s[pl.ds(r0, rows), 0:LANES] = (xr * kr - xi * ki).reshape(rows, LANES).astype(BF16)
            yf_s[pl.ds(r0, rows), LANES:2 * LANES] = (xr * ki + xi * kr).reshape(rows, LANES).astype(BF16)

        def stage_m2(g):
            r0 = g * rows if isinstance(g, int) else pl.multiple_of(g * rows, rows)
            c0 = g * unroll if isinstance(g, int) else pl.multiple_of(g * unroll, unroll)
            p = jnp.dot(yf_s[pl.ds(r0, rows), :], s2i_ref[...], preferred_element_type=F32)
            pr = p[:, 0:LANES].reshape(unroll, N1, LANES)
            pi = p[:, LANES:2 * LANES].reshape(unroll, N1, LANES)
            q_s[pl.ds(c0, unroll), 0:N1, :] = (pr * twr + pi * twi).astype(BF16)
            q_s[pl.ds(c0, unroll), N1:2 * N1, :] = (pi * twr - pr * twi).astype(BF16)

        def stage_i(g):
            for u in range(0, unroll, 2):
                cs = (g * unroll + u, g * unroll + u + 1)
                q2 = jnp.concatenate([q_s[c] for c in cs], axis=1)
                y = jnp.dot(s1i, q2, preferred_element_type=F32)
                for i, c in enumerate(cs):
                    y_s[chan_rows(c, 0), :] = y[0:h, i * LANES:(i + 1) * LANES]
                    y_s[chan_rows(c, 1), :] = y[h:2 * h, i * LANES:(i + 1) * LANES]

        G = n_groups
        for t in range(G + 3):
            if 0 <= t - 3 < G:
                stage_i(t - 3)
            if 0 <= t - 2 < G:
                stage_m2(t - 2)
            if 0 <= t - 1 < G:
                stage_m1(t - 1)
            if t < G:
                stage_a(t)

    def per_tile(fn):
        for bb in range(2):
            def body(t1, carry, bb=bb):
                fn(bb, t1, tile_rows(bb * h + t1))
                return carry
            lax.fori_loop(0, h, body, 0, unroll=8)

    def load_in(bb, t1, rws):
        src_s[rws, :] = v_ref[bb, t1].astype(F32)
    per_tile(load_in)
    long_conv(0)

    def gate1(bb, t1, rws):
        src_s[rws, :] = y_s[rws, :] * x1_ref[bb, t1].astype(F32)
    per_tile(gate1)
    long_conv(1)

    nw = jnp.broadcast_to(nw_ref[...], (n_ch, LANES))

    def gate2_norm(bb, t1, rws):
        z = y_s[rws, :] * x2_ref[bb, t1].astype(F32)
        ms = jnp.mean(z * z, axis=0, keepdims=True)
        out_ref[bb, t1] = (z * lax.rsqrt(ms + EPS) * nw).astype(BF16)
    per_tile(gate2_norm)


def _hyena(cols, nw_blk, hy4, khat, consts, unroll=HYENA_GROUP):
    B, h, _, _ = hy4.shape
    C = H_WIDTH
    n_ch = GROUP_W
    G = C // n_ch
    g0 = M_WIDTH // n_ch
    N1 = consts["N1"]
    const = lambda g, p: (0, 0)
    blk = (2, h, n_ch, LANES)
    pitch = n_ch + 8
    return pl.pallas_call(
        functools.partial(_hyena_kernel, unroll=unroll),
        grid=(G, B // 2),
        in_specs=[
            pl.BlockSpec((n_ch, 1), lambda g, p: (nw_blk + g, 0)),
            pl.BlockSpec(blk, lambda g, p: (p, 0, g0 + g, 0)),
            pl.BlockSpec(blk, lambda g, p: (p, 0, g0 + G + g, 0)),
            pl.BlockSpec(blk, lambda g, p: (p, 0, g0 + 2 * G + g, 0)),
            pl.BlockSpec((2, n_ch, N1, 2 * LANES), lambda g, p: (0, g, 0, 0)),
            pl.BlockSpec((2 * N1, 2 * h), const),
            pl.BlockSpec((2 * LANES, 2 * LANES), const),
            pl.BlockSpec((2 * LANES, 2 * LANES), const),
            pl.BlockSpec((2 * h, 2 * N1), const),
            pl.BlockSpec((N1, LANES), const),
            pl.BlockSpec((N1, LANES), const),
        ],
        out_specs=pl.BlockSpec(blk, lambda g, p: (p, 0, g, 0)),
        out_shape=jax.ShapeDtypeStruct((B, h, C, LANES), BF16),
        scratch_shapes=[
            pltpu.VMEM((2 * h * pitch, LANES), F32),
            pltpu.VMEM((n_ch * N1, 2 * LANES), BF16),
            pltpu.VMEM((n_ch * N1, 2 * LANES), BF16),
            pltpu.VMEM((n_ch, 2 * N1, LANES), BF16),
            pltpu.VMEM((2 * h * pitch, LANES), F32),
        ],
        compiler_params=_cparams("parallel", "arbitrary"),
        name="hyena",
    )(cols, hy4, hy4, hy4, khat,
      _bf(consts["s1c"]), _bf(consts["s2"]), _bf(consts["s2i"]), _bf(consts["s1i"]),
      jnp.asarray(consts["twr"]), jnp.asarray(consts["twi"]))


def _outmlp_kernel(x_ref, hm_ref, og_ref, yh_ref, nwm_ref, wo_ref, w1_ref, w2_ref, n_post_ref,
                   n_pre_ref, n_post2_ref, out_ref):
    n_t = hm_ref.shape[1]
    nwm = jnp.broadcast_to(nwm_ref[...], (M_WIDTH, LANES))
    tiles = []
    for j in range(n_t):
        hg = hm_ref[0, j].astype(F32) * _sigmoid(og_ref[0, j].astype(F32))
        heads = []
        for hd in range(M_HEADS):
            hh = hg[hd * HEAD_DIM:(hd + 1) * HEAD_DIM]
            ms = jnp.mean(hh * hh, axis=0, keepdims=True)
            heads.append(hh * lax.rsqrt(ms + EPS))
        ym = jnp.concatenate(heads, axis=0) * nwm
        yt = jnp.concatenate([ym, yh_ref[0, j].astype(F32)], axis=0)
        tiles.append(yt.T.astype(BF16))
    y = tiles[0] if n_t == 1 else jnp.concatenate(tiles, axis=0)
    mix = jnp.dot(y, wo_ref[...], preferred_element_type=F32)
    x1 = x_ref[0] + _rms_rows(mix, n_post_ref[...])
    hm = _rms_rows(x1, n_pre_ref[...]).astype(BF16)
    mid = jnp.maximum(jnp.dot(hm, w1_ref[...], preferred_element_type=F32), 0.0)
    mid = (mid * mid).astype(BF16)
    ff = jnp.dot(mid, w2_ref[...], preferred_element_type=F32)
    out_ref[0] = x1 + _rms_rows(ff, n_post2_ref[...])


def _outmlp(x, hm, mt, yh, nwm_col, wo, w1, w2, n_post, n_pre, n_post2, tm_rows):
    B, S, D = x.shape
    n_t = tm_rows // LANES
    og0 = mt.shape[2] // M_WIDTH - 1
    const = lambda b, i: (0, 0)
    resident = functools.partial(pl.BlockSpec, index_map=const, pipeline_mode=pl.Buffered(1))
    return pl.pallas_call(
        _outmlp_kernel,
        grid=(B, S // tm_rows),
        in_specs=[
            pl.BlockSpec((1, tm_rows, D), lambda b, i: (b, i, 0)),
            pl.BlockSpec((1, n_t, M_WIDTH, LANES), lambda b, i: (b, i, 0, 0)),
            pl.BlockSpec((1, n_t, M_WIDTH, LANES), lambda b, i: (b, i, og0, 0)),
            pl.BlockSpec((1, n_t, H_WIDTH, LANES), lambda b, i: (b, i, 0, 0)),
            pl.BlockSpec((M_WIDTH, 1), const),
            resident((D, D)),
            resident((D, D_FF)),
            resident((D_FF, D)),
            pl.BlockSpec((1, D), const),
            pl.BlockSpec((1, D), const),
            pl.BlockSpec((1, D), const),
        ],
        out_specs=pl.BlockSpec((1, tm_rows, D), lambda b, i: (b, i, 0)),
        out_shape=jax.ShapeDtypeStruct((B, S, D), F32),
        compiler_params=_cparams("parallel", "parallel"),
        name="outmlp",
    )(x, hm, mt, yh, nwm_col, wo, w1, w2, n_post, n_pre, n_post2)


def kernel(x, norm_mix_pre, norm_mix_post, norm_mlp_pre, norm_mlp_post, w_in, b_gates,
           conv_w, conv_b, mlstm_norm_w, hyena_norm_w, filt_w1, filt_b1, filt_w2, filt_b2,
           filt_w3, filt_b3, filt_w4, filt_freq, filt_bias, w_out, w_mlp_in, w_mlp_out):
    B, S, D = x.shape
    assert D == D_MODEL and B % 2 == 0 and S % CHUNK == 0
    row = lambda a: a.astype(F32).reshape(1, -1)
    tm_rows = min(PROJ_ROWS, S)
    cols = jnp.concatenate([mlstm_norm_w, hyena_norm_w, filt_b1, filt_b2, filt_b3, b_gates]
                           ).astype(F32).reshape(-1, 1)
    off_hn, off_fb, off_bg = M_WIDTH, M_WIDTH + H_WIDTH, M_WIDTH + H_WIDTH + 3 * FILTER_HIDDEN
    assert off_hn % GROUP_W == 0 and off_fb % FILTER_HIDDEN == 0 and off_bg % N_GATE == 0

    n_conv = 2 * M_WIDTH + 3 * H_WIDTH
    o_gate = n_conv + 2 * M_WIDTH
    assert w_in.shape == (D, o_gate + N_GATE)
    assert conv_w.shape == (3, n_conv)
    mt, k_tm, gcm = _proj(x, row(norm_mix_pre), w_in.astype(F32).T, cols, off_bg // N_GATE,
                          conv_w.astype(F32), row(conv_b), tm_rows)

    h_m = _mlstm(mt, k_tm, gcm)

    consts = _dft_constants(S)
    kern = _filt_mlp(filt_w1.astype(F32).T, cols, off_fb // FILTER_HIDDEN, filt_w2.astype(F32),
                     filt_w3.astype(F32), filt_freq.astype(F32).T, filt_w4.astype(F32), S)
    khat = _filt_fft(filt_bias.astype(F32).reshape(2 * H_WIDTH, 1, 1), kern, consts)
    khat = khat.reshape(2, H_WIDTH, consts["N1"], 2 * LANES)

    y_h = _hyena(cols, off_hn // GROUP_W, mt, khat, consts)

    return _outmlp(x, h_m, mt, y_h, cols,
                   w_out.astype(BF16), w_mlp_in.astype(BF16), w_mlp_out.astype(BF16),
                   row(norm_mix_post), row(norm_mlp_pre), row(norm_mlp_post), min(OUTMLP_ROWS, S))
```

```python
import functools
import math

import numpy as np
import jax
import jax.numpy as jnp
from jax import lax
from jax.experimental import pallas as pl
from jax.experimental.pallas import tpu as pltpu

F32 = jnp.float32
BF16 = jnp.bfloat16

D_MODEL = 1024
M_WIDTH = 512
M_HEADS = 4
HEAD_DIM = 128
H_WIDTH = 512
H_GROUPS = 8
HYENA_ORDER = 2
GROUP_W = H_WIDTH // H_GROUPS
CHUNK = 128
FILTER_EMB = 33
FILTER_HIDDEN = 64
DECAY_TARGET = 1e-2
FAST_DECAY_PCT = 0.3
SLOW_DECAY_PCT = 1.5
D_FF = 4 * D_MODEL
N_GATE = 16
EPS = 1e-6
LANES = 128
BF16_ROWS = 16
NEG_BIG = -1e30
VMEM_LIMIT = 56 * 1024 * 1024

PROJ_ROWS = 1024
PROJ_COL_BLK = 256
MLSTM_HEADS_PER_STEP = 4
FILT_MLP_ROWS = 256
FILT_FFT_ROWS = 32
HYENA_GROUP = 8
OUTMLP_ROWS = 512


def _cparams(*sem):
    return pltpu.CompilerParams(dimension_semantics=sem, vmem_limit_bytes=VMEM_LIMIT)


def _rms_rows(xf, w):
    ms = jnp.mean(xf * xf, axis=-1, keepdims=True)
    return xf * lax.rsqrt(ms + EPS) * w


def _sigmoid(x):
    return 1.0 / (1.0 + jnp.exp(-x))


def _log_sigmoid(x):
    return jnp.minimum(x, 0.0) - jnp.log(1.0 + jnp.exp(-jnp.abs(x)))


def _proj_kernel(x_ref, xp_ref, xn_ref, nw_ref, wt_ref, bgt_ref, cw_ref, cb_ref,
                 cm_ref, k_ref, gcm_ref, wbf_s, hall_s, pc_s, *, col_blk):
    TM = x_ref.shape[1]
    HALO = xp_ref.shape[1]
    j = pl.program_id(1)

    @pl.when(jnp.logical_and(pl.program_id(0) == 0, j == 0))
    def _():
        for c0 in range(0, wbf_s.shape[1], col_blk):
            wbf_s[:, c0:c0 + col_blk] = wt_ref[c0:c0 + col_blk, :].T.astype(BF16)

    nw = nw_ref[...]
    hn = _rms_rows(x_ref[0], nw).astype(BF16)
    hall_s[0:HALO, :] = _rms_rows(xp_ref[0], nw).astype(BF16)
    hall_s[HALO:HALO + TM, :] = hn
    hall_s[HALO + TM:2 * HALO + TM, :] = _rms_rows(xn_ref[0], nw).astype(BF16)
    keep_p = jnp.where(j > 0, 1.0, 0.0)
    keep_n = jnp.where(j < pl.num_programs(1) - 1, 1.0, 0.0)
    n_cm = cm_ref.shape[2]
    n_conv = cw_ref.shape[1]
    n_blk = n_cm // col_blk
    w_k0 = M_WIDTH
    w_gate = n_cm + M_WIDTH
    n_typ = N_GATE // M_HEADS
    order = [t * M_HEADS + hd for hd in range(M_HEADS) for t in range(n_typ)]
    wg = jnp.concatenate([wt_ref[w_gate + r:w_gate + r + 1, :] for r in order], axis=0)
    bg = jnp.concatenate([bgt_ref[r:r + 1, :] for r in order], axis=0)
    nt_dims = (((1,), (1,)), ((), ()))
    gt = lax.dot_general(wg.astype(BF16), hn, nt_dims, preferred_element_type=F32) + bg
    row = lax.broadcasted_iota(jnp.int32, gt.shape, 0)
    lg = jnp.where((row % 2) == 1, _log_sigmoid(gt), gt)
    for hd in range(M_HEADS):
        gcm_ref[0, hd] = lg[hd * n_typ:(hd + 1) * n_typ]

    n_rb = TM // LANES
    n_sl = col_blk // LANES

    def project(w_row0, slot):
        wblk = wbf_s[:, pl.ds(w_row0, col_blk)]
        res = jnp.dot(hall_s[...], wblk, preferred_element_type=F32)
        for u in range(n_sl):
            ls = slice(u * LANES, (u + 1) * LANES)
            pc_s[slot, u, 0:HALO, :] = res[0:HALO, ls] * keep_p
            pc_s[slot, u, HALO:HALO + TM, :] = res[HALO:HALO + TM, ls]
            pc_s[slot, u, HALO + TM:2 * HALO + TM, :] = res[HALO + TM:2 * HALO + TM, ls] * keep_n

    def conv_tile(slot, u, rb, taps, bias):
        r = HALO + rb * LANES
        return (pc_s[slot, u, pl.ds(r - 1, LANES, stride=1), :] * taps[0:1]
                + pc_s[slot, u, r:r + LANES, :] * taps[1:2]
                + pc_s[slot, u, pl.ds(r + 1, LANES, stride=1), :] * taps[2:3] + bias)

    n_q = M_WIDTH // col_blk
    assert n_blk % 2 == 0

    def w_row_of(c):
        if c == n_blk:
            return w_k0
        return (c if c < n_q else c + n_q) * col_blk

    def emit(c, slot):
        for u in range(n_sl):
            cs = slice(c * col_blk + u * LANES, c * col_blk + (u + 1) * LANES)
            w0 = w_row_of(c) + u * LANES
            for rb in range(n_rb):
                if w0 < n_conv:
                    cv = conv_tile(slot, u, rb, cw_ref[:, w0:w0 + LANES], cb_ref[:, w0:w0 + LANES])
                else:
                    cv = pc_s[slot, u, HALO + rb * LANES:HALO + (rb + 1) * LANES, :]
                cm_ref[0, rb, cs, :] = cv.T.astype(BF16)

    project(0, 0)
    for c in range(n_blk):
        emit(c, c % 2)
        project(w_row_of(c + 1), (c + 1) % 2)

    k_scale = HEAD_DIM ** -0.5
    n_k = M_WIDTH // col_blk
    for kb in range(n_k):
        if kb > 0:
            project(w_k0 + kb * col_blk, kb % 2)
        for u in range(n_sl):
            cs = slice(kb * col_blk + u * LANES, kb * col_blk + (u + 1) * LANES)
            for rb in range(n_rb):
                ks = slice(w_k0 + cs.start, w_k0 + cs.stop)
                cv = conv_tile(kb % 2, u, rb, cw_ref[:, ks], cb_ref[:, ks])
                k_ref[0, rb * LANES:(rb + 1) * LANES, cs] = (cv * _sigmoid(cv) * k_scale).astype(BF16)


def _proj(x, nw, w_in_t, cols, bg_blk, cw, cb, tm_rows, col_blk=PROJ_COL_BLK):
    B, S, D = x.shape
    HALO = BF16_ROWS
    n_conv = cw.shape[1]
    n_cm = w_in_t.shape[0] - N_GATE - M_WIDTH
    n_t = tm_rows // LANES
    hb = tm_rows // HALO
    last = S // HALO - 1
    const = lambda b, j: (0, 0)
    resident = functools.partial(pl.BlockSpec, index_map=const, pipeline_mode=pl.Buffered(1))
    return pl.pallas_call(
        functools.partial(_proj_kernel, col_blk=col_blk),
        grid=(B, S // tm_rows),
        in_specs=[
            pl.BlockSpec((1, tm_rows, D), lambda b, j: (b, j, 0)),
            pl.BlockSpec((1, HALO, D), lambda b, j: (b, jnp.maximum(j * hb - 1, 0), 0)),
            pl.BlockSpec((1, HALO, D), lambda b, j: (b, jnp.minimum((j + 1) * hb, last), 0)),
            pl.BlockSpec((1, D), const),
            resident(w_in_t.shape),
            pl.BlockSpec((N_GATE, 1), lambda b, j: (bg_blk, 0)),
            pl.BlockSpec((3, n_conv), const),
            pl.BlockSpec((1, n_conv), const),
        ],
        out_specs=[
            pl.BlockSpec((1, n_t, n_cm, LANES), lambda b, j: (b, j, 0, 0)),
            pl.BlockSpec((1, tm_rows, M_WIDTH), lambda b, j: (b, j, 0)),
            pl.BlockSpec((1, M_HEADS, N_GATE // M_HEADS, tm_rows), lambda b, j: (b, 0, 0, j)),
        ],
        out_shape=[
            jax.ShapeDtypeStruct((B, S // LANES, n_cm, LANES), BF16),
            jax.ShapeDtypeStruct((B, S, M_WIDTH), BF16),
            jax.ShapeDtypeStruct((B, M_HEADS, N_GATE // M_HEADS, S), F32),
        ],
        scratch_shapes=[pltpu.VMEM((D, n_cm + M_WIDTH), BF16),
                        pltpu.VMEM((tm_rows + 2 * HALO, D), BF16),
                        pltpu.VMEM((2, col_blk // LANES, tm_rows + 2 * HALO, LANES), F32)],
        compiler_params=_cparams("arbitrary", "arbitrary"),
        name="proj",
    )(x, x, x, nw, w_in_t, cols, cw, cb)


def _mlstm_kernel(qt_ref, vt_ref, k_ref, g_ref, out_ref, q_s, va_s, rows_s, stab_s, h_s, c_s):
    S = k_ref.shape[1]
    HB = g_ref.shape[1]
    L = CHUNK
    NC = S // L
    D = HEAD_DIM
    DA = D + BF16_ROWS
    W = HB * D

    R_A, R_MX, R_W, R_DEC, R_ISC, R_EMT, R_GT, R_AMAX = range(8)

    def scan(x, op, fill):
        lane = lax.broadcasted_iota(jnp.int32, x.shape, 1) % L
        pre = x
        suf = x
        d = 1
        while d < L:
            pre = op(pre, jnp.where(lane >= d, pltpu.roll(pre, d, 1), fill))
            suf = op(suf, jnp.where(lane < L - d, pltpu.roll(suf, S - d, 1), fill))
            d *= 2
        return pre, suf

    chains = [(hd, dr) for hd in range(HB) for dr in range(2)]
    assert HB % 2 == 0 and len(chains) <= 8
    a_rows = {}
    for h0 in range(0, HB, 2):
        g = jnp.concatenate([g_ref[0, h0, 0:4, :], g_ref[0, h0 + 1, 0:4, :]], axis=0)
        pre, suf = scan(g, jnp.add, 0.0)
        tot = pre + suf - g
        for j, hd in enumerate((h0, h0 + 1)):
            o = 4 * j
            b_rows = (pre[o + 1:o + 2], suf[o + 3:o + 4])
            for dr in range(2):
                a_rows[hd, dr] = g[o + 2 * dr:o + 2 * dr + 1] - b_rows[dr]
                rows_s[hd, dr, R_A:R_A + 1, :] = a_rows[hd, dr]
                rows_s[hd, dr, R_EMT:R_EMT + 1, :] = b_rows[dr]
                rows_s[hd, dr, R_GT:R_GT + 1, :] = tot[o + 2 * dr + 1:o + 2 * dr + 2]
    a8 = jnp.concatenate([a_rows[ch] for ch in chains]
                         + [jnp.zeros((8 - len(chains), S), F32)] * (len(chains) < 8), axis=0)
    pmax, smax = scan(a8, jnp.maximum, NEG_BIG)
    for i, (hd, dr) in enumerate(chains):
        rows_s[hd, dr, R_MX:R_MX + 1, :] = pmax[i:i + 1] if dr == 0 else smax[i:i + 1]
        rows_s[hd, dr, R_AMAX:R_AMAX + 1, :] = jnp.maximum(pmax[i:i + 1], smax[i:i + 1])

    def chunk_of(dr, i):
        return i if dr == 0 else NC - 1 - i

    def stabilisers(i, ms):
        out = []
        for (hd, dr), m in zip(chains, ms):
            r0 = pl.multiple_of(chunk_of(dr, i) * L, L)
            gt = rows_s[hd, dr, R_GT:R_GT + 1, pl.ds(r0, L)]
            m_end = jnp.maximum(gt + m, gt + rows_s[hd, dr, R_AMAX:R_AMAX + 1, pl.ds(r0, L)])
            stab_s[hd, dr, 0:1, pl.ds(r0, L)] = m
            stab_s[hd, dr, 1:2, pl.ds(r0, L)] = m_end
            out.append(m_end)
        return tuple(out)

    lax.fori_loop(0, NC, stabilisers, tuple(jnp.zeros((1, L), F32) for _ in chains), unroll=4)

    for hd, dr in chains:
        a = rows_s[hd, dr, R_A:R_A + 1, :]
        gt = rows_s[hd, dr, R_GT:R_GT + 1, :]
        m = stab_s[hd, dr, 0:1, :]
        m_end = stab_s[hd, dr, 1:2, :]
        mx = jnp.maximum(m, rows_s[hd, dr, R_MX:R_MX + 1, :])
        b = rows_s[hd, dr, R_EMT:R_EMT + 1, :]
        rows_s[hd, dr, R_MX:R_MX + 1, :] = mx
        rows_s[hd, dr, R_W:R_W + 1, :] = jnp.exp(gt + a - m_end)
        rows_s[hd, dr, R_DEC:R_DEC + 1, :] = jnp.exp(gt + m - m_end)
        rows_s[hd, dr, R_ISC:R_ISC + 1, :] = jnp.exp(m - mx)
        rows_s[hd, dr, R_EMT:R_EMT + 1, :] = jnp.exp(-(b + mx))

    ones_row = (lax.broadcasted_iota(jnp.int32, (BF16_ROWS, L), 0) == 0).astype(BF16)

    def prep(c, carry):
        r0 = pl.multiple_of(c * L, L)
        h_s[:, pl.ds(r0, L)] = jnp.zeros((W, L), F32)
        qpre = qt_ref[0, c].astype(F32)
        q_s[c] = (qpre * _sigmoid(qpre)).astype(BF16)
        for hd in range(HB):
            va_s[hd, 0:D, pl.ds(r0, L)] = vt_ref[0, c, hd * D:(hd + 1) * D, :]
            va_s[hd, D:DA, pl.ds(r0, L)] = ones_row
        return carry

    lax.fori_loop(0, NC, prep, 0)
    c_s[...] = jnp.zeros_like(c_s)

    si = lax.broadcasted_iota(jnp.int32, (L, L), 0)
    ti = lax.broadcasted_iota(jnp.int32, (L, L), 1)
    eye = si == ti
    valid = (si <= ti, si >= ti)

    def chain(hd, dr, c):
        r0 = pl.multiple_of(c * L, L)
        rows = rows_s[hd, dr, :, pl.ds(r0, L)]
        a_col = jnp.sum(jnp.where(eye, rows[R_A:R_A + 1], 0.0), axis=1, keepdims=True)
        wts = jnp.exp(jnp.where(valid[dr], a_col - rows[R_MX:R_MX + 1], NEG_BIG))
        kc = k_ref[0, pl.ds(r0, L), hd * D:(hd + 1) * D]
        qtc = q_s[c, hd * D:(hd + 1) * D, :]
        vac = va_s[hd, :, pl.ds(r0, L)]
        cmat = c_s[hd, dr]
        both = jnp.dot(jnp.concatenate([kc, cmat.astype(BF16)], axis=0), qtc,
                       preferred_element_type=F32)
        s_w = (both[0:L] * wts).astype(BF16)
        tot = (jnp.dot(vac, s_w, preferred_element_type=F32)
               + rows[R_ISC:R_ISC + 1] * both[L:L + DA])
        den = jnp.maximum(jnp.abs(tot[D:D + 1]), rows[R_EMT:R_EMT + 1])
        h = tot[0:D] * (1.0 / den)
        h_s[hd * D:(hd + 1) * D, pl.ds(r0, L)] = h_s[hd * D:(hd + 1) * D, pl.ds(r0, L)] + h
        u = (vac.astype(F32) * rows[R_W:R_W + 1]).astype(BF16)
        c_s[hd, dr] = rows[R_DEC:R_DEC + 1, 0:1] * cmat + jnp.dot(u, kc, preferred_element_type=F32)

    def step(i, carry):
        for hd, dr in chains:
            chain(hd, dr, chunk_of(dr, i))
        return carry

    lax.fori_loop(0, NC, step, 0)

    def fin(c, carry):
        out_ref[0, c] = h_s[:, pl.ds(pl.multiple_of(c * L, L), L)].astype(BF16)
        return carry

    lax.fori_loop(0, NC, fin, 0, unroll=4)


def _mlstm(mt, k_tm, gcm, heads_per_step=MLSTM_HEADS_PER_STEP):
    B, S, _ = k_tm.shape
    NC = S // CHUNK
    D = HEAD_DIM
    HB = heads_per_step
    v0 = (M_WIDTH + 3 * H_WIDTH) // (HB * D)
    W = HB * D
    NB = M_HEADS // HB
    DA = D + BF16_ROWS
    cm_blk = (1, NC, W, LANES)
    return pl.pallas_call(
        _mlstm_kernel,
        grid=(B, NB),
        in_specs=[
            pl.BlockSpec(cm_blk, lambda b, h: (b, 0, h, 0)),
            pl.BlockSpec(cm_blk, lambda b, h: (b, 0, v0 + h, 0)),
            pl.BlockSpec((1, S, W), lambda b, h: (b, 0, h)),
            pl.BlockSpec((1, HB, N_GATE // M_HEADS, S), lambda b, h: (b, h, 0, 0)),
        ],
        out_specs=pl.BlockSpec(cm_blk, lambda b, h: (b, 0, h, 0)),
        out_shape=jax.ShapeDtypeStruct((B, NC, M_WIDTH, LANES), BF16),
        scratch_shapes=[
            pltpu.VMEM((NC, W, LANES), BF16),
            pltpu.VMEM((HB, DA, S), BF16),
            pltpu.VMEM((HB, 2, 8, S), F32),
            pltpu.VMEM((HB, 2, 8, S), F32),
            pltpu.VMEM((W, S), F32),
            pltpu.VMEM((HB, 2, DA, D), F32),
        ],
        compiler_params=_cparams("parallel", "parallel"),
        name="mlstm",
    )(mt, mt, k_tm, gcm)


def _filt_mlp_kernel(w1t_ref, b1_ref, w2_ref, b2_ref, w3_ref, b3_ref, fr_ref, w4f_ref, w4b_ref,
                     fbias_ref, out_ref, h3_s, *, cb_rows):
    S = h3_s.shape[2]
    hi = lax.Precision.HIGHEST
    tn_dims = (((0,), (0,)), ((), ()))
    lane = lax.broadcasted_iota(jnp.int32, (1, S), 1)
    pos_f = lane.astype(F32)
    pos_b = (S - lane).astype(F32)

    def features(pos):
        bands = (FILTER_EMB - 1) // 2
        t = pos / (S - 1)
        ang = (2.0 * math.pi) * pos / S
        fidx = lax.broadcasted_iota(jnp.int32, (bands, 1), 0).astype(F32)
        f = 1e-4 + fidx * ((bands - 1 - 1e-4) / (bands - 1))
        fa = f * ang
        w1t = w1t_ref[...]
        pre = (w1t[:, 0:1] * t
               + jnp.dot(w1t[:, 1:1 + bands], jnp.cos(fa), precision=hi, preferred_element_type=F32)
               - jnp.dot(w1t[:, 1 + bands:], jnp.sin(fa), precision=hi, preferred_element_type=F32))
        fr = fr_ref[...]
        h = jnp.sin(fr[:, 0:1] * (pre + b1_ref[...]))
        h = jnp.sin(fr[:, 1:2] * (lax.dot_general(w2_ref[...], h, tn_dims, precision=hi,
                                                  preferred_element_type=F32) + b2_ref[...]))
        return jnp.sin(fr[:, 2:3] * (lax.dot_general(w3_ref[...], h, tn_dims, precision=hi,
                                                     preferred_element_type=F32) + b3_ref[...]))

    @pl.when(pl.program_id(0) == 0)
    def _():
        h3 = features(pos_f)
        h3_s[0] = h3
        n_t = S // LANES
        anti = (lax.broadcasted_iota(jnp.int32, (LANES, LANES), 0)
                + lax.broadcasted_iota(jnp.int32, (LANES, LANES), 1) == LANES - 1).astype(F32)
        rev = jnp.concatenate(
            [jnp.dot(h3[:, (n_t - 1 - u) * LANES:(n_t - u) * LANES], anti, precision=hi,
                     preferred_element_type=F32) for u in range(n_t)], axis=1)
        h3_s[1] = pltpu.roll(rev, 1, 1)

    r = pl.program_id(0) * cb_rows + lax.broadcasted_iota(jnp.int32, (cb_rows, 1), 0)
    ch = (r % H_WIDTH).astype(F32)
    max_decay = math.log(DECAY_TARGET) / FAST_DECAY_PCT
    min_decay = math.log(DECAY_TARGET) / SLOW_DECAY_PCT
    delta = jnp.abs(min_decay + ch * ((max_decay - min_decay) / (H_WIDTH - 1)))
    n_t = S // LANES
    for half, (w_ref, pos) in enumerate(((w4f_ref, pos_f), (w4b_ref, pos_b))):
        filt = lax.dot_general(w_ref[...].astype(BF16), h3_s[half].astype(BF16), tn_dims,
                               preferred_element_type=F32)
        filt = filt * jnp.exp(-(pos / (S - 1)) * delta)
        if half == 0:
            filt = jnp.where(lane == 0, filt + fbias_ref[...], filt)
        else:
            filt = jnp.where(lane == 0, 0.0, filt)
        for u in range(n_t):
            out_ref[half * n_t + u] = filt[:, u * LANES:(u + 1) * LANES].astype(BF16)


def _filt_mlp(w1t, cols, fb_blk, fbias_blk, w2, w3, fr, w4, S, cb_rows=FILT_MLP_ROWS):
    R = w4.shape[1] // 2
    Hd = FILTER_HIDDEN
    const = lambda i: (0, 0)
    nblk = R // cb_rows
    return pl.pallas_call(
        functools.partial(_filt_mlp_kernel, cb_rows=cb_rows),
        grid=(nblk,),
        in_specs=[
            pl.BlockSpec((Hd, FILTER_EMB), const),
            pl.BlockSpec((Hd, 1), lambda i: (fb_blk, 0)),
            pl.BlockSpec((Hd, Hd), const),
            pl.BlockSpec((Hd, 1), lambda i: (fb_blk + 1, 0)),
            pl.BlockSpec((Hd, Hd), const),
            pl.BlockSpec((Hd, 1), lambda i: (fb_blk + 2, 0)),
            pl.BlockSpec((Hd, 3), const),
            pl.BlockSpec((Hd, cb_rows), lambda i: (0, i)),
            pl.BlockSpec((Hd, cb_rows), lambda i: (0, nblk + i)),
            pl.BlockSpec((cb_rows, 1), lambda i: (fbias_blk + i, 0)),
        ],
        out_specs=pl.BlockSpec((2 * S // LANES, cb_rows, LANES), lambda i: (0, i, 0)),
        out_shape=jax.ShapeDtypeStruct((2 * S // LANES, R, LANES), BF16),
        scratch_shapes=[pltpu.VMEM((2, Hd, S), F32)],
        compiler_params=_cparams("arbitrary"),
        name="filt_mlp",
    )(w1t, cols, w2, cols, w3, cols, fr, w4, w4, cols)


@functools.lru_cache(maxsize=None)
def _dft_constants(S):
    N = 2 * S
    N2 = LANES
    N1 = N // N2
    h = N1 // 2
    k1 = np.arange(N1)
    k2 = np.arange(N2)
    a1 = -2.0 * np.pi * np.outer(k1, k1) / N1
    f1r, f1i = np.cos(a1), np.sin(a1)
    at = -2.0 * np.pi * np.outer(k1, k2) / N
    twr, twi = np.cos(at), np.sin(at)
    a2 = -2.0 * np.pi * np.outer(k2, k2) / N2
    f2r, f2i = np.cos(a2), np.sin(a2)
    s1c = np.block([[f1r[:, :h], -f1i[:, :h]], [f1i[:, :h], f1r[:, :h]]])
    s1r = np.concatenate([f1r, f1i], axis=0)
    s2 = np.block([[f2r, f2i], [-f2i, f2r]])
    s2i = np.block([[f2r, -f2i], [f2i, f2r]])
    s1i = np.block([[f1r[:h, :], f1i[:h, :]], [-f1i[:h, :], f1r[:h, :]]])
    cast = lambda a: np.asarray(a, np.float32)
    return dict(s1c=cast(s1c), s1r=cast(s1r), s2=cast(s2), s2i=cast(s2i), s1i=cast(s1i),
                twr=cast(twr), twi=cast(twi), N1=N1, h=h)


def _bf(a):
    return jnp.asarray(a, F32).astype(BF16)


def _filt_fft_kernel(kern_ref, s1r_ref, s2_ref, twr_ref, twi_ref, out_ref, src_s, slab_s, *, unroll):
    N1 = twr_ref.shape[0]
    n_ch = kern_ref.shape[1]
    pitch = src_s.shape[0] // N1
    scale = 1.0 / (N1 * LANES)
    twr = twr_ref[...]
    twi = twi_ref[...]
    s1r = s1r_ref[...]

    def load_in(t1, carry):
        src_s[pl.ds(pl.multiple_of(t1 * pitch, 8), n_ch), :] = kern_ref[t1].astype(F32)
        return carry
    lax.fori_loop(0, N1, load_in, 0, unroll=4)

    def per_group(j, carry):
        for u in range(0, unroll, 2):
            cs = (j * unroll + u, j * unroll + u + 1)
            z = jnp.concatenate([src_s[pl.ds(c, N1, stride=pitch), :] for c in cs], axis=1)
            a = jnp.dot(s1r, z.astype(BF16), preferred_element_type=F32)
            for i, c in enumerate(cs):
                ar = a[0:N1, i * LANES:(i + 1) * LANES]
                ai = a[N1:2 * N1, i * LANES:(i + 1) * LANES]
                r0 = pl.multiple_of(c * N1, N1)
                slab_s[pl.ds(r0, N1), 0:LANES] = (ar * twr - ai * twi).astype(BF16)
                slab_s[pl.ds(r0, N1), LANES:2 * LANES] = (ar * twi + ai * twr).astype(BF16)
        return carry

    lax.fori_loop(0, n_ch // unroll, per_group, 0)
    x = jnp.dot(slab_s[...], s2_ref[...], preferred_element_type=F32)
    out_ref[...] = (x * scale).astype(BF16).reshape(n_ch, N1, 2 * LANES)


def _filt_fft(kern, consts, n_ch=FILT_FFT_ROWS, unroll=FILT_FFT_ROWS):
    N1, R, _ = kern.shape
    const = lambda i: (0, 0)
    pitch = n_ch + 8
    return pl.pallas_call(
        functools.partial(_filt_fft_kernel, unroll=unroll),
        grid=(R // n_ch,),
        in_specs=[
            pl.BlockSpec((N1, n_ch, LANES), lambda i: (0, i, 0)),
            pl.BlockSpec((2 * N1, N1), const),
            pl.BlockSpec((2 * LANES, 2 * LANES), const),
            pl.BlockSpec((N1, LANES), const),
            pl.BlockSpec((N1, LANES), const),
        ],
        out_specs=pl.BlockSpec((n_ch, N1, 2 * LANES), lambda i: (i, 0, 0)),
        out_shape=jax.ShapeDtypeStruct((R, N1, 2 * LANES), BF16),
        scratch_shapes=[pltpu.VMEM((N1 * pitch, LANES), F32),
                        pltpu.VMEM((n_ch * N1, 2 * LANES), BF16)],
        compiler_params=_cparams("parallel"),
        name="filt_fft",
    )(kern, _bf(consts["s1r"]), _bf(consts["s2"]),
      jnp.asarray(consts["twr"]), jnp.asarray(consts["twi"]))


def _hyena_kernel(nw_ref, v_ref, x1_ref, x2_ref, khat_ref, s1c_ref, s2_ref, s2i_ref, s1i_ref,
                  twr_ref, twi_ref, out_ref, src_s, slab_s, yf_s, q_s, y_s, *, unroll):
    h = v_ref.shape[1]
    n_ch = v_ref.shape[2]
    N1 = 2 * h
    twr = twr_ref[...]
    twi = twi_ref[...]
    s1c = s1c_ref[...]
    s1i = s1i_ref[...]
    n_groups = n_ch // unroll
    rows = unroll * N1
    pitch = src_s.shape[0] // (2 * h)

    def tile_rows(i):
        return pl.ds(pl.multiple_of(i * pitch, 8), n_ch)

    def chan_rows(c, bb):
        return pl.ds(bb * h * pitch + c, h, stride=pitch)

    def long_conv(order):
        def stage_a(g):
            for u in range(0, unroll, 2):
                cs = (g * unroll + u, g * unroll + u + 1)
                z = jnp.concatenate(
                    [jnp.concatenate([src_s[chan_rows(c, 0), :], src_s[chan_rows(c, 1), :]], axis=0)
                     for c in cs], axis=1)
                a = jnp.dot(s1c, z.astype(BF16), preferred_element_type=F32)
                for i, c in enumerate(cs):
                    ar = a[0:N1, i * LANES:(i + 1) * LANES]
                    ai = a[N1:2 * N1, i * LANES:(i + 1) * LANES]
                    r0 = c * N1 if isinstance(c, int) else pl.multiple_of(c * N1, N1)
                    slab_s[pl.ds(r0, N1), 0:LANES] = (ar * twr - ai * twi).astype(BF16)
                    slab_s[pl.ds(r0, N1), LANES:2 * LANES] = (ar * twi + ai * twr).astype(BF16)

        def stage_m1(g):
            r0 = g * rows if isinstance(g, int) else pl.multiple_of(g * rows, rows)
            c0 = g * unroll if isinstance(g, int) else pl.multiple_of(g * unroll, unroll)
            x = jnp.dot(slab_s[pl.ds(r0, rows), :], s2_ref[...], preferred_element_type=F32)
            xr = x[:, 0:LANES].reshape(unroll, N1, LANES)
            xi = x[:, LANES:2 * LANES].reshape(unroll, N1, LANES)
            kr = khat_ref[order, pl.ds(c0, unroll), :, 0:LANES].astype(F32)
            ki = khat_ref[order, pl.ds(c0, unroll), :, LANES:2 * LANES].astype(F32)
            yf_s[pl.ds(r0, rows), 0:LANES] = (xr * kr - xi * ki).reshape(rows, LANES).astype(BF16)
            yf_s[pl.ds(r0, rows), LANES:2 * LANES] = (xr * ki + xi * kr).reshape(rows, LANES).astype(BF16)

        def stage_m2(g):
            r0 = g * rows if isinstance(g, int) else pl.multiple_of(g * rows, rows)
            c0 = g * unroll if isinstance(g, int) else pl.multiple_of(g * unroll, unroll)
            p = jnp.dot(yf_s[pl.ds(r0, rows), :], s2i_ref[...], preferred_element_type=F32)
            pr = p[:, 0:LANES].reshape(unroll, N1, LANES)
            pi = p[:, LANES:2 * LANES].reshape(unroll, N1, LANES)
            q_s[pl.ds(c0, unroll), 0:N1, :] = (pr * twr + pi * twi).astype(BF16)
            q_s[pl.ds(c0, unroll), N1:2 * N1, :] = (pi * twr - pr * twi).astype(BF16)

        def stage_i(g):
            for u in range(0, unroll, 2):
                cs = (g * unroll + u, g * unroll + u + 1)
                q2 = jnp.concatenate([q_s[c] for c in cs], axis=1)
                y = jnp.dot(s1i, q2, preferred_element_type=F32)
                for i, c in enumerate(cs):
                    y_s[chan_rows(c, 0), :] = y[0:h, i * LANES:(i + 1) * LANES]
                    y_s[chan_rows(c, 1), :] = y[h:2 * h, i * LANES:(i + 1) * LANES]

        G = n_groups
        for t in range(G + 3):
            if 0 <= t - 3 < G:
                stage_i(t - 3)
            if 0 <= t - 2 < G:
                stage_m2(t - 2)
            if 0 <= t - 1 < G:
                stage_m1(t - 1)
            if t < G:
                stage_a(t)

    def per_tile(fn):
        for bb in range(2):
            def body(t1, carry, bb=bb):
                fn(bb, t1, tile_rows(bb * h + t1))
                return carry
            lax.fori_loop(0, h, body, 0, unroll=8)

    def load_in(bb, t1, rws):
        src_s[rws, :] = v_ref[bb, t1].astype(F32)
    per_tile(load_in)
    long_conv(0)

    def gate1(bb, t1, rws):
        src_s[rws, :] = y_s[rws, :] * x1_ref[bb, t1].astype(F32)
    per_tile(gate1)
    long_conv(1)

    nw = jnp.broadcast_to(nw_ref[...], (n_ch, LANES))

    def gate2_norm(bb, t1, rws):
        z = y_s[rws, :] * x2_ref[bb, t1].astype(F32)
        ms = jnp.mean(z * z, axis=0, keepdims=True)
        out_ref[bb, t1] = (z * lax.rsqrt(ms + EPS) * nw).astype(BF16)
    per_tile(gate2_norm)


def _hyena(cols, nw_blk, hy4, khat, consts, unroll=HYENA_GROUP):
    B, h, _, _ = hy4.shape
    C = H_WIDTH
    n_ch = GROUP_W
    G = C // n_ch
    g0 = M_WIDTH // n_ch
    N1 = consts["N1"]
    const = lambda g, p: (0, 0)
    blk = (2, h, n_ch, LANES)
    pitch = n_ch + 8
    return pl.pallas_call(
        functools.partial(_hyena_kernel, unroll=unroll),
        grid=(G, B // 2),
        in_specs=[
            pl.BlockSpec((n_ch, 1), lambda g, p: (nw_blk + g, 0)),
            pl.BlockSpec(blk, lambda g, p: (p, 0, g0 + g, 0)),
            pl.BlockSpec(blk, lambda g, p: (p, 0, g0 + G + g, 0)),
            pl.BlockSpec(blk, lambda g, p: (p, 0, g0 + 2 * G + g, 0)),
            pl.BlockSpec((2, n_ch, N1, 2 * LANES), lambda g, p: (0, g, 0, 0)),
            pl.BlockSpec((2 * N1, 2 * h), const),
            pl.BlockSpec((2 * LANES, 2 * LANES), const),
            pl.BlockSpec((2 * LANES, 2 * LANES), const),
            pl.BlockSpec((2 * h, 2 * N1), const),
            pl.BlockSpec((N1, LANES), const),
            pl.BlockSpec((N1, LANES), const),
        ],
        out_specs=pl.BlockSpec(blk, lambda g, p: (p, 0, g, 0)),
        out_shape=jax.ShapeDtypeStruct((B, h, C, LANES), BF16),
        scratch_shapes=[
            pltpu.VMEM((2 * h * pitch, LANES), F32),
            pltpu.VMEM((n_ch * N1, 2 * LANES), BF16),
            pltpu.VMEM((n_ch * N1, 2 * LANES), BF16),
            pltpu.VMEM((n_ch, 2 * N1, LANES), BF16),
            pltpu.VMEM((2 * h * pitch, LANES), F32),
        ],
        compiler_params=_cparams("parallel", "arbitrary"),
        name="hyena",
    )(cols, hy4, hy4, hy4, khat,
      _bf(consts["s1c"]), _bf(consts["s2"]), _bf(consts["s2i"]), _bf(consts["s1i"]),
      jnp.asarray(consts["twr"]), jnp.asarray(consts["twi"]))


def _outmlp_kernel(x_ref, hm_ref, og_ref, yh_ref, nwm_ref, wo_ref, w1_ref, w2_ref, n_post_ref,
                   n_pre_ref, n_post2_ref, out_ref):
    n_t = hm_ref.shape[1]
    nwm = jnp.broadcast_to(nwm_ref[...], (M_WIDTH, LANES))
    tiles = []
    for j in range(n_t):
        hg = hm_ref[0, j].astype(F32) * _sigmoid(og_ref[0, j].astype(F32))
        heads = []
        for hd in range(M_HEADS):
            hh = hg[hd * HEAD_DIM:(hd + 1) * HEAD_DIM]
            ms = jnp.mean(hh * hh, axis=0, keepdims=True)
            heads.append(hh * lax.rsqrt(ms + EPS))
        ym = jnp.concatenate(heads, axis=0) * nwm
        yt = jnp.concatenate([ym, yh_ref[0, j].astype(F32)], axis=0)
        tiles.append(yt.T.astype(BF16))
    y = tiles[0] if n_t == 1 else jnp.concatenate(tiles, axis=0)
    mix = jnp.dot(y, wo_ref[...], preferred_element_type=F32)
    x1 = x_ref[0] + _rms_rows(mix, n_post_ref[...])
    hm = _rms_rows(x1, n_pre_ref[...]).astype(BF16)
    mid = jnp.maximum(jnp.dot(hm, w1_ref[...], preferred_element_type=F32), 0.0)
    mid = (mid * mid).astype(BF16)
    ff = jnp.dot(mid, w2_ref[...], preferred_element_type=F32)
    out_ref[0] = x1 + _rms_rows(ff, n_post2_ref[...])


def _outmlp(x, hm, mt, yh, cols, nwm_blk, wo, w1, w2, n_post, n_pre, n_post2, tm_rows):
    B, S, D = x.shape
    n_t = tm_rows // LANES
    og0 = mt.shape[2] // M_WIDTH - 1
    const = lambda b, i: (0, 0)
    resident = functools.partial(pl.BlockSpec, index_map=const, pipeline_mode=pl.Buffered(1))
    return pl.pallas_call(
        _outmlp_kernel,
        grid=(B, S // tm_rows),
        in_specs=[
            pl.BlockSpec((1, tm_rows, D), lambda b, i: (b, i, 0)),
            pl.BlockSpec((1, n_t, M_WIDTH, LANES), lambda b, i: (b, i, 0, 0)),
            pl.BlockSpec((1, n_t, M_WIDTH, LANES), lambda b, i: (b, i, og0, 0)),
            pl.BlockSpec((1, n_t, H_WIDTH, LANES), lambda b, i: (b, i, 0, 0)),
            pl.BlockSpec((M_WIDTH, 1), lambda b, i: (nwm_blk, 0)),
            resident((D, D)),
            resident((D, D_FF)),
            resident((D_FF, D)),
            pl.BlockSpec((1, D), const),
            pl.BlockSpec((1, D), const),
            pl.BlockSpec((1, D), const),
        ],
        out_specs=pl.BlockSpec((1, tm_rows, D), lambda b, i: (b, i, 0)),
        out_shape=jax.ShapeDtypeStruct((B, S, D), F32),
        compiler_params=_cparams("parallel", "parallel"),
        name="outmlp",
    )(x, hm, mt, yh, cols, wo, w1, w2, n_post, n_pre, n_post2)


def kernel(x, norm_mix_pre, norm_mix_post, norm_mlp_pre, norm_mlp_post, w_in, b_gates,
           conv_w, conv_b, mlstm_norm_w, hyena_norm_w, filt_w1, filt_b1, filt_w2, filt_b2,
           filt_w3, filt_b3, filt_w4, filt_freq, filt_bias, w_out, w_mlp_in, w_mlp_out):
    B, S, D = x.shape
    assert D == D_MODEL and B % 2 == 0 and S % CHUNK == 0
    row = lambda a: a.astype(F32).reshape(1, -1)
    tm_rows = min(PROJ_ROWS, S)
    cols = jnp.concatenate([filt_bias.reshape(-1), mlstm_norm_w, hyena_norm_w, filt_b1, filt_b2,
                            filt_b3, b_gates]).astype(F32).reshape(-1, 1)
    off_mn = HYENA_ORDER * H_WIDTH
    off_hn = off_mn + M_WIDTH
    off_fb = off_hn + H_WIDTH
    off_bg = off_fb + 3 * FILTER_HIDDEN
    assert (off_mn % M_WIDTH == 0 and off_hn % GROUP_W == 0 and off_fb % FILTER_HIDDEN == 0
            and off_bg % N_GATE == 0 and off_mn % FILT_MLP_ROWS == 0)

    n_conv = 2 * M_WIDTH + 3 * H_WIDTH
    o_gate = n_conv + 2 * M_WIDTH
    assert w_in.shape == (D, o_gate + N_GATE)
    assert conv_w.shape == (3, n_conv)
    mt, k_tm, gcm = _proj(x, row(norm_mix_pre), w_in.astype(F32).T, cols, off_bg // N_GATE,
                          conv_w.astype(F32), row(conv_b), tm_rows)

    h_m = _mlstm(mt, k_tm, gcm)

    consts = _dft_constants(S)
    kern = _filt_mlp(filt_w1.astype(F32).T, cols, off_fb // FILTER_HIDDEN, 0, filt_w2.astype(F32),
                     filt_w3.astype(F32), filt_freq.astype(F32).T, filt_w4.astype(F32), S)
    khat = _filt_fft(kern, consts)
    khat = khat.reshape(2, H_WIDTH, consts["N1"], 2 * LANES)

    y_h = _hyena(cols, off_hn // GROUP_W, mt, khat, consts)

    return _outmlp(x, h_m, mt, y_h, cols, off_mn // M_WIDTH,
                   w_out.astype(BF16), w_mlp_in.astype(BF16), w_mlp_out.astype(BF16),
                   row(norm_mix_post), row(norm_mlp_pre), row(norm_mlp_post), min(OUTMLP_ROWS, S))
```

```python
import functools
import math

import numpy as np
import jax
import jax.numpy as jnp
from jax import lax
from jax.experimental import pallas as pl
from jax.experimental.pallas import tpu as pltpu

F32 = jnp.float32
BF16 = jnp.bfloat16

D_MODEL = 1024
M_WIDTH = 512
M_HEADS = 4
HEAD_DIM = 128
H_WIDTH = 512
H_GROUPS = 8
HYENA_ORDER = 2
GROUP_W = H_WIDTH // H_GROUPS
CHUNK = 128
FILTER_EMB = 33
FILTER_HIDDEN = 64
DECAY_TARGET = 1e-2
FAST_DECAY_PCT = 0.3
SLOW_DECAY_PCT = 1.5
D_FF = 4 * D_MODEL
N_GATE = 16
EPS = 1e-6
LANES = 128
BF16_ROWS = 16
NEG_BIG = -1e30
VMEM_LIMIT = 56 * 1024 * 1024

PROJ_ROWS = 1024
PROJ_COL_BLK = 256
MLSTM_HEADS_PER_STEP = 4
FILT_MLP_ROWS = 256
FILT_FFT_ROWS = 32
HYENA_GROUP = 8
OUTMLP_ROWS = 512


def _cparams(*sem):
    return pltpu.CompilerParams(dimension_semantics=sem, vmem_limit_bytes=VMEM_LIMIT)


def _rms_rows(xf, w):
    ms = jnp.mean(xf * xf, axis=-1, keepdims=True)
    return xf * lax.rsqrt(ms + EPS) * w


def _sigmoid(x):
    return 1.0 / (1.0 + jnp.exp(-x))


def _log_sigmoid(x):
    return jnp.minimum(x, 0.0) - jnp.log(1.0 + jnp.exp(-jnp.abs(x)))


def _proj_kernel(x_ref, xp_ref, xn_ref, nw_ref, wt_ref, bgt_ref, cw_ref, cb_ref,
                 cm_ref, k_ref, gcm_ref, wbf_s, hall_s, pc_s, *, col_blk):
    TM = x_ref.shape[1]
    HALO = xp_ref.shape[1]
    j = pl.program_id(1)

    @pl.when(jnp.logical_and(pl.program_id(0) == 0, j == 0))
    def _():
        for c0 in range(0, wbf_s.shape[1], col_blk):
            wbf_s[:, c0:c0 + col_blk] = wt_ref[c0:c0 + col_blk, :].T.astype(BF16)

    nw = nw_ref[...]
    hn = _rms_rows(x_ref[0], nw).astype(BF16)
    hall_s[0:HALO, :] = _rms_rows(xp_ref[0], nw).astype(BF16)
    hall_s[HALO:HALO + TM, :] = hn
    hall_s[HALO + TM:2 * HALO + TM, :] = _rms_rows(xn_ref[0], nw).astype(BF16)
    keep_p = jnp.where(j > 0, 1.0, 0.0)
    keep_n = jnp.where(j < pl.num_programs(1) - 1, 1.0, 0.0)
    n_cm = cm_ref.shape[2]
    n_conv = cw_ref.shape[1]
    n_blk = n_cm // col_blk
    w_k0 = M_WIDTH
    w_gate = n_cm + M_WIDTH
    n_typ = N_GATE // M_HEADS
    order = [t * M_HEADS + hd for hd in range(M_HEADS) for t in range(n_typ)]
    wg = jnp.concatenate([wt_ref[w_gate + r:w_gate + r + 1, :] for r in order], axis=0)
    bg = jnp.concatenate([bgt_ref[r:r + 1, :] for r in order], axis=0)
    nt_dims = (((1,), (1,)), ((), ()))
    gt = lax.dot_general(wg.astype(BF16), hn, nt_dims, preferred_element_type=F32) + bg
    row = lax.broadcasted_iota(jnp.int32, gt.shape, 0)
    lg = jnp.where((row % 2) == 1, _log_sigmoid(gt), gt)
    for hd in range(M_HEADS):
        gcm_ref[0, hd] = lg[hd * n_typ:(hd + 1) * n_typ]

    n_rb = TM // LANES
    n_sl = col_blk // LANES

    def project(w_row0, slot):
        wblk = wbf_s[:, pl.ds(w_row0, col_blk)]
        res = jnp.dot(hall_s[...], wblk, preferred_element_type=F32)
        for u in range(n_sl):
            ls = slice(u * LANES, (u + 1) * LANES)
            pc_s[slot, u, 0:HALO, :] = res[0:HALO, ls] * keep_p
            pc_s[slot, u, HALO:HALO + TM, :] = res[HALO:HALO + TM, ls]
            pc_s[slot, u, HALO + TM:2 * HALO + TM, :] = res[HALO + TM:2 * HALO + TM, ls] * keep_n

    def conv_tile(slot, u, rb, taps, bias):
        r = HALO + rb * LANES
        return (pc_s[slot, u, pl.ds(r - 1, LANES, stride=1), :] * taps[0:1]
                + pc_s[slot, u, r:r + LANES, :] * taps[1:2]
                + pc_s[slot, u, pl.ds(r + 1, LANES, stride=1), :] * taps[2:3] + bias)

    n_q = M_WIDTH // col_blk
    assert n_blk % 2 == 0

    def w_row_of(c):
        if c == n_blk:
            return w_k0
        return (c if c < n_q else c + n_q) * col_blk

    def emit(c, slot):
        for u in range(n_sl):
            cs = slice(c * col_blk + u * LANES, c * col_blk + (u + 1) * LANES)
            w0 = w_row_of(c) + u * LANES
            for rb in range(n_rb):
                if w0 < n_conv:
                    cv = conv_tile(slot, u, rb, cw_ref[:, w0:w0 + LANES], cb_ref[:, w0:w0 + LANES])
                else:
                    cv = pc_s[slot, u, HALO + rb * LANES:HALO + (rb + 1) * LANES, :]
                cm_ref[0, rb, cs, :] = cv.T.astype(BF16)

    project(0, 0)
    for c in range(n_blk):
        emit(c, c % 2)
        project(w_row_of(c + 1), (c + 1) % 2)

    k_scale = HEAD_DIM ** -0.5
    n_k = M_WIDTH // col_blk
    for kb in range(n_k):
        if kb > 0:
            project(w_k0 + kb * col_blk, kb % 2)
        for u in range(n_sl):
            cs = slice(kb * col_blk + u * LANES, kb * col_blk + (u + 1) * LANES)
            for rb in range(n_rb):
                ks = slice(w_k0 + cs.start, w_k0 + cs.stop)
                cv = conv_tile(kb % 2, u, rb, cw_ref[:, ks], cb_ref[:, ks])
                k_ref[0, rb * LANES:(rb + 1) * LANES, cs] = (cv * _sigmoid(cv) * k_scale).astype(BF16)


def _proj(x, nw, w_in_t, cols, bg_blk, cw, cb, tm_rows, col_blk=PROJ_COL_BLK):
    B, S, D = x.shape
    HALO = BF16_ROWS
    n_conv = cw.shape[1]
    n_cm = w_in_t.shape[0] - N_GATE - M_WIDTH
    n_t = tm_rows // LANES
    hb = tm_rows // HALO
    last = S // HALO - 1
    const = lambda b, j: (0, 0)
    resident = functools.partial(pl.BlockSpec, index_map=const, pipeline_mode=pl.Buffered(1))
    return pl.pallas_call(
        functools.partial(_proj_kernel, col_blk=col_blk),
        grid=(B, S // tm_rows),
        in_specs=[
            pl.BlockSpec((1, tm_rows, D), lambda b, j: (b, j, 0)),
            pl.BlockSpec((1, HALO, D), lambda b, j: (b, jnp.maximum(j * hb - 1, 0), 0)),
            pl.BlockSpec((1, HALO, D), lambda b, j: (b, jnp.minimum((j + 1) * hb, last), 0)),
            pl.BlockSpec((1, D), const),
            resident(w_in_t.shape),
            pl.BlockSpec((N_GATE, 1), lambda b, j: (bg_blk, 0)),
            pl.BlockSpec((3, n_conv), const),
            pl.BlockSpec((1, n_conv), const),
        ],
        out_specs=[
            pl.BlockSpec((1, n_t, n_cm, LANES), lambda b, j: (b, j, 0, 0)),
            pl.BlockSpec((1, tm_rows, M_WIDTH), lambda b, j: (b, j, 0)),
            pl.BlockSpec((1, M_HEADS, N_GATE // M_HEADS, tm_rows), lambda b, j: (b, 0, 0, j)),
        ],
        out_shape=[
            jax.ShapeDtypeStruct((B, S // LANES, n_cm, LANES), BF16),
            jax.ShapeDtypeStruct((B, S, M_WIDTH), BF16),
            jax.ShapeDtypeStruct((B, M_HEADS, N_GATE // M_HEADS, S), F32),
        ],
        scratch_shapes=[pltpu.VMEM((D, n_cm + M_WIDTH), BF16),
                        pltpu.VMEM((tm_rows + 2 * HALO, D), BF16),
                        pltpu.VMEM((2, col_blk // LANES, tm_rows + 2 * HALO, LANES), F32)],
        compiler_params=_cparams("arbitrary", "arbitrary"),
        name="proj",
    )(x, x, x, nw, w_in_t, cols, cw, cb)


def _mlstm_kernel(qt_ref, vt_ref, k_ref, g_ref, out_ref, rows_s, stab_s, h_s, c_s):
    S = k_ref.shape[1]
    HB = g_ref.shape[1]
    L = CHUNK
    NC = S // L
    D = HEAD_DIM
    DA = D + BF16_ROWS
    W = HB * D

    R_A, R_MX, R_W, R_DEC, R_ISC, R_EMT, R_GT, R_AMAX = range(8)

    def scan(x, op, fill):
        lane = lax.broadcasted_iota(jnp.int32, x.shape, 1) % L
        pre = x
        suf = x
        d = 1
        while d < L:
            pre = op(pre, jnp.where(lane >= d, pltpu.roll(pre, d, 1), fill))
            suf = op(suf, jnp.where(lane < L - d, pltpu.roll(suf, S - d, 1), fill))
            d *= 2
        return pre, suf

    chains = [(hd, dr) for hd in range(HB) for dr in range(2)]
    assert HB % 2 == 0 and len(chains) <= 8
    a_rows = {}
    for h0 in range(0, HB, 2):
        g = jnp.concatenate([g_ref[0, h0, 0:4, :], g_ref[0, h0 + 1, 0:4, :]], axis=0)
        pre, suf = scan(g, jnp.add, 0.0)
        tot = pre + suf - g
        for j, hd in enumerate((h0, h0 + 1)):
            o = 4 * j
            b_rows = (pre[o + 1:o + 2], suf[o + 3:o + 4])
            for dr in range(2):
                a_rows[hd, dr] = g[o + 2 * dr:o + 2 * dr + 1] - b_rows[dr]
                rows_s[hd, dr, R_A:R_A + 1, :] = a_rows[hd, dr]
                rows_s[hd, dr, R_EMT:R_EMT + 1, :] = b_rows[dr]
                rows_s[hd, dr, R_GT:R_GT + 1, :] = tot[o + 2 * dr + 1:o + 2 * dr + 2]
    a8 = jnp.concatenate([a_rows[ch] for ch in chains]
                         + [jnp.zeros((8 - len(chains), S), F32)] * (len(chains) < 8), axis=0)
    pmax, smax = scan(a8, jnp.maximum, NEG_BIG)
    for i, (hd, dr) in enumerate(chains):
        rows_s[hd, dr, R_MX:R_MX + 1, :] = pmax[i:i + 1] if dr == 0 else smax[i:i + 1]
        rows_s[hd, dr, R_AMAX:R_AMAX + 1, :] = jnp.maximum(pmax[i:i + 1], smax[i:i + 1])

    def chunk_of(dr, i):
        return i if dr == 0 else NC - 1 - i

    def stabilisers(i, ms):
        out = []
        for (hd, dr), m in zip(chains, ms):
            r0 = pl.multiple_of(chunk_of(dr, i) * L, L)
            gt = rows_s[hd, dr, R_GT:R_GT + 1, pl.ds(r0, L)]
            m_end = jnp.maximum(gt + m, gt + rows_s[hd, dr, R_AMAX:R_AMAX + 1, pl.ds(r0, L)])
            stab_s[hd, dr, 0:1, pl.ds(r0, L)] = m
            stab_s[hd, dr, 1:2, pl.ds(r0, L)] = m_end
            out.append(m_end)
        return tuple(out)

    lax.fori_loop(0, NC, stabilisers, tuple(jnp.zeros((1, L), F32) for _ in chains), unroll=4)

    for hd, dr in chains:
        a = rows_s[hd, dr, R_A:R_A + 1, :]
        gt = rows_s[hd, dr, R_GT:R_GT + 1, :]
        m = stab_s[hd, dr, 0:1, :]
        m_end = stab_s[hd, dr, 1:2, :]
        mx = jnp.maximum(m, rows_s[hd, dr, R_MX:R_MX + 1, :])
        b = rows_s[hd, dr, R_EMT:R_EMT + 1, :]
        rows_s[hd, dr, R_MX:R_MX + 1, :] = mx
        rows_s[hd, dr, R_W:R_W + 1, :] = jnp.exp(gt + a - m_end)
        rows_s[hd, dr, R_DEC:R_DEC + 1, :] = jnp.exp(gt + m - m_end)
        rows_s[hd, dr, R_ISC:R_ISC + 1, :] = jnp.exp(m - mx)
        rows_s[hd, dr, R_EMT:R_EMT + 1, :] = jnp.exp(-(b + mx))

    ones_row = (lax.broadcasted_iota(jnp.int32, (BF16_ROWS, L), 0) == 0).astype(BF16)

    def clear(c, carry):
        h_s[:, pl.ds(pl.multiple_of(c * L, L), L)] = jnp.zeros((W, L), F32)
        return carry

    lax.fori_loop(0, NC, clear, 0, unroll=4)
    c_s[...] = jnp.zeros_like(c_s)

    si = lax.broadcasted_iota(jnp.int32, (L, L), 0)
    ti = lax.broadcasted_iota(jnp.int32, (L, L), 1)
    eye = si == ti
    valid = (si <= ti, si >= ti)

    def chain(hd, dr, c):
        r0 = pl.multiple_of(c * L, L)
        rows = rows_s[hd, dr, :, pl.ds(r0, L)]
        a_col = jnp.sum(jnp.where(eye, rows[R_A:R_A + 1], 0.0), axis=1, keepdims=True)
        wts = jnp.exp(jnp.where(valid[dr], a_col - rows[R_MX:R_MX + 1], NEG_BIG))
        kc = k_ref[0, pl.ds(r0, L), hd * D:(hd + 1) * D]
        qpre = qt_ref[0, c, hd * D:(hd + 1) * D, :].astype(F32)
        qtc = (qpre * _sigmoid(qpre)).astype(BF16)
        vac = jnp.concatenate([vt_ref[0, c, hd * D:(hd + 1) * D, :], ones_row], axis=0)
        cmat = c_s[hd, dr]
        both = jnp.dot(jnp.concatenate([kc, cmat.astype(BF16)], axis=0), qtc,
                       preferred_element_type=F32)
        s_w = (both[0:L] * wts).astype(BF16)
        tot = (jnp.dot(vac, s_w, preferred_element_type=F32)
               + rows[R_ISC:R_ISC + 1] * both[L:L + DA])
        den = jnp.maximum(jnp.abs(tot[D:D + 1]), rows[R_EMT:R_EMT + 1])
        h = tot[0:D] * (1.0 / den)
        h_s[hd * D:(hd + 1) * D, pl.ds(r0, L)] = h_s[hd * D:(hd + 1) * D, pl.ds(r0, L)] + h
        u = (vac.astype(F32) * rows[R_W:R_W + 1]).astype(BF16)
        c_s[hd, dr] = rows[R_DEC:R_DEC + 1, 0:1] * cmat + jnp.dot(u, kc, preferred_element_type=F32)

    def step(i, carry):
        for hd, dr in chains:
            chain(hd, dr, chunk_of(dr, i))
        return carry

    lax.fori_loop(0, NC, step, 0)

    def fin(c, carry):
        out_ref[0, c] = h_s[:, pl.ds(pl.multiple_of(c * L, L), L)].astype(BF16)
        return carry

    lax.fori_loop(0, NC, fin, 0, unroll=4)


def _mlstm(mt, k_tm, gcm, heads_per_step=MLSTM_HEADS_PER_STEP):
    B, S, _ = k_tm.shape
    NC = S // CHUNK
    D = HEAD_DIM
    HB = heads_per_step
    v0 = (M_WIDTH + 3 * H_WIDTH) // (HB * D)
    W = HB * D
    NB = M_HEADS // HB
    DA = D + BF16_ROWS
    cm_blk = (1, NC, W, LANES)
    return pl.pallas_call(
        _mlstm_kernel,
        grid=(B, NB),
        in_specs=[
            pl.BlockSpec(cm_blk, lambda b, h: (b, 0, h, 0)),
            pl.BlockSpec(cm_blk, lambda b, h: (b, 0, v0 + h, 0)),
            pl.BlockSpec((1, S, W), lambda b, h: (b, 0, h)),
            pl.BlockSpec((1, HB, N_GATE // M_HEADS, S), lambda b, h: (b, h, 0, 0)),
        ],
        out_specs=pl.BlockSpec(cm_blk, lambda b, h: (b, 0, h, 0)),
        out_shape=jax.ShapeDtypeStruct((B, NC, M_WIDTH, LANES), BF16),
        scratch_shapes=[
            pltpu.VMEM((HB, 2, 8, S), F32),
            pltpu.VMEM((HB, 2, 8, S), F32),
            pltpu.VMEM((W, S), F32),
            pltpu.VMEM((HB, 2, DA, D), F32),
        ],
        compiler_params=_cparams("parallel", "parallel"),
        name="mlstm",
    )(mt, mt, k_tm, gcm)


def _filt_mlp_kernel(w1t_ref, b1_ref, w2_ref, b2_ref, w3_ref, b3_ref, fr_ref, w4f_ref, w4b_ref,
                     fbias_ref, out_ref, h3_s, *, cb_rows):
    S = h3_s.shape[2]
    hi = lax.Precision.HIGHEST
    tn_dims = (((0,), (0,)), ((), ()))
    lane = lax.broadcasted_iota(jnp.int32, (1, S), 1)
    pos_f = lane.astype(F32)
    pos_b = (S - lane).astype(F32)

    def features(pos):
        bands = (FILTER_EMB - 1) // 2
        t = pos / (S - 1)
        ang = (2.0 * math.pi) * pos / S
        fidx = lax.broadcasted_iota(jnp.int32, (bands, 1), 0).astype(F32)
        f = 1e-4 + fidx * ((bands - 1 - 1e-4) / (bands - 1))
        fa = f * ang
        w1t = w1t_ref[...]
        pre = (w1t[:, 0:1] * t
               + jnp.dot(w1t[:, 1:1 + bands], jnp.cos(fa), precision=hi, preferred_element_type=F32)
               - jnp.dot(w1t[:, 1 + bands:], jnp.sin(fa), precision=hi, preferred_element_type=F32))
        fr = fr_ref[...]
        h = jnp.sin(fr[:, 0:1] * (pre + b1_ref[...]))
        h = jnp.sin(fr[:, 1:2] * (lax.dot_general(w2_ref[...], h, tn_dims, precision=hi,
                                                  preferred_element_type=F32) + b2_ref[...]))
        return jnp.sin(fr[:, 2:3] * (lax.dot_general(w3_ref[...], h, tn_dims, precision=hi,
                                                     preferred_element_type=F32) + b3_ref[...]))

    @pl.when(pl.program_id(0) == 0)
    def _():
        h3 = features(pos_f)
        h3_s[0] = h3
        n_t = S // LANES
        anti = (lax.broadcasted_iota(jnp.int32, (LANES, LANES), 0)
                + lax.broadcasted_iota(jnp.int32, (LANES, LANES), 1) == LANES - 1).astype(F32)
        rev = jnp.concatenate(
            [jnp.dot(h3[:, (n_t - 1 - u) * LANES:(n_t - u) * LANES], anti, precision=hi,
                     preferred_element_type=F32) for u in range(n_t)], axis=1)
        h3_s[1] = pltpu.roll(rev, 1, 1)

    r = pl.program_id(0) * cb_rows + lax.broadcasted_iota(jnp.int32, (cb_rows, 1), 0)
    ch = (r % H_WIDTH).astype(F32)
    max_decay = math.log(DECAY_TARGET) / FAST_DECAY_PCT
    min_decay = math.log(DECAY_TARGET) / SLOW_DECAY_PCT
    delta = jnp.abs(min_decay + ch * ((max_decay - min_decay) / (H_WIDTH - 1)))
    n_t = S // LANES
    for half, (w_ref, pos) in enumerate(((w4f_ref, pos_f), (w4b_ref, pos_b))):
        filt = lax.dot_general(w_ref[...].astype(BF16), h3_s[half].astype(BF16), tn_dims,
                               preferred_element_type=F32)
        filt = filt * jnp.exp(-(pos / (S - 1)) * delta)
        if half == 0:
            filt = jnp.where(lane == 0, filt + fbias_ref[...], filt)
        else:
            filt = jnp.where(lane == 0, 0.0, filt)
        for u in range(n_t):
            out_ref[half * n_t + u] = filt[:, u * LANES:(u + 1) * LANES].astype(BF16)


def _filt_mlp(w1t, cols, fb_blk, fbias_blk, w2, w3, fr, w4, S, cb_rows=FILT_MLP_ROWS):
    R = w4.shape[1] // 2
    Hd = FILTER_HIDDEN
    const = lambda i: (0, 0)
    nblk = R // cb_rows
    return pl.pallas_call(
        functools.partial(_filt_mlp_kernel, cb_rows=cb_rows),
        grid=(nblk,),
        in_specs=[
            pl.BlockSpec((Hd, FILTER_EMB), const),
            pl.BlockSpec((Hd, 1), lambda i: (fb_blk, 0)),
            pl.BlockSpec((Hd, Hd), const),
            pl.BlockSpec((Hd, 1), lambda i: (fb_blk + 1, 0)),
            pl.BlockSpec((Hd, Hd), const),
            pl.BlockSpec((Hd, 1), lambda i: (fb_blk + 2, 0)),
            pl.BlockSpec((Hd, 3), const),
            pl.BlockSpec((Hd, cb_rows), lambda i: (0, i)),
            pl.BlockSpec((Hd, cb_rows), lambda i: (0, nblk + i)),
            pl.BlockSpec((cb_rows, 1), lambda i: (fbias_blk + i, 0)),
        ],
        out_specs=pl.BlockSpec((2 * S // LANES, cb_rows, LANES), lambda i: (0, i, 0)),
        out_shape=jax.ShapeDtypeStruct((2 * S // LANES, R, LANES), BF16),
        scratch_shapes=[pltpu.VMEM((2, Hd, S), F32)],
        compiler_params=_cparams("arbitrary"),
        name="filt_mlp",
    )(w1t, cols, w2, cols, w3, cols, fr, w4, w4, cols)


@functools.lru_cache(maxsize=None)
def _dft_constants(S):
    N = 2 * S
    N2 = LANES
    N1 = N // N2
    h = N1 // 2
    k1 = np.arange(N1)
    k2 = np.arange(N2)
    a1 = -2.0 * np.pi * np.outer(k1, k1) / N1
    f1r, f1i = np.cos(a1), np.sin(a1)
    at = -2.0 * np.pi * np.outer(k1, k2) / N
    twr, twi = np.cos(at), np.sin(at)
    a2 = -2.0 * np.pi * np.outer(k2, k2) / N2
    f2r, f2i = np.cos(a2), np.sin(a2)
    s1c = np.block([[f1r[:, :h], -f1i[:, :h]], [f1i[:, :h], f1r[:, :h]]])
    s1r = np.concatenate([f1r, f1i], axis=0)
    s2 = np.block([[f2r, f2i], [-f2i, f2r]])
    s2i = np.block([[f2r, -f2i], [f2i, f2r]])
    s1i = np.block([[f1r[:h, :], f1i[:h, :]], [-f1i[:h, :], f1r[:h, :]]])
    cast = lambda a: np.asarray(a, np.float32)
    return dict(s1c=cast(s1c), s1r=cast(s1r), s2=cast(s2), s2i=cast(s2i), s1i=cast(s1i),
                twr=cast(twr), twi=cast(twi), N1=N1, h=h)


def _bf(a):
    return jnp.asarray(a, F32).astype(BF16)


def _filt_fft_kernel(kern_ref, s1r_ref, s2_ref, twr_ref, twi_ref, out_ref, src_s, slab_s, *, unroll):
    N1 = twr_ref.shape[0]
    n_ch = kern_ref.shape[1]
    pitch = src_s.shape[0] // N1
    scale = 1.0 / (N1 * LANES)
    twr = twr_ref[...]
    twi = twi_ref[...]
    s1r = s1r_ref[...]

    def load_in(t1, carry):
        src_s[pl.ds(pl.multiple_of(t1 * pitch, 8), n_ch), :] = kern_ref[t1].astype(F32)
        return carry
    lax.fori_loop(0, N1, load_in, 0, unroll=4)

    def per_group(j, carry):
        for u in range(0, unroll, 2):
            cs = (j * unroll + u, j * unroll + u + 1)
            z = jnp.concatenate([src_s[pl.ds(c, N1, stride=pitch), :] for c in cs], axis=1)
            a = jnp.dot(s1r, z.astype(BF16), preferred_element_type=F32)
            for i, c in enumerate(cs):
                ar = a[0:N1, i * LANES:(i + 1) * LANES]
                ai = a[N1:2 * N1, i * LANES:(i + 1) * LANES]
                r0 = pl.multiple_of(c * N1, N1)
                slab_s[pl.ds(r0, N1), 0:LANES] = (ar * twr - ai * twi).astype(BF16)
                slab_s[pl.ds(r0, N1), LANES:2 * LANES] = (ar * twi + ai * twr).astype(BF16)
        return carry

    lax.fori_loop(0, n_ch // unroll, per_group, 0)
    x = jnp.dot(slab_s[...], s2_ref[...], preferred_element_type=F32)
    out_ref[...] = (x * scale).astype(BF16).reshape(n_ch, N1, 2 * LANES)


def _filt_fft(kern, consts, n_ch=FILT_FFT_ROWS, unroll=FILT_FFT_ROWS):
    N1, R, _ = kern.shape
    const = lambda i: (0, 0)
    pitch = n_ch + 8
    return pl.pallas_call(
        functools.partial(_filt_fft_kernel, unroll=unroll),
        grid=(R // n_ch,),
        in_specs=[
            pl.BlockSpec((N1, n_ch, LANES), lambda i: (0, i, 0)),
            pl.BlockSpec((2 * N1, N1), const),
            pl.BlockSpec((2 * LANES, 2 * LANES), const),
            pl.BlockSpec((N1, LANES), const),
            pl.BlockSpec((N1, LANES), const),
        ],
        out_specs=pl.BlockSpec((n_ch, N1, 2 * LANES), lambda i: (i, 0, 0)),
        out_shape=jax.ShapeDtypeStruct((R, N1, 2 * LANES), BF16),
        scratch_shapes=[pltpu.VMEM((N1 * pitch, LANES), F32),
                        pltpu.VMEM((n_ch * N1, 2 * LANES), BF16)],
        compiler_params=_cparams("parallel"),
        name="filt_fft",
    )(kern, _bf(consts["s1r"]), _bf(consts["s2"]),
      jnp.asarray(consts["twr"]), jnp.asarray(consts["twi"]))


def _hyena_kernel(nw_ref, v_ref, x1_ref, x2_ref, khat_ref, s1c_ref, s2_ref, s2i_ref, s1i_ref,
                  twr_ref, twi_ref, out_ref, src_s, slab_s, yf_s, q_s, y_s, *, unroll):
    h = v_ref.shape[1]
    n_ch = v_ref.shape[2]
    N1 = 2 * h
    twr = twr_ref[...]
    twi = twi_ref[...]
    s1c = s1c_ref[...]
    s1i = s1i_ref[...]
    n_groups = n_ch // unroll
    rows = unroll * N1
    pitch = src_s.shape[0] // (2 * h)

    def tile_rows(i):
        return pl.ds(pl.multiple_of(i * pitch, 8), n_ch)

    def chan_rows(c, bb):
        return pl.ds(bb * h * pitch + c, h, stride=pitch)

    def long_conv(order):
        def stage_a(g):
            for u in range(0, unroll, 2):
                cs = (g * unroll + u, g * unroll + u + 1)
                z = jnp.concatenate(
                    [jnp.concatenate([src_s[chan_rows(c, 0), :], src_s[chan_rows(c, 1), :]], axis=0)
                     for c in cs], axis=1)
                a = jnp.dot(s1c, z.astype(BF16), preferred_element_type=F32)
                for i, c in enumerate(cs):
                    ar = a[0:N1, i * LANES:(i + 1) * LANES]
                    ai = a[N1:2 * N1, i * LANES:(i + 1) * LANES]
                    r0 = c * N1 if isinstance(c, int) else pl.multiple_of(c * N1, N1)
                    slab_s[pl.ds(r0, N1), 0:LANES] = (ar * twr - ai * twi).astype(BF16)
                    slab_s[pl.ds(r0, N1), LANES:2 * LANES] = (ar * twi + ai * twr).astype(BF16)

        def stage_m1(g):
            r0 = g * rows if isinstance(g, int) else pl.multiple_of(g * rows, rows)
            c0 = g * unroll if isinstance(g, int) else pl.multiple_of(g * unroll, unroll)
            x = jnp.dot(slab_s[pl.ds(r0, rows), :], s2_ref[...], preferred_element_type=F32)
            xr = x[:, 0:LANES].reshape(unroll, N1, LANES)
            xi = x[:, LANES:2 * LANES].reshape(unroll, N1, LANES)
            kr = khat_ref[order, pl.ds(c0, unroll), :, 0:LANES].astype(F32)
            ki = khat_ref[order, pl.ds(c0, unroll), :, LANES:2 * LANES].astype(F32)
            yf_s[pl.ds(r0, rows), 0:LANES] = (xr * kr - xi * ki).reshape(rows, LANES).astype(BF16)
            yf_s[pl.ds(r0, rows), LANES:2 * LANES] = (xr * ki + xi * kr).reshape(rows, LANES).astype(BF16)

        def stage_m2(g):
            r0 = g * rows if isinstance(g, int) else pl.multiple_of(g * rows, rows)
            c0 = g * unroll if isinstance(g, int) else pl.multiple_of(g * unroll, unroll)
            p = jnp.dot(yf_s[pl.ds(r0, rows), :], s2i_ref[...], preferred_element_type=F32)
            pr = p[:, 0:LANES].reshape(unroll, N1, LANES)
            pi = p[:, LANES:2 * LANES].reshape(unroll, N1, LANES)
            q_s[pl.ds(c0, unroll), 0:N1, :] = (pr * twr + pi * twi).astype(BF16)
            q_s[pl.ds(c0, unroll), N1:2 * N1, :] = (pi * twr - pr * twi).astype(BF16)

        def stage_i(g):
            for u in range(0, unroll, 2):
                cs = (g * unroll + u, g * unroll + u + 1)
                q2 = jnp.concatenate([q_s[c] for c in cs], axis=1)
                y = jnp.dot(s1i, q2, preferred_element_type=F32)
                for i, c in enumerate(cs):
                    y_s[chan_rows(c, 0), :] = y[0:h, i * LANES:(i + 1) * LANES]
                    y_s[chan_rows(c, 1), :] = y[h:2 * h, i * LANES:(i + 1) * LANES]

        G = n_groups
        for t in range(G + 3):
            if 0 <= t - 3 < G:
                stage_i(t - 3)
            if 0 <= t - 2 < G:
                stage_m2(t - 2)
            if 0 <= t - 1 < G:
                stage_m1(t - 1)
            if t < G:
                stage_a(t)

    def per_tile(fn):
        for bb in range(2):
            def body(t1, carry, bb=bb):
                fn(bb, t1, tile_rows(bb * h + t1))
                return carry
            lax.fori_loop(0, h, body, 0, unroll=8)

    def load_in(bb, t1, rws):
        src_s[rws, :] = v_ref[bb, t1].astype(F32)
    per_tile(load_in)
    long_conv(0)

    def gate1(bb, t1, rws):
        src_s[rws, :] = y_s[rws, :] * x1_ref[bb, t1].astype(F32)
    per_tile(gate1)
    long_conv(1)

    nw = jnp.broadcast_to(nw_ref[...], (n_ch, LANES))

    def gate2_norm(bb, t1, rws):
        z = y_s[rws, :] * x2_ref[bb, t1].astype(F32)
        ms = jnp.mean(z * z, axis=0, keepdims=True)
        out_ref[bb, t1] = (z * lax.rsqrt(ms + EPS) * nw).astype(BF16)
    per_tile(gate2_norm)


def _hyena(cols, nw_blk, hy4, khat, consts, unroll=HYENA_GROUP):
    B, h, _, _ = hy4.shape
    C = H_WIDTH
    n_ch = GROUP_W
    G = C // n_ch
    g0 = M_WIDTH // n_ch
    N1 = consts["N1"]
    const = lambda g, p: (0, 0)
    blk = (2, h, n_ch, LANES)
    pitch = n_ch + 8
    return pl.pallas_call(
        functools.partial(_hyena_kernel, unroll=unroll),
        grid=(G, B // 2),
        in_specs=[
            pl.BlockSpec((n_ch, 1), lambda g, p: (nw_blk + g, 0)),
            pl.BlockSpec(blk, lambda g, p: (p, 0, g0 + g, 0)),
            pl.BlockSpec(blk, lambda g, p: (p, 0, g0 + G + g, 0)),
            pl.BlockSpec(blk, lambda g, p: (p, 0, g0 + 2 * G + g, 0)),
            pl.BlockSpec((2, n_ch, N1, 2 * LANES), lambda g, p: (0, g, 0, 0)),
            pl.BlockSpec((2 * N1, 2 * h), const),
            pl.BlockSpec((2 * LANES, 2 * LANES), const),
            pl.BlockSpec((2 * LANES, 2 * LANES), const),
            pl.BlockSpec((2 * h, 2 * N1), const),
            pl.BlockSpec((N1, LANES), const),
            pl.BlockSpec((N1, LANES), const),
        ],
        out_specs=pl.BlockSpec(blk, lambda g, p: (p, 0, g, 0)),
        out_shape=jax.ShapeDtypeStruct((B, h, C, LANES), BF16),
        scratch_shapes=[
            pltpu.VMEM((2 * h * pitch, LANES), F32),
            pltpu.VMEM((n_ch * N1, 2 * LANES), BF16),
            pltpu.VMEM((n_ch * N1, 2 * LANES), BF16),
            pltpu.VMEM((n_ch, 2 * N1, LANES), BF16),
            pltpu.VMEM((2 * h * pitch, LANES), F32),
        ],
        compiler_params=_cparams("parallel", "arbitrary"),
        name="hyena",
    )(cols, hy4, hy4, hy4, khat,
      _bf(consts["s1c"]), _bf(consts["s2"]), _bf(consts["s2i"]), _bf(consts["s1i"]),
      jnp.asarray(consts["twr"]), jnp.asarray(consts["twi"]))


def _outmlp_kernel(x_ref, hm_ref, og_ref, yh_ref, nwm_ref, wo_ref, w1_ref, w2_ref, n_post_ref,
                   n_pre_ref, n_post2_ref, out_ref):
    n_t = hm_ref.shape[1]
    nwm = jnp.broadcast_to(nwm_ref[...], (M_WIDTH, LANES))
    tiles = []
    for j in range(n_t):
        hg = hm_ref[0, j].astype(F32) * _sigmoid(og_ref[0, j].astype(F32))
        heads = []
        for hd in range(M_HEADS):
            hh = hg[hd * HEAD_DIM:(hd + 1) * HEAD_DIM]
            ms = jnp.mean(hh * hh, axis=0, keepdims=True)
            heads.append(hh * lax.rsqrt(ms + EPS))
        ym = jnp.concatenate(heads, axis=0) * nwm
        yt = jnp.concatenate([ym, yh_ref[0, j].astype(F32)], axis=0)
        tiles.append(yt.T.astype(BF16))
    y = tiles[0] if n_t == 1 else jnp.concatenate(tiles, axis=0)
    mix = jnp.dot(y, wo_ref[...], preferred_element_type=F32)
    x1 = x_ref[0] + _rms_rows(mix, n_post_ref[...])
    hm = _rms_rows(x1, n_pre_ref[...]).astype(BF16)
    mid = jnp.maximum(jnp.dot(hm, w1_ref[...], preferred_element_type=F32), 0.0)
    mid = (mid * mid).astype(BF16)
    ff = jnp.dot(mid, w2_ref[...], preferred_element_type=F32)
    out_ref[0] = x1 + _rms_rows(ff, n_post2_ref[...])


def _outmlp(x, hm, mt, yh, cols, nwm_blk, wo, w1, w2, n_post, n_pre, n_post2, tm_rows):
    B, S, D = x.shape
    n_t = tm_rows // LANES
    og0 = mt.shape[2] // M_WIDTH - 1
    const = lambda b, i: (0, 0)
    resident = functools.partial(pl.BlockSpec, index_map=const, pipeline_mode=pl.Buffered(1))
    return pl.pallas_call(
        _outmlp_kernel,
        grid=(B, S // tm_rows),
        in_specs=[
            pl.BlockSpec((1, tm_rows, D), lambda b, i: (b, i, 0)),
            pl.BlockSpec((1, n_t, M_WIDTH, LANES), lambda b, i: (b, i, 0, 0)),
            pl.BlockSpec((1, n_t, M_WIDTH, LANES), lambda b, i: (b, i, og0, 0)),
            pl.BlockSpec((1, n_t, H_WIDTH, LANES), lambda b, i: (b, i, 0, 0)),
            pl.BlockSpec((M_WIDTH, 1), lambda b, i: (nwm_blk, 0)),
            resident((D, D)),
            resident((D, D_FF)),
            resident((D_FF, D)),
            pl.BlockSpec((1, D), const),
            pl.BlockSpec((1, D), const),
            pl.BlockSpec((1, D), const),
        ],
        out_specs=pl.BlockSpec((1, tm_rows, D), lambda b, i: (b, i, 0)),
        out_shape=jax.ShapeDtypeStruct((B, S, D), F32),
        compiler_params=_cparams("parallel", "parallel"),
        name="outmlp",
    )(x, hm, mt, yh, cols, wo, w1, w2, n_post, n_pre, n_post2)


def kernel(x, norm_mix_pre, norm_mix_post, norm_mlp_pre, norm_mlp_post, w_in, b_gates,
           conv_w, conv_b, mlstm_norm_w, hyena_norm_w, filt_w1, filt_b1, filt_w2, filt_b2,
           filt_w3, filt_b3, filt_w4, filt_freq, filt_bias, w_out, w_mlp_in, w_mlp_out):
    B, S, D = x.shape
    assert D == D_MODEL and B % 2 == 0 and S % CHUNK == 0
    row = lambda a: a.astype(F32).reshape(1, -1)
    tm_rows = min(PROJ_ROWS, S)
    cols = jnp.concatenate([filt_bias.reshape(-1), mlstm_norm_w, hyena_norm_w, filt_b1, filt_b2,
                            filt_b3, b_gates]).astype(F32).reshape(-1, 1)
    off_mn = HYENA_ORDER * H_WIDTH
    off_hn = off_mn + M_WIDTH
    off_fb = off_hn + H_WIDTH
    off_bg = off_fb + 3 * FILTER_HIDDEN
    assert (off_mn % M_WIDTH == 0 and off_hn % GROUP_W == 0 and off_fb % FILTER_HIDDEN == 0
            and off_bg % N_GATE == 0 and off_mn % FILT_MLP_ROWS == 0)

    n_conv = 2 * M_WIDTH + 3 * H_WIDTH
    o_gate = n_conv + 2 * M_WIDTH
    assert w_in.shape == (D, o_gate + N_GATE)
    assert conv_w.shape == (3, n_conv)
    mt, k_tm, gcm = _proj(x, row(norm_mix_pre), w_in.astype(F32).T, cols, off_bg // N_GATE,
                          conv_w.astype(F32), row(conv_b), tm_rows)

    h_m = _mlstm(mt, k_tm, gcm)

    consts = _dft_constants(S)
    kern = _filt_mlp(filt_w1.astype(F32).T, cols, off_fb // FILTER_HIDDEN, 0, filt_w2.astype(F32),
                     filt_w3.astype(F32), filt_freq.astype(F32).T, filt_w4.astype(F32), S)
    khat = _filt_fft(kern, consts)
    khat = khat.reshape(2, H_WIDTH, consts["N1"], 2 * LANES)

    y_h = _hyena(cols, off_hn // GROUP_W, mt, khat, consts)

    return _outmlp(x, h_m, mt, y_h, cols, off_mn // M_WIDTH,
                   w_out.astype(BF16), w_mlp_in.astype(BF16), w_mlp_out.astype(BF16),
                   row(norm_mix_post), row(norm_mlp_pre), row(norm_mlp_post), min(OUTMLP_ROWS, S))
```

```python
import functools
import math

import numpy as np
import jax
import jax.numpy as jnp
from jax import lax
from jax.experimental import pallas as pl
from jax.experimental.pallas import tpu as pltpu

F32 = jnp.float32
BF16 = jnp.bfloat16

D_MODEL = 1024
M_WIDTH = 512
M_HEADS = 4
HEAD_DIM = 128
H_WIDTH = 512
H_GROUPS = 8
HYENA_ORDER = 2
GROUP_W = H_WIDTH // H_GROUPS
CHUNK = 128
FILTER_EMB = 33
FILTER_HIDDEN = 64
DECAY_TARGET = 1e-2
FAST_DECAY_PCT = 0.3
SLOW_DECAY_PCT = 1.5
D_FF = 4 * D_MODEL
N_GATE = 16
EPS = 1e-6
LANES = 128
BF16_ROWS = 16
NEG_BIG = -1e30
VMEM_LIMIT = 56 * 1024 * 1024

PROJ_ROWS = 1024
PROJ_COL_BLK = 256
MLSTM_HEADS_PER_STEP = 4
FILT_MLP_ROWS = 256
FILT_FFT_ROWS = 32
HYENA_GROUP = 8
OUTMLP_ROWS = 512


def _cparams(*sem):
    return pltpu.CompilerParams(dimension_semantics=sem, vmem_limit_bytes=VMEM_LIMIT)


def _rms_rows(xf, w):
    ms = jnp.mean(xf * xf, axis=-1, keepdims=True)
    return xf * lax.rsqrt(ms + EPS) * w


def _sigmoid(x):
    return 1.0 / (1.0 + jnp.exp(-x))


def _log_sigmoid(x):
    return jnp.minimum(x, 0.0) - jnp.log(1.0 + jnp.exp(-jnp.abs(x)))


def _proj_kernel(x_ref, xp_ref, xn_ref, nw_ref, wt_ref, bgt_ref, cw_ref, cb_ref,
                 cm_ref, k_ref, gcm_ref, wbf_s, hall_s, pc_s, *, col_blk):
    TM = x_ref.shape[1]
    HALO = xp_ref.shape[1]
    j = pl.program_id(1)

    @pl.when(jnp.logical_and(pl.program_id(0) == 0, j == 0))
    def _():
        for c0 in range(0, wbf_s.shape[1], col_blk):
            wbf_s[:, c0:c0 + col_blk] = wt_ref[c0:c0 + col_blk, :].T.astype(BF16)

    nw = nw_ref[...]
    hn = _rms_rows(x_ref[0], nw).astype(BF16)
    hall_s[0:HALO, :] = _rms_rows(xp_ref[0], nw).astype(BF16)
    hall_s[HALO:HALO + TM, :] = hn
    hall_s[HALO + TM:2 * HALO + TM, :] = _rms_rows(xn_ref[0], nw).astype(BF16)
    keep_p = jnp.where(j > 0, 1.0, 0.0)
    keep_n = jnp.where(j < pl.num_programs(1) - 1, 1.0, 0.0)
    n_cm = cm_ref.shape[2]
    n_conv = cw_ref.shape[1]
    n_blk = n_cm // col_blk
    w_k0 = M_WIDTH
    w_gate = n_cm + M_WIDTH
    n_typ = N_GATE // M_HEADS
    order = [t * M_HEADS + hd for hd in range(M_HEADS) for t in range(n_typ)]
    wg = jnp.concatenate([wt_ref[w_gate + r:w_gate + r + 1, :] for r in order], axis=0)
    bg = jnp.concatenate([bgt_ref[r:r + 1, :] for r in order], axis=0)
    nt_dims = (((1,), (1,)), ((), ()))
    gt = lax.dot_general(wg.astype(BF16), hn, nt_dims, preferred_element_type=F32) + bg
    row = lax.broadcasted_iota(jnp.int32, gt.shape, 0)
    lg = jnp.where((row % 2) == 1, _log_sigmoid(gt), gt)
    for hd in range(M_HEADS):
        gcm_ref[0, hd] = lg[hd * n_typ:(hd + 1) * n_typ]

    n_rb = TM // LANES
    n_sl = col_blk // LANES

    def project(w_row0, slot):
        wblk = wbf_s[:, pl.ds(w_row0, col_blk)]
        res = jnp.dot(hall_s[...], wblk, preferred_element_type=F32)
        for u in range(n_sl):
            ls = slice(u * LANES, (u + 1) * LANES)
            pc_s[slot, u, 0:HALO, :] = res[0:HALO, ls] * keep_p
            pc_s[slot, u, HALO:HALO + TM, :] = res[HALO:HALO + TM, ls]
            pc_s[slot, u, HALO + TM:2 * HALO + TM, :] = res[HALO + TM:2 * HALO + TM, ls] * keep_n

    def conv_tile(slot, u, rb, taps, bias):
        r = HALO + rb * LANES
        return (pc_s[slot, u, pl.ds(r - 1, LANES, stride=1), :] * taps[0:1]
                + pc_s[slot, u, r:r + LANES, :] * taps[1:2]
                + pc_s[slot, u, pl.ds(r + 1, LANES, stride=1), :] * taps[2:3] + bias)

    n_q = M_WIDTH // col_blk
    assert n_blk % 2 == 0

    def w_row_of(c):
        if c == n_blk:
            return w_k0
        return (c if c < n_q else c + n_q) * col_blk

    def emit(c, slot):
        for u in range(n_sl):
            cs = slice(c * col_blk + u * LANES, c * col_blk + (u + 1) * LANES)
            w0 = w_row_of(c) + u * LANES
            for rb in range(n_rb):
                if w0 < n_conv:
                    cv = conv_tile(slot, u, rb, cw_ref[:, w0:w0 + LANES], cb_ref[:, w0:w0 + LANES])
                else:
                    cv = pc_s[slot, u, HALO + rb * LANES:HALO + (rb + 1) * LANES, :]
                cm_ref[0, rb, cs, :] = cv.T.astype(BF16)

    project(0, 0)
    for c in range(n_blk):
        emit(c, c % 2)
        project(w_row_of(c + 1), (c + 1) % 2)

    k_scale = HEAD_DIM ** -0.5
    n_k = M_WIDTH // col_blk
    for kb in range(n_k):
        if kb > 0:
            project(w_k0 + kb * col_blk, kb % 2)
        for u in range(n_sl):
            cs = slice(kb * col_blk + u * LANES, kb * col_blk + (u + 1) * LANES)
            for rb in range(n_rb):
                ks = slice(w_k0 + cs.start, w_k0 + cs.stop)
                cv = conv_tile(kb % 2, u, rb, cw_ref[:, ks], cb_ref[:, ks])
                k_ref[0, rb * LANES:(rb + 1) * LANES, cs] = (cv * _sigmoid(cv) * k_scale).astype(BF16)


def _proj(x, nw, w_in_t, cols, bg_blk, cw, cb, tm_rows, col_blk=PROJ_COL_BLK):
    B, S, D = x.shape
    HALO = BF16_ROWS
    n_conv = cw.shape[1]
    n_cm = w_in_t.shape[0] - N_GATE - M_WIDTH
    n_t = tm_rows // LANES
    hb = tm_rows // HALO
    last = S // HALO - 1
    const = lambda b, j: (0, 0)
    resident = functools.partial(pl.BlockSpec, index_map=const, pipeline_mode=pl.Buffered(1))
    return pl.pallas_call(
        functools.partial(_proj_kernel, col_blk=col_blk),
        grid=(B, S // tm_rows),
        in_specs=[
            pl.BlockSpec((1, tm_rows, D), lambda b, j: (b, j, 0)),
            pl.BlockSpec((1, HALO, D), lambda b, j: (b, jnp.maximum(j * hb - 1, 0), 0)),
            pl.BlockSpec((1, HALO, D), lambda b, j: (b, jnp.minimum((j + 1) * hb, last), 0)),
            pl.BlockSpec((1, D), const),
            resident(w_in_t.shape),
            pl.BlockSpec((N_GATE, 1), lambda b, j: (bg_blk, 0)),
            pl.BlockSpec((3, n_conv), const),
            pl.BlockSpec((1, n_conv), const),
        ],
        out_specs=[
            pl.BlockSpec((1, n_t, n_cm, LANES), lambda b, j: (b, j, 0, 0)),
            pl.BlockSpec((1, tm_rows, M_WIDTH), lambda b, j: (b, j, 0)),
            pl.BlockSpec((1, M_HEADS, N_GATE // M_HEADS, tm_rows), lambda b, j: (b, 0, 0, j)),
        ],
        out_shape=[
            jax.ShapeDtypeStruct((B, S // LANES, n_cm, LANES), BF16),
            jax.ShapeDtypeStruct((B, S, M_WIDTH), BF16),
            jax.ShapeDtypeStruct((B, M_HEADS, N_GATE // M_HEADS, S), F32),
        ],
        scratch_shapes=[pltpu.VMEM((D, n_cm + M_WIDTH), BF16),
                        pltpu.VMEM((tm_rows + 2 * HALO, D), BF16),
                        pltpu.VMEM((2, col_blk // LANES, tm_rows + 2 * HALO, LANES), F32)],
        compiler_params=_cparams("arbitrary", "arbitrary"),
        name="proj",
    )(x, x, x, nw, w_in_t, cols, cw, cb)


def _mlstm_kernel(qt_ref, vt_ref, k_ref, g_ref, out_ref, rows_s, stab_s, h_s, c_s):
    S = k_ref.shape[1]
    HB = g_ref.shape[1]
    L = CHUNK
    NC = S // L
    D = HEAD_DIM
    DA = D + BF16_ROWS
    W = HB * D

    R_A, R_MX, R_W, R_DEC, R_ISC, R_EMT, R_GT, R_AMAX = range(8)

    def scan(x, op, fill):
        lane = lax.broadcasted_iota(jnp.int32, x.shape, 1) % L
        pre = x
        suf = x
        d = 1
        while d < L:
            pre = op(pre, jnp.where(lane >= d, pltpu.roll(pre, d, 1), fill))
            suf = op(suf, jnp.where(lane < L - d, pltpu.roll(suf, S - d, 1), fill))
            d *= 2
        return pre, suf

    chains = [(hd, dr) for hd in range(HB) for dr in range(2)]
    assert HB % 2 == 0 and len(chains) <= 8
    a_rows = {}
    for h0 in range(0, HB, 2):
        g = jnp.concatenate([g_ref[0, h0, 0:4, :], g_ref[0, h0 + 1, 0:4, :]], axis=0)
        pre, suf = scan(g, jnp.add, 0.0)
        tot = pre + suf - g
        for j, hd in enumerate((h0, h0 + 1)):
            o = 4 * j
            b_rows = (pre[o + 1:o + 2], suf[o + 3:o + 4])
            for dr in range(2):
                a_rows[hd, dr] = g[o + 2 * dr:o + 2 * dr + 1] - b_rows[dr]
                rows_s[hd, dr, R_A:R_A + 1, :] = a_rows[hd, dr]
                rows_s[hd, dr, R_EMT:R_EMT + 1, :] = b_rows[dr]
                rows_s[hd, dr, R_GT:R_GT + 1, :] = tot[o + 2 * dr + 1:o + 2 * dr + 2]
    a8 = jnp.concatenate([a_rows[ch] for ch in chains]
                         + [jnp.zeros((8 - len(chains), S), F32)] * (len(chains) < 8), axis=0)
    pmax, smax = scan(a8, jnp.maximum, NEG_BIG)
    for i, (hd, dr) in enumerate(chains):
        rows_s[hd, dr, R_MX:R_MX + 1, :] = pmax[i:i + 1] if dr == 0 else smax[i:i + 1]
        rows_s[hd, dr, R_AMAX:R_AMAX + 1, :] = jnp.maximum(pmax[i:i + 1], smax[i:i + 1])

    def chunk_of(dr, i):
        return i if dr == 0 else NC - 1 - i

    def stabilisers(i, ms):
        out = []
        for (hd, dr), m in zip(chains, ms):
            r0 = pl.multiple_of(chunk_of(dr, i) * L, L)
            gt = rows_s[hd, dr, R_GT:R_GT + 1, pl.ds(r0, L)]
            m_end = jnp.maximum(gt + m, gt + rows_s[hd, dr, R_AMAX:R_AMAX + 1, pl.ds(r0, L)])
            stab_s[hd, dr, 0:1, pl.ds(r0, L)] = m
            stab_s[hd, dr, 1:2, pl.ds(r0, L)] = m_end
            out.append(m_end)
        return tuple(out)

    lax.fori_loop(0, NC, stabilisers, tuple(jnp.zeros((1, L), F32) for _ in chains), unroll=4)

    for hd, dr in chains:
        a = rows_s[hd, dr, R_A:R_A + 1, :]
        gt = rows_s[hd, dr, R_GT:R_GT + 1, :]
        m = stab_s[hd, dr, 0:1, :]
        m_end = stab_s[hd, dr, 1:2, :]
        mx = jnp.maximum(m, rows_s[hd, dr, R_MX:R_MX + 1, :])
        b = rows_s[hd, dr, R_EMT:R_EMT + 1, :]
        rows_s[hd, dr, R_MX:R_MX + 1, :] = mx
        rows_s[hd, dr, R_W:R_W + 1, :] = jnp.exp(gt + a - m_end)
        rows_s[hd, dr, R_DEC:R_DEC + 1, :] = jnp.exp(gt + m - m_end)
        rows_s[hd, dr, R_ISC:R_ISC + 1, :] = jnp.exp(m - mx)
        rows_s[hd, dr, R_EMT:R_EMT + 1, :] = jnp.exp(-(b + mx))

    ones_row = (lax.broadcasted_iota(jnp.int32, (BF16_ROWS, L), 0) == 0).astype(BF16)

    c_s[...] = jnp.zeros_like(c_s)

    si = lax.broadcasted_iota(jnp.int32, (L, L), 0)
    ti = lax.broadcasted_iota(jnp.int32, (L, L), 1)
    eye = si == ti
    valid = (si <= ti, si >= ti)

    def chain(hd, dr, c, second):
        r0 = pl.multiple_of(c * L, L)
        rows = rows_s[hd, dr, :, pl.ds(r0, L)]
        a_col = jnp.sum(jnp.where(eye, rows[R_A:R_A + 1], 0.0), axis=1, keepdims=True)
        wts = jnp.exp(jnp.where(valid[dr], a_col - rows[R_MX:R_MX + 1], NEG_BIG))
        kc = k_ref[0, pl.ds(r0, L), hd * D:(hd + 1) * D]
        qpre = qt_ref[0, c, hd * D:(hd + 1) * D, :].astype(F32)
        qtc = (qpre * _sigmoid(qpre)).astype(BF16)
        vac = jnp.concatenate([vt_ref[0, c, hd * D:(hd + 1) * D, :], ones_row], axis=0)
        cmat = c_s[hd, dr]
        both = jnp.dot(jnp.concatenate([kc, cmat.astype(BF16)], axis=0), qtc,
                       preferred_element_type=F32)
        s_w = (both[0:L] * wts).astype(BF16)
        tot = (jnp.dot(vac, s_w, preferred_element_type=F32)
               + rows[R_ISC:R_ISC + 1] * both[L:L + DA])
        den = jnp.maximum(jnp.abs(tot[D:D + 1]), rows[R_EMT:R_EMT + 1])
        h = tot[0:D] * (1.0 / den)
        hs = slice(hd * D, (hd + 1) * D)
        if second:
            out_ref[0, c, hs, :] = (h_s[hs, pl.ds(r0, L)] + h).astype(BF16)
        else:
            h_s[hs, pl.ds(r0, L)] = h
        u = (vac.astype(F32) * rows[R_W:R_W + 1]).astype(BF16)
        c_s[hd, dr] = rows[R_DEC:R_DEC + 1, 0:1] * cmat + jnp.dot(u, kc, preferred_element_type=F32)

    def step(i, carry, second):
        for hd, dr in chains:
            chain(hd, dr, chunk_of(dr, i), second)
        return carry

    assert NC % 2 == 0
    lax.fori_loop(0, NC // 2, functools.partial(step, second=False), 0)
    lax.fori_loop(NC // 2, NC, functools.partial(step, second=True), 0)


def _mlstm(mt, k_tm, gcm, heads_per_step=MLSTM_HEADS_PER_STEP):
    B, S, _ = k_tm.shape
    NC = S // CHUNK
    D = HEAD_DIM
    HB = heads_per_step
    v0 = (M_WIDTH + 3 * H_WIDTH) // (HB * D)
    W = HB * D
    NB = M_HEADS // HB
    DA = D + BF16_ROWS
    cm_blk = (1, NC, W, LANES)
    return pl.pallas_call(
        _mlstm_kernel,
        grid=(B, NB),
        in_specs=[
            pl.BlockSpec(cm_blk, lambda b, h: (b, 0, h, 0)),
            pl.BlockSpec(cm_blk, lambda b, h: (b, 0, v0 + h, 0)),
            pl.BlockSpec((1, S, W), lambda b, h: (b, 0, h)),
            pl.BlockSpec((1, HB, N_GATE // M_HEADS, S), lambda b, h: (b, h, 0, 0)),
        ],
        out_specs=pl.BlockSpec(cm_blk, lambda b, h: (b, 0, h, 0)),
        out_shape=jax.ShapeDtypeStruct((B, NC, M_WIDTH, LANES), BF16),
        scratch_shapes=[
            pltpu.VMEM((HB, 2, 8, S), F32),
            pltpu.VMEM((HB, 2, 8, S), F32),
            pltpu.VMEM((W, S), F32),
            pltpu.VMEM((HB, 2, DA, D), F32),
        ],
        compiler_params=_cparams("parallel", "parallel"),
        name="mlstm",
    )(mt, mt, k_tm, gcm)


def _filt_mlp_kernel(w1t_ref, b1_ref, w2_ref, b2_ref, w3_ref, b3_ref, fr_ref, w4f_ref, w4b_ref,
                     fbias_ref, out_ref, h3_s, *, cb_rows):
    S = h3_s.shape[2]
    hi = lax.Precision.HIGHEST
    tn_dims = (((0,), (0,)), ((), ()))
    lane = lax.broadcasted_iota(jnp.int32, (1, S), 1)
    pos_f = lane.astype(F32)
    pos_b = (S - lane).astype(F32)

    def features(pos):
        bands = (FILTER_EMB - 1) // 2
        t = pos / (S - 1)
        ang = (2.0 * math.pi) * pos / S
        fidx = lax.broadcasted_iota(jnp.int32, (bands, 1), 0).astype(F32)
        f = 1e-4 + fidx * ((bands - 1 - 1e-4) / (bands - 1))
        fa = f * ang
        w1t = w1t_ref[...]
        pre = (w1t[:, 0:1] * t
               + jnp.dot(w1t[:, 1:1 + bands], jnp.cos(fa), precision=hi, preferred_element_type=F32)
               - jnp.dot(w1t[:, 1 + bands:], jnp.sin(fa), precision=hi, preferred_element_type=F32))
        fr = fr_ref[...]
        h = jnp.sin(fr[:, 0:1] * (pre + b1_ref[...]))
        h = jnp.sin(fr[:, 1:2] * (lax.dot_general(w2_ref[...], h, tn_dims, precision=hi,
                                                  preferred_element_type=F32) + b2_ref[...]))
        return jnp.sin(fr[:, 2:3] * (lax.dot_general(w3_ref[...], h, tn_dims, precision=hi,
                                                     preferred_element_type=F32) + b3_ref[...]))

    @pl.when(pl.program_id(0) == 0)
    def _():
        h3 = features(pos_f)
        h3_s[0] = h3
        n_t = S // LANES
        anti = (lax.broadcasted_iota(jnp.int32, (LANES, LANES), 0)
                + lax.broadcasted_iota(jnp.int32, (LANES, LANES), 1) == LANES - 1).astype(F32)
        rev = jnp.concatenate(
            [jnp.dot(h3[:, (n_t - 1 - u) * LANES:(n_t - u) * LANES], anti, precision=hi,
                     preferred_element_type=F32) for u in range(n_t)], axis=1)
        h3_s[1] = pltpu.roll(rev, 1, 1)

    r = pl.program_id(0) * cb_rows + lax.broadcasted_iota(jnp.int32, (cb_rows, 1), 0)
    ch = (r % H_WIDTH).astype(F32)
    max_decay = math.log(DECAY_TARGET) / FAST_DECAY_PCT
    min_decay = math.log(DECAY_TARGET) / SLOW_DECAY_PCT
    delta = jnp.abs(min_decay + ch * ((max_decay - min_decay) / (H_WIDTH - 1)))
    n_t = S // LANES
    for half, (w_ref, pos) in enumerate(((w4f_ref, pos_f), (w4b_ref, pos_b))):
        filt = lax.dot_general(w_ref[...].astype(BF16), h3_s[half].astype(BF16), tn_dims,
                               preferred_element_type=F32)
        filt = filt * jnp.exp(-(pos / (S - 1)) * delta)
        if half == 0:
            filt = jnp.where(lane == 0, filt + fbias_ref[...], filt)
        else:
            filt = jnp.where(lane == 0, 0.0, filt)
        for u in range(n_t):
            out_ref[half * n_t + u] = filt[:, u * LANES:(u + 1) * LANES].astype(BF16)


def _filt_mlp(w1t, cols, fb_blk, fbias_blk, w2, w3, fr, w4, S, cb_rows=FILT_MLP_ROWS):
    R = w4.shape[1] // 2
    Hd = FILTER_HIDDEN
    const = lambda i: (0, 0)
    nblk = R // cb_rows
    return pl.pallas_call(
        functools.partial(_filt_mlp_kernel, cb_rows=cb_rows),
        grid=(nblk,),
        in_specs=[
            pl.BlockSpec((Hd, FILTER_EMB), const),
            pl.BlockSpec((Hd, 1), lambda i: (fb_blk, 0)),
            pl.BlockSpec((Hd, Hd), const),
            pl.BlockSpec((Hd, 1), lambda i: (fb_blk + 1, 0)),
            pl.BlockSpec((Hd, Hd), const),
            pl.BlockSpec((Hd, 1), lambda i: (fb_blk + 2, 0)),
            pl.BlockSpec((Hd, 3), const),
            pl.BlockSpec((Hd, cb_rows), lambda i: (0, i)),
            pl.BlockSpec((Hd, cb_rows), lambda i: (0, nblk + i)),
            pl.BlockSpec((cb_rows, 1), lambda i: (fbias_blk + i, 0)),
        ],
        out_specs=pl.BlockSpec((2 * S // LANES, cb_rows, LANES), lambda i: (0, i, 0)),
        out_shape=jax.ShapeDtypeStruct((2 * S // LANES, R, LANES), BF16),
        scratch_shapes=[pltpu.VMEM((2, Hd, S), F32)],
        compiler_params=_cparams("arbitrary"),
        name="filt_mlp",
    )(w1t, cols, w2, cols, w3, cols, fr, w4, w4, cols)


@functools.lru_cache(maxsize=None)
def _dft_constants(S):
    N = 2 * S
    N2 = LANES
    N1 = N // N2
    h = N1 // 2
    k1 = np.arange(N1)
    k2 = np.arange(N2)
    a1 = -2.0 * np.pi * np.outer(k1, k1) / N1
    f1r, f1i = np.cos(a1), np.sin(a1)
    at = -2.0 * np.pi * np.outer(k1, k2) / N
    twr, twi = np.cos(at), np.sin(at)
    a2 = -2.0 * np.pi * np.outer(k2, k2) / N2
    f2r, f2i = np.cos(a2), np.sin(a2)
    s1c = np.block([[f1r[:, :h], -f1i[:, :h]], [f1i[:, :h], f1r[:, :h]]])
    s1r = np.concatenate([f1r, f1i], axis=0)
    s2 = np.block([[f2r, f2i], [-f2i, f2r]])
    s2i = np.block([[f2r, -f2i], [f2i, f2r]])
    s1i = np.block([[f1r[:h, :], f1i[:h, :]], [-f1i[:h, :], f1r[:h, :]]])
    cast = lambda a: np.asarray(a, np.float32)
    return dict(s1c=cast(s1c), s1r=cast(s1r), s2=cast(s2), s2i=cast(s2i), s1i=cast(s1i),
                twr=cast(twr), twi=cast(twi), N1=N1, h=h)


def _bf(a):
    return jnp.asarray(a, F32).astype(BF16)


def _filt_fft_kernel(kern_ref, s1r_ref, s2_ref, twr_ref, twi_ref, out_ref, src_s, slab_s, *, unroll):
    N1 = twr_ref.shape[0]
    n_ch = kern_ref.shape[1]
    pitch = src_s.shape[0] // N1
    scale = 1.0 / (N1 * LANES)
    twr = twr_ref[...]
    twi = twi_ref[...]
    s1r = s1r_ref[...]

    def load_in(t1, carry):
        src_s[pl.ds(pl.multiple_of(t1 * pitch, 8), n_ch), :] = kern_ref[t1].astype(F32)
        return carry
    lax.fori_loop(0, N1, load_in, 0, unroll=4)

    def per_group(j, carry):
        for u in range(0, unroll, 2):
            cs = (j * unroll + u, j * unroll + u + 1)
            z = jnp.concatenate([src_s[pl.ds(c, N1, stride=pitch), :] for c in cs], axis=1)
            a = jnp.dot(s1r, z.astype(BF16), preferred_element_type=F32)
            for i, c in enumerate(cs):
                ar = a[0:N1, i * LANES:(i + 1) * LANES]
                ai = a[N1:2 * N1, i * LANES:(i + 1) * LANES]
                r0 = pl.multiple_of(c * N1, N1)
                slab_s[pl.ds(r0, N1), 0:LANES] = (ar * twr - ai * twi).astype(BF16)
                slab_s[pl.ds(r0, N1), LANES:2 * LANES] = (ar * twi + ai * twr).astype(BF16)
        return carry

    lax.fori_loop(0, n_ch // unroll, per_group, 0)
    x = jnp.dot(slab_s[...], s2_ref[...], preferred_element_type=F32)
    out_ref[...] = (x * scale).astype(BF16).reshape(n_ch, N1, 2 * LANES)


def _filt_fft(kern, consts, n_ch=FILT_FFT_ROWS, unroll=FILT_FFT_ROWS):
    N1, R, _ = kern.shape
    const = lambda i: (0, 0)
    pitch = n_ch + 8
    return pl.pallas_call(
        functools.partial(_filt_fft_kernel, unroll=unroll),
        grid=(R // n_ch,),
        in_specs=[
            pl.BlockSpec((N1, n_ch, LANES), lambda i: (0, i, 0)),
            pl.BlockSpec((2 * N1, N1), const),
            pl.BlockSpec((2 * LANES, 2 * LANES), const),
            pl.BlockSpec((N1, LANES), const),
            pl.BlockSpec((N1, LANES), const),
        ],
        out_specs=pl.BlockSpec((n_ch, N1, 2 * LANES), lambda i: (i, 0, 0)),
        out_shape=jax.ShapeDtypeStruct((R, N1, 2 * LANES), BF16),
        scratch_shapes=[pltpu.VMEM((N1 * pitch, LANES), F32),
                        pltpu.VMEM((n_ch * N1, 2 * LANES), BF16)],
        compiler_params=_cparams("parallel"),
        name="filt_fft",
    )(kern, _bf(consts["s1r"]), _bf(consts["s2"]),
      jnp.asarray(consts["twr"]), jnp.asarray(consts["twi"]))


def _hyena_kernel(nw_ref, v_ref, x1_ref, x2_ref, khat_ref, s1c_ref, s2_ref, s2i_ref, s1i_ref,
                  twr_ref, twi_ref, out_ref, src_s, slab_s, yf_s, q_s, y_s, *, unroll):
    h = v_ref.shape[1]
    n_ch = v_ref.shape[2]
    N1 = 2 * h
    twr = twr_ref[...]
    twi = twi_ref[...]
    s1c = s1c_ref[...]
    s1i = s1i_ref[...]
    n_groups = n_ch // unroll
    rows = unroll * N1
    pitch = src_s.shape[0] // (2 * h)

    def tile_rows(i):
        return pl.ds(pl.multiple_of(i * pitch, 8), n_ch)

    def chan_rows(c, bb):
        return pl.ds(bb * h * pitch + c, h, stride=pitch)

    def long_conv(order):
        def stage_a(g):
            for u in range(0, unroll, 2):
                cs = (g * unroll + u, g * unroll + u + 1)
                z = jnp.concatenate(
                    [jnp.concatenate([src_s[chan_rows(c, 0), :], src_s[chan_rows(c, 1), :]], axis=0)
                     for c in cs], axis=1)
                a = jnp.dot(s1c, z.astype(BF16), preferred_element_type=F32)
                for i, c in enumerate(cs):
                    ar = a[0:N1, i * LANES:(i + 1) * LANES]
                    ai = a[N1:2 * N1, i * LANES:(i + 1) * LANES]
                    r0 = c * N1 if isinstance(c, int) else pl.multiple_of(c * N1, N1)
                    slab_s[pl.ds(r0, N1), 0:LANES] = (ar * twr - ai * twi).astype(BF16)
                    slab_s[pl.ds(r0, N1), LANES:2 * LANES] = (ar * twi + ai * twr).astype(BF16)

        def stage_m1(g):
            r0 = g * rows if isinstance(g, int) else pl.multiple_of(g * rows, rows)
            c0 = g * unroll if isinstance(g, int) else pl.multiple_of(g * unroll, unroll)
            x = jnp.dot(slab_s[pl.ds(r0, rows), :], s2_ref[...], preferred_element_type=F32)
            xr = x[:, 0:LANES].reshape(unroll, N1, LANES)
            xi = x[:, LANES:2 * LANES].reshape(unroll, N1, LANES)
            kr = khat_ref[order, pl.ds(c0, unroll), :, 0:LANES].astype(F32)
            ki = khat_ref[order, pl.ds(c0, unroll), :, LANES:2 * LANES].astype(F32)
            yf_s[pl.ds(r0, rows), 0:LANES] = (xr * kr - xi * ki).reshape(rows, LANES).astype(BF16)
            yf_s[pl.ds(r0, rows), LANES:2 * LANES] = (xr * ki + xi * kr).reshape(rows, LANES).astype(BF16)

        def stage_m2(g):
            r0 = g * rows if isinstance(g, int) else pl.multiple_of(g * rows, rows)
            c0 = g * unroll if isinstance(g, int) else pl.multiple_of(g * unroll, unroll)
            p = jnp.dot(yf_s[pl.ds(r0, rows), :], s2i_ref[...], preferred_element_type=F32)
            pr = p[:, 0:LANES].reshape(unroll, N1, LANES)
            pi = p[:, LANES:2 * LANES].reshape(unroll, N1, LANES)
            q_s[pl.ds(c0, unroll), 0:N1, :] = (pr * twr + pi * twi).astype(BF16)
            q_s[pl.ds(c0, unroll), N1:2 * N1, :] = (pi * twr - pr * twi).astype(BF16)

        def stage_i(g):
            for u in range(0, unroll, 2):
                cs = (g * unroll + u, g * unroll + u + 1)
                q2 = jnp.concatenate([q_s[c] for c in cs], axis=1)
                y = jnp.dot(s1i, q2, preferred_element_type=F32)
                for i, c in enumerate(cs):
                    y_s[chan_rows(c, 0), :] = y[0:h, i * LANES:(i + 1) * LANES]
                    y_s[chan_rows(c, 1), :] = y[h:2 * h, i * LANES:(i + 1) * LANES]

        G = n_groups
        for t in range(G + 3):
            if 0 <= t - 3 < G:
                stage_i(t - 3)
            if 0 <= t - 2 < G:
                stage_m2(t - 2)
            if 0 <= t - 1 < G:
                stage_m1(t - 1)
            if t < G:
                stage_a(t)

    def per_tile(fn):
        for bb in range(2):
            def body(t1, carry, bb=bb):
                fn(bb, t1, tile_rows(bb * h + t1))
                return carry
            lax.fori_loop(0, h, body, 0, unroll=8)

    def load_in(bb, t1, rws):
        src_s[rws, :] = v_ref[bb, t1].astype(F32)
    per_tile(load_in)
    long_conv(0)

    def gate1(bb, t1, rws):
        src_s[rws, :] = y_s[rws, :] * x1_ref[bb, t1].astype(F32)
    per_tile(gate1)
    long_conv(1)

    nw = jnp.broadcast_to(nw_ref[...], (n_ch, LANES))

    def gate2_norm(bb, t1, rws):
        z = y_s[rws, :] * x2_ref[bb, t1].astype(F32)
        ms = jnp.mean(z * z, axis=0, keepdims=True)
        out_ref[bb, t1] = (z * lax.rsqrt(ms + EPS) * nw).astype(BF16)
    per_tile(gate2_norm)


def _hyena(cols, nw_blk, hy4, khat, consts, unroll=HYENA_GROUP):
    B, h, _, _ = hy4.shape
    C = H_WIDTH
    n_ch = GROUP_W
    G = C // n_ch
    g0 = M_WIDTH // n_ch
    N1 = consts["N1"]
    const = lambda g, p: (0, 0)
    blk = (2, h, n_ch, LANES)
    pitch = n_ch + 8
    return pl.pallas_call(
        functools.partial(_hyena_kernel, unroll=unroll),
        grid=(G, B // 2),
        in_specs=[
            pl.BlockSpec((n_ch, 1), lambda g, p: (nw_blk + g, 0)),
            pl.BlockSpec(blk, lambda g, p: (p, 0, g0 + g, 0)),
            pl.BlockSpec(blk, lambda g, p: (p, 0, g0 + G + g, 0)),
            pl.BlockSpec(blk, lambda g, p: (p, 0, g0 + 2 * G + g, 0)),
            pl.BlockSpec((2, n_ch, N1, 2 * LANES), lambda g, p: (0, g, 0, 0)),
            pl.BlockSpec((2 * N1, 2 * h), const),
            pl.BlockSpec((2 * LANES, 2 * LANES), const),
            pl.BlockSpec((2 * LANES, 2 * LANES), const),
            pl.BlockSpec((2 * h, 2 * N1), const),
            pl.BlockSpec((N1, LANES), const),
            pl.BlockSpec((N1, LANES), const),
        ],
        out_specs=pl.BlockSpec(blk, lambda g, p: (p, 0, g, 0)),
        out_shape=jax.ShapeDtypeStruct((B, h, C, LANES), BF16),
        scratch_shapes=[
            pltpu.VMEM((2 * h * pitch, LANES), F32),
            pltpu.VMEM((n_ch * N1, 2 * LANES), BF16),
            pltpu.VMEM((n_ch * N1, 2 * LANES), BF16),
            pltpu.VMEM((n_ch, 2 * N1, LANES), BF16),
            pltpu.VMEM((2 * h * pitch, LANES), F32),
        ],
        compiler_params=_cparams("parallel", "arbitrary"),
        name="hyena",
    )(cols, hy4, hy4, hy4, khat,
      _bf(consts["s1c"]), _bf(consts["s2"]), _bf(consts["s2i"]), _bf(consts["s1i"]),
      jnp.asarray(consts["twr"]), jnp.asarray(consts["twi"]))


def _outmlp_kernel(x_ref, hm_ref, og_ref, yh_ref, nwm_ref, wo_ref, w1_ref, w2_ref, n_post_ref,
                   n_pre_ref, n_post2_ref, out_ref):
    n_t = hm_ref.shape[1]
    nwm = jnp.broadcast_to(nwm_ref[...], (M_WIDTH, LANES))
    tiles = []
    for j in range(n_t):
        hg = hm_ref[0, j].astype(F32) * _sigmoid(og_ref[0, j].astype(F32))
        heads = []
        for hd in range(M_HEADS):
            hh = hg[hd * HEAD_DIM:(hd + 1) * HEAD_DIM]
            ms = jnp.mean(hh * hh, axis=0, keepdims=True)
            heads.append(hh * lax.rsqrt(ms + EPS))
        ym = jnp.concatenate(heads, axis=0) * nwm
        yt = jnp.concatenate([ym, yh_ref[0, j].astype(F32)], axis=0)
        tiles.append(yt.T.astype(BF16))
    y = tiles[0] if n_t == 1 else jnp.concatenate(tiles, axis=0)
    mix = jnp.dot(y, wo_ref[...], preferred_element_type=F32)
    x1 = x_ref[0] + _rms_rows(mix, n_post_ref[...])
    hm = _rms_rows(x1, n_pre_ref[...]).astype(BF16)
    mid = jnp.maximum(jnp.dot(hm, w1_ref[...], preferred_element_type=F32), 0.0)
    mid = (mid * mid).astype(BF16)
    ff = jnp.dot(mid, w2_ref[...], preferred_element_type=F32)
    out_ref[0] = x1 + _rms_rows(ff, n_post2_ref[...])


def _outmlp(x, hm, mt, yh, cols, nwm_blk, wo, w1, w2, n_post, n_pre, n_post2, tm_rows):
    B, S, D = x.shape
    n_t = tm_rows // LANES
    og0 = mt.shape[2] // M_WIDTH - 1
    const = lambda b, i: (0, 0)
    resident = functools.partial(pl.BlockSpec, index_map=const, pipeline_mode=pl.Buffered(1))
    return pl.pallas_call(
        _outmlp_kernel,
        grid=(B, S // tm_rows),
        in_specs=[
            pl.BlockSpec((1, tm_rows, D), lambda b, i: (b, i, 0)),
            pl.BlockSpec((1, n_t, M_WIDTH, LANES), lambda b, i: (b, i, 0, 0)),
            pl.BlockSpec((1, n_t, M_WIDTH, LANES), lambda b, i: (b, i, og0, 0)),
            pl.BlockSpec((1, n_t, H_WIDTH, LANES), lambda b, i: (b, i, 0, 0)),
            pl.BlockSpec((M_WIDTH, 1), lambda b, i: (nwm_blk, 0)),
            resident((D, D)),
            resident((D, D_FF)),
            resident((D_FF, D)),
            pl.BlockSpec((1, D), const),
            pl.BlockSpec((1, D), const),
            pl.BlockSpec((1, D), const),
        ],
        out_specs=pl.BlockSpec((1, tm_rows, D), lambda b, i: (b, i, 0)),
        out_shape=jax.ShapeDtypeStruct((B, S, D), F32),
        compiler_params=_cparams("parallel", "parallel"),
        name="outmlp",
    )(x, hm, mt, yh, cols, wo, w1, w2, n_post, n_pre, n_post2)


def kernel(x, norm_mix_pre, norm_mix_post, norm_mlp_pre, norm_mlp_post, w_in, b_gates,
           conv_w, conv_b, mlstm_norm_w, hyena_norm_w, filt_w1, filt_b1, filt_w2, filt_b2,
           filt_w3, filt_b3, filt_w4, filt_freq, filt_bias, w_out, w_mlp_in, w_mlp_out):
    B, S, D = x.shape
    assert D == D_MODEL and B % 2 == 0 and S % CHUNK == 0
    row = lambda a: a.astype(F32).reshape(1, -1)
    tm_rows = min(PROJ_ROWS, S)
    cols = jnp.concatenate([filt_bias.reshape(-1), mlstm_norm_w, hyena_norm_w, filt_b1, filt_b2,
                            filt_b3, b_gates]).astype(F32).reshape(-1, 1)
    off_mn = HYENA_ORDER * H_WIDTH
    off_hn = off_mn + M_WIDTH
    off_fb = off_hn + H_WIDTH
    off_bg = off_fb + 3 * FILTER_HIDDEN
    assert (off_mn % M_WIDTH == 0 and off_hn % GROUP_W == 0 and off_fb % FILTER_HIDDEN == 0
            and off_bg % N_GATE == 0 and off_mn % FILT_MLP_ROWS == 0)

    n_conv = 2 * M_WIDTH + 3 * H_WIDTH
    o_gate = n_conv + 2 * M_WIDTH
    assert w_in.shape == (D, o_gate + N_GATE)
    assert conv_w.shape == (3, n_conv)
    mt, k_tm, gcm = _proj(x, row(norm_mix_pre), w_in.astype(F32).T, cols, off_bg // N_GATE,
                          conv_w.astype(F32), row(conv_b), tm_rows)

    h_m = _mlstm(mt, k_tm, gcm)

    consts = _dft_constants(S)
    kern = _filt_mlp(filt_w1.astype(F32).T, cols, off_fb // FILTER_HIDDEN, 0, filt_w2.astype(F32),
                     filt_w3.astype(F32), filt_freq.astype(F32).T, filt_w4.astype(F32), S)
    khat = _filt_fft(kern, consts)
    khat = khat.reshape(2, H_WIDTH, consts["N1"], 2 * LANES)

    y_h = _hyena(cols, off_hn // GROUP_W, mt, khat, consts)

    return _outmlp(x, h_m, mt, y_h, cols, off_mn // M_WIDTH,
                   w_out.astype(BF16), w_mlp_in.astype(BF16), w_mlp_out.astype(BF16),
                   row(norm_mix_post), row(norm_mlp_pre), row(norm_mlp_post), min(OUTMLP_ROWS, S))
```

```python
import functools
import math

import numpy as np
import jax
import jax.numpy as jnp
from jax import lax
from jax.experimental import pallas as pl
from jax.experimental.pallas import tpu as pltpu

F32 = jnp.float32
BF16 = jnp.bfloat16

D_MODEL = 1024
M_WIDTH = 512
M_HEADS = 4
HEAD_DIM = 128
H_WIDTH = 512
H_GROUPS = 8
HYENA_ORDER = 2
GROUP_W = H_WIDTH // H_GROUPS
CHUNK = 128
FILTER_EMB = 33
FILTER_HIDDEN = 64
DECAY_TARGET = 1e-2
FAST_DECAY_PCT = 0.3
SLOW_DECAY_PCT = 1.5
D_FF = 4 * D_MODEL
N_GATE = 16
EPS = 1e-6
LANES = 128
BF16_ROWS = 16
NEG_BIG = -1e30
VMEM_LIMIT = 56 * 1024 * 1024

PROJ_ROWS = 1024
PROJ_COL_BLK = 256
MLSTM_HEADS_PER_STEP = 4
FILT_MLP_ROWS = 256
FILT_FFT_ROWS = 64
HYENA_GROUP = 8
OUTMLP_ROWS = 512


def _cparams(*sem):
    return pltpu.CompilerParams(dimension_semantics=sem, vmem_limit_bytes=VMEM_LIMIT)


def _rms_rows(xf, w):
    ms = jnp.mean(xf * xf, axis=-1, keepdims=True)
    return xf * lax.rsqrt(ms + EPS) * w


def _sigmoid(x):
    return 1.0 / (1.0 + jnp.exp(-x))


def _log_sigmoid(x):
    return jnp.minimum(x, 0.0) - jnp.log(1.0 + jnp.exp(-jnp.abs(x)))


def _proj_kernel(x_ref, xp_ref, xn_ref, nw_ref, wt_ref, bgt_ref, cw_ref, cb_ref,
                 cm_ref, k_ref, gcm_ref, wbf_s, hall_s, pc_s, *, col_blk):
    TM = x_ref.shape[1]
    HALO = xp_ref.shape[1]
    j = pl.program_id(1)

    @pl.when(jnp.logical_and(pl.program_id(0) == 0, j == 0))
    def _():
        for c0 in range(0, wbf_s.shape[1], col_blk):
            wbf_s[:, c0:c0 + col_blk] = wt_ref[c0:c0 + col_blk, :].T.astype(BF16)

    nw = nw_ref[...]
    hn = _rms_rows(x_ref[0], nw).astype(BF16)
    hall_s[0:HALO, :] = _rms_rows(xp_ref[0], nw).astype(BF16)
    hall_s[HALO:HALO + TM, :] = hn
    hall_s[HALO + TM:2 * HALO + TM, :] = _rms_rows(xn_ref[0], nw).astype(BF16)
    keep_p = jnp.where(j > 0, 1.0, 0.0)
    keep_n = jnp.where(j < pl.num_programs(1) - 1, 1.0, 0.0)
    n_cm = cm_ref.shape[2]
    n_conv = cw_ref.shape[1]
    n_blk = n_cm // col_blk
    w_k0 = M_WIDTH
    w_gate = n_cm + M_WIDTH
    n_typ = N_GATE // M_HEADS
    order = [t * M_HEADS + hd for hd in range(M_HEADS) for t in range(n_typ)]
    wg = jnp.concatenate([wt_ref[w_gate + r:w_gate + r + 1, :] for r in order], axis=0)
    bg = jnp.concatenate([bgt_ref[r:r + 1, :] for r in order], axis=0)
    nt_dims = (((1,), (1,)), ((), ()))
    gt = lax.dot_general(wg.astype(BF16), hn, nt_dims, preferred_element_type=F32) + bg
    row = lax.broadcasted_iota(jnp.int32, gt.shape, 0)
    lg = jnp.where((row % 2) == 1, _log_sigmoid(gt), gt)
    for hd in range(M_HEADS):
        gcm_ref[0, hd] = lg[hd * n_typ:(hd + 1) * n_typ]

    n_rb = TM // LANES
    n_sl = col_blk // LANES

    def project(w_row0, slot):
        wblk = wbf_s[:, pl.ds(w_row0, col_blk)]
        res = jnp.dot(hall_s[...], wblk, preferred_element_type=F32)
        for u in range(n_sl):
            ls = slice(u * LANES, (u + 1) * LANES)
            pc_s[slot, u, 0:HALO, :] = res[0:HALO, ls] * keep_p
            pc_s[slot, u, HALO:HALO + TM, :] = res[HALO:HALO + TM, ls]
            pc_s[slot, u, HALO + TM:2 * HALO + TM, :] = res[HALO + TM:2 * HALO + TM, ls] * keep_n

    def conv_tile(slot, u, rb, taps, bias):
        r = HALO + rb * LANES
        return (pc_s[slot, u, pl.ds(r - 1, LANES, stride=1), :] * taps[0:1]
                + pc_s[slot, u, r:r + LANES, :] * taps[1:2]
                + pc_s[slot, u, pl.ds(r + 1, LANES, stride=1), :] * taps[2:3] + bias)

    n_q = M_WIDTH // col_blk
    assert n_blk % 2 == 0

    def w_row_of(c):
        if c == n_blk:
            return w_k0
        return (c if c < n_q else c + n_q) * col_blk

    def emit(c, slot):
        for u in range(n_sl):
            cs = slice(c * col_blk + u * LANES, c * col_blk + (u + 1) * LANES)
            w0 = w_row_of(c) + u * LANES
            for rb in range(n_rb):
                if w0 < n_conv:
                    cv = conv_tile(slot, u, rb, cw_ref[:, w0:w0 + LANES], cb_ref[:, w0:w0 + LANES])
                else:
                    cv = pc_s[slot, u, HALO + rb * LANES:HALO + (rb + 1) * LANES, :]
                cm_ref[0, rb, cs, :] = cv.T.astype(BF16)

    project(0, 0)
    for c in range(n_blk):
        emit(c, c % 2)
        project(w_row_of(c + 1), (c + 1) % 2)

    k_scale = HEAD_DIM ** -0.5
    n_k = M_WIDTH // col_blk
    for kb in range(n_k):
        if kb > 0:
            project(w_k0 + kb * col_blk, kb % 2)
        for u in range(n_sl):
            cs = slice(kb * col_blk + u * LANES, kb * col_blk + (u + 1) * LANES)
            for rb in range(n_rb):
                ks = slice(w_k0 + cs.start, w_k0 + cs.stop)
                cv = conv_tile(kb % 2, u, rb, cw_ref[:, ks], cb_ref[:, ks])
                k_ref[0, rb * LANES:(rb + 1) * LANES, cs] = (cv * _sigmoid(cv) * k_scale).astype(BF16)


def _proj(x, nw, w_in_t, cols, bg_blk, cw, cb, tm_rows, col_blk=PROJ_COL_BLK):
    B, S, D = x.shape
    HALO = BF16_ROWS
    n_conv = cw.shape[1]
    n_cm = w_in_t.shape[0] - N_GATE - M_WIDTH
    n_t = tm_rows // LANES
    hb = tm_rows // HALO
    last = S // HALO - 1
    const = lambda b, j: (0, 0)
    resident = functools.partial(pl.BlockSpec, index_map=const, pipeline_mode=pl.Buffered(1))
    return pl.pallas_call(
        functools.partial(_proj_kernel, col_blk=col_blk),
        grid=(B, S // tm_rows),
        in_specs=[
            pl.BlockSpec((1, tm_rows, D), lambda b, j: (b, j, 0)),
            pl.BlockSpec((1, HALO, D), lambda b, j: (b, jnp.maximum(j * hb - 1, 0), 0)),
            pl.BlockSpec((1, HALO, D), lambda b, j: (b, jnp.minimum((j + 1) * hb, last), 0)),
            pl.BlockSpec((1, D), const),
            resident(w_in_t.shape),
            pl.BlockSpec((N_GATE, 1), lambda b, j: (bg_blk, 0)),
            pl.BlockSpec((3, n_conv), const),
            pl.BlockSpec((1, n_conv), const),
        ],
        out_specs=[
            pl.BlockSpec((1, n_t, n_cm, LANES), lambda b, j: (b, j, 0, 0)),
            pl.BlockSpec((1, tm_rows, M_WIDTH), lambda b, j: (b, j, 0)),
            pl.BlockSpec((1, M_HEADS, N_GATE // M_HEADS, tm_rows), lambda b, j: (b, 0, 0, j)),
        ],
        out_shape=[
            jax.ShapeDtypeStruct((B, S // LANES, n_cm, LANES), BF16),
            jax.ShapeDtypeStruct((B, S, M_WIDTH), BF16),
            jax.ShapeDtypeStruct((B, M_HEADS, N_GATE // M_HEADS, S), F32),
        ],
        scratch_shapes=[pltpu.VMEM((D, n_cm + M_WIDTH), BF16),
                        pltpu.VMEM((tm_rows + 2 * HALO, D), BF16),
                        pltpu.VMEM((2, col_blk // LANES, tm_rows + 2 * HALO, LANES), F32)],
        compiler_params=_cparams("arbitrary", "arbitrary"),
        name="proj",
    )(x, x, x, nw, w_in_t, cols, cw, cb)


def _mlstm_kernel(qt_ref, vt_ref, k_ref, g_ref, out_ref, rows_s, stab_s, h_s, c_s):
    S = k_ref.shape[1]
    HB = g_ref.shape[1]
    L = CHUNK
    NC = S // L
    D = HEAD_DIM
    DA = D + BF16_ROWS
    W = HB * D

    R_A, R_MX, R_W, R_DEC, R_ISC, R_EMT, R_GT, R_AMAX = range(8)

    def scan(x, op, fill):
        lane = lax.broadcasted_iota(jnp.int32, x.shape, 1) % L
        pre = x
        suf = x
        d = 1
        while d < L:
            pre = op(pre, jnp.where(lane >= d, pltpu.roll(pre, d, 1), fill))
            suf = op(suf, jnp.where(lane < L - d, pltpu.roll(suf, S - d, 1), fill))
            d *= 2
        return pre, suf

    chains = [(hd, dr) for hd in range(HB) for dr in range(2)]
    assert HB % 2 == 0 and len(chains) <= 8
    a_rows = {}
    for h0 in range(0, HB, 2):
        g = jnp.concatenate([g_ref[0, h0, 0:4, :], g_ref[0, h0 + 1, 0:4, :]], axis=0)
        pre, suf = scan(g, jnp.add, 0.0)
        tot = pre + suf - g
        for j, hd in enumerate((h0, h0 + 1)):
            o = 4 * j
            b_rows = (pre[o + 1:o + 2], suf[o + 3:o + 4])
            for dr in range(2):
                a_rows[hd, dr] = g[o + 2 * dr:o + 2 * dr + 1] - b_rows[dr]
                rows_s[hd, dr, R_A:R_A + 1, :] = a_rows[hd, dr]
                rows_s[hd, dr, R_EMT:R_EMT + 1, :] = b_rows[dr]
                rows_s[hd, dr, R_GT:R_GT + 1, :] = tot[o + 2 * dr + 1:o + 2 * dr + 2]
    a8 = jnp.concatenate([a_rows[ch] for ch in chains]
                         + [jnp.zeros((8 - len(chains), S), F32)] * (len(chains) < 8), axis=0)
    pmax, smax = scan(a8, jnp.maximum, NEG_BIG)
    for i, (hd, dr) in enumerate(chains):
        rows_s[hd, dr, R_MX:R_MX + 1, :] = pmax[i:i + 1] if dr == 0 else smax[i:i + 1]
        rows_s[hd, dr, R_AMAX:R_AMAX + 1, :] = jnp.maximum(pmax[i:i + 1], smax[i:i + 1])

    def chunk_of(dr, i):
        return i if dr == 0 else NC - 1 - i

    def stabilisers(i, ms):
        out = []
        for (hd, dr), m in zip(chains, ms):
            r0 = pl.multiple_of(chunk_of(dr, i) * L, L)
            gt = rows_s[hd, dr, R_GT:R_GT + 1, pl.ds(r0, L)]
            m_end = jnp.maximum(gt + m, gt + rows_s[hd, dr, R_AMAX:R_AMAX + 1, pl.ds(r0, L)])
            stab_s[hd, dr, 0:1, pl.ds(r0, L)] = m
            stab_s[hd, dr, 1:2, pl.ds(r0, L)] = m_end
            out.append(m_end)
        return tuple(out)

    lax.fori_loop(0, NC, stabilisers, tuple(jnp.zeros((1, L), F32) for _ in chains), unroll=4)

    for hd, dr in chains:
        a = rows_s[hd, dr, R_A:R_A + 1, :]
        gt = rows_s[hd, dr, R_GT:R_GT + 1, :]
        m = stab_s[hd, dr, 0:1, :]
        m_end = stab_s[hd, dr, 1:2, :]
        mx = jnp.maximum(m, rows_s[hd, dr, R_MX:R_MX + 1, :])
        b = rows_s[hd, dr, R_EMT:R_EMT + 1, :]
        rows_s[hd, dr, R_MX:R_MX + 1, :] = mx
        rows_s[hd, dr, R_W:R_W + 1, :] = jnp.exp(gt + a - m_end)
        rows_s[hd, dr, R_DEC:R_DEC + 1, :] = jnp.exp(gt + m - m_end)
        rows_s[hd, dr, R_ISC:R_ISC + 1, :] = jnp.exp(m - mx)
        rows_s[hd, dr, R_EMT:R_EMT + 1, :] = jnp.exp(-(b + mx))

    ones_row = (lax.broadcasted_iota(jnp.int32, (BF16_ROWS, L), 0) == 0).astype(BF16)

    c_s[...] = jnp.zeros_like(c_s)

    si = lax.broadcasted_iota(jnp.int32, (L, L), 0)
    ti = lax.broadcasted_iota(jnp.int32, (L, L), 1)
    eye = si == ti
    valid = (si <= ti, si >= ti)

    def chain(hd, dr, c, second):
        r0 = pl.multiple_of(c * L, L)
        rows = rows_s[hd, dr, :, pl.ds(r0, L)]
        a_col = jnp.sum(jnp.where(eye, rows[R_A:R_A + 1], 0.0), axis=1, keepdims=True)
        wts = jnp.exp(jnp.where(valid[dr], a_col - rows[R_MX:R_MX + 1], NEG_BIG))
        kc = k_ref[0, pl.ds(r0, L), hd * D:(hd + 1) * D]
        qpre = qt_ref[0, c, hd * D:(hd + 1) * D, :].astype(F32)
        qtc = (qpre * _sigmoid(qpre)).astype(BF16)
        vac = jnp.concatenate([vt_ref[0, c, hd * D:(hd + 1) * D, :], ones_row], axis=0)
        cmat = c_s[hd, dr]
        both = jnp.dot(jnp.concatenate([kc, cmat.astype(BF16)], axis=0), qtc,
                       preferred_element_type=F32)
        s_w = (both[0:L] * wts).astype(BF16)
        tot = (jnp.dot(vac, s_w, preferred_element_type=F32)
               + rows[R_ISC:R_ISC + 1] * both[L:L + DA])
        den = jnp.maximum(jnp.abs(tot[D:D + 1]), rows[R_EMT:R_EMT + 1])
        h = tot[0:D] * (1.0 / den)
        hs = slice(hd * D, (hd + 1) * D)
        if second:
            out_ref[0, c, hs, :] = (h_s[hs, pl.ds(r0, L)] + h).astype(BF16)
        else:
            h_s[hs, pl.ds(r0, L)] = h
        u = (vac.astype(F32) * rows[R_W:R_W + 1]).astype(BF16)
        c_s[hd, dr] = rows[R_DEC:R_DEC + 1, 0:1] * cmat + jnp.dot(u, kc, preferred_element_type=F32)

    def step(i, carry, second):
        for hd, dr in chains:
            chain(hd, dr, chunk_of(dr, i), second)
        return carry

    assert NC % 2 == 0
    lax.fori_loop(0, NC // 2, functools.partial(step, second=False), 0)
    lax.fori_loop(NC // 2, NC, functools.partial(step, second=True), 0)


def _mlstm(mt, k_tm, gcm, heads_per_step=MLSTM_HEADS_PER_STEP):
    B, S, _ = k_tm.shape
    NC = S // CHUNK
    D = HEAD_DIM
    HB = heads_per_step
    v0 = (M_WIDTH + 3 * H_WIDTH) // (HB * D)
    W = HB * D
    NB = M_HEADS // HB
    DA = D + BF16_ROWS
    cm_blk = (1, NC, W, LANES)
    return pl.pallas_call(
        _mlstm_kernel,
        grid=(B, NB),
        in_specs=[
            pl.BlockSpec(cm_blk, lambda b, h: (b, 0, h, 0)),
            pl.BlockSpec(cm_blk, lambda b, h: (b, 0, v0 + h, 0)),
            pl.BlockSpec((1, S, W), lambda b, h: (b, 0, h)),
            pl.BlockSpec((1, HB, N_GATE // M_HEADS, S), lambda b, h: (b, h, 0, 0)),
        ],
        out_specs=pl.BlockSpec(cm_blk, lambda b, h: (b, 0, h, 0)),
        out_shape=jax.ShapeDtypeStruct((B, NC, M_WIDTH, LANES), BF16),
        scratch_shapes=[
            pltpu.VMEM((HB, 2, 8, S), F32),
            pltpu.VMEM((HB, 2, 8, S), F32),
            pltpu.VMEM((W, S), F32),
            pltpu.VMEM((HB, 2, DA, D), F32),
        ],
        compiler_params=_cparams("parallel", "parallel"),
        name="mlstm",
    )(mt, mt, k_tm, gcm)


def _filt_mlp_kernel(w1t_ref, b1_ref, w2_ref, b2_ref, w3_ref, b3_ref, fr_ref, w4f_ref, w4b_ref,
                     fbias_ref, out_ref, h3_s, *, cb_rows):
    S = h3_s.shape[2]
    hi = lax.Precision.HIGHEST
    tn_dims = (((0,), (0,)), ((), ()))
    lane = lax.broadcasted_iota(jnp.int32, (1, S), 1)
    pos_f = lane.astype(F32)
    pos_b = (S - lane).astype(F32)

    def features(pos):
        bands = (FILTER_EMB - 1) // 2
        t = pos / (S - 1)
        ang = (2.0 * math.pi) * pos / S
        fidx = lax.broadcasted_iota(jnp.int32, (bands, 1), 0).astype(F32)
        f = 1e-4 + fidx * ((bands - 1 - 1e-4) / (bands - 1))
        fa = f * ang
        w1t = w1t_ref[...]
        pre = (w1t[:, 0:1] * t
               + jnp.dot(w1t[:, 1:1 + bands], jnp.cos(fa), precision=hi, preferred_element_type=F32)
               - jnp.dot(w1t[:, 1 + bands:], jnp.sin(fa), precision=hi, preferred_element_type=F32))
        fr = fr_ref[...]
        h = jnp.sin(fr[:, 0:1] * (pre + b1_ref[...]))
        h = jnp.sin(fr[:, 1:2] * (lax.dot_general(w2_ref[...], h, tn_dims, precision=hi,
                                                  preferred_element_type=F32) + b2_ref[...]))
        return jnp.sin(fr[:, 2:3] * (lax.dot_general(w3_ref[...], h, tn_dims, precision=hi,
                                                     preferred_element_type=F32) + b3_ref[...]))

    @pl.when(pl.program_id(0) == 0)
    def _():
        h3 = features(pos_f)
        h3_s[0] = h3
        n_t = S // LANES
        anti = (lax.broadcasted_iota(jnp.int32, (LANES, LANES), 0)
                + lax.broadcasted_iota(jnp.int32, (LANES, LANES), 1) == LANES - 1).astype(F32)
        rev = jnp.concatenate(
            [jnp.dot(h3[:, (n_t - 1 - u) * LANES:(n_t - u) * LANES], anti, precision=hi,
                     preferred_element_type=F32) for u in range(n_t)], axis=1)
        h3_s[1] = pltpu.roll(rev, 1, 1)

    r = pl.program_id(0) * cb_rows + lax.broadcasted_iota(jnp.int32, (cb_rows, 1), 0)
    ch = (r % H_WIDTH).astype(F32)
    max_decay = math.log(DECAY_TARGET) / FAST_DECAY_PCT
    min_decay = math.log(DECAY_TARGET) / SLOW_DECAY_PCT
    delta = jnp.abs(min_decay + ch * ((max_decay - min_decay) / (H_WIDTH - 1)))
    n_t = S // LANES
    for half, (w_ref, pos) in enumerate(((w4f_ref, pos_f), (w4b_ref, pos_b))):
        filt = lax.dot_general(w_ref[...].astype(BF16), h3_s[half].astype(BF16), tn_dims,
                               preferred_element_type=F32)
        filt = filt * jnp.exp(-(pos / (S - 1)) * delta)
        if half == 0:
            filt = jnp.where(lane == 0, filt + fbias_ref[...], filt)
        else:
            filt = jnp.where(lane == 0, 0.0, filt)
        for u in range(n_t):
            out_ref[half * n_t + u] = filt[:, u * LANES:(u + 1) * LANES].astype(BF16)


def _filt_mlp(w1t, cols, fb_blk, fbias_blk, w2, w3, fr, w4, S, cb_rows=FILT_MLP_ROWS):
    R = w4.shape[1] // 2
    Hd = FILTER_HIDDEN
    const = lambda i: (0, 0)
    nblk = R // cb_rows
    return pl.pallas_call(
        functools.partial(_filt_mlp_kernel, cb_rows=cb_rows),
        grid=(nblk,),
        in_specs=[
            pl.BlockSpec((Hd, FILTER_EMB), const),
            pl.BlockSpec((Hd, 1), lambda i: (fb_blk, 0)),
            pl.BlockSpec((Hd, Hd), const),
            pl.BlockSpec((Hd, 1), lambda i: (fb_blk + 1, 0)),
            pl.BlockSpec((Hd, Hd), const),
            pl.BlockSpec((Hd, 1), lambda i: (fb_blk + 2, 0)),
            pl.BlockSpec((Hd, 3), const),
            pl.BlockSpec((Hd, cb_rows), lambda i: (0, i)),
            pl.BlockSpec((Hd, cb_rows), lambda i: (0, nblk + i)),
            pl.BlockSpec((cb_rows, 1), lambda i: (fbias_blk + i, 0)),
        ],
        out_specs=pl.BlockSpec((2 * S // LANES, cb_rows, LANES), lambda i: (0, i, 0)),
        out_shape=jax.ShapeDtypeStruct((2 * S // LANES, R, LANES), BF16),
        scratch_shapes=[pltpu.VMEM((2, Hd, S), F32)],
        compiler_params=_cparams("arbitrary"),
        name="filt_mlp",
    )(w1t, cols, w2, cols, w3, cols, fr, w4, w4, cols)


@functools.lru_cache(maxsize=None)
def _dft_constants(S):
    N = 2 * S
    N2 = LANES
    N1 = N // N2
    h = N1 // 2
    k1 = np.arange(N1)
    k2 = np.arange(N2)
    a1 = -2.0 * np.pi * np.outer(k1, k1) / N1
    f1r, f1i = np.cos(a1), np.sin(a1)
    at = -2.0 * np.pi * np.outer(k1, k2) / N
    twr, twi = np.cos(at), np.sin(at)
    a2 = -2.0 * np.pi * np.outer(k2, k2) / N2
    f2r, f2i = np.cos(a2), np.sin(a2)
    s1c = np.block([[f1r[:, :h], -f1i[:, :h]], [f1i[:, :h], f1r[:, :h]]])
    s1r = np.concatenate([f1r, f1i], axis=0)
    s2 = np.block([[f2r, f2i], [-f2i, f2r]])
    s2i = np.block([[f2r, -f2i], [f2i, f2r]])
    s1i = np.block([[f1r[:h, :], f1i[:h, :]], [-f1i[:h, :], f1r[:h, :]]])
    cast = lambda a: np.asarray(a, np.float32)
    return dict(s1c=cast(s1c), s1r=cast(s1r), s2=cast(s2), s2i=cast(s2i), s1i=cast(s1i),
                twr=cast(twr), twi=cast(twi), N1=N1, h=h)


def _bf(a):
    return jnp.asarray(a, F32).astype(BF16)


def _filt_fft_kernel(kern_ref, s1r_ref, s2_ref, twr_ref, twi_ref, out_ref, src_s, slab_s, *, unroll):
    N1 = twr_ref.shape[0]
    n_ch = kern_ref.shape[1]
    pitch = src_s.shape[0] // N1
    scale = 1.0 / (N1 * LANES)
    twr = twr_ref[...]
    twi = twi_ref[...]
    s1r = s1r_ref[...]

    def load_in(t1, carry):
        src_s[pl.ds(pl.multiple_of(t1 * pitch, 8), n_ch), :] = kern_ref[t1].astype(F32)
        return carry
    lax.fori_loop(0, N1, load_in, 0, unroll=4)

    def per_group(j, carry):
        for u in range(0, unroll, 2):
            cs = (j * unroll + u, j * unroll + u + 1)
            z = jnp.concatenate([src_s[pl.ds(c, N1, stride=pitch), :] for c in cs], axis=1)
            a = jnp.dot(s1r, z.astype(BF16), preferred_element_type=F32)
            for i, c in enumerate(cs):
                ar = a[0:N1, i * LANES:(i + 1) * LANES]
                ai = a[N1:2 * N1, i * LANES:(i + 1) * LANES]
                r0 = pl.multiple_of(c * N1, N1)
                slab_s[pl.ds(r0, N1), 0:LANES] = (ar * twr - ai * twi).astype(BF16)
                slab_s[pl.ds(r0, N1), LANES:2 * LANES] = (ar * twi + ai * twr).astype(BF16)
        return carry

    lax.fori_loop(0, n_ch // unroll, per_group, 0)
    x = jnp.dot(slab_s[...], s2_ref[...], preferred_element_type=F32)
    out_ref[...] = (x * scale).astype(BF16).reshape(n_ch, N1, 2 * LANES)


def _filt_fft(kern, consts, n_ch=FILT_FFT_ROWS, unroll=FILT_FFT_ROWS):
    N1, R, _ = kern.shape
    const = lambda i: (0, 0)
    pitch = n_ch + 8
    return pl.pallas_call(
        functools.partial(_filt_fft_kernel, unroll=unroll),
        grid=(R // n_ch,),
        in_specs=[
            pl.BlockSpec((N1, n_ch, LANES), lambda i: (0, i, 0)),
            pl.BlockSpec((2 * N1, N1), const),
            pl.BlockSpec((2 * LANES, 2 * LANES), const),
            pl.BlockSpec((N1, LANES), const),
            pl.BlockSpec((N1, LANES), const),
        ],
        out_specs=pl.BlockSpec((n_ch, N1, 2 * LANES), lambda i: (i, 0, 0)),
        out_shape=jax.ShapeDtypeStruct((R, N1, 2 * LANES), BF16),
        scratch_shapes=[pltpu.VMEM((N1 * pitch, LANES), F32),
                        pltpu.VMEM((n_ch * N1, 2 * LANES), BF16)],
        compiler_params=_cparams("parallel"),
        name="filt_fft",
    )(kern, _bf(consts["s1r"]), _bf(consts["s2"]),
      jnp.asarray(consts["twr"]), jnp.asarray(consts["twi"]))


def _hyena_kernel(nw_ref, v_ref, x1_ref, x2_ref, khat_ref, s1c_ref, s2_ref, s2i_ref, s1i_ref,
                  twr_ref, twi_ref, out_ref, src_s, slab_s, yf_s, q_s, y_s, *, unroll):
    h = v_ref.shape[1]
    n_ch = v_ref.shape[2]
    N1 = 2 * h
    twr = twr_ref[...]
    twi = twi_ref[...]
    s1c = s1c_ref[...]
    s1i = s1i_ref[...]
    n_groups = n_ch // unroll
    rows = unroll * N1
    pitch = src_s.shape[0] // (2 * h)

    def tile_rows(i):
        return pl.ds(pl.multiple_of(i * pitch, 8), n_ch)

    def chan_rows(c, bb):
        return pl.ds(bb * h * pitch + c, h, stride=pitch)

    def long_conv(order):
        def stage_a(g):
            for u in range(0, unroll, 2):
                cs = (g * unroll + u, g * unroll + u + 1)
                z = jnp.concatenate(
                    [jnp.concatenate([src_s[chan_rows(c, 0), :], src_s[chan_rows(c, 1), :]], axis=0)
                     for c in cs], axis=1)
                a = jnp.dot(s1c, z.astype(BF16), preferred_element_type=F32)
                for i, c in enumerate(cs):
                    ar = a[0:N1, i * LANES:(i + 1) * LANES]
                    ai = a[N1:2 * N1, i * LANES:(i + 1) * LANES]
                    r0 = c * N1 if isinstance(c, int) else pl.multiple_of(c * N1, N1)
                    slab_s[pl.ds(r0, N1), 0:LANES] = (ar * twr - ai * twi).astype(BF16)
                    slab_s[pl.ds(r0, N1), LANES:2 * LANES] = (ar * twi + ai * twr).astype(BF16)

        def stage_m1(g):
            r0 = g * rows if isinstance(g, int) else pl.multiple_of(g * rows, rows)
            c0 = g * unroll if isinstance(g, int) else pl.multiple_of(g * unroll, unroll)
            x = jnp.dot(slab_s[pl.ds(r0, rows), :], s2_ref[...], preferred_element_type=F32)
            xr = x[:, 0:LANES].reshape(unroll, N1, LANES)
            xi = x[:, LANES:2 * LANES].reshape(unroll, N1, LANES)
            kr = khat_ref[order, pl.ds(c0, unroll), :, 0:LANES].astype(F32)
            ki = khat_ref[order, pl.ds(c0, unroll), :, LANES:2 * LANES].astype(F32)
            yf_s[pl.ds(r0, rows), 0:LANES] = (xr * kr - xi * ki).reshape(rows, LANES).astype(BF16)
            yf_s[pl.ds(r0, rows), LANES:2 * LANES] = (xr * ki + xi * kr).reshape(rows, LANES).astype(BF16)

        def stage_m2(g):
            r0 = g * rows if isinstance(g, int) else pl.multiple_of(g * rows, rows)
            c0 = g * unroll if isinstance(g, int) else pl.multiple_of(g * unroll, unroll)
            p = jnp.dot(yf_s[pl.ds(r0, rows), :], s2i_ref[...], preferred_element_type=F32)
            pr = p[:, 0:LANES].reshape(unroll, N1, LANES)
            pi = p[:, LANES:2 * LANES].reshape(unroll, N1, LANES)
            q_s[pl.ds(c0, unroll), 0:N1, :] = (pr * twr + pi * twi).astype(BF16)
            q_s[pl.ds(c0, unroll), N1:2 * N1, :] = (pi * twr - pr * twi).astype(BF16)

        def stage_i(g):
            for u in range(0, unroll, 2):
                cs = (g * unroll + u, g * unroll + u + 1)
                q2 = jnp.concatenate([q_s[c] for c in cs], axis=1)
                y = jnp.dot(s1i, q2, preferred_element_type=F32)
                for i, c in enumerate(cs):
                    y_s[chan_rows(c, 0), :] = y[0:h, i * LANES:(i + 1) * LANES]
                    y_s[chan_rows(c, 1), :] = y[h:2 * h, i * LANES:(i + 1) * LANES]

        G = n_groups
        for t in range(G + 3):
            if 0 <= t - 3 < G:
                stage_i(t - 3)
            if 0 <= t - 2 < G:
                stage_m2(t - 2)
            if 0 <= t - 1 < G:
                stage_m1(t - 1)
            if t < G:
                stage_a(t)

    def per_tile(fn):
        for bb in range(2):
            def body(t1, carry, bb=bb):
                fn(bb, t1, tile_rows(bb * h + t1))
                return carry
            lax.fori_loop(0, h, body, 0, unroll=8)

    def load_in(bb, t1, rws):
        src_s[rws, :] = v_ref[bb, t1].astype(F32)
    per_tile(load_in)
    long_conv(0)

    def gate1(bb, t1, rws):
        src_s[rws, :] = y_s[rws, :] * x1_ref[bb, t1].astype(F32)
    per_tile(gate1)
    long_conv(1)

    nw = jnp.broadcast_to(nw_ref[...], (n_ch, LANES))

    def gate2_norm(bb, t1, rws):
        z = y_s[rws, :] * x2_ref[bb, t1].astype(F32)
        ms = jnp.mean(z * z, axis=0, keepdims=True)
        out_ref[bb, t1] = (z * lax.rsqrt(ms + EPS) * nw).astype(BF16)
    per_tile(gate2_norm)


def _hyena(cols, nw_blk, hy4, khat, consts, unroll=HYENA_GROUP):
    B, h, _, _ = hy4.shape
    C = H_WIDTH
    n_ch = GROUP_W
    G = C // n_ch
    g0 = M_WIDTH // n_ch
    N1 = consts["N1"]
    const = lambda g, p: (0, 0)
    blk = (2, h, n_ch, LANES)
    pitch = n_ch + 8
    return pl.pallas_call(
        functools.partial(_hyena_kernel, unroll=unroll),
        grid=(G, B // 2),
        in_specs=[
            pl.BlockSpec((n_ch, 1), lambda g, p: (nw_blk + g, 0)),
            pl.BlockSpec(blk, lambda g, p: (p, 0, g0 + g, 0)),
            pl.BlockSpec(blk, lambda g, p: (p, 0, g0 + G + g, 0)),
            pl.BlockSpec(blk, lambda g, p: (p, 0, g0 + 2 * G + g, 0)),
            pl.BlockSpec((2, n_ch, N1, 2 * LANES), lambda g, p: (0, g, 0, 0)),
            pl.BlockSpec((2 * N1, 2 * h), const),
            pl.BlockSpec((2 * LANES, 2 * LANES), const),
            pl.BlockSpec((2 * LANES, 2 * LANES), const),
            pl.BlockSpec((2 * h, 2 * N1), const),
            pl.BlockSpec((N1, LANES), const),
            pl.BlockSpec((N1, LANES), const),
        ],
        out_specs=pl.BlockSpec(blk, lambda g, p: (p, 0, g, 0)),
        out_shape=jax.ShapeDtypeStruct((B, h, C, LANES), BF16),
        scratch_shapes=[
            pltpu.VMEM((2 * h * pitch, LANES), F32),
            pltpu.VMEM((n_ch * N1, 2 * LANES), BF16),
            pltpu.VMEM((n_ch * N1, 2 * LANES), BF16),
            pltpu.VMEM((n_ch, 2 * N1, LANES), BF16),
            pltpu.VMEM((2 * h * pitch, LANES), F32),
        ],
        compiler_params=_cparams("parallel", "arbitrary"),
        name="hyena",
    )(cols, hy4, hy4, hy4, khat,
      _bf(consts["s1c"]), _bf(consts["s2"]), _bf(consts["s2i"]), _bf(consts["s1i"]),
      jnp.asarray(consts["twr"]), jnp.asarray(consts["twi"]))


def _outmlp_kernel(x_ref, hm_ref, og_ref, yh_ref, nwm_ref, wo_ref, w1_ref, w2_ref, n_post_ref,
                   n_pre_ref, n_post2_ref, out_ref):
    n_t = hm_ref.shape[1]
    nwm = jnp.broadcast_to(nwm_ref[...], (M_WIDTH, LANES))
    tiles = []
    for j in range(n_t):
        hg = hm_ref[0, j].astype(F32) * _sigmoid(og_ref[0, j].astype(F32))
        heads = []
        for hd in range(M_HEADS):
            hh = hg[hd * HEAD_DIM:(hd + 1) * HEAD_DIM]
            ms = jnp.mean(hh * hh, axis=0, keepdims=True)
            heads.append(hh * lax.rsqrt(ms + EPS))
        ym = jnp.concatenate(heads, axis=0) * nwm
        yt = jnp.concatenate([ym, yh_ref[0, j].astype(F32)], axis=0)
        tiles.append(yt.T.astype(BF16))
    y = tiles[0] if n_t == 1 else jnp.concatenate(tiles, axis=0)
    mix = jnp.dot(y, wo_ref[...], preferred_element_type=F32)
    x1 = x_ref[0] + _rms_rows(mix, n_post_ref[...])
    hm = _rms_rows(x1, n_pre_ref[...]).astype(BF16)
    mid = jnp.maximum(jnp.dot(hm, w1_ref[...], preferred_element_type=F32), 0.0)
    mid = (mid * mid).astype(BF16)
    ff = jnp.dot(mid, w2_ref[...], preferred_element_type=F32)
    out_ref[0] = x1 + _rms_rows(ff, n_post2_ref[...])


def _outmlp(x, hm, mt, yh, cols, nwm_blk, wo, w1, w2, n_post, n_pre, n_post2, tm_rows):
    B, S, D = x.shape
    n_t = tm_rows // LANES
    og0 = mt.shape[2] // M_WIDTH - 1
    const = lambda b, i: (0, 0)
    resident = functools.partial(pl.BlockSpec, index_map=const, pipeline_mode=pl.Buffered(1))
    return pl.pallas_call(
        _outmlp_kernel,
        grid=(B, S // tm_rows),
        in_specs=[
            pl.BlockSpec((1, tm_rows, D), lambda b, i: (b, i, 0)),
            pl.BlockSpec((1, n_t, M_WIDTH, LANES), lambda b, i: (b, i, 0, 0)),
            pl.BlockSpec((1, n_t, M_WIDTH, LANES), lambda b, i: (b, i, og0, 0)),
            pl.BlockSpec((1, n_t, H_WIDTH, LANES), lambda b, i: (b, i, 0, 0)),
            pl.BlockSpec((M_WIDTH, 1), lambda b, i: (nwm_blk, 0)),
            resident((D, D)),
            resident((D, D_FF)),
            resident((D_FF, D)),
            pl.BlockSpec((1, D), const),
            pl.BlockSpec((1, D), const),
            pl.BlockSpec((1, D), const),
        ],
        out_specs=pl.BlockSpec((1, tm_rows, D), lambda b, i: (b, i, 0)),
        out_shape=jax.ShapeDtypeStruct((B, S, D), F32),
        compiler_params=_cparams("parallel", "parallel"),
        name="outmlp",
    )(x, hm, mt, yh, cols, wo, w1, w2, n_post, n_pre, n_post2)


def kernel(x, norm_mix_pre, norm_mix_post, norm_mlp_pre, norm_mlp_post, w_in, b_gates,
           conv_w, conv_b, mlstm_norm_w, hyena_norm_w, filt_w1, filt_b1, filt_w2, filt_b2,
           filt_w3, filt_b3, filt_w4, filt_freq, filt_bias, w_out, w_mlp_in, w_mlp_out):
    B, S, D = x.shape
    assert D == D_MODEL and B % 2 == 0 and S % CHUNK == 0
    row = lambda a: a.astype(F32).reshape(1, -1)
    tm_rows = min(PROJ_ROWS, S)
    cols = jnp.concatenate([filt_bias.reshape(-1), mlstm_norm_w, hyena_norm_w, filt_b1, filt_b2,
                            filt_b3, b_gates]).astype(F32).reshape(-1, 1)
    off_mn = HYENA_ORDER * H_WIDTH
    off_hn = off_mn + M_WIDTH
    off_fb = off_hn + H_WIDTH
    off_bg = off_fb + 3 * FILTER_HIDDEN
    assert (off_mn % M_WIDTH == 0 and off_hn % GROUP_W == 0 and off_fb % FILTER_HIDDEN == 0
            and off_bg % N_GATE == 0 and off_mn % FILT_MLP_ROWS == 0)

    n_conv = 2 * M_WIDTH + 3 * H_WIDTH
    o_gate = n_conv + 2 * M_WIDTH
    assert w_in.shape == (D, o_gate + N_GATE)
    assert conv_w.shape == (3, n_conv)
    mt, k_tm, gcm = _proj(x, row(norm_mix_pre), w_in.astype(F32).T, cols, off_bg // N_GATE,
                          conv_w.astype(F32), row(conv_b), tm_rows)

    h_m = _mlstm(mt, k_tm, gcm)

    consts = _dft_constants(S)
    kern = _filt_mlp(filt_w1.astype(F32).T, cols, off_fb // FILTER_HIDDEN, 0, filt_w2.astype(F32),
                     filt_w3.astype(F32), filt_freq.astype(F32).T, filt_w4.astype(F32), S)
    khat = _filt_fft(kern, consts)
    khat = khat.reshape(2, H_WIDTH, consts["N1"], 2 * LANES)

    y_h = _hyena(cols, off_hn // GROUP_W, mt, khat, consts)

    return _outmlp(x, h_m, mt, y_h, cols, off_mn // M_WIDTH,
                   w_out.astype(BF16), w_mlp_in.astype(BF16), w_mlp_out.astype(BF16),
                   row(norm_mix_post), row(norm_mlp_pre), row(norm_mlp_post), min(OUTMLP_ROWS, S))
```

```python
import functools
import math

import numpy as np
import jax
import jax.numpy as jnp
from jax import lax
from jax.experimental import pallas as pl
from jax.experimental.pallas import tpu as pltpu

F32 = jnp.float32
BF16 = jnp.bfloat16

D_MODEL = 1024
M_WIDTH = 512
M_HEADS = 4
HEAD_DIM = 128
H_WIDTH = 512
H_GROUPS = 8
HYENA_ORDER = 2
GROUP_W = H_WIDTH // H_GROUPS
CHUNK = 128
FILTER_EMB = 33
FILTER_HIDDEN = 64
DECAY_TARGET = 1e-2
FAST_DECAY_PCT = 0.3
SLOW_DECAY_PCT = 1.5
D_FF = 4 * D_MODEL
N_GATE = 16
EPS = 1e-6
LANES = 128
BF16_ROWS = 16
NEG_BIG = -1e30
VMEM_LIMIT = 56 * 1024 * 1024

PROJ_ROWS = 1024
PROJ_COL_BLK = 256
MLSTM_HEADS_PER_STEP = 4
FILT_MLP_ROWS = 256
FILT_FFT_ROWS = 128
HYENA_GROUP = 8
OUTMLP_ROWS = 512


def _cparams(*sem):
    return pltpu.CompilerParams(dimension_semantics=sem, vmem_limit_bytes=VMEM_LIMIT)


def _rms_rows(xf, w):
    ms = jnp.mean(xf * xf, axis=-1, keepdims=True)
    return xf * lax.rsqrt(ms + EPS) * w


def _sigmoid(x):
    return 1.0 / (1.0 + jnp.exp(-x))


def _log_sigmoid(x):
    return jnp.minimum(x, 0.0) - jnp.log(1.0 + jnp.exp(-jnp.abs(x)))


def _proj_kernel(x_ref, xp_ref, xn_ref, nw_ref, wt_ref, bgt_ref, cw_ref, cb_ref,
                 cm_ref, k_ref, gcm_ref, wbf_s, hall_s, pc_s, *, col_blk):
    TM = x_ref.shape[1]
    HALO = xp_ref.shape[1]
    j = pl.program_id(1)

    @pl.when(jnp.logical_and(pl.program_id(0) == 0, j == 0))
    def _():
        for c0 in range(0, wbf_s.shape[1], col_blk):
            wbf_s[:, c0:c0 + col_blk] = wt_ref[c0:c0 + col_blk, :].T.astype(BF16)

    nw = nw_ref[...]
    hn = _rms_rows(x_ref[0], nw).astype(BF16)
    hall_s[0:HALO, :] = _rms_rows(xp_ref[0], nw).astype(BF16)
    hall_s[HALO:HALO + TM, :] = hn
    hall_s[HALO + TM:2 * HALO + TM, :] = _rms_rows(xn_ref[0], nw).astype(BF16)
    keep_p = jnp.where(j > 0, 1.0, 0.0)
    keep_n = jnp.where(j < pl.num_programs(1) - 1, 1.0, 0.0)
    n_cm = cm_ref.shape[2]
    n_conv = cw_ref.shape[1]
    n_blk = n_cm // col_blk
    w_k0 = M_WIDTH
    w_gate = n_cm + M_WIDTH
    n_typ = N_GATE // M_HEADS
    order = [t * M_HEADS + hd for hd in range(M_HEADS) for t in range(n_typ)]
    wg = jnp.concatenate([wt_ref[w_gate + r:w_gate + r + 1, :] for r in order], axis=0)
    bg = jnp.concatenate([bgt_ref[r:r + 1, :] for r in order], axis=0)
    nt_dims = (((1,), (1,)), ((), ()))
    gt = lax.dot_general(wg.astype(BF16), hn, nt_dims, preferred_element_type=F32) + bg
    row = lax.broadcasted_iota(jnp.int32, gt.shape, 0)
    lg = jnp.where((row % 2) == 1, _log_sigmoid(gt), gt)
    for hd in range(M_HEADS):
        gcm_ref[0, hd] = lg[hd * n_typ:(hd + 1) * n_typ]

    n_rb = TM // LANES
    n_sl = col_blk // LANES

    def project(w_row0, slot):
        wblk = wbf_s[:, pl.ds(w_row0, col_blk)]
        res = jnp.dot(hall_s[...], wblk, preferred_element_type=F32)
        for u in range(n_sl):
            ls = slice(u * LANES, (u + 1) * LANES)
            pc_s[slot, u, 0:HALO, :] = res[0:HALO, ls] * keep_p
            pc_s[slot, u, HALO:HALO + TM, :] = res[HALO:HALO + TM, ls]
            pc_s[slot, u, HALO + TM:2 * HALO + TM, :] = res[HALO + TM:2 * HALO + TM, ls] * keep_n

    def conv_tile(slot, u, rb, taps, bias):
        r = HALO + rb * LANES
        return (pc_s[slot, u, pl.ds(r - 1, LANES, stride=1), :] * taps[0:1]
                + pc_s[slot, u, r:r + LANES, :] * taps[1:2]
                + pc_s[slot, u, pl.ds(r + 1, LANES, stride=1), :] * taps[2:3] + bias)

    n_q = M_WIDTH // col_blk
    assert n_blk % 2 == 0

    def w_row_of(c):
        if c == n_blk:
            return w_k0
        return (c if c < n_q else c + n_q) * col_blk

    def emit(c, slot):
        for u in range(n_sl):
            cs = slice(c * col_blk + u * LANES, c * col_blk + (u + 1) * LANES)
            w0 = w_row_of(c) + u * LANES
            for rb in range(n_rb):
                if w0 < n_conv:
                    cv = conv_tile(slot, u, rb, cw_ref[:, w0:w0 + LANES], cb_ref[:, w0:w0 + LANES])
                else:
                    cv = pc_s[slot, u, HALO + rb * LANES:HALO + (rb + 1) * LANES, :]
                cm_ref[0, rb, cs, :] = cv.T.astype(BF16)

    project(0, 0)
    for c in range(n_blk):
        emit(c, c % 2)
        project(w_row_of(c + 1), (c + 1) % 2)

    k_scale = HEAD_DIM ** -0.5
    n_k = M_WIDTH // col_blk
    for kb in range(n_k):
        if kb > 0:
            project(w_k0 + kb * col_blk, kb % 2)
        for u in range(n_sl):
            cs = slice(kb * col_blk + u * LANES, kb * col_blk + (u + 1) * LANES)
            for rb in range(n_rb):
                ks = slice(w_k0 + cs.start, w_k0 + cs.stop)
                cv = conv_tile(kb % 2, u, rb, cw_ref[:, ks], cb_ref[:, ks])
                k_ref[0, rb * LANES:(rb + 1) * LANES, cs] = (cv * _sigmoid(cv) * k_scale).astype(BF16)


def _proj(x, nw, w_in_t, cols, bg_blk, cw, cb, tm_rows, col_blk=PROJ_COL_BLK):
    B, S, D = x.shape
    HALO = BF16_ROWS
    n_conv = cw.shape[1]
    n_cm = w_in_t.shape[0] - N_GATE - M_WIDTH
    n_t = tm_rows // LANES
    hb = tm_rows // HALO
    last = S // HALO - 1
    const = lambda b, j: (0, 0)
    resident = functools.partial(pl.BlockSpec, index_map=const, pipeline_mode=pl.Buffered(1))
    return pl.pallas_call(
        functools.partial(_proj_kernel, col_blk=col_blk),
        grid=(B, S // tm_rows),
        in_specs=[
            pl.BlockSpec((1, tm_rows, D), lambda b, j: (b, j, 0)),
            pl.BlockSpec((1, HALO, D), lambda b, j: (b, jnp.maximum(j * hb - 1, 0), 0)),
            pl.BlockSpec((1, HALO, D), lambda b, j: (b, jnp.minimum((j + 1) * hb, last), 0)),
            pl.BlockSpec((1, D), const),
            resident(w_in_t.shape),
            pl.BlockSpec((N_GATE, 1), lambda b, j: (bg_blk, 0)),
            pl.BlockSpec((3, n_conv), const),
            pl.BlockSpec((1, n_conv), const),
        ],
        out_specs=[
            pl.BlockSpec((1, n_t, n_cm, LANES), lambda b, j: (b, j, 0, 0)),
            pl.BlockSpec((1, tm_rows, M_WIDTH), lambda b, j: (b, j, 0)),
            pl.BlockSpec((1, M_HEADS, N_GATE // M_HEADS, tm_rows), lambda b, j: (b, 0, 0, j)),
        ],
        out_shape=[
            jax.ShapeDtypeStruct((B, S // LANES, n_cm, LANES), BF16),
            jax.ShapeDtypeStruct((B, S, M_WIDTH), BF16),
            jax.ShapeDtypeStruct((B, M_HEADS, N_GATE // M_HEADS, S), F32),
        ],
        scratch_shapes=[pltpu.VMEM((D, n_cm + M_WIDTH), BF16),
                        pltpu.VMEM((tm_rows + 2 * HALO, D), BF16),
                        pltpu.VMEM((2, col_blk // LANES, tm_rows + 2 * HALO, LANES), F32)],
        compiler_params=_cparams("arbitrary", "arbitrary"),
        name="proj",
    )(x, x, x, nw, w_in_t, cols, cw, cb)


def _mlstm_kernel(qt_ref, vt_ref, k_ref, g_ref, out_ref, rows_s, stab_s, h_s, c_s):
    S = k_ref.shape[1]
    HB = g_ref.shape[1]
    L = CHUNK
    NC = S // L
    D = HEAD_DIM
    DA = D + BF16_ROWS
    W = HB * D

    R_A, R_MX, R_W, R_DEC, R_ISC, R_EMT, R_GT, R_AMAX = range(8)

    def scan(x, op, fill):
        lane = lax.broadcasted_iota(jnp.int32, x.shape, 1) % L
        pre = x
        suf = x
        d = 1
        while d < L:
            pre = op(pre, jnp.where(lane >= d, pltpu.roll(pre, d, 1), fill))
            suf = op(suf, jnp.where(lane < L - d, pltpu.roll(suf, S - d, 1), fill))
            d *= 2
        return pre, suf

    chains = [(hd, dr) for hd in range(HB) for dr in range(2)]
    assert HB % 2 == 0 and len(chains) <= 8
    a_rows = {}
    for h0 in range(0, HB, 2):
        g = jnp.concatenate([g_ref[0, h0, 0:4, :], g_ref[0, h0 + 1, 0:4, :]], axis=0)
        pre, suf = scan(g, jnp.add, 0.0)
        tot = pre + suf - g
        for j, hd in enumerate((h0, h0 + 1)):
            o = 4 * j
            b_rows = (pre[o + 1:o + 2], suf[o + 3:o + 4])
            for dr in range(2):
                a_rows[hd, dr] = g[o + 2 * dr:o + 2 * dr + 1] - b_rows[dr]
                rows_s[hd, dr, R_A:R_A + 1, :] = a_rows[hd, dr]
                rows_s[hd, dr, R_EMT:R_EMT + 1, :] = b_rows[dr]
                rows_s[hd, dr, R_GT:R_GT + 1, :] = tot[o + 2 * dr + 1:o + 2 * dr + 2]
    a8 = jnp.concatenate([a_rows[ch] for ch in chains]
                         + [jnp.zeros((8 - len(chains), S), F32)] * (len(chains) < 8), axis=0)
    pmax, smax = scan(a8, jnp.maximum, NEG_BIG)
    for i, (hd, dr) in enumerate(chains):
        rows_s[hd, dr, R_MX:R_MX + 1, :] = pmax[i:i + 1] if dr == 0 else smax[i:i + 1]
        rows_s[hd, dr, R_AMAX:R_AMAX + 1, :] = jnp.maximum(pmax[i:i + 1], smax[i:i + 1])

    def chunk_of(dr, i):
        return i if dr == 0 else NC - 1 - i

    def stabilisers(i, ms):
        out = []
        for (hd, dr), m in zip(chains, ms):
            r0 = pl.multiple_of(chunk_of(dr, i) * L, L)
            gt = rows_s[hd, dr, R_GT:R_GT + 1, pl.ds(r0, L)]
            m_end = jnp.maximum(gt + m, gt + rows_s[hd, dr, R_AMAX:R_AMAX + 1, pl.ds(r0, L)])
            stab_s[hd, dr, 0:1, pl.ds(r0, L)] = m
            stab_s[hd, dr, 1:2, pl.ds(r0, L)] = m_end
            out.append(m_end)
        return tuple(out)

    lax.fori_loop(0, NC, stabilisers, tuple(jnp.zeros((1, L), F32) for _ in chains), unroll=4)

    for hd, dr in chains:
        a = rows_s[hd, dr, R_A:R_A + 1, :]
        gt = rows_s[hd, dr, R_GT:R_GT + 1, :]
        m = stab_s[hd, dr, 0:1, :]
        m_end = stab_s[hd, dr, 1:2, :]
        mx = jnp.maximum(m, rows_s[hd, dr, R_MX:R_MX + 1, :])
        b = rows_s[hd, dr, R_EMT:R_EMT + 1, :]
        rows_s[hd, dr, R_MX:R_MX + 1, :] = mx
        rows_s[hd, dr, R_W:R_W + 1, :] = jnp.exp(gt + a - m_end)
        rows_s[hd, dr, R_DEC:R_DEC + 1, :] = jnp.exp(gt + m - m_end)
        rows_s[hd, dr, R_ISC:R_ISC + 1, :] = jnp.exp(m - mx)
        rows_s[hd, dr, R_EMT:R_EMT + 1, :] = jnp.exp(-(b + mx))

    ones_row = (lax.broadcasted_iota(jnp.int32, (BF16_ROWS, L), 0) == 0).astype(BF16)

    c_s[...] = jnp.zeros_like(c_s)

    si = lax.broadcasted_iota(jnp.int32, (L, L), 0)
    ti = lax.broadcasted_iota(jnp.int32, (L, L), 1)
    eye = si == ti
    valid = (si <= ti, si >= ti)

    def chain(hd, dr, c, second):
        r0 = pl.multiple_of(c * L, L)
        rows = rows_s[hd, dr, :, pl.ds(r0, L)]
        a_col = jnp.sum(jnp.where(eye, rows[R_A:R_A + 1], 0.0), axis=1, keepdims=True)
        wts = jnp.exp(jnp.where(valid[dr], a_col - rows[R_MX:R_MX + 1], NEG_BIG))
        kc = k_ref[0, pl.ds(r0, L), hd * D:(hd + 1) * D]
        qpre = qt_ref[0, c, hd * D:(hd + 1) * D, :].astype(F32)
        qtc = (qpre * _sigmoid(qpre)).astype(BF16)
        vac = jnp.concatenate([vt_ref[0, c, hd * D:(hd + 1) * D, :], ones_row], axis=0)
        cmat = c_s[hd, dr]
        both = jnp.dot(jnp.concatenate([kc, cmat.astype(BF16)], axis=0), qtc,
                       preferred_element_type=F32)
        s_w = (both[0:L] * wts).astype(BF16)
        tot = (jnp.dot(vac, s_w, preferred_element_type=F32)
               + rows[R_ISC:R_ISC + 1] * both[L:L + DA])
        den = jnp.maximum(jnp.abs(tot[D:D + 1]), rows[R_EMT:R_EMT + 1])
        h = tot[0:D] * (1.0 / den)
        hs = slice(hd * D, (hd + 1) * D)
        if second:
            out_ref[0, c, hs, :] = (h_s[hs, pl.ds(r0, L)] + h).astype(BF16)
        else:
            h_s[hs, pl.ds(r0, L)] = h
        u = (vac.astype(F32) * rows[R_W:R_W + 1]).astype(BF16)
        c_s[hd, dr] = rows[R_DEC:R_DEC + 1, 0:1] * cmat + jnp.dot(u, kc, preferred_element_type=F32)

    def step(i, carry, second):
        for hd, dr in chains:
            chain(hd, dr, chunk_of(dr, i), second)
        return carry

    assert NC % 2 == 0
    lax.fori_loop(0, NC // 2, functools.partial(step, second=False), 0)
    lax.fori_loop(NC // 2, NC, functools.partial(step, second=True), 0)


def _mlstm(mt, k_tm, gcm, heads_per_step=MLSTM_HEADS_PER_STEP):
    B, S, _ = k_tm.shape
    NC = S // CHUNK
    D = HEAD_DIM
    HB = heads_per_step
    v0 = (M_WIDTH + 3 * H_WIDTH) // (HB * D)
    W = HB * D
    NB = M_HEADS // HB
    DA = D + BF16_ROWS
    cm_blk = (1, NC, W, LANES)
    return pl.pallas_call(
        _mlstm_kernel,
        grid=(B, NB),
        in_specs=[
            pl.BlockSpec(cm_blk, lambda b, h: (b, 0, h, 0)),
            pl.BlockSpec(cm_blk, lambda b, h: (b, 0, v0 + h, 0)),
            pl.BlockSpec((1, S, W), lambda b, h: (b, 0, h)),
            pl.BlockSpec((1, HB, N_GATE // M_HEADS, S), lambda b, h: (b, h, 0, 0)),
        ],
        out_specs=pl.BlockSpec(cm_blk, lambda b, h: (b, 0, h, 0)),
        out_shape=jax.ShapeDtypeStruct((B, NC, M_WIDTH, LANES), BF16),
        scratch_shapes=[
            pltpu.VMEM((HB, 2, 8, S), F32),
            pltpu.VMEM((HB, 2, 8, S), F32),
            pltpu.VMEM((W, S), F32),
            pltpu.VMEM((HB, 2, DA, D), F32),
        ],
        compiler_params=_cparams("parallel", "parallel"),
        name="mlstm",
    )(mt, mt, k_tm, gcm)


def _filt_mlp_kernel(w1t_ref, b1_ref, w2_ref, b2_ref, w3_ref, b3_ref, fr_ref, w4f_ref, w4b_ref,
                     fbias_ref, out_ref, h3_s, *, cb_rows):
    S = h3_s.shape[2]
    hi = lax.Precision.HIGHEST
    tn_dims = (((0,), (0,)), ((), ()))
    lane = lax.broadcasted_iota(jnp.int32, (1, S), 1)
    pos_f = lane.astype(F32)
    pos_b = (S - lane).astype(F32)

    def features(pos):
        bands = (FILTER_EMB - 1) // 2
        t = pos / (S - 1)
        ang = (2.0 * math.pi) * pos / S
        fidx = lax.broadcasted_iota(jnp.int32, (bands, 1), 0).astype(F32)
        f = 1e-4 + fidx * ((bands - 1 - 1e-4) / (bands - 1))
        fa = f * ang
        w1t = w1t_ref[...]
        pre = (w1t[:, 0:1] * t
               + jnp.dot(w1t[:, 1:1 + bands], jnp.cos(fa), precision=hi, preferred_element_type=F32)
               - jnp.dot(w1t[:, 1 + bands:], jnp.sin(fa), precision=hi, preferred_element_type=F32))
        fr = fr_ref[...]
        h = jnp.sin(fr[:, 0:1] * (pre + b1_ref[...]))
        h = jnp.sin(fr[:, 1:2] * (lax.dot_general(w2_ref[...], h, tn_dims, precision=hi,
                                                  preferred_element_type=F32) + b2_ref[...]))
        return jnp.sin(fr[:, 2:3] * (lax.dot_general(w3_ref[...], h, tn_dims, precision=hi,
                                                     preferred_element_type=F32) + b3_ref[...]))

    @pl.when(pl.program_id(0) == 0)
    def _():
        h3 = features(pos_f)
        h3_s[0] = h3
        n_t = S // LANES
        anti = (lax.broadcasted_iota(jnp.int32, (LANES, LANES), 0)
                + lax.broadcasted_iota(jnp.int32, (LANES, LANES), 1) == LANES - 1).astype(F32)
        rev = jnp.concatenate(
            [jnp.dot(h3[:, (n_t - 1 - u) * LANES:(n_t - u) * LANES], anti, precision=hi,
                     preferred_element_type=F32) for u in range(n_t)], axis=1)
        h3_s[1] = pltpu.roll(rev, 1, 1)

    r = pl.program_id(0) * cb_rows + lax.broadcasted_iota(jnp.int32, (cb_rows, 1), 0)
    ch = (r % H_WIDTH).astype(F32)
    max_decay = math.log(DECAY_TARGET) / FAST_DECAY_PCT
    min_decay = math.log(DECAY_TARGET) / SLOW_DECAY_PCT
    delta = jnp.abs(min_decay + ch * ((max_decay - min_decay) / (H_WIDTH - 1)))
    n_t = S // LANES
    for half, (w_ref, pos) in enumerate(((w4f_ref, pos_f), (w4b_ref, pos_b))):
        filt = lax.dot_general(w_ref[...].astype(BF16), h3_s[half].astype(BF16), tn_dims,
                               preferred_element_type=F32)
        filt = filt * jnp.exp(-(pos / (S - 1)) * delta)
        if half == 0:
            filt = jnp.where(lane == 0, filt + fbias_ref[...], filt)
        else:
            filt = jnp.where(lane == 0, 0.0, filt)
        for u in range(n_t):
            out_ref[half * n_t + u] = filt[:, u * LANES:(u + 1) * LANES].astype(BF16)


def _filt_mlp(w1t, cols, fb_blk, fbias_blk, w2, w3, fr, w4, S, cb_rows=FILT_MLP_ROWS):
    R = w4.shape[1] // 2
    Hd = FILTER_HIDDEN
    const = lambda i: (0, 0)
    nblk = R // cb_rows
    return pl.pallas_call(
        functools.partial(_filt_mlp_kernel, cb_rows=cb_rows),
        grid=(nblk,),
        in_specs=[
            pl.BlockSpec((Hd, FILTER_EMB), const),
            pl.BlockSpec((Hd, 1), lambda i: (fb_blk, 0)),
            pl.BlockSpec((Hd, Hd), const),
            pl.BlockSpec((Hd, 1), lambda i: (fb_blk + 1, 0)),
            pl.BlockSpec((Hd, Hd), const),
            pl.BlockSpec((Hd, 1), lambda i: (fb_blk + 2, 0)),
            pl.BlockSpec((Hd, 3), const),
            pl.BlockSpec((Hd, cb_rows), lambda i: (0, i)),
            pl.BlockSpec((Hd, cb_rows), lambda i: (0, nblk + i)),
            pl.BlockSpec((cb_rows, 1), lambda i: (fbias_blk + i, 0)),
        ],
        out_specs=pl.BlockSpec((2 * S // LANES, cb_rows, LANES), lambda i: (0, i, 0)),
        out_shape=jax.ShapeDtypeStruct((2 * S // LANES, R, LANES), BF16),
        scratch_shapes=[pltpu.VMEM((2, Hd, S), F32)],
        compiler_params=_cparams("arbitrary"),
        name="filt_mlp",
    )(w1t, cols, w2, cols, w3, cols, fr, w4, w4, cols)


@functools.lru_cache(maxsize=None)
def _dft_constants(S):
    N = 2 * S
    N2 = LANES
    N1 = N // N2
    h = N1 // 2
    k1 = np.arange(N1)
    k2 = np.arange(N2)
    a1 = -2.0 * np.pi * np.outer(k1, k1) / N1
    f1r, f1i = np.cos(a1), np.sin(a1)
    at = -2.0 * np.pi * np.outer(k1, k2) / N
    twr, twi = np.cos(at), np.sin(at)
    a2 = -2.0 * np.pi * np.outer(k2, k2) / N2
    f2r, f2i = np.cos(a2), np.sin(a2)
    s1c = np.block([[f1r[:, :h], -f1i[:, :h]], [f1i[:, :h], f1r[:, :h]]])
    s1r = np.concatenate([f1r, f1i], axis=0)
    s2 = np.block([[f2r, f2i], [-f2i, f2r]])
    s2i = np.block([[f2r, -f2i], [f2i, f2r]])
    s1i = np.block([[f1r[:h, :], f1i[:h, :]], [-f1i[:h, :], f1r[:h, :]]])
    cast = lambda a: np.asarray(a, np.float32)
    return dict(s1c=cast(s1c), s1r=cast(s1r), s2=cast(s2), s2i=cast(s2i), s1i=cast(s1i),
                twr=cast(twr), twi=cast(twi), N1=N1, h=h)


def _bf(a):
    return jnp.asarray(a, F32).astype(BF16)


def _filt_fft_kernel(kern_ref, s1r_ref, s2_ref, twr_ref, twi_ref, out_ref, src_s, slab_s, *, unroll):
    N1 = twr_ref.shape[0]
    n_ch = kern_ref.shape[1]
    pitch = src_s.shape[0] // N1
    scale = 1.0 / (N1 * LANES)
    twr = twr_ref[...]
    twi = twi_ref[...]
    s1r = s1r_ref[...]

    def load_in(t1, carry):
        src_s[pl.ds(pl.multiple_of(t1 * pitch, 8), n_ch), :] = kern_ref[t1].astype(F32)
        return carry
    lax.fori_loop(0, N1, load_in, 0, unroll=4)

    def per_group(j, carry):
        for u in range(0, unroll, 2):
            cs = (j * unroll + u, j * unroll + u + 1)
            z = jnp.concatenate([src_s[pl.ds(c, N1, stride=pitch), :] for c in cs], axis=1)
            a = jnp.dot(s1r, z.astype(BF16), preferred_element_type=F32)
            for i, c in enumerate(cs):
                ar = a[0:N1, i * LANES:(i + 1) * LANES]
                ai = a[N1:2 * N1, i * LANES:(i + 1) * LANES]
                r0 = pl.multiple_of(c * N1, N1)
                slab_s[pl.ds(r0, N1), 0:LANES] = (ar * twr - ai * twi).astype(BF16)
                slab_s[pl.ds(r0, N1), LANES:2 * LANES] = (ar * twi + ai * twr).astype(BF16)
        return carry

    lax.fori_loop(0, n_ch // unroll, per_group, 0)
    x = jnp.dot(slab_s[...], s2_ref[...], preferred_element_type=F32)
    out_ref[...] = (x * scale).astype(BF16).reshape(n_ch, N1, 2 * LANES)


def _filt_fft(kern, consts, n_ch=FILT_FFT_ROWS, unroll=FILT_FFT_ROWS):
    N1, R, _ = kern.shape
    const = lambda i: (0, 0)
    pitch = n_ch + 8
    return pl.pallas_call(
        functools.partial(_filt_fft_kernel, unroll=unroll),
        grid=(R // n_ch,),
        in_specs=[
            pl.BlockSpec((N1, n_ch, LANES), lambda i: (0, i, 0)),
            pl.BlockSpec((2 * N1, N1), const),
            pl.BlockSpec((2 * LANES, 2 * LANES), const),
            pl.BlockSpec((N1, LANES), const),
            pl.BlockSpec((N1, LANES), const),
        ],
        out_specs=pl.BlockSpec((n_ch, N1, 2 * LANES), lambda i: (i, 0, 0)),
        out_shape=jax.ShapeDtypeStruct((R, N1, 2 * LANES), BF16),
        scratch_shapes=[pltpu.VMEM((N1 * pitch, LANES), F32),
                        pltpu.VMEM((n_ch * N1, 2 * LANES), BF16)],
        compiler_params=_cparams("parallel"),
        name="filt_fft",
    )(kern, _bf(consts["s1r"]), _bf(consts["s2"]),
      jnp.asarray(consts["twr"]), jnp.asarray(consts["twi"]))


def _hyena_kernel(nw_ref, v_ref, x1_ref, x2_ref, khat_ref, s1c_ref, s2_ref, s2i_ref, s1i_ref,
                  twr_ref, twi_ref, out_ref, src_s, slab_s, yf_s, q_s, y_s, *, unroll):
    h = v_ref.shape[1]
    n_ch = v_ref.shape[2]
    N1 = 2 * h
    twr = twr_ref[...]
    twi = twi_ref[...]
    s1c = s1c_ref[...]
    s1i = s1i_ref[...]
    n_groups = n_ch // unroll
    rows = unroll * N1
    pitch = src_s.shape[0] // (2 * h)

    def tile_rows(i):
        return pl.ds(pl.multiple_of(i * pitch, 8), n_ch)

    def chan_rows(c, bb):
        return pl.ds(bb * h * pitch + c, h, stride=pitch)

    def long_conv(order):
        def stage_a(g):
            for u in range(0, unroll, 2):
                cs = (g * unroll + u, g * unroll + u + 1)
                z = jnp.concatenate(
                    [jnp.concatenate([src_s[chan_rows(c, 0), :], src_s[chan_rows(c, 1), :]], axis=0)
                     for c in cs], axis=1)
                a = jnp.dot(s1c, z.astype(BF16), preferred_element_type=F32)
                for i, c in enumerate(cs):
                    ar = a[0:N1, i * LANES:(i + 1) * LANES]
                    ai = a[N1:2 * N1, i * LANES:(i + 1) * LANES]
                    r0 = c * N1 if isinstance(c, int) else pl.multiple_of(c * N1, N1)
                    slab_s[pl.ds(r0, N1), 0:LANES] = (ar * twr - ai * twi).astype(BF16)
                    slab_s[pl.ds(r0, N1), LANES:2 * LANES] = (ar * twi + ai * twr).astype(BF16)

        def stage_m1(g):
            r0 = g * rows if isinstance(g, int) else pl.multiple_of(g * rows, rows)
            c0 = g * unroll if isinstance(g, int) else pl.multiple_of(g * unroll, unroll)
            x = jnp.dot(slab_s[pl.ds(r0, rows), :], s2_ref[...], preferred_element_type=F32)
            xr = x[:, 0:LANES].reshape(unroll, N1, LANES)
            xi = x[:, LANES:2 * LANES].reshape(unroll, N1, LANES)
            kr = khat_ref[order, pl.ds(c0, unroll), :, 0:LANES].astype(F32)
            ki = khat_ref[order, pl.ds(c0, unroll), :, LANES:2 * LANES].astype(F32)
            yf_s[pl.ds(r0, rows), 0:LANES] = (xr * kr - xi * ki).reshape(rows, LANES).astype(BF16)
            yf_s[pl.ds(r0, rows), LANES:2 * LANES] = (xr * ki + xi * kr).reshape(rows, LANES).astype(BF16)

        def stage_m2(g):
            r0 = g * rows if isinstance(g, int) else pl.multiple_of(g * rows, rows)
            c0 = g * unroll if isinstance(g, int) else pl.multiple_of(g * unroll, unroll)
            p = jnp.dot(yf_s[pl.ds(r0, rows), :], s2i_ref[...], preferred_element_type=F32)
            pr = p[:, 0:LANES].reshape(unroll, N1, LANES)
            pi = p[:, LANES:2 * LANES].reshape(unroll, N1, LANES)
            q_s[pl.ds(c0, unroll), 0:N1, :] = (pr * twr + pi * twi).astype(BF16)
            q_s[pl.ds(c0, unroll), N1:2 * N1, :] = (pi * twr - pr * twi).astype(BF16)

        def stage_i(g):
            for u in range(0, unroll, 2):
                cs = (g * unroll + u, g * unroll + u + 1)
                q2 = jnp.concatenate([q_s[c] for c in cs], axis=1)
                y = jnp.dot(s1i, q2, preferred_element_type=F32)
                for i, c in enumerate(cs):
                    y_s[chan_rows(c, 0), :] = y[0:h, i * LANES:(i + 1) * LANES]
                    y_s[chan_rows(c, 1), :] = y[h:2 * h, i * LANES:(i + 1) * LANES]

        G = n_groups
        for t in range(G + 3):
            if 0 <= t - 3 < G:
                stage_i(t - 3)
            if 0 <= t - 2 < G:
                stage_m2(t - 2)
            if 0 <= t - 1 < G:
                stage_m1(t - 1)
            if t < G:
                stage_a(t)

    def per_tile(fn):
        for bb in range(2):
            def body(t1, carry, bb=bb):
                fn(bb, t1, tile_rows(bb * h + t1))
                return carry
            lax.fori_loop(0, h, body, 0, unroll=8)

    def load_in(bb, t1, rws):
        src_s[rws, :] = v_ref[bb, t1].astype(F32)
    per_tile(load_in)
    long_conv(0)

    def gate1(bb, t1, rws):
        src_s[rws, :] = y_s[rws, :] * x1_ref[bb, t1].astype(F32)
    per_tile(gate1)
    long_conv(1)

    nw = jnp.broadcast_to(nw_ref[...], (n_ch, LANES))

    def gate2_norm(bb, t1, rws):
        z = y_s[rws, :] * x2_ref[bb, t1].astype(F32)
        ms = jnp.mean(z * z, axis=0, keepdims=True)
        out_ref[bb, t1] = (z * lax.rsqrt(ms + EPS) * nw).astype(BF16)
    per_tile(gate2_norm)


def _hyena(cols, nw_blk, hy4, khat, consts, unroll=HYENA_GROUP):
    B, h, _, _ = hy4.shape
    C = H_WIDTH
    n_ch = GROUP_W
    G = C // n_ch
    g0 = M_WIDTH // n_ch
    N1 = consts["N1"]
    const = lambda g, p: (0, 0)
    blk = (2, h, n_ch, LANES)
    pitch = n_ch + 8
    return pl.pallas_call(
        functools.partial(_hyena_kernel, unroll=unroll),
        grid=(G, B // 2),
        in_specs=[
            pl.BlockSpec((n_ch, 1), lambda g, p: (nw_blk + g, 0)),
            pl.BlockSpec(blk, lambda g, p: (p, 0, g0 + g, 0)),
            pl.BlockSpec(blk, lambda g, p: (p, 0, g0 + G + g, 0)),
            pl.BlockSpec(blk, lambda g, p: (p, 0, g0 + 2 * G + g, 0)),
            pl.BlockSpec((2, n_ch, N1, 2 * LANES), lambda g, p: (0, g, 0, 0)),
            pl.BlockSpec((2 * N1, 2 * h), const),
            pl.BlockSpec((2 * LANES, 2 * LANES), const),
            pl.BlockSpec((2 * LANES, 2 * LANES), const),
            pl.BlockSpec((2 * h, 2 * N1), const),
            pl.BlockSpec((N1, LANES), const),
            pl.BlockSpec((N1, LANES), const),
        ],
        out_specs=pl.BlockSpec(blk, lambda g, p: (p, 0, g, 0)),
        out_shape=jax.ShapeDtypeStruct((B, h, C, LANES), BF16),
        scratch_shapes=[
            pltpu.VMEM((2 * h * pitch, LANES), F32),
            pltpu.VMEM((n_ch * N1, 2 * LANES), BF16),
            pltpu.VMEM((n_ch * N1, 2 * LANES), BF16),
            pltpu.VMEM((n_ch, 2 * N1, LANES), BF16),
            pltpu.VMEM((2 * h * pitch, LANES), F32),
        ],
        compiler_params=_cparams("parallel", "arbitrary"),
        name="hyena",
    )(cols, hy4, hy4, hy4, khat,
      _bf(consts["s1c"]), _bf(consts["s2"]), _bf(consts["s2i"]), _bf(consts["s1i"]),
      jnp.asarray(consts["twr"]), jnp.asarray(consts["twi"]))


def _outmlp_kernel(x_ref, hm_ref, og_ref, yh_ref, nwm_ref, wo_ref, w1_ref, w2_ref, n_post_ref,
                   n_pre_ref, n_post2_ref, out_ref):
    n_t = hm_ref.shape[1]
    nwm = jnp.broadcast_to(nwm_ref[...], (M_WIDTH, LANES))
    tiles = []
    for j in range(n_t):
        hg = hm_ref[0, j].astype(F32) * _sigmoid(og_ref[0, j].astype(F32))
        heads = []
        for hd in range(M_HEADS):
            hh = hg[hd * HEAD_DIM:(hd + 1) * HEAD_DIM]
            ms = jnp.mean(hh * hh, axis=0, keepdims=True)
            heads.append(hh * lax.rsqrt(ms + EPS))
        ym = jnp.concatenate(heads, axis=0) * nwm
        yt = jnp.concatenate([ym, yh_ref[0, j].astype(F32)], axis=0)
        tiles.append(yt.T.astype(BF16))
    y = tiles[0] if n_t == 1 else jnp.concatenate(tiles, axis=0)
    mix = jnp.dot(y, wo_ref[...], preferred_element_type=F32)
    x1 = x_ref[0] + _rms_rows(mix, n_post_ref[...])
    hm = _rms_rows(x1, n_pre_ref[...]).astype(BF16)
    mid = jnp.maximum(jnp.dot(hm, w1_ref[...], preferred_element_type=F32), 0.0)
    mid = (mid * mid).astype(BF16)
    ff = jnp.dot(mid, w2_ref[...], preferred_element_type=F32)
    out_ref[0] = x1 + _rms_rows(ff, n_post2_ref[...])


def _outmlp(x, hm, mt, yh, cols, nwm_blk, wo, w1, w2, n_post, n_pre, n_post2, tm_rows):
    B, S, D = x.shape
    n_t = tm_rows // LANES
    og0 = mt.shape[2] // M_WIDTH - 1
    const = lambda b, i: (0, 0)
    resident = functools.partial(pl.BlockSpec, index_map=const, pipeline_mode=pl.Buffered(1))
    return pl.pallas_call(
        _outmlp_kernel,
        grid=(B, S // tm_rows),
        in_specs=[
            pl.BlockSpec((1, tm_rows, D), lambda b, i: (b, i, 0)),
            pl.BlockSpec((1, n_t, M_WIDTH, LANES), lambda b, i: (b, i, 0, 0)),
            pl.BlockSpec((1, n_t, M_WIDTH, LANES), lambda b, i: (b, i, og0, 0)),
            pl.BlockSpec((1, n_t, H_WIDTH, LANES), lambda b, i: (b, i, 0, 0)),
            pl.BlockSpec((M_WIDTH, 1), lambda b, i: (nwm_blk, 0)),
            resident((D, D)),
            resident((D, D_FF)),
            resident((D_FF, D)),
            pl.BlockSpec((1, D), const),
            pl.BlockSpec((1, D), const),
            pl.BlockSpec((1, D), const),
        ],
        out_specs=pl.BlockSpec((1, tm_rows, D), lambda b, i: (b, i, 0)),
        out_shape=jax.ShapeDtypeStruct((B, S, D), F32),
        compiler_params=_cparams("parallel", "parallel"),
        name="outmlp",
    )(x, hm, mt, yh, cols, wo, w1, w2, n_post, n_pre, n_post2)


def kernel(x, norm_mix_pre, norm_mix_post, norm_mlp_pre, norm_mlp_post, w_in, b_gates,
           conv_w, conv_b, mlstm_norm_w, hyena_norm_w, filt_w1, filt_b1, filt_w2, filt_b2,
           filt_w3, filt_b3, filt_w4, filt_freq, filt_bias, w_out, w_mlp_in, w_mlp_out):
    B, S, D = x.shape
    assert D == D_MODEL and B % 2 == 0 and S % CHUNK == 0
    row = lambda a: a.astype(F32).reshape(1, -1)
    tm_rows = min(PROJ_ROWS, S)
    cols = jnp.concatenate([filt_bias.reshape(-1), mlstm_norm_w, hyena_norm_w, filt_b1, filt_b2,
                            filt_b3, b_gates]).astype(F32).reshape(-1, 1)
    off_mn = HYENA_ORDER * H_WIDTH
    off_hn = off_mn + M_WIDTH
    off_fb = off_hn + H_WIDTH
    off_bg = off_fb + 3 * FILTER_HIDDEN
    assert (off_mn % M_WIDTH == 0 and off_hn % GROUP_W == 0 and off_fb % FILTER_HIDDEN == 0
            and off_bg % N_GATE == 0 and off_mn % FILT_MLP_ROWS == 0)

    n_conv = 2 * M_WIDTH + 3 * H_WIDTH
    o_gate = n_conv + 2 * M_WIDTH
    assert w_in.shape == (D, o_gate + N_GATE)
    assert conv_w.shape == (3, n_conv)
    mt, k_tm, gcm = _proj(x, row(norm_mix_pre), w_in.astype(F32).T, cols, off_bg // N_GATE,
                          conv_w.astype(F32), row(conv_b), tm_rows)

    h_m = _mlstm(mt, k_tm, gcm)

    consts = _dft_constants(S)
    kern = _filt_mlp(filt_w1.astype(F32).T, cols, off_fb // FILTER_HIDDEN, 0, filt_w2.astype(F32),
                     filt_w3.astype(F32), filt_freq.astype(F32).T, filt_w4.astype(F32), S)
    khat = _filt_fft(kern, consts)
    khat = khat.reshape(2, H_WIDTH, consts["N1"], 2 * LANES)

    y_h = _hyena(cols, off_hn // GROUP_W, mt, khat, consts)

    return _outmlp(x, h_m, mt, y_h, cols, off_mn // M_WIDTH,
                   w_out.astype(BF16), w_mlp_in.astype(BF16), w_mlp_out.astype(BF16),
                   row(norm_mix_post), row(norm_mlp_pre), row(norm_mlp_post), min(OUTMLP_ROWS, S))
```

```python
import functools
import math

import numpy as np
import jax
import jax.numpy as jnp
from jax import lax
from jax.experimental import pallas as pl
from jax.experimental.pallas import tpu as pltpu

F32 = jnp.float32
BF16 = jnp.bfloat16

D_MODEL = 1024
M_WIDTH = 512
M_HEADS = 4
HEAD_DIM = 128
H_WIDTH = 512
H_GROUPS = 8
HYENA_ORDER = 2
GROUP_W = H_WIDTH // H_GROUPS
CHUNK = 128
FILTER_EMB = 33
FILTER_HIDDEN = 64
DECAY_TARGET = 1e-2
FAST_DECAY_PCT = 0.3
SLOW_DECAY_PCT = 1.5
D_FF = 4 * D_MODEL
N_GATE = 16
EPS = 1e-6
LANES = 128
BF16_ROWS = 16
NEG_BIG = -1e30
VMEM_LIMIT = 56 * 1024 * 1024

PROJ_ROWS = 1024
PROJ_COL_BLK = 256
MLSTM_HEADS_PER_STEP = 4
FILT_MLP_ROWS = 256
FILT_FFT_ROWS = 128
HYENA_GROUP = 8
OUTMLP_ROWS = 512


def _cparams(*sem):
    return pltpu.CompilerParams(dimension_semantics=sem, vmem_limit_bytes=VMEM_LIMIT)


def _rms_rows(xf, w):
    ms = jnp.mean(xf * xf, axis=-1, keepdims=True)
    return xf * lax.rsqrt(ms + EPS) * w


def _sigmoid(x):
    return 1.0 / (1.0 + jnp.exp(-x))


def _log_sigmoid(x):
    return jnp.minimum(x, 0.0) - jnp.log(1.0 + jnp.exp(-jnp.abs(x)))


def _proj_kernel(x_ref, xp_ref, xn_ref, nw_ref, wt_ref, bgt_ref, cw_ref, cb_ref,
                 cm_ref, k_ref, gcm_ref, wbf_s, hall_s, pc_s, *, col_blk):
    TM = x_ref.shape[1]
    HALO = xp_ref.shape[1]
    j = pl.program_id(1)

    @pl.when(jnp.logical_and(pl.program_id(0) == 0, j == 0))
    def _():
        for c0 in range(0, wbf_s.shape[1], col_blk):
            wbf_s[:, c0:c0 + col_blk] = wt_ref[c0:c0 + col_blk, :].T.astype(BF16)

    nw = nw_ref[...]
    hn = _rms_rows(x_ref[0], nw).astype(BF16)
    hall_s[0:HALO, :] = _rms_rows(xp_ref[0], nw).astype(BF16)
    hall_s[HALO:HALO + TM, :] = hn
    hall_s[HALO + TM:2 * HALO + TM, :] = _rms_rows(xn_ref[0], nw).astype(BF16)
    keep_p = jnp.where(j > 0, 1.0, 0.0)
    keep_n = jnp.where(j < pl.num_programs(1) - 1, 1.0, 0.0)
    n_cm = cm_ref.shape[2]
    n_conv = cw_ref.shape[1]
    n_blk = n_cm // col_blk
    w_k0 = M_WIDTH
    w_gate = n_cm + M_WIDTH
    n_typ = N_GATE // M_HEADS
    order = [t * M_HEADS + hd for hd in range(M_HEADS) for t in range(n_typ)]
    wg = jnp.concatenate([wt_ref[w_gate + r:w_gate + r + 1, :] for r in order], axis=0)
    bg = jnp.concatenate([bgt_ref[r:r + 1, :] for r in order], axis=0)
    nt_dims = (((1,), (1,)), ((), ()))
    gt = lax.dot_general(wg.astype(BF16), hn, nt_dims, preferred_element_type=F32) + bg
    row = lax.broadcasted_iota(jnp.int32, gt.shape, 0)
    lg = jnp.where((row % 2) == 1, _log_sigmoid(gt), gt)
    for hd in range(M_HEADS):
        gcm_ref[0, hd] = lg[hd * n_typ:(hd + 1) * n_typ]

    n_rb = TM // LANES
    n_sl = col_blk // LANES

    def project(w_row0, slot):
        wblk = wbf_s[:, pl.ds(w_row0, col_blk)]
        res = jnp.dot(hall_s[...], wblk, preferred_element_type=F32)
        for u in range(n_sl):
            ls = slice(u * LANES, (u + 1) * LANES)
            pc_s[slot, u, 0:HALO, :] = res[0:HALO, ls] * keep_p
            pc_s[slot, u, HALO:HALO + TM, :] = res[HALO:HALO + TM, ls]
            pc_s[slot, u, HALO + TM:2 * HALO + TM, :] = res[HALO + TM:2 * HALO + TM, ls] * keep_n

    def conv_tile(slot, u, rb, taps, bias):
        r = HALO + rb * LANES
        return (pc_s[slot, u, pl.ds(r - 1, LANES, stride=1), :] * taps[0:1]
                + pc_s[slot, u, r:r + LANES, :] * taps[1:2]
                + pc_s[slot, u, pl.ds(r + 1, LANES, stride=1), :] * taps[2:3] + bias)

    n_q = M_WIDTH // col_blk

    def w_row_of(c):
        return (c if c < n_q else c + n_q) * col_blk

    def emit(c, slot):
        for u in range(n_sl):
            cs = slice(c * col_blk + u * LANES, c * col_blk + (u + 1) * LANES)
            w0 = w_row_of(c) + u * LANES
            for rb in range(n_rb):
                if w0 < n_conv:
                    cv = conv_tile(slot, u, rb, cw_ref[:, w0:w0 + LANES], cb_ref[:, w0:w0 + LANES])
                else:
                    cv = pc_s[slot, u, HALO + rb * LANES:HALO + (rb + 1) * LANES, :]
                cm_ref[0, rb, cs, :] = cv.T.astype(BF16)

    k_scale = HEAD_DIM ** -0.5

    def emit_k(kb, slot):
        for u in range(n_sl):
            cs = slice(kb * col_blk + u * LANES, kb * col_blk + (u + 1) * LANES)
            for rb in range(n_rb):
                ks = slice(w_k0 + cs.start, w_k0 + cs.stop)
                cv = conv_tile(slot, u, rb, cw_ref[:, ks], cb_ref[:, ks])
                k_ref[0, rb * LANES:(rb + 1) * LANES, cs] = (cv * _sigmoid(cv) * k_scale).astype(BF16)

    cm_blocks = [(emit, c, w_row_of(c)) for c in range(n_blk)]
    jobs = ([j for j in cm_blocks if j[2] < n_conv]
            + [(emit_k, kb, w_k0 + kb * col_blk) for kb in range(M_WIDTH // col_blk)]
            + [j for j in cm_blocks if j[2] >= n_conv])
    project(jobs[0][2], 0)
    for i, (emit_fn, blk, _) in enumerate(jobs):
        emit_fn(blk, i % 2)
        if i + 1 < len(jobs):
            project(jobs[i + 1][2], (i + 1) % 2)


def _proj(x, nw, w_in_t, cols, bg_blk, cw, cb, tm_rows, col_blk=PROJ_COL_BLK):
    B, S, D = x.shape
    HALO = BF16_ROWS
    n_conv = cw.shape[1]
    n_cm = w_in_t.shape[0] - N_GATE - M_WIDTH
    n_t = tm_rows // LANES
    hb = tm_rows // HALO
    last = S // HALO - 1
    const = lambda b, j: (0, 0)
    resident = functools.partial(pl.BlockSpec, index_map=const, pipeline_mode=pl.Buffered(1))
    return pl.pallas_call(
        functools.partial(_proj_kernel, col_blk=col_blk),
        grid=(B, S // tm_rows),
        in_specs=[
            pl.BlockSpec((1, tm_rows, D), lambda b, j: (b, j, 0)),
            pl.BlockSpec((1, HALO, D), lambda b, j: (b, jnp.maximum(j * hb - 1, 0), 0)),
            pl.BlockSpec((1, HALO, D), lambda b, j: (b, jnp.minimum((j + 1) * hb, last), 0)),
            pl.BlockSpec((1, D), const),
            resident(w_in_t.shape),
            pl.BlockSpec((N_GATE, 1), lambda b, j: (bg_blk, 0)),
            pl.BlockSpec((3, n_conv), const),
            pl.BlockSpec((1, n_conv), const),
        ],
        out_specs=[
            pl.BlockSpec((1, n_t, n_cm, LANES), lambda b, j: (b, j, 0, 0)),
            pl.BlockSpec((1, tm_rows, M_WIDTH), lambda b, j: (b, j, 0)),
            pl.BlockSpec((1, M_HEADS, N_GATE // M_HEADS, tm_rows), lambda b, j: (b, 0, 0, j)),
        ],
        out_shape=[
            jax.ShapeDtypeStruct((B, S // LANES, n_cm, LANES), BF16),
            jax.ShapeDtypeStruct((B, S, M_WIDTH), BF16),
            jax.ShapeDtypeStruct((B, M_HEADS, N_GATE // M_HEADS, S), F32),
        ],
        scratch_shapes=[pltpu.VMEM((D, n_cm + M_WIDTH), BF16),
                        pltpu.VMEM((tm_rows + 2 * HALO, D), BF16),
                        pltpu.VMEM((2, col_blk // LANES, tm_rows + 2 * HALO, LANES), F32)],
        compiler_params=_cparams("arbitrary", "arbitrary"),
        name="proj",
    )(x, x, x, nw, w_in_t, cols, cw, cb)


def _mlstm_kernel(qt_ref, vt_ref, k_ref, g_ref, out_ref, rows_s, stab_s, h_s, c_s):
    S = k_ref.shape[1]
    HB = g_ref.shape[1]
    L = CHUNK
    NC = S // L
    D = HEAD_DIM
    DA = D + BF16_ROWS
    W = HB * D

    R_A, R_MX, R_W, R_DEC, R_ISC, R_EMT, R_GT, R_AMAX = range(8)

    def scan(x, op, fill):
        lane = lax.broadcasted_iota(jnp.int32, x.shape, 1) % L
        pre = x
        suf = x
        d = 1
        while d < L:
            pre = op(pre, jnp.where(lane >= d, pltpu.roll(pre, d, 1), fill))
            suf = op(suf, jnp.where(lane < L - d, pltpu.roll(suf, S - d, 1), fill))
            d *= 2
        return pre, suf

    chains = [(hd, dr) for hd in range(HB) for dr in range(2)]
    assert HB % 2 == 0 and len(chains) <= 8
    a_rows = {}
    for h0 in range(0, HB, 2):
        g = jnp.concatenate([g_ref[0, h0, 0:4, :], g_ref[0, h0 + 1, 0:4, :]], axis=0)
        pre, suf = scan(g, jnp.add, 0.0)
        tot = pre + suf - g
        for j, hd in enumerate((h0, h0 + 1)):
            o = 4 * j
            b_rows = (pre[o + 1:o + 2], suf[o + 3:o + 4])
            for dr in range(2):
                a_rows[hd, dr] = g[o + 2 * dr:o + 2 * dr + 1] - b_rows[dr]
                rows_s[hd, dr, R_A:R_A + 1, :] = a_rows[hd, dr]
                rows_s[hd, dr, R_EMT:R_EMT + 1, :] = b_rows[dr]
                rows_s[hd, dr, R_GT:R_GT + 1, :] = tot[o + 2 * dr + 1:o + 2 * dr + 2]
    a8 = jnp.concatenate([a_rows[ch] for ch in chains]
                         + [jnp.zeros((8 - len(chains), S), F32)] * (len(chains) < 8), axis=0)
    pmax, smax = scan(a8, jnp.maximum, NEG_BIG)
    for i, (hd, dr) in enumerate(chains):
        rows_s[hd, dr, R_MX:R_MX + 1, :] = pmax[i:i + 1] if dr == 0 else smax[i:i + 1]
        rows_s[hd, dr, R_AMAX:R_AMAX + 1, :] = jnp.maximum(pmax[i:i + 1], smax[i:i + 1])

    def chunk_of(dr, i):
        return i if dr == 0 else NC - 1 - i

    def stabilisers(i, ms):
        out = []
        for (hd, dr), m in zip(chains, ms):
            r0 = pl.multiple_of(chunk_of(dr, i) * L, L)
            gt = rows_s[hd, dr, R_GT:R_GT + 1, pl.ds(r0, L)]
            m_end = jnp.maximum(gt + m, gt + rows_s[hd, dr, R_AMAX:R_AMAX + 1, pl.ds(r0, L)])
            stab_s[hd, dr, 0:1, pl.ds(r0, L)] = m
            stab_s[hd, dr, 1:2, pl.ds(r0, L)] = m_end
            out.append(m_end)
        return tuple(out)

    lax.fori_loop(0, NC, stabilisers, tuple(jnp.zeros((1, L), F32) for _ in chains), unroll=4)

    for hd, dr in chains:
        a = rows_s[hd, dr, R_A:R_A + 1, :]
        gt = rows_s[hd, dr, R_GT:R_GT + 1, :]
        m = stab_s[hd, dr, 0:1, :]
        m_end = stab_s[hd, dr, 1:2, :]
        mx = jnp.maximum(m, rows_s[hd, dr, R_MX:R_MX + 1, :])
        b = rows_s[hd, dr, R_EMT:R_EMT + 1, :]
        rows_s[hd, dr, R_MX:R_MX + 1, :] = mx
        rows_s[hd, dr, R_W:R_W + 1, :] = jnp.exp(gt + a - m_end)
        rows_s[hd, dr, R_DEC:R_DEC + 1, :] = jnp.exp(gt + m - m_end)
        rows_s[hd, dr, R_ISC:R_ISC + 1, :] = jnp.exp(m - mx)
        rows_s[hd, dr, R_EMT:R_EMT + 1, :] = jnp.exp(-(b + mx))

    ones_row = (lax.broadcasted_iota(jnp.int32, (BF16_ROWS, L), 0) == 0).astype(BF16)

    c_s[...] = jnp.zeros_like(c_s)

    si = lax.broadcasted_iota(jnp.int32, (L, L), 0)
    ti = lax.broadcasted_iota(jnp.int32, (L, L), 1)
    eye = si == ti
    valid = (si <= ti, si >= ti)

    def chain(hd, dr, c, second):
        r0 = pl.multiple_of(c * L, L)
        rows = rows_s[hd, dr, :, pl.ds(r0, L)]
        a_col = jnp.sum(jnp.where(eye, rows[R_A:R_A + 1], 0.0), axis=1, keepdims=True)
        wts = jnp.exp(jnp.where(valid[dr], a_col - rows[R_MX:R_MX + 1], NEG_BIG))
        kc = k_ref[0, pl.ds(r0, L), hd * D:(hd + 1) * D]
        qpre = qt_ref[0, c, hd * D:(hd + 1) * D, :].astype(F32)
        qtc = (qpre * _sigmoid(qpre)).astype(BF16)
        vac = jnp.concatenate([vt_ref[0, c, hd * D:(hd + 1) * D, :], ones_row], axis=0)
        cmat = c_s[hd, dr]
        both = jnp.dot(jnp.concatenate([kc, cmat.astype(BF16)], axis=0), qtc,
                       preferred_element_type=F32)
        s_w = (both[0:L] * wts).astype(BF16)
        tot = (jnp.dot(vac, s_w, preferred_element_type=F32)
               + rows[R_ISC:R_ISC + 1] * both[L:L + DA])
        den = jnp.maximum(jnp.abs(tot[D:D + 1]), rows[R_EMT:R_EMT + 1])
        h = tot[0:D] * (1.0 / den)
        hs = slice(hd * D, (hd + 1) * D)
        if second:
            out_ref[0, c, hs, :] = (h_s[hs, pl.ds(r0, L)] + h).astype(BF16)
        else:
            h_s[hs, pl.ds(r0, L)] = h
        u = (vac.astype(F32) * rows[R_W:R_W + 1]).astype(BF16)
        c_s[hd, dr] = rows[R_DEC:R_DEC + 1, 0:1] * cmat + jnp.dot(u, kc, preferred_element_type=F32)

    def step(i, carry, second):
        for hd, dr in chains:
            chain(hd, dr, chunk_of(dr, i), second)
        return carry

    assert NC % 2 == 0
    lax.fori_loop(0, NC // 2, functools.partial(step, second=False), 0)
    lax.fori_loop(NC // 2, NC, functools.partial(step, second=True), 0)


def _mlstm(mt, k_tm, gcm, heads_per_step=MLSTM_HEADS_PER_STEP):
    B, S, _ = k_tm.shape
    NC = S // CHUNK
    D = HEAD_DIM
    HB = heads_per_step
    v0 = (M_WIDTH + 3 * H_WIDTH) // (HB * D)
    W = HB * D
    NB = M_HEADS // HB
    DA = D + BF16_ROWS
    cm_blk = (1, NC, W, LANES)
    return pl.pallas_call(
        _mlstm_kernel,
        grid=(B, NB),
        in_specs=[
            pl.BlockSpec(cm_blk, lambda b, h: (b, 0, h, 0)),
            pl.BlockSpec(cm_blk, lambda b, h: (b, 0, v0 + h, 0)),
            pl.BlockSpec((1, S, W), lambda b, h: (b, 0, h)),
            pl.BlockSpec((1, HB, N_GATE // M_HEADS, S), lambda b, h: (b, h, 0, 0)),
        ],
        out_specs=pl.BlockSpec(cm_blk, lambda b, h: (b, 0, h, 0)),
        out_shape=jax.ShapeDtypeStruct((B, NC, M_WIDTH, LANES), BF16),
        scratch_shapes=[
            pltpu.VMEM((HB, 2, 8, S), F32),
            pltpu.VMEM((HB, 2, 8, S), F32),
            pltpu.VMEM((W, S), F32),
            pltpu.VMEM((HB, 2, DA, D), F32),
        ],
        compiler_params=_cparams("parallel", "parallel"),
        name="mlstm",
    )(mt, mt, k_tm, gcm)


def _filt_mlp_kernel(w1t_ref, b1_ref, w2_ref, b2_ref, w3_ref, b3_ref, fr_ref, w4f_ref, w4b_ref,
                     fbias_ref, out_ref, h3_s, *, cb_rows):
    S = h3_s.shape[2]
    hi = lax.Precision.HIGHEST
    tn_dims = (((0,), (0,)), ((), ()))
    lane = lax.broadcasted_iota(jnp.int32, (1, S), 1)
    pos_f = lane.astype(F32)
    pos_b = (S - lane).astype(F32)

    def features(pos):
        bands = (FILTER_EMB - 1) // 2
        t = pos / (S - 1)
        ang = (2.0 * math.pi) * pos / S
        fidx = lax.broadcasted_iota(jnp.int32, (bands, 1), 0).astype(F32)
        f = 1e-4 + fidx * ((bands - 1 - 1e-4) / (bands - 1))
        fa = f * ang
        w1t = w1t_ref[...]
        pre = (w1t[:, 0:1] * t
               + jnp.dot(w1t[:, 1:1 + bands], jnp.cos(fa), precision=hi, preferred_element_type=F32)
               - jnp.dot(w1t[:, 1 + bands:], jnp.sin(fa), precision=hi, preferred_element_type=F32))
        fr = fr_ref[...]
        h = jnp.sin(fr[:, 0:1] * (pre + b1_ref[...]))
        h = jnp.sin(fr[:, 1:2] * (lax.dot_general(w2_ref[...], h, tn_dims, precision=hi,
                                                  preferred_element_type=F32) + b2_ref[...]))
        return jnp.sin(fr[:, 2:3] * (lax.dot_general(w3_ref[...], h, tn_dims, precision=hi,
                                                     preferred_element_type=F32) + b3_ref[...]))

    @pl.when(pl.program_id(0) == 0)
    def _():
        h3 = features(pos_f)
        h3_s[0] = h3
        n_t = S // LANES
        anti = (lax.broadcasted_iota(jnp.int32, (LANES, LANES), 0)
                + lax.broadcasted_iota(jnp.int32, (LANES, LANES), 1) == LANES - 1).astype(F32)
        rev = jnp.concatenate(
            [jnp.dot(h3[:, (n_t - 1 - u) * LANES:(n_t - u) * LANES], anti, precision=hi,
                     preferred_element_type=F32) for u in range(n_t)], axis=1)
        h3_s[1] = pltpu.roll(rev, 1, 1)

    r = pl.program_id(0) * cb_rows + lax.broadcasted_iota(jnp.int32, (cb_rows, 1), 0)
    ch = (r % H_WIDTH).astype(F32)
    max_decay = math.log(DECAY_TARGET) / FAST_DECAY_PCT
    min_decay = math.log(DECAY_TARGET) / SLOW_DECAY_PCT
    delta = jnp.abs(min_decay + ch * ((max_decay - min_decay) / (H_WIDTH - 1)))
    n_t = S // LANES
    for half, (w_ref, pos) in enumerate(((w4f_ref, pos_f), (w4b_ref, pos_b))):
        filt = lax.dot_general(w_ref[...].astype(BF16), h3_s[half].astype(BF16), tn_dims,
                               preferred_element_type=F32)
        filt = filt * jnp.exp(-(pos / (S - 1)) * delta)
        if half == 0:
            filt = jnp.where(lane == 0, filt + fbias_ref[...], filt)
        else:
            filt = jnp.where(lane == 0, 0.0, filt)
        for u in range(n_t):
            out_ref[half * n_t + u] = filt[:, u * LANES:(u + 1) * LANES].astype(BF16)


def _filt_mlp(w1t, cols, fb_blk, fbias_blk, w2, w3, fr, w4, S, cb_rows=FILT_MLP_ROWS):
    R = w4.shape[1] // 2
    Hd = FILTER_HIDDEN
    const = lambda i: (0, 0)
    nblk = R // cb_rows
    return pl.pallas_call(
        functools.partial(_filt_mlp_kernel, cb_rows=cb_rows),
        grid=(nblk,),
        in_specs=[
            pl.BlockSpec((Hd, FILTER_EMB), const),
            pl.BlockSpec((Hd, 1), lambda i: (fb_blk, 0)),
            pl.BlockSpec((Hd, Hd), const),
            pl.BlockSpec((Hd, 1), lambda i: (fb_blk + 1, 0)),
            pl.BlockSpec((Hd, Hd), const),
            pl.BlockSpec((Hd, 1), lambda i: (fb_blk + 2, 0)),
            pl.BlockSpec((Hd, 3), const),
            pl.BlockSpec((Hd, cb_rows), lambda i: (0, i)),
            pl.BlockSpec((Hd, cb_rows), lambda i: (0, nblk + i)),
            pl.BlockSpec((cb_rows, 1), lambda i: (fbias_blk + i, 0)),
        ],
        out_specs=pl.BlockSpec((2 * S // LANES, cb_rows, LANES), lambda i: (0, i, 0)),
        out_shape=jax.ShapeDtypeStruct((2 * S // LANES, R, LANES), BF16),
        scratch_shapes=[pltpu.VMEM((2, Hd, S), F32)],
        compiler_params=_cparams("arbitrary"),
        name="filt_mlp",
    )(w1t, cols, w2, cols, w3, cols, fr, w4, w4, cols)


@functools.lru_cache(maxsize=None)
def _dft_constants(S):
    N = 2 * S
    N2 = LANES
    N1 = N // N2
    h = N1 // 2
    k1 = np.arange(N1)
    k2 = np.arange(N2)
    a1 = -2.0 * np.pi * np.outer(k1, k1) / N1
    f1r, f1i = np.cos(a1), np.sin(a1)
    at = -2.0 * np.pi * np.outer(k1, k2) / N
    twr, twi = np.cos(at), np.sin(at)
    a2 = -2.0 * np.pi * np.outer(k2, k2) / N2
    f2r, f2i = np.cos(a2), np.sin(a2)
    s1c = np.block([[f1r[:, :h], -f1i[:, :h]], [f1i[:, :h], f1r[:, :h]]])
    s1r = np.concatenate([f1r, f1i], axis=0)
    s2 = np.block([[f2r, f2i], [-f2i, f2r]])
    s2i = np.block([[f2r, -f2i], [f2i, f2r]])
    s1i = np.block([[f1r[:h, :], f1i[:h, :]], [-f1i[:h, :], f1r[:h, :]]])
    cast = lambda a: np.asarray(a, np.float32)
    return dict(s1c=cast(s1c), s1r=cast(s1r), s2=cast(s2), s2i=cast(s2i), s1i=cast(s1i),
                twr=cast(twr), twi=cast(twi), N1=N1, h=h)


def _bf(a):
    return jnp.asarray(a, F32).astype(BF16)


def _filt_fft_kernel(kern_ref, s1r_ref, s2_ref, twr_ref, twi_ref, out_ref, src_s, slab_s, *, unroll):
    N1 = twr_ref.shape[0]
    n_ch = kern_ref.shape[1]
    pitch = src_s.shape[0] // N1
    scale = 1.0 / (N1 * LANES)
    twr = twr_ref[...]
    twi = twi_ref[...]
    s1r = s1r_ref[...]

    def load_in(t1, carry):
        src_s[pl.ds(pl.multiple_of(t1 * pitch, 8), n_ch), :] = kern_ref[t1].astype(F32)
        return carry
    lax.fori_loop(0, N1, load_in, 0, unroll=4)

    def per_group(j, carry):
        for u in range(0, unroll, 2):
            cs = (j * unroll + u, j * unroll + u + 1)
            z = jnp.concatenate([src_s[pl.ds(c, N1, stride=pitch), :] for c in cs], axis=1)
            a = jnp.dot(s1r, z.astype(BF16), preferred_element_type=F32)
            for i, c in enumerate(cs):
                ar = a[0:N1, i * LANES:(i + 1) * LANES]
                ai = a[N1:2 * N1, i * LANES:(i + 1) * LANES]
                r0 = pl.multiple_of(c * N1, N1)
                slab_s[pl.ds(r0, N1), 0:LANES] = (ar * twr - ai * twi).astype(BF16)
                slab_s[pl.ds(r0, N1), LANES:2 * LANES] = (ar * twi + ai * twr).astype(BF16)
        return carry

    lax.fori_loop(0, n_ch // unroll, per_group, 0)
    x = jnp.dot(slab_s[...], s2_ref[...], preferred_element_type=F32)
    out_ref[...] = (x * scale).astype(BF16).reshape(n_ch, N1, 2 * LANES)


def _filt_fft(kern, consts, n_ch=FILT_FFT_ROWS, unroll=FILT_FFT_ROWS):
    N1, R, _ = kern.shape
    const = lambda i: (0, 0)
    pitch = n_ch + 8
    return pl.pallas_call(
        functools.partial(_filt_fft_kernel, unroll=unroll),
        grid=(R // n_ch,),
        in_specs=[
            pl.BlockSpec((N1, n_ch, LANES), lambda i: (0, i, 0)),
            pl.BlockSpec((2 * N1, N1), const),
            pl.BlockSpec((2 * LANES, 2 * LANES), const),
            pl.BlockSpec((N1, LANES), const),
            pl.BlockSpec((N1, LANES), const),
        ],
        out_specs=pl.BlockSpec((n_ch, N1, 2 * LANES), lambda i: (i, 0, 0)),
        out_shape=jax.ShapeDtypeStruct((R, N1, 2 * LANES), BF16),
        scratch_shapes=[pltpu.VMEM((N1 * pitch, LANES), F32),
                        pltpu.VMEM((n_ch * N1, 2 * LANES), BF16)],
        compiler_params=_cparams("parallel"),
        name="filt_fft",
    )(kern, _bf(consts["s1r"]), _bf(consts["s2"]),
      jnp.asarray(consts["twr"]), jnp.asarray(consts["twi"]))


def _hyena_kernel(nw_ref, v_ref, x1_ref, x2_ref, khat_ref, s1c_ref, s2_ref, s2i_ref, s1i_ref,
                  twr_ref, twi_ref, out_ref, src_s, slab_s, yf_s, q_s, y_s, *, unroll):
    h = v_ref.shape[1]
    n_ch = v_ref.shape[2]
    N1 = 2 * h
    twr = twr_ref[...]
    twi = twi_ref[...]
    s1c = s1c_ref[...]
    s1i = s1i_ref[...]
    n_groups = n_ch // unroll
    rows = unroll * N1
    pitch = src_s.shape[0] // (2 * h)

    def tile_rows(i):
        return pl.ds(pl.multiple_of(i * pitch, 8), n_ch)

    def chan_rows(c, bb):
        return pl.ds(bb * h * pitch + c, h, stride=pitch)

    def long_conv(order):
        def stage_a(g):
            for u in range(0, unroll, 2):
                cs = (g * unroll + u, g * unroll + u + 1)
                z = jnp.concatenate(
                    [jnp.concatenate([src_s[chan_rows(c, 0), :], src_s[chan_rows(c, 1), :]], axis=0)
                     for c in cs], axis=1)
                a = jnp.dot(s1c, z.astype(BF16), preferred_element_type=F32)
                for i, c in enumerate(cs):
                    ar = a[0:N1, i * LANES:(i + 1) * LANES]
                    ai = a[N1:2 * N1, i * LANES:(i + 1) * LANES]
                    r0 = c * N1 if isinstance(c, int) else pl.multiple_of(c * N1, N1)
                    slab_s[pl.ds(r0, N1), 0:LANES] = (ar * twr - ai * twi).astype(BF16)
                    slab_s[pl.ds(r0, N1), LANES:2 * LANES] = (ar * twi + ai * twr).astype(BF16)

        def stage_m1(g):
            r0 = g * rows if isinstance(g, int) else pl.multiple_of(g * rows, rows)
            c0 = g * unroll if isinstance(g, int) else pl.multiple_of(g * unroll, unroll)
            x = jnp.dot(slab_s[pl.ds(r0, rows), :], s2_ref[...], preferred_element_type=F32)
            xr = x[:, 0:LANES].reshape(unroll, N1, LANES)
            xi = x[:, LANES:2 * LANES].reshape(unroll, N1, LANES)
            kr = khat_ref[order, pl.ds(c0, unroll), :, 0:LANES].astype(F32)
            ki = khat_ref[order, pl.ds(c0, unroll), :, LANES:2 * LANES].astype(F32)
            yf_s[pl.ds(r0, rows), 0:LANES] = (xr * kr - xi * ki).reshape(rows, LANES).astype(BF16)
            yf_s[pl.ds(r0, rows), LANES:2 * LANES] = (xr * ki + xi * kr).reshape(rows, LANES).astype(BF16)

        def stage_m2(g):
            r0 = g * rows if isinstance(g, int) else pl.multiple_of(g * rows, rows)
            c0 = g * unroll if isinstance(g, int) else pl.multiple_of(g * unroll, unroll)
            p = jnp.dot(yf_s[pl.ds(r0, rows), :], s2i_ref[...], preferred_element_type=F32)
            pr = p[:, 0:LANES].reshape(unroll, N1, LANES)
            pi = p[:, LANES:2 * LANES].reshape(unroll, N1, LANES)
            q_s[pl.ds(c0, unroll), 0:N1, :] = (pr * twr + pi * twi).astype(BF16)
            q_s[pl.ds(c0, unroll), N1:2 * N1, :] = (pi * twr - pr * twi).astype(BF16)

        def stage_i(g):
            for u in range(0, unroll, 2):
                cs = (g * unroll + u, g * unroll + u + 1)
                q2 = jnp.concatenate([q_s[c] for c in cs], axis=1)
                y = jnp.dot(s1i, q2, preferred_element_type=F32)
                for i, c in enumerate(cs):
                    y_s[chan_rows(c, 0), :] = y[0:h, i * LANES:(i + 1) * LANES]
                    y_s[chan_rows(c, 1), :] = y[h:2 * h, i * LANES:(i + 1) * LANES]

        G = n_groups
        for t in range(G + 3):
            if 0 <= t - 3 < G:
                stage_i(t - 3)
            if 0 <= t - 2 < G:
                stage_m2(t - 2)
            if 0 <= t - 1 < G:
                stage_m1(t - 1)
            if t < G:
                stage_a(t)

    def per_tile(fn):
        for bb in range(2):
            def body(t1, carry, bb=bb):
                fn(bb, t1, tile_rows(bb * h + t1))
                return carry
            lax.fori_loop(0, h, body, 0, unroll=8)

    def load_in(bb, t1, rws):
        src_s[rws, :] = v_ref[bb, t1].astype(F32)
    per_tile(load_in)
    long_conv(0)

    def gate1(bb, t1, rws):
        src_s[rws, :] = y_s[rws, :] * x1_ref[bb, t1].astype(F32)
    per_tile(gate1)
    long_conv(1)

    nw = jnp.broadcast_to(nw_ref[...], (n_ch, LANES))

    def gate2_norm(bb, t1, rws):
        z = y_s[rws, :] * x2_ref[bb, t1].astype(F32)
        ms = jnp.mean(z * z, axis=0, keepdims=True)
        out_ref[bb, t1] = (z * lax.rsqrt(ms + EPS) * nw).astype(BF16)
    per_tile(gate2_norm)


def _hyena(cols, nw_blk, hy4, khat, consts, unroll=HYENA_GROUP):
    B, h, _, _ = hy4.shape
    C = H_WIDTH
    n_ch = GROUP_W
    G = C // n_ch
    g0 = M_WIDTH // n_ch
    N1 = consts["N1"]
    const = lambda g, p: (0, 0)
    blk = (2, h, n_ch, LANES)
    pitch = n_ch + 8
    return pl.pallas_call(
        functools.partial(_hyena_kernel, unroll=unroll),
        grid=(G, B // 2),
        in_specs=[
            pl.BlockSpec((n_ch, 1), lambda g, p: (nw_blk + g, 0)),
            pl.BlockSpec(blk, lambda g, p: (p, 0, g0 + g, 0)),
            pl.BlockSpec(blk, lambda g, p: (p, 0, g0 + G + g, 0)),
            pl.BlockSpec(blk, lambda g, p: (p, 0, g0 + 2 * G + g, 0)),
            pl.BlockSpec((2, n_ch, N1, 2 * LANES), lambda g, p: (0, g, 0, 0)),
            pl.BlockSpec((2 * N1, 2 * h), const),
            pl.BlockSpec((2 * LANES, 2 * LANES), const),
            pl.BlockSpec((2 * LANES, 2 * LANES), const),
            pl.BlockSpec((2 * h, 2 * N1), const),
            pl.BlockSpec((N1, LANES), const),
            pl.BlockSpec((N1, LANES), const),
        ],
        out_specs=pl.BlockSpec(blk, lambda g, p: (p, 0, g, 0)),
        out_shape=jax.ShapeDtypeStruct((B, h, C, LANES), BF16),
        scratch_shapes=[
            pltpu.VMEM((2 * h * pitch, LANES), F32),
            pltpu.VMEM((n_ch * N1, 2 * LANES), BF16),
            pltpu.VMEM((n_ch * N1, 2 * LANES), BF16),
            pltpu.VMEM((n_ch, 2 * N1, LANES), BF16),
            pltpu.VMEM((2 * h * pitch, LANES), F32),
        ],
        compiler_params=_cparams("parallel", "arbitrary"),
        name="hyena",
    )(cols, hy4, hy4, hy4, khat,
      _bf(consts["s1c"]), _bf(consts["s2"]), _bf(consts["s2i"]), _bf(consts["s1i"]),
      jnp.asarray(consts["twr"]), jnp.asarray(consts["twi"]))


def _outmlp_kernel(x_ref, hm_ref, og_ref, yh_ref, nwm_ref, wo_ref, w1_ref, w2_ref, n_post_ref,
                   n_pre_ref, n_post2_ref, out_ref):
    n_t = hm_ref.shape[1]
    nwm = jnp.broadcast_to(nwm_ref[...], (M_WIDTH, LANES))
    tiles = []
    for j in range(n_t):
        hg = hm_ref[0, j].astype(F32) * _sigmoid(og_ref[0, j].astype(F32))
        heads = []
        for hd in range(M_HEADS):
            hh = hg[hd * HEAD_DIM:(hd + 1) * HEAD_DIM]
            ms = jnp.mean(hh * hh, axis=0, keepdims=True)
            heads.append(hh * lax.rsqrt(ms + EPS))
        ym = jnp.concatenate(heads, axis=0) * nwm
        yt = jnp.concatenate([ym, yh_ref[0, j].astype(F32)], axis=0)
        tiles.append(yt.T.astype(BF16))
    y = tiles[0] if n_t == 1 else jnp.concatenate(tiles, axis=0)
    mix = jnp.dot(y, wo_ref[...], preferred_element_type=F32)
    x1 = x_ref[0] + _rms_rows(mix, n_post_ref[...])
    hm = _rms_rows(x1, n_pre_ref[...]).astype(BF16)
    mid = jnp.maximum(jnp.dot(hm, w1_ref[...], preferred_element_type=F32), 0.0)
    mid = (mid * mid).astype(BF16)
    ff = jnp.dot(mid, w2_ref[...], preferred_element_type=F32)
    out_ref[0] = x1 + _rms_rows(ff, n_post2_ref[...])


def _outmlp(x, hm, mt, yh, cols, nwm_blk, wo, w1, w2, n_post, n_pre, n_post2, tm_rows):
    B, S, D = x.shape
    n_t = tm_rows // LANES
    og0 = mt.shape[2] // M_WIDTH - 1
    const = lambda b, i: (0, 0)
    resident = functools.partial(pl.BlockSpec, index_map=const, pipeline_mode=pl.Buffered(1))
    return pl.pallas_call(
        _outmlp_kernel,
        grid=(B, S // tm_rows),
        in_specs=[
            pl.BlockSpec((1, tm_rows, D), lambda b, i: (b, i, 0)),
            pl.BlockSpec((1, n_t, M_WIDTH, LANES), lambda b, i: (b, i, 0, 0)),
            pl.BlockSpec((1, n_t, M_WIDTH, LANES), lambda b, i: (b, i, og0, 0)),
            pl.BlockSpec((1, n_t, H_WIDTH, LANES), lambda b, i: (b, i, 0, 0)),
            pl.BlockSpec((M_WIDTH, 1), lambda b, i: (nwm_blk, 0)),
            resident((D, D)),
            resident((D, D_FF)),
            resident((D_FF, D)),
            pl.BlockSpec((1, D), const),
            pl.BlockSpec((1, D), const),
            pl.BlockSpec((1, D), const),
        ],
        out_specs=pl.BlockSpec((1, tm_rows, D), lambda b, i: (b, i, 0)),
        out_shape=jax.ShapeDtypeStruct((B, S, D), F32),
        compiler_params=_cparams("parallel", "parallel"),
        name="outmlp",
    )(x, hm, mt, yh, cols, wo, w1, w2, n_post, n_pre, n_post2)


def kernel(x, norm_mix_pre, norm_mix_post, norm_mlp_pre, norm_mlp_post, w_in, b_gates,
           conv_w, conv_b, mlstm_norm_w, hyena_norm_w, filt_w1, filt_b1, filt_w2, filt_b2,
           filt_w3, filt_b3, filt_w4, filt_freq, filt_bias, w_out, w_mlp_in, w_mlp_out):
    B, S, D = x.shape
    assert D == D_MODEL and B % 2 == 0 and S % CHUNK == 0
    row = lambda a: a.astype(F32).reshape(1, -1)
    tm_rows = min(PROJ_ROWS, S)
    cols = jnp.concatenate([filt_bias.reshape(-1), mlstm_norm_w, hyena_norm_w, filt_b1, filt_b2,
                            filt_b3, b_gates]).astype(F32).reshape(-1, 1)
    off_mn = HYENA_ORDER * H_WIDTH
    off_hn = off_mn + M_WIDTH
    off_fb = off_hn + H_WIDTH
    off_bg = off_fb + 3 * FILTER_HIDDEN
    assert (off_mn % M_WIDTH == 0 and off_hn % GROUP_W == 0 and off_fb % FILTER_HIDDEN == 0
            and off_bg % N_GATE == 0 and off_mn % FILT_MLP_ROWS == 0)

    n_conv = 2 * M_WIDTH + 3 * H_WIDTH
    o_gate = n_conv + 2 * M_WIDTH
    assert w_in.shape == (D, o_gate + N_GATE)
    assert conv_w.shape == (3, n_conv)
    mt, k_tm, gcm = _proj(x, row(norm_mix_pre), w_in.astype(F32).T, cols, off_bg // N_GATE,
                          conv_w.astype(F32), row(conv_b), tm_rows)

    h_m = _mlstm(mt, k_tm, gcm)

    consts = _dft_constants(S)
    kern = _filt_mlp(filt_w1.astype(F32).T, cols, off_fb // FILTER_HIDDEN, 0, filt_w2.astype(F32),
                     filt_w3.astype(F32), filt_freq.astype(F32).T, filt_w4.astype(F32), S)
    khat = _filt_fft(kern, consts)
    khat = khat.reshape(2, H_WIDTH, consts["N1"], 2 * LANES)

    y_h = _hyena(cols, off_hn // GROUP_W, mt, khat, consts)

    return _outmlp(x, h_m, mt, y_h, cols, off_mn // M_WIDTH,
                   w_out.astype(BF16), w_mlp_in.astype(BF16), w_mlp_out.astype(BF16),
                   row(norm_mix_post), row(norm_mlp_pre), row(norm_mlp_post), min(OUTMLP_ROWS, S))
```

```python
import functools
import math

import numpy as np
import jax
import jax.numpy as jnp
from jax import lax
from jax.experimental import pallas as pl
from jax.experimental.pallas import tpu as pltpu

F32 = jnp.float32
BF16 = jnp.bfloat16

D_MODEL = 1024
M_WIDTH = 512
M_HEADS = 4
HEAD_DIM = 128
H_WIDTH = 512
H_GROUPS = 8
HYENA_ORDER = 2
GROUP_W = H_WIDTH // H_GROUPS
CHUNK = 128
FILTER_EMB = 33
FILTER_HIDDEN = 64
DECAY_TARGET = 1e-2
FAST_DECAY_PCT = 0.3
SLOW_DECAY_PCT = 1.5
D_FF = 4 * D_MODEL
N_GATE = 16
EPS = 1e-6
LANES = 128
BF16_ROWS = 16
NEG_BIG = -1e30
VMEM_LIMIT = 56 * 1024 * 1024

PROJ_ROWS = 1024
PROJ_COL_BLK = 256
MLSTM_HEADS_PER_STEP = 4
FILT_MLP_ROWS = 256
FILT_FFT_ROWS = 128
HYENA_GROUP = 8
OUTMLP_ROWS = 512


def _cparams(*sem):
    return pltpu.CompilerParams(dimension_semantics=sem, vmem_limit_bytes=VMEM_LIMIT)


def _rms_rows(xf, w):
    ms = jnp.mean(xf * xf, axis=-1, keepdims=True)
    return xf * lax.rsqrt(ms + EPS) * w


def _sigmoid(x):
    return 1.0 / (1.0 + jnp.exp(-x))


def _log_sigmoid(x):
    return jnp.minimum(x, 0.0) - jnp.log(1.0 + jnp.exp(-jnp.abs(x)))


def _proj_kernel(x_ref, xp_ref, xn_ref, nw_ref, wt_ref, bgt_ref, cw_ref, cb_ref,
                 cm_ref, k_ref, gcm_ref, wbf_s, hall_s, pc_s, *, col_blk):
    TM = x_ref.shape[1]
    HALO = xp_ref.shape[1]
    j = pl.program_id(1)

    @pl.when(jnp.logical_and(pl.program_id(0) == 0, j == 0))
    def _():
        for c0 in range(0, wbf_s.shape[1], col_blk):
            wbf_s[:, c0:c0 + col_blk] = wt_ref[c0:c0 + col_blk, :].T.astype(BF16)

    nw = nw_ref[...]
    hn = _rms_rows(x_ref[0], nw).astype(BF16)
    hall_s[0:HALO, :] = _rms_rows(xp_ref[0], nw).astype(BF16)
    hall_s[HALO:HALO + TM, :] = hn
    hall_s[HALO + TM:2 * HALO + TM, :] = _rms_rows(xn_ref[0], nw).astype(BF16)
    keep_p = jnp.where(j > 0, 1.0, 0.0)
    keep_n = jnp.where(j < pl.num_programs(1) - 1, 1.0, 0.0)
    n_cm = cm_ref.shape[2]
    n_conv = cw_ref.shape[1]
    n_blk = n_cm // col_blk
    w_k0 = M_WIDTH
    w_gate = n_cm + M_WIDTH
    n_typ = N_GATE // M_HEADS
    order = [t * M_HEADS + hd for hd in range(M_HEADS) for t in range(n_typ)]
    wg = jnp.concatenate([wt_ref[w_gate + r:w_gate + r + 1, :] for r in order], axis=0)
    bg = jnp.concatenate([bgt_ref[r:r + 1, :] for r in order], axis=0)
    nt_dims = (((1,), (1,)), ((), ()))
    gt = lax.dot_general(wg.astype(BF16), hn, nt_dims, preferred_element_type=F32) + bg
    row = lax.broadcasted_iota(jnp.int32, gt.shape, 0)
    lg = jnp.where((row % 2) == 1, _log_sigmoid(gt), gt)
    for hd in range(M_HEADS):
        gcm_ref[0, hd] = lg[hd * n_typ:(hd + 1) * n_typ]

    n_rb = TM // LANES
    n_sl = col_blk // LANES

    def project(w_row0, slot):
        wblk = wbf_s[:, pl.ds(w_row0, col_blk)]
        res = jnp.dot(hall_s[...], wblk, preferred_element_type=F32)
        for u in range(n_sl):
            ls = slice(u * LANES, (u + 1) * LANES)
            pc_s[slot, u, 0:HALO, :] = res[0:HALO, ls] * keep_p
            pc_s[slot, u, HALO:HALO + TM, :] = res[HALO:HALO + TM, ls]
            pc_s[slot, u, HALO + TM:2 * HALO + TM, :] = res[HALO + TM:2 * HALO + TM, ls] * keep_n

    def conv_tile(slot, u, rb, taps, bias):
        r = HALO + rb * LANES
        return (pc_s[slot, u, pl.ds(r - 1, LANES, stride=1), :] * taps[0:1]
                + pc_s[slot, u, r:r + LANES, :] * taps[1:2]
                + pc_s[slot, u, pl.ds(r + 1, LANES, stride=1), :] * taps[2:3] + bias)

    n_q = M_WIDTH // col_blk
    assert n_blk % 2 == 0

    def w_row_of(c):
        if c == n_blk:
            return w_k0
        return (c if c < n_q else c + n_q) * col_blk

    def emit(c, slot):
        for u in range(n_sl):
            cs = slice(c * col_blk + u * LANES, c * col_blk + (u + 1) * LANES)
            w0 = w_row_of(c) + u * LANES
            for rb in range(n_rb):
                if w0 < n_conv:
                    cv = conv_tile(slot, u, rb, cw_ref[:, w0:w0 + LANES], cb_ref[:, w0:w0 + LANES])
                else:
                    cv = pc_s[slot, u, HALO + rb * LANES:HALO + (rb + 1) * LANES, :]
                cm_ref[0, rb, cs, :] = cv.T.astype(BF16)

    project(0, 0)
    for c in range(n_blk):
        emit(c, c % 2)
        project(w_row_of(c + 1), (c + 1) % 2)

    k_scale = HEAD_DIM ** -0.5
    n_k = M_WIDTH // col_blk
    for kb in range(n_k):
        if kb > 0:
            project(w_k0 + kb * col_blk, kb % 2)
        for u in range(n_sl):
            cs = slice(kb * col_blk + u * LANES, kb * col_blk + (u + 1) * LANES)
            for rb in range(n_rb):
                ks = slice(w_k0 + cs.start, w_k0 + cs.stop)
                cv = conv_tile(kb % 2, u, rb, cw_ref[:, ks], cb_ref[:, ks])
                k_ref[0, rb * LANES:(rb + 1) * LANES, cs] = (cv * _sigmoid(cv) * k_scale).astype(BF16)


def _proj(x, nw, w_in_t, cols, bg_blk, cw, cb, tm_rows, col_blk=PROJ_COL_BLK):
    B, S, D = x.shape
    HALO = BF16_ROWS
    n_conv = cw.shape[1]
    n_cm = w_in_t.shape[0] - N_GATE - M_WIDTH
    n_t = tm_rows // LANES
    hb = tm_rows // HALO
    last = S // HALO - 1
    const = lambda b, j: (0, 0)
    resident = functools.partial(pl.BlockSpec, index_map=const, pipeline_mode=pl.Buffered(1))
    return pl.pallas_call(
        functools.partial(_proj_kernel, col_blk=col_blk),
        grid=(B, S // tm_rows),
        in_specs=[
            pl.BlockSpec((1, tm_rows, D), lambda b, j: (b, j, 0)),
            pl.BlockSpec((1, HALO, D), lambda b, j: (b, jnp.maximum(j * hb - 1, 0), 0)),
            pl.BlockSpec((1, HALO, D), lambda b, j: (b, jnp.minimum((j + 1) * hb, last), 0)),
            pl.BlockSpec((1, D), const),
            resident(w_in_t.shape),
            pl.BlockSpec((N_GATE, 1), lambda b, j: (bg_blk, 0)),
            pl.BlockSpec((3, n_conv), const),
            pl.BlockSpec((1, n_conv), const),
        ],
        out_specs=[
            pl.BlockSpec((1, n_t, n_cm, LANES), lambda b, j: (b, j, 0, 0)),
            pl.BlockSpec((1, tm_rows, M_WIDTH), lambda b, j: (b, j, 0)),
            pl.BlockSpec((1, M_HEADS, N_GATE // M_HEADS, tm_rows), lambda b, j: (b, 0, 0, j)),
        ],
        out_shape=[
            jax.ShapeDtypeStruct((B, S // LANES, n_cm, LANES), BF16),
            jax.ShapeDtypeStruct((B, S, M_WIDTH), BF16),
            jax.ShapeDtypeStruct((B, M_HEADS, N_GATE // M_HEADS, S), F32),
        ],
        scratch_shapes=[pltpu.VMEM((D, n_cm + M_WIDTH), BF16),
                        pltpu.VMEM((tm_rows + 2 * HALO, D), BF16),
                        pltpu.VMEM((2, col_blk // LANES, tm_rows + 2 * HALO, LANES), F32)],
        compiler_params=_cparams("arbitrary", "arbitrary"),
        name="proj",
    )(x, x, x, nw, w_in_t, cols, cw, cb)


def _mlstm_kernel(qt_ref, vt_ref, k_ref, g_ref, out_ref, rows_s, stab_s, h_s, c_s):
    S = k_ref.shape[1]
    HB = g_ref.shape[1]
    L = CHUNK
    NC = S // L
    D = HEAD_DIM
    DA = D + BF16_ROWS
    W = HB * D

    R_A, R_MX, R_W, R_DEC, R_ISC, R_EMT, R_GT, R_AMAX = range(8)

    def scan(x, op, fill):
        lane = lax.broadcasted_iota(jnp.int32, x.shape, 1) % L
        pre = x
        suf = x
        d = 1
        while d < L:
            pre = op(pre, jnp.where(lane >= d, pltpu.roll(pre, d, 1), fill))
            suf = op(suf, jnp.where(lane < L - d, pltpu.roll(suf, S - d, 1), fill))
            d *= 2
        return pre, suf

    chains = [(hd, dr) for hd in range(HB) for dr in range(2)]
    assert HB % 2 == 0 and len(chains) <= 8
    a_rows = {}
    for h0 in range(0, HB, 2):
        g = jnp.concatenate([g_ref[0, h0, 0:4, :], g_ref[0, h0 + 1, 0:4, :]], axis=0)
        pre, suf = scan(g, jnp.add, 0.0)
        tot = pre + suf - g
        for j, hd in enumerate((h0, h0 + 1)):
            o = 4 * j
            b_rows = (pre[o + 1:o + 2], suf[o + 3:o + 4])
            for dr in range(2):
                a_rows[hd, dr] = g[o + 2 * dr:o + 2 * dr + 1] - b_rows[dr]
                rows_s[hd, dr, R_A:R_A + 1, :] = a_rows[hd, dr]
                rows_s[hd, dr, R_EMT:R_EMT + 1, :] = b_rows[dr]
                rows_s[hd, dr, R_GT:R_GT + 1, :] = tot[o + 2 * dr + 1:o + 2 * dr + 2]
    a8 = jnp.concatenate([a_rows[ch] for ch in chains]
                         + [jnp.zeros((8 - len(chains), S), F32)] * (len(chains) < 8), axis=0)
    pmax, smax = scan(a8, jnp.maximum, NEG_BIG)
    for i, (hd, dr) in enumerate(chains):
        rows_s[hd, dr, R_MX:R_MX + 1, :] = pmax[i:i + 1] if dr == 0 else smax[i:i + 1]
        rows_s[hd, dr, R_AMAX:R_AMAX + 1, :] = jnp.maximum(pmax[i:i + 1], smax[i:i + 1])

    def chunk_of(dr, i):
        return i if dr == 0 else NC - 1 - i

    def stabilisers(i, ms):
        out = []
        for (hd, dr), m in zip(chains, ms):
            r0 = pl.multiple_of(chunk_of(dr, i) * L, L)
            gt = rows_s[hd, dr, R_GT:R_GT + 1, pl.ds(r0, L)]
            m_end = jnp.maximum(gt + m, gt + rows_s[hd, dr, R_AMAX:R_AMAX + 1, pl.ds(r0, L)])
            stab_s[hd, dr, 0:1, pl.ds(r0, L)] = m
            stab_s[hd, dr, 1:2, pl.ds(r0, L)] = m_end
            out.append(m_end)
        return tuple(out)

    lax.fori_loop(0, NC, stabilisers, tuple(jnp.zeros((1, L), F32) for _ in chains), unroll=4)

    for hd, dr in chains:
        a = rows_s[hd, dr, R_A:R_A + 1, :]
        gt = rows_s[hd, dr, R_GT:R_GT + 1, :]
        m = stab_s[hd, dr, 0:1, :]
        m_end = stab_s[hd, dr, 1:2, :]
        mx = jnp.maximum(m, rows_s[hd, dr, R_MX:R_MX + 1, :])
        b = rows_s[hd, dr, R_EMT:R_EMT + 1, :]
        rows_s[hd, dr, R_MX:R_MX + 1, :] = mx
        rows_s[hd, dr, R_W:R_W + 1, :] = jnp.exp(gt + a - m_end)
        rows_s[hd, dr, R_DEC:R_DEC + 1, :] = jnp.exp(gt + m - m_end)
        rows_s[hd, dr, R_ISC:R_ISC + 1, :] = jnp.exp(m - mx)
        rows_s[hd, dr, R_EMT:R_EMT + 1, :] = jnp.exp(-(b + mx))

    ones_row = (lax.broadcasted_iota(jnp.int32, (BF16_ROWS, L), 0) == 0).astype(BF16)

    c_s[...] = jnp.zeros_like(c_s)

    si = lax.broadcasted_iota(jnp.int32, (L, L), 0)
    ti = lax.broadcasted_iota(jnp.int32, (L, L), 1)
    eye = si == ti
    valid = (si <= ti, si >= ti)

    def chain(hd, dr, c, second):
        r0 = pl.multiple_of(c * L, L)
        rows = rows_s[hd, dr, :, pl.ds(r0, L)]
        a_col = jnp.sum(jnp.where(eye, rows[R_A:R_A + 1], 0.0), axis=1, keepdims=True)
        wts = jnp.exp(jnp.where(valid[dr], a_col - rows[R_MX:R_MX + 1], NEG_BIG))
        kc = k_ref[0, pl.ds(r0, L), hd * D:(hd + 1) * D]
        qpre = qt_ref[0, c, hd * D:(hd + 1) * D, :].astype(F32)
        qtc = (qpre * _sigmoid(qpre)).astype(BF16)
        vac = jnp.concatenate([vt_ref[0, c, hd * D:(hd + 1) * D, :], ones_row], axis=0)
        cmat = c_s[hd, dr]
        both = jnp.dot(jnp.concatenate([kc, cmat.astype(BF16)], axis=0), qtc,
                       preferred_element_type=F32)
        s_w = (both[0:L] * wts).astype(BF16)
        tot = (jnp.dot(vac, s_w, preferred_element_type=F32)
               + rows[R_ISC:R_ISC + 1] * both[L:L + DA])
        den = jnp.maximum(jnp.abs(tot[D:D + 1]), rows[R_EMT:R_EMT + 1])
        h = tot[0:D] * (1.0 / den)
        hs = slice(hd * D, (hd + 1) * D)
        if second:
            out_ref[0, c, hs, :] = (h_s[hs, pl.ds(r0, L)] + h).astype(BF16)
        else:
            h_s[hs, pl.ds(r0, L)] = h
        u = (vac.astype(F32) * rows[R_W:R_W + 1]).astype(BF16)
        c_s[hd, dr] = rows[R_DEC:R_DEC + 1, 0:1] * cmat + jnp.dot(u, kc, preferred_element_type=F32)

    def step(i, carry, second):
        for hd, dr in chains:
            chain(hd, dr, chunk_of(dr, i), second)
        return carry

    assert NC % 2 == 0
    lax.fori_loop(0, NC // 2, functools.partial(step, second=False), 0)
    lax.fori_loop(NC // 2, NC, functools.partial(step, second=True), 0)


def _mlstm(mt, k_tm, gcm, heads_per_step=MLSTM_HEADS_PER_STEP):
    B, S, _ = k_tm.shape
    NC = S // CHUNK
    D = HEAD_DIM
    HB = heads_per_step
    v0 = (M_WIDTH + 3 * H_WIDTH) // (HB * D)
    W = HB * D
    NB = M_HEADS // HB
    DA = D + BF16_ROWS
    cm_blk = (1, NC, W, LANES)
    return pl.pallas_call(
        _mlstm_kernel,
        grid=(B, NB),
        in_specs=[
            pl.BlockSpec(cm_blk, lambda b, h: (b, 0, h, 0)),
            pl.BlockSpec(cm_blk, lambda b, h: (b, 0, v0 + h, 0)),
            pl.BlockSpec((1, S, W), lambda b, h: (b, 0, h)),
            pl.BlockSpec((1, HB, N_GATE // M_HEADS, S), lambda b, h: (b, h, 0, 0)),
        ],
        out_specs=pl.BlockSpec(cm_blk, lambda b, h: (b, 0, h, 0)),
        out_shape=jax.ShapeDtypeStruct((B, NC, M_WIDTH, LANES), BF16),
        scratch_shapes=[
            pltpu.VMEM((HB, 2, 8, S), F32),
            pltpu.VMEM((HB, 2, 8, S), F32),
            pltpu.VMEM((W, S), F32),
            pltpu.VMEM((HB, 2, DA, D), F32),
        ],
        compiler_params=_cparams("parallel", "parallel"),
        name="mlstm",
    )(mt, mt, k_tm, gcm)


def _filt_mlp_kernel(w1t_ref, b1_ref, w2_ref, b2_ref, w3_ref, b3_ref, fr_ref, w4f_ref, w4b_ref,
                     fbias_ref, out_ref, h3_s, *, cb_rows):
    S = h3_s.shape[2]
    hi = lax.Precision.HIGHEST
    tn_dims = (((0,), (0,)), ((), ()))
    lane = lax.broadcasted_iota(jnp.int32, (1, S), 1)
    pos_f = lane.astype(F32)
    pos_b = (S - lane).astype(F32)

    def features(pos):
        bands = (FILTER_EMB - 1) // 2
        t = pos / (S - 1)
        ang = (2.0 * math.pi) * pos / S
        fidx = lax.broadcasted_iota(jnp.int32, (bands, 1), 0).astype(F32)
        f = 1e-4 + fidx * ((bands - 1 - 1e-4) / (bands - 1))
        fa = f * ang
        w1t = w1t_ref[...]
        pre = (w1t[:, 0:1] * t
               + jnp.dot(w1t[:, 1:1 + bands], jnp.cos(fa), precision=hi, preferred_element_type=F32)
               - jnp.dot(w1t[:, 1 + bands:], jnp.sin(fa), precision=hi, preferred_element_type=F32))
        fr = fr_ref[...]
        h = jnp.sin(fr[:, 0:1] * (pre + b1_ref[...]))
        h = jnp.sin(fr[:, 1:2] * (lax.dot_general(w2_ref[...], h, tn_dims, precision=hi,
                                                  preferred_element_type=F32) + b2_ref[...]))
        return jnp.sin(fr[:, 2:3] * (lax.dot_general(w3_ref[...], h, tn_dims, precision=hi,
                                                     preferred_element_type=F32) + b3_ref[...]))

    @pl.when(pl.program_id(0) == 0)
    def _():
        h3 = features(pos_f)
        h3_s[0] = h3
        n_t = S // LANES
        anti = (lax.broadcasted_iota(jnp.int32, (LANES, LANES), 0)
                + lax.broadcasted_iota(jnp.int32, (LANES, LANES), 1) == LANES - 1).astype(F32)
        rev = jnp.concatenate(
            [jnp.dot(h3[:, (n_t - 1 - u) * LANES:(n_t - u) * LANES], anti, precision=hi,
                     preferred_element_type=F32) for u in range(n_t)], axis=1)
        h3_s[1] = pltpu.roll(rev, 1, 1)

    r = pl.program_id(0) * cb_rows + lax.broadcasted_iota(jnp.int32, (cb_rows, 1), 0)
    ch = (r % H_WIDTH).astype(F32)
    max_decay = math.log(DECAY_TARGET) / FAST_DECAY_PCT
    min_decay = math.log(DECAY_TARGET) / SLOW_DECAY_PCT
    delta = jnp.abs(min_decay + ch * ((max_decay - min_decay) / (H_WIDTH - 1)))
    n_t = S // LANES
    for half, (w_ref, pos) in enumerate(((w4f_ref, pos_f), (w4b_ref, pos_b))):
        filt = lax.dot_general(w_ref[...].astype(BF16), h3_s[half].astype(BF16), tn_dims,
                               preferred_element_type=F32)
        filt = filt * jnp.exp(-(pos / (S - 1)) * delta)
        if half == 0:
            filt = jnp.where(lane == 0, filt + fbias_ref[...], filt)
        else:
            filt = jnp.where(lane == 0, 0.0, filt)
        for u in range(n_t):
            out_ref[half * n_t + u] = filt[:, u * LANES:(u + 1) * LANES].astype(BF16)


def _filt_mlp(w1t, cols, fb_blk, fbias_blk, w2, w3, fr, w4, S, cb_rows=FILT_MLP_ROWS):
    R = w4.shape[1] // 2
    Hd = FILTER_HIDDEN
    const = lambda i: (0, 0)
    nblk = R // cb_rows
    return pl.pallas_call(
        functools.partial(_filt_mlp_kernel, cb_rows=cb_rows),
        grid=(nblk,),
        in_specs=[
            pl.BlockSpec((Hd, FILTER_EMB), const),
            pl.BlockSpec((Hd, 1), lambda i: (fb_blk, 0)),
            pl.BlockSpec((Hd, Hd), const),
            pl.BlockSpec((Hd, 1), lambda i: (fb_blk + 1, 0)),
            pl.BlockSpec((Hd, Hd), const),
            pl.BlockSpec((Hd, 1), lambda i: (fb_blk + 2, 0)),
            pl.BlockSpec((Hd, 3), const),
            pl.BlockSpec((Hd, cb_rows), lambda i: (0, i)),
            pl.BlockSpec((Hd, cb_rows), lambda i: (0, nblk + i)),
            pl.BlockSpec((cb_rows, 1), lambda i: (fbias_blk + i, 0)),
        ],
        out_specs=pl.BlockSpec((2 * S // LANES, cb_rows, LANES), lambda i: (0, i, 0)),
        out_shape=jax.ShapeDtypeStruct((2 * S // LANES, R, LANES), BF16),
        scratch_shapes=[pltpu.VMEM((2, Hd, S), F32)],
        compiler_params=_cparams("arbitrary"),
        name="filt_mlp",
    )(w1t, cols, w2, cols, w3, cols, fr, w4, w4, cols)


@functools.lru_cache(maxsize=None)
def _dft_constants(S):
    N = 2 * S
    N2 = LANES
    N1 = N // N2
    h = N1 // 2
    k1 = np.arange(N1)
    k2 = np.arange(N2)
    a1 = -2.0 * np.pi * np.outer(k1, k1) / N1
    f1r, f1i = np.cos(a1), np.sin(a1)
    at = -2.0 * np.pi * np.outer(k1, k2) / N
    twr, twi = np.cos(at), np.sin(at)
    a2 = -2.0 * np.pi * np.outer(k2, k2) / N2
    f2r, f2i = np.cos(a2), np.sin(a2)
    s1c = np.block([[f1r[:, :h], -f1i[:, :h]], [f1i[:, :h], f1r[:, :h]]])
    s1r = np.concatenate([f1r, f1i], axis=0)
    s2 = np.block([[f2r, f2i], [-f2i, f2r]])
    s2i = np.block([[f2r, -f2i], [f2i, f2r]])
    s1i = np.block([[f1r[:h, :], f1i[:h, :]], [-f1i[:h, :], f1r[:h, :]]])
    cast = lambda a: np.asarray(a, np.float32)
    return dict(s1c=cast(s1c), s1r=cast(s1r), s2=cast(s2), s2i=cast(s2i), s1i=cast(s1i),
                twr=cast(twr), twi=cast(twi), N1=N1, h=h)


def _bf(a):
    return jnp.asarray(a, F32).astype(BF16)


def _filt_fft_kernel(kern_ref, s1r_ref, s2_ref, twr_ref, twi_ref, out_ref, src_s, slab_s, *, unroll):
    N1 = twr_ref.shape[0]
    n_ch = kern_ref.shape[1]
    pitch = src_s.shape[0] // N1
    scale = 1.0 / (N1 * LANES)
    twr = twr_ref[...]
    twi = twi_ref[...]
    s1r = s1r_ref[...]

    def load_in(t1, carry):
        src_s[pl.ds(pl.multiple_of(t1 * pitch, 8), n_ch), :] = kern_ref[t1].astype(F32)
        return carry
    lax.fori_loop(0, N1, load_in, 0, unroll=4)

    def per_group(j, carry):
        for u in range(0, unroll, 2):
            cs = (j * unroll + u, j * unroll + u + 1)
            z = jnp.concatenate([src_s[pl.ds(c, N1, stride=pitch), :] for c in cs], axis=1)
            a = jnp.dot(s1r, z.astype(BF16), preferred_element_type=F32)
            for i, c in enumerate(cs):
                ar = a[0:N1, i * LANES:(i + 1) * LANES]
                ai = a[N1:2 * N1, i * LANES:(i + 1) * LANES]
                r0 = pl.multiple_of(c * N1, N1)
                slab_s[pl.ds(r0, N1), 0:LANES] = (ar * twr - ai * twi).astype(BF16)
                slab_s[pl.ds(r0, N1), LANES:2 * LANES] = (ar * twi + ai * twr).astype(BF16)
        return carry

    lax.fori_loop(0, n_ch // unroll, per_group, 0)
    x = jnp.dot(slab_s[...], s2_ref[...], preferred_element_type=F32)
    out_ref[...] = (x * scale).astype(BF16).reshape(n_ch, N1, 2 * LANES)


def _filt_fft(kern, consts, n_ch=FILT_FFT_ROWS, unroll=FILT_FFT_ROWS):
    N1, R, _ = kern.shape
    const = lambda i: (0, 0)
    pitch = n_ch + 8
    return pl.pallas_call(
        functools.partial(_filt_fft_kernel, unroll=unroll),
        grid=(R // n_ch,),
        in_specs=[
            pl.BlockSpec((N1, n_ch, LANES), lambda i: (0, i, 0)),
            pl.BlockSpec((2 * N1, N1), const),
            pl.BlockSpec((2 * LANES, 2 * LANES), const),
            pl.BlockSpec((N1, LANES), const),
            pl.BlockSpec((N1, LANES), const),
        ],
        out_specs=pl.BlockSpec((n_ch, N1, 2 * LANES), lambda i: (i, 0, 0)),
        out_shape=jax.ShapeDtypeStruct((R, N1, 2 * LANES), BF16),
        scratch_shapes=[pltpu.VMEM((N1 * pitch, LANES), F32),
                        pltpu.VMEM((n_ch * N1, 2 * LANES), BF16)],
        compiler_params=_cparams("parallel"),
        name="filt_fft",
    )(kern, _bf(consts["s1r"]), _bf(consts["s2"]),
      jnp.asarray(consts["twr"]), jnp.asarray(consts["twi"]))


def _hyena_kernel(nw_ref, v_ref, x1_ref, x2_ref, khat_ref, s1c_ref, s2_ref, s2i_ref, s1i_ref,
                  twr_ref, twi_ref, out_ref, src_s, slab_s, yf_s, q_s, y_s, *, unroll):
    h = v_ref.shape[1]
    n_ch = v_ref.shape[2]
    N1 = 2 * h
    twr = twr_ref[...]
    twi = twi_ref[...]
    s1c = s1c_ref[...]
    s1i = s1i_ref[...]
    n_groups = n_ch // unroll
    rows = unroll * N1
    pitch = src_s.shape[0] // (2 * h)

    def tile_rows(i):
        return pl.ds(i * pitch, n_ch, stride=1)

    def chan_rows(c, bb):
        return pl.ds(bb * h * pitch + c, h, stride=pitch)

    def long_conv(order):
        def stage_a(g):
            for u in range(0, unroll, 2):
                cs = (g * unroll + u, g * unroll + u + 1)
                z = jnp.concatenate(
                    [jnp.concatenate([src_s[chan_rows(c, 0), :], src_s[chan_rows(c, 1), :]], axis=0)
                     for c in cs], axis=1)
                a = jnp.dot(s1c, z.astype(BF16), preferred_element_type=F32)
                for i, c in enumerate(cs):
                    ar = a[0:N1, i * LANES:(i + 1) * LANES]
                    ai = a[N1:2 * N1, i * LANES:(i + 1) * LANES]
                    r0 = c * N1 if isinstance(c, int) else pl.multiple_of(c * N1, N1)
                    slab_s[pl.ds(r0, N1), 0:LANES] = (ar * twr - ai * twi).astype(BF16)
                    slab_s[pl.ds(r0, N1), LANES:2 * LANES] = (ar * twi + ai * twr).astype(BF16)

        def stage_m1(g):
            r0 = g * rows if isinstance(g, int) else pl.multiple_of(g * rows, rows)
            c0 = g * unroll if isinstance(g, int) else pl.multiple_of(g * unroll, unroll)
            x = jnp.dot(slab_s[pl.ds(r0, rows), :], s2_ref[...], preferred_element_type=F32)
            xr = x[:, 0:LANES].reshape(unroll, N1, LANES)
            xi = x[:, LANES:2 * LANES].reshape(unroll, N1, LANES)
            kr = khat_ref[order, pl.ds(c0, unroll), :, 0:LANES].astype(F32)
            ki = khat_ref[order, pl.ds(c0, unroll), :, LANES:2 * LANES].astype(F32)
            yf_s[pl.ds(r0, rows), 0:LANES] = (xr * kr - xi * ki).reshape(rows, LANES).astype(BF16)
            yf_s[pl.ds(r0, rows), LANES:2 * LANES] = (xr * ki + xi * kr).reshape(rows, LANES).astype(BF16)

        def stage_m2(g):
            r0 = g * rows if isinstance(g, int) else pl.multiple_of(g * rows, rows)
            c0 = g * unroll if isinstance(g, int) else pl.multiple_of(g * unroll, unroll)
            p = jnp.dot(yf_s[pl.ds(r0, rows), :], s2i_ref[...], preferred_element_type=F32)
            pr = p[:, 0:LANES].reshape(unroll, N1, LANES)
            pi = p[:, LANES:2 * LANES].reshape(unroll, N1, LANES)
            q_s[pl.ds(c0, unroll), 0:N1, :] = (pr * twr + pi * twi).astype(BF16)
            q_s[pl.ds(c0, unroll), N1:2 * N1, :] = (pi * twr - pr * twi).astype(BF16)

        def stage_i(g):
            for u in range(0, unroll, 2):
                cs = (g * unroll + u, g * unroll + u + 1)
                q2 = jnp.concatenate([q_s[c] for c in cs], axis=1)
                y = jnp.dot(s1i, q2, preferred_element_type=F32)
                for i, c in enumerate(cs):
                    y_s[chan_rows(c, 0), :] = y[0:h, i * LANES:(i + 1) * LANES]
                    y_s[chan_rows(c, 1), :] = y[h:2 * h, i * LANES:(i + 1) * LANES]

        G = n_groups
        for t in range(G + 3):
            if 0 <= t - 3 < G:
                stage_i(t - 3)
            if 0 <= t - 2 < G:
                stage_m2(t - 2)
            if 0 <= t - 1 < G:
                stage_m1(t - 1)
            if t < G:
                stage_a(t)

    def per_tile(fn):
        for bb in range(2):
            def body(t1, carry, bb=bb):
                fn(bb, t1, tile_rows(bb * h + t1))
                return carry
            lax.fori_loop(0, h, body, 0, unroll=8)

    def load_in(bb, t1, rws):
        src_s[rws, :] = v_ref[bb, t1].astype(F32)
    per_tile(load_in)
    long_conv(0)

    def gate1(bb, t1, rws):
        src_s[rws, :] = y_s[rws, :] * x1_ref[bb, t1].astype(F32)
    per_tile(gate1)
    long_conv(1)

    nw = jnp.broadcast_to(nw_ref[...], (n_ch, LANES))

    def gate2_norm(bb, t1, rws):
        z = y_s[rws, :] * x2_ref[bb, t1].astype(F32)
        ms = jnp.mean(z * z, axis=0, keepdims=True)
        out_ref[bb, t1] = (z * lax.rsqrt(ms + EPS) * nw).astype(BF16)
    per_tile(gate2_norm)


def _hyena(cols, nw_blk, hy4, khat, consts, unroll=HYENA_GROUP):
    B, h, _, _ = hy4.shape
    C = H_WIDTH
    n_ch = GROUP_W
    G = C // n_ch
    g0 = M_WIDTH // n_ch
    N1 = consts["N1"]
    const = lambda g, p: (0, 0)
    blk = (2, h, n_ch, LANES)
    pitch = n_ch + 9
    return pl.pallas_call(
        functools.partial(_hyena_kernel, unroll=unroll),
        grid=(G, B // 2),
        in_specs=[
            pl.BlockSpec((n_ch, 1), lambda g, p: (nw_blk + g, 0)),
            pl.BlockSpec(blk, lambda g, p: (p, 0, g0 + g, 0)),
            pl.BlockSpec(blk, lambda g, p: (p, 0, g0 + G + g, 0)),
            pl.BlockSpec(blk, lambda g, p: (p, 0, g0 + 2 * G + g, 0)),
            pl.BlockSpec((2, n_ch, N1, 2 * LANES), lambda g, p: (0, g, 0, 0)),
            pl.BlockSpec((2 * N1, 2 * h), const),
            pl.BlockSpec((2 * LANES, 2 * LANES), const),
            pl.BlockSpec((2 * LANES, 2 * LANES), const),
            pl.BlockSpec((2 * h, 2 * N1), const),
            pl.BlockSpec((N1, LANES), const),
            pl.BlockSpec((N1, LANES), const),
        ],
        out_specs=pl.BlockSpec(blk, lambda g, p: (p, 0, g, 0)),
        out_shape=jax.ShapeDtypeStruct((B, h, C, LANES), BF16),
        scratch_shapes=[
            pltpu.VMEM((2 * h * pitch, LANES), F32),
            pltpu.VMEM((n_ch * N1, 2 * LANES), BF16),
            pltpu.VMEM((n_ch * N1, 2 * LANES), BF16),
            pltpu.VMEM((n_ch, 2 * N1, LANES), BF16),
            pltpu.VMEM((2 * h * pitch, LANES), F32),
        ],
        compiler_params=_cparams("parallel", "arbitrary"),
        name="hyena",
    )(cols, hy4, hy4, hy4, khat,
      _bf(consts["s1c"]), _bf(consts["s2"]), _bf(consts["s2i"]), _bf(consts["s1i"]),
      jnp.asarray(consts["twr"]), jnp.asarray(consts["twi"]))


def _outmlp_kernel(x_ref, hm_ref, og_ref, yh_ref, nwm_ref, wo_ref, w1_ref, w2_ref, n_post_ref,
                   n_pre_ref, n_post2_ref, out_ref):
    n_t = hm_ref.shape[1]
    nwm = jnp.broadcast_to(nwm_ref[...], (M_WIDTH, LANES))
    tiles = []
    for j in range(n_t):
        hg = hm_ref[0, j].astype(F32) * _sigmoid(og_ref[0, j].astype(F32))
        heads = []
        for hd in range(M_HEADS):
            hh = hg[hd * HEAD_DIM:(hd + 1) * HEAD_DIM]
            ms = jnp.mean(hh * hh, axis=0, keepdims=True)
            heads.append(hh * lax.rsqrt(ms + EPS))
        ym = jnp.concatenate(heads, axis=0) * nwm
        yt = jnp.concatenate([ym, yh_ref[0, j].astype(F32)], axis=0)
        tiles.append(yt.T.astype(BF16))
    y = tiles[0] if n_t == 1 else jnp.concatenate(tiles, axis=0)
    mix = jnp.dot(y, wo_ref[...], preferred_element_type=F32)
    x1 = x_ref[0] + _rms_rows(mix, n_post_ref[...])
    hm = _rms_rows(x1, n_pre_ref[...]).astype(BF16)
    mid = jnp.maximum(jnp.dot(hm, w1_ref[...], preferred_element_type=F32), 0.0)
    mid = (mid * mid).astype(BF16)
    ff = jnp.dot(mid, w2_ref[...], preferred_element_type=F32)
    out_ref[0] = x1 + _rms_rows(ff, n_post2_ref[...])


def _outmlp(x, hm, mt, yh, cols, nwm_blk, wo, w1, w2, n_post, n_pre, n_post2, tm_rows):
    B, S, D = x.shape
    n_t = tm_rows // LANES
    og0 = mt.shape[2] // M_WIDTH - 1
    const = lambda b, i: (0, 0)
    resident = functools.partial(pl.BlockSpec, index_map=const, pipeline_mode=pl.Buffered(1))
    return pl.pallas_call(
        _outmlp_kernel,
        grid=(B, S // tm_rows),
        in_specs=[
            pl.BlockSpec((1, tm_rows, D), lambda b, i: (b, i, 0)),
            pl.BlockSpec((1, n_t, M_WIDTH, LANES), lambda b, i: (b, i, 0, 0)),
            pl.BlockSpec((1, n_t, M_WIDTH, LANES), lambda b, i: (b, i, og0, 0)),
            pl.BlockSpec((1, n_t, H_WIDTH, LANES), lambda b, i: (b, i, 0, 0)),
            pl.BlockSpec((M_WIDTH, 1), lambda b, i: (nwm_blk, 0)),
            resident((D, D)),
            resident((D, D_FF)),
            resident((D_FF, D)),
            pl.BlockSpec((1, D), const),
            pl.BlockSpec((1, D), const),
            pl.BlockSpec((1, D), const),
        ],
        out_specs=pl.BlockSpec((1, tm_rows, D), lambda b, i: (b, i, 0)),
        out_shape=jax.ShapeDtypeStruct((B, S, D), F32),
        compiler_params=_cparams("parallel", "parallel"),
        name="outmlp",
    )(x, hm, mt, yh, cols, wo, w1, w2, n_post, n_pre, n_post2)


def kernel(x, norm_mix_pre, norm_mix_post, norm_mlp_pre, norm_mlp_post, w_in, b_gates,
           conv_w, conv_b, mlstm_norm_w, hyena_norm_w, filt_w1, filt_b1, filt_w2, filt_b2,
           filt_w3, filt_b3, filt_w4, filt_freq, filt_bias, w_out, w_mlp_in, w_mlp_out):
    B, S, D = x.shape
    assert D == D_MODEL and B % 2 == 0 and S % CHUNK == 0
    row = lambda a: a.astype(F32).reshape(1, -1)
    tm_rows = min(PROJ_ROWS, S)
    cols = jnp.concatenate([filt_bias.reshape(-1), mlstm_norm_w, hyena_norm_w, filt_b1, filt_b2,
                            filt_b3, b_gates]).astype(F32).reshape(-1, 1)
    off_mn = HYENA_ORDER * H_WIDTH
    off_hn = off_mn + M_WIDTH
    off_fb = off_hn + H_WIDTH
    off_bg = off_fb + 3 * FILTER_HIDDEN
    assert (off_mn % M_WIDTH == 0 and off_hn % GROUP_W == 0 and off_fb % FILTER_HIDDEN == 0
            and off_bg % N_GATE == 0 and off_mn % FILT_MLP_ROWS == 0)

    n_conv = 2 * M_WIDTH + 3 * H_WIDTH
    o_gate = n_conv + 2 * M_WIDTH
    assert w_in.shape == (D, o_gate + N_GATE)
    assert conv_w.shape == (3, n_conv)
    mt, k_tm, gcm = _proj(x, row(norm_mix_pre), w_in.astype(F32).T, cols, off_bg // N_GATE,
                          conv_w.astype(F32), row(conv_b), tm_rows)

    h_m = _mlstm(mt, k_tm, gcm)

    consts = _dft_constants(S)
    kern = _filt_mlp(filt_w1.astype(F32).T, cols, off_fb // FILTER_HIDDEN, 0, filt_w2.astype(F32),
                     filt_w3.astype(F32), filt_freq.astype(F32).T, filt_w4.astype(F32), S)
    khat = _filt_fft(kern, consts)
    khat = khat.reshape(2, H_WIDTH, consts["N1"], 2 * LANES)

    y_h = _hyena(cols, off_hn // GROUP_W, mt, khat, consts)

    return _outmlp(x, h_m, mt, y_h, cols, off_mn // M_WIDTH,
                   w_out.astype(BF16), w_mlp_in.astype(BF16), w_mlp_out.astype(BF16),
                   row(norm_mix_post), row(norm_mlp_pre), row(norm_mlp_post), min(OUTMLP_ROWS, S))
```

```python
import functools
import math

import numpy as np
import jax
import jax.numpy as jnp
from jax import lax
from jax.experimental import pallas as pl
from jax.experimental.pallas import tpu as pltpu

F32 = jnp.float32
BF16 = jnp.bfloat16

D_MODEL = 1024
M_WIDTH = 512
M_HEADS = 4
HEAD_DIM = 128
H_WIDTH = 512
H_GROUPS = 8
HYENA_ORDER = 2
GROUP_W = H_WIDTH // H_GROUPS
CHUNK = 128
FILTER_EMB = 33
FILTER_HIDDEN = 64
DECAY_TARGET = 1e-2
FAST_DECAY_PCT = 0.3
SLOW_DECAY_PCT = 1.5
D_FF = 4 * D_MODEL
N_GATE = 16
EPS = 1e-6
LANES = 128
BF16_ROWS = 16
NEG_BIG = -1e30
VMEM_LIMIT = 56 * 1024 * 1024

PROJ_ROWS = 1024
PROJ_COL_BLK = 256
MLSTM_HEADS_PER_STEP = 4
FILT_MLP_ROWS = 256
FILT_FFT_ROWS = 128
HYENA_GROUP = 8
OUTMLP_ROWS = 512


def _cparams(*sem):
    return pltpu.CompilerParams(dimension_semantics=sem, vmem_limit_bytes=VMEM_LIMIT)


def _rms_rows(xf, w):
    ms = jnp.mean(xf * xf, axis=-1, keepdims=True)
    return xf * lax.rsqrt(ms + EPS) * w


def _sigmoid(x):
    return 1.0 / (1.0 + jnp.exp(-x))


def _log_sigmoid(x):
    return jnp.minimum(x, 0.0) - jnp.log(1.0 + jnp.exp(-jnp.abs(x)))


def _proj_kernel(x_ref, xp_ref, xn_ref, nw_ref, wt_ref, bgt_ref, cw_ref, cb_ref,
                 cm_ref, k_ref, gcm_ref, wbf_s, hall_s, pc_s, *, col_blk):
    TM = x_ref.shape[1]
    HALO = xp_ref.shape[1]
    j = pl.program_id(1)

    @pl.when(jnp.logical_and(pl.program_id(0) == 0, j == 0))
    def _():
        for c0 in range(0, wbf_s.shape[1], col_blk):
            wbf_s[:, c0:c0 + col_blk] = wt_ref[c0:c0 + col_blk, :].T.astype(BF16)

    nw = nw_ref[...]
    hn = _rms_rows(x_ref[0], nw).astype(BF16)
    hall_s[0:HALO, :] = _rms_rows(xp_ref[0], nw).astype(BF16)
    hall_s[HALO:HALO + TM, :] = hn
    hall_s[HALO + TM:2 * HALO + TM, :] = _rms_rows(xn_ref[0], nw).astype(BF16)
    keep_p = jnp.where(j > 0, 1.0, 0.0)
    keep_n = jnp.where(j < pl.num_programs(1) - 1, 1.0, 0.0)
    n_cm = cm_ref.shape[2]
    n_conv = cw_ref.shape[1]
    n_blk = n_cm // col_blk
    w_k0 = M_WIDTH
    w_gate = n_cm + M_WIDTH
    n_typ = N_GATE // M_HEADS
    order = [t * M_HEADS + hd for hd in range(M_HEADS) for t in range(n_typ)]
    wg = jnp.concatenate([wt_ref[w_gate + r:w_gate + r + 1, :] for r in order], axis=0)
    bg = jnp.concatenate([bgt_ref[r:r + 1, :] for r in order], axis=0)
    nt_dims = (((1,), (1,)), ((), ()))
    gt = lax.dot_general(wg.astype(BF16), hn, nt_dims, preferred_element_type=F32) + bg
    row = lax.broadcasted_iota(jnp.int32, gt.shape, 0)
    lg = jnp.where((row % 2) == 1, _log_sigmoid(gt), gt)
    for hd in range(M_HEADS):
        gcm_ref[0, hd] = lg[hd * n_typ:(hd + 1) * n_typ]

    n_rb = TM // LANES
    n_sl = col_blk // LANES

    def project(w_row0, slot):
        wblk = wbf_s[:, pl.ds(w_row0, col_blk)]
        res = jnp.dot(hall_s[...], wblk, preferred_element_type=F32)
        for u in range(n_sl):
            ls = slice(u * LANES, (u + 1) * LANES)
            pc_s[slot, u, 0:HALO, :] = res[0:HALO, ls] * keep_p
            pc_s[slot, u, HALO:HALO + TM, :] = res[HALO:HALO + TM, ls]
            pc_s[slot, u, HALO + TM:2 * HALO + TM, :] = res[HALO + TM:2 * HALO + TM, ls] * keep_n

    def conv_tile(slot, u, rb, taps, bias):
        r = HALO + rb * LANES
        return (pc_s[slot, u, pl.ds(r - 1, LANES, stride=1), :] * taps[0:1]
                + pc_s[slot, u, r:r + LANES, :] * taps[1:2]
                + pc_s[slot, u, pl.ds(r + 1, LANES, stride=1), :] * taps[2:3] + bias)

    n_q = M_WIDTH // col_blk
    assert n_blk % 2 == 0

    def w_row_of(c):
        if c == n_blk:
            return w_k0
        return (c if c < n_q else c + n_q) * col_blk

    def emit(c, slot):
        for u in range(n_sl):
            cs = slice(c * col_blk + u * LANES, c * col_blk + (u + 1) * LANES)
            w0 = w_row_of(c) + u * LANES
            for rb in range(n_rb):
                if w0 < n_conv:
                    cv = conv_tile(slot, u, rb, cw_ref[:, w0:w0 + LANES], cb_ref[:, w0:w0 + LANES])
                else:
                    cv = pc_s[slot, u, HALO + rb * LANES:HALO + (rb + 1) * LANES, :]
                cm_ref[0, rb, cs, :] = cv.T.astype(BF16)

    project(0, 0)
    for c in range(n_blk):
        emit(c, c % 2)
        project(w_row_of(c + 1), (c + 1) % 2)

    k_scale = HEAD_DIM ** -0.5
    n_k = M_WIDTH // col_blk
    for kb in range(n_k):
        if kb > 0:
            project(w_k0 + kb * col_blk, kb % 2)
        for u in range(n_sl):
            cs = slice(kb * col_blk + u * LANES, kb * col_blk + (u + 1) * LANES)
            for rb in range(n_rb):
                ks = slice(w_k0 + cs.start, w_k0 + cs.stop)
                cv = conv_tile(kb % 2, u, rb, cw_ref[:, ks], cb_ref[:, ks])
                k_ref[0, rb * LANES:(rb + 1) * LANES, cs] = (cv * _sigmoid(cv) * k_scale).astype(BF16)


def _proj(x, nw, w_in_t, cols, bg_blk, cw, cb, tm_rows, col_blk=PROJ_COL_BLK):
    B, S, D = x.shape
    HALO = BF16_ROWS
    n_conv = cw.shape[1]
    n_cm = w_in_t.shape[0] - N_GATE - M_WIDTH
    n_t = tm_rows // LANES
    hb = tm_rows // HALO
    last = S // HALO - 1
    const = lambda b, j: (0, 0)
    resident = functools.partial(pl.BlockSpec, index_map=const, pipeline_mode=pl.Buffered(1))
    return pl.pallas_call(
        functools.partial(_proj_kernel, col_blk=col_blk),
        grid=(B, S // tm_rows),
        in_specs=[
            pl.BlockSpec((1, tm_rows, D), lambda b, j: (b, j, 0)),
            pl.BlockSpec((1, HALO, D), lambda b, j: (b, jnp.maximum(j * hb - 1, 0), 0)),
            pl.BlockSpec((1, HALO, D), lambda b, j: (b, jnp.minimum((j + 1) * hb, last), 0)),
            pl.BlockSpec((1, D), const),
            resident(w_in_t.shape),
            pl.BlockSpec((N_GATE, 1), lambda b, j: (bg_blk, 0)),
            pl.BlockSpec((3, n_conv), const),
            pl.BlockSpec((1, n_conv), const),
        ],
        out_specs=[
            pl.BlockSpec((1, n_t, n_cm, LANES), lambda b, j: (b, j, 0, 0)),
            pl.BlockSpec((1, tm_rows, M_WIDTH), lambda b, j: (b, j, 0)),
            pl.BlockSpec((1, M_HEADS, N_GATE // M_HEADS, tm_rows), lambda b, j: (b, 0, 0, j)),
        ],
        out_shape=[
            jax.ShapeDtypeStruct((B, S // LANES, n_cm, LANES), BF16),
            jax.ShapeDtypeStruct((B, S, M_WIDTH), BF16),
            jax.ShapeDtypeStruct((B, M_HEADS, N_GATE // M_HEADS, S), F32),
        ],
        scratch_shapes=[pltpu.VMEM((D, n_cm + M_WIDTH), BF16),
                        pltpu.VMEM((tm_rows + 2 * HALO, D), BF16),
                        pltpu.VMEM((2, col_blk // LANES, tm_rows + 2 * HALO, LANES), F32)],
        compiler_params=_cparams("arbitrary", "arbitrary"),
        name="proj",
    )(x, x, x, nw, w_in_t, cols, cw, cb)


def _mlstm_kernel(qt_ref, vt_ref, k_ref, g_ref, out_ref, rows_s, stab_s, h_s, c_s):
    S = k_ref.shape[1]
    HB = g_ref.shape[1]
    L = CHUNK
    NC = S // L
    D = HEAD_DIM
    DA = D + BF16_ROWS
    W = HB * D

    R_A, R_MX, R_W, R_DEC, R_ISC, R_EMT, R_GT, R_AMAX = range(8)

    def scan(x, op, fill):
        lane = lax.broadcasted_iota(jnp.int32, x.shape, 1) % L
        pre = x
        suf = x
        d = 1
        while d < L:
            pre = op(pre, jnp.where(lane >= d, pltpu.roll(pre, d, 1), fill))
            suf = op(suf, jnp.where(lane < L - d, pltpu.roll(suf, S - d, 1), fill))
            d *= 2
        return pre, suf

    chains = [(hd, dr) for hd in range(HB) for dr in range(2)]
    assert HB % 2 == 0 and len(chains) <= 8
    a_rows = {}
    for h0 in range(0, HB, 2):
        g = jnp.concatenate([g_ref[0, h0, 0:4, :], g_ref[0, h0 + 1, 0:4, :]], axis=0)
        pre, suf = scan(g, jnp.add, 0.0)
        tot = pre + suf - g
        for j, hd in enumerate((h0, h0 + 1)):
            o = 4 * j
            b_rows = (pre[o + 1:o + 2], suf[o + 3:o + 4])
            for dr in range(2):
                a_rows[hd, dr] = g[o + 2 * dr:o + 2 * dr + 1] - b_rows[dr]
                rows_s[hd, dr, R_A:R_A + 1, :] = a_rows[hd, dr]
                rows_s[hd, dr, R_EMT:R_EMT + 1, :] = b_rows[dr]
                rows_s[hd, dr, R_GT:R_GT + 1, :] = tot[o + 2 * dr + 1:o + 2 * dr + 2]
    a8 = jnp.concatenate([a_rows[ch] for ch in chains]
                         + [jnp.zeros((8 - len(chains), S), F32)] * (len(chains) < 8), axis=0)
    pmax, smax = scan(a8, jnp.maximum, NEG_BIG)
    for i, (hd, dr) in enumerate(chains):
        rows_s[hd, dr, R_MX:R_MX + 1, :] = pmax[i:i + 1] if dr == 0 else smax[i:i + 1]
        rows_s[hd, dr, R_AMAX:R_AMAX + 1, :] = jnp.maximum(pmax[i:i + 1], smax[i:i + 1])

    def chunk_of(dr, i):
        return i if dr == 0 else NC - 1 - i

    def stabilisers(i, ms):
        out = []
        for (hd, dr), m in zip(chains, ms):
            r0 = pl.multiple_of(chunk_of(dr, i) * L, L)
            gt = rows_s[hd, dr, R_GT:R_GT + 1, pl.ds(r0, L)]
            m_end = jnp.maximum(gt + m, gt + rows_s[hd, dr, R_AMAX:R_AMAX + 1, pl.ds(r0, L)])
            stab_s[hd, dr, 0:1, pl.ds(r0, L)] = m
            stab_s[hd, dr, 1:2, pl.ds(r0, L)] = m_end
            out.append(m_end)
        return tuple(out)

    lax.fori_loop(0, NC, stabilisers, tuple(jnp.zeros((1, L), F32) for _ in chains), unroll=4)

    for hd, dr in chains:
        a = rows_s[hd, dr, R_A:R_A + 1, :]
        gt = rows_s[hd, dr, R_GT:R_GT + 1, :]
        m = stab_s[hd, dr, 0:1, :]
        m_end = stab_s[hd, dr, 1:2, :]
        mx = jnp.maximum(m, rows_s[hd, dr, R_MX:R_MX + 1, :])
        b = rows_s[hd, dr, R_EMT:R_EMT + 1, :]
        rows_s[hd, dr, R_MX:R_MX + 1, :] = mx
        rows_s[hd, dr, R_W:R_W + 1, :] = jnp.exp(gt + a - m_end)
        rows_s[hd, dr, R_DEC:R_DEC + 1, :] = jnp.exp(gt + m - m_end)
        rows_s[hd, dr, R_ISC:R_ISC + 1, :] = jnp.exp(m - mx)
        rows_s[hd, dr, R_EMT:R_EMT + 1, :] = jnp.exp(-(b + mx))

    ones_row = (lax.broadcasted_iota(jnp.int32, (BF16_ROWS, L), 0) == 0).astype(BF16)

    c_s[...] = jnp.zeros_like(c_s)

    si = lax.broadcasted_iota(jnp.int32, (L, L), 0)
    ti = lax.broadcasted_iota(jnp.int32, (L, L), 1)
    eye = si == ti
    valid = (si <= ti, si >= ti)

    def chain(hd, dr, c, second):
        r0 = pl.multiple_of(c * L, L)
        rows = rows_s[hd, dr, :, pl.ds(r0, L)]
        a_col = jnp.sum(jnp.where(eye, rows[R_A:R_A + 1], 0.0), axis=1, keepdims=True)
        wts = jnp.exp(jnp.where(valid[dr], a_col - rows[R_MX:R_MX + 1], NEG_BIG))
        kc = k_ref[0, pl.ds(r0, L), hd * D:(hd + 1) * D]
        qpre = qt_ref[0, c, hd * D:(hd + 1) * D, :].astype(F32)
        qtc = (qpre * _sigmoid(qpre)).astype(BF16)
        vac = jnp.concatenate([vt_ref[0, c, hd * D:(hd + 1) * D, :], ones_row], axis=0)
        cmat = c_s[hd, dr]
        both = jnp.dot(jnp.concatenate([kc, cmat.astype(BF16)], axis=0), qtc,
                       preferred_element_type=F32)
        s_w = (both[0:L] * wts).astype(BF16)
        tot = (jnp.dot(vac, s_w, preferred_element_type=F32)
               + rows[R_ISC:R_ISC + 1] * both[L:L + DA])
        den = jnp.maximum(jnp.abs(tot[D:D + 1]), rows[R_EMT:R_EMT + 1])
        h = tot[0:D] * (1.0 / den)
        hs = slice(hd * D, (hd + 1) * D)
        if second:
            out_ref[0, c, hs, :] = (h_s[hs, pl.ds(r0, L)] + h).astype(BF16)
        else:
            h_s[hs, pl.ds(r0, L)] = h
        u = (vac.astype(F32) * rows[R_W:R_W + 1]).astype(BF16)
        c_s[hd, dr] = rows[R_DEC:R_DEC + 1, 0:1] * cmat + jnp.dot(u, kc, preferred_element_type=F32)

    def step(i, carry, second):
        for hd, dr in chains:
            chain(hd, dr, chunk_of(dr, i), second)
        return carry

    assert NC % 2 == 0
    lax.fori_loop(0, NC // 2, functools.partial(step, second=False), 0)
    lax.fori_loop(NC // 2, NC, functools.partial(step, second=True), 0)


def _mlstm(mt, k_tm, gcm, heads_per_step=MLSTM_HEADS_PER_STEP):
    B, S, _ = k_tm.shape
    NC = S // CHUNK
    D = HEAD_DIM
    HB = heads_per_step
    v0 = (M_WIDTH + 3 * H_WIDTH) // (HB * D)
    W = HB * D
    NB = M_HEADS // HB
    DA = D + BF16_ROWS
    cm_blk = (1, NC, W, LANES)
    return pl.pallas_call(
        _mlstm_kernel,
        grid=(B, NB),
        in_specs=[
            pl.BlockSpec(cm_blk, lambda b, h: (b, 0, h, 0)),
            pl.BlockSpec(cm_blk, lambda b, h: (b, 0, v0 + h, 0)),
            pl.BlockSpec((1, S, W), lambda b, h: (b, 0, h)),
            pl.BlockSpec((1, HB, N_GATE // M_HEADS, S), lambda b, h: (b, h, 0, 0)),
        ],
        out_specs=pl.BlockSpec(cm_blk, lambda b, h: (b, 0, h, 0)),
        out_shape=jax.ShapeDtypeStruct((B, NC, M_WIDTH, LANES), BF16),
        scratch_shapes=[
            pltpu.VMEM((HB, 2, 8, S), F32),
            pltpu.VMEM((HB, 2, 8, S), F32),
            pltpu.VMEM((W, S), F32),
            pltpu.VMEM((HB, 2, DA, D), F32),
        ],
        compiler_params=_cparams("parallel", "parallel"),
        name="mlstm",
    )(mt, mt, k_tm, gcm)


def _filt_mlp_kernel(w1t_ref, b1_ref, w2_ref, b2_ref, w3_ref, b3_ref, fr_ref, w4f_ref, w4b_ref,
                     fbias_ref, out_ref, h3_s, *, cb_rows):
    S = h3_s.shape[2]
    hi = lax.Precision.HIGHEST
    tn_dims = (((0,), (0,)), ((), ()))
    lane = lax.broadcasted_iota(jnp.int32, (1, S), 1)
    pos_f = lane.astype(F32)
    pos_b = (S - lane).astype(F32)

    def features(pos):
        bands = (FILTER_EMB - 1) // 2
        t = pos / (S - 1)
        ang = (2.0 * math.pi) * pos / S
        fidx = lax.broadcasted_iota(jnp.int32, (bands, 1), 0).astype(F32)
        f = 1e-4 + fidx * ((bands - 1 - 1e-4) / (bands - 1))
        fa = f * ang
        w1t = w1t_ref[...]
        pre = (w1t[:, 0:1] * t
               + jnp.dot(w1t[:, 1:1 + bands], jnp.cos(fa), precision=hi, preferred_element_type=F32)
               - jnp.dot(w1t[:, 1 + bands:], jnp.sin(fa), precision=hi, preferred_element_type=F32))
        fr = fr_ref[...]
        h = jnp.sin(fr[:, 0:1] * (pre + b1_ref[...]))
        h = jnp.sin(fr[:, 1:2] * (lax.dot_general(w2_ref[...], h, tn_dims, precision=hi,
                                                  preferred_element_type=F32) + b2_ref[...]))
        return jnp.sin(fr[:, 2:3] * (lax.dot_general(w3_ref[...], h, tn_dims, precision=hi,
                                                     preferred_element_type=F32) + b3_ref[...]))

    @pl.when(pl.program_id(0) == 0)
    def _():
        h3 = features(pos_f)
        h3_s[0] = h3
        n_t = S // LANES
        anti = (lax.broadcasted_iota(jnp.int32, (LANES, LANES), 0)
                + lax.broadcasted_iota(jnp.int32, (LANES, LANES), 1) == LANES - 1).astype(F32)
        rev = jnp.concatenate(
            [jnp.dot(h3[:, (n_t - 1 - u) * LANES:(n_t - u) * LANES], anti, precision=hi,
                     preferred_element_type=F32) for u in range(n_t)], axis=1)
        h3_s[1] = pltpu.roll(rev, 1, 1)

    r = pl.program_id(0) * cb_rows + lax.broadcasted_iota(jnp.int32, (cb_rows, 1), 0)
    ch = (r % H_WIDTH).astype(F32)
    max_decay = math.log(DECAY_TARGET) / FAST_DECAY_PCT
    min_decay = math.log(DECAY_TARGET) / SLOW_DECAY_PCT
    delta = jnp.abs(min_decay + ch * ((max_decay - min_decay) / (H_WIDTH - 1)))
    n_t = S // LANES
    for half, (w_ref, pos) in enumerate(((w4f_ref, pos_f), (w4b_ref, pos_b))):
        filt = lax.dot_general(w_ref[...].astype(BF16), h3_s[half].astype(BF16), tn_dims,
                               preferred_element_type=F32)
        filt = filt * jnp.exp(-(pos / (S - 1)) * delta)
        if half == 0:
            filt = jnp.where(lane == 0, filt + fbias_ref[...], filt)
        else:
            filt = jnp.where(lane == 0, 0.0, filt)
        for u in range(n_t):
            out_ref[half * n_t + u] = filt[:, u * LANES:(u + 1) * LANES].astype(BF16)


def _filt_mlp(w1t, cols, fb_blk, fbias_blk, w2, w3, fr, w4, S, cb_rows=FILT_MLP_ROWS):
    R = w4.shape[1] // 2
    Hd = FILTER_HIDDEN
    const = lambda i: (0, 0)
    nblk = R // cb_rows
    return pl.pallas_call(
        functools.partial(_filt_mlp_kernel, cb_rows=cb_rows),
        grid=(nblk,),
        in_specs=[
            pl.BlockSpec((Hd, FILTER_EMB), const),
            pl.BlockSpec((Hd, 1), lambda i: (fb_blk, 0)),
            pl.BlockSpec((Hd, Hd), const),
            pl.BlockSpec((Hd, 1), lambda i: (fb_blk + 1, 0)),
            pl.BlockSpec((Hd, Hd), const),
            pl.BlockSpec((Hd, 1), lambda i: (fb_blk + 2, 0)),
            pl.BlockSpec((Hd, 3), const),
            pl.BlockSpec((Hd, cb_rows), lambda i: (0, i)),
            pl.BlockSpec((Hd, cb_rows), lambda i: (0, nblk + i)),
            pl.BlockSpec((cb_rows, 1), lambda i: (fbias_blk + i, 0)),
        ],
        out_specs=pl.BlockSpec((2 * S // LANES, cb_rows, LANES), lambda i: (0, i, 0)),
        out_shape=jax.ShapeDtypeStruct((2 * S // LANES, R, LANES), BF16),
        scratch_shapes=[pltpu.VMEM((2, Hd, S), F32)],
        compiler_params=_cparams("arbitrary"),
        name="filt_mlp",
    )(w1t, cols, w2, cols, w3, cols, fr, w4, w4, cols)


@functools.lru_cache(maxsize=None)
def _dft_constants(S):
    N = 2 * S
    N2 = LANES
    N1 = N // N2
    h = N1 // 2
    k1 = np.arange(N1)
    k2 = np.arange(N2)
    a1 = -2.0 * np.pi * np.outer(k1, k1) / N1
    f1r, f1i = np.cos(a1), np.sin(a1)
    at = -2.0 * np.pi * np.outer(k1, k2) / N
    twr, twi = np.cos(at), np.sin(at)
    a2 = -2.0 * np.pi * np.outer(k2, k2) / N2
    f2r, f2i = np.cos(a2), np.sin(a2)
    s1c = np.block([[f1r[:, :h], -f1i[:, :h]], [f1i[:, :h], f1r[:, :h]]])
    s1r = np.concatenate([f1r, f1i], axis=0)
    s2 = np.block([[f2r, f2i], [-f2i, f2r]])
    s2i = np.block([[f2r, -f2i], [f2i, f2r]])
    s1i = np.block([[f1r[:h, :], f1i[:h, :]], [-f1i[:h, :], f1r[:h, :]]])
    cast = lambda a: np.asarray(a, np.float32)
    return dict(s1c=cast(s1c), s1r=cast(s1r), s2=cast(s2), s2i=cast(s2i), s1i=cast(s1i),
                twr=cast(twr), twi=cast(twi), N1=N1, h=h)


def _bf(a):
    return jnp.asarray(a, F32).astype(BF16)


def _filt_fft_kernel(kern_ref, s1r_ref, s2_ref, twr_ref, twi_ref, out_ref, src_s, slab_s, *, unroll):
    N1 = twr_ref.shape[0]
    n_ch = kern_ref.shape[1]
    pitch = src_s.shape[0] // N1
    scale = 1.0 / (N1 * LANES)
    twr = twr_ref[...]
    twi = twi_ref[...]
    s1r = s1r_ref[...]

    def load_in(t1, carry):
        src_s[pl.ds(pl.multiple_of(t1 * pitch, 8), n_ch), :] = kern_ref[t1].astype(F32)
        return carry
    lax.fori_loop(0, N1, load_in, 0, unroll=4)

    def per_group(j, carry):
        for u in range(0, unroll, 2):
            cs = (j * unroll + u, j * unroll + u + 1)
            z = jnp.concatenate([src_s[pl.ds(c, N1, stride=pitch), :] for c in cs], axis=1)
            a = jnp.dot(s1r, z.astype(BF16), preferred_element_type=F32)
            for i, c in enumerate(cs):
                ar = a[0:N1, i * LANES:(i + 1) * LANES]
                ai = a[N1:2 * N1, i * LANES:(i + 1) * LANES]
                r0 = pl.multiple_of(c * N1, N1)
                slab_s[pl.ds(r0, N1), 0:LANES] = (ar * twr - ai * twi).astype(BF16)
                slab_s[pl.ds(r0, N1), LANES:2 * LANES] = (ar * twi + ai * twr).astype(BF16)
        return carry

    lax.fori_loop(0, n_ch // unroll, per_group, 0)
    x = jnp.dot(slab_s[...], s2_ref[...], preferred_element_type=F32)
    out_ref[...] = (x * scale).astype(BF16).reshape(n_ch, N1, 2 * LANES)


def _filt_fft(kern, consts, n_ch=FILT_FFT_ROWS, unroll=FILT_FFT_ROWS):
    N1, R, _ = kern.shape
    const = lambda i: (0, 0)
    pitch = n_ch + 8
    return pl.pallas_call(
        functools.partial(_filt_fft_kernel, unroll=unroll),
        grid=(R // n_ch,),
        in_specs=[
            pl.BlockSpec((N1, n_ch, LANES), lambda i: (0, i, 0)),
            pl.BlockSpec((2 * N1, N1), const),
            pl.BlockSpec((2 * LANES, 2 * LANES), const),
            pl.BlockSpec((N1, LANES), const),
            pl.BlockSpec((N1, LANES), const),
        ],
        out_specs=pl.BlockSpec((n_ch, N1, 2 * LANES), lambda i: (i, 0, 0)),
        out_shape=jax.ShapeDtypeStruct((R, N1, 2 * LANES), BF16),
        scratch_shapes=[pltpu.VMEM((N1 * pitch, LANES), F32),
                        pltpu.VMEM((n_ch * N1, 2 * LANES), BF16)],
        compiler_params=_cparams("parallel"),
        name="filt_fft",
    )(kern, _bf(consts["s1r"]), _bf(consts["s2"]),
      jnp.asarray(consts["twr"]), jnp.asarray(consts["twi"]))


def _hyena_kernel(nw_ref, v_ref, x1_ref, x2_ref, khat_ref, s1c_ref, s2_ref, s2i_ref, s1i_ref,
                  twr_ref, twi_ref, out_ref, src_s, slab_s, yf_s, q_s, y_s, *, unroll):
    h = v_ref.shape[1]
    n_ch = v_ref.shape[2]
    N1 = 2 * h
    twr = twr_ref[...]
    twi = twi_ref[...]
    s1c = s1c_ref[...]
    s1i = s1i_ref[...]
    n_groups = n_ch // unroll
    rows = unroll * N1
    pitch = src_s.shape[0] // (2 * h)

    def tile_rows(i):
        return pl.ds(i * pitch, n_ch, stride=1)

    def chan_rows(c, bb):
        return pl.ds(bb * h * pitch + c, h, stride=pitch)

    def long_conv(order):
        def stage_a(g):
            for u in range(0, unroll, 2):
                cs = (g * unroll + u, g * unroll + u + 1)
                z = jnp.concatenate(
                    [jnp.concatenate([src_s[chan_rows(c, 0), :], src_s[chan_rows(c, 1), :]], axis=0)
                     for c in cs], axis=1)
                a = jnp.dot(s1c, z.astype(BF16), preferred_element_type=F32)
                for i, c in enumerate(cs):
                    ar = a[0:N1, i * LANES:(i + 1) * LANES]
                    ai = a[N1:2 * N1, i * LANES:(i + 1) * LANES]
                    r0 = c * N1 if isinstance(c, int) else pl.multiple_of(c * N1, N1)
                    slab_s[pl.ds(r0, N1), 0:LANES] = (ar * twr - ai * twi).astype(BF16)
                    slab_s[pl.ds(r0, N1), LANES:2 * LANES] = (ar * twi + ai * twr).astype(BF16)

        def stage_m1(g):
            r0 = g * rows if isinstance(g, int) else pl.multiple_of(g * rows, rows)
            c0 = g * unroll if isinstance(g, int) else pl.multiple_of(g * unroll, unroll)
            x = jnp.dot(slab_s[pl.ds(r0, rows), :], s2_ref[...], preferred_element_type=F32)
            xr = x[:, 0:LANES].reshape(unroll, N1, LANES)
            xi = x[:, LANES:2 * LANES].reshape(unroll, N1, LANES)
            kr = khat_ref[order, pl.ds(c0, unroll), :, 0:LANES].astype(F32)
            ki = khat_ref[order, pl.ds(c0, unroll), :, LANES:2 * LANES].astype(F32)
            yf_s[pl.ds(r0, rows), 0:LANES] = (xr * kr - xi * ki).reshape(rows, LANES).astype(BF16)
            yf_s[pl.ds(r0, rows), LANES:2 * LANES] = (xr * ki + xi * kr).reshape(rows, LANES).astype(BF16)

        def stage_m2(g):
            r0 = g * rows if isinstance(g, int) else pl.multiple_of(g * rows, rows)
            c0 = g * unroll if isinstance(g, int) else pl.multiple_of(g * unroll, unroll)
            p = jnp.dot(yf_s[pl.ds(r0, rows), :], s2i_ref[...], preferred_element_type=F32)
            pr = p[:, 0:LANES].reshape(unroll, N1, LANES)
            pi = p[:, LANES:2 * LANES].reshape(unroll, N1, LANES)
            q_s[pl.ds(c0, unroll), 0:N1, :] = (pr * twr + pi * twi).astype(BF16)
            q_s[pl.ds(c0, unroll), N1:2 * N1, :] = (pi * twr - pr * twi).astype(BF16)

        def stage_i(g):
            for u in range(0, unroll, 2):
                cs = (g * unroll + u, g * unroll + u + 1)
                q2 = jnp.concatenate([q_s[c] for c in cs], axis=1)
                y = jnp.dot(s1i, q2, preferred_element_type=F32)
                for i, c in enumerate(cs):
                    y_s[chan_rows(c, 0), :] = y[0:h, i * LANES:(i + 1) * LANES]
                    y_s[chan_rows(c, 1), :] = y[h:2 * h, i * LANES:(i + 1) * LANES]

        G = n_groups
        for t in range(G + 3):
            if 0 <= t - 3 < G:
                stage_i(t - 3)
            if 0 <= t - 2 < G:
                stage_m2(t - 2)
            if 0 <= t - 1 < G:
                stage_m1(t - 1)
            if t < G:
                stage_a(t)

    def per_tile(fn):
        for bb in range(2):
            def body(t1, carry, bb=bb):
                fn(bb, t1, tile_rows(bb * h + t1))
                return carry
            lax.fori_loop(0, h, body, 0, unroll=8)

    def load_in(bb, t1, rws):
        src_s[rws, :] = v_ref[bb, t1].astype(F32)
    per_tile(load_in)
    long_conv(0)

    def gate1(bb, t1, rws):
        src_s[rws, :] = y_s[rws, :] * x1_ref[bb, t1].astype(F32)
    per_tile(gate1)
    long_conv(1)

    nw = jnp.broadcast_to(nw_ref[...], (n_ch, LANES))

    def gate2_norm(bb, t1, rws):
        z = y_s[rws, :] * x2_ref[bb, t1].astype(F32)
        ms = jnp.mean(z * z, axis=0, keepdims=True)
        out_ref[bb, t1] = (z * lax.rsqrt(ms + EPS) * nw).astype(BF16)
    per_tile(gate2_norm)


def _hyena(cols, nw_blk, hy4, khat, consts, unroll=HYENA_GROUP):
    B, h, _, _ = hy4.shape
    C = H_WIDTH
    n_ch = GROUP_W
    G = C // n_ch
    g0 = M_WIDTH // n_ch
    N1 = consts["N1"]
    const = lambda g, p: (0, 0)
    blk = (2, h, n_ch, LANES)
    pitch = n_ch + 9
    return pl.pallas_call(
        functools.partial(_hyena_kernel, unroll=unroll),
        grid=(G, B // 2),
        in_specs=[
            pl.BlockSpec((n_ch, 1), lambda g, p: (nw_blk + g, 0)),
            pl.BlockSpec(blk, lambda g, p: (p, 0, g0 + g, 0)),
            pl.BlockSpec(blk, lambda g, p: (p, 0, g0 + G + g, 0)),
            pl.BlockSpec(blk, lambda g, p: (p, 0, g0 + 2 * G + g, 0)),
            pl.BlockSpec((2, n_ch, N1, 2 * LANES), lambda g, p: (0, g, 0, 0)),
            pl.BlockSpec((2 * N1, 2 * h), const),
            pl.BlockSpec((2 * LANES, 2 * LANES), const),
            pl.BlockSpec((2 * LANES, 2 * LANES), const),
            pl.BlockSpec((2 * h, 2 * N1), const),
            pl.BlockSpec((N1, LANES), const),
            pl.BlockSpec((N1, LANES), const),
        ],
        out_specs=pl.BlockSpec(blk, lambda g, p: (p, 0, g, 0)),
        out_shape=jax.ShapeDtypeStruct((B, h, C, LANES), BF16),
        scratch_shapes=[
            pltpu.VMEM((2 * h * pitch, LANES), F32),
            pltpu.VMEM((n_ch * N1, 2 * LANES), BF16),
            pltpu.VMEM((n_ch * N1, 2 * LANES), BF16),
            pltpu.VMEM((n_ch, 2 * N1, LANES), BF16),
            pltpu.VMEM((2 * h * pitch, LANES), F32),
        ],
        compiler_params=_cparams("parallel", "arbitrary"),
        name="hyena",
    )(cols, hy4, hy4, hy4, khat,
      _bf(consts["s1c"]), _bf(consts["s2"]), _bf(consts["s2i"]), _bf(consts["s1i"]),
      jnp.asarray(consts["twr"]), jnp.asarray(consts["twi"]))


def _outmlp_kernel(x_ref, hm_ref, og_ref, yh_ref, nwm_ref, wo_ref, w1_hbm, w2_hbm, n_post_ref,
                   n_pre_ref, n_post2_ref, out_ref, w1_ref, w2_ref, sem):
    first = jnp.logical_and(pl.program_id(0) == 0, pl.program_id(1) == 0)
    w_copies = (pltpu.make_async_copy(w1_hbm, w1_ref, sem.at[0]),
                pltpu.make_async_copy(w2_hbm, w2_ref, sem.at[1]))

    @pl.when(first)
    def _():
        for cp in w_copies:
            cp.start()

    n_t = hm_ref.shape[1]
    nwm = jnp.broadcast_to(nwm_ref[...], (M_WIDTH, LANES))
    tiles = []
    for j in range(n_t):
        hg = hm_ref[0, j].astype(F32) * _sigmoid(og_ref[0, j].astype(F32))
        heads = []
        for hd in range(M_HEADS):
            hh = hg[hd * HEAD_DIM:(hd + 1) * HEAD_DIM]
            ms = jnp.mean(hh * hh, axis=0, keepdims=True)
            heads.append(hh * lax.rsqrt(ms + EPS))
        ym = jnp.concatenate(heads, axis=0) * nwm
        yt = jnp.concatenate([ym, yh_ref[0, j].astype(F32)], axis=0)
        tiles.append(yt.T.astype(BF16))
    y = tiles[0] if n_t == 1 else jnp.concatenate(tiles, axis=0)
    mix = jnp.dot(y, wo_ref[...], preferred_element_type=F32)
    x1 = x_ref[0] + _rms_rows(mix, n_post_ref[...])
    hm = _rms_rows(x1, n_pre_ref[...]).astype(BF16)

    @pl.when(first)
    def _():
        w_copies[0].wait()

    mid = jnp.maximum(jnp.dot(hm, w1_ref[...], preferred_element_type=F32), 0.0)
    mid = (mid * mid).astype(BF16)

    @pl.when(first)
    def _():
        w_copies[1].wait()

    ff = jnp.dot(mid, w2_ref[...], preferred_element_type=F32)
    out_ref[0] = x1 + _rms_rows(ff, n_post2_ref[...])


def _outmlp(x, hm, mt, yh, cols, nwm_blk, wo, w1, w2, n_post, n_pre, n_post2, tm_rows):
    B, S, D = x.shape
    n_t = tm_rows // LANES
    og0 = mt.shape[2] // M_WIDTH - 1
    const = lambda b, i: (0, 0)
    resident = functools.partial(pl.BlockSpec, index_map=const, pipeline_mode=pl.Buffered(1))
    return pl.pallas_call(
        _outmlp_kernel,
        grid=(B, S // tm_rows),
        in_specs=[
            pl.BlockSpec((1, tm_rows, D), lambda b, i: (b, i, 0)),
            pl.BlockSpec((1, n_t, M_WIDTH, LANES), lambda b, i: (b, i, 0, 0)),
            pl.BlockSpec((1, n_t, M_WIDTH, LANES), lambda b, i: (b, i, og0, 0)),
            pl.BlockSpec((1, n_t, H_WIDTH, LANES), lambda b, i: (b, i, 0, 0)),
            pl.BlockSpec((M_WIDTH, 1), lambda b, i: (nwm_blk, 0)),
            resident((D, D)),
            pl.BlockSpec(memory_space=pl.ANY),
            pl.BlockSpec(memory_space=pl.ANY),
            pl.BlockSpec((1, D), const),
            pl.BlockSpec((1, D), const),
            pl.BlockSpec((1, D), const),
        ],
        out_specs=pl.BlockSpec((1, tm_rows, D), lambda b, i: (b, i, 0)),
        out_shape=jax.ShapeDtypeStruct((B, S, D), F32),
        scratch_shapes=[pltpu.VMEM(w1.shape, BF16), pltpu.VMEM(w2.shape, BF16),
                        pltpu.SemaphoreType.DMA((2,))],
        compiler_params=_cparams("arbitrary", "arbitrary"),
        name="outmlp",
    )(x, hm, mt, yh, cols, wo, w1, w2, n_post, n_pre, n_post2)


def kernel(x, norm_mix_pre, norm_mix_post, norm_mlp_pre, norm_mlp_post, w_in, b_gates,
           conv_w, conv_b, mlstm_norm_w, hyena_norm_w, filt_w1, filt_b1, filt_w2, filt_b2,
           filt_w3, filt_b3, filt_w4, filt_freq, filt_bias, w_out, w_mlp_in, w_mlp_out):
    B, S, D = x.shape
    assert D == D_MODEL and B % 2 == 0 and S % CHUNK == 0
    row = lambda a: a.astype(F32).reshape(1, -1)
    tm_rows = min(PROJ_ROWS, S)
    cols = jnp.concatenate([filt_bias.reshape(-1), mlstm_norm_w, hyena_norm_w, filt_b1, filt_b2,
                            filt_b3, b_gates]).astype(F32).reshape(-1, 1)
    off_mn = HYENA_ORDER * H_WIDTH
    off_hn = off_mn + M_WIDTH
    off_fb = off_hn + H_WIDTH
    off_bg = off_fb + 3 * FILTER_HIDDEN
    assert (off_mn % M_WIDTH == 0 and off_hn % GROUP_W == 0 and off_fb % FILTER_HIDDEN == 0
            and off_bg % N_GATE == 0 and off_mn % FILT_MLP_ROWS == 0)

    n_conv = 2 * M_WIDTH + 3 * H_WIDTH
    o_gate = n_conv + 2 * M_WIDTH
    assert w_in.shape == (D, o_gate + N_GATE)
    assert conv_w.shape == (3, n_conv)
    mt, k_tm, gcm = _proj(x, row(norm_mix_pre), w_in.astype(F32).T, cols, off_bg // N_GATE,
                          conv_w.astype(F32), row(conv_b), tm_rows)

    h_m = _mlstm(mt, k_tm, gcm)

    consts = _dft_constants(S)
    kern = _filt_mlp(filt_w1.astype(F32).T, cols, off_fb // FILTER_HIDDEN, 0, filt_w2.astype(F32),
                     filt_w3.astype(F32), filt_freq.astype(F32).T, filt_w4.astype(F32), S)
    khat = _filt_fft(kern, consts)
    khat = khat.reshape(2, H_WIDTH, consts["N1"], 2 * LANES)

    y_h = _hyena(cols, off_hn // GROUP_W, mt, khat, consts)

    return _outmlp(x, h_m, mt, y_h, cols, off_mn // M_WIDTH,
                   w_out.astype(BF16), w_mlp_in.astype(BF16), w_mlp_out.astype(BF16),
                   row(norm_mix_post), row(norm_mlp_pre), row(norm_mlp_post), min(OUTMLP_ROWS, S))
```

```python
import functools
import math

import numpy as np
import jax
import jax.numpy as jnp
from jax import lax
from jax.experimental import pallas as pl
from jax.experimental.pallas import tpu as pltpu

F32 = jnp.float32
BF16 = jnp.bfloat16

D_MODEL = 1024
M_WIDTH = 512
M_HEADS = 4
HEAD_DIM = 128
H_WIDTH = 512
H_GROUPS = 8
HYENA_ORDER = 2
GROUP_W = H_WIDTH // H_GROUPS
CHUNK = 128
FILTER_EMB = 33
FILTER_HIDDEN = 64
DECAY_TARGET = 1e-2
FAST_DECAY_PCT = 0.3
SLOW_DECAY_PCT = 1.5
D_FF = 4 * D_MODEL
N_GATE = 16
EPS = 1e-6
LANES = 128
BF16_ROWS = 16
NEG_BIG = -1e30
VMEM_LIMIT = 56 * 1024 * 1024

PROJ_ROWS = 1024
PROJ_COL_BLK = 256
MLSTM_HEADS_PER_STEP = 4
FILT_MLP_ROWS = 256
FILT_FFT_ROWS = 128
HYENA_GROUP = 8
OUTMLP_ROWS = 512
CAST_ROWS = 128


def _cparams(*sem):
    return pltpu.CompilerParams(dimension_semantics=sem, vmem_limit_bytes=VMEM_LIMIT)


def _rms_rows(xf, w):
    ms = jnp.mean(xf * xf, axis=-1, keepdims=True)
    return xf * lax.rsqrt(ms + EPS) * w


def _sigmoid(x):
    return 1.0 / (1.0 + jnp.exp(-x))


def _log_sigmoid(x):
    return jnp.minimum(x, 0.0) - jnp.log(1.0 + jnp.exp(-jnp.abs(x)))


def _proj_kernel(x_ref, xp_ref, xn_ref, nw_ref, wt_ref, bgt_ref, cw_ref, cb_ref,
                 cm_ref, k_ref, gcm_ref, wbf_s, hall_s, pc_s, *, col_blk):
    TM = x_ref.shape[1]
    HALO = xp_ref.shape[1]
    j = pl.program_id(1)

    @pl.when(jnp.logical_and(pl.program_id(0) == 0, j == 0))
    def _():
        for c0 in range(0, wbf_s.shape[1], col_blk):
            wbf_s[:, c0:c0 + col_blk] = wt_ref[c0:c0 + col_blk, :].T.astype(BF16)

    nw = nw_ref[...]
    hn = _rms_rows(x_ref[0], nw).astype(BF16)
    hall_s[0:HALO, :] = _rms_rows(xp_ref[0], nw).astype(BF16)
    hall_s[HALO:HALO + TM, :] = hn
    hall_s[HALO + TM:2 * HALO + TM, :] = _rms_rows(xn_ref[0], nw).astype(BF16)
    keep_p = jnp.where(j > 0, 1.0, 0.0)
    keep_n = jnp.where(j < pl.num_programs(1) - 1, 1.0, 0.0)
    n_cm = cm_ref.shape[2]
    n_conv = cw_ref.shape[1]
    n_blk = n_cm // col_blk
    w_k0 = M_WIDTH
    w_gate = n_cm + M_WIDTH
    n_typ = N_GATE // M_HEADS
    order = [t * M_HEADS + hd for hd in range(M_HEADS) for t in range(n_typ)]
    wg = jnp.concatenate([wt_ref[w_gate + r:w_gate + r + 1, :] for r in order], axis=0)
    bg = jnp.concatenate([bgt_ref[r:r + 1, :] for r in order], axis=0)
    nt_dims = (((1,), (1,)), ((), ()))
    gt = lax.dot_general(wg.astype(BF16), hn, nt_dims, preferred_element_type=F32) + bg
    row = lax.broadcasted_iota(jnp.int32, gt.shape, 0)
    lg = jnp.where((row % 2) == 1, _log_sigmoid(gt), gt)
    for hd in range(M_HEADS):
        gcm_ref[0, hd] = lg[hd * n_typ:(hd + 1) * n_typ]

    n_rb = TM // LANES
    n_sl = col_blk // LANES

    def project(w_row0, slot):
        wblk = wbf_s[:, pl.ds(w_row0, col_blk)]
        res = jnp.dot(hall_s[...], wblk, preferred_element_type=F32)
        for u in range(n_sl):
            ls = slice(u * LANES, (u + 1) * LANES)
            pc_s[slot, u, 0:HALO, :] = res[0:HALO, ls] * keep_p
            pc_s[slot, u, HALO:HALO + TM, :] = res[HALO:HALO + TM, ls]
            pc_s[slot, u, HALO + TM:2 * HALO + TM, :] = res[HALO + TM:2 * HALO + TM, ls] * keep_n

    def conv_tile(slot, u, rb, taps, bias):
        r = HALO + rb * LANES
        return (pc_s[slot, u, pl.ds(r - 1, LANES, stride=1), :] * taps[0:1]
                + pc_s[slot, u, r:r + LANES, :] * taps[1:2]
                + pc_s[slot, u, pl.ds(r + 1, LANES, stride=1), :] * taps[2:3] + bias)

    n_q = M_WIDTH // col_blk
    assert n_blk % 2 == 0

    def w_row_of(c):
        if c == n_blk:
            return w_k0
        return (c if c < n_q else c + n_q) * col_blk

    def emit(c, slot):
        for u in range(n_sl):
            cs = slice(c * col_blk + u * LANES, c * col_blk + (u + 1) * LANES)
            w0 = w_row_of(c) + u * LANES
            for rb in range(n_rb):
                if w0 < n_conv:
                    cv = conv_tile(slot, u, rb, cw_ref[:, w0:w0 + LANES], cb_ref[:, w0:w0 + LANES])
                else:
                    cv = pc_s[slot, u, HALO + rb * LANES:HALO + (rb + 1) * LANES, :]
                cm_ref[0, rb, cs, :] = cv.T.astype(BF16)

    project(0, 0)
    for c in range(n_blk):
        emit(c, c % 2)
        project(w_row_of(c + 1), (c + 1) % 2)

    k_scale = HEAD_DIM ** -0.5
    n_k = M_WIDTH // col_blk
    for kb in range(n_k):
        if kb > 0:
            project(w_k0 + kb * col_blk, kb % 2)
        for u in range(n_sl):
            cs = slice(kb * col_blk + u * LANES, kb * col_blk + (u + 1) * LANES)
            for rb in range(n_rb):
                ks = slice(w_k0 + cs.start, w_k0 + cs.stop)
                cv = conv_tile(kb % 2, u, rb, cw_ref[:, ks], cb_ref[:, ks])
                k_ref[0, rb * LANES:(rb + 1) * LANES, cs] = (cv * _sigmoid(cv) * k_scale).astype(BF16)


def _proj(x, nw, w_in_t, cols, bg_blk, cw, cb, tm_rows, col_blk=PROJ_COL_BLK):
    B, S, D = x.shape
    HALO = BF16_ROWS
    n_conv = cw.shape[1]
    n_cm = w_in_t.shape[0] - N_GATE - M_WIDTH
    n_t = tm_rows // LANES
    hb = tm_rows // HALO
    last = S // HALO - 1
    const = lambda b, j: (0, 0)
    resident = functools.partial(pl.BlockSpec, index_map=const, pipeline_mode=pl.Buffered(1))
    return pl.pallas_call(
        functools.partial(_proj_kernel, col_blk=col_blk),
        grid=(B, S // tm_rows),
        in_specs=[
            pl.BlockSpec((1, tm_rows, D), lambda b, j: (b, j, 0)),
            pl.BlockSpec((1, HALO, D), lambda b, j: (b, jnp.maximum(j * hb - 1, 0), 0)),
            pl.BlockSpec((1, HALO, D), lambda b, j: (b, jnp.minimum((j + 1) * hb, last), 0)),
            pl.BlockSpec((1, D), const),
            resident(w_in_t.shape),
            pl.BlockSpec((N_GATE, 1), lambda b, j: (bg_blk, 0)),
            pl.BlockSpec((3, n_conv), const),
            pl.BlockSpec((1, n_conv), const),
        ],
        out_specs=[
            pl.BlockSpec((1, n_t, n_cm, LANES), lambda b, j: (b, j, 0, 0)),
            pl.BlockSpec((1, tm_rows, M_WIDTH), lambda b, j: (b, j, 0)),
            pl.BlockSpec((1, M_HEADS, N_GATE // M_HEADS, tm_rows), lambda b, j: (b, 0, 0, j)),
        ],
        out_shape=[
            jax.ShapeDtypeStruct((B, S // LANES, n_cm, LANES), BF16),
            jax.ShapeDtypeStruct((B, S, M_WIDTH), BF16),
            jax.ShapeDtypeStruct((B, M_HEADS, N_GATE // M_HEADS, S), F32),
        ],
        scratch_shapes=[pltpu.VMEM((D, n_cm + M_WIDTH), BF16),
                        pltpu.VMEM((tm_rows + 2 * HALO, D), BF16),
                        pltpu.VMEM((2, col_blk // LANES, tm_rows + 2 * HALO, LANES), F32)],
        compiler_params=_cparams("arbitrary", "arbitrary"),
        name="proj",
    )(x, x, x, nw, w_in_t, cols, cw, cb)


def _mlstm_kernel(qt_ref, vt_ref, k_ref, g_ref, out_ref, rows_s, stab_s, h_s, c_s):
    S = k_ref.shape[1]
    HB = g_ref.shape[1]
    L = CHUNK
    NC = S // L
    D = HEAD_DIM
    DA = D + BF16_ROWS
    W = HB * D

    R_A, R_MX, R_W, R_DEC, R_ISC, R_EMT, R_GT, R_AMAX = range(8)

    def scan(x, op, fill):
        lane = lax.broadcasted_iota(jnp.int32, x.shape, 1) % L
        pre = x
        suf = x
        d = 1
        while d < L:
            pre = op(pre, jnp.where(lane >= d, pltpu.roll(pre, d, 1), fill))
            suf = op(suf, jnp.where(lane < L - d, pltpu.roll(suf, S - d, 1), fill))
            d *= 2
        return pre, suf

    chains = [(hd, dr) for hd in range(HB) for dr in range(2)]
    assert HB % 2 == 0 and len(chains) <= 8
    a_rows = {}
    for h0 in range(0, HB, 2):
        g = jnp.concatenate([g_ref[0, h0, 0:4, :], g_ref[0, h0 + 1, 0:4, :]], axis=0)
        pre, suf = scan(g, jnp.add, 0.0)
        tot = pre + suf - g
        for j, hd in enumerate((h0, h0 + 1)):
            o = 4 * j
            b_rows = (pre[o + 1:o + 2], suf[o + 3:o + 4])
            for dr in range(2):
                a_rows[hd, dr] = g[o + 2 * dr:o + 2 * dr + 1] - b_rows[dr]
                rows_s[hd, dr, R_A:R_A + 1, :] = a_rows[hd, dr]
                rows_s[hd, dr, R_EMT:R_EMT + 1, :] = b_rows[dr]
                rows_s[hd, dr, R_GT:R_GT + 1, :] = tot[o + 2 * dr + 1:o + 2 * dr + 2]
    a8 = jnp.concatenate([a_rows[ch] for ch in chains]
                         + [jnp.zeros((8 - len(chains), S), F32)] * (len(chains) < 8), axis=0)
    pmax, smax = scan(a8, jnp.maximum, NEG_BIG)
    for i, (hd, dr) in enumerate(chains):
        rows_s[hd, dr, R_MX:R_MX + 1, :] = pmax[i:i + 1] if dr == 0 else smax[i:i + 1]
        rows_s[hd, dr, R_AMAX:R_AMAX + 1, :] = jnp.maximum(pmax[i:i + 1], smax[i:i + 1])

    def chunk_of(dr, i):
        return i if dr == 0 else NC - 1 - i

    def stabilisers(i, ms):
        out = []
        for (hd, dr), m in zip(chains, ms):
            r0 = pl.multiple_of(chunk_of(dr, i) * L, L)
            gt = rows_s[hd, dr, R_GT:R_GT + 1, pl.ds(r0, L)]
            m_end = jnp.maximum(gt + m, gt + rows_s[hd, dr, R_AMAX:R_AMAX + 1, pl.ds(r0, L)])
            stab_s[hd, dr, 0:1, pl.ds(r0, L)] = m
            stab_s[hd, dr, 1:2, pl.ds(r0, L)] = m_end
            out.append(m_end)
        return tuple(out)

    lax.fori_loop(0, NC, stabilisers, tuple(jnp.zeros((1, L), F32) for _ in chains), unroll=4)

    for hd, dr in chains:
        a = rows_s[hd, dr, R_A:R_A + 1, :]
        gt = rows_s[hd, dr, R_GT:R_GT + 1, :]
        m = stab_s[hd, dr, 0:1, :]
        m_end = stab_s[hd, dr, 1:2, :]
        mx = jnp.maximum(m, rows_s[hd, dr, R_MX:R_MX + 1, :])
        b = rows_s[hd, dr, R_EMT:R_EMT + 1, :]
        rows_s[hd, dr, R_MX:R_MX + 1, :] = mx
        rows_s[hd, dr, R_W:R_W + 1, :] = jnp.exp(gt + a - m_end)
        rows_s[hd, dr, R_DEC:R_DEC + 1, :] = jnp.exp(gt + m - m_end)
        rows_s[hd, dr, R_ISC:R_ISC + 1, :] = jnp.exp(m - mx)
        rows_s[hd, dr, R_EMT:R_EMT + 1, :] = jnp.exp(-(b + mx))

    ones_row = (lax.broadcasted_iota(jnp.int32, (BF16_ROWS, L), 0) == 0).astype(BF16)

    c_s[...] = jnp.zeros_like(c_s)

    si = lax.broadcasted_iota(jnp.int32, (L, L), 0)
    ti = lax.broadcasted_iota(jnp.int32, (L, L), 1)
    eye = si == ti
    valid = (si <= ti, si >= ti)

    def chain(hd, dr, c, second):
        r0 = pl.multiple_of(c * L, L)
        rows = rows_s[hd, dr, :, pl.ds(r0, L)]
        a_col = jnp.sum(jnp.where(eye, rows[R_A:R_A + 1], 0.0), axis=1, keepdims=True)
        wts = jnp.exp(jnp.where(valid[dr], a_col - rows[R_MX:R_MX + 1], NEG_BIG))
        kc = k_ref[0, pl.ds(r0, L), hd * D:(hd + 1) * D]
        qpre = qt_ref[0, c, hd * D:(hd + 1) * D, :].astype(F32)
        qtc = (qpre * _sigmoid(qpre)).astype(BF16)
        vac = jnp.concatenate([vt_ref[0, c, hd * D:(hd + 1) * D, :], ones_row], axis=0)
        cmat = c_s[hd, dr]
        both = jnp.dot(jnp.concatenate([kc, cmat.astype(BF16)], axis=0), qtc,
                       preferred_element_type=F32)
        s_w = (both[0:L] * wts).astype(BF16)
        tot = (jnp.dot(vac, s_w, preferred_element_type=F32)
               + rows[R_ISC:R_ISC + 1] * both[L:L + DA])
        den = jnp.maximum(jnp.abs(tot[D:D + 1]), rows[R_EMT:R_EMT + 1])
        h = tot[0:D] * (1.0 / den)
        hs = slice(hd * D, (hd + 1) * D)
        if second:
            out_ref[0, c, hs, :] = (h_s[hs, pl.ds(r0, L)] + h).astype(BF16)
        else:
            h_s[hs, pl.ds(r0, L)] = h
        u = (vac.astype(F32) * rows[R_W:R_W + 1]).astype(BF16)
        c_s[hd, dr] = rows[R_DEC:R_DEC + 1, 0:1] * cmat + jnp.dot(u, kc, preferred_element_type=F32)

    def step(i, carry, second):
        for hd, dr in chains:
            chain(hd, dr, chunk_of(dr, i), second)
        return carry

    assert NC % 2 == 0
    lax.fori_loop(0, NC // 2, functools.partial(step, second=False), 0)
    lax.fori_loop(NC // 2, NC, functools.partial(step, second=True), 0)


def _mlstm(mt, k_tm, gcm, heads_per_step=MLSTM_HEADS_PER_STEP):
    B, S, _ = k_tm.shape
    NC = S // CHUNK
    D = HEAD_DIM
    HB = heads_per_step
    v0 = (M_WIDTH + 3 * H_WIDTH) // (HB * D)
    W = HB * D
    NB = M_HEADS // HB
    DA = D + BF16_ROWS
    cm_blk = (1, NC, W, LANES)
    return pl.pallas_call(
        _mlstm_kernel,
        grid=(B, NB),
        in_specs=[
            pl.BlockSpec(cm_blk, lambda b, h: (b, 0, h, 0)),
            pl.BlockSpec(cm_blk, lambda b, h: (b, 0, v0 + h, 0)),
            pl.BlockSpec((1, S, W), lambda b, h: (b, 0, h)),
            pl.BlockSpec((1, HB, N_GATE // M_HEADS, S), lambda b, h: (b, h, 0, 0)),
        ],
        out_specs=pl.BlockSpec(cm_blk, lambda b, h: (b, 0, h, 0)),
        out_shape=jax.ShapeDtypeStruct((B, NC, M_WIDTH, LANES), BF16),
        scratch_shapes=[
            pltpu.VMEM((HB, 2, 8, S), F32),
            pltpu.VMEM((HB, 2, 8, S), F32),
            pltpu.VMEM((W, S), F32),
            pltpu.VMEM((HB, 2, DA, D), F32),
        ],
        compiler_params=_cparams("parallel", "parallel"),
        name="mlstm",
    )(mt, mt, k_tm, gcm)


def _filt_mlp_kernel(w1t_ref, b1_ref, w2_ref, b2_ref, w3_ref, b3_ref, fr_ref, w4f_ref, w4b_ref,
                     fbias_ref, out_ref, h3_s, *, cb_rows):
    S = h3_s.shape[2]
    hi = lax.Precision.HIGHEST
    tn_dims = (((0,), (0,)), ((), ()))
    lane = lax.broadcasted_iota(jnp.int32, (1, S), 1)
    pos_f = lane.astype(F32)
    pos_b = (S - lane).astype(F32)

    def features(pos):
        bands = (FILTER_EMB - 1) // 2
        t = pos / (S - 1)
        ang = (2.0 * math.pi) * pos / S
        fidx = lax.broadcasted_iota(jnp.int32, (bands, 1), 0).astype(F32)
        f = 1e-4 + fidx * ((bands - 1 - 1e-4) / (bands - 1))
        fa = f * ang
        w1t = w1t_ref[...]
        pre = (w1t[:, 0:1] * t
               + jnp.dot(w1t[:, 1:1 + bands], jnp.cos(fa), precision=hi, preferred_element_type=F32)
               - jnp.dot(w1t[:, 1 + bands:], jnp.sin(fa), precision=hi, preferred_element_type=F32))
        fr = fr_ref[...]
        h = jnp.sin(fr[:, 0:1] * (pre + b1_ref[...]))
        h = jnp.sin(fr[:, 1:2] * (lax.dot_general(w2_ref[...], h, tn_dims, precision=hi,
                                                  preferred_element_type=F32) + b2_ref[...]))
        return jnp.sin(fr[:, 2:3] * (lax.dot_general(w3_ref[...], h, tn_dims, precision=hi,
                                                     preferred_element_type=F32) + b3_ref[...]))

    @pl.when(pl.program_id(0) == 0)
    def _():
        h3 = features(pos_f)
        h3_s[0] = h3
        n_t = S // LANES
        anti = (lax.broadcasted_iota(jnp.int32, (LANES, LANES), 0)
                + lax.broadcasted_iota(jnp.int32, (LANES, LANES), 1) == LANES - 1).astype(F32)
        rev = jnp.concatenate(
            [jnp.dot(h3[:, (n_t - 1 - u) * LANES:(n_t - u) * LANES], anti, precision=hi,
                     preferred_element_type=F32) for u in range(n_t)], axis=1)
        h3_s[1] = pltpu.roll(rev, 1, 1)

    r = pl.program_id(0) * cb_rows + lax.broadcasted_iota(jnp.int32, (cb_rows, 1), 0)
    ch = (r % H_WIDTH).astype(F32)
    max_decay = math.log(DECAY_TARGET) / FAST_DECAY_PCT
    min_decay = math.log(DECAY_TARGET) / SLOW_DECAY_PCT
    delta = jnp.abs(min_decay + ch * ((max_decay - min_decay) / (H_WIDTH - 1)))
    n_t = S // LANES
    for half, (w_ref, pos) in enumerate(((w4f_ref, pos_f), (w4b_ref, pos_b))):
        filt = lax.dot_general(w_ref[...].astype(BF16), h3_s[half].astype(BF16), tn_dims,
                               preferred_element_type=F32)
        filt = filt * jnp.exp(-(pos / (S - 1)) * delta)
        if half == 0:
            filt = jnp.where(lane == 0, filt + fbias_ref[...], filt)
        else:
            filt = jnp.where(lane == 0, 0.0, filt)
        for u in range(n_t):
            out_ref[half * n_t + u] = filt[:, u * LANES:(u + 1) * LANES].astype(BF16)


def _filt_mlp(w1t, cols, fb_blk, fbias_blk, w2, w3, fr, w4, S, cb_rows=FILT_MLP_ROWS):
    R = w4.shape[1] // 2
    Hd = FILTER_HIDDEN
    const = lambda i: (0, 0)
    nblk = R // cb_rows
    return pl.pallas_call(
        functools.partial(_filt_mlp_kernel, cb_rows=cb_rows),
        grid=(nblk,),
        in_specs=[
            pl.BlockSpec((Hd, FILTER_EMB), const),
            pl.BlockSpec((Hd, 1), lambda i: (fb_blk, 0)),
            pl.BlockSpec((Hd, Hd), const),
            pl.BlockSpec((Hd, 1), lambda i: (fb_blk + 1, 0)),
            pl.BlockSpec((Hd, Hd), const),
            pl.BlockSpec((Hd, 1), lambda i: (fb_blk + 2, 0)),
            pl.BlockSpec((Hd, 3), const),
            pl.BlockSpec((Hd, cb_rows), lambda i: (0, i)),
            pl.BlockSpec((Hd, cb_rows), lambda i: (0, nblk + i)),
            pl.BlockSpec((cb_rows, 1), lambda i: (fbias_blk + i, 0)),
        ],
        out_specs=pl.BlockSpec((2 * S // LANES, cb_rows, LANES), lambda i: (0, i, 0)),
        out_shape=jax.ShapeDtypeStruct((2 * S // LANES, R, LANES), BF16),
        scratch_shapes=[pltpu.VMEM((2, Hd, S), F32)],
        compiler_params=_cparams("arbitrary"),
        name="filt_mlp",
    )(w1t, cols, w2, cols, w3, cols, fr, w4, w4, cols)


@functools.lru_cache(maxsize=None)
def _dft_constants(S):
    N = 2 * S
    N2 = LANES
    N1 = N // N2
    h = N1 // 2
    k1 = np.arange(N1)
    k2 = np.arange(N2)
    a1 = -2.0 * np.pi * np.outer(k1, k1) / N1
    f1r, f1i = np.cos(a1), np.sin(a1)
    at = -2.0 * np.pi * np.outer(k1, k2) / N
    twr, twi = np.cos(at), np.sin(at)
    a2 = -2.0 * np.pi * np.outer(k2, k2) / N2
    f2r, f2i = np.cos(a2), np.sin(a2)
    s1c = np.block([[f1r[:, :h], -f1i[:, :h]], [f1i[:, :h], f1r[:, :h]]])
    s1r = np.concatenate([f1r, f1i], axis=0)
    s2 = np.block([[f2r, f2i], [-f2i, f2r]])
    s2i = np.block([[f2r, -f2i], [f2i, f2r]])
    s1i = np.block([[f1r[:h, :], f1i[:h, :]], [-f1i[:h, :], f1r[:h, :]]])
    cast = lambda a: np.asarray(a, np.float32)
    return dict(s1c=cast(s1c), s1r=cast(s1r), s2=cast(s2), s2i=cast(s2i), s1i=cast(s1i),
                twr=cast(twr), twi=cast(twi), N1=N1, h=h)


def _bf(a):
    return jnp.asarray(a, F32).astype(BF16)


def _filt_fft_kernel(kern_ref, s1r_ref, s2_ref, twr_ref, twi_ref, out_ref, src_s, slab_s, *, unroll):
    N1 = twr_ref.shape[0]
    n_ch = kern_ref.shape[1]
    pitch = src_s.shape[0] // N1
    scale = 1.0 / (N1 * LANES)
    twr = twr_ref[...]
    twi = twi_ref[...]
    s1r = s1r_ref[...]

    def load_in(t1, carry):
        src_s[pl.ds(pl.multiple_of(t1 * pitch, 8), n_ch), :] = kern_ref[t1].astype(F32)
        return carry
    lax.fori_loop(0, N1, load_in, 0, unroll=4)

    def per_group(j, carry):
        for u in range(0, unroll, 2):
            cs = (j * unroll + u, j * unroll + u + 1)
            z = jnp.concatenate([src_s[pl.ds(c, N1, stride=pitch), :] for c in cs], axis=1)
            a = jnp.dot(s1r, z.astype(BF16), preferred_element_type=F32)
            for i, c in enumerate(cs):
                ar = a[0:N1, i * LANES:(i + 1) * LANES]
                ai = a[N1:2 * N1, i * LANES:(i + 1) * LANES]
                r0 = pl.multiple_of(c * N1, N1)
                slab_s[pl.ds(r0, N1), 0:LANES] = (ar * twr - ai * twi).astype(BF16)
                slab_s[pl.ds(r0, N1), LANES:2 * LANES] = (ar * twi + ai * twr).astype(BF16)
        return carry

    lax.fori_loop(0, n_ch // unroll, per_group, 0)
    x = jnp.dot(slab_s[...], s2_ref[...], preferred_element_type=F32)
    out_ref[...] = (x * scale).astype(BF16).reshape(n_ch, N1, 2 * LANES)


def _filt_fft(kern, consts, n_ch=FILT_FFT_ROWS, unroll=FILT_FFT_ROWS):
    N1, R, _ = kern.shape
    const = lambda i: (0, 0)
    pitch = n_ch + 8
    return pl.pallas_call(
        functools.partial(_filt_fft_kernel, unroll=unroll),
        grid=(R // n_ch,),
        in_specs=[
            pl.BlockSpec((N1, n_ch, LANES), lambda i: (0, i, 0)),
            pl.BlockSpec((2 * N1, N1), const),
            pl.BlockSpec((2 * LANES, 2 * LANES), const),
            pl.BlockSpec((N1, LANES), const),
            pl.BlockSpec((N1, LANES), const),
        ],
        out_specs=pl.BlockSpec((n_ch, N1, 2 * LANES), lambda i: (i, 0, 0)),
        out_shape=jax.ShapeDtypeStruct((R, N1, 2 * LANES), BF16),
        scratch_shapes=[pltpu.VMEM((N1 * pitch, LANES), F32),
                        pltpu.VMEM((n_ch * N1, 2 * LANES), BF16)],
        compiler_params=_cparams("parallel"),
        name="filt_fft",
    )(kern, _bf(consts["s1r"]), _bf(consts["s2"]),
      jnp.asarray(consts["twr"]), jnp.asarray(consts["twi"]))


def _hyena_kernel(nw_ref, v_ref, x1_ref, x2_ref, khat_ref, s1c_ref, s2_ref, s2i_ref, s1i_ref,
                  twr_ref, twi_ref, out_ref, src_s, slab_s, yf_s, q_s, y_s, *, unroll):
    h = v_ref.shape[1]
    n_ch = v_ref.shape[2]
    N1 = 2 * h
    twr = twr_ref[...]
    twi = twi_ref[...]
    s1c = s1c_ref[...]
    s1i = s1i_ref[...]
    n_groups = n_ch // unroll
    rows = unroll * N1
    pitch = src_s.shape[0] // (2 * h)

    def tile_rows(i):
        return pl.ds(i * pitch, n_ch, stride=1)

    def chan_rows(c, bb):
        return pl.ds(bb * h * pitch + c, h, stride=pitch)

    def long_conv(order):
        def stage_a(g):
            for u in range(0, unroll, 2):
                cs = (g * unroll + u, g * unroll + u + 1)
                z = jnp.concatenate(
                    [jnp.concatenate([src_s[chan_rows(c, 0), :], src_s[chan_rows(c, 1), :]], axis=0)
                     for c in cs], axis=1)
                a = jnp.dot(s1c, z.astype(BF16), preferred_element_type=F32)
                for i, c in enumerate(cs):
                    ar = a[0:N1, i * LANES:(i + 1) * LANES]
                    ai = a[N1:2 * N1, i * LANES:(i + 1) * LANES]
                    r0 = c * N1 if isinstance(c, int) else pl.multiple_of(c * N1, N1)
                    slab_s[pl.ds(r0, N1), 0:LANES] = (ar * twr - ai * twi).astype(BF16)
                    slab_s[pl.ds(r0, N1), LANES:2 * LANES] = (ar * twi + ai * twr).astype(BF16)

        def stage_m1(g):
            r0 = g * rows if isinstance(g, int) else pl.multiple_of(g * rows, rows)
            c0 = g * unroll if isinstance(g, int) else pl.multiple_of(g * unroll, unroll)
            x = jnp.dot(slab_s[pl.ds(r0, rows), :], s2_ref[...], preferred_element_type=F32)
            xr = x[:, 0:LANES].reshape(unroll, N1, LANES)
            xi = x[:, LANES:2 * LANES].reshape(unroll, N1, LANES)
            kr = khat_ref[order, pl.ds(c0, unroll), :, 0:LANES].astype(F32)
            ki = khat_ref[order, pl.ds(c0, unroll), :, LANES:2 * LANES].astype(F32)
            yf_s[pl.ds(r0, rows), 0:LANES] = (xr * kr - xi * ki).reshape(rows, LANES).astype(BF16)
            yf_s[pl.ds(r0, rows), LANES:2 * LANES] = (xr * ki + xi * kr).reshape(rows, LANES).astype(BF16)

        def stage_m2(g):
            r0 = g * rows if isinstance(g, int) else pl.multiple_of(g * rows, rows)
            c0 = g * unroll if isinstance(g, int) else pl.multiple_of(g * unroll, unroll)
            p = jnp.dot(yf_s[pl.ds(r0, rows), :], s2i_ref[...], preferred_element_type=F32)
            pr = p[:, 0:LANES].reshape(unroll, N1, LANES)
            pi = p[:, LANES:2 * LANES].reshape(unroll, N1, LANES)
            q_s[pl.ds(c0, unroll), 0:N1, :] = (pr * twr + pi * twi).astype(BF16)
            q_s[pl.ds(c0, unroll), N1:2 * N1, :] = (pi * twr - pr * twi).astype(BF16)

        def stage_i(g):
            for u in range(0, unroll, 2):
                cs = (g * unroll + u, g * unroll + u + 1)
                q2 = jnp.concatenate([q_s[c] for c in cs], axis=1)
                y = jnp.dot(s1i, q2, preferred_element_type=F32)
                for i, c in enumerate(cs):
                    y_s[chan_rows(c, 0), :] = y[0:h, i * LANES:(i + 1) * LANES]
                    y_s[chan_rows(c, 1), :] = y[h:2 * h, i * LANES:(i + 1) * LANES]

        G = n_groups
        for t in range(G + 3):
            if 0 <= t - 3 < G:
                stage_i(t - 3)
            if 0 <= t - 2 < G:
                stage_m2(t - 2)
            if 0 <= t - 1 < G:
                stage_m1(t - 1)
            if t < G:
                stage_a(t)

    def per_tile(fn):
        for bb in range(2):
            def body(t1, carry, bb=bb):
                fn(bb, t1, tile_rows(bb * h + t1))
                return carry
            lax.fori_loop(0, h, body, 0, unroll=8)

    def load_in(bb, t1, rws):
        src_s[rws, :] = v_ref[bb, t1].astype(F32)
    per_tile(load_in)
    long_conv(0)

    def gate1(bb, t1, rws):
        src_s[rws, :] = y_s[rws, :] * x1_ref[bb, t1].astype(F32)
    per_tile(gate1)
    long_conv(1)

    nw = jnp.broadcast_to(nw_ref[...], (n_ch, LANES))

    def gate2_norm(bb, t1, rws):
        z = y_s[rws, :] * x2_ref[bb, t1].astype(F32)
        ms = jnp.mean(z * z, axis=0, keepdims=True)
        out_ref[bb, t1] = (z * lax.rsqrt(ms + EPS) * nw).astype(BF16)
    per_tile(gate2_norm)


def _hyena(cols, nw_blk, hy4, khat, consts, unroll=HYENA_GROUP):
    B, h, _, _ = hy4.shape
    C = H_WIDTH
    n_ch = GROUP_W
    G = C // n_ch
    g0 = M_WIDTH // n_ch
    N1 = consts["N1"]
    const = lambda g, p: (0, 0)
    blk = (2, h, n_ch, LANES)
    pitch = n_ch + 9
    return pl.pallas_call(
        functools.partial(_hyena_kernel, unroll=unroll),
        grid=(G, B // 2),
        in_specs=[
            pl.BlockSpec((n_ch, 1), lambda g, p: (nw_blk + g, 0)),
            pl.BlockSpec(blk, lambda g, p: (p, 0, g0 + g, 0)),
            pl.BlockSpec(blk, lambda g, p: (p, 0, g0 + G + g, 0)),
            pl.BlockSpec(blk, lambda g, p: (p, 0, g0 + 2 * G + g, 0)),
            pl.BlockSpec((2, n_ch, N1, 2 * LANES), lambda g, p: (0, g, 0, 0)),
            pl.BlockSpec((2 * N1, 2 * h), const),
            pl.BlockSpec((2 * LANES, 2 * LANES), const),
            pl.BlockSpec((2 * LANES, 2 * LANES), const),
            pl.BlockSpec((2 * h, 2 * N1), const),
            pl.BlockSpec((N1, LANES), const),
            pl.BlockSpec((N1, LANES), const),
        ],
        out_specs=pl.BlockSpec(blk, lambda g, p: (p, 0, g, 0)),
        out_shape=jax.ShapeDtypeStruct((B, h, C, LANES), BF16),
        scratch_shapes=[
            pltpu.VMEM((2 * h * pitch, LANES), F32),
            pltpu.VMEM((n_ch * N1, 2 * LANES), BF16),
            pltpu.VMEM((n_ch * N1, 2 * LANES), BF16),
            pltpu.VMEM((n_ch, 2 * N1, LANES), BF16),
            pltpu.VMEM((2 * h * pitch, LANES), F32),
        ],
        compiler_params=_cparams("parallel", "arbitrary"),
        name="hyena",
    )(cols, hy4, hy4, hy4, khat,
      _bf(consts["s1c"]), _bf(consts["s2"]), _bf(consts["s2i"]), _bf(consts["s1i"]),
      jnp.asarray(consts["twr"]), jnp.asarray(consts["twi"]))


def _outmlp_kernel(x_ref, hm_ref, og_ref, yh_ref, nwm_ref, wo_ref, w1_ref, w2_ref, n_post_ref,
                   n_pre_ref, n_post2_ref, out_ref):
    n_t = hm_ref.shape[1]
    nwm = jnp.broadcast_to(nwm_ref[...], (M_WIDTH, LANES))
    tiles = []
    for j in range(n_t):
        hg = hm_ref[0, j].astype(F32) * _sigmoid(og_ref[0, j].astype(F32))
        heads = []
        for hd in range(M_HEADS):
            hh = hg[hd * HEAD_DIM:(hd + 1) * HEAD_DIM]
            ms = jnp.mean(hh * hh, axis=0, keepdims=True)
            heads.append(hh * lax.rsqrt(ms + EPS))
        ym = jnp.concatenate(heads, axis=0) * nwm
        yt = jnp.concatenate([ym, yh_ref[0, j].astype(F32)], axis=0)
        tiles.append(yt.T.astype(BF16))
    y = tiles[0] if n_t == 1 else jnp.concatenate(tiles, axis=0)
    mix = jnp.dot(y, wo_ref[...], preferred_element_type=F32)
    x1 = x_ref[0] + _rms_rows(mix, n_post_ref[...])
    hm = _rms_rows(x1, n_pre_ref[...]).astype(BF16)
    mid = jnp.maximum(jnp.dot(hm, w1_ref[...], preferred_element_type=F32), 0.0)
    mid = (mid * mid).astype(BF16)
    ff = jnp.dot(mid, w2_ref[...], preferred_element_type=F32)
    out_ref[0] = x1 + _rms_rows(ff, n_post2_ref[...])


def _outmlp(x, hm, mt, yh, cols, nwm_blk, wo, w1, w2, n_post, n_pre, n_post2, tm_rows):
    B, S, D = x.shape
    n_t = tm_rows // LANES
    og0 = mt.shape[2] // M_WIDTH - 1
    const = lambda b, i: (0, 0)
    resident = functools.partial(pl.BlockSpec, index_map=const, pipeline_mode=pl.Buffered(1))
    return pl.pallas_call(
        _outmlp_kernel,
        grid=(B, S // tm_rows),
        in_specs=[
            pl.BlockSpec((1, tm_rows, D), lambda b, i: (b, i, 0)),
            pl.BlockSpec((1, n_t, M_WIDTH, LANES), lambda b, i: (b, i, 0, 0)),
            pl.BlockSpec((1, n_t, M_WIDTH, LANES), lambda b, i: (b, i, og0, 0)),
            pl.BlockSpec((1, n_t, H_WIDTH, LANES), lambda b, i: (b, i, 0, 0)),
            pl.BlockSpec((M_WIDTH, 1), lambda b, i: (nwm_blk, 0)),
            resident((D, D)),
            resident((D, D_FF)),
            resident((D_FF, D)),
            pl.BlockSpec((1, D), const),
            pl.BlockSpec((1, D), const),
            pl.BlockSpec((1, D), const),
        ],
        out_specs=pl.BlockSpec((1, tm_rows, D), lambda b, i: (b, i, 0)),
        out_shape=jax.ShapeDtypeStruct((B, S, D), F32),
        compiler_params=_cparams("parallel", "parallel"),
        name="outmlp",
    )(x, hm, mt, yh, cols, wo, w1, w2, n_post, n_pre, n_post2)


def _cast_kernel(w_ref, out_ref):
    out_ref[...] = w_ref[...].astype(BF16)


def _to_bf16(w, rows=CAST_ROWS):
    R, C = w.shape
    return pl.pallas_call(
        _cast_kernel,
        grid=(R // rows,),
        in_specs=[pl.BlockSpec((rows, C), lambda i: (i, 0))],
        out_specs=pl.BlockSpec((rows, C), lambda i: (i, 0)),
        out_shape=jax.ShapeDtypeStruct((R, C), BF16),
        compiler_params=_cparams("parallel"),
        name="wcast",
    )(w)


def kernel(x, norm_mix_pre, norm_mix_post, norm_mlp_pre, norm_mlp_post, w_in, b_gates,
           conv_w, conv_b, mlstm_norm_w, hyena_norm_w, filt_w1, filt_b1, filt_w2, filt_b2,
           filt_w3, filt_b3, filt_w4, filt_freq, filt_bias, w_out, w_mlp_in, w_mlp_out):
    B, S, D = x.shape
    assert D == D_MODEL and B % 2 == 0 and S % CHUNK == 0
    row = lambda a: a.astype(F32).reshape(1, -1)
    tm_rows = min(PROJ_ROWS, S)
    cols = jnp.concatenate([filt_bias.reshape(-1), mlstm_norm_w, hyena_norm_w, filt_b1, filt_b2,
                            filt_b3, b_gates]).astype(F32).reshape(-1, 1)
    off_mn = HYENA_ORDER * H_WIDTH
    off_hn = off_mn + M_WIDTH
    off_fb = off_hn + H_WIDTH
    off_bg = off_fb + 3 * FILTER_HIDDEN
    assert (off_mn % M_WIDTH == 0 and off_hn % GROUP_W == 0 and off_fb % FILTER_HIDDEN == 0
            and off_bg % N_GATE == 0 and off_mn % FILT_MLP_ROWS == 0)

    n_conv = 2 * M_WIDTH + 3 * H_WIDTH
    o_gate = n_conv + 2 * M_WIDTH
    assert w_in.shape == (D, o_gate + N_GATE)
    assert conv_w.shape == (3, n_conv)
    mt, k_tm, gcm = _proj(x, row(norm_mix_pre), w_in.astype(F32).T, cols, off_bg // N_GATE,
                          conv_w.astype(F32), row(conv_b), tm_rows)

    h_m = _mlstm(mt, k_tm, gcm)

    consts = _dft_constants(S)
    kern = _filt_mlp(filt_w1.astype(F32).T, cols, off_fb // FILTER_HIDDEN, 0, filt_w2.astype(F32),
                     filt_w3.astype(F32), filt_freq.astype(F32).T, filt_w4.astype(F32), S)
    khat = _filt_fft(kern, consts)
    khat = khat.reshape(2, H_WIDTH, consts["N1"], 2 * LANES)

    y_h = _hyena(cols, off_hn // GROUP_W, mt, khat, consts)

    return _outmlp(x, h_m, mt, y_h, cols, off_mn // M_WIDTH,
                   w_out.astype(BF16), _to_bf16(w_mlp_in.astype(F32)), w_mlp_out.astype(BF16),
                   row(norm_mix_post), row(norm_mlp_pre), row(norm_mlp_post), min(OUTMLP_ROWS, S))
```
